```python
import math
import jax, jax.numpy as jnp
from jax import lax
import numpy as np

D_MODEL = 1024
BATCH = 8
SEQ = 4096
DEPTH = 2

F32 = jnp.float32
HEAD_DIM = 64
HG_WIDTH = D_MODEL // 4
HG_HEADS = HG_WIDTH // HEAD_DIM
HG_CHUNK = 64
GM_WIDTH = D_MODEL // 4
GM_GROUPS = GM_WIDTH // HEAD_DIM
GM_CHUNK = 128
NSA_WIDTH = D_MODEL // 2
NSA_HEADS = NSA_WIDTH // HEAD_DIM
NSA_KV_GROUPS = 2
NSA_HPG = NSA_HEADS // NSA_KV_GROUPS
NSA_KV_WIDTH = NSA_KV_GROUPS * HEAD_DIM
CMP_BLOCK = 32
CMP_STRIDE = 16
CMP_HIDDEN = 128
SEL_BLOCK = 64
N_SEL = 16
WINDOW = 512
NSA_QBLOCK = 64
N_GATES = 3
IMP_FORCE = 1e9
IMP_FUTURE = -1e9
NEG_INF = -1e30
MIX_WIDTH = HG_WIDTH + GM_WIDTH + NSA_WIDTH
IN_SPLITS = (HG_WIDTH,) * 4 + (GM_WIDTH,) * 2 + (NSA_WIDTH,) + (NSA_KV_WIDTH,) * 6 + (NSA_HEADS * N_GATES,)
IN_WIDTH = sum(IN_SPLITS)
N_EXPERTS = 32
TOP_K = 4
D_EXPERT = D_MODEL
SWIGLU_ALPHA = 1.702
SWIGLU_LIMIT = 7.0
EXPERT_BLOCK = 512
ROPE_THETA = 10000.0
DEEPNORM_ALPHA = (2 * DEPTH) ** 0.25
DEEPNORM_BETA = (8 * DEPTH) ** -0.25
LN_EPS = 1e-5
RMS_EPS = 1e-6

kernel_name = "hybrid_hgrn2_gmlp_nsa_moe_deepnorm"


def layer_norm(x, w, b):
    xf = x.astype(F32)
    mu = jnp.mean(xf, axis=-1, keepdims=True)
    var = jnp.mean(jnp.square(xf - mu), axis=-1, keepdims=True)
    return ((xf - mu) * lax.rsqrt(var + LN_EPS) * w.astype(F32) + b.astype(F32)).astype(x.dtype)


def rms_norm_heads(x, w):
    shp = x.shape
    xf = x.astype(F32).reshape(shp[:-1] + (shp[-1] // HEAD_DIM, HEAD_DIM))
    xf = xf * lax.rsqrt(jnp.mean(jnp.square(xf), axis=-1, keepdims=True) + RMS_EPS)
    return (xf.reshape(shp) * w.astype(F32)).astype(x.dtype)


def masked_softmax(s, mask):
    s = jnp.where(mask, s.astype(F32), NEG_INF)
    return jax.nn.softmax(s, axis=-1) * mask


def rope_tables(positions):
    inv = ROPE_THETA ** (-jnp.arange(0, HEAD_DIM, 2, dtype=F32) / HEAD_DIM)
    ang = positions.astype(F32)[..., None] * inv
    return jnp.cos(ang)[:, :, None, :], jnp.sin(ang)[:, :, None, :]


def apply_rope(x, cos, sin):
    xf = x.astype(F32)
    x1, x2 = xf[..., :HEAD_DIM // 2], xf[..., HEAD_DIM // 2:]
    return jnp.concatenate([x1 * cos - x2 * sin, x2 * cos + x1 * sin], axis=-1).astype(x.dtype)


def hgrn2_mixer(q, f_raw, v, gate, lb, norm_w):
    B, S, _ = q.shape
    H, Dk, C = HG_HEADS, HEAD_DIM, HG_CHUNK
    n_ch = S // C
    dt = q.dtype
    lb = lb.astype(F32)
    f_raw = f_raw.astype(F32)
    qf = jax.nn.silu(q.astype(F32))
    log_f = jnp.logaddexp(jnp.log(lb), jnp.log1p(-lb) + jax.nn.log_sigmoid(f_raw))
    k = (1.0 - lb) * jax.nn.sigmoid(-f_raw)

    def chunks(a):
        return a.reshape(B, n_ch, C, H, Dk).transpose(1, 0, 3, 2, 4)

    causal = jnp.tril(jnp.ones((C, C), dtype=bool))

    def step(state, inp):
        qc, kc, vc, gc = inp
        b = jnp.cumsum(gc, axis=2)
        diff = b[:, :, :, None, :] - b[:, :, None, :, :]
        decay = jnp.exp(jnp.where(causal[:, :, None], diff, -jnp.inf))
        attn = jnp.einsum('bhtk,bhtsk,bhsk->bhts', qc, decay, kc)
        o = (jnp.einsum('bhts,bhsv->bhtv', attn, vc)
             + jnp.einsum('bhtk,bhkv->bhtv', qc * jnp.exp(b), state))
        b_last = b[:, :, -1]
        state = (jnp.exp(b_last)[..., None] * state
                 + jnp.einsum('bhsk,bhsv->bhkv', kc * jnp.exp(b_last[:, :, None] - b), vc))
        return state, o

    s0 = jnp.zeros((B, H, Dk, Dk), F32)
    _, o = lax.scan(step, s0, (chunks(qf), chunks(k), chunks(v.astype(F32)), chunks(log_f)))
    o = o.transpose(1, 0, 3, 2, 4).reshape(B, S, H * Dk)
    return (rms_norm_heads(o, norm_w) * jax.nn.sigmoid(gate.astype(F32))).astype(dt)


def gmlp_mixer(u, v, ln_w, ln_b, w_s, b_s, norm_w):
    B, S, _ = u.shape
    n_ch = S // GM_CHUNK
    u = jax.nn.gelu(u)
    v = layer_norm(jax.nn.gelu(v), ln_w, ln_b)
    v = v.reshape(B, n_ch, GM_CHUNK, GM_GROUPS, HEAD_DIM)
    w = w_s * jnp.tril(jnp.ones((GM_CHUNK, GM_CHUNK), w_s.dtype))
    sv = jnp.einsum('gts,bnsgc->bntgc', w, v) + b_s.T[:, :, None]
    return rms_norm_heads(u * sv.reshape(B, S, GM_WIDTH), norm_w)


def compress_blocks(k, pe, w1, w2):
    B, S, G, Dh = k.shape
    n_cmp = (S - CMP_BLOCK) // CMP_STRIDE + 1
    idx = np.arange(n_cmp)[:, None] * CMP_STRIDE + np.arange(CMP_BLOCK)[None, :]
    blk = k[:, idx] + pe[:, None, :]
    blk = blk.transpose(0, 3, 1, 2, 4).reshape(B, G, n_cmp, CMP_BLOCK * Dh)
    return jax.nn.gelu(blk @ w1) @ w2


def nsa_mixer(q, k_c, v_c, k_s, v_s, k_w, v_w, gates, pe, w1, w2, cos, sin):
    B, S, H, Dh = q.shape
    G, HPG, QB = NSA_KV_GROUPS, NSA_HPG, NSA_QBLOCK
    nqb = S // QB
    scale = 1.0 / math.sqrt(Dh)
    kc = compress_blocks(k_c, pe[0], w1[0], w2[0])
    vc = compress_blocks(v_c, pe[1], w1[1], w2[1])
    n_cmp = kc.shape[2]
    cmp_end = jnp.asarray(np.arange(n_cmp) * CMP_STRIDE + CMP_BLOCK - 1)
    n_sb = S // SEL_BLOCK
    n_sel = min(N_SEL, n_sb)
    units = np.arange(n_cmp)[:, None] + np.arange(CMP_BLOCK // CMP_STRIDE)[None, :]
    overlap = jax.nn.one_hot(units // (SEL_BLOCK // CMP_STRIDE), n_sb, dtype=F32).sum(axis=1)
    q_rot = apply_rope(q, cos, sin)
    ks_r = apply_rope(k_s, cos, sin)
    kw_r = apply_rope(k_w, cos, sin)
    ks_blk = ks_r.reshape(B, n_sb, SEL_BLOCK, G, Dh).transpose(0, 3, 1, 2, 4)
    vs_blk = v_s.reshape(B, n_sb, SEL_BLOCK, G, Dh).transpose(0, 3, 1, 2, 4)
    pad = ((0, 0), (WINDOW, 0), (0, 0), (0, 0))
    kw_pad = jnp.pad(kw_r, pad)
    vw_pad = jnp.pad(v_w, pad)
    b_idx = jnp.arange(B)[:, None, None, None]
    g_idx = jnp.arange(G)[None, :, None, None]

    def to_blocks(a):
        return a.reshape((B, nqb, QB) + a.shape[2:]).swapaxes(0, 1)

    def grouped(a):
        return a.reshape(B, QB, G, HPG, a.shape[-1]).transpose(0, 2, 3, 1, 4)

    def block_fn(args):
        qb_raw, qb_rot, gb, qi = args
        t = qi * QB + jnp.arange(QB)
        qc = grouped(qb_raw) * scale
        qr = grouped(qb_rot) * scale
        p_c = masked_softmax(jnp.einsum('bghqd,bgcd->bghqc', qc, kc), cmp_end[None, :] <= t[:, None])
        o_c = jnp.einsum('bghqc,bgcd->bghqd', p_c.astype(vc.dtype), vc)
        imp = jnp.einsum('bghqc,cj->bgqj', p_c, overlap)
        j = jnp.arange(n_sb)[None, :]
        cur = (t // SEL_BLOCK)[:, None]
        imp = jnp.where(j > cur, IMP_FUTURE, imp)
        imp = jnp.where((j == 0) | (j == cur) | (j == cur - 1), IMP_FORCE, imp)
        _, sel = lax.top_k(imp, n_sel)
        k_sel = ks_blk[b_idx, g_idx, sel].reshape(B, G, QB, n_sel * SEL_BLOCK, Dh)
        v_sel = vs_blk[b_idx, g_idx, sel].reshape(B, G, QB, n_sel * SEL_BLOCK, Dh)
        kpos = (sel[..., None] * SEL_BLOCK + jnp.arange(SEL_BLOCK)).reshape(B, G, QB, n_sel * SEL_BLOCK)
        m_s = kpos <= t[:, None]
        p_s = masked_softmax(jnp.einsum('bghqd,bgqkd->bghqk', qr, k_sel), m_s[:, :, None])
        o_s = jnp.einsum('bghqk,bgqkd->bghqd', p_s.astype(v_sel.dtype), v_sel)
        kw = lax.dynamic_slice_in_dim(kw_pad, qi * QB, QB + WINDOW, axis=1).transpose(0, 2, 1, 3)
        vw = lax.dynamic_slice_in_dim(vw_pad, qi * QB, QB + WINDOW, axis=1).transpose(0, 2, 1, 3)
        wpos = qi * QB - WINDOW + jnp.arange(QB + WINDOW)
        m_w = (wpos[None, :] <= t[:, None]) & (wpos[None, :] > t[:, None] - WINDOW)
        p_w = masked_softmax(jnp.einsum('bghqd,bgkd->bghqk', qr, kw), m_w)
        o_w = jnp.einsum('bghqk,bgkd->bghqd', p_w.astype(vw.dtype), vw)
        g = grouped(gb)
        o = g[..., 0:1] * o_c + g[..., 1:2] * o_s + g[..., 2:3] * o_w
        return o.transpose(0, 3, 1, 2, 4).reshape(B, QB, H * Dh)

    out = lax.map(block_fn, (to_blocks(q), to_blocks(q_rot), to_blocks(gates), jnp.arange(nqb)))
    return out.swapaxes(0, 1).reshape(B, S, H * Dh)


def moe_ffn(x2d, router_w, router_b, w_up, b_up, w_down, b_down):
    N, D = x2d.shape
    logits = x2d.astype(F32) @ router_w.astype(F32) + router_b.astype(F32)
    top_logit, top_e = lax.top_k(logits, TOP_K)
    top_w = jax.nn.softmax(top_logit, axis=-1)
    n_assign = N * TOP_K
    flat_e = top_e.reshape(-1)
    order = jnp.argsort(flat_e)
    sorted_e = flat_e[order]
    sorted_tok = (order // TOP_K).astype(jnp.int32)
    counts = jnp.bincount(flat_e, length=N_EXPERTS)
    padded = (counts + EXPERT_BLOCK - 1) // EXPERT_BLOCK * EXPERT_BLOCK
    start = jnp.cumsum(counts) - counts
    pad_end = jnp.cumsum(padded)
    pad_start = pad_end - padded
    dest = pad_start[sorted_e] + jnp.arange(n_assign) - start[sorted_e]
    n_blocks = -(-(n_assign + N_EXPERTS * (EXPERT_BLOCK - 1)) // EXPERT_BLOCK)
    n_rows = n_blocks * EXPERT_BLOCK
    tok_buf = jnp.full((n_rows,), N, jnp.int32).at[dest].set(sorted_tok)
    w_buf = jnp.zeros((n_rows,), F32).at[dest].set(top_w.reshape(-1)[order])
    block_e = jnp.clip(jnp.searchsorted(pad_end, jnp.arange(n_blocks) * EXPERT_BLOCK, side='right'),
                       0, N_EXPERTS - 1)
    x_pad = jnp.concatenate([x2d, jnp.zeros((1, D), x2d.dtype)], axis=0)
    xb = x_pad[tok_buf].reshape(n_blocks, EXPERT_BLOCK, D)

    def expert_block(args):
        xe, e = args
        h = xe @ w_up[e] + b_up[e]
        glu = jnp.minimum(h[:, :D_EXPERT], SWIGLU_LIMIT)
        lin = jnp.clip(h[:, D_EXPERT:], -SWIGLU_LIMIT, SWIGLU_LIMIT)
        act = glu * jax.nn.sigmoid(SWIGLU_ALPHA * glu) * (lin + 1.0)
        return act @ w_down[e] + b_down[e]

    yb = lax.map(expert_block, (xb, block_e)).reshape(n_rows, D)
    out = jax.ops.segment_sum(yb * w_buf[:, None].astype(yb.dtype), tok_buf, num_segments=N + 1)
    return out[:N]


def setup_inputs(seed: int = 0) -> dict:
    key = jax.random.key(seed)
    ks = jax.random.split(key, 32)
    L, D, E, Fd = DEPTH, D_MODEL, N_EXPERTS, D_EXPERT

    def nrm(k, shape, scale):
        return scale * jax.random.normal(k, shape, F32)

    def gain(k, shape):
        return 1.0 + 0.1 * jax.random.normal(k, shape, F32)

    positions = (jnp.arange(SEQ, dtype=jnp.int32)[None, :]
                 + jax.random.randint(ks[1], (BATCH, 1), 0, 1024, dtype=jnp.int32))
    return {
        "x": nrm(ks[0], (BATCH, SEQ, D), 1.0),
        "positions": positions,
        "w_in": nrm(ks[2], (L, D, IN_WIDTH), D ** -0.5),
        "hg_lower_bounds": nrm(ks[3], (L, HG_WIDTH), 1.0),
        "hg_norm_w": gain(ks[4], (L, HG_WIDTH)),
        "gm_ln_w": gain(ks[5], (L, GM_WIDTH)),
        "gm_ln_b": nrm(ks[6], (L, GM_WIDTH), 0.02),
        "gm_spatial_w": nrm(ks[7], (L, GM_GROUPS, GM_CHUNK, GM_CHUNK), GM_CHUNK ** -0.5),
        "gm_spatial_b": gain(ks[8], (L, GM_GROUPS, GM_CHUNK)),
        "gm_norm_w": gain(ks[9], (L, GM_WIDTH)),
        "nsa_cmp_pe": nrm(ks[10], (L, 2, CMP_BLOCK, HEAD_DIM), 0.1),
        "nsa_cmp_w1": nrm(ks[11], (L, 2, CMP_BLOCK * HEAD_DIM, CMP_HIDDEN), (CMP_BLOCK * HEAD_DIM) ** -0.5),
        "nsa_cmp_w2": nrm(ks[12], (L, 2, CMP_HIDDEN, HEAD_DIM), CMP_HIDDEN ** -0.5),
        "nsa_norm_w": gain(ks[13], (L, NSA_WIDTH)),
        "w_out": nrm(ks[14], (L, MIX_WIDTH, D), DEEPNORM_BETA * MIX_WIDTH ** -0.5),
        "ln1_w": gain(ks[15], (L, D)),
        "ln1_b": nrm(ks[16], (L, D), 0.02),
        "router_w": nrm(ks[17], (L, D, E), D ** -0.5),
        "router_b": nrm(ks[18], (L, E), 0.01),
        "exp_w_up": nrm(ks[19], (L, E, D, 2 * Fd), D ** -0.5),
        "exp_b_up": nrm(ks[20], (L, E, 2 * Fd), 0.01),
        "exp_w_down": nrm(ks[21], (L, E, Fd, D), DEEPNORM_BETA * Fd ** -0.5),
        "exp_b_down": nrm(ks[22], (L, E, D), 0.01),
        "ln2_w": gain(ks[23], (L, D)),
        "ln2_b": nrm(ks[24], (L, D), 0.02),
    }


def reference(x, positions, w_in, hg_lower_bounds, hg_norm_w, gm_ln_w, gm_ln_b, gm_spatial_w,
              gm_spatial_b, gm_norm_w, nsa_cmp_pe, nsa_cmp_w1, nsa_cmp_w2, nsa_norm_w, w_out,
              ln1_w, ln1_b, router_w, router_b, exp_w_up, exp_b_up, exp_w_down, exp_b_down,
              ln2_w, ln2_b):
    B, S, D = x.shape
    cos, sin = rope_tables(positions)
    lb_all = jnp.cumsum(jax.nn.softmax(hg_lower_bounds.astype(F32), axis=0), axis=0)
    lb_all = lb_all - lb_all[0:1]
    split_points = [int(p) for p in np.cumsum(IN_SPLITS)[:-1]]

    def heads(a, n):
        return a.reshape(B, S, n, HEAD_DIM)

    for l in range(DEPTH):
        h = x @ w_in[l]
        (hg_q, hg_f, hg_i, hg_g, gm_u, gm_v, nsa_q, k_c, v_c, k_s, v_s, k_w, v_w,
         nsa_g) = jnp.split(h, split_points, axis=-1)
        y_hg = hgrn2_mixer(hg_q, hg_f, hg_i, hg_g, lb_all[l], hg_norm_w[l])
        y_gm = gmlp_mixer(gm_u, gm_v, gm_ln_w[l], gm_ln_b[l], gm_spatial_w[l], gm_spatial_b[l], gm_norm_w[l])
        gates = jax.nn.sigmoid(nsa_g).reshape(B, S, NSA_HEADS, N_GATES)
        y_nsa = nsa_mixer(heads(nsa_q, NSA_HEADS), heads(k_c, NSA_KV_GROUPS), heads(v_c, NSA_KV_GROUPS),
                          heads(k_s, NSA_KV_GROUPS), heads(v_s, NSA_KV_GROUPS), heads(k_w, NSA_KV_GROUPS),
                          heads(v_w, NSA_KV_GROUPS), gates, nsa_cmp_pe[l], nsa_cmp_w1[l], nsa_cmp_w2[l],
                          cos, sin)
        y_nsa = rms_norm_heads(y_nsa, nsa_norm_w[l])
        mix = jnp.concatenate([y_hg, y_gm, y_nsa], axis=-1) @ w_out[l]
        x = layer_norm(DEEPNORM_ALPHA * x + mix, ln1_w[l], ln1_b[l])
        moe_out = moe_ffn(x.reshape(B * S, D), router_w[l], router_b[l], exp_w_up[l], exp_b_up[l],
                          exp_w_down[l], exp_b_down[l]).reshape(B, S, D)
        x = layer_norm(DEEPNORM_ALPHA * x + moe_out, ln2_w[l], ln2_b[l])
    return x
```

```python
import functools
import math

import numpy as np
import jax
import jax.numpy as jnp
from jax import lax
from jax.experimental import pallas as pl
from jax.experimental.pallas import tpu as pltpu

F32 = jnp.float32
BF16 = jnp.bfloat16
HIGHEST = lax.Precision.HIGHEST

HEAD_DIM = 64
LANES = 128
VMEM_LIMIT = 48 * 1024 * 1024

HG_CHUNK = 64
GM_CHUNK = 128
NSA_KV_GROUPS = 2
NSA_HPG = 4
CMP_BLOCK = 32
CMP_STRIDE = 16
CMP_HIDDEN = 128
SEL_BLOCK = 64
N_SEL = 16
WINDOW = 512
N_GATES = 3
IMP_FORCE = 1e9
IMP_FUTURE = -1e9
NEG_INF = -1e30
N_EXPERTS = 32
TOP_K = 4
SWIGLU_ALPHA = 1.702
SWIGLU_LIMIT = 7.0
EXPERT_BLOCK = 512
ROPE_THETA = 10000.0
LN_EPS = 1e-5
RMS_EPS = 1e-6


def _params(*sem):
    return pltpu.CompilerParams(dimension_semantics=sem, vmem_limit_bytes=VMEM_LIMIT)


def _dot(a, b):
    return jnp.dot(a, b, preferred_element_type=F32)


def _dot_nt(a, b, precision=None):
    return lax.dot_general(a, b, (((1,), (1,)), ((), ())), precision=precision,
                           preferred_element_type=F32)


def _dot_tn(a, b):
    return lax.dot_general(a, b, (((0,), (0,)), ((), ())), preferred_element_type=F32)


def _sigmoid(x):
    return 1.0 / (1.0 + jnp.exp(-x))


def _gelu(x):
    return 0.5 * x * (1.0 + jnp.tanh(0.7978845608028654 * (x + 0.044715 * x * x * x)))


def _layer_norm(x, w, b):
    mu = jnp.mean(x, axis=-1, keepdims=True)
    xc = x - mu
    var = jnp.mean(xc * xc, axis=-1, keepdims=True)
    return xc * lax.rsqrt(var + LN_EPS) * w + b


def _head_mean_sq(o, bd_ones):
    return jnp.dot(o * o, bd_ones, precision=HIGHEST, preferred_element_type=F32) * (1.0 / HEAD_DIM)


def _rope_kernel(pos_ref, inv_ref, sign_ref, cos_ref, sin_ref):
    ang = pos_ref[...] * inv_ref[...]
    cos_ref[...] = jnp.cos(ang)
    sin_ref[...] = jnp.sin(ang) * sign_ref[...]


def rope_tables(positions):
    n = positions.size
    tile = min(n, 2048)
    pos = jnp.broadcast_to(positions.reshape(n, 1).astype(F32), (n, HEAD_DIM))
    inv = ROPE_THETA ** (-jnp.arange(0, HEAD_DIM, 2, dtype=F32) / HEAD_DIM)
    inv = jnp.concatenate([inv, inv]).reshape(1, HEAD_DIM)
    sign = jnp.concatenate([-jnp.ones((HEAD_DIM // 2,), F32), jnp.ones((HEAD_DIM // 2,), F32)]).reshape(1, HEAD_DIM)
    row = pl.BlockSpec((tile, HEAD_DIM), lambda i: (i, 0))
    const = pl.BlockSpec((1, HEAD_DIM), lambda i: (0, 0))
    return pl.pallas_call(
        _rope_kernel, grid=(n // tile,), in_specs=[row, const, const], out_specs=[row, row],
        out_shape=[jax.ShapeDtypeStruct((n, HEAD_DIM), F32)] * 2,
        compiler_params=_params("parallel"), name="rope_tables")(pos, inv, sign)


def _in_proj_kernel(x_ref, w_ref, h_ref):
    h_ref[...] = _dot(x_ref[...].astype(BF16), w_ref[...])


def in_proj(x2d, w_bf16):
    n, d = x2d.shape
    width = w_bf16.shape[1]
    tile = min(n, 512)
    return pl.pallas_call(
        _in_proj_kernel, grid=(n // tile,),
        in_specs=[pl.BlockSpec((tile, d), lambda i: (i, 0)), pl.BlockSpec((d, width), lambda i: (0, 0))],
        out_specs=pl.BlockSpec((tile, width), lambda i: (i, 0)),
        out_shape=jax.ShapeDtypeStruct((n, width), F32),
        compiler_params=_params("parallel"), name="in_proj")(x2d, w_bf16)


HG_LEVELS = (0, 64, 32, 16, 8, 4, 2)


def _hgrn_constants():
    c = HG_CHUNK
    t = np.arange(c)
    tril = (t[:, None] >= t[None, :]).astype(np.float32)
    rows = [tril]
    masks = []
    for m in HG_LEVELS:
        if m == 0:
            rows.append(tril)
            masks.append(np.eye(c, dtype=np.float32))
            continue
        ref = (t // m) * m + m // 2 - 1
        rows.append(tril[ref])
        masks.append(((t[:, None] // m == t[None, :] // m) & (t[:, None] % m >= m // 2)
                      & (t[None, :] % m < m // 2)).astype(np.float32))
    pmat = np.concatenate(rows, axis=0)
    masks = np.stack([np.tile(mk, (4, 1)) for mk in masks])
    return pmat, masks


def _hgrn_kernel(q_ref, f_ref, i_ref, g_ref, lb_ref, nw_ref, pmat_ref, masks_ref, bd_ref, hm_ref,
                 o_ref, state_ref):
    c = HG_CHUNK

    @pl.when(pl.program_id(1) == 0)
    def _():
        state_ref[...] = jnp.zeros_like(state_ref)

    lb = lb_ref[...]
    fr = f_ref[...]
    hq = q_ref[...]
    qf = hq * _sigmoid(hq)
    log_sig = jnp.minimum(fr, 0.0) - jnp.log(1.0 + jnp.exp(-jnp.abs(fr)))
    a = jnp.log(lb)
    cc = jnp.log(1.0 - lb) + log_sig
    log_f = jnp.maximum(a, cc) + jnp.log(1.0 + jnp.exp(-jnp.abs(a - cc)))
    kk = (1.0 - lb) * _sigmoid(-fr)
    v = i_ref[...]
    vb = v.astype(BF16)
    bd = bd_ref[...]
    hm = hm_ref[...]

    allb = jnp.dot(pmat_ref[...], log_f, precision=HIGHEST, preferred_element_type=F32)
    b = allb[0:c]

    att = jnp.zeros((4 * c, c), F32)
    for li in range(len(HG_LEVELS)):
        r = allb[(li + 1) * c:(li + 2) * c]
        ql = qf * jnp.exp(jnp.minimum(b - r, 0.0))
        kl = (kk * jnp.exp(jnp.minimum(r - b, 0.0))).astype(BF16)
        qs = jnp.concatenate([ql * hm[h:h + 1] for h in range(4)], axis=0).astype(BF16)
        att = att + masks_ref[li] * _dot_nt(qs, kl)
    o_st = _dot(att.astype(BF16), vb)
    o = jnp.zeros_like(v)
    for h in range(4):
        o = o + o_st[h * c:(h + 1) * c] * hm[h:h + 1]

    st = state_ref[...]
    o = o + _dot_nt((qf * jnp.exp(b)).astype(BF16), st.astype(BF16))
    b_last = b[c - 1:c]
    kdec = (kk * jnp.exp(b_last - b)).astype(BF16)
    state_ref[...] = st * jnp.exp(b_last) + bd * _dot_tn(vb, kdec)

    ms = _head_mean_sq(o, bd)
    y = o * lax.rsqrt(ms + RMS_EPS) * nw_ref[...] * _sigmoid(g_ref[...])
    o_ref[...] = y.astype(o_ref.dtype)


def hgrn2(h, lb, norm_w, batch, seq):
    n = h.shape[0]
    w = lb.shape[-1]
    c = HG_CHUNK
    nch = seq // c
    pmat, masks = _hgrn_constants()
    lane_head = np.arange(w) // HEAD_DIM
    bd = (lane_head[:, None] == lane_head[None, :]).astype(np.float32)
    hm = (np.arange(4)[:, None] == lane_head[None, :]).astype(np.float32)

    def col(j):
        return pl.BlockSpec((c, w), lambda b, i, j=j: (b * nch + i, j))

    def const(shape):
        return pl.BlockSpec(shape, lambda b, i: (0,) * len(shape))

    return pl.pallas_call(
        _hgrn_kernel, grid=(batch, nch),
        in_specs=[col(0), col(1), col(2), col(3), const((1, w)), const((1, w)),
                  const(pmat.shape), const(masks.shape), const(bd.shape), const(hm.shape)],
        out_specs=pl.BlockSpec((c, w), lambda b, i: (b * nch + i, 0)),
        out_shape=jax.ShapeDtypeStruct((n, w), BF16),
        scratch_shapes=[pltpu.VMEM((w, w), F32)],
        compiler_params=_params("parallel", "arbitrary"), name="hgrn2",
    )(h, h, h, h, lb.reshape(1, w), norm_w.reshape(1, w), jnp.asarray(pmat), jnp.asarray(masks),
      jnp.asarray(bd), jnp.asarray(hm))


def _gmlp_kernel(u_ref, v_ref, lnw_ref, lnb_ref, ws_ref, bias_ref, nw_ref, bd_ref, hm_ref, o_ref):
    c = GM_CHUNK
    u = _gelu(u_ref[...])
    v = _layer_norm(_gelu(v_ref[...]), lnw_ref[...], lnb_ref[...])
    vb = v.astype(BF16)
    hm = hm_ref[...]
    causal = lax.broadcasted_iota(jnp.int32, (c, c), 0) >= lax.broadcasted_iota(jnp.int32, (c, c), 1)
    sv = bias_ref[...]
    for g in range(ws_ref.shape[0]):
        wg = jnp.where(causal, ws_ref[g], 0.0).astype(BF16)
        sv = sv + _dot(wg, vb) * hm[g:g + 1]
    y = u * sv
    ms = _head_mean_sq(y, bd_ref[...])
    o_ref[...] = (y * lax.rsqrt(ms + RMS_EPS) * nw_ref[...]).astype(o_ref.dtype)


def gmlp(h, ln_w, ln_b, w_s, b_s, norm_w, u_col, v_col):
    n = h.shape[0]
    groups, c, _ = w_s.shape
    w = groups * HEAD_DIM
    lane_head = np.arange(w) // HEAD_DIM
    bd = (lane_head[:, None] == lane_head[None, :]).astype(np.float32)
    hm = (np.arange(groups)[:, None] == lane_head[None, :]).astype(np.float32)
    bias = jnp.repeat(b_s.T, HEAD_DIM, axis=1)

    def const(shape):
        return pl.BlockSpec(shape, lambda i: (0,) * len(shape))

    return pl.pallas_call(
        _gmlp_kernel, grid=(n // c,),
        in_specs=[pl.BlockSpec((c, w), lambda i: (i, u_col)), pl.BlockSpec((c, w), lambda i: (i, v_col)),
                  const((1, w)), const((1, w)), const(w_s.shape), const((c, w)), const((1, w)),
                  const(bd.shape), const(hm.shape)],
        out_specs=pl.BlockSpec((c, w), lambda i: (i, 0)),
        out_shape=jax.ShapeDtypeStruct((n, w), BF16),
        compiler_params=_params("parallel"), name="gmlp",
    )(h, h, ln_w.reshape(1, w), ln_b.reshape(1, w), w_s, bias, norm_w.reshape(1, w),
      jnp.asarray(bd), jnp.asarray(hm))


def _compress_kernel(u_ref, wtop_ref, wbot_ref, pe_ref, w2_ref, o_ref):
    u = u_ref[...].astype(BF16)
    wtop = wtop_ref[...]
    wbot = wbot_ref[...]
    pe = pe_ref[...].astype(BF16)
    const = _dot(pe[0:1], wtop) + _dot(pe[1:2], wbot)
    p = _dot(u, wtop)
    q = _dot(u, wbot)
    q_next = jnp.concatenate([q[1:], jnp.zeros_like(q[0:1])], axis=0)
    hid = _gelu(p + q_next + const)
    o_ref[...] = _dot(hid.astype(BF16), w2_ref[...]).astype(o_ref.dtype)


def compress(kv, pe, w1, w2, batch, seq):
    g = NSA_KV_GROUPS
    half = CMP_STRIDE
    units = seq // half
    gw = g * HEAD_DIM
    u = kv.reshape(batch * units, half * gw)
    eye = jnp.eye(g, dtype=F32)
    w1r = w1.reshape(2, half, HEAD_DIM, CMP_HIDDEN)
    wbd = jnp.einsum('hjdn,gk->hjgdkn', w1r, eye).reshape(2, half * gw, g * CMP_HIDDEN).astype(BF16)
    w2bd = jnp.einsum('nd,gk->gnkd', w2, eye).reshape(g * CMP_HIDDEN, gw).astype(BF16)
    pe2 = jnp.broadcast_to(pe.reshape(2, half, 1, HEAD_DIM), (2, half, g, HEAD_DIM)).reshape(2, half * gw)

    def const(shape):
        return pl.BlockSpec(shape, lambda b: (0,) * len(shape))

    out = pl.pallas_call(
        _compress_kernel, grid=(batch,),
        in_specs=[pl.BlockSpec((units, half * gw), lambda b: (b, 0)),
                  const(wbd.shape[1:]), const(wbd.shape[1:]), const(pe2.shape), const(w2bd.shape)],
        out_specs=pl.BlockSpec((units, gw), lambda b: (b, 0)),
        out_shape=jax.ShapeDtypeStruct((batch * units, gw), BF16),
        compiler_params=_params("parallel"), name="nsa_compress",
    )(u, wbd[0], wbd[1], pe2, w2bd)
    return out.reshape(batch, units, gw)


def _rot_half_pairs(x):
    lane = lax.broadcasted_iota(jnp.int32, x.shape, 1)
    fwd = pltpu.roll(x, 32, axis=1)
    bwd = pltpu.roll(x, 96, axis=1)
    return jnp.where((lane % HEAD_DIM) < HEAD_DIM // 2, bwd, fwd)


def _kprep_kernel(ks_ref, vs_ref, kw_ref, vw_ref, cos_ref, sin_ref, ksa_ref, vso_ref, kwo_ref, vwo_ref):
    t = ks_ref.shape[0]
    cos = cos_ref[...]
    sin = sin_ref[...]
    cos2 = jnp.concatenate([cos, cos], axis=1)
    sin2 = jnp.concatenate([sin, sin], axis=1)
    ks = ks_ref[...]
    kw = kw_ref[...]
    ks_r = ks * cos2 + _rot_half_pairs(ks) * sin2
    kw_r = kw * cos2 + _rot_half_pairs(kw) * sin2
    pos = pl.program_id(1) * t + lax.broadcasted_iota(jnp.int32, (t, HEAD_DIM), 0)
    onehot = (pos // SEL_BLOCK == lax.broadcasted_iota(jnp.int32, (t, HEAD_DIM), 1)).astype(F32)
    vs = vs_ref[...]
    vw = vw_ref[...]
    for g in range(NSA_KV_GROUPS):
        sl = slice(g * HEAD_DIM, (g + 1) * HEAD_DIM)
        ksa_ref[g] = jnp.concatenate([ks_r[:, sl], onehot], axis=1).astype(BF16)
        vso_ref[g] = vs[:, sl].astype(BF16)
        kwo_ref[g] = kw_r[:, sl].astype(BF16)
        vwo_ref[g] = vw[:, sl].astype(BF16)


def nsa_kprep(h, cosf, sinf, batch, seq, col0):
    g = NSA_KV_GROUPS
    t = min(seq, 512)
    nt = seq // t

    def col(j):
        return pl.BlockSpec((t, LANES), lambda b, i, j=j: (b * nt + i, col0 + j))

    tab = pl.BlockSpec((t, HEAD_DIM), lambda b, i: (b * nt + i, 0))

    def out(wd):
        return pl.BlockSpec((None, g, t, wd), lambda b, i: (b, 0, i, 0))

    return pl.pallas_call(
        _kprep_kernel, grid=(batch, nt),
        in_specs=[col(0), col(1), col(2), col(3), tab, tab],
        out_specs=[out(2 * HEAD_DIM), out(HEAD_DIM), out(HEAD_DIM), out(HEAD_DIM)],
        out_shape=[jax.ShapeDtypeStruct((batch, g, seq, 2 * HEAD_DIM), BF16)]
        + [jax.ShapeDtypeStruct((batch, g, seq, HEAD_DIM), BF16)] * 3,
        compiler_params=_params("parallel", "parallel"), name="nsa_kprep",
    )(h, h, h, h, cosf, sinf)


def _nsa_kernel(hq_ref, gate_ref, cos_ref, sin_ref, kc_ref, vc_ref, ksa_ref, vs_ref, kw_ref, vw_ref,
                ovl_ref, nw_ref, o_ref, imp_ref, *, tq, n_sb):
    g = pl.program_id(1)
    qi = pl.program_id(2)
    hpg = NSA_HPG
    rows = hpg * tq
    t0 = qi * tq
    scale = 1.0 / math.sqrt(HEAD_DIM)
    half = HEAD_DIM // 2

    hq = hq_ref[...]
    cos = cos_ref[...]
    sin = sin_ref[...]
    q_raw, q_rot = [], []
    for h in range(hpg):
        qh = hq[:, h * HEAD_DIM:(h + 1) * HEAD_DIM]
        swapped = jnp.concatenate([qh[:, half:], qh[:, :half]], axis=1)
        q_raw.append(qh * scale)
        q_rot.append((qh * cos + swapped * sin) * scale)
    q_raw = jnp.concatenate(q_raw, axis=0).astype(BF16)
    q_rot = jnp.concatenate(q_rot, axis=0)

    tpos = t0 + lax.broadcasted_iota(jnp.int32, (tq, 1), 0)
    tpos_r = jnp.concatenate([tpos] * hpg, axis=0)

    n_cmp = kc_ref.shape[0]
    s_c = _dot_nt(q_raw, kc_ref[...])
    cmp_end = lax.broadcasted_iota(jnp.int32, (1, n_cmp), 1) * CMP_STRIDE + (CMP_BLOCK - 1)
    mask_c = cmp_end <= tpos_r
    s_c = jnp.where(mask_c, s_c, NEG_INF)
    e_c = jnp.exp(s_c - jnp.max(s_c, axis=-1, keepdims=True))
    p_c = jnp.where(mask_c, e_c / jnp.sum(e_c, axis=-1, keepdims=True), 0.0)
    o_c = _dot(p_c.astype(BF16), vc_ref[...])

    p_sum = p_c[0:tq]
    for h in range(1, hpg):
        p_sum = p_sum + p_c[h * tq:(h + 1) * tq]
    imp = _dot_nt(ovl_ref[...], p_sum, precision=HIGHEST)
    blk = lax.broadcasted_iota(jnp.int32, (n_sb, tq), 0)
    cur = (t0 + lax.broadcasted_iota(jnp.int32, (n_sb, tq), 1)) // SEL_BLOCK
    imp = jnp.where(blk > cur, IMP_FUTURE, imp)
    imp = jnp.where((blk == 0) | (blk == cur) | (blk == cur - 1), IMP_FORCE, imp)
    imp_ref[...] = imp

    def rank_body(j, rank):
        row = imp_ref[pl.ds(j, 1), :]
        beats = (row > imp) | ((row == imp) & (j < blk))
        return rank + beats.astype(jnp.int32)

    rank = lax.fori_loop(0, n_sb, rank_body, jnp.zeros((n_sb, tq), jnp.int32))
    sel_bias = jnp.where(rank < min(N_SEL, n_sb), 0.0, NEG_INF).T
    if n_sb < HEAD_DIM:
        sel_bias = jnp.concatenate([sel_bias, jnp.zeros((tq, HEAD_DIM - n_sb), F32)], axis=1)
    q_aug = jnp.concatenate([q_rot, jnp.concatenate([sel_bias] * hpg, axis=0)], axis=1).astype(BF16)
    q_rot = q_rot.astype(BF16)

    def flash_step(s, v, m, l, acc):
        m_new = jnp.maximum(m, jnp.max(s, axis=-1, keepdims=True))
        alpha = jnp.exp(m - m_new)
        p = jnp.exp(s - m_new)
        l = alpha * l + jnp.sum(p, axis=-1, keepdims=True)
        acc = alpha * acc + _dot(p.astype(BF16), v)
        return m_new, l, acc

    init = (jnp.full((rows, 1), NEG_INF, F32), jnp.zeros((rows, 1), F32), jnp.zeros((rows, HEAD_DIM), F32))

    def sel_body(kt, carry):
        k0 = pl.multiple_of(kt * tq, tq)
        s = _dot_nt(q_aug, ksa_ref[pl.ds(k0, tq), :])
        kpos = k0 + lax.broadcasted_iota(jnp.int32, (1, tq), 1)
        s = jnp.where(kpos <= tpos_r, s, NEG_INF)
        return flash_step(s, vs_ref[pl.ds(k0, tq), :], *carry)

    _, l_s, acc_s = lax.fori_loop(0, qi + 1, sel_body, init)
    o_s = acc_s / l_s

    def win_body(kt, carry):
        k0 = pl.multiple_of(kt * tq, tq)
        s = _dot_nt(q_rot, kw_ref[pl.ds(k0, tq), :])
        kpos = k0 + lax.broadcasted_iota(jnp.int32, (1, tq), 1)
        s = jnp.where((kpos <= tpos_r) & (kpos > tpos_r - WINDOW), s, NEG_INF)
        return flash_step(s, vw_ref[pl.ds(k0, tq), :], *carry)

    first = jnp.maximum(qi - (WINDOW + tq - 1) // tq, 0)
    m_w, l_w, acc_w = lax.fori_loop(first, qi + 1, win_body, init)
    n_pad = jnp.maximum(WINDOW - 1 - tpos_r, 0).astype(F32)
    m_f = jnp.where(n_pad > 0.0, jnp.maximum(m_w, 0.0), m_w)
    a_w = jnp.exp(m_w - m_f)
    o_w = (acc_w * a_w) / (l_w * a_w + n_pad * jnp.exp(-m_f))

    gates = _sigmoid(gate_ref[...])
    nw = nw_ref[...]
    for h in range(hpg):
        sl = slice(h * tq, (h + 1) * tq)
        c0 = (g * hpg + h) * N_GATES
        gh = [jnp.sum(jnp.where(lax.broadcasted_iota(jnp.int32, gates.shape, 1) == c0 + i, gates, 0.0),
                      axis=-1, keepdims=True) for i in range(N_GATES)]
        o = gh[0] * o_c[sl] + gh[1] * o_s[sl] + gh[2] * o_w[sl]
        ms = jnp.mean(o * o, axis=-1, keepdims=True)
        o_ref[h] = (o * lax.rsqrt(ms + RMS_EPS) * nw[h:h + 1]).astype(o_ref.dtype)


def nsa_attention(h, cosf, sinf, kc, vc, ksa, vs, kw, vw, norm_w, batch, seq, q_col0, gate_col):
    g, hpg = NSA_KV_GROUPS, NSA_HPG
    tq = min(seq, 256)
    nq = seq // tq
    n_sb = seq // SEL_BLOCK
    assert n_sb <= HEAD_DIM, "selection-block one-hot shares the 64 spare key lanes"
    n_cmp = kc.shape[2]
    units = np.arange(n_cmp)[:, None] + np.arange(CMP_BLOCK // CMP_STRIDE)[None, :]
    ovl = np.zeros((n_cmp, n_sb), np.float32)
    for c in range((seq - CMP_BLOCK) // CMP_STRIDE + 1):
        for u in units[c]:
            ovl[c, u // (SEL_BLOCK // CMP_STRIDE)] += 1.0
    ovl_t = jnp.asarray(ovl.T)

    def per_bg(shape):
        return pl.BlockSpec((None, None) + shape, lambda b, gi, qi: (b, gi, 0, 0))

    kern = functools.partial(_nsa_kernel, tq=tq, n_sb=n_sb)
    return pl.pallas_call(
        kern, grid=(batch, g, nq),
        in_specs=[pl.BlockSpec((tq, hpg * HEAD_DIM), lambda b, gi, qi: (b * nq + qi, q_col0 + gi)),
                  pl.BlockSpec((tq, LANES), lambda b, gi, qi: (b * nq + qi, gate_col)),
                  pl.BlockSpec((tq, HEAD_DIM), lambda b, gi, qi: (b * nq + qi, 0)),
                  pl.BlockSpec((tq, HEAD_DIM), lambda b, gi, qi: (b * nq + qi, 0)),
                  per_bg((n_cmp, HEAD_DIM)), per_bg((n_cmp, HEAD_DIM)),
                  per_bg((seq, 2 * HEAD_DIM)), per_bg((seq, HEAD_DIM)),
                  per_bg((seq, HEAD_DIM)), per_bg((seq, HEAD_DIM)),
                  pl.BlockSpec((n_sb, n_cmp), lambda b, gi, qi: (0, 0)),
                  pl.BlockSpec((None, hpg, HEAD_DIM), lambda b, gi, qi: (gi, 0, 0))],
        out_specs=pl.BlockSpec((None, hpg, tq, HEAD_DIM), lambda b, gi, qi: (b, gi, qi, 0)),
        out_shape=jax.ShapeDtypeStruct((batch, g * hpg, seq, HEAD_DIM), BF16),
        scratch_shapes=[pltpu.VMEM((n_sb, tq), F32)],
        compiler_params=_params("parallel", "parallel", "arbitrary"), name="nsa_attention",
    )(h, h, cosf, sinf, kc, vc, ksa, vs, kw, vw, ovl_t, norm_w.reshape(g, hpg, HEAD_DIM))


def _out_proj_kernel(x_ref, yhg_ref, ygm_ref, ynsa_ref, whg_ref, wgm_ref, wnsa_ref, lnw_ref, lnb_ref,
                     o_ref, ob_ref, *, alpha):
    mix = _dot(yhg_ref[...], whg_ref[...]) + _dot(ygm_ref[...], wgm_ref[...])
    for h in range(ynsa_ref.shape[0]):
        mix = mix + _dot(ynsa_ref[h], wnsa_ref[h])
    y = _layer_norm(alpha * x_ref[...] + mix, lnw_ref[...], lnb_ref[...])
    o_ref[...] = y
    ob_ref[...] = y.astype(BF16)


def out_proj_ln(x2d, y_hg, y_gm, y_nsa, w_out, ln_w, ln_b, alpha, batch, seq):
    n, d = x2d.shape
    heads = y_nsa.shape[1]
    whg = w_out[:y_hg.shape[1]].astype(BF16)
    wgm = w_out[y_hg.shape[1]:y_hg.shape[1] + y_gm.shape[1]].astype(BF16)
    wnsa = w_out[y_hg.shape[1] + y_gm.shape[1]:].reshape(heads, HEAD_DIM, d).astype(BF16)
    t = min(seq, 512)
    nt = seq // t

    def row(wd):
        return pl.BlockSpec((t, wd), lambda b, i: (b * nt + i, 0))

    def const(shape):
        return pl.BlockSpec(shape, lambda b, i: (0,) * len(shape))

    kern = functools.partial(_out_proj_kernel, alpha=alpha)
    return pl.pallas_call(
        kern, grid=(batch, nt),
        in_specs=[row(d), row(y_hg.shape[1]), row(y_gm.shape[1]),
                  pl.BlockSpec((None, heads, t, HEAD_DIM), lambda b, i: (b, 0, i, 0)),
                  const(whg.shape), const(wgm.shape), const(wnsa.shape), const((1, d)), const((1, d))],
        out_specs=[row(d), row(d)],
        out_shape=[jax.ShapeDtypeStruct((n, d), F32), jax.ShapeDtypeStruct((n, d), BF16)],
        compiler_params=_params("parallel", "parallel"), name="out_proj_ln",
    )(x2d, y_hg, y_gm, y_nsa, whg, wgm, wnsa, ln_w.reshape(1, d), ln_b.reshape(1, d))


def _router_kernel(x_ref, w_ref, b_ref, e_ref, p_ref, r_ref, cnt_ref, carry_ref):
    t = x_ref.shape[0]

    @pl.when(pl.program_id(0) == 0)
    def _():
        carry_ref[...] = jnp.zeros_like(carry_ref)

    logits = jnp.dot(x_ref[...], w_ref[...], precision=HIGHEST, preferred_element_type=F32) + b_ref[...]
    lane = lax.broadcasted_iota(jnp.int32, logits.shape, 1)
    work = logits
    vals, idxs = [], []
    sel = jnp.zeros(logits.shape, F32)
    for _ in range(TOP_K):
        m = jnp.max(work, axis=-1, keepdims=True)
        idx = jnp.min(jnp.where(work == m, lane, LANES), axis=-1, keepdims=True)
        hit = lane == idx
        sel = jnp.where(hit, 1.0, sel)
        work = jnp.where(hit, -jnp.inf, work)
        vals.append(m)
        idxs.append(idx)
    exps = [jnp.exp(v - vals[0]) for v in vals]
    den = exps[0] + exps[1] + exps[2] + exps[3]
    strict = (lax.broadcasted_iota(jnp.int32, (t, t), 0) > lax.broadcasted_iota(jnp.int32, (t, t), 1))
    before = _dot(strict.astype(BF16), sel.astype(BF16)) + carry_ref[...]
    ranks = [jnp.sum(jnp.where(lane == idx, before, 0.0), axis=-1, keepdims=True) for idx in idxs]
    kcol = lax.broadcasted_iota(jnp.int32, (t, TOP_K), 1)
    e_out = jnp.zeros((t, TOP_K), jnp.int32)
    p_out = jnp.zeros((t, TOP_K), F32)
    r_out = jnp.zeros((t, TOP_K), jnp.int32)
    for k in range(TOP_K):
        e_out = jnp.where(kcol == k, idxs[k], e_out)
        p_out = jnp.where(kcol == k, exps[k] / den, p_out)
        r_out = jnp.where(kcol == k, ranks[k].astype(jnp.int32), r_out)
    e_ref[...] = e_out
    p_ref[...] = p_out
    r_ref[...] = r_out
    carry_ref[...] = carry_ref[...] + jnp.sum(sel, axis=0, keepdims=True)
    cnt_ref[...] = carry_ref[...].astype(jnp.int32)


def moe_router(x2d, router_w, router_b):
    n, d = x2d.shape
    e = router_w.shape[1]
    t = min(n, 512)
    w = jnp.zeros((d, LANES), F32).at[:, :e].set(router_w)
    b = jnp.full((1, LANES), NEG_INF, F32).at[0, :e].set(router_b)
    row4 = pl.BlockSpec((t, TOP_K), lambda i: (i, 0))
    top_e, top_p, rank, counts = pl.pallas_call(
        _router_kernel, grid=(n // t,),
        in_specs=[pl.BlockSpec((t, d), lambda i: (i, 0)), pl.BlockSpec((d, LANES), lambda i: (0, 0)),
                  pl.BlockSpec((1, LANES), lambda i: (0, 0))],
        out_specs=[row4, row4, row4, pl.BlockSpec((1, LANES), lambda i: (0, 0))],
        out_shape=[jax.ShapeDtypeStruct((n, TOP_K), jnp.int32), jax.ShapeDtypeStruct((n, TOP_K), F32),
                   jax.ShapeDtypeStruct((n, TOP_K), jnp.int32), jax.ShapeDtypeStruct((1, LANES), jnp.int32)],
        scratch_shapes=[pltpu.VMEM((1, LANES), F32)],
        compiler_params=_params("arbitrary"), name="moe_router",
    )(x2d, w, b)
    return top_e, top_p, rank, counts[0, :e]


def _expert_kernel(be_ref, x_ref, wu_ref, bu_ref, wd_ref, bd_ref, o_ref):
    f = wd_ref.shape[0]
    hcat = _dot(x_ref[...], wu_ref[...]) + bu_ref[...]
    glu = jnp.minimum(hcat[:, :f], SWIGLU_LIMIT)
    lin = jnp.clip(hcat[:, f:], -SWIGLU_LIMIT, SWIGLU_LIMIT)
    act = glu * _sigmoid(SWIGLU_ALPHA * glu) * (lin + 1.0)
    o_ref[...] = (_dot(act.astype(BF16), wd_ref[...]) + bd_ref[...]).astype(o_ref.dtype)


def moe_experts(xb, block_e, w_up, b_up, w_down, b_down):
    rows, d = xb.shape
    e, _, f2 = w_up.shape
    f = f2 // 2
    nb = rows // EXPERT_BLOCK
    grid_spec = pltpu.PrefetchScalarGridSpec(
        num_scalar_prefetch=1, grid=(nb,),
        in_specs=[pl.BlockSpec((EXPERT_BLOCK, d), lambda i, be: (i, 0)),
                  pl.BlockSpec((None, d, f2), lambda i, be: (be[i], 0, 0)),
                  pl.BlockSpec((None, 1, f2), lambda i, be: (be[i], 0, 0)),
                  pl.BlockSpec((None, f, d), lambda i, be: (be[i], 0, 0)),
                  pl.BlockSpec((None, 1, d), lambda i, be: (be[i], 0, 0))],
        out_specs=pl.BlockSpec((EXPERT_BLOCK, d), lambda i, be: (i, 0)))
    return pl.pallas_call(
        _expert_kernel, grid_spec=grid_spec,
        out_shape=jax.ShapeDtypeStruct((rows, d), BF16),
        compiler_params=_params("arbitrary"), name="moe_experts",
    )(block_e, xb, w_up, b_up.reshape(e, 1, f2), w_down, b_down.reshape(e, 1, d))


def _combine_kernel(x_ref, y_ref, p_ref, lnw_ref, lnb_ref, o_ref, *, alpha):
    p = p_ref[...]
    d = x_ref.shape[1]
    acc = alpha * x_ref[...]
    for k in range(TOP_K):
        acc = acc + p[:, k:k + 1] * y_ref[:, k * d:(k + 1) * d].astype(F32)
    o_ref[...] = _layer_norm(acc, lnw_ref[...], lnb_ref[...])


def combine_ln(x2d, y_gathered, top_p, ln_w, ln_b, alpha):
    n, d = x2d.shape
    t = min(n, 256)
    kern = functools.partial(_combine_kernel, alpha=alpha)
    return pl.pallas_call(
        kern, grid=(n // t,),
        in_specs=[pl.BlockSpec((t, d), lambda i: (i, 0)), pl.BlockSpec((t, TOP_K * d), lambda i: (i, 0)),
                  pl.BlockSpec((t, TOP_K), lambda i: (i, 0)),
                  pl.BlockSpec((1, d), lambda i: (0, 0)), pl.BlockSpec((1, d), lambda i: (0, 0))],
        out_specs=pl.BlockSpec((t, d), lambda i: (i, 0)),
        out_shape=jax.ShapeDtypeStruct((n, d), F32),
        compiler_params=_params("parallel"), name="moe_combine_ln",
    )(x2d, y_gathered.reshape(n, TOP_K * d), top_p, ln_w.reshape(1, d), ln_b.reshape(1, d))


def moe_ffn_ln(x_f32, x_bf16, router_w, router_b, w_up, b_up, w_down, b_down, ln_w, ln_b, alpha):
    n, d = x_f32.shape
    top_e, top_p, rank, counts = moe_router(x_f32, router_w, router_b)
    padded = (counts + EXPERT_BLOCK - 1) // EXPERT_BLOCK * EXPERT_BLOCK
    pad_end = jnp.cumsum(padded)
    pad_start = pad_end - padded
    n_assign = n * TOP_K
    n_blocks = -(-(n_assign + N_EXPERTS * (EXPERT_BLOCK - 1)) // EXPERT_BLOCK)
    dest = pad_start[top_e] + rank
    block_e = jnp.clip(jnp.searchsorted(pad_end, jnp.arange(n_blocks) * EXPERT_BLOCK, side='right'),
                       0, N_EXPERTS - 1).astype(jnp.int32)
    tok = jnp.broadcast_to(jnp.arange(n, dtype=jnp.int32)[:, None], (n, TOP_K))
    tok_buf = jnp.zeros((n_blocks * EXPERT_BLOCK,), jnp.int32).at[dest.reshape(-1)].set(tok.reshape(-1))
    xb = x_bf16[tok_buf]
    yb = moe_experts(xb, block_e, w_up.astype(BF16), b_up, w_down.astype(BF16), b_down)
    return combine_ln(x_f32, yb[dest], top_p, ln_w, ln_b, alpha)


def kernel(x, positions, w_in, hg_lower_bounds, hg_norm_w, gm_ln_w, gm_ln_b, gm_spatial_w, gm_spatial_b, gm_norm_w, nsa_cmp_pe, nsa_cmp_w1, nsa_cmp_w2, nsa_norm_w, w_out, ln1_w, ln1_b, router_w, router_b, exp_w_up, exp_b_up, exp_w_down, exp_b_down, ln2_w, ln2_b):
    batch, seq, d = x.shape
    depth = w_in.shape[0]
    n = batch * seq
    alpha = (2 * depth) ** 0.25
    hg_w = hg_norm_w.shape[1]
    gm_w = gm_norm_w.shape[1]
    nsa_w = nsa_norm_w.shape[1]
    kv_w = NSA_KV_GROUPS * HEAD_DIM
    in_width = w_in.shape[2]
    off_gm = 4 * hg_w
    off_q = off_gm + 2 * gm_w
    off_kv = off_q + nsa_w
    off_gate = off_kv + 6 * kv_w
    width_pad = -(-in_width // LANES) * LANES

    cosf, sinf = rope_tables(positions)
    lb_all = jnp.cumsum(jax.nn.softmax(hg_lower_bounds.astype(F32), axis=0), axis=0)
    lb_all = lb_all - lb_all[0:1]

    x2d = x.reshape(n, d)
    for l in range(depth):
        w_l = jnp.pad(w_in[l], ((0, 0), (0, width_pad - in_width))).astype(BF16)
        h = in_proj(x2d, w_l)
        y_hg = hgrn2(h, lb_all[l], hg_norm_w[l], batch, seq)
        y_gm = gmlp(h, gm_ln_w[l], gm_ln_b[l], gm_spatial_w[l], gm_spatial_b[l], gm_norm_w[l],
                    off_gm // gm_w, off_gm // gm_w + 1)
        h3 = h.reshape(batch, seq, width_pad)
        kc = compress(h3[:, :, off_kv:off_kv + kv_w], nsa_cmp_pe[l, 0], nsa_cmp_w1[l, 0], nsa_cmp_w2[l, 0], batch, seq)
        vc = compress(h3[:, :, off_kv + kv_w:off_kv + 2 * kv_w], nsa_cmp_pe[l, 1], nsa_cmp_w1[l, 1],
                      nsa_cmp_w2[l, 1], batch, seq)
        n_cmp = kc.shape[1]
        kc = kc.reshape(batch, n_cmp, NSA_KV_GROUPS, HEAD_DIM).transpose(0, 2, 1, 3)
        vc = vc.reshape(batch, n_cmp, NSA_KV_GROUPS, HEAD_DIM).transpose(0, 2, 1, 3)
        ksa, vs, kw, vw = nsa_kprep(h, cosf, sinf, batch, seq, (off_kv + 2 * kv_w) // LANES)
        y_nsa = nsa_attention(h, cosf, sinf, kc, vc, ksa, vs, kw, vw, nsa_norm_w[l], batch, seq,
                              off_q // (NSA_HPG * HEAD_DIM), off_gate // LANES)
        x1, x1b = out_proj_ln(x2d, y_hg, y_gm, y_nsa, w_out[l], ln1_w[l], ln1_b[l], alpha, batch, seq)
        x2d = moe_ffn_ln(x1, x1b, router_w[l], router_b[l], exp_w_up[l], exp_b_up[l], exp_w_down[l],
                         exp_b_down[l], ln2_w[l], ln2_b[l], alpha)
    return x2d.reshape(batch, seq, d)
```

```python
import functools
import math

import numpy as np
import jax
import jax.numpy as jnp
from jax import lax
from jax.experimental import pallas as pl
from jax.experimental.pallas import tpu as pltpu

F32 = jnp.float32
BF16 = jnp.bfloat16
HIGHEST = lax.Precision.HIGHEST

HEAD_DIM = 64
LANES = 128
VMEM_LIMIT = 48 * 1024 * 1024

HG_CHUNK = 64
GM_CHUNK = 128
NSA_KV_GROUPS = 2
NSA_HPG = 4
CMP_BLOCK = 32
CMP_STRIDE = 16
CMP_HIDDEN = 128
SEL_BLOCK = 64
N_SEL = 16
WINDOW = 512
N_GATES = 3
IMP_FORCE = 1e9
IMP_FUTURE = -1e9
NEG_INF = -1e30
N_EXPERTS = 32
TOP_K = 4
SWIGLU_ALPHA = 1.702
SWIGLU_LIMIT = 7.0
EXPERT_BLOCK = 512
ROPE_THETA = 10000.0
LN_EPS = 1e-5
RMS_EPS = 1e-6
V_ROWS = HEAD_DIM + 16


def _params(*sem):
    return pltpu.CompilerParams(dimension_semantics=sem, vmem_limit_bytes=VMEM_LIMIT)


def _dot(a, b):
    return jnp.dot(a, b, preferred_element_type=F32)


def _dot_nt(a, b, precision=None):
    return lax.dot_general(a, b, (((1,), (1,)), ((), ())), precision=precision,
                           preferred_element_type=F32)


def _dot_tn(a, b):
    return lax.dot_general(a, b, (((0,), (0,)), ((), ())), preferred_element_type=F32)


def _sigmoid(x):
    return 1.0 / (1.0 + jnp.exp(-x))


def _gelu(x):
    return 0.5 * x * (1.0 + jnp.tanh(0.7978845608028654 * (x + 0.044715 * x * x * x)))


def _layer_norm(x, w, b):
    mu = jnp.mean(x, axis=-1, keepdims=True)
    xc = x - mu
    var = jnp.mean(xc * xc, axis=-1, keepdims=True)
    return xc * lax.rsqrt(var + LN_EPS) * w + b


def _head_mean_sq(o, bd_ones):
    return jnp.dot(o * o, bd_ones, precision=HIGHEST, preferred_element_type=F32) * (1.0 / HEAD_DIM)


def _rope_kernel(pos_ref, posr_ref, inv_ref, sign_ref, invc_ref, signc_ref, cos_ref, sin_ref, cost_ref, sint_ref):
    ang = pos_ref[...] * inv_ref[...]
    cos_ref[...] = jnp.cos(ang)
    sin_ref[...] = jnp.sin(ang) * sign_ref[...]
    ang_t = invc_ref[...] * posr_ref[...]
    cost_ref[...] = jnp.cos(ang_t)
    sint_ref[...] = jnp.sin(ang_t) * signc_ref[...]


def rope_tables(positions):
    n = positions.size
    tile = min(n, 2048)
    posf = positions.reshape(n).astype(F32)
    pos = jnp.broadcast_to(posf[:, None], (n, HEAD_DIM))
    inv = ROPE_THETA ** (-jnp.arange(0, HEAD_DIM, 2, dtype=F32) / HEAD_DIM)
    inv = jnp.concatenate([inv, inv])
    sign = jnp.concatenate([-jnp.ones((HEAD_DIM // 2,), F32), jnp.ones((HEAD_DIM // 2,), F32)])
    row = pl.BlockSpec((tile, HEAD_DIM), lambda i: (i, 0))
    rowt = pl.BlockSpec((HEAD_DIM, tile), lambda i: (0, i))
    const = pl.BlockSpec((1, HEAD_DIM), lambda i: (0, 0))
    constc = pl.BlockSpec((HEAD_DIM, 1), lambda i: (0, 0))
    return pl.pallas_call(
        _rope_kernel, grid=(n // tile,),
        in_specs=[row, pl.BlockSpec((1, tile), lambda i: (0, i)), const, const, constc, constc],
        out_specs=[row, row, rowt, rowt],
        out_shape=[jax.ShapeDtypeStruct((n, HEAD_DIM), F32)] * 2 + [jax.ShapeDtypeStruct((HEAD_DIM, n), F32)] * 2,
        compiler_params=_params("parallel"), name="rope_tables",
    )(pos, posf.reshape(1, n), inv.reshape(1, HEAD_DIM), sign.reshape(1, HEAD_DIM),
      inv.reshape(HEAD_DIM, 1), sign.reshape(HEAD_DIM, 1))


def _in_proj_kernel(x_ref, w_ref, h_ref):
    h_ref[...] = _dot(x_ref[...].astype(BF16), w_ref[...])


def in_proj(x2d, w_bf16):
    n, d = x2d.shape
    width = w_bf16.shape[1]
    tile = min(n, 512)
    return pl.pallas_call(
        _in_proj_kernel, grid=(n // tile,),
        in_specs=[pl.BlockSpec((tile, d), lambda i: (i, 0)), pl.BlockSpec((d, width), lambda i: (0, 0))],
        out_specs=pl.BlockSpec((tile, width), lambda i: (i, 0)),
        out_shape=jax.ShapeDtypeStruct((n, width), F32),
        compiler_params=_params("parallel"), name="in_proj")(x2d, w_bf16)


HG_LEVELS = (0, 64, 32, 16, 8, 4, 2)


def _hgrn_constants():
    c = HG_CHUNK
    t = np.arange(c)
    tril = (t[:, None] >= t[None, :]).astype(np.float32)
    rows = [tril]
    masks = []
    for m in HG_LEVELS:
        if m == 0:
            rows.append(tril)
            masks.append(np.eye(c, dtype=np.float32))
            continue
        ref = (t // m) * m + m // 2 - 1
        rows.append(tril[ref])
        masks.append(((t[:, None] // m == t[None, :] // m) & (t[:, None] % m >= m // 2)
                      & (t[None, :] % m < m // 2)).astype(np.float32))
    pmat = np.concatenate(rows, axis=0)
    masks = np.stack([np.tile(mk, (4, 1)) for mk in masks])
    return pmat, masks


def _hgrn_kernel(q_ref, f_ref, i_ref, g_ref, lb_ref, nw_ref, pmat_ref, masks_ref, bd_ref, hm_ref,
                 o_ref, state_ref):
    c = HG_CHUNK

    @pl.when(pl.program_id(1) == 0)
    def _():
        state_ref[...] = jnp.zeros_like(state_ref)

    lb = lb_ref[...]
    fr = f_ref[...]
    hq = q_ref[...]
    qf = hq * _sigmoid(hq)
    log_sig = jnp.minimum(fr, 0.0) - jnp.log(1.0 + jnp.exp(-jnp.abs(fr)))
    a = jnp.log(lb)
    cc = jnp.log(1.0 - lb) + log_sig
    log_f = jnp.maximum(a, cc) + jnp.log(1.0 + jnp.exp(-jnp.abs(a - cc)))
    kk = (1.0 - lb) * _sigmoid(-fr)
    v = i_ref[...]
    vb = v.astype(BF16)
    bd = bd_ref[...]
    hm = hm_ref[...]

    allb = jnp.dot(pmat_ref[...], log_f, precision=HIGHEST, preferred_element_type=F32)
    b = allb[0:c]

    att = jnp.zeros((4 * c, c), F32)
    for li in range(len(HG_LEVELS)):
        r = allb[(li + 1) * c:(li + 2) * c]
        ql = qf * jnp.exp(jnp.minimum(b - r, 0.0))
        kl = (kk * jnp.exp(jnp.minimum(r - b, 0.0))).astype(BF16)
        qs = jnp.concatenate([ql * hm[h:h + 1] for h in range(4)], axis=0).astype(BF16)
        att = att + masks_ref[li] * _dot_nt(qs, kl)
    o_st = _dot(att.astype(BF16), vb)
    o = jnp.zeros_like(v)
    for h in range(4):
        o = o + o_st[h * c:(h + 1) * c] * hm[h:h + 1]

    st = state_ref[...]
    o = o + _dot_nt((qf * jnp.exp(b)).astype(BF16), st.astype(BF16))
    b_last = b[c - 1:c]
    kdec = (kk * jnp.exp(b_last - b)).astype(BF16)
    state_ref[...] = st * jnp.exp(b_last) + bd * _dot_tn(vb, kdec)

    ms = _head_mean_sq(o, bd)
    y = o * lax.rsqrt(ms + RMS_EPS) * nw_ref[...] * _sigmoid(g_ref[...])
    o_ref[...] = y.astype(o_ref.dtype)


def hgrn2(h, lb, norm_w, batch, seq):
    n = h.shape[0]
    w = lb.shape[-1]
    c = HG_CHUNK
    nch = seq // c
    pmat, masks = _hgrn_constants()
    lane_head = np.arange(w) // HEAD_DIM
    bd = (lane_head[:, None] == lane_head[None, :]).astype(np.float32)
    hm = (np.arange(4)[:, None] == lane_head[None, :]).astype(np.float32)

    def col(j):
        return pl.BlockSpec((c, w), lambda b, i, j=j: (b * nch + i, j))

    def const(shape):
        return pl.BlockSpec(shape, lambda b, i: (0,) * len(shape))

    return pl.pallas_call(
        _hgrn_kernel, grid=(batch, nch),
        in_specs=[col(0), col(1), col(2), col(3), const((1, w)), const((1, w)),
                  const(pmat.shape), const(masks.shape), const(bd.shape), const(hm.shape)],
        out_specs=pl.BlockSpec((c, w), lambda b, i: (b * nch + i, 0)),
        out_shape=jax.ShapeDtypeStruct((n, w), BF16),
        scratch_shapes=[pltpu.VMEM((w, w), F32)],
        compiler_params=_params("parallel", "arbitrary"), name="hgrn2",
    )(h, h, h, h, lb.reshape(1, w), norm_w.reshape(1, w), jnp.asarray(pmat), jnp.asarray(masks),
      jnp.asarray(bd), jnp.asarray(hm))


def _gmlp_kernel(u_ref, v_ref, lnw_ref, lnb_ref, ws_ref, bias_ref, nw_ref, bd_ref, hm_ref, o_ref):
    c = GM_CHUNK
    u = _gelu(u_ref[...])
    v = _layer_norm(_gelu(v_ref[...]), lnw_ref[...], lnb_ref[...])
    vb = v.astype(BF16)
    hm = hm_ref[...]
    causal = lax.broadcasted_iota(jnp.int32, (c, c), 0) >= lax.broadcasted_iota(jnp.int32, (c, c), 1)
    sv = bias_ref[...]
    for g in range(ws_ref.shape[0]):
        wg = jnp.where(causal, ws_ref[g], 0.0).astype(BF16)
        sv = sv + _dot(wg, vb) * hm[g:g + 1]
    y = u * sv
    ms = _head_mean_sq(y, bd_ref[...])
    o_ref[...] = (y * lax.rsqrt(ms + RMS_EPS) * nw_ref[...]).astype(o_ref.dtype)


def gmlp(h, ln_w, ln_b, w_s, b_s, norm_w, u_col, v_col):
    n = h.shape[0]
    groups, c, _ = w_s.shape
    w = groups * HEAD_DIM
    lane_head = np.arange(w) // HEAD_DIM
    bd = (lane_head[:, None] == lane_head[None, :]).astype(np.float32)
    hm = (np.arange(groups)[:, None] == lane_head[None, :]).astype(np.float32)
    bias = jnp.repeat(b_s.T, HEAD_DIM, axis=1)

    def const(shape):
        return pl.BlockSpec(shape, lambda i: (0,) * len(shape))

    return pl.pallas_call(
        _gmlp_kernel, grid=(n // c,),
        in_specs=[pl.BlockSpec((c, w), lambda i: (i, u_col)), pl.BlockSpec((c, w), lambda i: (i, v_col)),
                  const((1, w)), const((1, w)), const(w_s.shape), const((c, w)), const((1, w)),
                  const(bd.shape), const(hm.shape)],
        out_specs=pl.BlockSpec((c, w), lambda i: (i, 0)),
        out_shape=jax.ShapeDtypeStruct((n, w), BF16),
        compiler_params=_params("parallel"), name="gmlp",
    )(h, h, ln_w.reshape(1, w), ln_b.reshape(1, w), w_s, bias, norm_w.reshape(1, w),
      jnp.asarray(bd), jnp.asarray(hm))


def _compress_kernel(u_ref, wtop_ref, wbot_ref, pe_ref, w2_ref, o_ref):
    u = u_ref[...].astype(BF16)
    wtop = wtop_ref[...]
    wbot = wbot_ref[...]
    pe = pe_ref[...].astype(BF16)
    const = _dot(pe[0:1], wtop) + _dot(pe[1:2], wbot)
    p = _dot(u, wtop)
    q = _dot(u, wbot)
    q_next = jnp.concatenate([q[1:], jnp.zeros_like(q[0:1])], axis=0)
    hid = _gelu(p + q_next + const)
    o_ref[...] = _dot(hid.astype(BF16), w2_ref[...]).astype(o_ref.dtype)


def compress(kv, pe, w1, w2, batch, seq):
    g = NSA_KV_GROUPS
    half = CMP_STRIDE
    units = seq // half
    gw = g * HEAD_DIM
    u = kv.reshape(batch * units, half * gw)
    eye = jnp.eye(g, dtype=F32)
    w1r = w1.reshape(2, half, HEAD_DIM, CMP_HIDDEN)
    wbd = jnp.einsum('hjdn,gk->hjgdkn', w1r, eye).reshape(2, half * gw, g * CMP_HIDDEN).astype(BF16)
    w2bd = jnp.einsum('nd,gk->gnkd', w2, eye).reshape(g * CMP_HIDDEN, gw).astype(BF16)
    pe2 = jnp.broadcast_to(pe.reshape(2, half, 1, HEAD_DIM), (2, half, g, HEAD_DIM)).reshape(2, half * gw)

    def const(shape):
        return pl.BlockSpec(shape, lambda b: (0,) * len(shape))

    out = pl.pallas_call(
        _compress_kernel, grid=(batch,),
        in_specs=[pl.BlockSpec((units, half * gw), lambda b: (b, 0)),
                  const(wbd.shape[1:]), const(wbd.shape[1:]), const(pe2.shape), const(w2bd.shape)],
        out_specs=pl.BlockSpec((units, gw), lambda b: (b, 0)),
        out_shape=jax.ShapeDtypeStruct((batch * units, gw), BF16),
        compiler_params=_params("parallel"), name="nsa_compress",
    )(u, wbd[0], wbd[1], pe2, w2bd)
    return out.reshape(batch, units, gw)


def _rot_half_pairs(x):
    lane = lax.broadcasted_iota(jnp.int32, x.shape, 1)
    fwd = pltpu.roll(x, 32, axis=1)
    bwd = pltpu.roll(x, 96, axis=1)
    return jnp.where((lane % HEAD_DIM) < HEAD_DIM // 2, bwd, fwd)


def _kprep_kernel(ks_ref, vs_ref, kw_ref, vw_ref, cos_ref, sin_ref, ksa_ref, vso_ref, kwo_ref, vwo_ref):
    t = ks_ref.shape[0]
    cos = cos_ref[...]
    sin = sin_ref[...]
    cos2 = jnp.concatenate([cos, cos], axis=1)
    sin2 = jnp.concatenate([sin, sin], axis=1)
    ks = ks_ref[...]
    kw = kw_ref[...]
    ks_r = ks * cos2 + _rot_half_pairs(ks) * sin2
    kw_r = kw * cos2 + _rot_half_pairs(kw) * sin2
    pos = pl.program_id(1) * t + lax.broadcasted_iota(jnp.int32, (t, HEAD_DIM), 0)
    onehot = (pos // SEL_BLOCK == lax.broadcasted_iota(jnp.int32, (t, HEAD_DIM), 1)).astype(F32)
    vs_t = vs_ref[...].T
    vw_t = vw_ref[...].T
    tail = (lax.broadcasted_iota(jnp.int32, (V_ROWS - HEAD_DIM, t), 0) == 0).astype(F32)
    for g in range(NSA_KV_GROUPS):
        sl = slice(g * HEAD_DIM, (g + 1) * HEAD_DIM)
        ksa_ref[g] = jnp.concatenate([ks_r[:, sl], onehot], axis=1).astype(BF16)
        vso_ref[g] = jnp.concatenate([vs_t[sl], tail], axis=0).astype(BF16)
        kwo_ref[g] = kw_r[:, sl].astype(BF16)
        vwo_ref[g] = jnp.concatenate([vw_t[sl], tail], axis=0).astype(BF16)


def nsa_kprep(h, cosf, sinf, batch, seq, col0):
    g = NSA_KV_GROUPS
    t = min(seq, 512)
    nt = seq // t

    def col(j):
        return pl.BlockSpec((t, LANES), lambda b, i, j=j: (b * nt + i, col0 + j))

    tab = pl.BlockSpec((t, HEAD_DIM), lambda b, i: (b * nt + i, 0))

    def out(wd):
        return pl.BlockSpec((None, g, t, wd), lambda b, i: (b, 0, i, 0))

    out_t = pl.BlockSpec((None, g, V_ROWS, t), lambda b, i: (b, 0, 0, i))
    k_shape = jax.ShapeDtypeStruct((batch, g, seq, HEAD_DIM), BF16)
    v_shape = jax.ShapeDtypeStruct((batch, g, V_ROWS, seq), BF16)
    return pl.pallas_call(
        _kprep_kernel, grid=(batch, nt),
        in_specs=[col(0), col(1), col(2), col(3), tab, tab],
        out_specs=[out(2 * HEAD_DIM), out_t, out(HEAD_DIM), out_t],
        out_shape=[jax.ShapeDtypeStruct((batch, g, seq, 2 * HEAD_DIM), BF16), v_shape, k_shape, v_shape],
        compiler_params=_params("parallel", "parallel"), name="nsa_kprep",
    )(h, h, h, h, cosf, sinf)


def _nsa_kernel(hq_ref, gate_ref, cos_ref, sin_ref, kc_ref, vc_ref, ksa_ref, vs_ref, kw_ref, vw_ref,
                ovl_ref, nw_ref, o_ref, imp_ref, *, tq, tk, n_sb):
    g = pl.program_id(1)
    qi = pl.program_id(2)
    hpg = NSA_HPG
    rows = hpg * tq
    t0 = qi * tq
    scale = 1.0 / math.sqrt(HEAD_DIM)
    half = HEAD_DIM // 2

    hq_t = hq_ref[...].T
    cos = cos_ref[...]
    sin = sin_ref[...]
    q_raw, q_rot = [], []
    for h in range(hpg):
        qh = hq_t[h * HEAD_DIM:(h + 1) * HEAD_DIM]
        swapped = jnp.concatenate([qh[half:], qh[:half]], axis=0)
        q_raw.append(qh * scale)
        q_rot.append((qh * cos + swapped * sin) * scale)
    q_raw = jnp.concatenate(q_raw, axis=1).astype(BF16)
    q_rot = jnp.concatenate(q_rot, axis=1)

    tpos = t0 + lax.broadcasted_iota(jnp.int32, (1, tq), 1)
    tpos_r = jnp.concatenate([tpos] * hpg, axis=1)

    def flash_step(s, v_t, m, acc):
        m_new = jnp.maximum(m, jnp.max(s, axis=0, keepdims=True))
        alpha = jnp.exp(m - m_new)
        p = jnp.exp(s - m_new).astype(BF16)
        return m_new, alpha * acc + _dot(v_t, p)

    init = (jnp.full((1, rows), NEG_INF, F32), jnp.zeros((V_ROWS, rows), F32))

    wk = WINDOW + tq
    k0 = pl.multiple_of(jnp.maximum(t0 - WINDOW, 0), tq)
    kpos = k0 + lax.broadcasted_iota(jnp.int32, (wk, 1), 0)
    s = _dot(kw_ref[pl.ds(k0, wk), :], q_rot.astype(BF16))
    s = jnp.where((kpos <= tpos_r) & (kpos > tpos_r - WINDOW), s, NEG_INF)
    m_w, acc_w = flash_step(s, vw_ref[:, pl.ds(k0, wk)], *init)
    n_pad = jnp.maximum(WINDOW - 1 - tpos_r, 0).astype(F32)
    m_f = jnp.where(n_pad > 0.0, jnp.maximum(m_w, 0.0), m_w)
    a_w = jnp.exp(m_w - m_f)
    o_w = acc_w[0:HEAD_DIM] * (a_w / (acc_w[HEAD_DIM:HEAD_DIM + 1] * a_w + n_pad * jnp.exp(-m_f)))

    n_cmp = kc_ref.shape[0]
    s_c = _dot(kc_ref[...], q_raw)
    cmp_end = lax.broadcasted_iota(jnp.int32, (n_cmp, 1), 0) * CMP_STRIDE + (CMP_BLOCK - 1)
    mask_c = cmp_end <= tpos_r
    s_c = jnp.where(mask_c, s_c, NEG_INF)
    e_c = jnp.exp(s_c - jnp.max(s_c, axis=0, keepdims=True))
    p_c = jnp.where(mask_c, e_c * (1.0 / jnp.sum(e_c, axis=0, keepdims=True)), 0.0)
    o_c = _dot(vc_ref[...], p_c.astype(BF16))

    p_sum = p_c[:, 0:tq]
    for h in range(1, hpg):
        p_sum = p_sum + p_c[:, h * tq:(h + 1) * tq]
    imp = jnp.dot(ovl_ref[...], p_sum, precision=HIGHEST, preferred_element_type=F32)
    blk = lax.broadcasted_iota(jnp.int32, (n_sb, tq), 0)
    cur = (t0 + lax.broadcasted_iota(jnp.int32, (n_sb, tq), 1)) // SEL_BLOCK
    imp = jnp.where(blk > cur, IMP_FUTURE, imp)
    imp = jnp.where((blk == 0) | (blk == cur) | (blk == cur - 1), IMP_FORCE, imp)
    imp_ref[...] = imp

    def rank_body(j, rank):
        row = imp_ref[pl.ds(j, 1), :]
        ge = jnp.where(row >= imp, 1, 0)
        gt = jnp.where(row > imp, 1, 0)
        return rank + jnp.where(j < blk, ge, gt)

    rank = lax.fori_loop(0, n_sb, rank_body, jnp.zeros((n_sb, tq), jnp.int32), unroll=8)
    sel_bias = jnp.where(rank < min(N_SEL, n_sb), 0.0, NEG_INF)
    if n_sb < HEAD_DIM:
        sel_bias = jnp.concatenate([sel_bias, jnp.zeros((HEAD_DIM - n_sb, tq), F32)], axis=0)
    q_aug = jnp.concatenate([q_rot, jnp.concatenate([sel_bias] * hpg, axis=1)], axis=0).astype(BF16)

    def scores(kt):
        k0 = pl.multiple_of(kt * tk, tk)
        return _dot(ksa_ref[pl.ds(k0, tk), :], q_aug)

    def sel_body(kt, carry):
        return flash_step(scores(kt), vs_ref[:, pl.ds(pl.multiple_of(kt * tk, tk), tk)], *carry)

    n_full = t0 // tk
    m_s, acc_s = lax.fori_loop(0, n_full, sel_body, init)
    k0 = pl.multiple_of(n_full * tk, tk)
    kpos = k0 + lax.broadcasted_iota(jnp.int32, (tk, 1), 0)
    _, acc_s = flash_step(jnp.where(kpos <= tpos_r, scores(n_full), NEG_INF), vs_ref[:, pl.ds(k0, tk)], m_s, acc_s)
    o_s = acc_s[0:HEAD_DIM] * (1.0 / acc_s[HEAD_DIM:HEAD_DIM + 1])

    gates = _sigmoid(gate_ref[...].T)
    nw = nw_ref[...]
    outs = []
    for h in range(hpg):
        sl = slice(h * tq, (h + 1) * tq)
        gh = []
        for i in range(N_GATES):
            row = jnp.zeros((1, tq), F32)
            for gi in range(NSA_KV_GROUPS):
                r = (gi * hpg + h) * N_GATES + i
                row = jnp.where(g == gi, gates[r:r + 1], row)
            gh.append(row)
        o = gh[0] * o_c[:, sl] + gh[1] * o_s[:, sl] + gh[2] * o_w[:, sl]
        ms = jnp.mean(o * o, axis=0, keepdims=True)
        outs.append(o * lax.rsqrt(ms + RMS_EPS) * nw[:, h:h + 1])
    o_ref[...] = jnp.concatenate(outs, axis=0).T.astype(o_ref.dtype)


def nsa_attention(h, cos_t, sin_t, kc, vc_t, ksa, vs_t, kw, vw_t, norm_w, batch, seq, q_col0, gate_col):
    g, hpg = NSA_KV_GROUPS, NSA_HPG
    tq = min(seq, 256)
    nq = seq // tq
    n_sb = seq // SEL_BLOCK
    assert n_sb <= HEAD_DIM, "selection-block one-hot shares the 64 spare key lanes"
    n_cmp = kc.shape[2]
    units = np.arange(n_cmp)[:, None] + np.arange(CMP_BLOCK // CMP_STRIDE)[None, :]
    ovl = np.zeros((n_cmp, n_sb), np.float32)
    for c in range((seq - CMP_BLOCK) // CMP_STRIDE + 1):
        for u in units[c]:
            ovl[c, u // (SEL_BLOCK // CMP_STRIDE)] += 1.0
    ovl_t = jnp.asarray(ovl.T)

    def per_bg(shape):
        return pl.BlockSpec((None, None) + shape, lambda b, gi, qi: (b, gi, 0, 0))

    tab = pl.BlockSpec((HEAD_DIM, tq), lambda b, gi, qi: (0, b * nq + qi))
    tk = min(seq, 512)
    assert seq >= WINDOW + tq and seq % tk == 0 and tk % tq == 0
    kern = functools.partial(_nsa_kernel, tq=tq, tk=tk, n_sb=n_sb)
    return pl.pallas_call(
        kern, grid=(batch, g, nq),
        in_specs=[pl.BlockSpec((tq, hpg * HEAD_DIM), lambda b, gi, qi: (b * nq + qi, q_col0 + gi)),
                  pl.BlockSpec((tq, LANES), lambda b, gi, qi: (b * nq + qi, gate_col)),
                  tab, tab,
                  per_bg((n_cmp, HEAD_DIM)), per_bg((HEAD_DIM, n_cmp)),
                  per_bg((seq, 2 * HEAD_DIM)), per_bg((V_ROWS, seq)),
                  per_bg((seq, HEAD_DIM)), per_bg((V_ROWS, seq)),
                  pl.BlockSpec((n_sb, n_cmp), lambda b, gi, qi: (0, 0)),
                  pl.BlockSpec((None, HEAD_DIM, hpg), lambda b, gi, qi: (gi, 0, 0))],
        out_specs=pl.BlockSpec((tq, hpg * HEAD_DIM), lambda b, gi, qi: (b * nq + qi, gi)),
        out_shape=jax.ShapeDtypeStruct((batch * seq, g * hpg * HEAD_DIM), BF16),
        scratch_shapes=[pltpu.VMEM((n_sb, tq), F32)],
        compiler_params=_params("parallel", "parallel", "arbitrary"), name="nsa_attention",
    )(h, h, cos_t, sin_t, kc, vc_t, ksa, vs_t, kw, vw_t, ovl_t,
      norm_w.reshape(g, hpg, HEAD_DIM).transpose(0, 2, 1))


def _out_proj_kernel(x_ref, yhg_ref, ygm_ref, ynsa_ref, whg_ref, wgm_ref, wnsa_ref, lnw_ref, lnb_ref,
                     o_ref, ob_ref, *, alpha):
    mix = (_dot(yhg_ref[...], whg_ref[...]) + _dot(ygm_ref[...], wgm_ref[...])
           + _dot(ynsa_ref[...], wnsa_ref[...]))
    y = _layer_norm(alpha * x_ref[...] + mix, lnw_ref[...], lnb_ref[...])
    o_ref[...] = y
    ob_ref[...] = y.astype(BF16)


def out_proj_ln(x2d, y_hg, y_gm, y_nsa, w_out, ln_w, ln_b, alpha):
    n, d = x2d.shape
    w1, w2 = y_hg.shape[1], y_hg.shape[1] + y_gm.shape[1]
    whg = w_out[:w1].astype(BF16)
    wgm = w_out[w1:w2].astype(BF16)
    wnsa = w_out[w2:].astype(BF16)
    t = min(n, 512)

    def row(wd):
        return pl.BlockSpec((t, wd), lambda i: (i, 0))

    def const(shape):
        return pl.BlockSpec(shape, lambda i: (0,) * len(shape))

    kern = functools.partial(_out_proj_kernel, alpha=alpha)
    return pl.pallas_call(
        kern, grid=(n // t,),
        in_specs=[row(d), row(y_hg.shape[1]), row(y_gm.shape[1]), row(y_nsa.shape[1]),
                  const(whg.shape), const(wgm.shape), const(wnsa.shape), const((1, d)), const((1, d))],
        out_specs=[row(d), row(d)],
        out_shape=[jax.ShapeDtypeStruct((n, d), F32), jax.ShapeDtypeStruct((n, d), BF16)],
        compiler_params=_params("parallel"), name="out_proj_ln",
    )(x2d, y_hg, y_gm, y_nsa, whg, wgm, wnsa, ln_w.reshape(1, d), ln_b.reshape(1, d))


def _router_kernel(x_ref, w_ref, b_ref, e_ref, p_ref, r_ref, cnt_ref, carry_ref):
    t = x_ref.shape[0]

    @pl.when(pl.program_id(0) == 0)
    def _():
        carry_ref[...] = jnp.zeros_like(carry_ref)

    logits = jnp.dot(x_ref[...], w_ref[...], precision=HIGHEST, preferred_element_type=F32) + b_ref[...]
    lane = lax.broadcasted_iota(jnp.int32, logits.shape, 1)
    work = logits
    vals, idxs = [], []
    sel = jnp.zeros(logits.shape, F32)
    for _ in range(TOP_K):
        m = jnp.max(work, axis=-1, keepdims=True)
        idx = jnp.min(jnp.where(work == m, lane, LANES), axis=-1, keepdims=True)
        hit = lane == idx
        sel = jnp.where(hit, 1.0, sel)
        work = jnp.where(hit, -jnp.inf, work)
        vals.append(m)
        idxs.append(idx)
    exps = [jnp.exp(v - vals[0]) for v in vals]
    den = exps[0] + exps[1] + exps[2] + exps[3]
    strict = (lax.broadcasted_iota(jnp.int32, (t, t), 0) > lax.broadcasted_iota(jnp.int32, (t, t), 1))
    before = _dot(strict.astype(BF16), sel.astype(BF16)) + carry_ref[...]
    ranks = [jnp.sum(jnp.where(lane == idx, before, 0.0), axis=-1, keepdims=True) for idx in idxs]
    kcol = lax.broadcasted_iota(jnp.int32, (t, TOP_K), 1)
    e_out = jnp.zeros((t, TOP_K), jnp.int32)
    p_out = jnp.zeros((t, TOP_K), F32)
    r_out = jnp.zeros((t, TOP_K), jnp.int32)
    for k in range(TOP_K):
        e_out = jnp.where(kcol == k, idxs[k], e_out)
        p_out = jnp.where(kcol == k, exps[k] / den, p_out)
        r_out = jnp.where(kcol == k, ranks[k].astype(jnp.int32), r_out)
    e_ref[...] = e_out
    p_ref[...] = p_out
    r_ref[...] = r_out
    carry_ref[...] = carry_ref[...] + jnp.sum(sel, axis=0, keepdims=True)
    cnt_ref[...] = carry_ref[...].astype(jnp.int32)


def moe_router(x2d, router_w, router_b):
    n, d = x2d.shape
    e = router_w.shape[1]
    t = min(n, 512)
    w = jnp.zeros((d, LANES), F32).at[:, :e].set(router_w)
    b = jnp.full((1, LANES), NEG_INF, F32).at[0, :e].set(router_b)
    row4 = pl.BlockSpec((t, TOP_K), lambda i: (i, 0))
    top_e, top_p, rank, counts = pl.pallas_call(
        _router_kernel, grid=(n // t,),
        in_specs=[pl.BlockSpec((t, d), lambda i: (i, 0)), pl.BlockSpec((d, LANES), lambda i: (0, 0)),
                  pl.BlockSpec((1, LANES), lambda i: (0, 0))],
        out_specs=[row4, row4, row4, pl.BlockSpec((1, LANES), lambda i: (0, 0))],
        out_shape=[jax.ShapeDtypeStruct((n, TOP_K), jnp.int32), jax.ShapeDtypeStruct((n, TOP_K), F32),
                   jax.ShapeDtypeStruct((n, TOP_K), jnp.int32), jax.ShapeDtypeStruct((1, LANES), jnp.int32)],
        scratch_shapes=[pltpu.VMEM((1, LANES), F32)],
        compiler_params=_params("arbitrary"), name="moe_router",
    )(x2d, w, b)
    return top_e, top_p, rank, counts[0, :e]


def _expert_kernel(be_ref, x_ref, wu_ref, bu_ref, wd_ref, bd_ref, o_ref):
    f = wd_ref.shape[0]
    hcat = _dot(x_ref[...], wu_ref[...]) + bu_ref[...]
    glu = jnp.minimum(hcat[:, :f], SWIGLU_LIMIT)
    lin = jnp.clip(hcat[:, f:], -SWIGLU_LIMIT, SWIGLU_LIMIT)
    act = glu * _sigmoid(SWIGLU_ALPHA * glu) * (lin + 1.0)
    o_ref[...] = (_dot(act.astype(BF16), wd_ref[...]) + bd_ref[...]).astype(o_ref.dtype)


def moe_experts(xb, block_e, w_up, b_up, w_down, b_down):
    rows, d = xb.shape
    e, _, f2 = w_up.shape
    f = f2 // 2
    nb = rows // EXPERT_BLOCK
    grid_spec = pltpu.PrefetchScalarGridSpec(
        num_scalar_prefetch=1, grid=(nb,),
        in_specs=[pl.BlockSpec((EXPERT_BLOCK, d), lambda i, be: (i, 0)),
                  pl.BlockSpec((None, d, f2), lambda i, be: (be[i], 0, 0)),
                  pl.BlockSpec((None, 1, f2), lambda i, be: (be[i], 0, 0)),
                  pl.BlockSpec((None, f, d), lambda i, be: (be[i], 0, 0)),
                  pl.BlockSpec((None, 1, d), lambda i, be: (be[i], 0, 0))],
        out_specs=pl.BlockSpec((EXPERT_BLOCK, d), lambda i, be: (i, 0)))
    return pl.pallas_call(
        _expert_kernel, grid_spec=grid_spec,
        out_shape=jax.ShapeDtypeStruct((rows, d), BF16),
        compiler_params=_params("arbitrary"), name="moe_experts",
    )(block_e, xb, w_up, b_up.reshape(e, 1, f2), w_down, b_down.reshape(e, 1, d))


def _combine_kernel(x_ref, y_ref, p_ref, lnw_ref, lnb_ref, o_ref, *, alpha):
    p = p_ref[...]
    d = x_ref.shape[1]
    acc = alpha * x_ref[...]
    for k in range(TOP_K):
        acc = acc + p[:, k:k + 1] * y_ref[:, k * d:(k + 1) * d].astype(F32)
    o_ref[...] = _layer_norm(acc, lnw_ref[...], lnb_ref[...])


def combine_ln(x2d, y_gathered, top_p, ln_w, ln_b, alpha):
    n, d = x2d.shape
    t = min(n, 256)
    kern = functools.partial(_combine_kernel, alpha=alpha)
    return pl.pallas_call(
        kern, grid=(n // t,),
        in_specs=[pl.BlockSpec((t, d), lambda i: (i, 0)), pl.BlockSpec((t, TOP_K * d), lambda i: (i, 0)),
                  pl.BlockSpec((t, TOP_K), lambda i: (i, 0)),
                  pl.BlockSpec((1, d), lambda i: (0, 0)), pl.BlockSpec((1, d), lambda i: (0, 0))],
        out_specs=pl.BlockSpec((t, d), lambda i: (i, 0)),
        out_shape=jax.ShapeDtypeStruct((n, d), F32),
        compiler_params=_params("parallel"), name="moe_combine_ln",
    )(x2d, y_gathered.reshape(n, TOP_K * d), top_p, ln_w.reshape(1, d), ln_b.reshape(1, d))


def moe_ffn_ln(x_f32, x_bf16, router_w, router_b, w_up, b_up, w_down, b_down, ln_w, ln_b, alpha):
    n, d = x_f32.shape
    top_e, top_p, rank, counts = moe_router(x_f32, router_w, router_b)
    padded = (counts + EXPERT_BLOCK - 1) // EXPERT_BLOCK * EXPERT_BLOCK
    pad_end = jnp.cumsum(padded)
    pad_start = pad_end - padded
    n_assign = n * TOP_K
    n_blocks = -(-(n_assign + N_EXPERTS * (EXPERT_BLOCK - 1)) // EXPERT_BLOCK)
    dest = pad_start[top_e] + rank
    block_e = jnp.clip(jnp.searchsorted(pad_end, jnp.arange(n_blocks) * EXPERT_BLOCK, side='right'),
                       0, N_EXPERTS - 1).astype(jnp.int32)
    tok = jnp.broadcast_to(jnp.arange(n, dtype=jnp.int32)[:, None], (n, TOP_K))
    tok_buf = jnp.zeros((n_blocks * EXPERT_BLOCK,), jnp.int32).at[dest.reshape(-1)].set(tok.reshape(-1))
    xb = x_bf16[tok_buf]
    yb = moe_experts(xb, block_e, w_up.astype(BF16), b_up, w_down.astype(BF16), b_down)
    return combine_ln(x_f32, yb[dest], top_p, ln_w, ln_b, alpha)


def kernel(x, positions, w_in, hg_lower_bounds, hg_norm_w, gm_ln_w, gm_ln_b, gm_spatial_w, gm_spatial_b, gm_norm_w, nsa_cmp_pe, nsa_cmp_w1, nsa_cmp_w2, nsa_norm_w, w_out, ln1_w, ln1_b, router_w, router_b, exp_w_up, exp_b_up, exp_w_down, exp_b_down, ln2_w, ln2_b):
    batch, seq, d = x.shape
    depth = w_in.shape[0]
    n = batch * seq
    alpha = (2 * depth) ** 0.25
    hg_w = hg_norm_w.shape[1]
    gm_w = gm_norm_w.shape[1]
    nsa_w = nsa_norm_w.shape[1]
    kv_w = NSA_KV_GROUPS * HEAD_DIM
    in_width = w_in.shape[2]
    off_gm = 4 * hg_w
    off_q = off_gm + 2 * gm_w
    off_kv = off_q + nsa_w
    off_gate = off_kv + 6 * kv_w
    width_pad = -(-in_width // LANES) * LANES

    cosf, sinf, cos_t, sin_t = rope_tables(positions)
    lb_all = jnp.cumsum(jax.nn.softmax(hg_lower_bounds.astype(F32), axis=0), axis=0)
    lb_all = lb_all - lb_all[0:1]

    x2d = x.reshape(n, d)
    for l in range(depth):
        w_l = jnp.pad(w_in[l], ((0, 0), (0, width_pad - in_width))).astype(BF16)
        h = in_proj(x2d, w_l)
        y_hg = hgrn2(h, lb_all[l], hg_norm_w[l], batch, seq)
        y_gm = gmlp(h, gm_ln_w[l], gm_ln_b[l], gm_spatial_w[l], gm_spatial_b[l], gm_norm_w[l],
                    off_gm // gm_w, off_gm // gm_w + 1)
        h3 = h.reshape(batch, seq, width_pad)
        kc = compress(h3[:, :, off_kv:off_kv + kv_w], nsa_cmp_pe[l, 0], nsa_cmp_w1[l, 0], nsa_cmp_w2[l, 0], batch, seq)
        vc = compress(h3[:, :, off_kv + kv_w:off_kv + 2 * kv_w], nsa_cmp_pe[l, 1], nsa_cmp_w1[l, 1],
                      nsa_cmp_w2[l, 1], batch, seq)
        n_cmp = kc.shape[1]
        kc = kc.reshape(batch, n_cmp, NSA_KV_GROUPS, HEAD_DIM).transpose(0, 2, 1, 3)
        vc_t = vc.reshape(batch, n_cmp, NSA_KV_GROUPS, HEAD_DIM).transpose(0, 2, 3, 1)
        ksa, vs_t, kw, vw_t = nsa_kprep(h, cosf, sinf, batch, seq, (off_kv + 2 * kv_w) // LANES)
        y_nsa = nsa_attention(h, cos_t, sin_t, kc, vc_t, ksa, vs_t, kw, vw_t, nsa_norm_w[l], batch, seq,
                              off_q // (NSA_HPG * HEAD_DIM), off_gate // LANES)
        x1, x1b = out_proj_ln(x2d, y_hg, y_gm, y_nsa, w_out[l], ln1_w[l], ln1_b[l], alpha)
        x2d = moe_ffn_ln(x1, x1b, router_w[l], router_b[l], exp_w_up[l], exp_b_up[l], exp_w_down[l],
                         exp_b_down[l], ln2_w[l], ln2_b[l], alpha)
    return x2d.reshape(batch, seq, d)
```

```python
import functools
import math

import numpy as np
import jax
import jax.numpy as jnp
from jax import lax
from jax.experimental import pallas as pl
from jax.experimental.pallas import tpu as pltpu

F32 = jnp.float32
BF16 = jnp.bfloat16
HIGHEST = lax.Precision.HIGHEST

HEAD_DIM = 64
LANES = 128
VMEM_LIMIT = 48 * 1024 * 1024
EXPERT_VMEM_LIMIT = 56 * 1024 * 1024

HG_CHUNK = 64
GM_CHUNK = 128
NSA_KV_GROUPS = 2
NSA_HPG = 4
CMP_BLOCK = 32
CMP_STRIDE = 16
CMP_HIDDEN = 128
SEL_BLOCK = 64
N_SEL = 16
WINDOW = 512
N_GATES = 3
IMP_FORCE = 1e9
IMP_FUTURE = -1e9
NEG_INF = -1e30
N_EXPERTS = 32
TOP_K = 4
SWIGLU_ALPHA = 1.702
SWIGLU_LIMIT = 7.0
EXPERT_BLOCK = 512
ROPE_THETA = 10000.0
LN_EPS = 1e-5
RMS_EPS = 1e-6
V_ROWS = HEAD_DIM + 16


def _params(*sem):
    return pltpu.CompilerParams(dimension_semantics=sem, vmem_limit_bytes=VMEM_LIMIT)


def _dot(a, b):
    return jnp.dot(a, b, preferred_element_type=F32)


def _dot_nt(a, b, precision=None):
    return lax.dot_general(a, b, (((1,), (1,)), ((), ())), precision=precision,
                           preferred_element_type=F32)


def _dot_tn(a, b):
    return lax.dot_general(a, b, (((0,), (0,)), ((), ())), preferred_element_type=F32)


def _sigmoid(x):
    return 1.0 / (1.0 + jnp.exp(-x))


def _gelu(x):
    return 0.5 * x * (1.0 + jnp.tanh(0.7978845608028654 * (x + 0.044715 * x * x * x)))


def _layer_norm(x, w, b):
    mu = jnp.mean(x, axis=-1, keepdims=True)
    xc = x - mu
    var = jnp.mean(xc * xc, axis=-1, keepdims=True)
    return xc * lax.rsqrt(var + LN_EPS) * w + b


def _head_mean_sq(o, bd_ones):
    sq = o * o
    hi = sq.astype(BF16)
    lo = (sq - hi.astype(F32)).astype(BF16)
    ones = bd_ones.astype(BF16)
    return (_dot(hi, ones) + _dot(lo, ones)) * (1.0 / HEAD_DIM)


def _rope_kernel(pos_ref, posr_ref, inv_ref, sign_ref, invc_ref, signc_ref, cos_ref, sin_ref, cost_ref, sint_ref):
    ang = pos_ref[...] * inv_ref[...]
    cos_ref[...] = jnp.cos(ang)
    sin_ref[...] = jnp.sin(ang) * sign_ref[...]
    ang_t = invc_ref[...] * posr_ref[...]
    cost_ref[...] = jnp.cos(ang_t)
    sint_ref[...] = jnp.sin(ang_t) * signc_ref[...]


def rope_tables(positions):
    n = positions.size
    tile = min(n, 2048)
    posf = positions.reshape(n).astype(F32)
    pos = jnp.broadcast_to(posf[:, None], (n, HEAD_DIM))
    inv = ROPE_THETA ** (-jnp.arange(0, HEAD_DIM, 2, dtype=F32) / HEAD_DIM)
    inv = jnp.concatenate([inv, inv])
    sign = jnp.concatenate([-jnp.ones((HEAD_DIM // 2,), F32), jnp.ones((HEAD_DIM // 2,), F32)])
    row = pl.BlockSpec((tile, HEAD_DIM), lambda i: (i, 0))
    rowt = pl.BlockSpec((HEAD_DIM, tile), lambda i: (0, i))
    const = pl.BlockSpec((1, HEAD_DIM), lambda i: (0, 0))
    constc = pl.BlockSpec((HEAD_DIM, 1), lambda i: (0, 0))
    return pl.pallas_call(
        _rope_kernel, grid=(n // tile,),
        in_specs=[row, pl.BlockSpec((1, tile), lambda i: (0, i)), const, const, constc, constc],
        out_specs=[row, row, rowt, rowt],
        out_shape=[jax.ShapeDtypeStruct((n, HEAD_DIM), F32)] * 2 + [jax.ShapeDtypeStruct((HEAD_DIM, n), F32)] * 2,
        compiler_params=_params("parallel"), name="rope_tables",
    )(pos, posf.reshape(1, n), inv.reshape(1, HEAD_DIM), sign.reshape(1, HEAD_DIM),
      inv.reshape(HEAD_DIM, 1), sign.reshape(HEAD_DIM, 1))


def _in_proj_kernel(x_ref, w_ref, h_ref):
    h_ref[...] = _dot(x_ref[...].astype(BF16), w_ref[...])


def in_proj(x2d, w_bf16):
    n, d = x2d.shape
    width = w_bf16.shape[1]
    tile = min(n, 512)
    return pl.pallas_call(
        _in_proj_kernel, grid=(n // tile,),
        in_specs=[pl.BlockSpec((tile, d), lambda i: (i, 0)), pl.BlockSpec((d, width), lambda i: (0, 0))],
        out_specs=pl.BlockSpec((tile, width), lambda i: (i, 0)),
        out_shape=jax.ShapeDtypeStruct((n, width), F32),
        compiler_params=_params("parallel"), name="in_proj")(x2d, w_bf16)


HG_LEVELS = (64, 32, 16, 8, 4, 2)
HG_BATCH = 4


def _hgrn_constants():
    c = HG_CHUNK
    t = np.arange(c)
    u = t[None, :]
    rows = [u <= t[:, None], u > t[:, None]]
    masks = [np.eye(c, dtype=bool)]
    for m in HG_LEVELS:
        ref = ((t // m) * m + m // 2 - 1)[:, None]
        second = (t % m >= m // 2)[:, None]
        rows.append(((u > ref) & (u <= t[:, None]) & second) | ((u > t[:, None]) & (u <= ref) & ~second))
        masks.append((t[:, None] // m == t[None, :] // m) & second & (t[None, :] % m < m // 2))
    pmat = np.concatenate(rows, axis=0).astype(np.float32)
    masks = np.stack([np.tile(mk, (1, 4)) for mk in masks]).astype(np.float32)
    return pmat, masks


def _hgrn_kernel(q_ref, f_ref, i_ref, g_ref, lb_ref, nw_ref, pmat_ref, masks_ref, bd_ref, hm_ref,
                 o_ref, state_ref):
    c = HG_CHUNK

    @pl.when(pl.program_id(1) == 0)
    def _():
        state_ref[...] = jnp.zeros_like(state_ref)

    lb = lb_ref[...]
    bd = bd_ref[...]
    hm = hm_ref[...]
    pmat = pmat_ref[...]
    a = jnp.log(lb)
    log1m = jnp.log(1.0 - lb)
    for bi in range(q_ref.shape[0]):
        fr = f_ref[bi]
        hq = q_ref[bi]
        qf = hq * _sigmoid(hq)
        log_sig = jnp.minimum(fr, 0.0) - jnp.log(1.0 + jnp.exp(-jnp.abs(fr)))
        cc = log1m + log_sig
        log_f = jnp.maximum(a, cc) + jnp.log(1.0 + jnp.exp(-jnp.abs(a - cc)))
        kk = (1.0 - lb) * _sigmoid(-fr)
        v = i_ref[bi]
        vb = v.astype(BF16)

        hi = log_f.astype(BF16)
        lo = (log_f - hi.astype(F32)).astype(BF16)
        sums = jnp.minimum(_dot(pmat, hi) + _dot(pmat, lo), 0.0)
        e_all = jnp.exp(sums)
        e_b = e_all[0:c]
        e_rest = e_all[c:2 * c]

        def stacked(x):
            return jnp.concatenate([x * hm[h:h + 1] for h in range(4)], axis=0).astype(BF16)

        att = masks_ref[0] * _dot_nt(qf.astype(BF16), stacked(kk))
        for li in range(len(HG_LEVELS)):
            e_l = e_all[(2 + li) * c:(3 + li) * c]
            att = att + masks_ref[li + 1] * _dot_nt((qf * e_l).astype(BF16), stacked(kk * e_l))
        o = _dot(att.astype(BF16), stacked(v))

        st = state_ref[bi]
        o = o + _dot_nt((qf * e_b).astype(BF16), st.astype(BF16))
        state_ref[bi] = st * e_b[c - 1:c] + bd * _dot_tn(vb, (kk * e_rest).astype(BF16))

        ms = _head_mean_sq(o, bd)
        y = o * lax.rsqrt(ms + RMS_EPS) * nw_ref[...] * _sigmoid(g_ref[bi])
        o_ref[bi] = y.astype(o_ref.dtype)


def hgrn2(h3, lb, norm_w):
    batch, seq, _ = h3.shape
    w = lb.shape[-1]
    c = HG_CHUNK
    nb = math.gcd(batch, HG_BATCH)
    pmat, masks = _hgrn_constants()
    lane_head = np.arange(w) // HEAD_DIM
    bd = (lane_head[:, None] == lane_head[None, :]).astype(np.float32)
    hm = (np.arange(4)[:, None] == lane_head[None, :]).astype(np.float32)

    def col(j):
        return pl.BlockSpec((nb, c, w), lambda b, i, j=j: (b, i, j))

    def const(shape):
        return pl.BlockSpec(shape, lambda b, i: (0,) * len(shape))

    return pl.pallas_call(
        _hgrn_kernel, grid=(batch // nb, seq // c),
        in_specs=[col(0), col(1), col(2), col(3), const((1, w)), const((1, w)),
                  const(pmat.shape), const(masks.shape), const(bd.shape), const(hm.shape)],
        out_specs=pl.BlockSpec((nb, c, w), lambda b, i: (b, i, 0)),
        out_shape=jax.ShapeDtypeStruct((batch, seq, w), BF16),
        scratch_shapes=[pltpu.VMEM((nb, w, w), F32)],
        compiler_params=_params("parallel", "arbitrary"), name="hgrn2",
    )(h3, h3, h3, h3, lb.reshape(1, w), norm_w.reshape(1, w), jnp.asarray(pmat, BF16), jnp.asarray(masks),
      jnp.asarray(bd), jnp.asarray(hm))


def _gmlp_kernel(u_ref, v_ref, lnw_ref, lnb_ref, ws_ref, bias_ref, nw_ref, bd_ref, hm_ref, o_ref):
    c = GM_CHUNK
    u = _gelu(u_ref[...])
    v = _layer_norm(_gelu(v_ref[...]), lnw_ref[...], lnb_ref[...])
    vb = v.astype(BF16)
    hm = hm_ref[...]
    causal = lax.broadcasted_iota(jnp.int32, (c, c), 0) >= lax.broadcasted_iota(jnp.int32, (c, c), 1)
    sv = bias_ref[...]
    for g in range(ws_ref.shape[0]):
        wg = jnp.where(causal, ws_ref[g], 0.0).astype(BF16)
        sv = sv + _dot(wg, vb) * hm[g:g + 1]
    y = u * sv
    ms = _head_mean_sq(y, bd_ref[...])
    o_ref[...] = (y * lax.rsqrt(ms + RMS_EPS) * nw_ref[...]).astype(o_ref.dtype)


def gmlp(h, ln_w, ln_b, w_s, b_s, norm_w, u_col, v_col):
    n = h.shape[0]
    groups, c, _ = w_s.shape
    w = groups * HEAD_DIM
    lane_head = np.arange(w) // HEAD_DIM
    bd = (lane_head[:, None] == lane_head[None, :]).astype(np.float32)
    hm = (np.arange(groups)[:, None] == lane_head[None, :]).astype(np.float32)
    bias = jnp.repeat(b_s.T, HEAD_DIM, axis=1)

    def const(shape):
        return pl.BlockSpec(shape, lambda i: (0,) * len(shape))

    return pl.pallas_call(
        _gmlp_kernel, grid=(n // c,),
        in_specs=[pl.BlockSpec((c, w), lambda i: (i, u_col)), pl.BlockSpec((c, w), lambda i: (i, v_col)),
                  const((1, w)), const((1, w)), const(w_s.shape), const((c, w)), const((1, w)),
                  const(bd.shape), const(hm.shape)],
        out_specs=pl.BlockSpec((c, w), lambda i: (i, 0)),
        out_shape=jax.ShapeDtypeStruct((n, w), BF16),
        compiler_params=_params("parallel"), name="gmlp",
    )(h, h, ln_w.reshape(1, w), ln_b.reshape(1, w), w_s, bias, norm_w.reshape(1, w),
      jnp.asarray(bd), jnp.asarray(hm))


def _compress_kernel(u_ref, wtop_ref, wbot_ref, pe_ref, w2_ref, o_ref):
    u = u_ref[...].astype(BF16)
    wtop = wtop_ref[...]
    wbot = wbot_ref[...]
    pe = pe_ref[...].astype(BF16)
    const = _dot(pe[0:1], wtop) + _dot(pe[1:2], wbot)
    p = _dot(u, wtop)
    q = _dot(u, wbot)
    q_next = jnp.concatenate([q[1:], jnp.zeros_like(q[0:1])], axis=0)
    hid = _gelu(p + q_next + const)
    o_ref[...] = _dot(hid.astype(BF16), w2_ref[...]).astype(o_ref.dtype)


def compress(kv, pe, w1, w2, batch, seq):
    g = NSA_KV_GROUPS
    half = CMP_STRIDE
    units = seq // half
    gw = g * HEAD_DIM
    u = kv.reshape(batch * units, half * gw)
    eye = jnp.eye(g, dtype=F32)
    w1r = w1.reshape(2, half, HEAD_DIM, CMP_HIDDEN)
    wbd = jnp.einsum('hjdn,gk->hjgdkn', w1r, eye).reshape(2, half * gw, g * CMP_HIDDEN).astype(BF16)
    w2bd = jnp.einsum('nd,gk->gnkd', w2, eye).reshape(g * CMP_HIDDEN, gw).astype(BF16)
    pe2 = jnp.broadcast_to(pe.reshape(2, half, 1, HEAD_DIM), (2, half, g, HEAD_DIM)).reshape(2, half * gw)

    def const(shape):
        return pl.BlockSpec(shape, lambda b: (0,) * len(shape))

    out = pl.pallas_call(
        _compress_kernel, grid=(batch,),
        in_specs=[pl.BlockSpec((units, half * gw), lambda b: (b, 0)),
                  const(wbd.shape[1:]), const(wbd.shape[1:]), const(pe2.shape), const(w2bd.shape)],
        out_specs=pl.BlockSpec((units, gw), lambda b: (b, 0)),
        out_shape=jax.ShapeDtypeStruct((batch * units, gw), BF16),
        compiler_params=_params("parallel"), name="nsa_compress",
    )(u, wbd[0], wbd[1], pe2, w2bd)
    return out.reshape(batch, units, gw)


def _rot_half_pairs(x):
    lane = lax.broadcasted_iota(jnp.int32, x.shape, 1)
    fwd = pltpu.roll(x, 32, axis=1)
    bwd = pltpu.roll(x, 96, axis=1)
    return jnp.where((lane % HEAD_DIM) < HEAD_DIM // 2, bwd, fwd)


def _kprep_kernel(ks_ref, vs_ref, kw_ref, vw_ref, cos_ref, sin_ref, ksa_ref, vso_ref, kwo_ref, vwo_ref):
    t = ks_ref.shape[0]
    cos = cos_ref[...]
    sin = sin_ref[...]
    cos2 = jnp.concatenate([cos, cos], axis=1)
    sin2 = jnp.concatenate([sin, sin], axis=1)
    ks = ks_ref[...]
    kw = kw_ref[...]
    ks_r = ks * cos2 + _rot_half_pairs(ks) * sin2
    kw_r = kw * cos2 + _rot_half_pairs(kw) * sin2
    pos = pl.program_id(1) * t + lax.broadcasted_iota(jnp.int32, (t, HEAD_DIM), 0)
    onehot = (pos // SEL_BLOCK == lax.broadcasted_iota(jnp.int32, (t, HEAD_DIM), 1)).astype(F32)
    vs_t = vs_ref[...].T
    vw_t = vw_ref[...].T
    tail = (lax.broadcasted_iota(jnp.int32, (V_ROWS - HEAD_DIM, t), 0) == 0).astype(F32)
    for g in range(NSA_KV_GROUPS):
        sl = slice(g * HEAD_DIM, (g + 1) * HEAD_DIM)
        ksa_ref[g] = jnp.concatenate([ks_r[:, sl], onehot], axis=1).astype(BF16)
        vso_ref[g] = jnp.concatenate([vs_t[sl], tail], axis=0).astype(BF16)
        kwo_ref[g] = kw_r[:, sl].astype(BF16)
        vwo_ref[g] = jnp.concatenate([vw_t[sl], tail], axis=0).astype(BF16)


def nsa_kprep(h, cosf, sinf, batch, seq, col0):
    g = NSA_KV_GROUPS
    t = min(seq, 512)
    nt = seq // t

    def col(j):
        return pl.BlockSpec((t, LANES), lambda b, i, j=j: (b * nt + i, col0 + j))

    tab = pl.BlockSpec((t, HEAD_DIM), lambda b, i: (b * nt + i, 0))

    def out(wd):
        return pl.BlockSpec((None, g, t, wd), lambda b, i: (b, 0, i, 0))

    out_t = pl.BlockSpec((None, g, V_ROWS, t), lambda b, i: (b, 0, 0, i))
    k_shape = jax.ShapeDtypeStruct((batch, g, seq, HEAD_DIM), BF16)
    v_shape = jax.ShapeDtypeStruct((batch, g, V_ROWS, seq), BF16)
    return pl.pallas_call(
        _kprep_kernel, grid=(batch, nt),
        in_specs=[col(0), col(1), col(2), col(3), tab, tab],
        out_specs=[out(2 * HEAD_DIM), out_t, out(HEAD_DIM), out_t],
        out_shape=[jax.ShapeDtypeStruct((batch, g, seq, 2 * HEAD_DIM), BF16), v_shape, k_shape, v_shape],
        compiler_params=_params("parallel", "parallel"), name="nsa_kprep",
    )(h, h, h, h, cosf, sinf)


def _nsa_kernel(hq_ref, gate_ref, cos_ref, sin_ref, kc_ref, vc_ref, ksa_ref, vs_ref, kw_ref, vw_ref,
                ovl_ref, nw_ref, o_ref, imp_ref, *, tq, tk, n_sb):
    g = pl.program_id(1)
    qi = pl.program_id(2)
    hpg = NSA_HPG
    rows = hpg * tq
    t0 = qi * tq
    scale = 1.0 / math.sqrt(HEAD_DIM)
    half = HEAD_DIM // 2

    hq_t = hq_ref[...].T
    cos = cos_ref[...]
    sin = sin_ref[...]
    q_raw, q_rot = [], []
    for h in range(hpg):
        qh = hq_t[h * HEAD_DIM:(h + 1) * HEAD_DIM]
        swapped = jnp.concatenate([qh[half:], qh[:half]], axis=0)
        q_raw.append(qh * scale)
        q_rot.append((qh * cos + swapped * sin) * scale)
    q_raw = jnp.concatenate(q_raw, axis=1).astype(BF16)
    q_rot = jnp.concatenate(q_rot, axis=1)

    tpos = t0 + lax.broadcasted_iota(jnp.int32, (1, tq), 1)
    tpos_r = jnp.concatenate([tpos] * hpg, axis=1)

    def flash_step(s, v_t, m, acc):
        m_new = jnp.maximum(m, jnp.max(s, axis=0, keepdims=True))
        alpha = jnp.exp(m - m_new)
        p = jnp.exp(s - m_new).astype(BF16)
        return m_new, alpha * acc + _dot(v_t, p)

    init = (jnp.full((1, rows), NEG_INF, F32), jnp.zeros((V_ROWS, rows), F32))

    wk = WINDOW + tq
    k0 = pl.multiple_of(jnp.maximum(t0 - WINDOW, 0), tq)
    kpos = k0 + lax.broadcasted_iota(jnp.int32, (wk, 1), 0)
    s = _dot(kw_ref[pl.ds(k0, wk), :], q_rot.astype(BF16))
    s = jnp.where((kpos <= tpos_r) & (kpos > tpos_r - WINDOW), s, NEG_INF)
    m_w, acc_w = flash_step(s, vw_ref[:, pl.ds(k0, wk)], *init)
    n_pad = jnp.maximum(WINDOW - 1 - tpos_r, 0).astype(F32)
    m_f = jnp.where(n_pad > 0.0, jnp.maximum(m_w, 0.0), m_w)
    a_w = jnp.exp(m_w - m_f)
    o_w = acc_w[0:HEAD_DIM] * (a_w / (acc_w[HEAD_DIM:HEAD_DIM + 1] * a_w + n_pad * jnp.exp(-m_f)))

    n_cmp = kc_ref.shape[0]
    s_c = _dot(kc_ref[...], q_raw)
    cmp_end = lax.broadcasted_iota(jnp.int32, (n_cmp, 1), 0) * CMP_STRIDE + (CMP_BLOCK - 1)
    mask_c = cmp_end <= tpos_r
    s_c = jnp.where(mask_c, s_c, NEG_INF)
    e_c = jnp.exp(s_c - jnp.max(s_c, axis=0, keepdims=True))
    p_c = jnp.where(mask_c, e_c * (1.0 / jnp.sum(e_c, axis=0, keepdims=True)), 0.0)
    o_c = _dot(vc_ref[...], p_c.astype(BF16))

    p_sum = p_c[:, 0:tq]
    for h in range(1, hpg):
        p_sum = p_sum + p_c[:, h * tq:(h + 1) * tq]
    imp = jnp.dot(ovl_ref[...], p_sum, precision=HIGHEST, preferred_element_type=F32)
    blk = lax.broadcasted_iota(jnp.int32, (n_sb, tq), 0)
    cur = (t0 + lax.broadcasted_iota(jnp.int32, (n_sb, tq), 1)) // SEL_BLOCK
    imp = jnp.where(blk > cur, IMP_FUTURE, imp)
    imp = jnp.where((blk == 0) | (blk == cur) | (blk == cur - 1), IMP_FORCE, imp)
    imp_ref[...] = imp

    def rank_body(j, rank):
        row = imp_ref[pl.ds(j, 1), :]
        ge = jnp.where(row >= imp, 1, 0)
        gt = jnp.where(row > imp, 1, 0)
        return rank + jnp.where(j < blk, ge, gt)

    rank = lax.fori_loop(0, n_sb, rank_body, jnp.zeros((n_sb, tq), jnp.int32), unroll=8)
    sel_bias = jnp.where(rank < min(N_SEL, n_sb), 0.0, NEG_INF)
    if n_sb < HEAD_DIM:
        sel_bias = jnp.concatenate([sel_bias, jnp.zeros((HEAD_DIM - n_sb, tq), F32)], axis=0)
    q_aug = jnp.concatenate([q_rot, jnp.concatenate([sel_bias] * hpg, axis=1)], axis=0).astype(BF16)

    def scores(kt):
        k0 = pl.multiple_of(kt * tk, tk)
        return _dot(ksa_ref[pl.ds(k0, tk), :], q_aug)

    def sel_body(kt, carry):
        return flash_step(scores(kt), vs_ref[:, pl.ds(pl.multiple_of(kt * tk, tk), tk)], *carry)

    n_full = t0 // tk
    m_s, acc_s = lax.fori_loop(0, n_full, sel_body, init)
    k0 = pl.multiple_of(n_full * tk, tk)
    kpos = k0 + lax.broadcasted_iota(jnp.int32, (tk, 1), 0)
    _, acc_s = flash_step(jnp.where(kpos <= tpos_r, scores(n_full), NEG_INF), vs_ref[:, pl.ds(k0, tk)], m_s, acc_s)
    o_s = acc_s[0:HEAD_DIM] * (1.0 / acc_s[HEAD_DIM:HEAD_DIM + 1])

    gates = _sigmoid(gate_ref[...].T)
    nw = nw_ref[...]
    outs = []
    for h in range(hpg):
        sl = slice(h * tq, (h + 1) * tq)
        gh = []
        for i in range(N_GATES):
            row = jnp.zeros((1, tq), F32)
            for gi in range(NSA_KV_GROUPS):
                r = (gi * hpg + h) * N_GATES + i
                row = jnp.where(g == gi, gates[r:r + 1], row)
            gh.append(row)
        o = gh[0] * o_c[:, sl] + gh[1] * o_s[:, sl] + gh[2] * o_w[:, sl]
        ms = jnp.mean(o * o, axis=0, keepdims=True)
        outs.append(o * lax.rsqrt(ms + RMS_EPS) * nw[:, h:h + 1])
    o_ref[...] = jnp.concatenate(outs, axis=0).T.astype(o_ref.dtype)


def nsa_attention(h, cos_t, sin_t, kc, vc_t, ksa, vs_t, kw, vw_t, norm_w, batch, seq, q_col0, gate_col):
    g, hpg = NSA_KV_GROUPS, NSA_HPG
    tq = min(seq, 256)
    nq = seq // tq
    n_sb = seq // SEL_BLOCK
    assert n_sb <= HEAD_DIM, "selection-block one-hot shares the 64 spare key lanes"
    n_cmp = kc.shape[2]
    units = np.arange(n_cmp)[:, None] + np.arange(CMP_BLOCK // CMP_STRIDE)[None, :]
    ovl = np.zeros((n_cmp, n_sb), np.float32)
    for c in range((seq - CMP_BLOCK) // CMP_STRIDE + 1):
        for u in units[c]:
            ovl[c, u // (SEL_BLOCK // CMP_STRIDE)] += 1.0
    ovl_t = jnp.asarray(ovl.T)

    def per_bg(shape):
        return pl.BlockSpec((None, None) + shape, lambda b, gi, qi: (b, gi, 0, 0))

    tab = pl.BlockSpec((HEAD_DIM, tq), lambda b, gi, qi: (0, b * nq + qi))
    tk = min(seq, 512)
    assert seq >= WINDOW + tq and seq % tk == 0 and tk % tq == 0
    kern = functools.partial(_nsa_kernel, tq=tq, tk=tk, n_sb=n_sb)
    return pl.pallas_call(
        kern, grid=(batch, g, nq),
        in_specs=[pl.BlockSpec((tq, hpg * HEAD_DIM), lambda b, gi, qi: (b * nq + qi, q_col0 + gi)),
                  pl.BlockSpec((tq, LANES), lambda b, gi, qi: (b * nq + qi, gate_col)),
                  tab, tab,
                  per_bg((n_cmp, HEAD_DIM)), per_bg((HEAD_DIM, n_cmp)),
                  per_bg((seq, 2 * HEAD_DIM)), per_bg((V_ROWS, seq)),
                  per_bg((seq, HEAD_DIM)), per_bg((V_ROWS, seq)),
                  pl.BlockSpec((n_sb, n_cmp), lambda b, gi, qi: (0, 0)),
                  pl.BlockSpec((None, HEAD_DIM, hpg), lambda b, gi, qi: (gi, 0, 0))],
        out_specs=pl.BlockSpec((tq, hpg * HEAD_DIM), lambda b, gi, qi: (b * nq + qi, gi)),
        out_shape=jax.ShapeDtypeStruct((batch * seq, g * hpg * HEAD_DIM), BF16),
        scratch_shapes=[pltpu.VMEM((n_sb, tq), F32)],
        compiler_params=_params("parallel", "parallel", "arbitrary"), name="nsa_attention",
    )(h, h, cos_t, sin_t, kc, vc_t, ksa, vs_t, kw, vw_t, ovl_t,
      norm_w.reshape(g, hpg, HEAD_DIM).transpose(0, 2, 1))


def _out_proj_kernel(x_ref, yhg_ref, ygm_ref, ynsa_ref, whg_ref, wgm_ref, wnsa_ref, lnw_ref, lnb_ref,
                     o_ref, ob_ref, *, alpha):
    mix = (_dot(yhg_ref[...], whg_ref[...]) + _dot(ygm_ref[...], wgm_ref[...])
           + _dot(ynsa_ref[...], wnsa_ref[...]))
    y = _layer_norm(alpha * x_ref[...] + mix, lnw_ref[...], lnb_ref[...])
    o_ref[...] = y
    ob_ref[...] = y.astype(BF16)


def out_proj_ln(x2d, y_hg, y_gm, y_nsa, w_out, ln_w, ln_b, alpha):
    n, d = x2d.shape
    w1, w2 = y_hg.shape[1], y_hg.shape[1] + y_gm.shape[1]
    whg = w_out[:w1].astype(BF16)
    wgm = w_out[w1:w2].astype(BF16)
    wnsa = w_out[w2:].astype(BF16)
    t = min(n, 512)

    def row(wd):
        return pl.BlockSpec((t, wd), lambda i: (i, 0))

    def const(shape):
        return pl.BlockSpec(shape, lambda i: (0,) * len(shape))

    kern = functools.partial(_out_proj_kernel, alpha=alpha)
    return pl.pallas_call(
        kern, grid=(n // t,),
        in_specs=[row(d), row(y_hg.shape[1]), row(y_gm.shape[1]), row(y_nsa.shape[1]),
                  const(whg.shape), const(wgm.shape), const(wnsa.shape), const((1, d)), const((1, d))],
        out_specs=[row(d), row(d)],
        out_shape=[jax.ShapeDtypeStruct((n, d), F32), jax.ShapeDtypeStruct((n, d), BF16)],
        compiler_params=_params("parallel"), name="out_proj_ln",
    )(x2d, y_hg, y_gm, y_nsa, whg, wgm, wnsa, ln_w.reshape(1, d), ln_b.reshape(1, d))


def _router_kernel(x_ref, w_ref, b_ref, e_ref, p_ref, r_ref, cnt_ref, carry_ref):
    t = x_ref.shape[0]

    @pl.when(pl.program_id(0) == 0)
    def _():
        carry_ref[...] = jnp.zeros_like(carry_ref)

    logits = jnp.dot(x_ref[...], w_ref[...], precision=HIGHEST, preferred_element_type=F32) + b_ref[...]
    lane = lax.broadcasted_iota(jnp.int32, logits.shape, 1)
    work = logits
    vals, idxs = [], []
    sel = jnp.zeros(logits.shape, F32)
    for _ in range(TOP_K):
        m = jnp.max(work, axis=-1, keepdims=True)
        idx = jnp.min(jnp.where(work == m, lane, LANES), axis=-1, keepdims=True)
        hit = lane == idx
        sel = jnp.where(hit, 1.0, sel)
        work = jnp.where(hit, -jnp.inf, work)
        vals.append(m)
        idxs.append(idx)
    exps = [jnp.exp(v - vals[0]) for v in vals]
    den = exps[0] + exps[1] + exps[2] + exps[3]
    strict = (lax.broadcasted_iota(jnp.int32, (t, t), 0) > lax.broadcasted_iota(jnp.int32, (t, t), 1))
    before = _dot(strict.astype(BF16), sel.astype(BF16)) + carry_ref[...]
    ranks = [jnp.sum(jnp.where(lane == idx, before, 0.0), axis=-1, keepdims=True) for idx in idxs]
    kcol = lax.broadcasted_iota(jnp.int32, (t, TOP_K), 1)
    e_out = jnp.zeros((t, TOP_K), jnp.int32)
    p_out = jnp.zeros((t, TOP_K), F32)
    r_out = jnp.zeros((t, TOP_K), jnp.int32)
    for k in range(TOP_K):
        e_out = jnp.where(kcol == k, idxs[k], e_out)
        p_out = jnp.where(kcol == k, exps[k] / den, p_out)
        r_out = jnp.where(kcol == k, ranks[k].astype(jnp.int32), r_out)
    e_ref[...] = e_out
    p_ref[...] = p_out
    r_ref[...] = r_out
    carry_ref[...] = carry_ref[...] + jnp.sum(sel, axis=0, keepdims=True)
    cnt_ref[...] = carry_ref[...].astype(jnp.int32)


def moe_router(x2d, router_w, router_b):
    n, d = x2d.shape
    e = router_w.shape[1]
    t = min(n, 512)
    w = jnp.zeros((d, LANES), F32).at[:, :e].set(router_w)
    b = jnp.full((1, LANES), NEG_INF, F32).at[0, :e].set(router_b)
    row4 = pl.BlockSpec((t, TOP_K), lambda i: (i, 0))
    top_e, top_p, rank, counts = pl.pallas_call(
        _router_kernel, grid=(n // t,),
        in_specs=[pl.BlockSpec((t, d), lambda i: (i, 0)), pl.BlockSpec((d, LANES), lambda i: (0, 0)),
                  pl.BlockSpec((1, LANES), lambda i: (0, 0))],
        out_specs=[row4, row4, row4, pl.BlockSpec((1, LANES), lambda i: (0, 0))],
        out_shape=[jax.ShapeDtypeStruct((n, TOP_K), jnp.int32), jax.ShapeDtypeStruct((n, TOP_K), F32),
                   jax.ShapeDtypeStruct((n, TOP_K), jnp.int32), jax.ShapeDtypeStruct((1, LANES), jnp.int32)],
        scratch_shapes=[pltpu.VMEM((1, LANES), F32)],
        compiler_params=_params("arbitrary"), name="moe_router",
    )(x2d, w, b)
    return top_e, top_p, rank, counts[0, :e]


def _expert_kernel(be_ref, x_ref, wu_ref, bu_ref, wd_ref, bd_ref, o_ref, wu_bf, wd_bf):
    i = pl.program_id(0)
    f = wd_ref.shape[0]

    @pl.when((i == 0) | (be_ref[i] != be_ref[jnp.maximum(i - 1, 0)]))
    def _():
        wu_bf[...] = wu_ref[...].astype(BF16)
        wd_bf[...] = wd_ref[...].astype(BF16)

    hcat = _dot(x_ref[...], wu_bf[...]) + bu_ref[...]
    glu = jnp.minimum(hcat[:, :f], SWIGLU_LIMIT)
    lin = jnp.clip(hcat[:, f:], -SWIGLU_LIMIT, SWIGLU_LIMIT)
    act = glu * _sigmoid(SWIGLU_ALPHA * glu) * (lin + 1.0)
    o_ref[...] = (_dot(act.astype(BF16), wd_bf[...]) + bd_ref[...]).astype(o_ref.dtype)


def moe_experts(xb, block_e, w_up, b_up, w_down, b_down):
    rows, d = xb.shape
    e, _, f2 = w_up.shape
    f = f2 // 2
    nb = rows // EXPERT_BLOCK
    grid_spec = pltpu.PrefetchScalarGridSpec(
        num_scalar_prefetch=1, grid=(nb,),
        in_specs=[pl.BlockSpec((EXPERT_BLOCK, d), lambda i, be: (i, 0)),
                  pl.BlockSpec((None, d, f2), lambda i, be: (be[i], 0, 0)),
                  pl.BlockSpec((None, 1, f2), lambda i, be: (be[i], 0, 0)),
                  pl.BlockSpec((None, f, d), lambda i, be: (be[i], 0, 0)),
                  pl.BlockSpec((None, 1, d), lambda i, be: (be[i], 0, 0))],
        out_specs=pl.BlockSpec((EXPERT_BLOCK, d), lambda i, be: (i, 0)),
        scratch_shapes=[pltpu.VMEM((d, f2), BF16), pltpu.VMEM((f, d), BF16)])
    return pl.pallas_call(
        _expert_kernel, grid_spec=grid_spec,
        out_shape=jax.ShapeDtypeStruct((rows, d), BF16),
        compiler_params=pltpu.CompilerParams(dimension_semantics=("arbitrary",), vmem_limit_bytes=EXPERT_VMEM_LIMIT),
        name="moe_experts",
    )(block_e, xb, w_up, b_up.reshape(e, 1, f2), w_down, b_down.reshape(e, 1, d))


def _combine_kernel(x_ref, y_ref, p_ref, lnw_ref, lnb_ref, o_ref, *, alpha):
    p = p_ref[...]
    acc = alpha * x_ref[...]
    for k in range(TOP_K):
        acc = acc + p[:, k:k + 1] * y_ref[k].astype(F32)
    o_ref[...] = _layer_norm(acc, lnw_ref[...], lnb_ref[...])


def combine_ln(x2d, y_gathered, top_p, ln_w, ln_b, alpha):
    n, d = x2d.shape
    t = min(n, 256)
    kern = functools.partial(_combine_kernel, alpha=alpha)
    return pl.pallas_call(
        kern, grid=(n // t,),
        in_specs=[pl.BlockSpec((t, d), lambda i: (i, 0)), pl.BlockSpec((TOP_K, t, d), lambda i: (0, i, 0)),
                  pl.BlockSpec((t, TOP_K), lambda i: (i, 0)),
                  pl.BlockSpec((1, d), lambda i: (0, 0)), pl.BlockSpec((1, d), lambda i: (0, 0))],
        out_specs=pl.BlockSpec((t, d), lambda i: (i, 0)),
        out_shape=jax.ShapeDtypeStruct((n, d), F32),
        compiler_params=_params("parallel"), name="moe_combine_ln",
    )(x2d, y_gathered, top_p, ln_w.reshape(1, d), ln_b.reshape(1, d))


def moe_ffn_ln(x_f32, x_bf16, router_w, router_b, w_up, b_up, w_down, b_down, ln_w, ln_b, alpha):
    n, d = x_f32.shape
    top_e, top_p, rank, counts = moe_router(x_f32, router_w, router_b)
    padded = (counts + EXPERT_BLOCK - 1) // EXPERT_BLOCK * EXPERT_BLOCK
    pad_end = jnp.cumsum(padded)
    pad_start = pad_end - padded
    n_assign = n * TOP_K
    n_blocks = -(-(n_assign + N_EXPERTS * (EXPERT_BLOCK - 1)) // EXPERT_BLOCK)
    start = jnp.cumsum(counts) - counts
    dest = pad_start[top_e] + rank
    block_first = jnp.arange(n_blocks, dtype=jnp.int32) * EXPERT_BLOCK
    block_e = jnp.clip(jnp.sum((pad_end[None, :] <= block_first[:, None]).astype(jnp.int32), axis=1),
                       0, N_EXPERTS - 1)
    order = jnp.argsort(top_e.reshape(-1), stable=True)
    tok_sorted = (order // TOP_K).astype(jnp.int32)
    row = jnp.arange(n_blocks * EXPERT_BLOCK, dtype=jnp.int32)
    row_e = jnp.repeat(block_e, EXPERT_BLOCK)
    off = row - pad_start[row_e]
    src = jnp.where(off < counts[row_e], tok_sorted[jnp.clip(start[row_e] + off, 0, n_assign - 1)], row % n)
    xb = x_bf16[src]
    yb = moe_experts(xb, block_e, w_up, b_up, w_down, b_down)
    return combine_ln(x_f32, yb[dest.T], top_p, ln_w, ln_b, alpha)


def kernel(x, positions, w_in, hg_lower_bounds, hg_norm_w, gm_ln_w, gm_ln_b, gm_spatial_w, gm_spatial_b, gm_norm_w, nsa_cmp_pe, nsa_cmp_w1, nsa_cmp_w2, nsa_norm_w, w_out, ln1_w, ln1_b, router_w, router_b, exp_w_up, exp_b_up, exp_w_down, exp_b_down, ln2_w, ln2_b):
    batch, seq, d = x.shape
    depth = w_in.shape[0]
    n = batch * seq
    alpha = (2 * depth) ** 0.25
    hg_w = hg_norm_w.shape[1]
    gm_w = gm_norm_w.shape[1]
    nsa_w = nsa_norm_w.shape[1]
    kv_w = NSA_KV_GROUPS * HEAD_DIM
    in_width = w_in.shape[2]
    off_gm = 4 * hg_w
    off_q = off_gm + 2 * gm_w
    off_kv = off_q + nsa_w
    off_gate = off_kv + 6 * kv_w
    width_pad = -(-in_width // LANES) * LANES

    cosf, sinf, cos_t, sin_t = rope_tables(positions)
    lb_all = jnp.cumsum(jax.nn.softmax(hg_lower_bounds.astype(F32), axis=0), axis=0)
    lb_all = lb_all - lb_all[0:1]

    x2d = x.reshape(n, d)
    for l in range(depth):
        w_l = jnp.pad(w_in[l], ((0, 0), (0, width_pad - in_width))).astype(BF16)
        h = in_proj(x2d, w_l)
        h3 = h.reshape(batch, seq, width_pad)
        y_hg = hgrn2(h3, lb_all[l], hg_norm_w[l]).reshape(n, hg_w)
        y_gm = gmlp(h, gm_ln_w[l], gm_ln_b[l], gm_spatial_w[l], gm_spatial_b[l], gm_norm_w[l],
                    off_gm // gm_w, off_gm // gm_w + 1)
        kc = compress(h3[:, :, off_kv:off_kv + kv_w], nsa_cmp_pe[l, 0], nsa_cmp_w1[l, 0], nsa_cmp_w2[l, 0], batch, seq)
        vc = compress(h3[:, :, off_kv + kv_w:off_kv + 2 * kv_w], nsa_cmp_pe[l, 1], nsa_cmp_w1[l, 1],
                      nsa_cmp_w2[l, 1], batch, seq)
        n_cmp = kc.shape[1]
        kc = kc.reshape(batch, n_cmp, NSA_KV_GROUPS, HEAD_DIM).transpose(0, 2, 1, 3)
        vc_t = vc.reshape(batch, n_cmp, NSA_KV_GROUPS, HEAD_DIM).transpose(0, 2, 3, 1)
        ksa, vs_t, kw, vw_t = nsa_kprep(h, cosf, sinf, batch, seq, (off_kv + 2 * kv_w) // LANES)
        y_nsa = nsa_attention(h, cos_t, sin_t, kc, vc_t, ksa, vs_t, kw, vw_t, nsa_norm_w[l], batch, seq,
                              off_q // (NSA_HPG * HEAD_DIM), off_gate // LANES)
        x1, x1b = out_proj_ln(x2d, y_hg, y_gm, y_nsa, w_out[l], ln1_w[l], ln1_b[l], alpha)
        x2d = moe_ffn_ln(x1, x1b, router_w[l], router_b[l], exp_w_up[l], exp_b_up[l], exp_w_down[l],
                         exp_b_down[l], ln2_w[l], ln2_b[l], alpha)
    return x2d.reshape(batch, seq, d)
```

```python
import functools
import math

import numpy as np
import jax
import jax.numpy as jnp
from jax import lax
from jax.experimental import pallas as pl
from jax.experimental.pallas import tpu as pltpu

F32 = jnp.float32
BF16 = jnp.bfloat16
HIGHEST = lax.Precision.HIGHEST

HEAD_DIM = 64
LANES = 128
VMEM_LIMIT = 48 * 1024 * 1024
EXPERT_VMEM_LIMIT = 56 * 1024 * 1024

HG_CHUNK = 64
GM_CHUNK = 128
NSA_KV_GROUPS = 2
NSA_HPG = 4
CMP_BLOCK = 32
CMP_STRIDE = 16
CMP_HIDDEN = 128
SEL_BLOCK = 64
N_SEL = 16
WINDOW = 512
N_GATES = 3
IMP_FORCE = 1e9
IMP_FUTURE = -1e9
NEG_INF = -1e30
N_EXPERTS = 32
TOP_K = 4
SWIGLU_ALPHA = 1.702
SWIGLU_LIMIT = 7.0
EXPERT_BLOCK = 512
ROPE_THETA = 10000.0
LN_EPS = 1e-5
RMS_EPS = 1e-6
V_ROWS = HEAD_DIM + 16


def _params(*sem):
    return pltpu.CompilerParams(dimension_semantics=sem, vmem_limit_bytes=VMEM_LIMIT)


def _dot(a, b):
    return jnp.dot(a, b, preferred_element_type=F32)


def _dot_nt(a, b, precision=None):
    return lax.dot_general(a, b, (((1,), (1,)), ((), ())), precision=precision,
                           preferred_element_type=F32)


def _dot_tn(a, b):
    return lax.dot_general(a, b, (((0,), (0,)), ((), ())), preferred_element_type=F32)


def _sigmoid(x):
    return 1.0 / (1.0 + jnp.exp(-x))


def _gelu(x):
    return 0.5 * x * (1.0 + jnp.tanh(0.7978845608028654 * (x + 0.044715 * x * x * x)))


def _layer_norm(x, w, b):
    mu = jnp.mean(x, axis=-1, keepdims=True)
    xc = x - mu
    var = jnp.mean(xc * xc, axis=-1, keepdims=True)
    return xc * lax.rsqrt(var + LN_EPS) * w + b


def _head_mean_sq(o, bd_ones):
    sq = o * o
    hi = sq.astype(BF16)
    lo = (sq - hi.astype(F32)).astype(BF16)
    ones = bd_ones.astype(BF16)
    return (_dot(hi, ones) + _dot(lo, ones)) * (1.0 / HEAD_DIM)


def _rope_kernel(pos_ref, posr_ref, inv_ref, sign_ref, invc_ref, signc_ref, cos_ref, sin_ref, cost_ref, sint_ref):
    ang = pos_ref[...] * inv_ref[...]
    cos_ref[...] = jnp.cos(ang)
    sin_ref[...] = jnp.sin(ang) * sign_ref[...]
    ang_t = invc_ref[...] * posr_ref[...]
    cost_ref[...] = jnp.cos(ang_t)
    sint_ref[...] = jnp.sin(ang_t) * signc_ref[...]


def rope_tables(positions):
    n = positions.size
    tile = min(n, 2048)
    posf = positions.reshape(n).astype(F32)
    pos = jnp.broadcast_to(posf[:, None], (n, HEAD_DIM))
    inv = ROPE_THETA ** (-jnp.arange(0, HEAD_DIM, 2, dtype=F32) / HEAD_DIM)
    inv = jnp.concatenate([inv, inv])
    sign = jnp.concatenate([-jnp.ones((HEAD_DIM // 2,), F32), jnp.ones((HEAD_DIM // 2,), F32)])
    row = pl.BlockSpec((tile, HEAD_DIM), lambda i: (i, 0))
    rowt = pl.BlockSpec((HEAD_DIM, tile), lambda i: (0, i))
    const = pl.BlockSpec((1, HEAD_DIM), lambda i: (0, 0))
    constc = pl.BlockSpec((HEAD_DIM, 1), lambda i: (0, 0))
    return pl.pallas_call(
        _rope_kernel, grid=(n // tile,),
        in_specs=[row, pl.BlockSpec((1, tile), lambda i: (0, i)), const, const, constc, constc],
        out_specs=[row, row, rowt, rowt],
        out_shape=[jax.ShapeDtypeStruct((n, HEAD_DIM), F32)] * 2 + [jax.ShapeDtypeStruct((HEAD_DIM, n), F32)] * 2,
        compiler_params=_params("parallel"), name="rope_tables",
    )(pos, posf.reshape(1, n), inv.reshape(1, HEAD_DIM), sign.reshape(1, HEAD_DIM),
      inv.reshape(HEAD_DIM, 1), sign.reshape(HEAD_DIM, 1))


def _in_proj_kernel(x_ref, w_ref, h_ref):
    h_ref[...] = _dot(x_ref[...].astype(BF16), w_ref[...])


def in_proj(x2d, w_bf16):
    n, d = x2d.shape
    width = w_bf16.shape[1]
    tile = min(n, 512)
    return pl.pallas_call(
        _in_proj_kernel, grid=(n // tile,),
        in_specs=[pl.BlockSpec((tile, d), lambda i: (i, 0)), pl.BlockSpec((d, width), lambda i: (0, 0))],
        out_specs=pl.BlockSpec((tile, width), lambda i: (i, 0)),
        out_shape=jax.ShapeDtypeStruct((n, width), F32),
        compiler_params=_params("parallel"), name="in_proj")(x2d, w_bf16)


HG_LEVELS = (64, 32, 16, 8, 4, 2)
HG_BATCH = 4


def _hgrn_constants():
    c = HG_CHUNK
    t = np.arange(c)
    u = t[None, :]
    rows = [u <= t[:, None], u > t[:, None]]
    masks = [np.eye(c, dtype=bool)]
    for m in HG_LEVELS:
        ref = ((t // m) * m + m // 2 - 1)[:, None]
        second = (t % m >= m // 2)[:, None]
        rows.append(((u > ref) & (u <= t[:, None]) & second) | ((u > t[:, None]) & (u <= ref) & ~second))
        masks.append((t[:, None] // m == t[None, :] // m) & second & (t[None, :] % m < m // 2))
    pmat = np.concatenate(rows, axis=0).astype(np.float32)
    masks = np.stack([np.tile(mk, (1, 4)) for mk in masks]).astype(np.float32)
    return pmat, masks


def _hgrn_kernel(q_ref, f_ref, i_ref, g_ref, lb_ref, nw_ref, pmat_ref, masks_ref, bd_ref, hm_ref,
                 o_ref, state_ref):
    c = HG_CHUNK

    @pl.when(pl.program_id(1) == 0)
    def _():
        state_ref[...] = jnp.zeros_like(state_ref)

    lb = lb_ref[...]
    bd = bd_ref[...]
    hm = hm_ref[...]
    pmat = pmat_ref[...]
    a = jnp.log(lb)
    log1m = jnp.log(1.0 - lb)
    for bi in range(q_ref.shape[0]):
        fr = f_ref[bi]
        hq = q_ref[bi]
        qf = hq * _sigmoid(hq)
        log_sig = jnp.minimum(fr, 0.0) - jnp.log(1.0 + jnp.exp(-jnp.abs(fr)))
        cc = log1m + log_sig
        log_f = jnp.maximum(a, cc) + jnp.log(1.0 + jnp.exp(-jnp.abs(a - cc)))
        kk = (1.0 - lb) * _sigmoid(-fr)
        v = i_ref[bi]
        vb = v.astype(BF16)

        hi = log_f.astype(BF16)
        lo = (log_f - hi.astype(F32)).astype(BF16)
        sums = jnp.minimum(_dot(pmat, hi) + _dot(pmat, lo), 0.0)
        e_all = jnp.exp(sums)
        e_b = e_all[0:c]
        e_rest = e_all[c:2 * c]

        def stacked(x):
            return jnp.concatenate([x * hm[h:h + 1] for h in range(4)], axis=0).astype(BF16)

        att = masks_ref[0] * _dot_nt(qf.astype(BF16), stacked(kk))
        for li in range(len(HG_LEVELS)):
            e_l = e_all[(2 + li) * c:(3 + li) * c]
            att = att + masks_ref[li + 1] * _dot_nt((qf * e_l).astype(BF16), stacked(kk * e_l))
        o = _dot(att.astype(BF16), stacked(v))

        st = state_ref[bi]
        o = o + _dot_nt((qf * e_b).astype(BF16), st.astype(BF16))
        state_ref[bi] = st * e_b[c - 1:c] + bd * _dot_tn(vb, (kk * e_rest).astype(BF16))

        ms = _head_mean_sq(o, bd)
        y = o * lax.rsqrt(ms + RMS_EPS) * nw_ref[...] * _sigmoid(g_ref[bi])
        o_ref[bi] = y.astype(o_ref.dtype)


def hgrn2(h3, lb, norm_w):
    batch, seq, _ = h3.shape
    w = lb.shape[-1]
    c = HG_CHUNK
    nb = math.gcd(batch, HG_BATCH)
    pmat, masks = _hgrn_constants()
    lane_head = np.arange(w) // HEAD_DIM
    bd = (lane_head[:, None] == lane_head[None, :]).astype(np.float32)
    hm = (np.arange(4)[:, None] == lane_head[None, :]).astype(np.float32)

    def col(j):
        return pl.BlockSpec((nb, c, w), lambda b, i, j=j: (b, i, j))

    def const(shape):
        return pl.BlockSpec(shape, lambda b, i: (0,) * len(shape))

    return pl.pallas_call(
        _hgrn_kernel, grid=(batch // nb, seq // c),
        in_specs=[col(0), col(1), col(2), col(3), const((1, w)), const((1, w)),
                  const(pmat.shape), const(masks.shape), const(bd.shape), const(hm.shape)],
        out_specs=pl.BlockSpec((nb, c, w), lambda b, i: (b, i, 0)),
        out_shape=jax.ShapeDtypeStruct((batch, seq, w), BF16),
        scratch_shapes=[pltpu.VMEM((nb, w, w), F32)],
        compiler_params=_params("parallel", "arbitrary"), name="hgrn2",
    )(h3, h3, h3, h3, lb.reshape(1, w), norm_w.reshape(1, w), jnp.asarray(pmat, BF16), jnp.asarray(masks),
      jnp.asarray(bd), jnp.asarray(hm))


def _gmlp_kernel(u_ref, v_ref, lnw_ref, lnb_ref, ws_ref, bias_ref, nw_ref, bd_ref, hm_ref, o_ref):
    c = GM_CHUNK
    u = _gelu(u_ref[...])
    v = _layer_norm(_gelu(v_ref[...]), lnw_ref[...], lnb_ref[...])
    vb = v.astype(BF16)
    hm = hm_ref[...]
    causal = lax.broadcasted_iota(jnp.int32, (c, c), 0) >= lax.broadcasted_iota(jnp.int32, (c, c), 1)
    sv = bias_ref[...]
    for g in range(ws_ref.shape[0]):
        wg = jnp.where(causal, ws_ref[g], 0.0).astype(BF16)
        sv = sv + _dot(wg, vb) * hm[g:g + 1]
    y = u * sv
    ms = _head_mean_sq(y, bd_ref[...])
    o_ref[...] = (y * lax.rsqrt(ms + RMS_EPS) * nw_ref[...]).astype(o_ref.dtype)


def gmlp(h, ln_w, ln_b, w_s, b_s, norm_w, u_col, v_col):
    n = h.shape[0]
    groups, c, _ = w_s.shape
    w = groups * HEAD_DIM
    lane_head = np.arange(w) // HEAD_DIM
    bd = (lane_head[:, None] == lane_head[None, :]).astype(np.float32)
    hm = (np.arange(groups)[:, None] == lane_head[None, :]).astype(np.float32)
    bias = jnp.repeat(b_s.T, HEAD_DIM, axis=1)

    def const(shape):
        return pl.BlockSpec(shape, lambda i: (0,) * len(shape))

    return pl.pallas_call(
        _gmlp_kernel, grid=(n // c,),
        in_specs=[pl.BlockSpec((c, w), lambda i: (i, u_col)), pl.BlockSpec((c, w), lambda i: (i, v_col)),
                  const((1, w)), const((1, w)), const(w_s.shape), const((c, w)), const((1, w)),
                  const(bd.shape), const(hm.shape)],
        out_specs=pl.BlockSpec((c, w), lambda i: (i, 0)),
        out_shape=jax.ShapeDtypeStruct((n, w), BF16),
        compiler_params=_params("parallel"), name="gmlp",
    )(h, h, ln_w.reshape(1, w), ln_b.reshape(1, w), w_s, bias, norm_w.reshape(1, w),
      jnp.asarray(bd), jnp.asarray(hm))


def _compress_kernel(u_ref, wtop_ref, wbot_ref, pe_ref, w2_ref, o_ref):
    u = u_ref[...].astype(BF16)
    wtop = wtop_ref[...]
    wbot = wbot_ref[...]
    pe = pe_ref[...].astype(BF16)
    const = _dot(pe[0:1], wtop) + _dot(pe[1:2], wbot)
    p = _dot(u, wtop)
    q = _dot(u, wbot)
    q_next = jnp.concatenate([q[1:], jnp.zeros_like(q[0:1])], axis=0)
    hid = _gelu(p + q_next + const)
    o_ref[...] = _dot(hid.astype(BF16), w2_ref[...]).astype(o_ref.dtype)


def compress(kv, pe, w1, w2, batch, seq):
    g = NSA_KV_GROUPS
    half = CMP_STRIDE
    units = seq // half
    gw = g * HEAD_DIM
    u = kv.reshape(batch * units, half * gw)
    eye = jnp.eye(g, dtype=F32)
    w1r = w1.reshape(2, half, HEAD_DIM, CMP_HIDDEN)
    wbd = jnp.einsum('hjdn,gk->hjgdkn', w1r, eye).reshape(2, half * gw, g * CMP_HIDDEN).astype(BF16)
    w2bd = jnp.einsum('nd,gk->gnkd', w2, eye).reshape(g * CMP_HIDDEN, gw).astype(BF16)
    pe2 = jnp.broadcast_to(pe.reshape(2, half, 1, HEAD_DIM), (2, half, g, HEAD_DIM)).reshape(2, half * gw)

    def const(shape):
        return pl.BlockSpec(shape, lambda b: (0,) * len(shape))

    out = pl.pallas_call(
        _compress_kernel, grid=(batch,),
        in_specs=[pl.BlockSpec((units, half * gw), lambda b: (b, 0)),
                  const(wbd.shape[1:]), const(wbd.shape[1:]), const(pe2.shape), const(w2bd.shape)],
        out_specs=pl.BlockSpec((units, gw), lambda b: (b, 0)),
        out_shape=jax.ShapeDtypeStruct((batch * units, gw), BF16),
        compiler_params=_params("parallel"), name="nsa_compress",
    )(u, wbd[0], wbd[1], pe2, w2bd)
    return out.reshape(batch, units, gw)


def _rot_half_pairs(x):
    lane = lax.broadcasted_iota(jnp.int32, x.shape, 1)
    fwd = pltpu.roll(x, 32, axis=1)
    bwd = pltpu.roll(x, 96, axis=1)
    return jnp.where((lane % HEAD_DIM) < HEAD_DIM // 2, bwd, fwd)


def _kprep_kernel(ks_ref, vs_ref, kw_ref, vw_ref, cos_ref, sin_ref, ksa_ref, vso_ref, kwo_ref, vwo_ref):
    t = ks_ref.shape[0]
    cos = cos_ref[...]
    sin = sin_ref[...]
    cos2 = jnp.concatenate([cos, cos], axis=1)
    sin2 = jnp.concatenate([sin, sin], axis=1)
    ks = ks_ref[...]
    kw = kw_ref[...]
    ks_r = ks * cos2 + _rot_half_pairs(ks) * sin2
    kw_r = kw * cos2 + _rot_half_pairs(kw) * sin2
    pos = pl.program_id(1) * t + lax.broadcasted_iota(jnp.int32, (t, HEAD_DIM), 0)
    onehot = (pos // SEL_BLOCK == lax.broadcasted_iota(jnp.int32, (t, HEAD_DIM), 1)).astype(F32)
    vs_t = vs_ref[...].T
    vw_t = vw_ref[...].T
    tail = (lax.broadcasted_iota(jnp.int32, (V_ROWS - HEAD_DIM, t), 0) == 0).astype(F32)
    for g in range(NSA_KV_GROUPS):
        sl = slice(g * HEAD_DIM, (g + 1) * HEAD_DIM)
        ksa_ref[g] = jnp.concatenate([ks_r[:, sl], onehot], axis=1).astype(BF16)
        vso_ref[g] = jnp.concatenate([vs_t[sl], tail], axis=0).astype(BF16)
        kwo_ref[g] = kw_r[:, sl].astype(BF16)
        vwo_ref[g] = jnp.concatenate([vw_t[sl], tail], axis=0).astype(BF16)


def nsa_kprep(h, cosf, sinf, batch, seq, col0):
    g = NSA_KV_GROUPS
    t = min(seq, 512)
    nt = seq // t

    def col(j):
        return pl.BlockSpec((t, LANES), lambda b, i, j=j: (b * nt + i, col0 + j))

    tab = pl.BlockSpec((t, HEAD_DIM), lambda b, i: (b * nt + i, 0))

    def out(wd):
        return pl.BlockSpec((None, g, t, wd), lambda b, i: (b, 0, i, 0))

    out_t = pl.BlockSpec((None, g, V_ROWS, t), lambda b, i: (b, 0, 0, i))
    k_shape = jax.ShapeDtypeStruct((batch, g, seq, HEAD_DIM), BF16)
    v_shape = jax.ShapeDtypeStruct((batch, g, V_ROWS, seq), BF16)
    return pl.pallas_call(
        _kprep_kernel, grid=(batch, nt),
        in_specs=[col(0), col(1), col(2), col(3), tab, tab],
        out_specs=[out(2 * HEAD_DIM), out_t, out(HEAD_DIM), out_t],
        out_shape=[jax.ShapeDtypeStruct((batch, g, seq, 2 * HEAD_DIM), BF16), v_shape, k_shape, v_shape],
        compiler_params=_params("parallel", "parallel"), name="nsa_kprep",
    )(h, h, h, h, cosf, sinf)


def _nsa_kernel(hq_ref, gate_ref, cos_ref, sin_ref, kc_ref, vc_ref, ksa_ref, vs_ref, kw_ref, vw_ref,
                ovl_ref, nw_ref, o_ref, imp_ref, *, tq, tk, n_sb):
    qi = pl.program_id(1)
    hpg = NSA_HPG
    groups = NSA_KV_GROUPS
    rows = hpg * tq
    t0 = qi * tq
    scale = 1.0 / math.sqrt(HEAD_DIM)
    half = HEAD_DIM // 2

    hq_t = hq_ref[...].T
    cos = cos_ref[...]
    sin = sin_ref[...]
    q_raw, q_rot = [], []
    for g in range(groups):
        raw_g, rot_g = [], []
        for h in range(hpg):
            r0 = (g * hpg + h) * HEAD_DIM
            qh = hq_t[r0:r0 + HEAD_DIM]
            swapped = jnp.concatenate([qh[half:], qh[:half]], axis=0)
            raw_g.append(qh * scale)
            rot_g.append((qh * cos + swapped * sin) * scale)
        q_raw.append(jnp.concatenate(raw_g, axis=1).astype(BF16))
        q_rot.append(jnp.concatenate(rot_g, axis=1))

    tpos = t0 + lax.broadcasted_iota(jnp.int32, (1, tq), 1)
    tpos_r = jnp.concatenate([tpos] * hpg, axis=1)

    def flash_step(s, v_t, m, acc):
        m_new = jnp.maximum(m, jnp.max(s, axis=0, keepdims=True))
        alpha = jnp.exp(m - m_new)
        p = jnp.exp(s - m_new).astype(BF16)
        return m_new, alpha * acc + _dot(v_t, p)

    init = (jnp.full((1, rows), NEG_INF, F32), jnp.zeros((V_ROWS, rows), F32))

    wk = WINDOW + tq
    kw0 = pl.multiple_of(jnp.maximum(t0 - WINDOW, 0), tq)
    kpos_w = kw0 + lax.broadcasted_iota(jnp.int32, (wk, 1), 0)
    mask_w = (kpos_w <= tpos_r) & (kpos_w > tpos_r - WINDOW)
    n_pad = jnp.maximum(WINDOW - 1 - tpos_r, 0).astype(F32)
    n_cmp = kc_ref.shape[1]
    cmp_end = lax.broadcasted_iota(jnp.int32, (n_cmp, 1), 0) * CMP_STRIDE + (CMP_BLOCK - 1)
    mask_c = cmp_end <= tpos_r
    blk = lax.broadcasted_iota(jnp.int32, (n_sb, tq), 0)
    cur = (t0 + lax.broadcasted_iota(jnp.int32, (n_sb, tq), 1)) // SEL_BLOCK
    o_w, o_c, imps = [], [], []
    for g in range(groups):
        s = _dot(kw_ref[g, pl.ds(kw0, wk), :], q_rot[g].astype(BF16))
        m_w, acc_w = flash_step(jnp.where(mask_w, s, NEG_INF), vw_ref[g, :, pl.ds(kw0, wk)], *init)
        m_f = jnp.where(n_pad > 0.0, jnp.maximum(m_w, 0.0), m_w)
        a_w = jnp.exp(m_w - m_f)
        o_w.append(acc_w[0:HEAD_DIM] * (a_w / (acc_w[HEAD_DIM:HEAD_DIM + 1] * a_w + n_pad * jnp.exp(-m_f))))

        s_c = jnp.where(mask_c, _dot(kc_ref[g], q_raw[g]), NEG_INF)
        e_c = jnp.exp(s_c - jnp.max(s_c, axis=0, keepdims=True))
        p_c = jnp.where(mask_c, e_c * (1.0 / jnp.sum(e_c, axis=0, keepdims=True)), 0.0)
        o_c.append(_dot(vc_ref[g], p_c.astype(BF16)))

        p_sum = p_c[:, 0:tq]
        for h in range(1, hpg):
            p_sum = p_sum + p_c[:, h * tq:(h + 1) * tq]
        imp = jnp.dot(ovl_ref[...], p_sum, precision=HIGHEST, preferred_element_type=F32)
        imp = jnp.where(blk > cur, IMP_FUTURE, imp)
        imp = jnp.where((blk == 0) | (blk == cur) | (blk == cur - 1), IMP_FORCE, imp)
        imp_ref[g] = imp
        imps.append(imp)

    def rank_body(j, ranks):
        out = []
        for g in range(groups):
            row = imp_ref[g, pl.ds(j, 1), :]
            ge = jnp.where(row >= imps[g], 1, 0)
            gt = jnp.where(row > imps[g], 1, 0)
            out.append(ranks[g] + jnp.where(j < blk, ge, gt))
        return tuple(out)

    ranks = lax.fori_loop(0, n_sb, rank_body, (jnp.zeros((n_sb, tq), jnp.int32),) * groups, unroll=8)
    q_aug = []
    for g in range(groups):
        sel_bias = jnp.where(ranks[g] < min(N_SEL, n_sb), 0.0, NEG_INF)
        if n_sb < HEAD_DIM:
            sel_bias = jnp.concatenate([sel_bias, jnp.zeros((HEAD_DIM - n_sb, tq), F32)], axis=0)
        q_aug.append(jnp.concatenate([q_rot[g], jnp.concatenate([sel_bias] * hpg, axis=1)], axis=0).astype(BF16))

    def scores(g, kt):
        k0 = pl.multiple_of(kt * tk, tk)
        return _dot(ksa_ref[g, pl.ds(k0, tk), :], q_aug[g])

    def sel_body(kt, carry):
        k0 = pl.multiple_of(kt * tk, tk)
        s = [scores(g, kt) for g in range(groups)]
        m_new = [jnp.maximum(carry[g][0], jnp.max(s[g], axis=0, keepdims=True)) for g in range(groups)]
        p = [jnp.exp(s[g] - m_new[g]).astype(BF16) for g in range(groups)]
        return tuple((m_new[g], jnp.exp(carry[g][0] - m_new[g]) * carry[g][1] + _dot(vs_ref[g, :, pl.ds(k0, tk)], p[g]))
                     for g in range(groups))

    n_full = t0 // tk
    carry = lax.fori_loop(0, n_full, sel_body, (init,) * groups)
    k0 = pl.multiple_of(n_full * tk, tk)
    kpos = k0 + lax.broadcasted_iota(jnp.int32, (tk, 1), 0)
    mask_s = kpos <= tpos_r

    gates = _sigmoid(gate_ref[...].T)
    nw = nw_ref[...]
    outs = []
    for g in range(groups):
        _, acc_s = flash_step(jnp.where(mask_s, scores(g, n_full), NEG_INF), vs_ref[g, :, pl.ds(k0, tk)], *carry[g])
        o_s = acc_s[0:HEAD_DIM] * (1.0 / acc_s[HEAD_DIM:HEAD_DIM + 1])
        for h in range(hpg):
            sl = slice(h * tq, (h + 1) * tq)
            r = (g * hpg + h) * N_GATES
            o = gates[r:r + 1] * o_c[g][:, sl] + gates[r + 1:r + 2] * o_s[:, sl] + gates[r + 2:r + 3] * o_w[g][:, sl]
            ms = jnp.mean(o * o, axis=0, keepdims=True)
            outs.append(o * lax.rsqrt(ms + RMS_EPS) * nw[:, g * hpg + h:g * hpg + h + 1])
    o_ref[...] = jnp.concatenate(outs, axis=0).T.astype(o_ref.dtype)


def nsa_attention(h, cos_t, sin_t, kc, vc_t, ksa, vs_t, kw, vw_t, norm_w, batch, seq, q_col0, gate_col):
    g, hpg = NSA_KV_GROUPS, NSA_HPG
    tq = min(seq, 256)
    nq = seq // tq
    n_sb = seq // SEL_BLOCK
    assert n_sb <= HEAD_DIM, "selection-block one-hot shares the 64 spare key lanes"
    n_cmp = kc.shape[2]
    units = np.arange(n_cmp)[:, None] + np.arange(CMP_BLOCK // CMP_STRIDE)[None, :]
    ovl = np.zeros((n_cmp, n_sb), np.float32)
    for c in range((seq - CMP_BLOCK) // CMP_STRIDE + 1):
        for u in units[c]:
            ovl[c, u // (SEL_BLOCK // CMP_STRIDE)] += 1.0
    ovl_t = jnp.asarray(ovl.T)

    def per_b(shape):
        return pl.BlockSpec((None, g) + shape, lambda b, qi: (b, 0, 0, 0))

    width = g * hpg * HEAD_DIM
    tab = pl.BlockSpec((HEAD_DIM, tq), lambda b, qi: (0, b * nq + qi))
    tk = min(seq, 512)
    assert seq >= WINDOW + tq and seq % tk == 0 and tk % tq == 0
    kern = functools.partial(_nsa_kernel, tq=tq, tk=tk, n_sb=n_sb)
    return pl.pallas_call(
        kern, grid=(batch, nq),
        in_specs=[pl.BlockSpec((tq, width), lambda b, qi: (b * nq + qi, q_col0)),
                  pl.BlockSpec((tq, LANES), lambda b, qi: (b * nq + qi, gate_col)),
                  tab, tab,
                  per_b((n_cmp, HEAD_DIM)), per_b((HEAD_DIM, n_cmp)),
                  per_b((seq, 2 * HEAD_DIM)), per_b((V_ROWS, seq)),
                  per_b((seq, HEAD_DIM)), per_b((V_ROWS, seq)),
                  pl.BlockSpec((n_sb, n_cmp), lambda b, qi: (0, 0)),
                  pl.BlockSpec((HEAD_DIM, g * hpg), lambda b, qi: (0, 0))],
        out_specs=pl.BlockSpec((tq, width), lambda b, qi: (b * nq + qi, 0)),
        out_shape=jax.ShapeDtypeStruct((batch * seq, width), BF16),
        scratch_shapes=[pltpu.VMEM((g, n_sb, tq), F32)],
        compiler_params=_params("parallel", "arbitrary"), name="nsa_attention",
    )(h, h, cos_t, sin_t, kc, vc_t, ksa, vs_t, kw, vw_t, ovl_t, norm_w.reshape(g * hpg, HEAD_DIM).T)


def _out_proj_kernel(x_ref, yhg_ref, ygm_ref, ynsa_ref, whg_ref, wgm_ref, wnsa_ref, lnw_ref, lnb_ref,
                     o_ref, ob_ref, *, alpha):
    mix = (_dot(yhg_ref[...], whg_ref[...]) + _dot(ygm_ref[...], wgm_ref[...])
           + _dot(ynsa_ref[...], wnsa_ref[...]))
    y = _layer_norm(alpha * x_ref[...] + mix, lnw_ref[...], lnb_ref[...])
    o_ref[...] = y
    ob_ref[...] = y.astype(BF16)


def out_proj_ln(x2d, y_hg, y_gm, y_nsa, w_out, ln_w, ln_b, alpha):
    n, d = x2d.shape
    w1, w2 = y_hg.shape[1], y_hg.shape[1] + y_gm.shape[1]
    whg = w_out[:w1].astype(BF16)
    wgm = w_out[w1:w2].astype(BF16)
    wnsa = w_out[w2:].astype(BF16)
    t = min(n, 512)

    def row(wd):
        return pl.BlockSpec((t, wd), lambda i: (i, 0))

    def const(shape):
        return pl.BlockSpec(shape, lambda i: (0,) * len(shape))

    kern = functools.partial(_out_proj_kernel, alpha=alpha)
    return pl.pallas_call(
        kern, grid=(n // t,),
        in_specs=[row(d), row(y_hg.shape[1]), row(y_gm.shape[1]), row(y_nsa.shape[1]),
                  const(whg.shape), const(wgm.shape), const(wnsa.shape), const((1, d)), const((1, d))],
        out_specs=[row(d), row(d)],
        out_shape=[jax.ShapeDtypeStruct((n, d), F32), jax.ShapeDtypeStruct((n, d), BF16)],
        compiler_params=_params("parallel"), name="out_proj_ln",
    )(x2d, y_hg, y_gm, y_nsa, whg, wgm, wnsa, ln_w.reshape(1, d), ln_b.reshape(1, d))


def _router_kernel(x_ref, w_ref, b_ref, e_ref, p_ref, r_ref, cnt_ref, carry_ref):
    t = x_ref.shape[0]

    @pl.when(pl.program_id(0) == 0)
    def _():
        carry_ref[...] = jnp.zeros_like(carry_ref)

    logits = jnp.dot(x_ref[...], w_ref[...], precision=HIGHEST, preferred_element_type=F32) + b_ref[...]
    lane = lax.broadcasted_iota(jnp.int32, logits.shape, 1)
    work = logits
    vals, idxs = [], []
    sel = jnp.zeros(logits.shape, F32)
    for _ in range(TOP_K):
        m = jnp.max(work, axis=-1, keepdims=True)
        idx = jnp.min(jnp.where(work == m, lane, LANES), axis=-1, keepdims=True)
        hit = lane == idx
        sel = jnp.where(hit, 1.0, sel)
        work = jnp.where(hit, -jnp.inf, work)
        vals.append(m)
        idxs.append(idx)
    exps = [jnp.exp(v - vals[0]) for v in vals]
    den = exps[0] + exps[1] + exps[2] + exps[3]
    strict = (lax.broadcasted_iota(jnp.int32, (t, t), 0) > lax.broadcasted_iota(jnp.int32, (t, t), 1))
    before = _dot(strict.astype(BF16), sel.astype(BF16)) + carry_ref[...]
    ranks = [jnp.sum(jnp.where(lane == idx, before, 0.0), axis=-1, keepdims=True) for idx in idxs]
    kcol = lax.broadcasted_iota(jnp.int32, (t, TOP_K), 1)
    e_out = jnp.zeros((t, TOP_K), jnp.int32)
    p_out = jnp.zeros((t, TOP_K), F32)
    r_out = jnp.zeros((t, TOP_K), jnp.int32)
    for k in range(TOP_K):
        e_out = jnp.where(kcol == k, idxs[k], e_out)
        p_out = jnp.where(kcol == k, exps[k] / den, p_out)
        r_out = jnp.where(kcol == k, ranks[k].astype(jnp.int32), r_out)
    e_ref[...] = e_out
    p_ref[...] = p_out
    r_ref[...] = r_out
    carry_ref[...] = carry_ref[...] + jnp.sum(sel, axis=0, keepdims=True)
    cnt_ref[...] = carry_ref[...].astype(jnp.int32)


def moe_router(x2d, router_w, router_b):
    n, d = x2d.shape
    e = router_w.shape[1]
    t = min(n, 512)
    w = jnp.zeros((d, LANES), F32).at[:, :e].set(router_w)
    b = jnp.full((1, LANES), NEG_INF, F32).at[0, :e].set(router_b)
    row4 = pl.BlockSpec((t, TOP_K), lambda i: (i, 0))
    top_e, top_p, rank, counts = pl.pallas_call(
        _router_kernel, grid=(n // t,),
        in_specs=[pl.BlockSpec((t, d), lambda i: (i, 0)), pl.BlockSpec((d, LANES), lambda i: (0, 0)),
                  pl.BlockSpec((1, LANES), lambda i: (0, 0))],
        out_specs=[row4, row4, row4, pl.BlockSpec((1, LANES), lambda i: (0, 0))],
        out_shape=[jax.ShapeDtypeStruct((n, TOP_K), jnp.int32), jax.ShapeDtypeStruct((n, TOP_K), F32),
                   jax.ShapeDtypeStruct((n, TOP_K), jnp.int32), jax.ShapeDtypeStruct((1, LANES), jnp.int32)],
        scratch_shapes=[pltpu.VMEM((1, LANES), F32)],
        compiler_params=_params("arbitrary"), name="moe_router",
    )(x2d, w, b)
    return top_e, top_p, rank, counts[0, :e]


def _expert_kernel(be_ref, x_ref, wu_ref, bu_ref, wd_ref, bd_ref, o_ref, wu_bf, wd_bf):
    i = pl.program_id(0)
    f = wd_ref.shape[0]
    n_used = be_ref[pl.num_programs(0)]

    @pl.when((i == 0) | (be_ref[i] != be_ref[jnp.maximum(i - 1, 0)]))
    def _():
        wu_bf[...] = wu_ref[...].astype(BF16)
        wd_bf[...] = wd_ref[...].astype(BF16)

    @pl.when(i < n_used)
    def _():
        hcat = _dot(x_ref[...], wu_bf[...]) + bu_ref[...]
        glu = jnp.minimum(hcat[:, :f], SWIGLU_LIMIT)
        lin = jnp.clip(hcat[:, f:], -SWIGLU_LIMIT, SWIGLU_LIMIT)
        act = glu * _sigmoid(SWIGLU_ALPHA * glu) * (lin + 1.0)
        o_ref[...] = (_dot(act.astype(BF16), wd_bf[...]) + bd_ref[...]).astype(o_ref.dtype)

    @pl.when(i >= n_used)
    def _():
        o_ref[...] = jnp.zeros_like(o_ref)


def moe_experts(xb, block_e, n_used, w_up, b_up, w_down, b_down, layer):
    rows, d = xb.shape
    _, e, _, f2 = w_up.shape
    f = f2 // 2
    nb = rows // EXPERT_BLOCK
    grid_spec = pltpu.PrefetchScalarGridSpec(
        num_scalar_prefetch=1, grid=(nb,),
        in_specs=[pl.BlockSpec((EXPERT_BLOCK, d), lambda i, be: (i, 0)),
                  pl.BlockSpec((None, None, d, f2), lambda i, be: (layer, be[i], 0, 0)),
                  pl.BlockSpec((None, None, 1, f2), lambda i, be: (layer, be[i], 0, 0)),
                  pl.BlockSpec((None, None, f, d), lambda i, be: (layer, be[i], 0, 0)),
                  pl.BlockSpec((None, None, 1, d), lambda i, be: (layer, be[i], 0, 0))],
        out_specs=pl.BlockSpec((EXPERT_BLOCK, d), lambda i, be: (i, 0)),
        scratch_shapes=[pltpu.VMEM((d, f2), BF16), pltpu.VMEM((f, d), BF16)])
    depth = w_up.shape[0]
    return pl.pallas_call(
        _expert_kernel, grid_spec=grid_spec,
        out_shape=jax.ShapeDtypeStruct((rows, d), BF16),
        compiler_params=pltpu.CompilerParams(dimension_semantics=("arbitrary",), vmem_limit_bytes=EXPERT_VMEM_LIMIT),
        name="moe_experts",
    )(jnp.concatenate([block_e, n_used.reshape(1)]), xb, w_up, b_up.reshape(depth, e, 1, f2), w_down,
      b_down.reshape(depth, e, 1, d))


def _combine_kernel(x_ref, y_ref, p_ref, lnw_ref, lnb_ref, o_ref, *, alpha):
    p = p_ref[...]
    acc = alpha * x_ref[...]
    for k in range(TOP_K):
        acc = acc + p[:, k:k + 1] * y_ref[k].astype(F32)
    o_ref[...] = _layer_norm(acc, lnw_ref[...], lnb_ref[...])


def combine_ln(x2d, y_gathered, top_p, ln_w, ln_b, alpha):
    n, d = x2d.shape
    t = min(n, 256)
    kern = functools.partial(_combine_kernel, alpha=alpha)
    return pl.pallas_call(
        kern, grid=(n // t,),
        in_specs=[pl.BlockSpec((t, d), lambda i: (i, 0)), pl.BlockSpec((TOP_K, t, d), lambda i: (0, i, 0)),
                  pl.BlockSpec((t, TOP_K), lambda i: (i, 0)),
                  pl.BlockSpec((1, d), lambda i: (0, 0)), pl.BlockSpec((1, d), lambda i: (0, 0))],
        out_specs=pl.BlockSpec((t, d), lambda i: (i, 0)),
        out_shape=jax.ShapeDtypeStruct((n, d), F32),
        compiler_params=_params("parallel"), name="moe_combine_ln",
    )(x2d, y_gathered, top_p, ln_w.reshape(1, d), ln_b.reshape(1, d))


def moe_ffn_ln(x_f32, x_bf16, router_w, router_b, w_up, b_up, w_down, b_down, layer, ln_w, ln_b, alpha):
    n, d = x_f32.shape
    top_e, top_p, rank, counts = moe_router(x_f32, router_w, router_b)
    padded = (counts + EXPERT_BLOCK - 1) // EXPERT_BLOCK * EXPERT_BLOCK
    pad_end = jnp.cumsum(padded)
    pad_start = pad_end - padded
    n_assign = n * TOP_K
    n_blocks = -(-(n_assign + N_EXPERTS * (EXPERT_BLOCK - 1)) // EXPERT_BLOCK)
    start = jnp.cumsum(counts) - counts
    dest = pad_start[top_e] + rank
    block_first = jnp.arange(n_blocks, dtype=jnp.int32) * EXPERT_BLOCK
    block_e = jnp.clip(jnp.sum((pad_end[None, :] <= block_first[:, None]).astype(jnp.int32), axis=1),
                       0, N_EXPERTS - 1)
    order = jnp.argsort(top_e.reshape(-1), stable=True)
    tok_sorted = (order // TOP_K).astype(jnp.int32)
    row = jnp.arange(n_blocks * EXPERT_BLOCK, dtype=jnp.int32)
    row_e = jnp.repeat(block_e, EXPERT_BLOCK)
    off = row - pad_start[row_e]
    src = jnp.where(off < counts[row_e], tok_sorted[jnp.clip(start[row_e] + off, 0, n_assign - 1)], row % n)
    xb = x_bf16[src]
    n_used = (pad_end[-1] // EXPERT_BLOCK).astype(jnp.int32)
    yb = moe_experts(xb, block_e, n_used, w_up, b_up, w_down, b_down, layer)
    return combine_ln(x_f32, yb[dest.T], top_p, ln_w, ln_b, alpha)


def kernel(x, positions, w_in, hg_lower_bounds, hg_norm_w, gm_ln_w, gm_ln_b, gm_spatial_w, gm_spatial_b, gm_norm_w, nsa_cmp_pe, nsa_cmp_w1, nsa_cmp_w2, nsa_norm_w, w_out, ln1_w, ln1_b, router_w, router_b, exp_w_up, exp_b_up, exp_w_down, exp_b_down, ln2_w, ln2_b):
    batch, seq, d = x.shape
    depth = w_in.shape[0]
    n = batch * seq
    alpha = (2 * depth) ** 0.25
    hg_w = hg_norm_w.shape[1]
    gm_w = gm_norm_w.shape[1]
    nsa_w = nsa_norm_w.shape[1]
    kv_w = NSA_KV_GROUPS * HEAD_DIM
    in_width = w_in.shape[2]
    off_gm = 4 * hg_w
    off_q = off_gm + 2 * gm_w
    off_kv = off_q + nsa_w
    off_gate = off_kv + 6 * kv_w
    width_pad = -(-in_width // LANES) * LANES

    cosf, sinf, cos_t, sin_t = rope_tables(positions)
    lb_all = jnp.cumsum(jax.nn.softmax(hg_lower_bounds.astype(F32), axis=0), axis=0)
    lb_all = lb_all - lb_all[0:1]

    x2d = x.reshape(n, d)
    for l in range(depth):
        w_l = jnp.pad(w_in[l], ((0, 0), (0, width_pad - in_width))).astype(BF16)
        h = in_proj(x2d, w_l)
        h3 = h.reshape(batch, seq, width_pad)
        y_hg = hgrn2(h3, lb_all[l], hg_norm_w[l]).reshape(n, hg_w)
        y_gm = gmlp(h, gm_ln_w[l], gm_ln_b[l], gm_spatial_w[l], gm_spatial_b[l], gm_norm_w[l],
                    off_gm // gm_w, off_gm // gm_w + 1)
        kc = compress(h3[:, :, off_kv:off_kv + kv_w], nsa_cmp_pe[l, 0], nsa_cmp_w1[l, 0], nsa_cmp_w2[l, 0], batch, seq)
        vc = compress(h3[:, :, off_kv + kv_w:off_kv + 2 * kv_w], nsa_cmp_pe[l, 1], nsa_cmp_w1[l, 1],
                      nsa_cmp_w2[l, 1], batch, seq)
        n_cmp = kc.shape[1]
        kc = kc.reshape(batch, n_cmp, NSA_KV_GROUPS, HEAD_DIM).transpose(0, 2, 1, 3)
        vc_t = vc.reshape(batch, n_cmp, NSA_KV_GROUPS, HEAD_DIM).transpose(0, 2, 3, 1)
        ksa, vs_t, kw, vw_t = nsa_kprep(h, cosf, sinf, batch, seq, (off_kv + 2 * kv_w) // LANES)
        y_nsa = nsa_attention(h, cos_t, sin_t, kc, vc_t, ksa, vs_t, kw, vw_t, nsa_norm_w[l], batch, seq,
                              off_q // nsa_w, off_gate // LANES)
        x1, x1b = out_proj_ln(x2d, y_hg, y_gm, y_nsa, w_out[l], ln1_w[l], ln1_b[l], alpha)
        x2d = moe_ffn_ln(x1, x1b, router_w[l], router_b[l], exp_w_up, exp_b_up, exp_w_down, exp_b_down, l,
                         ln2_w[l], ln2_b[l], alpha)
    return x2d.reshape(batch, seq, d)
```

```python
import functools
import math

import numpy as np
import jax
import jax.numpy as jnp
from jax import lax
from jax.experimental import pallas as pl
from jax.experimental.pallas import tpu as pltpu

F32 = jnp.float32
BF16 = jnp.bfloat16
HIGHEST = lax.Precision.HIGHEST

HEAD_DIM = 64
LANES = 128
VMEM_LIMIT = 48 * 1024 * 1024
EXPERT_VMEM_LIMIT = 56 * 1024 * 1024

HG_CHUNK = 64
GM_CHUNK = 128
GM_TILE_CHUNKS = 4
NSA_KV_GROUPS = 2
NSA_HPG = 4
CMP_BLOCK = 32
CMP_STRIDE = 16
CMP_HIDDEN = 128
SEL_BLOCK = 64
N_SEL = 16
WINDOW = 512
N_GATES = 3
IMP_FORCE = 1e9
IMP_FUTURE = -1e9
NEG_INF = -1e30
N_EXPERTS = 32
TOP_K = 4
SWIGLU_ALPHA = 1.702
SWIGLU_LIMIT = 7.0
EXPERT_BLOCK = 512
ROPE_THETA = 10000.0
LN_EPS = 1e-5
RMS_EPS = 1e-6
V_ROWS = HEAD_DIM + 16


def _params(*sem):
    return pltpu.CompilerParams(dimension_semantics=sem, vmem_limit_bytes=VMEM_LIMIT)


def _dot(a, b):
    return jnp.dot(a, b, preferred_element_type=F32)


def _dot_nt(a, b, precision=None):
    return lax.dot_general(a, b, (((1,), (1,)), ((), ())), precision=precision,
                           preferred_element_type=F32)


def _dot_tn(a, b):
    return lax.dot_general(a, b, (((0,), (0,)), ((), ())), preferred_element_type=F32)


def _sigmoid(x):
    return 1.0 / (1.0 + jnp.exp(-x))


def _gelu(x):
    return 0.5 * x * (1.0 + jnp.tanh(0.7978845608028654 * (x + 0.044715 * x * x * x)))


def _layer_norm(x, w, b):
    mu = jnp.mean(x, axis=-1, keepdims=True)
    xc = x - mu
    var = jnp.mean(xc * xc, axis=-1, keepdims=True)
    return xc * lax.rsqrt(var + LN_EPS) * w + b


def _pack_bf16_pairs(y):
    w = y.shape[1] // 2
    bits = pltpu.bitcast(y.astype(BF16).astype(F32), jnp.uint32)
    return lax.shift_right_logical(bits[:, :w], jnp.uint32(16)) | (bits[:, w:] & jnp.uint32(0xFFFF0000))


def _unpack_bf16_pairs(u):
    lo = pltpu.bitcast(lax.shift_left(u, jnp.uint32(16)), F32)
    hi = pltpu.bitcast(u & jnp.uint32(0xFFFF0000), F32)
    return lo, hi


def _head_mean_sq(o, bd_ones):
    sq = o * o
    hi = sq.astype(BF16)
    lo = (sq - hi.astype(F32)).astype(BF16)
    ones = bd_ones.astype(BF16)
    return (_dot(hi, ones) + _dot(lo, ones)) * (1.0 / HEAD_DIM)


def _rope_kernel(pos_ref, posr_ref, inv_ref, sign_ref, invc_ref, signc_ref, cos_ref, sin_ref, cost_ref, sint_ref):
    ang = pos_ref[...] * inv_ref[...]
    cos_ref[...] = jnp.cos(ang)
    sin_ref[...] = jnp.sin(ang) * sign_ref[...]
    ang_t = invc_ref[...] * posr_ref[...]
    cost_ref[...] = jnp.cos(ang_t)
    sint_ref[...] = jnp.sin(ang_t) * signc_ref[...]


def rope_tables(positions):
    n = positions.size
    tile = min(n, 2048)
    posf = positions.reshape(n).astype(F32)
    pos = jnp.broadcast_to(posf[:, None], (n, HEAD_DIM))
    inv = ROPE_THETA ** (-jnp.arange(0, HEAD_DIM, 2, dtype=F32) / HEAD_DIM)
    inv = jnp.concatenate([inv, inv])
    sign = jnp.concatenate([-jnp.ones((HEAD_DIM // 2,), F32), jnp.ones((HEAD_DIM // 2,), F32)])
    row = pl.BlockSpec((tile, HEAD_DIM), lambda i: (i, 0))
    rowt = pl.BlockSpec((HEAD_DIM, tile), lambda i: (0, i))
    const = pl.BlockSpec((1, HEAD_DIM), lambda i: (0, 0))
    constc = pl.BlockSpec((HEAD_DIM, 1), lambda i: (0, 0))
    return pl.pallas_call(
        _rope_kernel, grid=(n // tile,),
        in_specs=[row, pl.BlockSpec((1, tile), lambda i: (0, i)), const, const, constc, constc],
        out_specs=[row, row, rowt, rowt],
        out_shape=[jax.ShapeDtypeStruct((n, HEAD_DIM), F32)] * 2 + [jax.ShapeDtypeStruct((HEAD_DIM, n), F32)] * 2,
        compiler_params=_params("parallel"), name="rope_tables",
    )(pos, posf.reshape(1, n), inv.reshape(1, HEAD_DIM), sign.reshape(1, HEAD_DIM),
      inv.reshape(HEAD_DIM, 1), sign.reshape(HEAD_DIM, 1))


def _in_proj_kernel(x_ref, w_ref, h_ref):
    h_ref[...] = _dot(x_ref[...].astype(BF16), w_ref[...])


def in_proj(x2d, w_bf16):
    n, d = x2d.shape
    width = w_bf16.shape[1]
    tile = min(n, 512)
    return pl.pallas_call(
        _in_proj_kernel, grid=(n // tile,),
        in_specs=[pl.BlockSpec((tile, d), lambda i: (i, 0)), pl.BlockSpec((d, width), lambda i: (0, 0))],
        out_specs=pl.BlockSpec((tile, width), lambda i: (i, 0)),
        out_shape=jax.ShapeDtypeStruct((n, width), F32),
        compiler_params=_params("parallel"), name="in_proj")(x2d, w_bf16)


HG_LEVELS = (64, 32, 16, 8, 4, 2)
HG_BATCH = 4


def _hgrn_constants():
    c = HG_CHUNK
    t = np.arange(c)
    u = t[None, :]
    rows = [u <= t[:, None], u > t[:, None]]
    masks = [np.eye(c, dtype=bool)]
    for m in HG_LEVELS:
        ref = ((t // m) * m + m // 2 - 1)[:, None]
        second = (t % m >= m // 2)[:, None]
        rows.append(((u > ref) & (u <= t[:, None]) & second) | ((u > t[:, None]) & (u <= ref) & ~second))
        masks.append((t[:, None] // m == t[None, :] // m) & second & (t[None, :] % m < m // 2))
    pmat = np.concatenate(rows, axis=0).astype(np.float32)
    masks = np.stack([np.tile(mk, (1, 4)) for mk in masks]).astype(np.float32)
    return pmat, masks


def _hgrn_kernel(q_ref, f_ref, i_ref, g_ref, lb_ref, nw_ref, pmat_ref, masks_ref, bd_ref, hm_ref,
                 o_ref, state_ref):
    c = HG_CHUNK

    @pl.when(pl.program_id(1) == 0)
    def _():
        state_ref[...] = jnp.zeros_like(state_ref)

    lb = lb_ref[...]
    bd = bd_ref[...]
    hm = hm_ref[...]
    pmat = pmat_ref[...]
    a = jnp.log(lb)
    log1m = jnp.log(1.0 - lb)
    for bi in range(q_ref.shape[0]):
        fr = f_ref[bi]
        hq = q_ref[bi]
        qf = hq * _sigmoid(hq)
        log_sig = jnp.minimum(fr, 0.0) - jnp.log(1.0 + jnp.exp(-jnp.abs(fr)))
        cc = log1m + log_sig
        log_f = jnp.maximum(a, cc) + jnp.log(1.0 + jnp.exp(-jnp.abs(a - cc)))
        kk = (1.0 - lb) * _sigmoid(-fr)
        v = i_ref[bi]
        vb = v.astype(BF16)

        hi = log_f.astype(BF16)
        lo = (log_f - hi.astype(F32)).astype(BF16)
        sums = jnp.minimum(_dot(pmat, hi) + _dot(pmat, lo), 0.0)
        e_all = jnp.exp(sums)
        e_b = e_all[0:c]
        e_rest = e_all[c:2 * c]

        def stacked(x):
            return jnp.concatenate([x * hm[h:h + 1] for h in range(4)], axis=0).astype(BF16)

        att = masks_ref[0] * _dot_nt(qf.astype(BF16), stacked(kk))
        for li in range(len(HG_LEVELS)):
            e_l = e_all[(2 + li) * c:(3 + li) * c]
            att = att + masks_ref[li + 1] * _dot_nt((qf * e_l).astype(BF16), stacked(kk * e_l))
        o = _dot(att.astype(BF16), stacked(v))

        st = state_ref[bi]
        o = o + _dot_nt((qf * e_b).astype(BF16), st.astype(BF16))
        state_ref[bi] = st * e_b[c - 1:c] + bd * _dot_tn(vb, (kk * e_rest).astype(BF16))

        ms = _head_mean_sq(o, bd)
        y = o * lax.rsqrt(ms + RMS_EPS) * nw_ref[...] * _sigmoid(g_ref[bi])
        o_ref[bi] = y.astype(o_ref.dtype)


def hgrn2(h3, lb, norm_w):
    batch, seq, _ = h3.shape
    w = lb.shape[-1]
    c = HG_CHUNK
    nb = math.gcd(batch, HG_BATCH)
    pmat, masks = _hgrn_constants()
    lane_head = np.arange(w) // HEAD_DIM
    bd = (lane_head[:, None] == lane_head[None, :]).astype(np.float32)
    hm = (np.arange(4)[:, None] == lane_head[None, :]).astype(np.float32)

    def col(j):
        return pl.BlockSpec((nb, c, w), lambda b, i, j=j: (b, i, j))

    def const(shape):
        return pl.BlockSpec(shape, lambda b, i: (0,) * len(shape))

    return pl.pallas_call(
        _hgrn_kernel, grid=(batch // nb, seq // c),
        in_specs=[col(0), col(1), col(2), col(3), const((1, w)), const((1, w)),
                  const(pmat.shape), const(masks.shape), const(bd.shape), const(hm.shape)],
        out_specs=pl.BlockSpec((nb, c, w), lambda b, i: (b, i, 0)),
        out_shape=jax.ShapeDtypeStruct((batch, seq, w), BF16),
        scratch_shapes=[pltpu.VMEM((nb, w, w), F32)],
        compiler_params=_params("parallel", "arbitrary"), name="hgrn2",
    )(h3, h3, h3, h3, lb.reshape(1, w), norm_w.reshape(1, w), jnp.asarray(pmat, BF16), jnp.asarray(masks),
      jnp.asarray(bd), jnp.asarray(hm))


def _gmlp_kernel(u_ref, v_ref, lnw_ref, lnb_ref, ws_ref, bias_ref, nw_ref, bd_ref, hm_ref, o_ref):
    c = GM_CHUNK
    groups = ws_ref.shape[0]
    u = _gelu(u_ref[...])
    v = _layer_norm(_gelu(v_ref[...]), lnw_ref[...], lnb_ref[...])
    hm = hm_ref[...]
    bd = bd_ref[...]
    causal = lax.broadcasted_iota(jnp.int32, (c, c), 0) >= lax.broadcasted_iota(jnp.int32, (c, c), 1)
    w_cat = jnp.concatenate([jnp.where(causal, ws_ref[g], 0.0).astype(BF16) for g in range(groups)], axis=1)
    for j in range(u.shape[0] // c):
        rows = slice(j * c, (j + 1) * c)
        v_j = v[rows]
        v_bd = jnp.concatenate([v_j * hm[g:g + 1] for g in range(groups)], axis=0).astype(BF16)
        y = u[rows] * (bias_ref[...] + _dot(w_cat, v_bd))
        ms = _head_mean_sq(y, bd)
        o_ref[rows, :] = (y * lax.rsqrt(ms + RMS_EPS) * nw_ref[...]).astype(o_ref.dtype)


def gmlp(h, ln_w, ln_b, w_s, b_s, norm_w, u_col, v_col):
    n = h.shape[0]
    groups, c, _ = w_s.shape
    w = groups * HEAD_DIM
    lane_head = np.arange(w) // HEAD_DIM
    bd = (lane_head[:, None] == lane_head[None, :]).astype(np.float32)
    hm = (np.arange(groups)[:, None] == lane_head[None, :]).astype(np.float32)
    bias = jnp.repeat(b_s.T, HEAD_DIM, axis=1)

    def const(shape):
        return pl.BlockSpec(shape, lambda i: (0,) * len(shape))

    t = math.gcd(n, GM_TILE_CHUNKS * c)
    return pl.pallas_call(
        _gmlp_kernel, grid=(n // t,),
        in_specs=[pl.BlockSpec((t, w), lambda i: (i, u_col)), pl.BlockSpec((t, w), lambda i: (i, v_col)),
                  const((1, w)), const((1, w)), const(w_s.shape), const((c, w)), const((1, w)),
                  const(bd.shape), const(hm.shape)],
        out_specs=pl.BlockSpec((t, w), lambda i: (i, 0)),
        out_shape=jax.ShapeDtypeStruct((n, w), BF16),
        compiler_params=_params("parallel"), name="gmlp",
    )(h, h, ln_w.reshape(1, w), ln_b.reshape(1, w), w_s, bias, norm_w.reshape(1, w),
      jnp.asarray(bd), jnp.asarray(hm))


def _compress_kernel(u_ref, wtop_ref, wbot_ref, pe_ref, w2_ref, o_ref):
    u = u_ref[...].astype(BF16)
    wtop = wtop_ref[...]
    wbot = wbot_ref[...]
    pe = pe_ref[...].astype(BF16)
    const = _dot(pe[0:1], wtop) + _dot(pe[1:2], wbot)
    p = _dot(u, wtop)
    q = _dot(u, wbot)
    q_next = jnp.concatenate([q[1:], jnp.zeros_like(q[0:1])], axis=0)
    hid = _gelu(p + q_next + const)
    o_ref[...] = _dot(hid.astype(BF16), w2_ref[...]).astype(o_ref.dtype)


def compress(kv, pe, w1, w2, batch, seq):
    g = NSA_KV_GROUPS
    half = CMP_STRIDE
    units = seq // half
    gw = g * HEAD_DIM
    u = kv.reshape(batch * units, half * gw)
    eye = jnp.eye(g, dtype=F32)
    w1r = w1.reshape(2, half, HEAD_DIM, CMP_HIDDEN)
    wbd = jnp.einsum('hjdn,gk->hjgdkn', w1r, eye).reshape(2, half * gw, g * CMP_HIDDEN).astype(BF16)
    w2bd = jnp.einsum('nd,gk->gnkd', w2, eye).reshape(g * CMP_HIDDEN, gw).astype(BF16)
    pe2 = jnp.broadcast_to(pe.reshape(2, half, 1, HEAD_DIM), (2, half, g, HEAD_DIM)).reshape(2, half * gw)

    def const(shape):
        return pl.BlockSpec(shape, lambda b: (0,) * len(shape))

    out = pl.pallas_call(
        _compress_kernel, grid=(batch,),
        in_specs=[pl.BlockSpec((units, half * gw), lambda b: (b, 0)),
                  const(wbd.shape[1:]), const(wbd.shape[1:]), const(pe2.shape), const(w2bd.shape)],
        out_specs=pl.BlockSpec((units, gw), lambda b: (b, 0)),
        out_shape=jax.ShapeDtypeStruct((batch * units, gw), BF16),
        compiler_params=_params("parallel"), name="nsa_compress",
    )(u, wbd[0], wbd[1], pe2, w2bd)
    return out.reshape(batch, units, gw)


def _rot_half_pairs(x):
    lane = lax.broadcasted_iota(jnp.int32, x.shape, 1)
    fwd = pltpu.roll(x, 32, axis=1)
    bwd = pltpu.roll(x, 96, axis=1)
    return jnp.where((lane % HEAD_DIM) < HEAD_DIM // 2, bwd, fwd)


def _kprep_kernel(ks_ref, vs_ref, kw_ref, vw_ref, cos_ref, sin_ref, ksa_ref, vso_ref, kwo_ref, vwo_ref):
    t = ks_ref.shape[0]
    cos = cos_ref[...]
    sin = sin_ref[...]
    cos2 = jnp.concatenate([cos, cos], axis=1)
    sin2 = jnp.concatenate([sin, sin], axis=1)
    ks = ks_ref[...]
    kw = kw_ref[...]
    ks_r = ks * cos2 + _rot_half_pairs(ks) * sin2
    kw_r = kw * cos2 + _rot_half_pairs(kw) * sin2
    pos = pl.program_id(1) * t + lax.broadcasted_iota(jnp.int32, (t, HEAD_DIM), 0)
    onehot = (pos // SEL_BLOCK == lax.broadcasted_iota(jnp.int32, (t, HEAD_DIM), 1)).astype(F32)
    vs_t = vs_ref[...].T
    vw_t = vw_ref[...].T
    tail = (lax.broadcasted_iota(jnp.int32, (V_ROWS - HEAD_DIM, t), 0) == 0).astype(F32)
    for g in range(NSA_KV_GROUPS):
        sl = slice(g * HEAD_DIM, (g + 1) * HEAD_DIM)
        ksa_ref[g] = jnp.concatenate([ks_r[:, sl], onehot], axis=1).astype(BF16)
        vso_ref[g] = jnp.concatenate([vs_t[sl], tail], axis=0).astype(BF16)
        kwo_ref[g] = kw_r[:, sl].astype(BF16)
        vwo_ref[g] = jnp.concatenate([vw_t[sl], tail], axis=0).astype(BF16)


def nsa_kprep(h, cosf, sinf, batch, seq, col0):
    g = NSA_KV_GROUPS
    t = min(seq, 512)
    nt = seq // t

    def col(j):
        return pl.BlockSpec((t, LANES), lambda b, i, j=j: (b * nt + i, col0 + j))

    tab = pl.BlockSpec((t, HEAD_DIM), lambda b, i: (b * nt + i, 0))

    def out(wd):
        return pl.BlockSpec((None, g, t, wd), lambda b, i: (b, 0, i, 0))

    out_t = pl.BlockSpec((None, g, V_ROWS, t), lambda b, i: (b, 0, 0, i))
    k_shape = jax.ShapeDtypeStruct((batch, g, seq, HEAD_DIM), BF16)
    v_shape = jax.ShapeDtypeStruct((batch, g, V_ROWS, seq), BF16)
    return pl.pallas_call(
        _kprep_kernel, grid=(batch, nt),
        in_specs=[col(0), col(1), col(2), col(3), tab, tab],
        out_specs=[out(2 * HEAD_DIM), out_t, out(HEAD_DIM), out_t],
        out_shape=[jax.ShapeDtypeStruct((batch, g, seq, 2 * HEAD_DIM), BF16), v_shape, k_shape, v_shape],
        compiler_params=_params("parallel", "parallel"), name="nsa_kprep",
    )(h, h, h, h, cosf, sinf)


def _nsa_kernel(hq_ref, gate_ref, cos_ref, sin_ref, kc_ref, vc_ref, ksa_ref, vs_ref, kw_ref, vw_ref,
                ovl_ref, nw_ref, o_ref, imp_ref, *, tq, tk, n_sb):
    qi = pl.program_id(1)
    hpg = NSA_HPG
    groups = NSA_KV_GROUPS
    rows = hpg * tq
    t0 = qi * tq
    scale = 1.0 / math.sqrt(HEAD_DIM)
    half = HEAD_DIM // 2

    hq_t = hq_ref[...].T
    cos = cos_ref[...]
    sin = sin_ref[...]
    q_raw, q_rot = [], []
    for g in range(groups):
        raw_g, rot_g = [], []
        for h in range(hpg):
            r0 = (g * hpg + h) * HEAD_DIM
            qh = hq_t[r0:r0 + HEAD_DIM]
            swapped = jnp.concatenate([qh[half:], qh[:half]], axis=0)
            raw_g.append(qh * scale)
            rot_g.append((qh * cos + swapped * sin) * scale)
        q_raw.append(jnp.concatenate(raw_g, axis=1).astype(BF16))
        q_rot.append(jnp.concatenate(rot_g, axis=1))

    tpos = t0 + lax.broadcasted_iota(jnp.int32, (1, tq), 1)
    tpos_r = jnp.concatenate([tpos] * hpg, axis=1)

    def flash_step(s, v_t, m, acc):
        m_new = jnp.maximum(m, jnp.max(s, axis=0, keepdims=True))
        alpha = jnp.exp(m - m_new)
        p = jnp.exp(s - m_new).astype(BF16)
        return m_new, alpha * acc + _dot(v_t, p)

    init = (jnp.full((1, rows), NEG_INF, F32), jnp.zeros((V_ROWS, rows), F32))

    wk = WINDOW + tq
    kw0 = pl.multiple_of(jnp.maximum(t0 - WINDOW, 0), tq)
    kpos_w = kw0 + lax.broadcasted_iota(jnp.int32, (wk, 1), 0)
    mask_w = (kpos_w <= tpos_r) & (kpos_w > tpos_r - WINDOW)
    n_pad = jnp.maximum(WINDOW - 1 - tpos_r, 0).astype(F32)
    n_cmp = kc_ref.shape[1]
    cmp_end = lax.broadcasted_iota(jnp.int32, (n_cmp, 1), 0) * CMP_STRIDE + (CMP_BLOCK - 1)
    mask_c = cmp_end <= tpos_r
    blk = lax.broadcasted_iota(jnp.int32, (n_sb, tq), 0)
    cur = (t0 + lax.broadcasted_iota(jnp.int32, (n_sb, tq), 1)) // SEL_BLOCK
    o_w, o_c, imps = [], [], []
    for g in range(groups):
        s = _dot(kw_ref[g, pl.ds(kw0, wk), :], q_rot[g].astype(BF16))
        m_w, acc_w = flash_step(jnp.where(mask_w, s, NEG_INF), vw_ref[g, :, pl.ds(kw0, wk)], *init)
        m_f = jnp.where(n_pad > 0.0, jnp.maximum(m_w, 0.0), m_w)
        a_w = jnp.exp(m_w - m_f)
        o_w.append(acc_w[0:HEAD_DIM] * (a_w / (acc_w[HEAD_DIM:HEAD_DIM + 1] * a_w + n_pad * jnp.exp(-m_f))))

        s_c = jnp.where(mask_c, _dot(kc_ref[g], q_raw[g]), NEG_INF)
        e_c = jnp.exp(s_c - jnp.max(s_c, axis=0, keepdims=True))
        p_c = jnp.where(mask_c, e_c * (1.0 / jnp.sum(e_c, axis=0, keepdims=True)), 0.0)
        o_c.append(_dot(vc_ref[g], p_c.astype(BF16)))

        p_sum = p_c[:, 0:tq]
        for h in range(1, hpg):
            p_sum = p_sum + p_c[:, h * tq:(h + 1) * tq]
        imp = jnp.dot(ovl_ref[...], p_sum, precision=HIGHEST, preferred_element_type=F32)
        imp = jnp.where(blk > cur, IMP_FUTURE, imp)
        imp = jnp.where((blk == 0) | (blk == cur) | (blk == cur - 1), IMP_FORCE, imp)
        imp_ref[g] = imp
        imps.append(imp)

    def rank_body(j, ranks):
        out = []
        for g in range(groups):
            row = imp_ref[g, pl.ds(j, 1), :]
            ge = jnp.where(row >= imps[g], 1, 0)
            gt = jnp.where(row > imps[g], 1, 0)
            out.append(ranks[g] + jnp.where(j < blk, ge, gt))
        return tuple(out)

    ranks = lax.fori_loop(0, n_sb, rank_body, (jnp.zeros((n_sb, tq), jnp.int32),) * groups, unroll=8)
    q_aug = []
    for g in range(groups):
        sel_bias = jnp.where(ranks[g] < min(N_SEL, n_sb), 0.0, NEG_INF)
        if n_sb < HEAD_DIM:
            sel_bias = jnp.concatenate([sel_bias, jnp.zeros((HEAD_DIM - n_sb, tq), F32)], axis=0)
        q_aug.append(jnp.concatenate([q_rot[g], jnp.concatenate([sel_bias] * hpg, axis=1)], axis=0).astype(BF16))

    def scores(g, kt):
        k0 = pl.multiple_of(kt * tk, tk)
        return _dot(ksa_ref[g, pl.ds(k0, tk), :], q_aug[g])

    def sel_body(kt, carry):
        k0 = pl.multiple_of(kt * tk, tk)
        s = [scores(g, kt) for g in range(groups)]
        m_new = [jnp.maximum(carry[g][0], jnp.max(s[g], axis=0, keepdims=True)) for g in range(groups)]
        p = [jnp.exp(s[g] - m_new[g]).astype(BF16) for g in range(groups)]
        return tuple((m_new[g], jnp.exp(carry[g][0] - m_new[g]) * carry[g][1] + _dot(vs_ref[g, :, pl.ds(k0, tk)], p[g]))
                     for g in range(groups))

    n_full = t0 // tk
    carry = lax.fori_loop(0, n_full, sel_body, (init,) * groups)
    k0 = pl.multiple_of(n_full * tk, tk)
    kpos = k0 + lax.broadcasted_iota(jnp.int32, (tk, 1), 0)
    mask_s = kpos <= tpos_r

    gates = _sigmoid(gate_ref[...].T)
    nw = nw_ref[...]
    outs = []
    for g in range(groups):
        _, acc_s = flash_step(jnp.where(mask_s, scores(g, n_full), NEG_INF), vs_ref[g, :, pl.ds(k0, tk)], *carry[g])
        o_s = acc_s[0:HEAD_DIM] * (1.0 / acc_s[HEAD_DIM:HEAD_DIM + 1])
        for h in range(hpg):
            sl = slice(h * tq, (h + 1) * tq)
            r = (g * hpg + h) * N_GATES
            o = gates[r:r + 1] * o_c[g][:, sl] + gates[r + 1:r + 2] * o_s[:, sl] + gates[r + 2:r + 3] * o_w[g][:, sl]
            ms = jnp.mean(o * o, axis=0, keepdims=True)
            outs.append(o * lax.rsqrt(ms + RMS_EPS) * nw[:, g * hpg + h:g * hpg + h + 1])
    o_ref[...] = jnp.concatenate(outs, axis=0).T.astype(o_ref.dtype)


def nsa_attention(h, cos_t, sin_t, kc, vc_t, ksa, vs_t, kw, vw_t, norm_w, batch, seq, q_col0, gate_col):
    g, hpg = NSA_KV_GROUPS, NSA_HPG
    tq = min(seq, 256)
    nq = seq // tq
    n_sb = seq // SEL_BLOCK
    assert n_sb <= HEAD_DIM, "selection-block one-hot shares the 64 spare key lanes"
    n_cmp = kc.shape[2]
    units = np.arange(n_cmp)[:, None] + np.arange(CMP_BLOCK // CMP_STRIDE)[None, :]
    ovl = np.zeros((n_cmp, n_sb), np.float32)
    for c in range((seq - CMP_BLOCK) // CMP_STRIDE + 1):
        for u in units[c]:
            ovl[c, u // (SEL_BLOCK // CMP_STRIDE)] += 1.0
    ovl_t = jnp.asarray(ovl.T)

    def per_b(shape):
        return pl.BlockSpec((None, g) + shape, lambda b, qi: (b, 0, 0, 0))

    width = g * hpg * HEAD_DIM
    tab = pl.BlockSpec((HEAD_DIM, tq), lambda b, qi: (0, b * nq + qi))
    tk = min(seq, 512)
    assert seq >= WINDOW + tq and seq % tk == 0 and tk % tq == 0
    kern = functools.partial(_nsa_kernel, tq=tq, tk=tk, n_sb=n_sb)
    return pl.pallas_call(
        kern, grid=(batch, nq),
        in_specs=[pl.BlockSpec((tq, width), lambda b, qi: (b * nq + qi, q_col0)),
                  pl.BlockSpec((tq, LANES), lambda b, qi: (b * nq + qi, gate_col)),
                  tab, tab,
                  per_b((n_cmp, HEAD_DIM)), per_b((HEAD_DIM, n_cmp)),
                  per_b((seq, 2 * HEAD_DIM)), per_b((V_ROWS, seq)),
                  per_b((seq, HEAD_DIM)), per_b((V_ROWS, seq)),
                  pl.BlockSpec((n_sb, n_cmp), lambda b, qi: (0, 0)),
                  pl.BlockSpec((HEAD_DIM, g * hpg), lambda b, qi: (0, 0))],
        out_specs=pl.BlockSpec((tq, width), lambda b, qi: (b * nq + qi, 0)),
        out_shape=jax.ShapeDtypeStruct((batch * seq, width), BF16),
        scratch_shapes=[pltpu.VMEM((g, n_sb, tq), F32)],
        compiler_params=_params("parallel", "arbitrary"), name="nsa_attention",
    )(h, h, cos_t, sin_t, kc, vc_t, ksa, vs_t, kw, vw_t, ovl_t, norm_w.reshape(g * hpg, HEAD_DIM).T)


def _out_proj_kernel(x_ref, yhg_ref, ygm_ref, ynsa_ref, whg_ref, wgm_ref, wnsa_ref, lnw_ref, lnb_ref,
                     o_ref, ob_ref, *, alpha):
    mix = (_dot(yhg_ref[...], whg_ref[...]) + _dot(ygm_ref[...], wgm_ref[...])
           + _dot(ynsa_ref[...], wnsa_ref[...]))
    y = _layer_norm(alpha * x_ref[...] + mix, lnw_ref[...], lnb_ref[...])
    o_ref[...] = y
    ob_ref[...] = _pack_bf16_pairs(y)


def out_proj_ln(x2d, y_hg, y_gm, y_nsa, w_out, ln_w, ln_b, alpha):
    n, d = x2d.shape
    w1, w2 = y_hg.shape[1], y_hg.shape[1] + y_gm.shape[1]
    whg = w_out[:w1].astype(BF16)
    wgm = w_out[w1:w2].astype(BF16)
    wnsa = w_out[w2:].astype(BF16)
    t = min(n, 512)

    def row(wd):
        return pl.BlockSpec((t, wd), lambda i: (i, 0))

    def const(shape):
        return pl.BlockSpec(shape, lambda i: (0,) * len(shape))

    kern = functools.partial(_out_proj_kernel, alpha=alpha)
    return pl.pallas_call(
        kern, grid=(n // t,),
        in_specs=[row(d), row(y_hg.shape[1]), row(y_gm.shape[1]), row(y_nsa.shape[1]),
                  const(whg.shape), const(wgm.shape), const(wnsa.shape), const((1, d)), const((1, d))],
        out_specs=[row(d), row(d // 2)],
        out_shape=[jax.ShapeDtypeStruct((n, d), F32), jax.ShapeDtypeStruct((n, d // 2), jnp.uint32)],
        compiler_params=_params("parallel"), name="out_proj_ln",
    )(x2d, y_hg, y_gm, y_nsa, whg, wgm, wnsa, ln_w.reshape(1, d), ln_b.reshape(1, d))


def _router_kernel(x_ref, w_ref, b_ref, e_ref, p_ref, r_ref, cnt_ref, carry_ref):
    t = x_ref.shape[0]

    @pl.when(pl.program_id(0) == 0)
    def _():
        carry_ref[...] = jnp.zeros_like(carry_ref)

    x = x_ref[...]
    x_hi = x.astype(BF16)
    x_lo = (x - x_hi.astype(F32)).astype(BF16)
    w = w_ref[...]
    w_hi = w.astype(BF16)
    w_lo = (w - w_hi.astype(F32)).astype(BF16)
    logits = _dot(x_hi, w_hi) + (_dot(x_lo, w_hi) + _dot(x_hi, w_lo)) + b_ref[...]
    lane = lax.broadcasted_iota(jnp.int32, logits.shape, 1)
    work = logits
    vals, idxs = [], []
    sel = jnp.zeros(logits.shape, F32)
    for _ in range(TOP_K):
        m = jnp.max(work, axis=-1, keepdims=True)
        idx = jnp.min(jnp.where(work == m, lane, LANES), axis=-1, keepdims=True)
        hit = lane == idx
        sel = jnp.where(hit, 1.0, sel)
        work = jnp.where(hit, -jnp.inf, work)
        vals.append(m)
        idxs.append(idx)
    exps = [jnp.exp(v - vals[0]) for v in vals]
    den = exps[0] + exps[1] + exps[2] + exps[3]
    strict = (lax.broadcasted_iota(jnp.int32, (t, t), 0) > lax.broadcasted_iota(jnp.int32, (t, t), 1))
    before = _dot(strict.astype(BF16), sel.astype(BF16)) + carry_ref[...]
    ranks = [jnp.sum(jnp.where(lane == idx, before, 0.0), axis=-1, keepdims=True) for idx in idxs]
    kcol = lax.broadcasted_iota(jnp.int32, (t, TOP_K), 1)
    e_out = jnp.zeros((t, TOP_K), jnp.int32)
    p_out = jnp.zeros((t, TOP_K), F32)
    r_out = jnp.zeros((t, TOP_K), jnp.int32)
    for k in range(TOP_K):
        e_out = jnp.where(kcol == k, idxs[k], e_out)
        p_out = jnp.where(kcol == k, exps[k] / den, p_out)
        r_out = jnp.where(kcol == k, ranks[k].astype(jnp.int32), r_out)
    e_ref[...] = e_out
    p_ref[...] = p_out
    r_ref[...] = r_out
    carry_ref[...] = carry_ref[...] + jnp.sum(sel, axis=0, keepdims=True)
    cnt_ref[...] = carry_ref[...].astype(jnp.int32)


def moe_router(x2d, router_w, router_b):
    n, d = x2d.shape
    e = router_w.shape[1]
    t = min(n, 512)
    w = jnp.zeros((d, LANES), F32).at[:, :e].set(router_w)
    b = jnp.full((1, LANES), NEG_INF, F32).at[0, :e].set(router_b)
    row4 = pl.BlockSpec((t, TOP_K), lambda i: (i, 0))
    top_e, top_p, rank, counts = pl.pallas_call(
        _router_kernel, grid=(n // t,),
        in_specs=[pl.BlockSpec((t, d), lambda i: (i, 0)), pl.BlockSpec((d, LANES), lambda i: (0, 0)),
                  pl.BlockSpec((1, LANES), lambda i: (0, 0))],
        out_specs=[row4, row4, row4, pl.BlockSpec((1, LANES), lambda i: (0, 0))],
        out_shape=[jax.ShapeDtypeStruct((n, TOP_K), jnp.int32), jax.ShapeDtypeStruct((n, TOP_K), F32),
                   jax.ShapeDtypeStruct((n, TOP_K), jnp.int32), jax.ShapeDtypeStruct((1, LANES), jnp.int32)],
        scratch_shapes=[pltpu.VMEM((1, LANES), F32)],
        compiler_params=_params("arbitrary"), name="moe_router",
    )(x2d, w, b)
    return top_e, top_p, rank, counts[0, :e]


def _expert_kernel(be_ref, x_ref, wu_ref, bu_ref, wd_ref, bd_ref, o_ref, wu_bf, wd_bf):
    i = pl.program_id(0)
    f = wd_ref.shape[0]
    n_used = be_ref[pl.num_programs(0)]

    @pl.when((i == 0) | (be_ref[i] != be_ref[jnp.maximum(i - 1, 0)]))
    def _():
        wu_bf[...] = wu_ref[...].astype(BF16)
        wd_bf[...] = wd_ref[...].astype(BF16)

    @pl.when(i < n_used)
    def _():
        x_lo, x_hi = _unpack_bf16_pairs(x_ref[...])
        x = jnp.concatenate([x_lo.astype(BF16), x_hi.astype(BF16)], axis=1)
        hcat = _dot(x, wu_bf[...]) + bu_ref[...]
        glu = jnp.minimum(hcat[:, :f], SWIGLU_LIMIT)
        lin = jnp.clip(hcat[:, f:], -SWIGLU_LIMIT, SWIGLU_LIMIT)
        act = glu * _sigmoid(SWIGLU_ALPHA * glu) * (lin + 1.0)
        o_ref[...] = _pack_bf16_pairs(_dot(act.astype(BF16), wd_bf[...]) + bd_ref[...])

    @pl.when(i >= n_used)
    def _():
        o_ref[...] = jnp.zeros_like(o_ref)


def moe_experts(xb, block_e, n_used, w_up, b_up, w_down, b_down, layer):
    rows = xb.shape[0]
    _, e, d, f2 = w_up.shape
    f = f2 // 2
    nb = rows // EXPERT_BLOCK
    grid_spec = pltpu.PrefetchScalarGridSpec(
        num_scalar_prefetch=1, grid=(nb,),
        in_specs=[pl.BlockSpec((EXPERT_BLOCK, d // 2), lambda i, be: (i, 0)),
                  pl.BlockSpec((None, None, d, f2), lambda i, be: (layer, be[i], 0, 0)),
                  pl.BlockSpec((None, None, 1, f2), lambda i, be: (layer, be[i], 0, 0)),
                  pl.BlockSpec((None, None, f, d), lambda i, be: (layer, be[i], 0, 0)),
                  pl.BlockSpec((None, None, 1, d), lambda i, be: (layer, be[i], 0, 0))],
        out_specs=pl.BlockSpec((EXPERT_BLOCK, d // 2), lambda i, be: (i, 0)),
        scratch_shapes=[pltpu.VMEM((d, f2), BF16), pltpu.VMEM((f, d), BF16)])
    depth = w_up.shape[0]
    return pl.pallas_call(
        _expert_kernel, grid_spec=grid_spec,
        out_shape=jax.ShapeDtypeStruct((rows, d // 2), jnp.uint32),
        compiler_params=pltpu.CompilerParams(dimension_semantics=("arbitrary",), vmem_limit_bytes=EXPERT_VMEM_LIMIT),
        name="moe_experts",
    )(jnp.concatenate([block_e, n_used.reshape(1)]), xb, w_up, b_up.reshape(depth, e, 1, f2), w_down,
      b_down.reshape(depth, e, 1, d))


def _combine_kernel(x_ref, y_ref, p_ref, lnw_ref, lnb_ref, o_ref, *, alpha):
    p = p_ref[...]
    moe = jnp.zeros(x_ref.shape, F32)
    for k in range(TOP_K):
        y_lo, y_hi = _unpack_bf16_pairs(y_ref[k])
        moe = moe + p[:, k:k + 1] * jnp.concatenate([y_lo, y_hi], axis=1)
    o_ref[...] = _layer_norm(alpha * x_ref[...] + moe, lnw_ref[...], lnb_ref[...])


def combine_ln(x2d, y_gathered, top_p, ln_w, ln_b, alpha):
    n, d = x2d.shape
    t = min(n, 256)
    kern = functools.partial(_combine_kernel, alpha=alpha)
    return pl.pallas_call(
        kern, grid=(n // t,),
        in_specs=[pl.BlockSpec((t, d), lambda i: (i, 0)), pl.BlockSpec((TOP_K, t, d // 2), lambda i: (0, i, 0)),
                  pl.BlockSpec((t, TOP_K), lambda i: (i, 0)),
                  pl.BlockSpec((1, d), lambda i: (0, 0)), pl.BlockSpec((1, d), lambda i: (0, 0))],
        out_specs=pl.BlockSpec((t, d), lambda i: (i, 0)),
        out_shape=jax.ShapeDtypeStruct((n, d), F32),
        compiler_params=_params("parallel"), name="moe_combine_ln",
    )(x2d, y_gathered, top_p, ln_w.reshape(1, d), ln_b.reshape(1, d))


def moe_ffn_ln(x_f32, x_bf16, router_w, router_b, w_up, b_up, w_down, b_down, layer, ln_w, ln_b, alpha):
    n, d = x_f32.shape
    top_e, top_p, rank, counts = moe_router(x_f32, router_w, router_b)
    padded = (counts + EXPERT_BLOCK - 1) // EXPERT_BLOCK * EXPERT_BLOCK
    pad_end = jnp.cumsum(padded)
    pad_start = pad_end - padded
    n_assign = n * TOP_K
    n_blocks = -(-(n_assign + N_EXPERTS * (EXPERT_BLOCK - 1)) // EXPERT_BLOCK)
    start = jnp.cumsum(counts) - counts
    dest = pad_start[top_e] + rank
    block_first = jnp.arange(n_blocks, dtype=jnp.int32) * EXPERT_BLOCK
    block_e = jnp.clip(jnp.sum((pad_end[None, :] <= block_first[:, None]).astype(jnp.int32), axis=1),
                       0, N_EXPERTS - 1)
    order = jnp.argsort(top_e.reshape(-1), stable=True)
    tok_sorted = (order // TOP_K).astype(jnp.int32)
    row = jnp.arange(n_blocks * EXPERT_BLOCK, dtype=jnp.int32)
    row_e = jnp.repeat(block_e, EXPERT_BLOCK)
    off = row - pad_start[row_e]
    src = jnp.where(off < counts[row_e], tok_sorted[jnp.clip(start[row_e] + off, 0, n_assign - 1)], row % n)
    xb = x_bf16[src]
    n_used = (pad_end[-1] // EXPERT_BLOCK).astype(jnp.int32)
    yb = moe_experts(xb, block_e, n_used, w_up, b_up, w_down, b_down, layer)
    return combine_ln(x_f32, yb[dest.T], top_p, ln_w, ln_b, alpha)


def kernel(x, positions, w_in, hg_lower_bounds, hg_norm_w, gm_ln_w, gm_ln_b, gm_spatial_w, gm_spatial_b, gm_norm_w, nsa_cmp_pe, nsa_cmp_w1, nsa_cmp_w2, nsa_norm_w, w_out, ln1_w, ln1_b, router_w, router_b, exp_w_up, exp_b_up, exp_w_down, exp_b_down, ln2_w, ln2_b):
    batch, seq, d = x.shape
    depth = w_in.shape[0]
    n = batch * seq
    alpha = (2 * depth) ** 0.25
    hg_w = hg_norm_w.shape[1]
    gm_w = gm_norm_w.shape[1]
    nsa_w = nsa_norm_w.shape[1]
    kv_w = NSA_KV_GROUPS * HEAD_DIM
    in_width = w_in.shape[2]
    off_gm = 4 * hg_w
    off_q = off_gm + 2 * gm_w
    off_kv = off_q + nsa_w
    off_gate = off_kv + 6 * kv_w
    width_pad = -(-in_width // LANES) * LANES

    cosf, sinf, cos_t, sin_t = rope_tables(positions)
    lb_all = jnp.cumsum(jax.nn.softmax(hg_lower_bounds.astype(F32), axis=0), axis=0)
    lb_all = lb_all - lb_all[0:1]

    x2d = x.reshape(n, d)
    for l in range(depth):
        w_l = jnp.pad(w_in[l], ((0, 0), (0, width_pad - in_width))).astype(BF16)
        h = in_proj(x2d, w_l)
        h3 = h.reshape(batch, seq, width_pad)
        y_hg = hgrn2(h3, lb_all[l], hg_norm_w[l]).reshape(n, hg_w)
        y_gm = gmlp(h, gm_ln_w[l], gm_ln_b[l], gm_spatial_w[l], gm_spatial_b[l], gm_norm_w[l],
                    off_gm // gm_w, off_gm // gm_w + 1)
        kc = compress(h3[:, :, off_kv:off_kv + kv_w], nsa_cmp_pe[l, 0], nsa_cmp_w1[l, 0], nsa_cmp_w2[l, 0], batch, seq)
        vc = compress(h3[:, :, off_kv + kv_w:off_kv + 2 * kv_w], nsa_cmp_pe[l, 1], nsa_cmp_w1[l, 1],
                      nsa_cmp_w2[l, 1], batch, seq)
        n_cmp = kc.shape[1]
        kc = kc.reshape(batch, n_cmp, NSA_KV_GROUPS, HEAD_DIM).transpose(0, 2, 1, 3)
        vc_t = vc.reshape(batch, n_cmp, NSA_KV_GROUPS, HEAD_DIM).transpose(0, 2, 3, 1)
        ksa, vs_t, kw, vw_t = nsa_kprep(h, cosf, sinf, batch, seq, (off_kv + 2 * kv_w) // LANES)
        y_nsa = nsa_attention(h, cos_t, sin_t, kc, vc_t, ksa, vs_t, kw, vw_t, nsa_norm_w[l], batch, seq,
                              off_q // nsa_w, off_gate // LANES)
        x1, x1b = out_proj_ln(x2d, y_hg, y_gm, y_nsa, w_out[l], ln1_w[l], ln1_b[l], alpha)
        x2d = moe_ffn_ln(x1, x1b, router_w[l], router_b[l], exp_w_up, exp_b_up, exp_w_down, exp_b_down, l,
                         ln2_w[l], ln2_b[l], alpha)
    return x2d.reshape(batch, seq, d)
```

```python
import functools
import math

import numpy as np
import jax
import jax.numpy as jnp
from jax import lax
from jax.experimental import pallas as pl
from jax.experimental.pallas import tpu as pltpu
from jax.experimental.pallas import tpu_sc as plsc

F32 = jnp.float32
BF16 = jnp.bfloat16
HIGHEST = lax.Precision.HIGHEST

HEAD_DIM = 64
LANES = 128
VMEM_LIMIT = 48 * 1024 * 1024
EXPERT_VMEM_LIMIT = 56 * 1024 * 1024

HG_CHUNK = 64
GM_CHUNK = 128
GM_TILE_CHUNKS = 4
NSA_KV_GROUPS = 2
NSA_HPG = 4
CMP_BLOCK = 32
CMP_STRIDE = 16
CMP_HIDDEN = 128
SEL_BLOCK = 64
N_SEL = 16
WINDOW = 512
N_GATES = 3
IMP_FORCE = 1e9
IMP_FUTURE = -1e9
NEG_INF = -1e30
N_EXPERTS = 32
TOP_K = 4
SWIGLU_ALPHA = 1.702
SWIGLU_LIMIT = 7.0
EXPERT_BLOCK = 512
SC_ROW_WORDS = 256
SC_WINDOW = 128
ROPE_THETA = 10000.0
LN_EPS = 1e-5
RMS_EPS = 1e-6
V_ROWS = HEAD_DIM + 16


def _params(*sem):
    return pltpu.CompilerParams(dimension_semantics=sem, vmem_limit_bytes=VMEM_LIMIT)


def _dot(a, b):
    return jnp.dot(a, b, preferred_element_type=F32)


def _dot_nt(a, b, precision=None):
    return lax.dot_general(a, b, (((1,), (1,)), ((), ())), precision=precision,
                           preferred_element_type=F32)


def _dot_tn(a, b):
    return lax.dot_general(a, b, (((0,), (0,)), ((), ())), preferred_element_type=F32)


def _sigmoid(x):
    return 1.0 / (1.0 + jnp.exp(-x))


def _gelu(x):
    return 0.5 * x * (1.0 + jnp.tanh(0.7978845608028654 * (x + 0.044715 * x * x * x)))


def _layer_norm(x, w, b):
    mu = jnp.mean(x, axis=-1, keepdims=True)
    xc = x - mu
    var = jnp.mean(xc * xc, axis=-1, keepdims=True)
    return xc * lax.rsqrt(var + LN_EPS) * w + b


def _pack_bf16_pairs(y):
    w = y.shape[1] // 2
    bits = pltpu.bitcast(y.astype(BF16).astype(F32), jnp.uint32)
    return lax.shift_right_logical(bits[:, :w], jnp.uint32(16)) | (bits[:, w:] & jnp.uint32(0xFFFF0000))


def _unpack_bf16_pairs(u):
    lo = pltpu.bitcast(lax.shift_left(u, jnp.uint32(16)), F32)
    hi = pltpu.bitcast(u & jnp.uint32(0xFFFF0000), F32)
    return lo, hi


def _head_mean_sq(o, bd_ones):
    sq = o * o
    hi = sq.astype(BF16)
    lo = (sq - hi.astype(F32)).astype(BF16)
    ones = bd_ones.astype(BF16)
    return (_dot(hi, ones) + _dot(lo, ones)) * (1.0 / HEAD_DIM)


def _rope_kernel(pos_ref, posr_ref, inv_ref, sign_ref, invc_ref, signc_ref, cos_ref, sin_ref, cost_ref, sint_ref):
    ang = pos_ref[...] * inv_ref[...]
    cos_ref[...] = jnp.cos(ang)
    sin_ref[...] = jnp.sin(ang) * sign_ref[...]
    ang_t = invc_ref[...] * posr_ref[...]
    cost_ref[...] = jnp.cos(ang_t)
    sint_ref[...] = jnp.sin(ang_t) * signc_ref[...]


def rope_tables(positions):
    n = positions.size
    tile = min(n, 2048)
    posf = positions.reshape(n).astype(F32)
    pos = jnp.broadcast_to(posf[:, None], (n, HEAD_DIM))
    inv = ROPE_THETA ** (-jnp.arange(0, HEAD_DIM, 2, dtype=F32) / HEAD_DIM)
    inv = jnp.concatenate([inv, inv])
    sign = jnp.concatenate([-jnp.ones((HEAD_DIM // 2,), F32), jnp.ones((HEAD_DIM // 2,), F32)])
    row = pl.BlockSpec((tile, HEAD_DIM), lambda i: (i, 0))
    rowt = pl.BlockSpec((HEAD_DIM, tile), lambda i: (0, i))
    const = pl.BlockSpec((1, HEAD_DIM), lambda i: (0, 0))
    constc = pl.BlockSpec((HEAD_DIM, 1), lambda i: (0, 0))
    return pl.pallas_call(
        _rope_kernel, grid=(n // tile,),
        in_specs=[row, pl.BlockSpec((1, tile), lambda i: (0, i)), const, const, constc, constc],
        out_specs=[row, row, rowt, rowt],
        out_shape=[jax.ShapeDtypeStruct((n, HEAD_DIM), F32)] * 2 + [jax.ShapeDtypeStruct((HEAD_DIM, n), F32)] * 2,
        compiler_params=_params("parallel"), name="rope_tables",
    )(pos, posf.reshape(1, n), inv.reshape(1, HEAD_DIM), sign.reshape(1, HEAD_DIM),
      inv.reshape(HEAD_DIM, 1), sign.reshape(HEAD_DIM, 1))


def _in_proj_kernel(x_ref, w_ref, h_ref):
    h_ref[...] = _dot(x_ref[...].astype(BF16), w_ref[...])


def in_proj(x2d, w_bf16):
    n, d = x2d.shape
    width = w_bf16.shape[1]
    tile = min(n, 512)
    return pl.pallas_call(
        _in_proj_kernel, grid=(n // tile,),
        in_specs=[pl.BlockSpec((tile, d), lambda i: (i, 0)), pl.BlockSpec((d, width), lambda i: (0, 0))],
        out_specs=pl.BlockSpec((tile, width), lambda i: (i, 0)),
        out_shape=jax.ShapeDtypeStruct((n, width), F32),
        compiler_params=_params("parallel"), name="in_proj")(x2d, w_bf16)


HG_LEVELS = (64, 32, 16, 8, 4, 2)
HG_BATCH = 4


def _hgrn_constants():
    c = HG_CHUNK
    t = np.arange(c)
    u = t[None, :]
    rows = [u <= t[:, None], u > t[:, None]]
    masks = [np.eye(c, dtype=bool)]
    for m in HG_LEVELS:
        ref = ((t // m) * m + m // 2 - 1)[:, None]
        second = (t % m >= m // 2)[:, None]
        rows.append(((u > ref) & (u <= t[:, None]) & second) | ((u > t[:, None]) & (u <= ref) & ~second))
        masks.append((t[:, None] // m == t[None, :] // m) & second & (t[None, :] % m < m // 2))
    pmat = np.concatenate(rows, axis=0).astype(np.float32)
    masks = np.stack([np.tile(mk, (1, 4)) for mk in masks]).astype(np.float32)
    return pmat, masks


def _hgrn_kernel(q_ref, f_ref, i_ref, g_ref, lb_ref, nw_ref, pmat_ref, masks_ref, bd_ref, hm_ref,
                 o_ref, state_ref):
    c = HG_CHUNK

    @pl.when(pl.program_id(1) == 0)
    def _():
        state_ref[...] = jnp.zeros_like(state_ref)

    lb = lb_ref[...]
    bd = bd_ref[...]
    hm = hm_ref[...]
    pmat = pmat_ref[...]
    a = jnp.log(lb)
    log1m = jnp.log(1.0 - lb)
    for bi in range(q_ref.shape[0]):
        fr = f_ref[bi]
        hq = q_ref[bi]
        qf = hq * _sigmoid(hq)
        log_sig = jnp.minimum(fr, 0.0) - jnp.log(1.0 + jnp.exp(-jnp.abs(fr)))
        cc = log1m + log_sig
        log_f = jnp.maximum(a, cc) + jnp.log(1.0 + jnp.exp(-jnp.abs(a - cc)))
        kk = (1.0 - lb) * _sigmoid(-fr)
        v = i_ref[bi]
        vb = v.astype(BF16)

        hi = log_f.astype(BF16)
        lo = (log_f - hi.astype(F32)).astype(BF16)
        sums = jnp.minimum(_dot(pmat, hi) + _dot(pmat, lo), 0.0)
        e_all = jnp.exp(sums)
        e_b = e_all[0:c]
        e_rest = e_all[c:2 * c]

        def stacked(x):
            return jnp.concatenate([x * hm[h:h + 1] for h in range(4)], axis=0).astype(BF16)

        att = masks_ref[0] * _dot_nt(qf.astype(BF16), stacked(kk))
        for li in range(len(HG_LEVELS)):
            e_l = e_all[(2 + li) * c:(3 + li) * c]
            att = att + masks_ref[li + 1] * _dot_nt((qf * e_l).astype(BF16), stacked(kk * e_l))
        o = _dot(att.astype(BF16), stacked(v))

        st = state_ref[bi]
        o = o + _dot_nt((qf * e_b).astype(BF16), st.astype(BF16))
        state_ref[bi] = st * e_b[c - 1:c] + bd * _dot_tn(vb, (kk * e_rest).astype(BF16))

        ms = _head_mean_sq(o, bd)
        y = o * lax.rsqrt(ms + RMS_EPS) * nw_ref[...] * _sigmoid(g_ref[bi])
        o_ref[bi] = y.astype(o_ref.dtype)


def hgrn2(h3, lb, norm_w):
    batch, seq, _ = h3.shape
    w = lb.shape[-1]
    c = HG_CHUNK
    nb = math.gcd(batch, HG_BATCH)
    pmat, masks = _hgrn_constants()
    lane_head = np.arange(w) // HEAD_DIM
    bd = (lane_head[:, None] == lane_head[None, :]).astype(np.float32)
    hm = (np.arange(4)[:, None] == lane_head[None, :]).astype(np.float32)

    def col(j):
        return pl.BlockSpec((nb, c, w), lambda b, i, j=j: (b, i, j))

    def const(shape):
        return pl.BlockSpec(shape, lambda b, i: (0,) * len(shape))

    return pl.pallas_call(
        _hgrn_kernel, grid=(batch // nb, seq // c),
        in_specs=[col(0), col(1), col(2), col(3), const((1, w)), const((1, w)),
                  const(pmat.shape), const(masks.shape), const(bd.shape), const(hm.shape)],
        out_specs=pl.BlockSpec((nb, c, w), lambda b, i: (b, i, 0)),
        out_shape=jax.ShapeDtypeStruct((batch, seq, w), BF16),
        scratch_shapes=[pltpu.VMEM((nb, w, w), F32)],
        compiler_params=_params("parallel", "arbitrary"), name="hgrn2",
    )(h3, h3, h3, h3, lb.reshape(1, w), norm_w.reshape(1, w), jnp.asarray(pmat, BF16), jnp.asarray(masks),
      jnp.asarray(bd), jnp.asarray(hm))


def _gmlp_kernel(u_ref, v_ref, lnw_ref, lnb_ref, ws_ref, bias_ref, nw_ref, bd_ref, hm_ref, o_ref):
    c = GM_CHUNK
    groups = ws_ref.shape[0]
    u = _gelu(u_ref[...])
    v = _layer_norm(_gelu(v_ref[...]), lnw_ref[...], lnb_ref[...])
    hm = hm_ref[...]
    bd = bd_ref[...]
    causal = lax.broadcasted_iota(jnp.int32, (c, c), 0) >= lax.broadcasted_iota(jnp.int32, (c, c), 1)
    w_cat = jnp.concatenate([jnp.where(causal, ws_ref[g], 0.0).astype(BF16) for g in range(groups)], axis=1)
    for j in range(u.shape[0] // c):
        rows = slice(j * c, (j + 1) * c)
        v_j = v[rows]
        v_bd = jnp.concatenate([v_j * hm[g:g + 1] for g in range(groups)], axis=0).astype(BF16)
        y = u[rows] * (bias_ref[...] + _dot(w_cat, v_bd))
        ms = _head_mean_sq(y, bd)
        o_ref[rows, :] = (y * lax.rsqrt(ms + RMS_EPS) * nw_ref[...]).astype(o_ref.dtype)


def gmlp(h, ln_w, ln_b, w_s, b_s, norm_w, u_col, v_col):
    n = h.shape[0]
    groups, c, _ = w_s.shape
    w = groups * HEAD_DIM
    lane_head = np.arange(w) // HEAD_DIM
    bd = (lane_head[:, None] == lane_head[None, :]).astype(np.float32)
    hm = (np.arange(groups)[:, None] == lane_head[None, :]).astype(np.float32)
    bias = jnp.repeat(b_s.T, HEAD_DIM, axis=1)

    def const(shape):
        return pl.BlockSpec(shape, lambda i: (0,) * len(shape))

    t = math.gcd(n, GM_TILE_CHUNKS * c)
    return pl.pallas_call(
        _gmlp_kernel, grid=(n // t,),
        in_specs=[pl.BlockSpec((t, w), lambda i: (i, u_col)), pl.BlockSpec((t, w), lambda i: (i, v_col)),
                  const((1, w)), const((1, w)), const(w_s.shape), const((c, w)), const((1, w)),
                  const(bd.shape), const(hm.shape)],
        out_specs=pl.BlockSpec((t, w), lambda i: (i, 0)),
        out_shape=jax.ShapeDtypeStruct((n, w), BF16),
        compiler_params=_params("parallel"), name="gmlp",
    )(h, h, ln_w.reshape(1, w), ln_b.reshape(1, w), w_s, bias, norm_w.reshape(1, w),
      jnp.asarray(bd), jnp.asarray(hm))


def _compress_kernel(u_ref, wtop_ref, wbot_ref, pe_ref, w2_ref, o_ref):
    u = u_ref[...].astype(BF16)
    wtop = wtop_ref[...]
    wbot = wbot_ref[...]
    pe = pe_ref[...].astype(BF16)
    const = _dot(pe[0:1], wtop) + _dot(pe[1:2], wbot)
    p = _dot(u, wtop)
    q = _dot(u, wbot)
    q_next = jnp.concatenate([q[1:], jnp.zeros_like(q[0:1])], axis=0)
    hid = _gelu(p + q_next + const)
    o_ref[...] = _dot(hid.astype(BF16), w2_ref[...]).astype(o_ref.dtype)


def compress(kv, pe, w1, w2, batch, seq):
    g = NSA_KV_GROUPS
    half = CMP_STRIDE
    units = seq // half
    gw = g * HEAD_DIM
    u = kv.reshape(batch * units, half * gw)
    eye = jnp.eye(g, dtype=F32)
    w1r = w1.reshape(2, half, HEAD_DIM, CMP_HIDDEN)
    wbd = jnp.einsum('hjdn,gk->hjgdkn', w1r, eye).reshape(2, half * gw, g * CMP_HIDDEN).astype(BF16)
    w2bd = jnp.einsum('nd,gk->gnkd', w2, eye).reshape(g * CMP_HIDDEN, gw).astype(BF16)
    pe2 = jnp.broadcast_to(pe.reshape(2, half, 1, HEAD_DIM), (2, half, g, HEAD_DIM)).reshape(2, half * gw)

    def const(shape):
        return pl.BlockSpec(shape, lambda b: (0,) * len(shape))

    out = pl.pallas_call(
        _compress_kernel, grid=(batch,),
        in_specs=[pl.BlockSpec((units, half * gw), lambda b: (b, 0)),
                  const(wbd.shape[1:]), const(wbd.shape[1:]), const(pe2.shape), const(w2bd.shape)],
        out_specs=pl.BlockSpec((units, gw), lambda b: (b, 0)),
        out_shape=jax.ShapeDtypeStruct((batch * units, gw), BF16),
        compiler_params=_params("parallel"), name="nsa_compress",
    )(u, wbd[0], wbd[1], pe2, w2bd)
    return out.reshape(batch, units, gw)


def _rot_half_pairs(x):
    lane = lax.broadcasted_iota(jnp.int32, x.shape, 1)
    fwd = pltpu.roll(x, 32, axis=1)
    bwd = pltpu.roll(x, 96, axis=1)
    return jnp.where((lane % HEAD_DIM) < HEAD_DIM // 2, bwd, fwd)


def _kprep_kernel(ks_ref, vs_ref, kw_ref, vw_ref, cos_ref, sin_ref, ksa_ref, vso_ref, kwo_ref, vwo_ref):
    t = ks_ref.shape[0]
    cos = cos_ref[...]
    sin = sin_ref[...]
    cos2 = jnp.concatenate([cos, cos], axis=1)
    sin2 = jnp.concatenate([sin, sin], axis=1)
    ks = ks_ref[...]
    kw = kw_ref[...]
    ks_r = ks * cos2 + _rot_half_pairs(ks) * sin2
    kw_r = kw * cos2 + _rot_half_pairs(kw) * sin2
    pos = pl.program_id(1) * t + lax.broadcasted_iota(jnp.int32, (t, HEAD_DIM), 0)
    onehot = (pos // SEL_BLOCK == lax.broadcasted_iota(jnp.int32, (t, HEAD_DIM), 1)).astype(F32)
    vs_t = vs_ref[...].T
    vw_t = vw_ref[...].T
    tail = (lax.broadcasted_iota(jnp.int32, (V_ROWS - HEAD_DIM, t), 0) == 0).astype(F32)
    for g in range(NSA_KV_GROUPS):
        sl = slice(g * HEAD_DIM, (g + 1) * HEAD_DIM)
        ksa_ref[g] = jnp.concatenate([ks_r[:, sl], onehot], axis=1).astype(BF16)
        vso_ref[g] = jnp.concatenate([vs_t[sl], tail], axis=0).astype(BF16)
        kwo_ref[g] = kw_r[:, sl].astype(BF16)
        vwo_ref[g] = jnp.concatenate([vw_t[sl], tail], axis=0).astype(BF16)


def nsa_kprep(h, cosf, sinf, batch, seq, col0):
    g = NSA_KV_GROUPS
    t = min(seq, 512)
    nt = seq // t

    def col(j):
        return pl.BlockSpec((t, LANES), lambda b, i, j=j: (b * nt + i, col0 + j))

    tab = pl.BlockSpec((t, HEAD_DIM), lambda b, i: (b * nt + i, 0))

    def out(wd):
        return pl.BlockSpec((None, g, t, wd), lambda b, i: (b, 0, i, 0))

    out_t = pl.BlockSpec((None, g, V_ROWS, t), lambda b, i: (b, 0, 0, i))
    k_shape = jax.ShapeDtypeStruct((batch, g, seq, HEAD_DIM), BF16)
    v_shape = jax.ShapeDtypeStruct((batch, g, V_ROWS, seq), BF16)
    return pl.pallas_call(
        _kprep_kernel, grid=(batch, nt),
        in_specs=[col(0), col(1), col(2), col(3), tab, tab],
        out_specs=[out(2 * HEAD_DIM), out_t, out(HEAD_DIM), out_t],
        out_shape=[jax.ShapeDtypeStruct((batch, g, seq, 2 * HEAD_DIM), BF16), v_shape, k_shape, v_shape],
        compiler_params=_params("parallel", "parallel"), name="nsa_kprep",
    )(h, h, h, h, cosf, sinf)


def _nsa_kernel(hq_ref, gate_ref, cos_ref, sin_ref, kc_ref, vc_ref, ksa_ref, vs_ref, kw_ref, vw_ref,
                ovl_ref, nw_ref, o_ref, imp_ref, *, tq, tk, n_sb):
    qi = pl.program_id(1)
    hpg = NSA_HPG
    groups = NSA_KV_GROUPS
    rows = hpg * tq
    t0 = qi * tq
    scale = 1.0 / math.sqrt(HEAD_DIM)
    half = HEAD_DIM // 2

    hq_t = hq_ref[...].T
    cos = cos_ref[...]
    sin = sin_ref[...]
    q_raw, q_rot = [], []
    for g in range(groups):
        raw_g, rot_g = [], []
        for h in range(hpg):
            r0 = (g * hpg + h) * HEAD_DIM
            qh = hq_t[r0:r0 + HEAD_DIM]
            swapped = jnp.concatenate([qh[half:], qh[:half]], axis=0)
            raw_g.append(qh * scale)
            rot_g.append((qh * cos + swapped * sin) * scale)
        q_raw.append(jnp.concatenate(raw_g, axis=1).astype(BF16))
        q_rot.append(jnp.concatenate(rot_g, axis=1))

    tpos = t0 + lax.broadcasted_iota(jnp.int32, (1, tq), 1)
    tpos_r = jnp.concatenate([tpos] * hpg, axis=1)

    def flash_step(s, v_t, m, acc):
        m_new = jnp.maximum(m, jnp.max(s, axis=0, keepdims=True))
        alpha = jnp.exp(m - m_new)
        p = jnp.exp(s - m_new).astype(BF16)
        return m_new, alpha * acc + _dot(v_t, p)

    init = (jnp.full((1, rows), NEG_INF, F32), jnp.zeros((V_ROWS, rows), F32))

    wk = WINDOW + tq
    kw0 = pl.multiple_of(jnp.maximum(t0 - WINDOW, 0), tq)
    kpos_w = kw0 + lax.broadcasted_iota(jnp.int32, (wk, 1), 0)
    mask_w = (kpos_w <= tpos_r) & (kpos_w > tpos_r - WINDOW)
    n_pad = jnp.maximum(WINDOW - 1 - tpos_r, 0).astype(F32)
    n_cmp = kc_ref.shape[1]
    cmp_end = lax.broadcasted_iota(jnp.int32, (n_cmp, 1), 0) * CMP_STRIDE + (CMP_BLOCK - 1)
    mask_c = cmp_end <= tpos_r
    blk = lax.broadcasted_iota(jnp.int32, (n_sb, tq), 0)
    cur = (t0 + lax.broadcasted_iota(jnp.int32, (n_sb, tq), 1)) // SEL_BLOCK
    o_w, o_c, imps = [], [], []
    for g in range(groups):
        s = _dot(kw_ref[g, pl.ds(kw0, wk), :], q_rot[g].astype(BF16))
        m_w, acc_w = flash_step(jnp.where(mask_w, s, NEG_INF), vw_ref[g, :, pl.ds(kw0, wk)], *init)
        m_f = jnp.where(n_pad > 0.0, jnp.maximum(m_w, 0.0), m_w)
        a_w = jnp.exp(m_w - m_f)
        o_w.append(acc_w[0:HEAD_DIM] * (a_w / (acc_w[HEAD_DIM:HEAD_DIM + 1] * a_w + n_pad * jnp.exp(-m_f))))

        s_c = jnp.where(mask_c, _dot(kc_ref[g], q_raw[g]), NEG_INF)
        e_c = jnp.exp(s_c - jnp.max(s_c, axis=0, keepdims=True))
        p_c = jnp.where(mask_c, e_c * (1.0 / jnp.sum(e_c, axis=0, keepdims=True)), 0.0)
        o_c.append(_dot(vc_ref[g], p_c.astype(BF16)))

        p_sum = p_c[:, 0:tq]
        for h in range(1, hpg):
            p_sum = p_sum + p_c[:, h * tq:(h + 1) * tq]
        imp = jnp.dot(ovl_ref[...], p_sum, precision=HIGHEST, preferred_element_type=F32)
        imp = jnp.where(blk > cur, IMP_FUTURE, imp)
        imp = jnp.where((blk == 0) | (blk == cur) | (blk == cur - 1), IMP_FORCE, imp)
        imp_ref[g] = imp
        imps.append(imp)

    def rank_body(j, ranks):
        out = []
        for g in range(groups):
            row = imp_ref[g, pl.ds(j, 1), :]
            ge = jnp.where(row >= imps[g], 1, 0)
            gt = jnp.where(row > imps[g], 1, 0)
            out.append(ranks[g] + jnp.where(j < blk, ge, gt))
        return tuple(out)

    ranks = lax.fori_loop(0, n_sb, rank_body, (jnp.zeros((n_sb, tq), jnp.int32),) * groups, unroll=8)
    q_aug = []
    for g in range(groups):
        sel_bias = jnp.where(ranks[g] < min(N_SEL, n_sb), 0.0, NEG_INF)
        if n_sb < HEAD_DIM:
            sel_bias = jnp.concatenate([sel_bias, jnp.zeros((HEAD_DIM - n_sb, tq), F32)], axis=0)
        q_aug.append(jnp.concatenate([q_rot[g], jnp.concatenate([sel_bias] * hpg, axis=1)], axis=0).astype(BF16))

    def scores(g, kt):
        k0 = pl.multiple_of(kt * tk, tk)
        return _dot(ksa_ref[g, pl.ds(k0, tk), :], q_aug[g])

    def sel_body(kt, carry):
        k0 = pl.multiple_of(kt * tk, tk)
        s = [scores(g, kt) for g in range(groups)]
        m_new = [jnp.maximum(carry[g][0], jnp.max(s[g], axis=0, keepdims=True)) for g in range(groups)]
        p = [jnp.exp(s[g] - m_new[g]).astype(BF16) for g in range(groups)]
        return tuple((m_new[g], jnp.exp(carry[g][0] - m_new[g]) * carry[g][1] + _dot(vs_ref[g, :, pl.ds(k0, tk)], p[g]))
                     for g in range(groups))

    n_full = t0 // tk
    carry = lax.fori_loop(0, n_full, sel_body, (init,) * groups)
    k0 = pl.multiple_of(n_full * tk, tk)
    kpos = k0 + lax.broadcasted_iota(jnp.int32, (tk, 1), 0)
    mask_s = kpos <= tpos_r

    gates = _sigmoid(gate_ref[...].T)
    nw = nw_ref[...]
    outs = []
    for g in range(groups):
        _, acc_s = flash_step(jnp.where(mask_s, scores(g, n_full), NEG_INF), vs_ref[g, :, pl.ds(k0, tk)], *carry[g])
        o_s = acc_s[0:HEAD_DIM] * (1.0 / acc_s[HEAD_DIM:HEAD_DIM + 1])
        for h in range(hpg):
            sl = slice(h * tq, (h + 1) * tq)
            r = (g * hpg + h) * N_GATES
            o = gates[r:r + 1] * o_c[g][:, sl] + gates[r + 1:r + 2] * o_s[:, sl] + gates[r + 2:r + 3] * o_w[g][:, sl]
            ms = jnp.mean(o * o, axis=0, keepdims=True)
            outs.append(o * lax.rsqrt(ms + RMS_EPS) * nw[:, g * hpg + h:g * hpg + h + 1])
    o_ref[...] = jnp.concatenate(outs, axis=0).T.astype(o_ref.dtype)


def nsa_attention(h, cos_t, sin_t, kc, vc_t, ksa, vs_t, kw, vw_t, norm_w, batch, seq, q_col0, gate_col):
    g, hpg = NSA_KV_GROUPS, NSA_HPG
    tq = min(seq, 256)
    nq = seq // tq
    n_sb = seq // SEL_BLOCK
    assert n_sb <= HEAD_DIM, "selection-block one-hot shares the 64 spare key lanes"
    n_cmp = kc.shape[2]
    units = np.arange(n_cmp)[:, None] + np.arange(CMP_BLOCK // CMP_STRIDE)[None, :]
    ovl = np.zeros((n_cmp, n_sb), np.float32)
    for c in range((seq - CMP_BLOCK) // CMP_STRIDE + 1):
        for u in units[c]:
            ovl[c, u // (SEL_BLOCK // CMP_STRIDE)] += 1.0
    ovl_t = jnp.asarray(ovl.T)

    def per_b(shape):
        return pl.BlockSpec((None, g) + shape, lambda b, qi: (b, 0, 0, 0))

    width = g * hpg * HEAD_DIM
    tab = pl.BlockSpec((HEAD_DIM, tq), lambda b, qi: (0, b * nq + qi))
    tk = min(seq, 512)
    assert seq >= WINDOW + tq and seq % tk == 0 and tk % tq == 0
    kern = functools.partial(_nsa_kernel, tq=tq, tk=tk, n_sb=n_sb)
    return pl.pallas_call(
        kern, grid=(batch, nq),
        in_specs=[pl.BlockSpec((tq, width), lambda b, qi: (b * nq + qi, q_col0)),
                  pl.BlockSpec((tq, LANES), lambda b, qi: (b * nq + qi, gate_col)),
                  tab, tab,
                  per_b((n_cmp, HEAD_DIM)), per_b((HEAD_DIM, n_cmp)),
                  per_b((seq, 2 * HEAD_DIM)), per_b((V_ROWS, seq)),
                  per_b((seq, HEAD_DIM)), per_b((V_ROWS, seq)),
                  pl.BlockSpec((n_sb, n_cmp), lambda b, qi: (0, 0)),
                  pl.BlockSpec((HEAD_DIM, g * hpg), lambda b, qi: (0, 0))],
        out_specs=pl.BlockSpec((tq, width), lambda b, qi: (b * nq + qi, 0)),
        out_shape=jax.ShapeDtypeStruct((batch * seq, width), BF16),
        scratch_shapes=[pltpu.VMEM((g, n_sb, tq), F32)],
        compiler_params=_params("parallel", "arbitrary"), name="nsa_attention",
    )(h, h, cos_t, sin_t, kc, vc_t, ksa, vs_t, kw, vw_t, ovl_t, norm_w.reshape(g * hpg, HEAD_DIM).T)


def _out_proj_kernel(x_ref, yhg_ref, ygm_ref, ynsa_ref, whg_ref, wgm_ref, wnsa_ref, lnw_ref, lnb_ref,
                     o_ref, ob_ref, *, alpha):
    mix = (_dot(yhg_ref[...], whg_ref[...]) + _dot(ygm_ref[...], wgm_ref[...])
           + _dot(ynsa_ref[...], wnsa_ref[...]))
    y = _layer_norm(alpha * x_ref[...] + mix, lnw_ref[...], lnb_ref[...])
    o_ref[...] = y
    ob_ref[...] = _pack_bf16_pairs(y)


def out_proj_ln(x2d, y_hg, y_gm, y_nsa, w_out, ln_w, ln_b, alpha):
    n, d = x2d.shape
    w1, w2 = y_hg.shape[1], y_hg.shape[1] + y_gm.shape[1]
    whg = w_out[:w1].astype(BF16)
    wgm = w_out[w1:w2].astype(BF16)
    wnsa = w_out[w2:].astype(BF16)
    t = min(n, 512)

    def row(wd):
        return pl.BlockSpec((t, wd), lambda i: (i, 0))

    def const(shape):
        return pl.BlockSpec(shape, lambda i: (0,) * len(shape))

    kern = functools.partial(_out_proj_kernel, alpha=alpha)
    return pl.pallas_call(
        kern, grid=(n // t,),
        in_specs=[row(d), row(y_hg.shape[1]), row(y_gm.shape[1]), row(y_nsa.shape[1]),
                  const(whg.shape), const(wgm.shape), const(wnsa.shape), const((1, d)), const((1, d))],
        out_specs=[row(d), row(d // 2)],
        out_shape=[jax.ShapeDtypeStruct((n, d), F32), jax.ShapeDtypeStruct((n, d // 2), jnp.uint32)],
        compiler_params=_params("parallel"), name="out_proj_ln",
    )(x2d, y_hg, y_gm, y_nsa, whg, wgm, wnsa, ln_w.reshape(1, d), ln_b.reshape(1, d))


def _router_kernel(x_ref, w_ref, b_ref, e_ref, p_ref, r_ref, cnt_ref, carry_ref):
    t = x_ref.shape[0]

    @pl.when(pl.program_id(0) == 0)
    def _():
        carry_ref[...] = jnp.zeros_like(carry_ref)

    x = x_ref[...]
    x_hi = x.astype(BF16)
    x_lo = (x - x_hi.astype(F32)).astype(BF16)
    w = w_ref[...]
    w_hi = w.astype(BF16)
    w_lo = (w - w_hi.astype(F32)).astype(BF16)
    logits = _dot(x_hi, w_hi) + (_dot(x_lo, w_hi) + _dot(x_hi, w_lo)) + b_ref[...]
    lane = lax.broadcasted_iota(jnp.int32, logits.shape, 1)
    work = logits
    vals, idxs = [], []
    sel = jnp.zeros(logits.shape, F32)
    for _ in range(TOP_K):
        m = jnp.max(work, axis=-1, keepdims=True)
        idx = jnp.min(jnp.where(work == m, lane, LANES), axis=-1, keepdims=True)
        hit = lane == idx
        sel = jnp.where(hit, 1.0, sel)
        work = jnp.where(hit, -jnp.inf, work)
        vals.append(m)
        idxs.append(idx)
    exps = [jnp.exp(v - vals[0]) for v in vals]
    den = exps[0] + exps[1] + exps[2] + exps[3]
    strict = (lax.broadcasted_iota(jnp.int32, (t, t), 0) > lax.broadcasted_iota(jnp.int32, (t, t), 1))
    before = _dot(strict.astype(BF16), sel.astype(BF16)) + carry_ref[...]
    ranks = [jnp.sum(jnp.where(lane == idx, before, 0.0), axis=-1, keepdims=True) for idx in idxs]
    kcol = lax.broadcasted_iota(jnp.int32, (t, TOP_K), 1)
    e_out = jnp.zeros((t, TOP_K), jnp.int32)
    p_out = jnp.zeros((t, TOP_K), F32)
    r_out = jnp.zeros((t, TOP_K), jnp.int32)
    for k in range(TOP_K):
        e_out = jnp.where(kcol == k, idxs[k], e_out)
        p_out = jnp.where(kcol == k, exps[k] / den, p_out)
        r_out = jnp.where(kcol == k, ranks[k].astype(jnp.int32), r_out)
    e_ref[...] = e_out
    p_ref[...] = p_out
    r_ref[...] = r_out
    carry_ref[...] = carry_ref[...] + jnp.sum(sel, axis=0, keepdims=True)
    cnt_ref[...] = carry_ref[...].astype(jnp.int32)


def moe_router(x2d, router_w, router_b):
    n, d = x2d.shape
    e = router_w.shape[1]
    t = min(n, 512)
    w = jnp.zeros((d, LANES), F32).at[:, :e].set(router_w)
    b = jnp.full((1, LANES), NEG_INF, F32).at[0, :e].set(router_b)
    row4 = pl.BlockSpec((t, TOP_K), lambda i: (i, 0))
    top_e, top_p, rank, counts = pl.pallas_call(
        _router_kernel, grid=(n // t,),
        in_specs=[pl.BlockSpec((t, d), lambda i: (i, 0)), pl.BlockSpec((d, LANES), lambda i: (0, 0)),
                  pl.BlockSpec((1, LANES), lambda i: (0, 0))],
        out_specs=[row4, row4, row4, pl.BlockSpec((1, LANES), lambda i: (0, 0))],
        out_shape=[jax.ShapeDtypeStruct((n, TOP_K), jnp.int32), jax.ShapeDtypeStruct((n, TOP_K), F32),
                   jax.ShapeDtypeStruct((n, TOP_K), jnp.int32), jax.ShapeDtypeStruct((1, LANES), jnp.int32)],
        scratch_shapes=[pltpu.VMEM((1, LANES), F32)],
        compiler_params=_params("arbitrary"), name="moe_router",
    )(x2d, w, b)
    return top_e, top_p, rank, counts[0, :e]


def _expert_kernel(be_ref, valid_ref, x_ref, wu_ref, bu_ref, wd_ref, bd_ref, o_ref, wu_bf, wd_bf):
    i = pl.program_id(0)
    f = wd_ref.shape[0]
    n_used = be_ref[pl.num_programs(0)]

    @pl.when((i == 0) | (be_ref[i] != be_ref[jnp.maximum(i - 1, 0)]))
    def _():
        wu_bf[...] = wu_ref[...].astype(BF16)
        wd_bf[...] = wd_ref[...].astype(BF16)

    @pl.when(i < n_used)
    def _():
        live = lax.broadcasted_iota(jnp.int32, x_ref.shape, 0) < valid_ref[i]
        x_lo, x_hi = _unpack_bf16_pairs(jnp.where(live, x_ref[...], jnp.uint32(0)))
        x = jnp.concatenate([x_lo.astype(BF16), x_hi.astype(BF16)], axis=1)
        hcat = _dot(x, wu_bf[...]) + bu_ref[...]
        glu = jnp.minimum(hcat[:, :f], SWIGLU_LIMIT)
        lin = jnp.clip(hcat[:, f:], -SWIGLU_LIMIT, SWIGLU_LIMIT)
        act = glu * _sigmoid(SWIGLU_ALPHA * glu) * (lin + 1.0)
        o_ref[...] = _pack_bf16_pairs(_dot(act.astype(BF16), wd_bf[...]) + bd_ref[...])

    @pl.when(i >= n_used)
    def _():
        o_ref[...] = jnp.zeros_like(o_ref)


def moe_experts(xb, block_e, n_used, block_valid, w_up, b_up, w_down, b_down, layer):
    rows = xb.shape[0]
    _, e, d, f2 = w_up.shape
    f = f2 // 2
    nb = rows // EXPERT_BLOCK
    grid_spec = pltpu.PrefetchScalarGridSpec(
        num_scalar_prefetch=2, grid=(nb,),
        in_specs=[pl.BlockSpec((EXPERT_BLOCK, d // 2), lambda i, be, nv: (i, 0)),
                  pl.BlockSpec((None, None, d, f2), lambda i, be, nv: (layer, be[i], 0, 0)),
                  pl.BlockSpec((None, None, 1, f2), lambda i, be, nv: (layer, be[i], 0, 0)),
                  pl.BlockSpec((None, None, f, d), lambda i, be, nv: (layer, be[i], 0, 0)),
                  pl.BlockSpec((None, None, 1, d), lambda i, be, nv: (layer, be[i], 0, 0))],
        out_specs=pl.BlockSpec((EXPERT_BLOCK, d // 2), lambda i, be, nv: (i, 0)),
        scratch_shapes=[pltpu.VMEM((d, f2), BF16), pltpu.VMEM((f, d), BF16)])
    depth = w_up.shape[0]
    return pl.pallas_call(
        _expert_kernel, grid_spec=grid_spec,
        out_shape=jax.ShapeDtypeStruct((rows, d // 2), jnp.uint32),
        compiler_params=pltpu.CompilerParams(dimension_semantics=("arbitrary",), vmem_limit_bytes=EXPERT_VMEM_LIMIT),
        name="moe_experts",
    )(jnp.concatenate([block_e, n_used.reshape(1)]), block_valid, xb, w_up, b_up.reshape(depth, e, 1, f2),
      w_down, b_down.reshape(depth, e, 1, d))


def _combine_kernel(x_ref, y_ref, p_ref, lnw_ref, lnb_ref, o_ref, *, alpha):
    p = p_ref[...]
    moe = jnp.zeros(x_ref.shape, F32)
    for k in range(TOP_K):
        y_lo, y_hi = _unpack_bf16_pairs(y_ref[k])
        moe = moe + p[:, k:k + 1] * jnp.concatenate([y_lo, y_hi], axis=1)
    o_ref[...] = _layer_norm(alpha * x_ref[...] + moe, lnw_ref[...], lnb_ref[...])


def combine_ln(x2d, y_gathered, top_p, ln_w, ln_b, alpha):
    n, d = x2d.shape
    t = min(n, 256)
    kern = functools.partial(_combine_kernel, alpha=alpha)
    return pl.pallas_call(
        kern, grid=(n // t,),
        in_specs=[pl.BlockSpec((t, d), lambda i: (i, 0)), pl.BlockSpec((TOP_K, t, d // 2), lambda i: (0, i, 0)),
                  pl.BlockSpec((t, TOP_K), lambda i: (i, 0)),
                  pl.BlockSpec((1, d), lambda i: (0, 0)), pl.BlockSpec((1, d), lambda i: (0, 0))],
        out_specs=pl.BlockSpec((t, d), lambda i: (i, 0)),
        out_shape=jax.ShapeDtypeStruct((n, d), F32),
        compiler_params=_params("parallel"), name="moe_combine_ln",
    )(x2d, y_gathered, top_p, ln_w.reshape(1, d), ln_b.reshape(1, d))


def _sc_mesh():
    return plsc.VectorSubcoreMesh(core_axis_name="core", subcore_axis_name="subcore")


def _split_rows(idx):
    return (idx[..., None] * 2 + jnp.arange(2, dtype=jnp.int32)).reshape(idx.shape[:-1] + (2 * idx.shape[-1],))


def sc_gather_rows(table, idx):
    v, w = table.shape
    assert w == 2 * SC_ROW_WORDS
    r2 = 2 * idx.shape[0]

    @pl.kernel(out_type=jax.ShapeDtypeStruct((r2, SC_ROW_WORDS), table.dtype), mesh=_sc_mesh(),
               name="sc_gather_rows")
    def gather(x_hbm, i_hbm, o_hbm):
        def body(i_vmem, o_vmem):
            pltpu.sync_copy(x_hbm.at[i_vmem.at[0]], o_vmem)

        pltpu.emit_pipeline(
            body, grid=(r2 // SC_WINDOW,),
            in_specs=[pl.BlockSpec((1, SC_WINDOW), lambda i: (0, i))],
            out_specs=[pl.BlockSpec((SC_WINDOW, SC_ROW_WORDS), lambda i: (i, 0))],
            core_axis_name=("core", "subcore"), dimension_semantics=(pltpu.PARALLEL,),
        )(i_hbm, o_hbm)

    return gather(table.reshape(2 * v, SC_ROW_WORDS), _split_rows(idx).reshape(1, r2)).reshape(idx.shape[0], w)


def sc_scatter_rows(x, dest_t, n_rows):
    n, w = x.shape
    assert w == 2 * SC_ROW_WORDS
    copies = dest_t.shape[0]

    @pl.kernel(out_type=jax.ShapeDtypeStruct((2 * n_rows, SC_ROW_WORDS), x.dtype), mesh=_sc_mesh(),
               scratch_types=[], name="sc_scatter_rows")
    def scatter(x_hbm, i_hbm, o_hbm):
        def body(x_vmem, i_vmem):
            for k in range(copies):
                pltpu.sync_copy(x_vmem, o_hbm.at[i_vmem.at[k]])

        pltpu.emit_pipeline(
            body, grid=(2 * n // SC_WINDOW,),
            in_specs=[pl.BlockSpec((SC_WINDOW, SC_ROW_WORDS), lambda i: (i, 0)),
                      pl.BlockSpec((copies, SC_WINDOW), lambda i: (0, i))],
            out_specs=[],
            core_axis_name=("core", "subcore"), dimension_semantics=(pltpu.PARALLEL,),
        )(x_hbm, i_hbm)

    return scatter(x.reshape(2 * n, SC_ROW_WORDS), _split_rows(dest_t)).reshape(n_rows, w)


def moe_ffn_ln(x_f32, x_packed, router_w, router_b, w_up, b_up, w_down, b_down, layer, ln_w, ln_b, alpha):
    n, d = x_f32.shape
    top_e, top_p, rank, counts = moe_router(x_f32, router_w, router_b)
    padded = (counts + EXPERT_BLOCK - 1) // EXPERT_BLOCK * EXPERT_BLOCK
    pad_end = jnp.cumsum(padded)
    pad_start = pad_end - padded
    n_assign = n * TOP_K
    n_blocks = -(-(n_assign + N_EXPERTS * (EXPERT_BLOCK - 1)) // EXPERT_BLOCK)
    dest_t = (pad_start[top_e] + rank).T
    block_first = jnp.arange(n_blocks, dtype=jnp.int32) * EXPERT_BLOCK
    block_e = jnp.clip(jnp.sum((pad_end[None, :] <= block_first[:, None]).astype(jnp.int32), axis=1),
                       0, N_EXPERTS - 1)
    block_valid = jnp.clip(counts[block_e] - (block_first - pad_start[block_e]), 0, EXPERT_BLOCK)
    n_used = (pad_end[-1] // EXPERT_BLOCK).astype(jnp.int32)
    xb = sc_scatter_rows(x_packed, dest_t, n_blocks * EXPERT_BLOCK)
    yb = moe_experts(xb, block_e, n_used, block_valid.astype(jnp.int32), w_up, b_up, w_down, b_down, layer)
    y_tok = sc_gather_rows(yb, dest_t.reshape(-1)).reshape(TOP_K, n, d // 2)
    return combine_ln(x_f32, y_tok, top_p, ln_w, ln_b, alpha)


def kernel(x, positions, w_in, hg_lower_bounds, hg_norm_w, gm_ln_w, gm_ln_b, gm_spatial_w, gm_spatial_b, gm_norm_w, nsa_cmp_pe, nsa_cmp_w1, nsa_cmp_w2, nsa_norm_w, w_out, ln1_w, ln1_b, router_w, router_b, exp_w_up, exp_b_up, exp_w_down, exp_b_down, ln2_w, ln2_b):
    batch, seq, d = x.shape
    depth = w_in.shape[0]
    n = batch * seq
    alpha = (2 * depth) ** 0.25
    hg_w = hg_norm_w.shape[1]
    gm_w = gm_norm_w.shape[1]
    nsa_w = nsa_norm_w.shape[1]
    kv_w = NSA_KV_GROUPS * HEAD_DIM
    in_width = w_in.shape[2]
    off_gm = 4 * hg_w
    off_q = off_gm + 2 * gm_w
    off_kv = off_q + nsa_w
    off_gate = off_kv + 6 * kv_w
    width_pad = -(-in_width // LANES) * LANES

    cosf, sinf, cos_t, sin_t = rope_tables(positions)
    lb_all = jnp.cumsum(jax.nn.softmax(hg_lower_bounds.astype(F32), axis=0), axis=0)
    lb_all = lb_all - lb_all[0:1]

    x2d = x.reshape(n, d)
    for l in range(depth):
        w_l = jnp.pad(w_in[l], ((0, 0), (0, width_pad - in_width))).astype(BF16)
        h = in_proj(x2d, w_l)
        h3 = h.reshape(batch, seq, width_pad)
        y_hg = hgrn2(h3, lb_all[l], hg_norm_w[l]).reshape(n, hg_w)
        y_gm = gmlp(h, gm_ln_w[l], gm_ln_b[l], gm_spatial_w[l], gm_spatial_b[l], gm_norm_w[l],
                    off_gm // gm_w, off_gm // gm_w + 1)
        kc = compress(h3[:, :, off_kv:off_kv + kv_w], nsa_cmp_pe[l, 0], nsa_cmp_w1[l, 0], nsa_cmp_w2[l, 0], batch, seq)
        vc = compress(h3[:, :, off_kv + kv_w:off_kv + 2 * kv_w], nsa_cmp_pe[l, 1], nsa_cmp_w1[l, 1],
                      nsa_cmp_w2[l, 1], batch, seq)
        n_cmp = kc.shape[1]
        kc = kc.reshape(batch, n_cmp, NSA_KV_GROUPS, HEAD_DIM).transpose(0, 2, 1, 3)
        vc_t = vc.reshape(batch, n_cmp, NSA_KV_GROUPS, HEAD_DIM).transpose(0, 2, 3, 1)
        ksa, vs_t, kw, vw_t = nsa_kprep(h, cosf, sinf, batch, seq, (off_kv + 2 * kv_w) // LANES)
        y_nsa = nsa_attention(h, cos_t, sin_t, kc, vc_t, ksa, vs_t, kw, vw_t, nsa_norm_w[l], batch, seq,
                              off_q // nsa_w, off_gate // LANES)
        x1, x1b = out_proj_ln(x2d, y_hg, y_gm, y_nsa, w_out[l], ln1_w[l], ln1_b[l], alpha)
        x2d = moe_ffn_ln(x1, x1b, router_w[l], router_b[l], exp_w_up, exp_b_up, exp_w_down, exp_b_down, l,
                         ln2_w[l], ln2_b[l], alpha)
    return x2d.reshape(batch, seq, d)
```

```python
import functools
import math

import numpy as np
import jax
import jax.numpy as jnp
from jax import lax
from jax.experimental import pallas as pl
from jax.experimental.pallas import tpu as pltpu
from jax.experimental.pallas import tpu_sc as plsc

F32 = jnp.float32
BF16 = jnp.bfloat16
HIGHEST = lax.Precision.HIGHEST

HEAD_DIM = 64
LANES = 128
VMEM_LIMIT = 48 * 1024 * 1024
EXPERT_VMEM_LIMIT = 56 * 1024 * 1024

HG_CHUNK = 64
GM_CHUNK = 128
GM_TILE_CHUNKS = 4
NSA_KV_GROUPS = 2
NSA_HPG = 4
CMP_BLOCK = 32
CMP_STRIDE = 16
CMP_HIDDEN = 128
SEL_BLOCK = 64
N_SEL = 16
WINDOW = 512
N_GATES = 3
IMP_FORCE = 1e9
IMP_FUTURE = -1e9
NEG_INF = -1e30
N_EXPERTS = 32
TOP_K = 4
SWIGLU_ALPHA = 1.702
SWIGLU_LIMIT = 7.0
EXPERT_BLOCK = 512
SC_ROW_WORDS = 256
SC_WINDOW = 128
ROPE_THETA = 10000.0
LN_EPS = 1e-5
RMS_EPS = 1e-6
V_ROWS = HEAD_DIM + 16


def _params(*sem):
    return pltpu.CompilerParams(dimension_semantics=sem, vmem_limit_bytes=VMEM_LIMIT)


def _dot(a, b):
    return jnp.dot(a, b, preferred_element_type=F32)


def _dot_nt(a, b, precision=None):
    return lax.dot_general(a, b, (((1,), (1,)), ((), ())), precision=precision,
                           preferred_element_type=F32)


def _dot_tn(a, b):
    return lax.dot_general(a, b, (((0,), (0,)), ((), ())), preferred_element_type=F32)


def _sigmoid(x):
    return 1.0 / (1.0 + jnp.exp(-x))


def _gelu(x):
    return 0.5 * x * (1.0 + jnp.tanh(0.7978845608028654 * (x + 0.044715 * x * x * x)))


def _layer_norm(x, w, b):
    mu = jnp.mean(x, axis=-1, keepdims=True)
    xc = x - mu
    var = jnp.mean(xc * xc, axis=-1, keepdims=True)
    return xc * lax.rsqrt(var + LN_EPS) * w + b


def _pack_bf16_pairs(y):
    w = y.shape[1] // 2
    bits = pltpu.bitcast(y.astype(BF16).astype(F32), jnp.uint32)
    return lax.shift_right_logical(bits[:, :w], jnp.uint32(16)) | (bits[:, w:] & jnp.uint32(0xFFFF0000))


def _unpack_bf16_pairs(u):
    lo = pltpu.bitcast(lax.shift_left(u, jnp.uint32(16)), F32)
    hi = pltpu.bitcast(u & jnp.uint32(0xFFFF0000), F32)
    return lo, hi


def _store_word_tables(refs, packed):
    for j, ref in enumerate(refs):
        ref[...] = packed[:, j * SC_ROW_WORDS:(j + 1) * SC_ROW_WORDS]


def _head_mean_sq(o, bd_ones):
    sq = o * o
    hi = sq.astype(BF16)
    lo = (sq - hi.astype(F32)).astype(BF16)
    ones = bd_ones.astype(BF16)
    return (_dot(hi, ones) + _dot(lo, ones)) * (1.0 / HEAD_DIM)


def _rope_kernel(pos_ref, posr_ref, inv_ref, sign_ref, invc_ref, signc_ref, cos_ref, sin_ref, cost_ref, sint_ref):
    ang = pos_ref[...] * inv_ref[...]
    cos_ref[...] = jnp.cos(ang)
    sin_ref[...] = jnp.sin(ang) * sign_ref[...]
    ang_t = invc_ref[...] * posr_ref[...]
    cost_ref[...] = jnp.cos(ang_t)
    sint_ref[...] = jnp.sin(ang_t) * signc_ref[...]


def rope_tables(positions):
    n = positions.size
    tile = min(n, 2048)
    posf = positions.reshape(n).astype(F32)
    pos = jnp.broadcast_to(posf[:, None], (n, HEAD_DIM))
    inv = ROPE_THETA ** (-jnp.arange(0, HEAD_DIM, 2, dtype=F32) / HEAD_DIM)
    inv = jnp.concatenate([inv, inv])
    sign = jnp.concatenate([-jnp.ones((HEAD_DIM // 2,), F32), jnp.ones((HEAD_DIM // 2,), F32)])
    row = pl.BlockSpec((tile, HEAD_DIM), lambda i: (i, 0))
    rowt = pl.BlockSpec((HEAD_DIM, tile), lambda i: (0, i))
    const = pl.BlockSpec((1, HEAD_DIM), lambda i: (0, 0))
    constc = pl.BlockSpec((HEAD_DIM, 1), lambda i: (0, 0))
    return pl.pallas_call(
        _rope_kernel, grid=(n // tile,),
        in_specs=[row, pl.BlockSpec((1, tile), lambda i: (0, i)), const, const, constc, constc],
        out_specs=[row, row, rowt, rowt],
        out_shape=[jax.ShapeDtypeStruct((n, HEAD_DIM), F32)] * 2 + [jax.ShapeDtypeStruct((HEAD_DIM, n), F32)] * 2,
        compiler_params=_params("parallel"), name="rope_tables",
    )(pos, posf.reshape(1, n), inv.reshape(1, HEAD_DIM), sign.reshape(1, HEAD_DIM),
      inv.reshape(HEAD_DIM, 1), sign.reshape(HEAD_DIM, 1))


def _in_proj_kernel(x_ref, w_ref, h_ref):
    h_ref[...] = _dot(x_ref[...].astype(BF16), w_ref[...])


def in_proj(x2d, w_bf16):
    n, d = x2d.shape
    width = w_bf16.shape[1]
    tile = min(n, 512)
    return pl.pallas_call(
        _in_proj_kernel, grid=(n // tile,),
        in_specs=[pl.BlockSpec((tile, d), lambda i: (i, 0)), pl.BlockSpec((d, width), lambda i: (0, 0))],
        out_specs=pl.BlockSpec((tile, width), lambda i: (i, 0)),
        out_shape=jax.ShapeDtypeStruct((n, width), F32),
        compiler_params=_params("parallel"), name="in_proj")(x2d, w_bf16)


HG_LEVELS = (64, 32, 16, 8, 4, 2)
HG_BATCH = 4


def _hgrn_constants():
    c = HG_CHUNK
    t = np.arange(c)
    u = t[None, :]
    rows = [u <= t[:, None], u > t[:, None]]
    masks = [np.eye(c, dtype=bool)]
    for m in HG_LEVELS:
        ref = ((t // m) * m + m // 2 - 1)[:, None]
        second = (t % m >= m // 2)[:, None]
        rows.append(((u > ref) & (u <= t[:, None]) & second) | ((u > t[:, None]) & (u <= ref) & ~second))
        masks.append((t[:, None] // m == t[None, :] // m) & second & (t[None, :] % m < m // 2))
    pmat = np.concatenate(rows, axis=0).astype(np.float32)
    masks = np.stack([np.tile(mk, (1, 4)) for mk in masks]).astype(np.float32)
    return pmat, masks


def _hgrn_kernel(q_ref, f_ref, i_ref, g_ref, lb_ref, nw_ref, pmat_ref, masks_ref, bd_ref, hm_ref,
                 o_ref, state_ref):
    c = HG_CHUNK

    @pl.when(pl.program_id(1) == 0)
    def _():
        state_ref[...] = jnp.zeros_like(state_ref)

    lb = lb_ref[...]
    bd = bd_ref[...]
    hm = hm_ref[...]
    pmat = pmat_ref[...]
    a = jnp.log(lb)
    log1m = jnp.log(1.0 - lb)
    for bi in range(q_ref.shape[0]):
        fr = f_ref[bi]
        hq = q_ref[bi]
        qf = hq * _sigmoid(hq)
        log_sig = jnp.minimum(fr, 0.0) - jnp.log(1.0 + jnp.exp(-jnp.abs(fr)))
        cc = log1m + log_sig
        log_f = jnp.maximum(a, cc) + jnp.log(1.0 + jnp.exp(-jnp.abs(a - cc)))
        kk = (1.0 - lb) * _sigmoid(-fr)
        v = i_ref[bi]
        vb = v.astype(BF16)

        hi = log_f.astype(BF16)
        lo = (log_f - hi.astype(F32)).astype(BF16)
        sums = jnp.minimum(_dot(pmat, hi) + _dot(pmat, lo), 0.0)
        e_all = jnp.exp(sums)
        e_b = e_all[0:c]
        e_rest = e_all[c:2 * c]

        def stacked(x):
            return jnp.concatenate([x * hm[h:h + 1] for h in range(4)], axis=0).astype(BF16)

        att = masks_ref[0] * _dot_nt(qf.astype(BF16), stacked(kk))
        for li in range(len(HG_LEVELS)):
            e_l = e_all[(2 + li) * c:(3 + li) * c]
            att = att + masks_ref[li + 1] * _dot_nt((qf * e_l).astype(BF16), stacked(kk * e_l))
        o = _dot(att.astype(BF16), stacked(v))

        st = state_ref[bi]
        o = o + _dot_nt((qf * e_b).astype(BF16), st.astype(BF16))
        state_ref[bi] = st * e_b[c - 1:c] + bd * _dot_tn(vb, (kk * e_rest).astype(BF16))

        ms = _head_mean_sq(o, bd)
        y = o * lax.rsqrt(ms + RMS_EPS) * nw_ref[...] * _sigmoid(g_ref[bi])
        o_ref[bi] = y.astype(o_ref.dtype)


def hgrn2(h3, lb, norm_w):
    batch, seq, _ = h3.shape
    w = lb.shape[-1]
    c = HG_CHUNK
    nb = math.gcd(batch, HG_BATCH)
    pmat, masks = _hgrn_constants()
    lane_head = np.arange(w) // HEAD_DIM
    bd = (lane_head[:, None] == lane_head[None, :]).astype(np.float32)
    hm = (np.arange(4)[:, None] == lane_head[None, :]).astype(np.float32)

    def col(j):
        return pl.BlockSpec((nb, c, w), lambda b, i, j=j: (b, i, j))

    def const(shape):
        return pl.BlockSpec(shape, lambda b, i: (0,) * len(shape))

    return pl.pallas_call(
        _hgrn_kernel, grid=(batch // nb, seq // c),
        in_specs=[col(0), col(1), col(2), col(3), const((1, w)), const((1, w)),
                  const(pmat.shape), const(masks.shape), const(bd.shape), const(hm.shape)],
        out_specs=pl.BlockSpec((nb, c, w), lambda b, i: (b, i, 0)),
        out_shape=jax.ShapeDtypeStruct((batch, seq, w), BF16),
        scratch_shapes=[pltpu.VMEM((nb, w, w), F32)],
        compiler_params=_params("parallel", "arbitrary"), name="hgrn2",
    )(h3, h3, h3, h3, lb.reshape(1, w), norm_w.reshape(1, w), jnp.asarray(pmat, BF16), jnp.asarray(masks),
      jnp.asarray(bd), jnp.asarray(hm))


def _gmlp_kernel(u_ref, v_ref, lnw_ref, lnb_ref, ws_ref, bias_ref, nw_ref, bd_ref, hm_ref, o_ref):
    c = GM_CHUNK
    groups = ws_ref.shape[0]
    u = _gelu(u_ref[...])
    v = _layer_norm(_gelu(v_ref[...]), lnw_ref[...], lnb_ref[...])
    hm = hm_ref[...]
    bd = bd_ref[...]
    causal = lax.broadcasted_iota(jnp.int32, (c, c), 0) >= lax.broadcasted_iota(jnp.int32, (c, c), 1)
    w_cat = jnp.concatenate([jnp.where(causal, ws_ref[g], 0.0).astype(BF16) for g in range(groups)], axis=1)
    for j in range(u.shape[0] // c):
        rows = slice(j * c, (j + 1) * c)
        v_j = v[rows]
        v_bd = jnp.concatenate([v_j * hm[g:g + 1] for g in range(groups)], axis=0).astype(BF16)
        y = u[rows] * (bias_ref[...] + _dot(w_cat, v_bd))
        ms = _head_mean_sq(y, bd)
        o_ref[rows, :] = (y * lax.rsqrt(ms + RMS_EPS) * nw_ref[...]).astype(o_ref.dtype)


def gmlp(h, ln_w, ln_b, w_s, b_s, norm_w, u_col, v_col):
    n = h.shape[0]
    groups, c, _ = w_s.shape
    w = groups * HEAD_DIM
    lane_head = np.arange(w) // HEAD_DIM
    bd = (lane_head[:, None] == lane_head[None, :]).astype(np.float32)
    hm = (np.arange(groups)[:, None] == lane_head[None, :]).astype(np.float32)
    bias = jnp.repeat(b_s.T, HEAD_DIM, axis=1)

    def const(shape):
        return pl.BlockSpec(shape, lambda i: (0,) * len(shape))

    t = math.gcd(n, GM_TILE_CHUNKS * c)
    return pl.pallas_call(
        _gmlp_kernel, grid=(n // t,),
        in_specs=[pl.BlockSpec((t, w), lambda i: (i, u_col)), pl.BlockSpec((t, w), lambda i: (i, v_col)),
                  const((1, w)), const((1, w)), const(w_s.shape), const((c, w)), const((1, w)),
                  const(bd.shape), const(hm.shape)],
        out_specs=pl.BlockSpec((t, w), lambda i: (i, 0)),
        out_shape=jax.ShapeDtypeStruct((n, w), BF16),
        compiler_params=_params("parallel"), name="gmlp",
    )(h, h, ln_w.reshape(1, w), ln_b.reshape(1, w), w_s, bias, norm_w.reshape(1, w),
      jnp.asarray(bd), jnp.asarray(hm))


def _compress_kernel(u_ref, wtop_ref, wbot_ref, pe_ref, w2_ref, o_ref):
    u = u_ref[...].astype(BF16)
    wtop = wtop_ref[...]
    wbot = wbot_ref[...]
    pe = pe_ref[...].astype(BF16)
    const = _dot(pe[0:1], wtop) + _dot(pe[1:2], wbot)
    p = _dot(u, wtop)
    q = _dot(u, wbot)
    q_next = jnp.concatenate([q[1:], jnp.zeros_like(q[0:1])], axis=0)
    hid = _gelu(p + q_next + const)
    o_ref[...] = _dot(hid.astype(BF16), w2_ref[...]).astype(o_ref.dtype)


def compress(kv, pe, w1, w2, batch, seq):
    g = NSA_KV_GROUPS
    half = CMP_STRIDE
    units = seq // half
    gw = g * HEAD_DIM
    u = kv.reshape(batch * units, half * gw)
    eye = jnp.eye(g, dtype=F32)
    w1r = w1.reshape(2, half, HEAD_DIM, CMP_HIDDEN)
    wbd = jnp.einsum('hjdn,gk->hjgdkn', w1r, eye).reshape(2, half * gw, g * CMP_HIDDEN).astype(BF16)
    w2bd = jnp.einsum('nd,gk->gnkd', w2, eye).reshape(g * CMP_HIDDEN, gw).astype(BF16)
    pe2 = jnp.broadcast_to(pe.reshape(2, half, 1, HEAD_DIM), (2, half, g, HEAD_DIM)).reshape(2, half * gw)

    def const(shape):
        return pl.BlockSpec(shape, lambda b: (0,) * len(shape))

    out = pl.pallas_call(
        _compress_kernel, grid=(batch,),
        in_specs=[pl.BlockSpec((units, half * gw), lambda b: (b, 0)),
                  const(wbd.shape[1:]), const(wbd.shape[1:]), const(pe2.shape), const(w2bd.shape)],
        out_specs=pl.BlockSpec((units, gw), lambda b: (b, 0)),
        out_shape=jax.ShapeDtypeStruct((batch * units, gw), BF16),
        compiler_params=_params("parallel"), name="nsa_compress",
    )(u, wbd[0], wbd[1], pe2, w2bd)
    return out.reshape(batch, units, gw)


def _rot_half_pairs(x):
    lane = lax.broadcasted_iota(jnp.int32, x.shape, 1)
    fwd = pltpu.roll(x, 32, axis=1)
    bwd = pltpu.roll(x, 96, axis=1)
    return jnp.where((lane % HEAD_DIM) < HEAD_DIM // 2, bwd, fwd)


def _kprep_kernel(ks_ref, vs_ref, kw_ref, vw_ref, cos_ref, sin_ref, ksa_ref, vso_ref, kwo_ref, vwo_ref):
    t = ks_ref.shape[0]
    cos = cos_ref[...]
    sin = sin_ref[...]
    cos2 = jnp.concatenate([cos, cos], axis=1)
    sin2 = jnp.concatenate([sin, sin], axis=1)
    ks = ks_ref[...]
    kw = kw_ref[...]
    ks_r = ks * cos2 + _rot_half_pairs(ks) * sin2
    kw_r = kw * cos2 + _rot_half_pairs(kw) * sin2
    pos = pl.program_id(1) * t + lax.broadcasted_iota(jnp.int32, (t, HEAD_DIM), 0)
    onehot = (pos // SEL_BLOCK == lax.broadcasted_iota(jnp.int32, (t, HEAD_DIM), 1)).astype(F32)
    vs_t = vs_ref[...].T
    vw_t = vw_ref[...].T
    tail = (lax.broadcasted_iota(jnp.int32, (V_ROWS - HEAD_DIM, t), 0) == 0).astype(F32)
    for g in range(NSA_KV_GROUPS):
        sl = slice(g * HEAD_DIM, (g + 1) * HEAD_DIM)
        ksa_ref[g] = jnp.concatenate([ks_r[:, sl], onehot], axis=1).astype(BF16)
        vso_ref[g] = jnp.concatenate([vs_t[sl], tail], axis=0).astype(BF16)
        kwo_ref[g] = kw_r[:, sl].astype(BF16)
        vwo_ref[g] = jnp.concatenate([vw_t[sl], tail], axis=0).astype(BF16)


def nsa_kprep(h, cosf, sinf, batch, seq, col0):
    g = NSA_KV_GROUPS
    t = min(seq, 512)
    nt = seq // t

    def col(j):
        return pl.BlockSpec((t, LANES), lambda b, i, j=j: (b * nt + i, col0 + j))

    tab = pl.BlockSpec((t, HEAD_DIM), lambda b, i: (b * nt + i, 0))

    def out(wd):
        return pl.BlockSpec((None, g, t, wd), lambda b, i: (b, 0, i, 0))

    out_t = pl.BlockSpec((None, g, V_ROWS, t), lambda b, i: (b, 0, 0, i))
    k_shape = jax.ShapeDtypeStruct((batch, g, seq, HEAD_DIM), BF16)
    v_shape = jax.ShapeDtypeStruct((batch, g, V_ROWS, seq), BF16)
    return pl.pallas_call(
        _kprep_kernel, grid=(batch, nt),
        in_specs=[col(0), col(1), col(2), col(3), tab, tab],
        out_specs=[out(2 * HEAD_DIM), out_t, out(HEAD_DIM), out_t],
        out_shape=[jax.ShapeDtypeStruct((batch, g, seq, 2 * HEAD_DIM), BF16), v_shape, k_shape, v_shape],
        compiler_params=_params("parallel", "parallel"), name="nsa_kprep",
    )(h, h, h, h, cosf, sinf)


def _nsa_kernel(hq_ref, gate_ref, cos_ref, sin_ref, kc_ref, vc_ref, ksa_ref, vs_ref, kw_ref, vw_ref,
                ovl_ref, nw_ref, o_ref, imp_ref, *, tq, tk, n_sb):
    qi = pl.program_id(1)
    hpg = NSA_HPG
    groups = NSA_KV_GROUPS
    rows = hpg * tq
    t0 = qi * tq
    scale = 1.0 / math.sqrt(HEAD_DIM)
    half = HEAD_DIM // 2

    hq_t = hq_ref[...].T
    cos = cos_ref[...]
    sin = sin_ref[...]
    q_raw, q_rot = [], []
    for g in range(groups):
        raw_g, rot_g = [], []
        for h in range(hpg):
            r0 = (g * hpg + h) * HEAD_DIM
            qh = hq_t[r0:r0 + HEAD_DIM]
            swapped = jnp.concatenate([qh[half:], qh[:half]], axis=0)
            raw_g.append(qh * scale)
            rot_g.append((qh * cos + swapped * sin) * scale)
        q_raw.append(jnp.concatenate(raw_g, axis=1).astype(BF16))
        q_rot.append(jnp.concatenate(rot_g, axis=1))

    tpos = t0 + lax.broadcasted_iota(jnp.int32, (1, tq), 1)
    tpos_r = jnp.concatenate([tpos] * hpg, axis=1)

    def flash_step(s, v_t, m, acc):
        m_new = jnp.maximum(m, jnp.max(s, axis=0, keepdims=True))
        alpha = jnp.exp(m - m_new)
        p = jnp.exp(s - m_new).astype(BF16)
        return m_new, alpha * acc + _dot(v_t, p)

    init = (jnp.full((1, rows), NEG_INF, F32), jnp.zeros((V_ROWS, rows), F32))

    wk = WINDOW + tq
    kw0 = pl.multiple_of(jnp.maximum(t0 - WINDOW, 0), tq)
    kpos_w = kw0 + lax.broadcasted_iota(jnp.int32, (wk, 1), 0)
    mask_w = (kpos_w <= tpos_r) & (kpos_w > tpos_r - WINDOW)
    n_pad = jnp.maximum(WINDOW - 1 - tpos_r, 0).astype(F32)
    n_cmp = kc_ref.shape[1]
    cmp_end = lax.broadcasted_iota(jnp.int32, (n_cmp, 1), 0) * CMP_STRIDE + (CMP_BLOCK - 1)
    mask_c = cmp_end <= tpos_r
    blk = lax.broadcasted_iota(jnp.int32, (n_sb, tq), 0)
    cur = (t0 + lax.broadcasted_iota(jnp.int32, (n_sb, tq), 1)) // SEL_BLOCK
    o_w, o_c, imps = [], [], []
    for g in range(groups):
        s = _dot(kw_ref[g, pl.ds(kw0, wk), :], q_rot[g].astype(BF16))
        m_w, acc_w = flash_step(jnp.where(mask_w, s, NEG_INF), vw_ref[g, :, pl.ds(kw0, wk)], *init)
        m_f = jnp.where(n_pad > 0.0, jnp.maximum(m_w, 0.0), m_w)
        a_w = jnp.exp(m_w - m_f)
        o_w.append(acc_w[0:HEAD_DIM] * (a_w / (acc_w[HEAD_DIM:HEAD_DIM + 1] * a_w + n_pad * jnp.exp(-m_f))))

        s_c = jnp.where(mask_c, _dot(kc_ref[g], q_raw[g]), NEG_INF)
        e_c = jnp.exp(s_c - jnp.max(s_c, axis=0, keepdims=True))
        p_c = jnp.where(mask_c, e_c * (1.0 / jnp.sum(e_c, axis=0, keepdims=True)), 0.0)
        o_c.append(_dot(vc_ref[g], p_c.astype(BF16)))

        p_sum = p_c[:, 0:tq]
        for h in range(1, hpg):
            p_sum = p_sum + p_c[:, h * tq:(h + 1) * tq]
        imp = jnp.dot(ovl_ref[...], p_sum, precision=HIGHEST, preferred_element_type=F32)
        imp = jnp.where(blk > cur, IMP_FUTURE, imp)
        imp = jnp.where((blk == 0) | (blk == cur) | (blk == cur - 1), IMP_FORCE, imp)
        imp_ref[g] = imp
        imps.append(imp)

    def rank_body(j, ranks):
        out = []
        for g in range(groups):
            row = imp_ref[g, pl.ds(j, 1), :]
            ge = jnp.where(row >= imps[g], 1, 0)
            gt = jnp.where(row > imps[g], 1, 0)
            out.append(ranks[g] + jnp.where(j < blk, ge, gt))
        return tuple(out)

    ranks = lax.fori_loop(0, n_sb, rank_body, (jnp.zeros((n_sb, tq), jnp.int32),) * groups, unroll=8)
    q_aug = []
    for g in range(groups):
        sel_bias = jnp.where(ranks[g] < min(N_SEL, n_sb), 0.0, NEG_INF)
        if n_sb < HEAD_DIM:
            sel_bias = jnp.concatenate([sel_bias, jnp.zeros((HEAD_DIM - n_sb, tq), F32)], axis=0)
        q_aug.append(jnp.concatenate([q_rot[g], jnp.concatenate([sel_bias] * hpg, axis=1)], axis=0).astype(BF16))

    def scores(g, kt):
        k0 = pl.multiple_of(kt * tk, tk)
        return _dot(ksa_ref[g, pl.ds(k0, tk), :], q_aug[g])

    def sel_body(kt, carry):
        k0 = pl.multiple_of(kt * tk, tk)
        s = [scores(g, kt) for g in range(groups)]
        m_new = [jnp.maximum(carry[g][0], jnp.max(s[g], axis=0, keepdims=True)) for g in range(groups)]
        p = [jnp.exp(s[g] - m_new[g]).astype(BF16) for g in range(groups)]
        return tuple((m_new[g], jnp.exp(carry[g][0] - m_new[g]) * carry[g][1] + _dot(vs_ref[g, :, pl.ds(k0, tk)], p[g]))
                     for g in range(groups))

    n_full = t0 // tk
    carry = lax.fori_loop(0, n_full, sel_body, (init,) * groups)
    k0 = pl.multiple_of(n_full * tk, tk)
    kpos = k0 + lax.broadcasted_iota(jnp.int32, (tk, 1), 0)
    mask_s = kpos <= tpos_r

    gates = _sigmoid(gate_ref[...].T)
    nw = nw_ref[...]
    outs = []
    for g in range(groups):
        _, acc_s = flash_step(jnp.where(mask_s, scores(g, n_full), NEG_INF), vs_ref[g, :, pl.ds(k0, tk)], *carry[g])
        o_s = acc_s[0:HEAD_DIM] * (1.0 / acc_s[HEAD_DIM:HEAD_DIM + 1])
        for h in range(hpg):
            sl = slice(h * tq, (h + 1) * tq)
            r = (g * hpg + h) * N_GATES
            o = gates[r:r + 1] * o_c[g][:, sl] + gates[r + 1:r + 2] * o_s[:, sl] + gates[r + 2:r + 3] * o_w[g][:, sl]
            ms = jnp.mean(o * o, axis=0, keepdims=True)
            outs.append(o * lax.rsqrt(ms + RMS_EPS) * nw[:, g * hpg + h:g * hpg + h + 1])
    o_ref[...] = jnp.concatenate(outs, axis=0).T.astype(o_ref.dtype)


def nsa_attention(h, cos_t, sin_t, kc, vc_t, ksa, vs_t, kw, vw_t, norm_w, batch, seq, q_col0, gate_col):
    g, hpg = NSA_KV_GROUPS, NSA_HPG
    tq = min(seq, 256)
    nq = seq // tq
    n_sb = seq // SEL_BLOCK
    assert n_sb <= HEAD_DIM, "selection-block one-hot shares the 64 spare key lanes"
    n_cmp = kc.shape[2]
    units = np.arange(n_cmp)[:, None] + np.arange(CMP_BLOCK // CMP_STRIDE)[None, :]
    ovl = np.zeros((n_cmp, n_sb), np.float32)
    for c in range((seq - CMP_BLOCK) // CMP_STRIDE + 1):
        for u in units[c]:
            ovl[c, u // (SEL_BLOCK // CMP_STRIDE)] += 1.0
    ovl_t = jnp.asarray(ovl.T)

    def per_b(shape):
        return pl.BlockSpec((None, g) + shape, lambda b, qi: (b, 0, 0, 0))

    width = g * hpg * HEAD_DIM
    tab = pl.BlockSpec((HEAD_DIM, tq), lambda b, qi: (0, b * nq + qi))
    tk = min(seq, 512)
    assert seq >= WINDOW + tq and seq % tk == 0 and tk % tq == 0
    kern = functools.partial(_nsa_kernel, tq=tq, tk=tk, n_sb=n_sb)
    return pl.pallas_call(
        kern, grid=(batch, nq),
        in_specs=[pl.BlockSpec((tq, width), lambda b, qi: (b * nq + qi, q_col0)),
                  pl.BlockSpec((tq, LANES), lambda b, qi: (b * nq + qi, gate_col)),
                  tab, tab,
                  per_b((n_cmp, HEAD_DIM)), per_b((HEAD_DIM, n_cmp)),
                  per_b((seq, 2 * HEAD_DIM)), per_b((V_ROWS, seq)),
                  per_b((seq, HEAD_DIM)), per_b((V_ROWS, seq)),
                  pl.BlockSpec((n_sb, n_cmp), lambda b, qi: (0, 0)),
                  pl.BlockSpec((HEAD_DIM, g * hpg), lambda b, qi: (0, 0))],
        out_specs=pl.BlockSpec((tq, width), lambda b, qi: (b * nq + qi, 0)),
        out_shape=jax.ShapeDtypeStruct((batch * seq, width), BF16),
        scratch_shapes=[pltpu.VMEM((g, n_sb, tq), F32)],
        compiler_params=_params("parallel", "arbitrary"), name="nsa_attention",
    )(h, h, cos_t, sin_t, kc, vc_t, ksa, vs_t, kw, vw_t, ovl_t, norm_w.reshape(g * hpg, HEAD_DIM).T)


def _out_proj_kernel(x_ref, yhg_ref, ygm_ref, ynsa_ref, whg_ref, wgm_ref, wnsa_ref, lnw_ref, lnb_ref,
                     o_ref, oa_ref, ob_ref, *, alpha):
    mix = (_dot(yhg_ref[...], whg_ref[...]) + _dot(ygm_ref[...], wgm_ref[...])
           + _dot(ynsa_ref[...], wnsa_ref[...]))
    y = _layer_norm(alpha * x_ref[...] + mix, lnw_ref[...], lnb_ref[...])
    o_ref[...] = y
    _store_word_tables((oa_ref, ob_ref), _pack_bf16_pairs(y))


def out_proj_ln(x2d, y_hg, y_gm, y_nsa, w_out, ln_w, ln_b, alpha):
    n, d = x2d.shape
    w1, w2 = y_hg.shape[1], y_hg.shape[1] + y_gm.shape[1]
    whg = w_out[:w1].astype(BF16)
    wgm = w_out[w1:w2].astype(BF16)
    wnsa = w_out[w2:].astype(BF16)
    t = min(n, 512)

    def row(wd):
        return pl.BlockSpec((t, wd), lambda i: (i, 0))

    def const(shape):
        return pl.BlockSpec(shape, lambda i: (0,) * len(shape))

    kern = functools.partial(_out_proj_kernel, alpha=alpha)
    return pl.pallas_call(
        kern, grid=(n // t,),
        in_specs=[row(d), row(y_hg.shape[1]), row(y_gm.shape[1]), row(y_nsa.shape[1]),
                  const(whg.shape), const(wgm.shape), const(wnsa.shape), const((1, d)), const((1, d))],
        out_specs=[row(d), row(SC_ROW_WORDS), row(SC_ROW_WORDS)],
        out_shape=[jax.ShapeDtypeStruct((n, d), F32)] + [jax.ShapeDtypeStruct((n, SC_ROW_WORDS), jnp.uint32)] * 2,
        compiler_params=_params("parallel"), name="out_proj_ln",
    )(x2d, y_hg, y_gm, y_nsa, whg, wgm, wnsa, ln_w.reshape(1, d), ln_b.reshape(1, d))


def _router_kernel(x_ref, w_ref, b_ref, e_ref, p_ref, r_ref, cnt_ref, carry_ref):
    t = x_ref.shape[0]

    @pl.when(pl.program_id(0) == 0)
    def _():
        carry_ref[...] = jnp.zeros_like(carry_ref)

    x = x_ref[...]
    x_hi = x.astype(BF16)
    x_lo = (x - x_hi.astype(F32)).astype(BF16)
    w = w_ref[...]
    w_hi = w.astype(BF16)
    w_lo = (w - w_hi.astype(F32)).astype(BF16)
    logits = _dot(x_hi, w_hi) + (_dot(x_lo, w_hi) + _dot(x_hi, w_lo)) + b_ref[...]
    lane = lax.broadcasted_iota(jnp.int32, logits.shape, 1)
    work = logits
    vals, idxs = [], []
    sel = jnp.zeros(logits.shape, F32)
    for _ in range(TOP_K):
        m = jnp.max(work, axis=-1, keepdims=True)
        idx = jnp.min(jnp.where(work == m, lane, LANES), axis=-1, keepdims=True)
        hit = lane == idx
        sel = jnp.where(hit, 1.0, sel)
        work = jnp.where(hit, -jnp.inf, work)
        vals.append(m)
        idxs.append(idx)
    exps = [jnp.exp(v - vals[0]) for v in vals]
    den = exps[0] + exps[1] + exps[2] + exps[3]
    strict = (lax.broadcasted_iota(jnp.int32, (t, t), 0) > lax.broadcasted_iota(jnp.int32, (t, t), 1))
    before = _dot(strict.astype(BF16), sel.astype(BF16)) + carry_ref[...]
    ranks = [jnp.sum(jnp.where(lane == idx, before, 0.0), axis=-1, keepdims=True) for idx in idxs]
    kcol = lax.broadcasted_iota(jnp.int32, (t, TOP_K), 1)
    e_out = jnp.zeros((t, TOP_K), jnp.int32)
    p_out = jnp.zeros((t, TOP_K), F32)
    r_out = jnp.zeros((t, TOP_K), jnp.int32)
    for k in range(TOP_K):
        e_out = jnp.where(kcol == k, idxs[k], e_out)
        p_out = jnp.where(kcol == k, exps[k] / den, p_out)
        r_out = jnp.where(kcol == k, ranks[k].astype(jnp.int32), r_out)
    e_ref[...] = e_out
    p_ref[...] = p_out
    r_ref[...] = r_out
    carry_ref[...] = carry_ref[...] + jnp.sum(sel, axis=0, keepdims=True)
    cnt_ref[...] = carry_ref[...].astype(jnp.int32)


def moe_router(x2d, router_w, router_b):
    n, d = x2d.shape
    e = router_w.shape[1]
    t = min(n, 512)
    w = jnp.zeros((d, LANES), F32).at[:, :e].set(router_w)
    b = jnp.full((1, LANES), NEG_INF, F32).at[0, :e].set(router_b)
    row4 = pl.BlockSpec((t, TOP_K), lambda i: (i, 0))
    top_e, top_p, rank, counts = pl.pallas_call(
        _router_kernel, grid=(n // t,),
        in_specs=[pl.BlockSpec((t, d), lambda i: (i, 0)), pl.BlockSpec((d, LANES), lambda i: (0, 0)),
                  pl.BlockSpec((1, LANES), lambda i: (0, 0))],
        out_specs=[row4, row4, row4, pl.BlockSpec((1, LANES), lambda i: (0, 0))],
        out_shape=[jax.ShapeDtypeStruct((n, TOP_K), jnp.int32), jax.ShapeDtypeStruct((n, TOP_K), F32),
                   jax.ShapeDtypeStruct((n, TOP_K), jnp.int32), jax.ShapeDtypeStruct((1, LANES), jnp.int32)],
        scratch_shapes=[pltpu.VMEM((1, LANES), F32)],
        compiler_params=_params("arbitrary"), name="moe_router",
    )(x2d, w, b)
    return top_e, top_p, rank, counts[0, :e]


def _expert_kernel(be_ref, valid_ref, xa_ref, xb_ref, wu_ref, bu_ref, wd_ref, bd_ref, oa_ref, ob_ref,
                   wu_bf, wd_bf):
    i = pl.program_id(0)
    f = wd_ref.shape[0]
    n_used = be_ref[pl.num_programs(0)]

    @pl.when((i == 0) | (be_ref[i] != be_ref[jnp.maximum(i - 1, 0)]))
    def _():
        wu_bf[...] = wu_ref[...].astype(BF16)
        wd_bf[...] = wd_ref[...].astype(BF16)

    @pl.when(i < n_used)
    def _():
        packed = jnp.concatenate([xa_ref[...], xb_ref[...]], axis=1)
        live = lax.broadcasted_iota(jnp.int32, packed.shape, 0) < valid_ref[i]
        x_lo, x_hi = _unpack_bf16_pairs(jnp.where(live, packed, jnp.uint32(0)))
        x = jnp.concatenate([x_lo.astype(BF16), x_hi.astype(BF16)], axis=1)
        hcat = _dot(x, wu_bf[...]) + bu_ref[...]
        glu = jnp.minimum(hcat[:, :f], SWIGLU_LIMIT)
        lin = jnp.clip(hcat[:, f:], -SWIGLU_LIMIT, SWIGLU_LIMIT)
        act = glu * _sigmoid(SWIGLU_ALPHA * glu) * (lin + 1.0)
        _store_word_tables((oa_ref, ob_ref), _pack_bf16_pairs(_dot(act.astype(BF16), wd_bf[...]) + bd_ref[...]))

    @pl.when(i >= n_used)
    def _():
        oa_ref[...] = jnp.zeros_like(oa_ref)
        ob_ref[...] = jnp.zeros_like(ob_ref)


def moe_experts(xa, xb, block_e, n_used, block_valid, w_up, b_up, w_down, b_down, layer):
    rows = xa.shape[0]
    _, e, d, f2 = w_up.shape
    f = f2 // 2
    nb = rows // EXPERT_BLOCK
    words = pl.BlockSpec((EXPERT_BLOCK, SC_ROW_WORDS), lambda i, be, nv: (i, 0))
    grid_spec = pltpu.PrefetchScalarGridSpec(
        num_scalar_prefetch=2, grid=(nb,),
        in_specs=[words, words,
                  pl.BlockSpec((None, None, d, f2), lambda i, be, nv: (layer, be[i], 0, 0)),
                  pl.BlockSpec((None, None, 1, f2), lambda i, be, nv: (layer, be[i], 0, 0)),
                  pl.BlockSpec((None, None, f, d), lambda i, be, nv: (layer, be[i], 0, 0)),
                  pl.BlockSpec((None, None, 1, d), lambda i, be, nv: (layer, be[i], 0, 0))],
        out_specs=[words, words],
        scratch_shapes=[pltpu.VMEM((d, f2), BF16), pltpu.VMEM((f, d), BF16)])
    depth = w_up.shape[0]
    return pl.pallas_call(
        _expert_kernel, grid_spec=grid_spec,
        out_shape=[jax.ShapeDtypeStruct((rows, SC_ROW_WORDS), jnp.uint32)] * 2,
        compiler_params=pltpu.CompilerParams(dimension_semantics=("arbitrary",), vmem_limit_bytes=EXPERT_VMEM_LIMIT),
        name="moe_experts",
    )(jnp.concatenate([block_e, n_used.reshape(1)]), block_valid, xa, xb, w_up, b_up.reshape(depth, e, 1, f2),
      w_down, b_down.reshape(depth, e, 1, d))


def _combine_kernel(x_ref, ya_ref, yb_ref, p_ref, lnw_ref, lnb_ref, o_ref, *, alpha):
    p = p_ref[...]
    moe = jnp.zeros(x_ref.shape, F32)
    for k in range(TOP_K):
        y_lo, y_hi = _unpack_bf16_pairs(jnp.concatenate([ya_ref[k], yb_ref[k]], axis=1))
        moe = moe + p[:, k:k + 1] * jnp.concatenate([y_lo, y_hi], axis=1)
    o_ref[...] = _layer_norm(alpha * x_ref[...] + moe, lnw_ref[...], lnb_ref[...])


def combine_ln(x2d, ya, yb, top_p, ln_w, ln_b, alpha):
    n, d = x2d.shape
    t = min(n, 256)
    kern = functools.partial(_combine_kernel, alpha=alpha)
    words = pl.BlockSpec((TOP_K, t, SC_ROW_WORDS), lambda i: (0, i, 0))
    return pl.pallas_call(
        kern, grid=(n // t,),
        in_specs=[pl.BlockSpec((t, d), lambda i: (i, 0)), words, words,
                  pl.BlockSpec((t, TOP_K), lambda i: (i, 0)),
                  pl.BlockSpec((1, d), lambda i: (0, 0)), pl.BlockSpec((1, d), lambda i: (0, 0))],
        out_specs=pl.BlockSpec((t, d), lambda i: (i, 0)),
        out_shape=jax.ShapeDtypeStruct((n, d), F32),
        compiler_params=_params("parallel"), name="moe_combine_ln",
    )(x2d, ya, yb, top_p, ln_w.reshape(1, d), ln_b.reshape(1, d))


def _sc_mesh():
    return plsc.VectorSubcoreMesh(core_axis_name="core", subcore_axis_name="subcore")


def sc_gather_rows(tables, idx):
    r = idx.shape[0]
    nt = len(tables)
    out = jax.ShapeDtypeStruct((r, SC_ROW_WORDS), tables[0].dtype)

    @pl.kernel(out_type=(out,) * nt, mesh=_sc_mesh(), name="sc_gather_rows")
    def gather(*refs):
        x_hbm, i_hbm, o_hbm = refs[:nt], refs[nt], refs[nt + 1:]
        for j in range(nt):
            def body(i_vmem, o_vmem, table=x_hbm[j]):
                pltpu.sync_copy(table.at[i_vmem.at[0]], o_vmem)

            pltpu.emit_pipeline(
                body, grid=(r // SC_WINDOW,),
                in_specs=[pl.BlockSpec((1, SC_WINDOW), lambda i: (0, i))],
                out_specs=[pl.BlockSpec((SC_WINDOW, SC_ROW_WORDS), lambda i: (i, 0))],
                core_axis_name=("core", "subcore"), dimension_semantics=(pltpu.PARALLEL,),
            )(i_hbm, o_hbm[j])

    return gather(*tables, idx.reshape(1, r))


def sc_scatter_rows(tables, dest_t, n_rows):
    n = tables[0].shape[0]
    nt = len(tables)
    copies = dest_t.shape[0]
    out = jax.ShapeDtypeStruct((n_rows, SC_ROW_WORDS), tables[0].dtype)

    @pl.kernel(out_type=(out,) * nt, mesh=_sc_mesh(), scratch_types=[], name="sc_scatter_rows")
    def scatter(*refs):
        x_hbm, i_hbm, o_hbm = refs[:nt], refs[nt], refs[nt + 1:]
        for j in range(nt):
            def body(x_vmem, i_vmem, out_j=o_hbm[j]):
                for k in range(copies):
                    pltpu.sync_copy(x_vmem, out_j.at[i_vmem.at[k]])

            pltpu.emit_pipeline(
                body, grid=(n // SC_WINDOW,),
                in_specs=[pl.BlockSpec((SC_WINDOW, SC_ROW_WORDS), lambda i: (i, 0)),
                          pl.BlockSpec((copies, SC_WINDOW), lambda i: (0, i))],
                out_specs=[],
                core_axis_name=("core", "subcore"), dimension_semantics=(pltpu.PARALLEL,),
            )(x_hbm[j], i_hbm)

    return scatter(*tables, dest_t)


def moe_ffn_ln(x_f32, x_packed, router_w, router_b, w_up, b_up, w_down, b_down, layer, ln_w, ln_b, alpha):
    n, d = x_f32.shape
    top_e, top_p, rank, counts = moe_router(x_f32, router_w, router_b)
    padded = (counts + EXPERT_BLOCK - 1) // EXPERT_BLOCK * EXPERT_BLOCK
    pad_end = jnp.cumsum(padded)
    pad_start = pad_end - padded
    n_assign = n * TOP_K
    n_blocks = -(-(n_assign + N_EXPERTS * (EXPERT_BLOCK - 1)) // EXPERT_BLOCK)
    dest_t = (pad_start[top_e] + rank).T
    block_first = jnp.arange(n_blocks, dtype=jnp.int32) * EXPERT_BLOCK
    block_e = jnp.clip(jnp.sum((pad_end[None, :] <= block_first[:, None]).astype(jnp.int32), axis=1),
                       0, N_EXPERTS - 1)
    block_valid = jnp.clip(counts[block_e] - (block_first - pad_start[block_e]), 0, EXPERT_BLOCK)
    n_used = (pad_end[-1] // EXPERT_BLOCK).astype(jnp.int32)
    xa, xb = sc_scatter_rows(x_packed, dest_t, n_blocks * EXPERT_BLOCK)
    ya, yb = moe_experts(xa, xb, block_e, n_used, block_valid.astype(jnp.int32), w_up, b_up, w_down, b_down,
                         layer)
    ya, yb = sc_gather_rows((ya, yb), dest_t.reshape(-1))
    return combine_ln(x_f32, ya.reshape(TOP_K, n, SC_ROW_WORDS), yb.reshape(TOP_K, n, SC_ROW_WORDS), top_p,
                      ln_w, ln_b, alpha)


def kernel(x, positions, w_in, hg_lower_bounds, hg_norm_w, gm_ln_w, gm_ln_b, gm_spatial_w, gm_spatial_b, gm_norm_w, nsa_cmp_pe, nsa_cmp_w1, nsa_cmp_w2, nsa_norm_w, w_out, ln1_w, ln1_b, router_w, router_b, exp_w_up, exp_b_up, exp_w_down, exp_b_down, ln2_w, ln2_b):
    batch, seq, d = x.shape
    depth = w_in.shape[0]
    n = batch * seq
    alpha = (2 * depth) ** 0.25
    hg_w = hg_norm_w.shape[1]
    gm_w = gm_norm_w.shape[1]
    nsa_w = nsa_norm_w.shape[1]
    kv_w = NSA_KV_GROUPS * HEAD_DIM
    in_width = w_in.shape[2]
    off_gm = 4 * hg_w
    off_q = off_gm + 2 * gm_w
    off_kv = off_q + nsa_w
    off_gate = off_kv + 6 * kv_w
    width_pad = -(-in_width // LANES) * LANES

    cosf, sinf, cos_t, sin_t = rope_tables(positions)
    lb_all = jnp.cumsum(jax.nn.softmax(hg_lower_bounds.astype(F32), axis=0), axis=0)
    lb_all = lb_all - lb_all[0:1]

    x2d = x.reshape(n, d)
    for l in range(depth):
        w_l = jnp.pad(w_in[l], ((0, 0), (0, width_pad - in_width))).astype(BF16)
        h = in_proj(x2d, w_l)
        h3 = h.reshape(batch, seq, width_pad)
        y_hg = hgrn2(h3, lb_all[l], hg_norm_w[l]).reshape(n, hg_w)
        y_gm = gmlp(h, gm_ln_w[l], gm_ln_b[l], gm_spatial_w[l], gm_spatial_b[l], gm_norm_w[l],
                    off_gm // gm_w, off_gm // gm_w + 1)
        kc = compress(h3[:, :, off_kv:off_kv + kv_w], nsa_cmp_pe[l, 0], nsa_cmp_w1[l, 0], nsa_cmp_w2[l, 0], batch, seq)
        vc = compress(h3[:, :, off_kv + kv_w:off_kv + 2 * kv_w], nsa_cmp_pe[l, 1], nsa_cmp_w1[l, 1],
                      nsa_cmp_w2[l, 1], batch, seq)
        n_cmp = kc.shape[1]
        kc = kc.reshape(batch, n_cmp, NSA_KV_GROUPS, HEAD_DIM).transpose(0, 2, 1, 3)
        vc_t = vc.reshape(batch, n_cmp, NSA_KV_GROUPS, HEAD_DIM).transpose(0, 2, 3, 1)
        ksa, vs_t, kw, vw_t = nsa_kprep(h, cosf, sinf, batch, seq, (off_kv + 2 * kv_w) // LANES)
        y_nsa = nsa_attention(h, cos_t, sin_t, kc, vc_t, ksa, vs_t, kw, vw_t, nsa_norm_w[l], batch, seq,
                              off_q // nsa_w, off_gate // LANES)
        x1, x1a, x1b = out_proj_ln(x2d, y_hg, y_gm, y_nsa, w_out[l], ln1_w[l], ln1_b[l], alpha)
        x2d = moe_ffn_ln(x1, (x1a, x1b), router_w[l], router_b[l], exp_w_up, exp_b_up, exp_w_down, exp_b_down, l,
                         ln2_w[l], ln2_b[l], alpha)
    return x2d.reshape(batch, seq, d)
```

```python
import functools
import math

import numpy as np
import jax
import jax.numpy as jnp
from jax import lax
from jax.experimental import pallas as pl
from jax.experimental.pallas import tpu as pltpu
from jax.experimental.pallas import tpu_sc as plsc

F32 = jnp.float32
BF16 = jnp.bfloat16
HIGHEST = lax.Precision.HIGHEST

HEAD_DIM = 64
LANES = 128
VMEM_LIMIT = 48 * 1024 * 1024
EXPERT_VMEM_LIMIT = 56 * 1024 * 1024

HG_CHUNK = 64
GM_CHUNK = 128
GM_TILE_CHUNKS = 4
NSA_KV_GROUPS = 2
NSA_HPG = 4
CMP_BLOCK = 32
CMP_STRIDE = 16
CMP_HIDDEN = 128
SEL_BLOCK = 64
N_SEL = 16
WINDOW = 512
N_GATES = 3
IMP_FORCE = 1e9
FORCE_KEY = int(np.float32(IMP_FORCE).view(np.int32))
NEG_INF = -1e30
N_EXPERTS = 32
TOP_K = 4
SWIGLU_ALPHA = 1.702
SWIGLU_LIMIT = 7.0
EXPERT_BLOCK = 512
SC_ROW_WORDS = 256
SC_WINDOW = 128
ROPE_THETA = 10000.0
LN_EPS = 1e-5
RMS_EPS = 1e-6
V_ROWS = HEAD_DIM + 16


def _params(*sem):
    return pltpu.CompilerParams(dimension_semantics=sem, vmem_limit_bytes=VMEM_LIMIT)


def _dot(a, b):
    return jnp.dot(a, b, preferred_element_type=F32)


def _dot_nt(a, b, precision=None):
    return lax.dot_general(a, b, (((1,), (1,)), ((), ())), precision=precision,
                           preferred_element_type=F32)


def _dot_tn(a, b):
    return lax.dot_general(a, b, (((0,), (0,)), ((), ())), preferred_element_type=F32)


def _sigmoid(x):
    return 1.0 / (1.0 + jnp.exp(-x))


def _gelu(x):
    return 0.5 * x * (1.0 + jnp.tanh(0.7978845608028654 * (x + 0.044715 * x * x * x)))


def _layer_norm(x, w, b):
    mu = jnp.mean(x, axis=-1, keepdims=True)
    xc = x - mu
    var = jnp.mean(xc * xc, axis=-1, keepdims=True)
    return xc * lax.rsqrt(var + LN_EPS) * w + b


def _pack_bf16_pairs(y):
    w = y.shape[1] // 2
    bits = pltpu.bitcast(y.astype(BF16).astype(F32), jnp.uint32)
    return lax.shift_right_logical(bits[:, :w], jnp.uint32(16)) | (bits[:, w:] & jnp.uint32(0xFFFF0000))


def _unpack_bf16_pairs(u):
    lo = pltpu.bitcast(lax.shift_left(u, jnp.uint32(16)), F32)
    hi = pltpu.bitcast(u & jnp.uint32(0xFFFF0000), F32)
    return lo, hi


def _store_word_tables(refs, packed):
    for j, ref in enumerate(refs):
        ref[...] = packed[:, j * SC_ROW_WORDS:(j + 1) * SC_ROW_WORDS]


def _head_mean_sq(o, bd_ones):
    sq = o * o
    hi = sq.astype(BF16)
    lo = (sq - hi.astype(F32)).astype(BF16)
    ones = bd_ones.astype(BF16)
    return (_dot(hi, ones) + _dot(lo, ones)) * (1.0 / HEAD_DIM)


def _rope_kernel(pos_ref, posr_ref, inv_ref, sign_ref, invc_ref, signc_ref, cos_ref, sin_ref, cost_ref, sint_ref):
    ang = pos_ref[...] * inv_ref[...]
    cos_ref[...] = jnp.cos(ang)
    sin_ref[...] = jnp.sin(ang) * sign_ref[...]
    ang_t = invc_ref[...] * posr_ref[...]
    cost_ref[...] = jnp.cos(ang_t)
    sint_ref[...] = jnp.sin(ang_t) * signc_ref[...]


def rope_tables(positions):
    n = positions.size
    tile = min(n, 2048)
    posf = positions.reshape(n).astype(F32)
    pos = jnp.broadcast_to(posf[:, None], (n, HEAD_DIM))
    inv = ROPE_THETA ** (-jnp.arange(0, HEAD_DIM, 2, dtype=F32) / HEAD_DIM)
    inv = jnp.concatenate([inv, inv])
    sign = jnp.concatenate([-jnp.ones((HEAD_DIM // 2,), F32), jnp.ones((HEAD_DIM // 2,), F32)])
    row = pl.BlockSpec((tile, HEAD_DIM), lambda i: (i, 0))
    rowt = pl.BlockSpec((HEAD_DIM, tile), lambda i: (0, i))
    const = pl.BlockSpec((1, HEAD_DIM), lambda i: (0, 0))
    constc = pl.BlockSpec((HEAD_DIM, 1), lambda i: (0, 0))
    return pl.pallas_call(
        _rope_kernel, grid=(n // tile,),
        in_specs=[row, pl.BlockSpec((1, tile), lambda i: (0, i)), const, const, constc, constc],
        out_specs=[row, row, rowt, rowt],
        out_shape=[jax.ShapeDtypeStruct((n, HEAD_DIM), F32)] * 2 + [jax.ShapeDtypeStruct((HEAD_DIM, n), F32)] * 2,
        compiler_params=_params("parallel"), name="rope_tables",
    )(pos, posf.reshape(1, n), inv.reshape(1, HEAD_DIM), sign.reshape(1, HEAD_DIM),
      inv.reshape(HEAD_DIM, 1), sign.reshape(HEAD_DIM, 1))


def _in_proj_kernel(x_ref, w_ref, h_ref):
    h_ref[...] = _dot(x_ref[...].astype(BF16), w_ref[...])


def in_proj(x2d, w_bf16):
    n, d = x2d.shape
    width = w_bf16.shape[1]
    tile = min(n, 512)
    return pl.pallas_call(
        _in_proj_kernel, grid=(n // tile,),
        in_specs=[pl.BlockSpec((tile, d), lambda i: (i, 0)), pl.BlockSpec((d, width), lambda i: (0, 0))],
        out_specs=pl.BlockSpec((tile, width), lambda i: (i, 0)),
        out_shape=jax.ShapeDtypeStruct((n, width), F32),
        compiler_params=_params("parallel"), name="in_proj")(x2d, w_bf16)


HG_LEVELS = (64, 32, 16, 8, 4, 2)
HG_BATCH = 8


def _hgrn_constants():
    c = HG_CHUNK
    t = np.arange(c)
    u = t[None, :]
    rows = [u <= t[:, None], u > t[:, None]]
    masks = [np.eye(c, dtype=bool)]
    for m in HG_LEVELS:
        ref = ((t // m) * m + m // 2 - 1)[:, None]
        second = (t % m >= m // 2)[:, None]
        rows.append(((u > ref) & (u <= t[:, None]) & second) | ((u > t[:, None]) & (u <= ref) & ~second))
        masks.append((t[:, None] // m == t[None, :] // m) & second & (t[None, :] % m < m // 2))
    pmat = np.concatenate(rows, axis=0).astype(np.float32)
    masks = np.stack([np.tile(mk, (1, 4)) for mk in masks]).astype(np.float32)
    return pmat, masks


def _hgrn_kernel(q_ref, f_ref, i_ref, g_ref, lb_ref, nw_ref, pmat_ref, masks_ref, bd_ref, hm_ref,
                 o_ref, state_ref):
    c = HG_CHUNK

    @pl.when(pl.program_id(1) == 0)
    def _():
        state_ref[...] = jnp.zeros_like(state_ref)

    lb = lb_ref[...]
    bd = bd_ref[...]
    hm = hm_ref[...]
    pmat = pmat_ref[...]
    a = jnp.log(lb)
    log1m = jnp.log(1.0 - lb)
    for bi in range(q_ref.shape[0]):
        fr = f_ref[bi]
        hq = q_ref[bi]
        qf = hq * _sigmoid(hq)
        log_sig = jnp.minimum(fr, 0.0) - jnp.log(1.0 + jnp.exp(-jnp.abs(fr)))
        cc = log1m + log_sig
        log_f = jnp.maximum(a, cc) + jnp.log(1.0 + jnp.exp(-jnp.abs(a - cc)))
        kk = (1.0 - lb) * _sigmoid(-fr)
        v = i_ref[bi]
        vb = v.astype(BF16)

        hi = log_f.astype(BF16)
        lo = (log_f - hi.astype(F32)).astype(BF16)
        sums = jnp.minimum(_dot(pmat, hi) + _dot(pmat, lo), 0.0)
        e_all = jnp.exp(sums)
        e_b = e_all[0:c]
        e_rest = e_all[c:2 * c]

        def stacked(x):
            return jnp.concatenate([x * hm[h:h + 1] for h in range(4)], axis=0).astype(BF16)

        att = masks_ref[0] * _dot_nt(qf.astype(BF16), stacked(kk))
        for li in range(len(HG_LEVELS)):
            e_l = e_all[(2 + li) * c:(3 + li) * c]
            att = att + masks_ref[li + 1] * _dot_nt((qf * e_l).astype(BF16), stacked(kk * e_l))
        o = _dot(att.astype(BF16), stacked(v))

        st = state_ref[bi]
        o = o + _dot_nt((qf * e_b).astype(BF16), st.astype(BF16))
        state_ref[bi] = st * e_b[c - 1:c] + bd * _dot_tn(vb, (kk * e_rest).astype(BF16))

        ms = _head_mean_sq(o, bd)
        y = o * lax.rsqrt(ms + RMS_EPS) * nw_ref[...] * _sigmoid(g_ref[bi])
        o_ref[bi] = y.astype(o_ref.dtype)


def hgrn2(h3, lb, norm_w):
    batch, seq, _ = h3.shape
    w = lb.shape[-1]
    c = HG_CHUNK
    nb = math.gcd(batch, HG_BATCH)
    pmat, masks = _hgrn_constants()
    lane_head = np.arange(w) // HEAD_DIM
    bd = (lane_head[:, None] == lane_head[None, :]).astype(np.float32)
    hm = (np.arange(4)[:, None] == lane_head[None, :]).astype(np.float32)

    def col(j):
        return pl.BlockSpec((nb, c, w), lambda b, i, j=j: (b, i, j))

    def const(shape):
        return pl.BlockSpec(shape, lambda b, i: (0,) * len(shape))

    return pl.pallas_call(
        _hgrn_kernel, grid=(batch // nb, seq // c),
        in_specs=[col(0), col(1), col(2), col(3), const((1, w)), const((1, w)),
                  const(pmat.shape), const(masks.shape), const(bd.shape), const(hm.shape)],
        out_specs=pl.BlockSpec((nb, c, w), lambda b, i: (b, i, 0)),
        out_shape=jax.ShapeDtypeStruct((batch, seq, w), BF16),
        scratch_shapes=[pltpu.VMEM((nb, w, w), F32)],
        compiler_params=_params("parallel", "arbitrary"), name="hgrn2",
    )(h3, h3, h3, h3, lb.reshape(1, w), norm_w.reshape(1, w), jnp.asarray(pmat, BF16), jnp.asarray(masks),
      jnp.asarray(bd), jnp.asarray(hm))


def _gmlp_kernel(u_ref, v_ref, lnw_ref, lnb_ref, ws_ref, bias_ref, nw_ref, bd_ref, hm_ref, o_ref):
    c = GM_CHUNK
    groups = ws_ref.shape[0]
    u = _gelu(u_ref[...])
    v = _layer_norm(_gelu(v_ref[...]), lnw_ref[...], lnb_ref[...])
    hm = hm_ref[...]
    bd = bd_ref[...]
    causal = lax.broadcasted_iota(jnp.int32, (c, c), 0) >= lax.broadcasted_iota(jnp.int32, (c, c), 1)
    w_cat = jnp.concatenate([jnp.where(causal, ws_ref[g], 0.0).astype(BF16) for g in range(groups)], axis=1)
    for j in range(u.shape[0] // c):
        rows = slice(j * c, (j + 1) * c)
        v_j = v[rows]
        v_bd = jnp.concatenate([v_j * hm[g:g + 1] for g in range(groups)], axis=0).astype(BF16)
        y = u[rows] * (bias_ref[...] + _dot(w_cat, v_bd))
        ms = _head_mean_sq(y, bd)
        o_ref[rows, :] = (y * lax.rsqrt(ms + RMS_EPS) * nw_ref[...]).astype(o_ref.dtype)


def gmlp(h, ln_w, ln_b, w_s, b_s, norm_w, u_col, v_col):
    n = h.shape[0]
    groups, c, _ = w_s.shape
    w = groups * HEAD_DIM
    lane_head = np.arange(w) // HEAD_DIM
    bd = (lane_head[:, None] == lane_head[None, :]).astype(np.float32)
    hm = (np.arange(groups)[:, None] == lane_head[None, :]).astype(np.float32)
    bias = jnp.repeat(b_s.T, HEAD_DIM, axis=1)

    def const(shape):
        return pl.BlockSpec(shape, lambda i: (0,) * len(shape))

    t = math.gcd(n, GM_TILE_CHUNKS * c)
    return pl.pallas_call(
        _gmlp_kernel, grid=(n // t,),
        in_specs=[pl.BlockSpec((t, w), lambda i: (i, u_col)), pl.BlockSpec((t, w), lambda i: (i, v_col)),
                  const((1, w)), const((1, w)), const(w_s.shape), const((c, w)), const((1, w)),
                  const(bd.shape), const(hm.shape)],
        out_specs=pl.BlockSpec((t, w), lambda i: (i, 0)),
        out_shape=jax.ShapeDtypeStruct((n, w), BF16),
        compiler_params=_params("parallel"), name="gmlp",
    )(h, h, ln_w.reshape(1, w), ln_b.reshape(1, w), w_s, bias, norm_w.reshape(1, w),
      jnp.asarray(bd), jnp.asarray(hm))


def _compress_kernel(u_ref, wtop_ref, wbot_ref, pe_ref, w2_ref, o_ref):
    u = u_ref[...].astype(BF16)
    wtop = wtop_ref[...]
    wbot = wbot_ref[...]
    pe = pe_ref[...].astype(BF16)
    const = _dot(pe[0:1], wtop) + _dot(pe[1:2], wbot)
    p = _dot(u, wtop)
    q = _dot(u, wbot)
    q_next = jnp.concatenate([q[1:], jnp.zeros_like(q[0:1])], axis=0)
    hid = _gelu(p + q_next + const)
    o_ref[...] = _dot(hid.astype(BF16), w2_ref[...]).astype(o_ref.dtype)


def compress(kv, pe, w1, w2, batch, seq):
    g = NSA_KV_GROUPS
    half = CMP_STRIDE
    units = seq // half
    gw = g * HEAD_DIM
    u = kv.reshape(batch * units, half * gw)
    eye = jnp.eye(g, dtype=F32)
    w1r = w1.reshape(2, half, HEAD_DIM, CMP_HIDDEN)
    wbd = jnp.einsum('hjdn,gk->hjgdkn', w1r, eye).reshape(2, half * gw, g * CMP_HIDDEN).astype(BF16)
    w2bd = jnp.einsum('nd,gk->gnkd', w2, eye).reshape(g * CMP_HIDDEN, gw).astype(BF16)
    pe2 = jnp.broadcast_to(pe.reshape(2, half, 1, HEAD_DIM), (2, half, g, HEAD_DIM)).reshape(2, half * gw)

    def const(shape):
        return pl.BlockSpec(shape, lambda b: (0,) * len(shape))

    out = pl.pallas_call(
        _compress_kernel, grid=(batch,),
        in_specs=[pl.BlockSpec((units, half * gw), lambda b: (b, 0)),
                  const(wbd.shape[1:]), const(wbd.shape[1:]), const(pe2.shape), const(w2bd.shape)],
        out_specs=pl.BlockSpec((units, gw), lambda b: (b, 0)),
        out_shape=jax.ShapeDtypeStruct((batch * units, gw), BF16),
        compiler_params=_params("parallel"), name="nsa_compress",
    )(u, wbd[0], wbd[1], pe2, w2bd)
    return out.reshape(batch, units, gw)


def _rot_half_pairs(x):
    lane = lax.broadcasted_iota(jnp.int32, x.shape, 1)
    fwd = pltpu.roll(x, 32, axis=1)
    bwd = pltpu.roll(x, 96, axis=1)
    return jnp.where((lane % HEAD_DIM) < HEAD_DIM // 2, bwd, fwd)


def _kprep_kernel(ks_ref, vs_ref, kw_ref, vw_ref, cos_ref, sin_ref, ksa_ref, vso_ref, kwo_ref, vwo_ref):
    t = ks_ref.shape[0]
    cos = cos_ref[...]
    sin = sin_ref[...]
    cos2 = jnp.concatenate([cos, cos], axis=1)
    sin2 = jnp.concatenate([sin, sin], axis=1)
    ks = ks_ref[...]
    kw = kw_ref[...]
    ks_r = ks * cos2 + _rot_half_pairs(ks) * sin2
    kw_r = kw * cos2 + _rot_half_pairs(kw) * sin2
    pos = pl.program_id(1) * t + lax.broadcasted_iota(jnp.int32, (t, HEAD_DIM), 0)
    onehot = (pos // SEL_BLOCK == lax.broadcasted_iota(jnp.int32, (t, HEAD_DIM), 1)).astype(F32)
    vs_t = vs_ref[...].T
    vw_t = vw_ref[...].T
    tail = (lax.broadcasted_iota(jnp.int32, (V_ROWS - HEAD_DIM, t), 0) == 0).astype(F32)
    for g in range(NSA_KV_GROUPS):
        sl = slice(g * HEAD_DIM, (g + 1) * HEAD_DIM)
        ksa_ref[g] = jnp.concatenate([ks_r[:, sl], onehot], axis=1).astype(BF16)
        vso_ref[g] = jnp.concatenate([vs_t[sl], tail], axis=0).astype(BF16)
        kwo_ref[g] = kw_r[:, sl].astype(BF16)
        vwo_ref[g] = jnp.concatenate([vw_t[sl], tail], axis=0).astype(BF16)


def nsa_kprep(h, cosf, sinf, batch, seq, col0):
    g = NSA_KV_GROUPS
    t = min(seq, 512)
    nt = seq // t

    def col(j):
        return pl.BlockSpec((t, LANES), lambda b, i, j=j: (b * nt + i, col0 + j))

    tab = pl.BlockSpec((t, HEAD_DIM), lambda b, i: (b * nt + i, 0))

    def out(wd):
        return pl.BlockSpec((None, g, t, wd), lambda b, i: (b, 0, i, 0))

    out_t = pl.BlockSpec((None, g, V_ROWS, t), lambda b, i: (b, 0, 0, i))
    k_shape = jax.ShapeDtypeStruct((batch, g, seq, HEAD_DIM), BF16)
    v_shape = jax.ShapeDtypeStruct((batch, g, V_ROWS, seq), BF16)
    return pl.pallas_call(
        _kprep_kernel, grid=(batch, nt),
        in_specs=[col(0), col(1), col(2), col(3), tab, tab],
        out_specs=[out(2 * HEAD_DIM), out_t, out(HEAD_DIM), out_t],
        out_shape=[jax.ShapeDtypeStruct((batch, g, seq, 2 * HEAD_DIM), BF16), v_shape, k_shape, v_shape],
        compiler_params=_params("parallel", "parallel"), name="nsa_kprep",
    )(h, h, h, h, cosf, sinf)


def _nsa_kernel(hq_ref, gate_ref, cos_ref, sin_ref, kc_ref, vc_ref, ksa_ref, vs_ref, kw_ref, vw_ref,
                ovl_ref, nw_ref, o_ref, *, tq, tk, n_sb):
    qi = pl.program_id(1)
    hpg = NSA_HPG
    groups = NSA_KV_GROUPS
    rows = hpg * tq
    t0 = qi * tq
    scale = 1.0 / math.sqrt(HEAD_DIM)
    half = HEAD_DIM // 2

    hq_t = hq_ref[...].T
    cos = cos_ref[...]
    sin = sin_ref[...]
    q_raw, q_rot = [], []
    for g in range(groups):
        raw_g, rot_g = [], []
        for h in range(hpg):
            r0 = (g * hpg + h) * HEAD_DIM
            qh = hq_t[r0:r0 + HEAD_DIM]
            swapped = jnp.concatenate([qh[half:], qh[:half]], axis=0)
            raw_g.append(qh * scale)
            rot_g.append((qh * cos + swapped * sin) * scale)
        q_raw.append(jnp.concatenate(raw_g, axis=1).astype(BF16))
        q_rot.append(jnp.concatenate(rot_g, axis=1))

    tpos = t0 + lax.broadcasted_iota(jnp.int32, (1, tq), 1)
    tpos_r = jnp.concatenate([tpos] * hpg, axis=1)

    def flash_step(s, v_t, m, acc):
        m_new = jnp.maximum(m, jnp.max(s, axis=0, keepdims=True))
        alpha = jnp.exp(m - m_new)
        p = jnp.exp(s - m_new).astype(BF16)
        return m_new, alpha * acc + _dot(v_t, p)

    init = (jnp.full((1, rows), NEG_INF, F32), jnp.zeros((V_ROWS, rows), F32))

    wk = WINDOW + tq
    kw0 = pl.multiple_of(jnp.maximum(t0 - WINDOW, 0), tq)
    kpos_w = kw0 + lax.broadcasted_iota(jnp.int32, (wk, 1), 0)
    mask_w = (kpos_w <= tpos_r) & (kpos_w > tpos_r - WINDOW)
    n_pad = jnp.maximum(WINDOW - 1 - tpos_r, 0).astype(F32)
    n_cmp = kc_ref.shape[1]
    cmp_end = lax.broadcasted_iota(jnp.int32, (n_cmp, 1), 0) * CMP_STRIDE + (CMP_BLOCK - 1)
    mask_c = cmp_end <= tpos_r
    blk = lax.broadcasted_iota(jnp.int32, (n_sb, tq), 0)
    cur = (t0 + lax.broadcasted_iota(jnp.int32, (n_sb, tq), 1)) // SEL_BLOCK
    o_w, o_c, imps = [], [], []
    for g in range(groups):
        s = _dot(kw_ref[g, pl.ds(kw0, wk), :], q_rot[g].astype(BF16))
        m_w, acc_w = flash_step(jnp.where(mask_w, s, NEG_INF), vw_ref[g, :, pl.ds(kw0, wk)], *init)
        m_f = jnp.where(n_pad > 0.0, jnp.maximum(m_w, 0.0), m_w)
        a_w = jnp.exp(m_w - m_f)
        o_w.append(acc_w[0:HEAD_DIM] * (a_w / (acc_w[HEAD_DIM:HEAD_DIM + 1] * a_w + n_pad * jnp.exp(-m_f))))

        s_c = jnp.where(mask_c, _dot(kc_ref[g], q_raw[g]), NEG_INF)
        e_c = jnp.exp(s_c - jnp.max(s_c, axis=0, keepdims=True))
        p_c = jnp.where(mask_c, e_c * (1.0 / jnp.sum(e_c, axis=0, keepdims=True)), 0.0)
        o_c.append(_dot(vc_ref[g], p_c.astype(BF16)))

        p_sum = p_c[:, 0:tq]
        for h in range(1, hpg):
            p_sum = p_sum + p_c[:, h * tq:(h + 1) * tq]
        imp = jnp.dot(ovl_ref[...], p_sum, precision=HIGHEST, preferred_element_type=F32)
        key = pltpu.bitcast(jnp.maximum(imp, 0.0), jnp.int32)
        key = jnp.where((blk == 0) | (blk == cur) | (blk == cur - 1), FORCE_KEY, key)
        imps.append(jnp.where(blk > cur, -1, key))

    n_sel = min(N_SEL, n_sb)

    def bit_body(it, taus):
        bit = lax.shift_left(jnp.int32(1), 30 - it)
        out = []
        for g in range(groups):
            cand = taus[g] | bit
            cnt = jnp.sum(jnp.where(imps[g] >= cand, 1, 0), axis=0, keepdims=True)
            out.append(jnp.where(cnt >= n_sel, cand, taus[g]))
        return tuple(out)

    taus = lax.fori_loop(0, 31, bit_body, (jnp.zeros((1, tq), jnp.int32),) * groups)
    lower = (lax.broadcasted_iota(jnp.int32, (n_sb, n_sb), 0)
             > lax.broadcasted_iota(jnp.int32, (n_sb, n_sb), 1)).astype(BF16)
    q_aug = []
    for g in range(groups):
        above = imps[g] > taus[g]
        equal = imps[g] == taus[g]
        need = n_sel - jnp.sum(jnp.where(above, 1, 0), axis=0, keepdims=True)
        earlier = _dot(lower, jnp.where(equal, 1.0, 0.0).astype(BF16))
        selected = above | (equal & (earlier < need.astype(F32)))
        sel_bias = jnp.where(selected, 0.0, NEG_INF)
        if n_sb < HEAD_DIM:
            sel_bias = jnp.concatenate([sel_bias, jnp.zeros((HEAD_DIM - n_sb, tq), F32)], axis=0)
        q_aug.append(jnp.concatenate([q_rot[g], jnp.concatenate([sel_bias] * hpg, axis=1)], axis=0).astype(BF16))

    def scores(g, kt):
        k0 = pl.multiple_of(kt * tk, tk)
        return _dot(ksa_ref[g, pl.ds(k0, tk), :], q_aug[g])

    def sel_body(kt, carry):
        k0 = pl.multiple_of(kt * tk, tk)
        s = [scores(g, kt) for g in range(groups)]
        m_new = [jnp.maximum(carry[g][0], jnp.max(s[g], axis=0, keepdims=True)) for g in range(groups)]
        p = [jnp.exp(s[g] - m_new[g]).astype(BF16) for g in range(groups)]
        return tuple((m_new[g], jnp.exp(carry[g][0] - m_new[g]) * carry[g][1] + _dot(vs_ref[g, :, pl.ds(k0, tk)], p[g]))
                     for g in range(groups))

    n_full = t0 // tk
    carry = lax.fori_loop(0, n_full, sel_body, (init,) * groups)
    k0 = pl.multiple_of(n_full * tk, tk)
    kpos = k0 + lax.broadcasted_iota(jnp.int32, (tk, 1), 0)
    mask_s = kpos <= tpos_r

    gates = _sigmoid(gate_ref[...].T)
    nw = nw_ref[...]
    outs = []
    for g in range(groups):
        _, acc_s = flash_step(jnp.where(mask_s, scores(g, n_full), NEG_INF), vs_ref[g, :, pl.ds(k0, tk)], *carry[g])
        o_s = acc_s[0:HEAD_DIM] * (1.0 / acc_s[HEAD_DIM:HEAD_DIM + 1])
        for h in range(hpg):
            sl = slice(h * tq, (h + 1) * tq)
            r = (g * hpg + h) * N_GATES
            o = gates[r:r + 1] * o_c[g][:, sl] + gates[r + 1:r + 2] * o_s[:, sl] + gates[r + 2:r + 3] * o_w[g][:, sl]
            ms = jnp.mean(o * o, axis=0, keepdims=True)
            outs.append(o * lax.rsqrt(ms + RMS_EPS) * nw[:, g * hpg + h:g * hpg + h + 1])
    o_ref[...] = jnp.concatenate(outs, axis=0).T.astype(o_ref.dtype)


def nsa_attention(h, cos_t, sin_t, kc, vc_t, ksa, vs_t, kw, vw_t, norm_w, batch, seq, q_col0, gate_col):
    g, hpg = NSA_KV_GROUPS, NSA_HPG
    tq = min(seq, 256)
    nq = seq // tq
    n_sb = seq // SEL_BLOCK
    assert n_sb <= HEAD_DIM, "selection-block one-hot shares the 64 spare key lanes"
    n_cmp = kc.shape[2]
    units = np.arange(n_cmp)[:, None] + np.arange(CMP_BLOCK // CMP_STRIDE)[None, :]
    ovl = np.zeros((n_cmp, n_sb), np.float32)
    for c in range((seq - CMP_BLOCK) // CMP_STRIDE + 1):
        for u in units[c]:
            ovl[c, u // (SEL_BLOCK // CMP_STRIDE)] += 1.0
    ovl_t = jnp.asarray(ovl.T)

    def per_b(shape):
        return pl.BlockSpec((None, g) + shape, lambda b, qi: (b, 0, 0, 0))

    width = g * hpg * HEAD_DIM
    tab = pl.BlockSpec((HEAD_DIM, tq), lambda b, qi: (0, b * nq + qi))
    tk = min(seq, 512)
    assert seq >= WINDOW + tq and seq % tk == 0 and tk % tq == 0
    kern = functools.partial(_nsa_kernel, tq=tq, tk=tk, n_sb=n_sb)
    return pl.pallas_call(
        kern, grid=(batch, nq),
        in_specs=[pl.BlockSpec((tq, width), lambda b, qi: (b * nq + qi, q_col0)),
                  pl.BlockSpec((tq, LANES), lambda b, qi: (b * nq + qi, gate_col)),
                  tab, tab,
                  per_b((n_cmp, HEAD_DIM)), per_b((HEAD_DIM, n_cmp)),
                  per_b((seq, 2 * HEAD_DIM)), per_b((V_ROWS, seq)),
                  per_b((seq, HEAD_DIM)), per_b((V_ROWS, seq)),
                  pl.BlockSpec((n_sb, n_cmp), lambda b, qi: (0, 0)),
                  pl.BlockSpec((HEAD_DIM, g * hpg), lambda b, qi: (0, 0))],
        out_specs=pl.BlockSpec((tq, width), lambda b, qi: (b * nq + qi, 0)),
        out_shape=jax.ShapeDtypeStruct((batch * seq, width), BF16),
        compiler_params=_params("parallel", "arbitrary"), name="nsa_attention",
    )(h, h, cos_t, sin_t, kc, vc_t, ksa, vs_t, kw, vw_t, ovl_t, norm_w.reshape(g * hpg, HEAD_DIM).T)


def _out_proj_kernel(x_ref, yhg_ref, ygm_ref, ynsa_ref, whg_ref, wgm_ref, wnsa_ref, lnw_ref, lnb_ref,
                     o_ref, oa_ref, ob_ref, *, alpha):
    mix = (_dot(yhg_ref[...], whg_ref[...]) + _dot(ygm_ref[...], wgm_ref[...])
           + _dot(ynsa_ref[...], wnsa_ref[...]))
    y = _layer_norm(alpha * x_ref[...] + mix, lnw_ref[...], lnb_ref[...])
    o_ref[...] = y
    _store_word_tables((oa_ref, ob_ref), _pack_bf16_pairs(y))


def out_proj_ln(x2d, y_hg, y_gm, y_nsa, w_out, ln_w, ln_b, alpha):
    n, d = x2d.shape
    w1, w2 = y_hg.shape[1], y_hg.shape[1] + y_gm.shape[1]
    whg = w_out[:w1].astype(BF16)
    wgm = w_out[w1:w2].astype(BF16)
    wnsa = w_out[w2:].astype(BF16)
    t = min(n, 512)

    def row(wd):
        return pl.BlockSpec((t, wd), lambda i: (i, 0))

    def const(shape):
        return pl.BlockSpec(shape, lambda i: (0,) * len(shape))

    kern = functools.partial(_out_proj_kernel, alpha=alpha)
    return pl.pallas_call(
        kern, grid=(n // t,),
        in_specs=[row(d), row(y_hg.shape[1]), row(y_gm.shape[1]), row(y_nsa.shape[1]),
                  const(whg.shape), const(wgm.shape), const(wnsa.shape), const((1, d)), const((1, d))],
        out_specs=[row(d), row(SC_ROW_WORDS), row(SC_ROW_WORDS)],
        out_shape=[jax.ShapeDtypeStruct((n, d), F32)] + [jax.ShapeDtypeStruct((n, SC_ROW_WORDS), jnp.uint32)] * 2,
        compiler_params=_params("parallel"), name="out_proj_ln",
    )(x2d, y_hg, y_gm, y_nsa, whg, wgm, wnsa, ln_w.reshape(1, d), ln_b.reshape(1, d))


def _router_kernel(x_ref, w_ref, b_ref, e_ref, p_ref, r_ref, cnt_ref, carry_ref):
    t = x_ref.shape[0]

    @pl.when(pl.program_id(0) == 0)
    def _():
        carry_ref[...] = jnp.zeros_like(carry_ref)

    x = x_ref[...]
    x_hi = x.astype(BF16)
    x_lo = (x - x_hi.astype(F32)).astype(BF16)
    w = w_ref[...]
    w_hi = w.astype(BF16)
    w_lo = (w - w_hi.astype(F32)).astype(BF16)
    logits = _dot(x_hi, w_hi) + (_dot(x_lo, w_hi) + _dot(x_hi, w_lo)) + b_ref[...]
    lane = lax.broadcasted_iota(jnp.int32, logits.shape, 1)
    work = logits
    vals, idxs = [], []
    sel = jnp.zeros(logits.shape, F32)
    for _ in range(TOP_K):
        m = jnp.max(work, axis=-1, keepdims=True)
        idx = jnp.min(jnp.where(work == m, lane, LANES), axis=-1, keepdims=True)
        hit = lane == idx
        sel = jnp.where(hit, 1.0, sel)
        work = jnp.where(hit, -jnp.inf, work)
        vals.append(m)
        idxs.append(idx)
    exps = [jnp.exp(v - vals[0]) for v in vals]
    den = exps[0] + exps[1] + exps[2] + exps[3]
    strict = (lax.broadcasted_iota(jnp.int32, (t, t), 0) > lax.broadcasted_iota(jnp.int32, (t, t), 1))
    before = _dot(strict.astype(BF16), sel.astype(BF16)) + carry_ref[...]
    ranks = [jnp.sum(jnp.where(lane == idx, before, 0.0), axis=-1, keepdims=True) for idx in idxs]
    kcol = lax.broadcasted_iota(jnp.int32, (t, TOP_K), 1)
    e_out = jnp.zeros((t, TOP_K), jnp.int32)
    p_out = jnp.zeros((t, TOP_K), F32)
    r_out = jnp.zeros((t, TOP_K), jnp.int32)
    for k in range(TOP_K):
        e_out = jnp.where(kcol == k, idxs[k], e_out)
        p_out = jnp.where(kcol == k, exps[k] / den, p_out)
        r_out = jnp.where(kcol == k, ranks[k].astype(jnp.int32), r_out)
    e_ref[...] = e_out
    p_ref[...] = p_out
    r_ref[...] = r_out
    carry_ref[...] = carry_ref[...] + jnp.sum(sel, axis=0, keepdims=True)
    cnt_ref[...] = carry_ref[...].astype(jnp.int32)


def moe_router(x2d, router_w, router_b):
    n, d = x2d.shape
    e = router_w.shape[1]
    t = min(n, 512)
    w = jnp.zeros((d, LANES), F32).at[:, :e].set(router_w)
    b = jnp.full((1, LANES), NEG_INF, F32).at[0, :e].set(router_b)
    row4 = pl.BlockSpec((t, TOP_K), lambda i: (i, 0))
    top_e, top_p, rank, counts = pl.pallas_call(
        _router_kernel, grid=(n // t,),
        in_specs=[pl.BlockSpec((t, d), lambda i: (i, 0)), pl.BlockSpec((d, LANES), lambda i: (0, 0)),
                  pl.BlockSpec((1, LANES), lambda i: (0, 0))],
        out_specs=[row4, row4, row4, pl.BlockSpec((1, LANES), lambda i: (0, 0))],
        out_shape=[jax.ShapeDtypeStruct((n, TOP_K), jnp.int32), jax.ShapeDtypeStruct((n, TOP_K), F32),
                   jax.ShapeDtypeStruct((n, TOP_K), jnp.int32), jax.ShapeDtypeStruct((1, LANES), jnp.int32)],
        scratch_shapes=[pltpu.VMEM((1, LANES), F32)],
        compiler_params=_params("arbitrary"), name="moe_router",
    )(x2d, w, b)
    return top_e, top_p, rank, counts[0, :e]


def _expert_kernel(be_ref, valid_ref, xa_ref, xb_ref, wu_ref, bu_ref, wd_ref, bd_ref, oa_ref, ob_ref,
                   wu_bf, wd_bf):
    i = pl.program_id(0)
    f = wd_ref.shape[0]
    n_used = be_ref[pl.num_programs(0)]

    @pl.when((i == 0) | (be_ref[i] != be_ref[jnp.maximum(i - 1, 0)]))
    def _():
        wu_bf[...] = wu_ref[...].astype(BF16)
        wd_bf[...] = wd_ref[...].astype(BF16)

    @pl.when(i < n_used)
    def _():
        packed = jnp.concatenate([xa_ref[...], xb_ref[...]], axis=1)
        live = lax.broadcasted_iota(jnp.int32, packed.shape, 0) < valid_ref[i]
        x_lo, x_hi = _unpack_bf16_pairs(jnp.where(live, packed, jnp.uint32(0)))
        x = jnp.concatenate([x_lo.astype(BF16), x_hi.astype(BF16)], axis=1)
        hcat = _dot(x, wu_bf[...]) + bu_ref[...]
        glu = jnp.minimum(hcat[:, :f], SWIGLU_LIMIT)
        lin = jnp.clip(hcat[:, f:], -SWIGLU_LIMIT, SWIGLU_LIMIT)
        act = glu * _sigmoid(SWIGLU_ALPHA * glu) * (lin + 1.0)
        _store_word_tables((oa_ref, ob_ref), _pack_bf16_pairs(_dot(act.astype(BF16), wd_bf[...]) + bd_ref[...]))

    @pl.when(i >= n_used)
    def _():
        oa_ref[...] = jnp.zeros_like(oa_ref)
        ob_ref[...] = jnp.zeros_like(ob_ref)


def moe_experts(xa, xb, block_e, n_used, block_valid, w_up, b_up, w_down, b_down, layer):
    rows = xa.shape[0]
    _, e, d, f2 = w_up.shape
    f = f2 // 2
    nb = rows // EXPERT_BLOCK
    words = pl.BlockSpec((EXPERT_BLOCK, SC_ROW_WORDS), lambda i, be, nv: (i, 0))
    grid_spec = pltpu.PrefetchScalarGridSpec(
        num_scalar_prefetch=2, grid=(nb,),
        in_specs=[words, words,
                  pl.BlockSpec((None, None, d, f2), lambda i, be, nv: (layer, be[i], 0, 0)),
                  pl.BlockSpec((None, None, 1, f2), lambda i, be, nv: (layer, be[i], 0, 0)),
                  pl.BlockSpec((None, None, f, d), lambda i, be, nv: (layer, be[i], 0, 0)),
                  pl.BlockSpec((None, None, 1, d), lambda i, be, nv: (layer, be[i], 0, 0))],
        out_specs=[words, words],
        scratch_shapes=[pltpu.VMEM((d, f2), BF16), pltpu.VMEM((f, d), BF16)])
    depth = w_up.shape[0]
    return pl.pallas_call(
        _expert_kernel, grid_spec=grid_spec,
        out_shape=[jax.ShapeDtypeStruct((rows, SC_ROW_WORDS), jnp.uint32)] * 2,
        compiler_params=pltpu.CompilerParams(dimension_semantics=("arbitrary",), vmem_limit_bytes=EXPERT_VMEM_LIMIT),
        name="moe_experts",
    )(jnp.concatenate([block_e, n_used.reshape(1)]), block_valid, xa, xb, w_up, b_up.reshape(depth, e, 1, f2),
      w_down, b_down.reshape(depth, e, 1, d))


def _combine_kernel(x_ref, ya_ref, yb_ref, p_ref, lnw_ref, lnb_ref, o_ref, *, alpha):
    p = p_ref[...]
    moe = jnp.zeros(x_ref.shape, F32)
    for k in range(TOP_K):
        y_lo, y_hi = _unpack_bf16_pairs(jnp.concatenate([ya_ref[k], yb_ref[k]], axis=1))
        moe = moe + p[:, k:k + 1] * jnp.concatenate([y_lo, y_hi], axis=1)
    o_ref[...] = _layer_norm(alpha * x_ref[...] + moe, lnw_ref[...], lnb_ref[...])


def combine_ln(x2d, ya, yb, top_p, ln_w, ln_b, alpha):
    n, d = x2d.shape
    t = min(n, 256)
    kern = functools.partial(_combine_kernel, alpha=alpha)
    words = pl.BlockSpec((TOP_K, t, SC_ROW_WORDS), lambda i: (0, i, 0))
    return pl.pallas_call(
        kern, grid=(n // t,),
        in_specs=[pl.BlockSpec((t, d), lambda i: (i, 0)), words, words,
                  pl.BlockSpec((t, TOP_K), lambda i: (i, 0)),
                  pl.BlockSpec((1, d), lambda i: (0, 0)), pl.BlockSpec((1, d), lambda i: (0, 0))],
        out_specs=pl.BlockSpec((t, d), lambda i: (i, 0)),
        out_shape=jax.ShapeDtypeStruct((n, d), F32),
        compiler_params=_params("parallel"), name="moe_combine_ln",
    )(x2d, ya, yb, top_p, ln_w.reshape(1, d), ln_b.reshape(1, d))


def _sc_mesh():
    return plsc.VectorSubcoreMesh(core_axis_name="core", subcore_axis_name="subcore")


def sc_gather_rows(tables, idx):
    r = idx.shape[0]
    nt = len(tables)
    out = jax.ShapeDtypeStruct((r, SC_ROW_WORDS), tables[0].dtype)

    @pl.kernel(out_type=(out,) * nt, mesh=_sc_mesh(), name="sc_gather_rows")
    def gather(*refs):
        x_hbm, i_hbm, o_hbm = refs[:nt], refs[nt], refs[nt + 1:]
        for j in range(nt):
            def body(i_vmem, o_vmem, table=x_hbm[j]):
                pltpu.sync_copy(table.at[i_vmem.at[0]], o_vmem)

            pltpu.emit_pipeline(
                body, grid=(r // SC_WINDOW,),
                in_specs=[pl.BlockSpec((1, SC_WINDOW), lambda i: (0, i))],
                out_specs=[pl.BlockSpec((SC_WINDOW, SC_ROW_WORDS), lambda i: (i, 0))],
                core_axis_name=("core", "subcore"), dimension_semantics=(pltpu.PARALLEL,),
            )(i_hbm, o_hbm[j])

    return gather(*tables, idx.reshape(1, r))


def sc_scatter_rows(tables, dest_t, n_rows):
    n = tables[0].shape[0]
    nt = len(tables)
    copies = dest_t.shape[0]
    out = jax.ShapeDtypeStruct((n_rows, SC_ROW_WORDS), tables[0].dtype)

    @pl.kernel(out_type=(out,) * nt, mesh=_sc_mesh(), scratch_types=[], name="sc_scatter_rows")
    def scatter(*refs):
        x_hbm, i_hbm, o_hbm = refs[:nt], refs[nt], refs[nt + 1:]
        for j in range(nt):
            def body(x_vmem, i_vmem, out_j=o_hbm[j]):
                for k in range(copies):
                    pltpu.sync_copy(x_vmem, out_j.at[i_vmem.at[k]])

            pltpu.emit_pipeline(
                body, grid=(n // SC_WINDOW,),
                in_specs=[pl.BlockSpec((SC_WINDOW, SC_ROW_WORDS), lambda i: (i, 0)),
                          pl.BlockSpec((copies, SC_WINDOW), lambda i: (0, i))],
                out_specs=[],
                core_axis_name=("core", "subcore"), dimension_semantics=(pltpu.PARALLEL,),
            )(x_hbm[j], i_hbm)

    return scatter(*tables, dest_t)


def moe_ffn_ln(x_f32, x_packed, router_w, router_b, w_up, b_up, w_down, b_down, layer, ln_w, ln_b, alpha):
    n, d = x_f32.shape
    top_e, top_p, rank, counts = moe_router(x_f32, router_w, router_b)
    padded = (counts + EXPERT_BLOCK - 1) // EXPERT_BLOCK * EXPERT_BLOCK
    pad_end = jnp.cumsum(padded)
    pad_start = pad_end - padded
    n_assign = n * TOP_K
    n_blocks = -(-(n_assign + N_EXPERTS * (EXPERT_BLOCK - 1)) // EXPERT_BLOCK)
    dest_t = (pad_start[top_e] + rank).T
    block_first = jnp.arange(n_blocks, dtype=jnp.int32) * EXPERT_BLOCK
    block_e = jnp.clip(jnp.sum((pad_end[None, :] <= block_first[:, None]).astype(jnp.int32), axis=1),
                       0, N_EXPERTS - 1)
    block_valid = jnp.clip(counts[block_e] - (block_first - pad_start[block_e]), 0, EXPERT_BLOCK)
    n_used = (pad_end[-1] // EXPERT_BLOCK).astype(jnp.int32)
    xa, xb = sc_scatter_rows(x_packed, dest_t, n_blocks * EXPERT_BLOCK)
    ya, yb = moe_experts(xa, xb, block_e, n_used, block_valid.astype(jnp.int32), w_up, b_up, w_down, b_down,
                         layer)
    ya, yb = sc_gather_rows((ya, yb), dest_t.reshape(-1))
    return combine_ln(x_f32, ya.reshape(TOP_K, n, SC_ROW_WORDS), yb.reshape(TOP_K, n, SC_ROW_WORDS), top_p,
                      ln_w, ln_b, alpha)


def kernel(x, positions, w_in, hg_lower_bounds, hg_norm_w, gm_ln_w, gm_ln_b, gm_spatial_w, gm_spatial_b, gm_norm_w, nsa_cmp_pe, nsa_cmp_w1, nsa_cmp_w2, nsa_norm_w, w_out, ln1_w, ln1_b, router_w, router_b, exp_w_up, exp_b_up, exp_w_down, exp_b_down, ln2_w, ln2_b):
    batch, seq, d = x.shape
    depth = w_in.shape[0]
    n = batch * seq
    alpha = (2 * depth) ** 0.25
    hg_w = hg_norm_w.shape[1]
    gm_w = gm_norm_w.shape[1]
    nsa_w = nsa_norm_w.shape[1]
    kv_w = NSA_KV_GROUPS * HEAD_DIM
    in_width = w_in.shape[2]
    off_gm = 4 * hg_w
    off_q = off_gm + 2 * gm_w
    off_kv = off_q + nsa_w
    off_gate = off_kv + 6 * kv_w
    width_pad = -(-in_width // LANES) * LANES

    cosf, sinf, cos_t, sin_t = rope_tables(positions)
    lb_all = jnp.cumsum(jax.nn.softmax(hg_lower_bounds.astype(F32), axis=0), axis=0)
    lb_all = lb_all - lb_all[0:1]

    x2d = x.reshape(n, d)
    for l in range(depth):
        w_l = jnp.pad(w_in[l], ((0, 0), (0, width_pad - in_width))).astype(BF16)
        h = in_proj(x2d, w_l)
        h3 = h.reshape(batch, seq, width_pad)
        y_hg = hgrn2(h3, lb_all[l], hg_norm_w[l]).reshape(n, hg_w)
        y_gm = gmlp(h, gm_ln_w[l], gm_ln_b[l], gm_spatial_w[l], gm_spatial_b[l], gm_norm_w[l],
                    off_gm // gm_w, off_gm // gm_w + 1)
        kc = compress(h3[:, :, off_kv:off_kv + kv_w], nsa_cmp_pe[l, 0], nsa_cmp_w1[l, 0], nsa_cmp_w2[l, 0], batch, seq)
        vc = compress(h3[:, :, off_kv + kv_w:off_kv + 2 * kv_w], nsa_cmp_pe[l, 1], nsa_cmp_w1[l, 1],
                      nsa_cmp_w2[l, 1], batch, seq)
        n_cmp = kc.shape[1]
        kc = kc.reshape(batch, n_cmp, NSA_KV_GROUPS, HEAD_DIM).transpose(0, 2, 1, 3)
        vc_t = vc.reshape(batch, n_cmp, NSA_KV_GROUPS, HEAD_DIM).transpose(0, 2, 3, 1)
        ksa, vs_t, kw, vw_t = nsa_kprep(h, cosf, sinf, batch, seq, (off_kv + 2 * kv_w) // LANES)
        y_nsa = nsa_attention(h, cos_t, sin_t, kc, vc_t, ksa, vs_t, kw, vw_t, nsa_norm_w[l], batch, seq,
                              off_q // nsa_w, off_gate // LANES)
        x1, x1a, x1b = out_proj_ln(x2d, y_hg, y_gm, y_nsa, w_out[l], ln1_w[l], ln1_b[l], alpha)
        x2d = moe_ffn_ln(x1, (x1a, x1b), router_w[l], router_b[l], exp_w_up, exp_b_up, exp_w_down, exp_b_down, l,
                         ln2_w[l], ln2_b[l], alpha)
    return x2d.reshape(batch, seq, d)
```

```python
import functools
import math

import numpy as np
import jax
import jax.numpy as jnp
from jax import lax
from jax.experimental import pallas as pl
from jax.experimental.pallas import tpu as pltpu
from jax.experimental.pallas import tpu_sc as plsc

F32 = jnp.float32
BF16 = jnp.bfloat16
HIGHEST = lax.Precision.HIGHEST

HEAD_DIM = 64
LANES = 128
VMEM_LIMIT = 48 * 1024 * 1024
EXPERT_VMEM_LIMIT = 56 * 1024 * 1024

HG_CHUNK = 64
GM_CHUNK = 128
GM_TILE_CHUNKS = 4
NSA_KV_GROUPS = 2
NSA_HPG = 4
CMP_BLOCK = 32
CMP_STRIDE = 16
CMP_HIDDEN = 128
SEL_BLOCK = 64
N_SEL = 16
WINDOW = 512
N_GATES = 3
IMP_FORCE = 1e9
FORCE_KEY = int(np.float32(IMP_FORCE).view(np.int32))
NEG_INF = -1e30
N_EXPERTS = 32
TOP_K = 4
SWIGLU_ALPHA = 1.702
SWIGLU_LIMIT = 7.0
EXPERT_BLOCK = 512
SC_ROW_WORDS = 256
SC_WINDOW = 128
ROPE_THETA = 10000.0
LOG2_E = 1.4426950408889634
LN_EPS = 1e-5
RMS_EPS = 1e-6
V_ROWS = HEAD_DIM + 16


def _params(*sem):
    return pltpu.CompilerParams(dimension_semantics=sem, vmem_limit_bytes=VMEM_LIMIT)


def _dot(a, b):
    return jnp.dot(a, b, preferred_element_type=F32)


def _dot_nt(a, b, precision=None):
    return lax.dot_general(a, b, (((1,), (1,)), ((), ())), precision=precision,
                           preferred_element_type=F32)


def _dot_tn(a, b):
    return lax.dot_general(a, b, (((0,), (0,)), ((), ())), preferred_element_type=F32)


def _sigmoid(x):
    return 1.0 / (1.0 + jnp.exp(-x))


def _gelu(x):
    return 0.5 * x * (1.0 + jnp.tanh(0.7978845608028654 * (x + 0.044715 * x * x * x)))


def _layer_norm(x, w, b):
    mu = jnp.mean(x, axis=-1, keepdims=True)
    xc = x - mu
    var = jnp.mean(xc * xc, axis=-1, keepdims=True)
    return xc * lax.rsqrt(var + LN_EPS) * w + b


def _pack_bf16_pairs(y):
    w = y.shape[1] // 2
    bits = pltpu.bitcast(y.astype(BF16).astype(F32), jnp.uint32)
    return lax.shift_right_logical(bits[:, :w], jnp.uint32(16)) | (bits[:, w:] & jnp.uint32(0xFFFF0000))


def _unpack_bf16_pairs(u):
    lo = pltpu.bitcast(lax.shift_left(u, jnp.uint32(16)), F32)
    hi = pltpu.bitcast(u & jnp.uint32(0xFFFF0000), F32)
    return lo, hi


def _store_word_tables(refs, packed):
    for j, ref in enumerate(refs):
        ref[...] = packed[:, j * SC_ROW_WORDS:(j + 1) * SC_ROW_WORDS]


def _head_mean_sq(o, bd_ones):
    sq = o * o
    hi = sq.astype(BF16)
    lo = (sq - hi.astype(F32)).astype(BF16)
    ones = bd_ones.astype(BF16)
    return (_dot(hi, ones) + _dot(lo, ones)) * (1.0 / HEAD_DIM)


def _rope_kernel(pos_ref, inv_ref, cos_ref, sin_ref, cost_ref, sint_ref):
    ang = inv_ref[...] * pos_ref[...]
    c = jnp.cos(ang)
    s = jnp.sin(ang)
    cos_t = jnp.concatenate([c, c], axis=0)
    sin_t = jnp.concatenate([-s, s], axis=0)
    cost_ref[...] = cos_t
    sint_ref[...] = sin_t
    cos_ref[...] = cos_t.T
    sin_ref[...] = sin_t.T


def rope_tables(positions):
    n = positions.size
    tile = min(n, 2048)
    posf = positions.reshape(1, n).astype(F32)
    inv = ROPE_THETA ** (-jnp.arange(0, HEAD_DIM, 2, dtype=F32) / HEAD_DIM)
    row = pl.BlockSpec((tile, HEAD_DIM), lambda i: (i, 0))
    rowt = pl.BlockSpec((HEAD_DIM, tile), lambda i: (0, i))
    return pl.pallas_call(
        _rope_kernel, grid=(n // tile,),
        in_specs=[pl.BlockSpec((1, tile), lambda i: (0, i)), pl.BlockSpec((HEAD_DIM // 2, 1), lambda i: (0, 0))],
        out_specs=[row, row, rowt, rowt],
        out_shape=[jax.ShapeDtypeStruct((n, HEAD_DIM), F32)] * 2 + [jax.ShapeDtypeStruct((HEAD_DIM, n), F32)] * 2,
        compiler_params=_params("parallel"), name="rope_tables",
    )(posf, inv.reshape(HEAD_DIM // 2, 1))


def _in_proj_kernel(x_ref, w_ref, h_ref):
    h_ref[...] = _dot(x_ref[...].astype(BF16), w_ref[...])


def in_proj(x2d, w_bf16):
    n, d = x2d.shape
    width = w_bf16.shape[1]
    tile = min(n, 512)
    return pl.pallas_call(
        _in_proj_kernel, grid=(n // tile,),
        in_specs=[pl.BlockSpec((tile, d), lambda i: (i, 0)), pl.BlockSpec((d, width), lambda i: (0, 0))],
        out_specs=pl.BlockSpec((tile, width), lambda i: (i, 0)),
        out_shape=jax.ShapeDtypeStruct((n, width), F32),
        compiler_params=_params("parallel"), name="in_proj")(x2d, w_bf16)


HG_LEVELS = (64, 32, 16, 8, 4, 2)
HG_BATCH = 8


def _hgrn_constants():
    c = HG_CHUNK
    t = np.arange(c)
    u = t[None, :]
    rows = [u <= t[:, None], u > t[:, None]]
    masks = [np.eye(c, dtype=bool)]
    for m in HG_LEVELS:
        ref = ((t // m) * m + m // 2 - 1)[:, None]
        second = (t % m >= m // 2)[:, None]
        rows.append(((u > ref) & (u <= t[:, None]) & second) | ((u > t[:, None]) & (u <= ref) & ~second))
        masks.append((t[:, None] // m == t[None, :] // m) & second & (t[None, :] % m < m // 2))
    pmat = np.concatenate(rows, axis=0).astype(np.float32)
    masks = np.stack([np.tile(mk, (1, 4)) for mk in masks]).astype(np.float32)
    return pmat, masks


def _hgrn_kernel(q_ref, f_ref, i_ref, g_ref, lb_ref, nw_ref, pmat_ref, masks_ref, bd_ref, hm_ref,
                 o_ref, state_ref):
    c = HG_CHUNK

    @pl.when(pl.program_id(1) == 0)
    def _():
        state_ref[...] = jnp.zeros_like(state_ref)

    lb = lb_ref[...]
    bd = bd_ref[...]
    hm = hm_ref[...]
    pmat = pmat_ref[...]
    a = jnp.log(lb)
    log1m = jnp.log(1.0 - lb)
    for bi in range(q_ref.shape[0]):
        fr = f_ref[bi]
        hq = q_ref[bi]
        qf = hq * _sigmoid(hq)
        log_sig = jnp.minimum(fr, 0.0) - jnp.log(1.0 + jnp.exp(-jnp.abs(fr)))
        cc = log1m + log_sig
        log_f = jnp.maximum(a, cc) + jnp.log(1.0 + jnp.exp(-jnp.abs(a - cc)))
        kk = (1.0 - lb) * _sigmoid(-fr)
        v = i_ref[bi]
        vb = v.astype(BF16)

        hi = log_f.astype(BF16)
        lo = (log_f - hi.astype(F32)).astype(BF16)
        sums = jnp.minimum(_dot(pmat, hi) + _dot(pmat, lo), 0.0)
        e_all = jnp.exp(sums)
        e_b = e_all[0:c]
        e_rest = e_all[c:2 * c]

        def stacked(x):
            return jnp.concatenate([x * hm[h:h + 1] for h in range(4)], axis=0).astype(BF16)

        att = masks_ref[0] * _dot_nt(qf.astype(BF16), stacked(kk))
        for li in range(len(HG_LEVELS)):
            e_l = e_all[(2 + li) * c:(3 + li) * c]
            att = att + masks_ref[li + 1] * _dot_nt((qf * e_l).astype(BF16), stacked(kk * e_l))
        o = _dot(att.astype(BF16), stacked(v))

        st = state_ref[bi]
        o = o + _dot_nt((qf * e_b).astype(BF16), st.astype(BF16))
        state_ref[bi] = st * e_b[c - 1:c] + bd * _dot_tn(vb, (kk * e_rest).astype(BF16))

        ms = _head_mean_sq(o, bd)
        y = o * lax.rsqrt(ms + RMS_EPS) * nw_ref[...] * _sigmoid(g_ref[bi])
        o_ref[bi] = y.astype(o_ref.dtype)


def hgrn2(h3, lb, norm_w):
    batch, seq, _ = h3.shape
    w = lb.shape[-1]
    c = HG_CHUNK
    nb = math.gcd(batch, HG_BATCH)
    pmat, masks = _hgrn_constants()
    lane_head = np.arange(w) // HEAD_DIM
    bd = (lane_head[:, None] == lane_head[None, :]).astype(np.float32)
    hm = (np.arange(4)[:, None] == lane_head[None, :]).astype(np.float32)

    def col(j):
        return pl.BlockSpec((nb, c, w), lambda b, i, j=j: (b, i, j))

    def const(shape):
        return pl.BlockSpec(shape, lambda b, i: (0,) * len(shape))

    return pl.pallas_call(
        _hgrn_kernel, grid=(batch // nb, seq // c),
        in_specs=[col(0), col(1), col(2), col(3), const((1, w)), const((1, w)),
                  const(pmat.shape), const(masks.shape), const(bd.shape), const(hm.shape)],
        out_specs=pl.BlockSpec((nb, c, w), lambda b, i: (b, i, 0)),
        out_shape=jax.ShapeDtypeStruct((batch, seq, w), BF16),
        scratch_shapes=[pltpu.VMEM((nb, w, w), F32)],
        compiler_params=_params("parallel", "arbitrary"), name="hgrn2",
    )(h3, h3, h3, h3, lb.reshape(1, w), norm_w.reshape(1, w), jnp.asarray(pmat, BF16), jnp.asarray(masks),
      jnp.asarray(bd), jnp.asarray(hm))


def _gmlp_kernel(u_ref, v_ref, lnw_ref, lnb_ref, ws_ref, bias_ref, nw_ref, bd_ref, hm_ref, o_ref):
    c = GM_CHUNK
    groups = ws_ref.shape[0]
    u = _gelu(u_ref[...])
    v = _layer_norm(_gelu(v_ref[...]), lnw_ref[...], lnb_ref[...])
    hm = hm_ref[...]
    bd = bd_ref[...]
    causal = lax.broadcasted_iota(jnp.int32, (c, c), 0) >= lax.broadcasted_iota(jnp.int32, (c, c), 1)
    w_cat = jnp.concatenate([jnp.where(causal, ws_ref[g], 0.0).astype(BF16) for g in range(groups)], axis=1)
    for j in range(u.shape[0] // c):
        rows = slice(j * c, (j + 1) * c)
        v_j = v[rows]
        v_bd = jnp.concatenate([v_j * hm[g:g + 1] for g in range(groups)], axis=0).astype(BF16)
        y = u[rows] * (bias_ref[...] + _dot(w_cat, v_bd))
        ms = _head_mean_sq(y, bd)
        o_ref[rows, :] = (y * lax.rsqrt(ms + RMS_EPS) * nw_ref[...]).astype(o_ref.dtype)


def gmlp(h, ln_w, ln_b, w_s, b_s, norm_w, u_col, v_col):
    n = h.shape[0]
    groups, c, _ = w_s.shape
    w = groups * HEAD_DIM
    lane_head = np.arange(w) // HEAD_DIM
    bd = (lane_head[:, None] == lane_head[None, :]).astype(np.float32)
    hm = (np.arange(groups)[:, None] == lane_head[None, :]).astype(np.float32)
    bias = jnp.repeat(b_s.T, HEAD_DIM, axis=1)

    def const(shape):
        return pl.BlockSpec(shape, lambda i: (0,) * len(shape))

    t = math.gcd(n, GM_TILE_CHUNKS * c)
    return pl.pallas_call(
        _gmlp_kernel, grid=(n // t,),
        in_specs=[pl.BlockSpec((t, w), lambda i: (i, u_col)), pl.BlockSpec((t, w), lambda i: (i, v_col)),
                  const((1, w)), const((1, w)), const(w_s.shape), const((c, w)), const((1, w)),
                  const(bd.shape), const(hm.shape)],
        out_specs=pl.BlockSpec((t, w), lambda i: (i, 0)),
        out_shape=jax.ShapeDtypeStruct((n, w), BF16),
        compiler_params=_params("parallel"), name="gmlp",
    )(h, h, ln_w.reshape(1, w), ln_b.reshape(1, w), w_s, bias, norm_w.reshape(1, w),
      jnp.asarray(bd), jnp.asarray(hm))


def _compress_kernel(u_ref, wtop_ref, wbot_ref, pe_ref, w2_ref, o_ref):
    u = u_ref[...].astype(BF16)
    wtop = wtop_ref[...]
    wbot = wbot_ref[...]
    pe = pe_ref[...].astype(BF16)
    const = _dot(pe[0:1], wtop) + _dot(pe[1:2], wbot)
    p = _dot(u, wtop)
    q = _dot(u, wbot)
    q_next = jnp.concatenate([q[1:], jnp.zeros_like(q[0:1])], axis=0)
    hid = _gelu(p + q_next + const)
    o_ref[...] = _dot(hid.astype(BF16), w2_ref[...]).astype(o_ref.dtype)


def compress(kv, pe, w1, w2, batch, seq):
    g = NSA_KV_GROUPS
    half = CMP_STRIDE
    units = seq // half
    gw = g * HEAD_DIM
    u = kv.reshape(batch * units, half * gw)
    eye = jnp.eye(g, dtype=F32)
    w1r = w1.reshape(2, half, HEAD_DIM, CMP_HIDDEN)
    wbd = jnp.einsum('hjdn,gk->hjgdkn', w1r, eye).reshape(2, half * gw, g * CMP_HIDDEN).astype(BF16)
    w2bd = jnp.einsum('nd,gk->gnkd', w2, eye).reshape(g * CMP_HIDDEN, gw).astype(BF16)
    pe2 = jnp.broadcast_to(pe.reshape(2, half, 1, HEAD_DIM), (2, half, g, HEAD_DIM)).reshape(2, half * gw)

    def const(shape):
        return pl.BlockSpec(shape, lambda b: (0,) * len(shape))

    out = pl.pallas_call(
        _compress_kernel, grid=(batch,),
        in_specs=[pl.BlockSpec((units, half * gw), lambda b: (b, 0)),
                  const(wbd.shape[1:]), const(wbd.shape[1:]), const(pe2.shape), const(w2bd.shape)],
        out_specs=pl.BlockSpec((units, gw), lambda b: (b, 0)),
        out_shape=jax.ShapeDtypeStruct((batch * units, gw), BF16),
        compiler_params=_params("parallel"), name="nsa_compress",
    )(u, wbd[0], wbd[1], pe2, w2bd)
    return out.reshape(batch, units, gw)


def _rot_half_pairs(x):
    lane = lax.broadcasted_iota(jnp.int32, x.shape, 1)
    fwd = pltpu.roll(x, 32, axis=1)
    bwd = pltpu.roll(x, 96, axis=1)
    return jnp.where((lane % HEAD_DIM) < HEAD_DIM // 2, bwd, fwd)


def _kprep_kernel(ks_ref, vs_ref, kw_ref, vw_ref, cos_ref, sin_ref, ksa_ref, vso_ref, kwo_ref, vwo_ref):
    t = ks_ref.shape[0]
    cos = cos_ref[...]
    sin = sin_ref[...]
    cos2 = jnp.concatenate([cos, cos], axis=1)
    sin2 = jnp.concatenate([sin, sin], axis=1)
    ks = ks_ref[...]
    kw = kw_ref[...]
    ks_r = ks * cos2 + _rot_half_pairs(ks) * sin2
    kw_r = kw * cos2 + _rot_half_pairs(kw) * sin2
    pos = pl.program_id(1) * t + lax.broadcasted_iota(jnp.int32, (t, HEAD_DIM), 0)
    onehot = (pos // SEL_BLOCK == lax.broadcasted_iota(jnp.int32, (t, HEAD_DIM), 1)).astype(F32)
    vs_t = vs_ref[...].T
    vw_t = vw_ref[...].T
    tail = (lax.broadcasted_iota(jnp.int32, (V_ROWS - HEAD_DIM, t), 0) == 0).astype(F32)
    for g in range(NSA_KV_GROUPS):
        sl = slice(g * HEAD_DIM, (g + 1) * HEAD_DIM)
        ksa_ref[g] = jnp.concatenate([ks_r[:, sl], onehot], axis=1).astype(BF16)
        vso_ref[g] = jnp.concatenate([vs_t[sl], tail], axis=0).astype(BF16)
        kwo_ref[g] = kw_r[:, sl].astype(BF16)
        vwo_ref[g] = jnp.concatenate([vw_t[sl], tail], axis=0).astype(BF16)


def nsa_kprep(h, cosf, sinf, batch, seq, col0):
    g = NSA_KV_GROUPS
    t = min(seq, 512)
    nt = seq // t

    def col(j):
        return pl.BlockSpec((t, LANES), lambda b, i, j=j: (b * nt + i, col0 + j))

    tab = pl.BlockSpec((t, HEAD_DIM), lambda b, i: (b * nt + i, 0))

    def out(wd):
        return pl.BlockSpec((None, g, t, wd), lambda b, i: (b, 0, i, 0))

    out_t = pl.BlockSpec((None, g, V_ROWS, t), lambda b, i: (b, 0, 0, i))
    k_shape = jax.ShapeDtypeStruct((batch, g, seq, HEAD_DIM), BF16)
    v_shape = jax.ShapeDtypeStruct((batch, g, V_ROWS, seq), BF16)
    return pl.pallas_call(
        _kprep_kernel, grid=(batch, nt),
        in_specs=[col(0), col(1), col(2), col(3), tab, tab],
        out_specs=[out(2 * HEAD_DIM), out_t, out(HEAD_DIM), out_t],
        out_shape=[jax.ShapeDtypeStruct((batch, g, seq, 2 * HEAD_DIM), BF16), v_shape, k_shape, v_shape],
        compiler_params=_params("parallel", "parallel"), name="nsa_kprep",
    )(h, h, h, h, cosf, sinf)


def _nsa_kernel(hq_ref, gate_ref, cos_ref, sin_ref, kc_ref, vc_ref, ksa_ref, vs_ref, kw_ref, vw_ref,
                ovl_ref, nw_ref, o_ref, *, tq, tk, n_sb):
    qi = pl.program_id(1)
    hpg = NSA_HPG
    groups = NSA_KV_GROUPS
    rows = hpg * tq
    t0 = qi * tq
    scale = 1.0 / math.sqrt(HEAD_DIM)
    half = HEAD_DIM // 2

    hq_t = hq_ref[...].T
    cos = cos_ref[...]
    sin = sin_ref[...]
    q_raw, q_rot = [], []
    for g in range(groups):
        raw_g, rot_g = [], []
        for h in range(hpg):
            r0 = (g * hpg + h) * HEAD_DIM
            qh = hq_t[r0:r0 + HEAD_DIM]
            swapped = jnp.concatenate([qh[half:], qh[:half]], axis=0)
            raw_g.append(qh * (scale * LOG2_E))
            rot_g.append((qh * cos + swapped * sin) * (scale * LOG2_E))
        q_raw.append(jnp.concatenate(raw_g, axis=1).astype(BF16))
        q_rot.append(jnp.concatenate(rot_g, axis=1))

    tpos = t0 + lax.broadcasted_iota(jnp.int32, (1, tq), 1)
    tpos_r = jnp.concatenate([tpos] * hpg, axis=1)

    def flash_step(s, v_t, m, acc):
        m_new = jnp.maximum(m, jnp.max(s, axis=0, keepdims=True))
        alpha = jnp.exp2(m - m_new)
        p = jnp.exp2(s - m_new).astype(BF16)
        return m_new, alpha * acc + _dot(v_t, p)

    init = (jnp.full((1, rows), NEG_INF, F32), jnp.zeros((V_ROWS, rows), F32))

    wk = WINDOW + tq
    kw0 = pl.multiple_of(jnp.maximum(t0 - WINDOW, 0), tq)
    kpos_w = kw0 + lax.broadcasted_iota(jnp.int32, (wk, 1), 0)
    mask_w = (kpos_w <= tpos_r) & (kpos_w > tpos_r - WINDOW)
    n_pad = jnp.maximum(WINDOW - 1 - tpos_r, 0).astype(F32)
    n_cmp = kc_ref.shape[1]
    cmp_end = lax.broadcasted_iota(jnp.int32, (n_cmp, 1), 0) * CMP_STRIDE + (CMP_BLOCK - 1)
    mask_c = cmp_end <= tpos_r
    blk = lax.broadcasted_iota(jnp.int32, (n_sb, tq), 0)
    cur = (t0 + lax.broadcasted_iota(jnp.int32, (n_sb, tq), 1)) // SEL_BLOCK
    o_w, o_c, imps = [], [], []
    for g in range(groups):
        s = _dot(kw_ref[g, pl.ds(kw0, wk), :], q_rot[g].astype(BF16))
        m_w, acc_w = flash_step(jnp.where(mask_w, s, NEG_INF), vw_ref[g, :, pl.ds(kw0, wk)], *init)
        m_f = jnp.where(n_pad > 0.0, jnp.maximum(m_w, 0.0), m_w)
        a_w = jnp.exp2(m_w - m_f)
        o_w.append(acc_w[0:HEAD_DIM] * (a_w / (acc_w[HEAD_DIM:HEAD_DIM + 1] * a_w + n_pad * jnp.exp2(-m_f))))

        s_c = jnp.where(mask_c, _dot(kc_ref[g], q_raw[g]), NEG_INF)
        e_c = jnp.exp2(s_c - jnp.max(s_c, axis=0, keepdims=True))
        p_c = jnp.where(mask_c, e_c * (1.0 / jnp.sum(e_c, axis=0, keepdims=True)), 0.0)
        o_c.append(_dot(vc_ref[g], p_c.astype(BF16)))

        p_sum = p_c[:, 0:tq]
        for h in range(1, hpg):
            p_sum = p_sum + p_c[:, h * tq:(h + 1) * tq]
        imp = jnp.dot(ovl_ref[...], p_sum, precision=HIGHEST, preferred_element_type=F32)
        key = pltpu.bitcast(jnp.maximum(imp, 0.0), jnp.int32)
        key = jnp.where((blk == 0) | (blk == cur) | (blk == cur - 1), FORCE_KEY, key)
        imps.append(jnp.where(blk > cur, -1, key))

    n_sel = min(N_SEL, n_sb)

    def bit_body(it, taus):
        bit = lax.shift_left(jnp.int32(1), 30 - it)
        out = []
        for g in range(groups):
            cand = taus[g] | bit
            cnt = jnp.sum(jnp.where(imps[g] >= cand, 1, 0), axis=0, keepdims=True)
            out.append(jnp.where(cnt >= n_sel, cand, taus[g]))
        return tuple(out)

    taus = lax.fori_loop(0, 31, bit_body, (jnp.zeros((1, tq), jnp.int32),) * groups)
    lower = (lax.broadcasted_iota(jnp.int32, (n_sb, n_sb), 0)
             > lax.broadcasted_iota(jnp.int32, (n_sb, n_sb), 1)).astype(BF16)
    q_aug = []
    for g in range(groups):
        above = imps[g] > taus[g]
        equal = imps[g] == taus[g]
        need = n_sel - jnp.sum(jnp.where(above, 1, 0), axis=0, keepdims=True)
        earlier = _dot(lower, jnp.where(equal, 1.0, 0.0).astype(BF16))
        selected = above | (equal & (earlier < need.astype(F32)))
        sel_bias = jnp.where(selected, 0.0, NEG_INF)
        if n_sb < HEAD_DIM:
            sel_bias = jnp.concatenate([sel_bias, jnp.zeros((HEAD_DIM - n_sb, tq), F32)], axis=0)
        q_aug.append(jnp.concatenate([q_rot[g], jnp.concatenate([sel_bias] * hpg, axis=1)], axis=0).astype(BF16))

    def scores(g, kt):
        k0 = pl.multiple_of(kt * tk, tk)
        return _dot(ksa_ref[g, pl.ds(k0, tk), :], q_aug[g])

    def sel_body(kt, carry):
        k0 = pl.multiple_of(kt * tk, tk)
        s = [scores(g, kt) for g in range(groups)]
        m_new = [jnp.maximum(carry[g][0], jnp.max(s[g], axis=0, keepdims=True)) for g in range(groups)]
        p = [jnp.exp2(s[g] - m_new[g]).astype(BF16) for g in range(groups)]
        return tuple((m_new[g], jnp.exp2(carry[g][0] - m_new[g]) * carry[g][1] + _dot(vs_ref[g, :, pl.ds(k0, tk)], p[g]))
                     for g in range(groups))

    n_full = t0 // tk
    carry = lax.fori_loop(0, n_full, sel_body, (init,) * groups)
    k0 = pl.multiple_of(n_full * tk, tk)
    kpos = k0 + lax.broadcasted_iota(jnp.int32, (tk, 1), 0)
    mask_s = kpos <= tpos_r

    gates = _sigmoid(gate_ref[...].T)
    nw = nw_ref[...]
    outs = []
    for g in range(groups):
        _, acc_s = flash_step(jnp.where(mask_s, scores(g, n_full), NEG_INF), vs_ref[g, :, pl.ds(k0, tk)], *carry[g])
        o_s = acc_s[0:HEAD_DIM] * (1.0 / acc_s[HEAD_DIM:HEAD_DIM + 1])
        for h in range(hpg):
            sl = slice(h * tq, (h + 1) * tq)
            r = (g * hpg + h) * N_GATES
            o = gates[r:r + 1] * o_c[g][:, sl] + gates[r + 1:r + 2] * o_s[:, sl] + gates[r + 2:r + 3] * o_w[g][:, sl]
            ms = jnp.mean(o * o, axis=0, keepdims=True)
            outs.append(o * lax.rsqrt(ms + RMS_EPS) * nw[:, g * hpg + h:g * hpg + h + 1])
    o_ref[...] = jnp.concatenate(outs, axis=0).T.astype(o_ref.dtype)


def nsa_attention(h, cos_t, sin_t, kc, vc_t, ksa, vs_t, kw, vw_t, norm_w, batch, seq, q_col0, gate_col):
    g, hpg = NSA_KV_GROUPS, NSA_HPG
    tq = min(seq, 256)
    nq = seq // tq
    n_sb = seq // SEL_BLOCK
    assert n_sb <= HEAD_DIM, "selection-block one-hot shares the 64 spare key lanes"
    n_cmp = kc.shape[2]
    units = np.arange(n_cmp)[:, None] + np.arange(CMP_BLOCK // CMP_STRIDE)[None, :]
    ovl = np.zeros((n_cmp, n_sb), np.float32)
    for c in range((seq - CMP_BLOCK) // CMP_STRIDE + 1):
        for u in units[c]:
            ovl[c, u // (SEL_BLOCK // CMP_STRIDE)] += 1.0
    ovl_t = jnp.asarray(ovl.T)

    def per_b(shape):
        return pl.BlockSpec((None, g) + shape, lambda b, qi: (b, 0, 0, 0))

    width = g * hpg * HEAD_DIM
    tab = pl.BlockSpec((HEAD_DIM, tq), lambda b, qi: (0, b * nq + qi))
    tk = min(seq, 512)
    assert seq >= WINDOW + tq and seq % tk == 0 and tk % tq == 0
    kern = functools.partial(_nsa_kernel, tq=tq, tk=tk, n_sb=n_sb)
    return pl.pallas_call(
        kern, grid=(batch, nq),
        in_specs=[pl.BlockSpec((tq, width), lambda b, qi: (b * nq + qi, q_col0)),
                  pl.BlockSpec((tq, LANES), lambda b, qi: (b * nq + qi, gate_col)),
                  tab, tab,
                  per_b((n_cmp, HEAD_DIM)), per_b((HEAD_DIM, n_cmp)),
                  per_b((seq, 2 * HEAD_DIM)), per_b((V_ROWS, seq)),
                  per_b((seq, HEAD_DIM)), per_b((V_ROWS, seq)),
                  pl.BlockSpec((n_sb, n_cmp), lambda b, qi: (0, 0)),
                  pl.BlockSpec((HEAD_DIM, g * hpg), lambda b, qi: (0, 0))],
        out_specs=pl.BlockSpec((tq, width), lambda b, qi: (b * nq + qi, 0)),
        out_shape=jax.ShapeDtypeStruct((batch * seq, width), BF16),
        compiler_params=_params("parallel", "arbitrary"), name="nsa_attention",
    )(h, h, cos_t, sin_t, kc, vc_t, ksa, vs_t, kw, vw_t, ovl_t, norm_w.reshape(g * hpg, HEAD_DIM).T)


def _out_proj_kernel(x_ref, yhg_ref, ygm_ref, ynsa_ref, whg_ref, wgm_ref, wnsa_ref, lnw_ref, lnb_ref,
                     o_ref, oa_ref, ob_ref, *, alpha):
    mix = (_dot(yhg_ref[...], whg_ref[...]) + _dot(ygm_ref[...], wgm_ref[...])
           + _dot(ynsa_ref[...], wnsa_ref[...]))
    y = _layer_norm(alpha * x_ref[...] + mix, lnw_ref[...], lnb_ref[...])
    o_ref[...] = y
    _store_word_tables((oa_ref, ob_ref), _pack_bf16_pairs(y))


def out_proj_ln(x2d, y_hg, y_gm, y_nsa, w_out, ln_w, ln_b, alpha):
    n, d = x2d.shape
    w1, w2 = y_hg.shape[1], y_hg.shape[1] + y_gm.shape[1]
    whg = w_out[:w1].astype(BF16)
    wgm = w_out[w1:w2].astype(BF16)
    wnsa = w_out[w2:].astype(BF16)
    t = min(n, 512)

    def row(wd):
        return pl.BlockSpec((t, wd), lambda i: (i, 0))

    def const(shape):
        return pl.BlockSpec(shape, lambda i: (0,) * len(shape))

    kern = functools.partial(_out_proj_kernel, alpha=alpha)
    return pl.pallas_call(
        kern, grid=(n // t,),
        in_specs=[row(d), row(y_hg.shape[1]), row(y_gm.shape[1]), row(y_nsa.shape[1]),
                  const(whg.shape), const(wgm.shape), const(wnsa.shape), const((1, d)), const((1, d))],
        out_specs=[row(d), row(SC_ROW_WORDS), row(SC_ROW_WORDS)],
        out_shape=[jax.ShapeDtypeStruct((n, d), F32)] + [jax.ShapeDtypeStruct((n, SC_ROW_WORDS), jnp.uint32)] * 2,
        compiler_params=_params("parallel"), name="out_proj_ln",
    )(x2d, y_hg, y_gm, y_nsa, whg, wgm, wnsa, ln_w.reshape(1, d), ln_b.reshape(1, d))


def _router_kernel(x_ref, w_ref, b_ref, e_ref, p_ref, r_ref, cnt_ref, carry_ref):
    t = x_ref.shape[0]

    @pl.when(pl.program_id(0) == 0)
    def _():
        carry_ref[...] = jnp.zeros_like(carry_ref)

    x = x_ref[...]
    x_hi = x.astype(BF16)
    x_lo = (x - x_hi.astype(F32)).astype(BF16)
    w = w_ref[...]
    w_hi = w.astype(BF16)
    w_lo = (w - w_hi.astype(F32)).astype(BF16)
    logits = _dot(x_hi, w_hi) + (_dot(x_lo, w_hi) + _dot(x_hi, w_lo)) + b_ref[...]
    lane = lax.broadcasted_iota(jnp.int32, logits.shape, 1)
    work = logits
    vals, idxs = [], []
    sel = jnp.zeros(logits.shape, F32)
    for _ in range(TOP_K):
        m = jnp.max(work, axis=-1, keepdims=True)
        idx = jnp.min(jnp.where(work == m, lane, LANES), axis=-1, keepdims=True)
        hit = lane == idx
        sel = jnp.where(hit, 1.0, sel)
        work = jnp.where(hit, -jnp.inf, work)
        vals.append(m)
        idxs.append(idx)
    exps = [jnp.exp(v - vals[0]) for v in vals]
    den = exps[0] + exps[1] + exps[2] + exps[3]
    strict = (lax.broadcasted_iota(jnp.int32, (t, t), 0) > lax.broadcasted_iota(jnp.int32, (t, t), 1))
    before = _dot(strict.astype(BF16), sel.astype(BF16)) + carry_ref[...]
    ranks = [jnp.sum(jnp.where(lane == idx, before, 0.0), axis=-1, keepdims=True) for idx in idxs]
    kcol = lax.broadcasted_iota(jnp.int32, (t, TOP_K), 1)
    e_out = jnp.zeros((t, TOP_K), jnp.int32)
    p_out = jnp.zeros((t, TOP_K), F32)
    r_out = jnp.zeros((t, TOP_K), jnp.int32)
    for k in range(TOP_K):
        e_out = jnp.where(kcol == k, idxs[k], e_out)
        p_out = jnp.where(kcol == k, exps[k] / den, p_out)
        r_out = jnp.where(kcol == k, ranks[k].astype(jnp.int32), r_out)
    e_ref[...] = e_out
    p_ref[...] = p_out
    r_ref[...] = r_out
    carry_ref[...] = carry_ref[...] + jnp.sum(sel, axis=0, keepdims=True)
    cnt_ref[...] = carry_ref[...].astype(jnp.int32)


def moe_router(x2d, router_w, router_b):
    n, d = x2d.shape
    e = router_w.shape[1]
    t = min(n, 512)
    w = jnp.zeros((d, LANES), F32).at[:, :e].set(router_w)
    b = jnp.full((1, LANES), NEG_INF, F32).at[0, :e].set(router_b)
    row4 = pl.BlockSpec((t, TOP_K), lambda i: (i, 0))
    top_e, top_p, rank, counts = pl.pallas_call(
        _router_kernel, grid=(n // t,),
        in_specs=[pl.BlockSpec((t, d), lambda i: (i, 0)), pl.BlockSpec((d, LANES), lambda i: (0, 0)),
                  pl.BlockSpec((1, LANES), lambda i: (0, 0))],
        out_specs=[row4, row4, row4, pl.BlockSpec((1, LANES), lambda i: (0, 0))],
        out_shape=[jax.ShapeDtypeStruct((n, TOP_K), jnp.int32), jax.ShapeDtypeStruct((n, TOP_K), F32),
                   jax.ShapeDtypeStruct((n, TOP_K), jnp.int32), jax.ShapeDtypeStruct((1, LANES), jnp.int32)],
        scratch_shapes=[pltpu.VMEM((1, LANES), F32)],
        compiler_params=_params("arbitrary"), name="moe_router",
    )(x2d, w, b)
    return top_e, top_p, rank, counts[0, :e]


def _expert_kernel(be_ref, valid_ref, xa_ref, xb_ref, wu_ref, bu_ref, wd_ref, bd_ref, oa_ref, ob_ref,
                   wu_bf, wd_bf):
    i = pl.program_id(0)
    f = wd_ref.shape[0]
    n_used = be_ref[pl.num_programs(0)]

    @pl.when((i == 0) | (be_ref[i] != be_ref[jnp.maximum(i - 1, 0)]))
    def _():
        wu_bf[...] = wu_ref[...].astype(BF16)
        wd_bf[...] = wd_ref[...].astype(BF16)

    @pl.when(i < n_used)
    def _():
        packed = jnp.concatenate([xa_ref[...], xb_ref[...]], axis=1)
        live = lax.broadcasted_iota(jnp.int32, packed.shape, 0) < valid_ref[i]
        x_lo, x_hi = _unpack_bf16_pairs(jnp.where(live, packed, jnp.uint32(0)))
        x = jnp.concatenate([x_lo.astype(BF16), x_hi.astype(BF16)], axis=1)
        hcat = _dot(x, wu_bf[...]) + bu_ref[...]
        glu = jnp.minimum(hcat[:, :f], SWIGLU_LIMIT)
        lin = jnp.clip(hcat[:, f:], -SWIGLU_LIMIT, SWIGLU_LIMIT)
        act = glu * _sigmoid(SWIGLU_ALPHA * glu) * (lin + 1.0)
        _store_word_tables((oa_ref, ob_ref), _pack_bf16_pairs(_dot(act.astype(BF16), wd_bf[...]) + bd_ref[...]))

    @pl.when(i >= n_used)
    def _():
        oa_ref[...] = jnp.zeros_like(oa_ref)
        ob_ref[...] = jnp.zeros_like(ob_ref)


def moe_experts(xa, xb, block_e, n_used, block_valid, w_up, b_up, w_down, b_down, layer):
    rows = xa.shape[0]
    _, e, d, f2 = w_up.shape
    f = f2 // 2
    nb = rows // EXPERT_BLOCK
    words = pl.BlockSpec((EXPERT_BLOCK, SC_ROW_WORDS), lambda i, be, nv: (i, 0))
    grid_spec = pltpu.PrefetchScalarGridSpec(
        num_scalar_prefetch=2, grid=(nb,),
        in_specs=[words, words,
                  pl.BlockSpec((None, None, d, f2), lambda i, be, nv: (layer, be[i], 0, 0)),
                  pl.BlockSpec((None, None, 1, f2), lambda i, be, nv: (layer, be[i], 0, 0)),
                  pl.BlockSpec((None, None, f, d), lambda i, be, nv: (layer, be[i], 0, 0)),
                  pl.BlockSpec((None, None, 1, d), lambda i, be, nv: (layer, be[i], 0, 0))],
        out_specs=[words, words],
        scratch_shapes=[pltpu.VMEM((d, f2), BF16), pltpu.VMEM((f, d), BF16)])
    depth = w_up.shape[0]
    return pl.pallas_call(
        _expert_kernel, grid_spec=grid_spec,
        out_shape=[jax.ShapeDtypeStruct((rows, SC_ROW_WORDS), jnp.uint32)] * 2,
        compiler_params=pltpu.CompilerParams(dimension_semantics=("arbitrary",), vmem_limit_bytes=EXPERT_VMEM_LIMIT),
        name="moe_experts",
    )(jnp.concatenate([block_e, n_used.reshape(1)]), block_valid, xa, xb, w_up, b_up.reshape(depth, e, 1, f2),
      w_down, b_down.reshape(depth, e, 1, d))


def _combine_kernel(x_ref, ya_ref, yb_ref, p_ref, lnw_ref, lnb_ref, o_ref, *, alpha):
    p = p_ref[...]
    moe = jnp.zeros(x_ref.shape, F32)
    for k in range(TOP_K):
        y_lo, y_hi = _unpack_bf16_pairs(jnp.concatenate([ya_ref[k], yb_ref[k]], axis=1))
        moe = moe + p[:, k:k + 1] * jnp.concatenate([y_lo, y_hi], axis=1)
    o_ref[...] = _layer_norm(alpha * x_ref[...] + moe, lnw_ref[...], lnb_ref[...])


def combine_ln(x2d, ya, yb, top_p, ln_w, ln_b, alpha):
    n, d = x2d.shape
    t = min(n, 256)
    kern = functools.partial(_combine_kernel, alpha=alpha)
    words = pl.BlockSpec((TOP_K, t, SC_ROW_WORDS), lambda i: (0, i, 0))
    return pl.pallas_call(
        kern, grid=(n // t,),
        in_specs=[pl.BlockSpec((t, d), lambda i: (i, 0)), words, words,
                  pl.BlockSpec((t, TOP_K), lambda i: (i, 0)),
                  pl.BlockSpec((1, d), lambda i: (0, 0)), pl.BlockSpec((1, d), lambda i: (0, 0))],
        out_specs=pl.BlockSpec((t, d), lambda i: (i, 0)),
        out_shape=jax.ShapeDtypeStruct((n, d), F32),
        compiler_params=_params("parallel"), name="moe_combine_ln",
    )(x2d, ya, yb, top_p, ln_w.reshape(1, d), ln_b.reshape(1, d))


def _sc_mesh():
    return plsc.VectorSubcoreMesh(core_axis_name="core", subcore_axis_name="subcore")


def sc_gather_rows(tables, idx):
    r = idx.shape[0]
    nt = len(tables)
    out = jax.ShapeDtypeStruct((r, SC_ROW_WORDS), tables[0].dtype)

    @pl.kernel(out_type=(out,) * nt, mesh=_sc_mesh(), name="sc_gather_rows")
    def gather(*refs):
        x_hbm, i_hbm, o_hbm = refs[:nt], refs[nt], refs[nt + 1:]
        for j in range(nt):
            def body(i_vmem, o_vmem, table=x_hbm[j]):
                pltpu.sync_copy(table.at[i_vmem.at[0]], o_vmem)

            pltpu.emit_pipeline(
                body, grid=(r // SC_WINDOW,),
                in_specs=[pl.BlockSpec((1, SC_WINDOW), lambda i: (0, i))],
                out_specs=[pl.BlockSpec((SC_WINDOW, SC_ROW_WORDS), lambda i: (i, 0))],
                core_axis_name=("core", "subcore"), dimension_semantics=(pltpu.PARALLEL,),
            )(i_hbm, o_hbm[j])

    return gather(*tables, idx.reshape(1, r))


def sc_scatter_rows(tables, dest_t, n_rows):
    n = tables[0].shape[0]
    nt = len(tables)
    copies = dest_t.shape[0]
    out = jax.ShapeDtypeStruct((n_rows, SC_ROW_WORDS), tables[0].dtype)

    @pl.kernel(out_type=(out,) * nt, mesh=_sc_mesh(), scratch_types=[], name="sc_scatter_rows")
    def scatter(*refs):
        x_hbm, i_hbm, o_hbm = refs[:nt], refs[nt], refs[nt + 1:]
        for j in range(nt):
            def body(x_vmem, i_vmem, out_j=o_hbm[j]):
                for k in range(copies):
                    pltpu.sync_copy(x_vmem, out_j.at[i_vmem.at[k]])

            pltpu.emit_pipeline(
                body, grid=(n // SC_WINDOW,),
                in_specs=[pl.BlockSpec((SC_WINDOW, SC_ROW_WORDS), lambda i: (i, 0)),
                          pl.BlockSpec((copies, SC_WINDOW), lambda i: (0, i))],
                out_specs=[],
                core_axis_name=("core", "subcore"), dimension_semantics=(pltpu.PARALLEL,),
            )(x_hbm[j], i_hbm)

    return scatter(*tables, dest_t)


def moe_ffn_ln(x_f32, x_packed, router_w, router_b, w_up, b_up, w_down, b_down, layer, ln_w, ln_b, alpha):
    n, d = x_f32.shape
    top_e, top_p, rank, counts = moe_router(x_f32, router_w, router_b)
    padded = (counts + EXPERT_BLOCK - 1) // EXPERT_BLOCK * EXPERT_BLOCK
    pad_end = jnp.cumsum(padded)
    pad_start = pad_end - padded
    n_assign = n * TOP_K
    n_blocks = -(-(n_assign + N_EXPERTS * (EXPERT_BLOCK - 1)) // EXPERT_BLOCK)
    dest_t = (pad_start[top_e] + rank).T
    block_first = jnp.arange(n_blocks, dtype=jnp.int32) * EXPERT_BLOCK
    block_e = jnp.clip(jnp.sum((pad_end[None, :] <= block_first[:, None]).astype(jnp.int32), axis=1),
                       0, N_EXPERTS - 1)
    block_valid = jnp.clip(counts[block_e] - (block_first - pad_start[block_e]), 0, EXPERT_BLOCK)
    n_used = (pad_end[-1] // EXPERT_BLOCK).astype(jnp.int32)
    xa, xb = sc_scatter_rows(x_packed, dest_t, n_blocks * EXPERT_BLOCK)
    ya, yb = moe_experts(xa, xb, block_e, n_used, block_valid.astype(jnp.int32), w_up, b_up, w_down, b_down,
                         layer)
    ya, yb = sc_gather_rows((ya, yb), dest_t.reshape(-1))
    return combine_ln(x_f32, ya.reshape(TOP_K, n, SC_ROW_WORDS), yb.reshape(TOP_K, n, SC_ROW_WORDS), top_p,
                      ln_w, ln_b, alpha)


def kernel(x, positions, w_in, hg_lower_bounds, hg_norm_w, gm_ln_w, gm_ln_b, gm_spatial_w, gm_spatial_b, gm_norm_w, nsa_cmp_pe, nsa_cmp_w1, nsa_cmp_w2, nsa_norm_w, w_out, ln1_w, ln1_b, router_w, router_b, exp_w_up, exp_b_up, exp_w_down, exp_b_down, ln2_w, ln2_b):
    batch, seq, d = x.shape
    depth = w_in.shape[0]
    n = batch * seq
    alpha = (2 * depth) ** 0.25
    hg_w = hg_norm_w.shape[1]
    gm_w = gm_norm_w.shape[1]
    nsa_w = nsa_norm_w.shape[1]
    kv_w = NSA_KV_GROUPS * HEAD_DIM
    in_width = w_in.shape[2]
    off_gm = 4 * hg_w
    off_q = off_gm + 2 * gm_w
    off_kv = off_q + nsa_w
    off_gate = off_kv + 6 * kv_w
    width_pad = -(-in_width // LANES) * LANES

    cosf, sinf, cos_t, sin_t = rope_tables(positions)
    lb_all = jnp.cumsum(jax.nn.softmax(hg_lower_bounds.astype(F32), axis=0), axis=0)
    lb_all = lb_all - lb_all[0:1]

    x2d = x.reshape(n, d)
    for l in range(depth):
        w_l = jnp.pad(w_in[l], ((0, 0), (0, width_pad - in_width))).astype(BF16)
        h = in_proj(x2d, w_l)
        h3 = h.reshape(batch, seq, width_pad)
        y_hg = hgrn2(h3, lb_all[l], hg_norm_w[l]).reshape(n, hg_w)
        y_gm = gmlp(h, gm_ln_w[l], gm_ln_b[l], gm_spatial_w[l], gm_spatial_b[l], gm_norm_w[l],
                    off_gm // gm_w, off_gm // gm_w + 1)
        kc = compress(h3[:, :, off_kv:off_kv + kv_w], nsa_cmp_pe[l, 0], nsa_cmp_w1[l, 0], nsa_cmp_w2[l, 0], batch, seq)
        vc = compress(h3[:, :, off_kv + kv_w:off_kv + 2 * kv_w], nsa_cmp_pe[l, 1], nsa_cmp_w1[l, 1],
                      nsa_cmp_w2[l, 1], batch, seq)
        n_cmp = kc.shape[1]
        kc = kc.reshape(batch, n_cmp, NSA_KV_GROUPS, HEAD_DIM).transpose(0, 2, 1, 3)
        vc_t = vc.reshape(batch, n_cmp, NSA_KV_GROUPS, HEAD_DIM).transpose(0, 2, 3, 1)
        ksa, vs_t, kw, vw_t = nsa_kprep(h, cosf, sinf, batch, seq, (off_kv + 2 * kv_w) // LANES)
        y_nsa = nsa_attention(h, cos_t, sin_t, kc, vc_t, ksa, vs_t, kw, vw_t, nsa_norm_w[l], batch, seq,
                              off_q // nsa_w, off_gate // LANES)
        x1, x1a, x1b = out_proj_ln(x2d, y_hg, y_gm, y_nsa, w_out[l], ln1_w[l], ln1_b[l], alpha)
        x2d = moe_ffn_ln(x1, (x1a, x1b), router_w[l], router_b[l], exp_w_up, exp_b_up, exp_w_down, exp_b_down, l,
                         ln2_w[l], ln2_b[l], alpha)
    return x2d.reshape(batch, seq, d)
```

```python
import functools
import math

import numpy as np
import jax
import jax.numpy as jnp
from jax import lax
from jax.experimental import pallas as pl
from jax.experimental.pallas import tpu as pltpu
from jax.experimental.pallas import tpu_sc as plsc

F32 = jnp.float32
BF16 = jnp.bfloat16
HIGHEST = lax.Precision.HIGHEST

HEAD_DIM = 64
LANES = 128
VMEM_LIMIT = 48 * 1024 * 1024
EXPERT_VMEM_LIMIT = 56 * 1024 * 1024

HG_CHUNK = 64
GM_CHUNK = 128
GM_TILE_CHUNKS = 4
NSA_KV_GROUPS = 2
NSA_HPG = 4
CMP_BLOCK = 32
CMP_STRIDE = 16
CMP_HIDDEN = 128
SEL_BLOCK = 64
N_SEL = 16
WINDOW = 512
N_GATES = 3
IMP_FORCE = 1e9
FORCE_KEY = int(np.float32(IMP_FORCE).view(np.int32))
NEG_INF = -1e30
N_EXPERTS = 32
TOP_K = 4
SWIGLU_ALPHA = 1.702
SWIGLU_LIMIT = 7.0
EXPERT_BLOCK = 512
SC_ROW_WORDS = 256
SC_WINDOW = 128
ROPE_THETA = 10000.0
LOG2_E = 1.4426950408889634
LN_EPS = 1e-5
RMS_EPS = 1e-6
V_ROWS = HEAD_DIM + 16


def _params(*sem):
    return pltpu.CompilerParams(dimension_semantics=sem, vmem_limit_bytes=VMEM_LIMIT)


def _dot(a, b):
    return jnp.dot(a, b, preferred_element_type=F32)


def _dot_nt(a, b, precision=None):
    return lax.dot_general(a, b, (((1,), (1,)), ((), ())), precision=precision,
                           preferred_element_type=F32)


def _dot_tn(a, b):
    return lax.dot_general(a, b, (((0,), (0,)), ((), ())), preferred_element_type=F32)


def _sigmoid(x):
    return 1.0 / (1.0 + jnp.exp(-x))


def _gelu(x):
    return 0.5 * x * (1.0 + jnp.tanh(0.7978845608028654 * (x + 0.044715 * x * x * x)))


def _layer_norm(x, w, b):
    mu = jnp.mean(x, axis=-1, keepdims=True)
    xc = x - mu
    var = jnp.mean(xc * xc, axis=-1, keepdims=True)
    return xc * lax.rsqrt(var + LN_EPS) * w + b


def _pack_bf16_pairs(y):
    w = y.shape[1] // 2
    bits = pltpu.bitcast(y.astype(BF16).astype(F32), jnp.uint32)
    return lax.shift_right_logical(bits[:, :w], jnp.uint32(16)) | (bits[:, w:] & jnp.uint32(0xFFFF0000))


def _unpack_bf16_pairs(u):
    lo = pltpu.bitcast(lax.shift_left(u, jnp.uint32(16)), F32)
    hi = pltpu.bitcast(u & jnp.uint32(0xFFFF0000), F32)
    return lo, hi


def _store_word_tables(refs, packed):
    for j, ref in enumerate(refs):
        ref[...] = packed[:, j * SC_ROW_WORDS:(j + 1) * SC_ROW_WORDS]


def _head_mean_sq(o, bd_ones):
    sq = o * o
    hi = sq.astype(BF16)
    lo = (sq - hi.astype(F32)).astype(BF16)
    ones = bd_ones.astype(BF16)
    return (_dot(hi, ones) + _dot(lo, ones)) * (1.0 / HEAD_DIM)


def _rope_kernel(pos_ref, inv_ref, cos_ref, sin_ref, cost_ref, sint_ref):
    ang = inv_ref[...] * pos_ref[...]
    c = jnp.cos(ang)
    s = jnp.sin(ang)
    cos_t = jnp.concatenate([c, c], axis=0)
    sin_t = jnp.concatenate([-s, s], axis=0)
    cost_ref[...] = cos_t
    sint_ref[...] = sin_t
    cos_ref[...] = cos_t.T
    sin_ref[...] = sin_t.T


def rope_tables(positions):
    n = positions.size
    tile = min(n, 2048)
    posf = positions.reshape(1, n).astype(F32)
    inv = ROPE_THETA ** (-jnp.arange(0, HEAD_DIM, 2, dtype=F32) / HEAD_DIM)
    row = pl.BlockSpec((tile, HEAD_DIM), lambda i: (i, 0))
    rowt = pl.BlockSpec((HEAD_DIM, tile), lambda i: (0, i))
    return pl.pallas_call(
        _rope_kernel, grid=(n // tile,),
        in_specs=[pl.BlockSpec((1, tile), lambda i: (0, i)), pl.BlockSpec((HEAD_DIM // 2, 1), lambda i: (0, 0))],
        out_specs=[row, row, rowt, rowt],
        out_shape=[jax.ShapeDtypeStruct((n, HEAD_DIM), F32)] * 2 + [jax.ShapeDtypeStruct((HEAD_DIM, n), F32)] * 2,
        compiler_params=_params("parallel"), name="rope_tables",
    )(posf, inv.reshape(HEAD_DIM // 2, 1))


def _in_proj_kernel(x_ref, w_ref, h_ref):
    h_ref[...] = _dot(x_ref[...].astype(BF16), w_ref[...])


def in_proj(x2d, w_bf16):
    n, d = x2d.shape
    width = w_bf16.shape[1]
    tile = min(n, 512)
    return pl.pallas_call(
        _in_proj_kernel, grid=(n // tile,),
        in_specs=[pl.BlockSpec((tile, d), lambda i: (i, 0)), pl.BlockSpec((d, width), lambda i: (0, 0))],
        out_specs=pl.BlockSpec((tile, width), lambda i: (i, 0)),
        out_shape=jax.ShapeDtypeStruct((n, width), F32),
        compiler_params=_params("parallel"), name="in_proj")(x2d, w_bf16)


HG_LEVELS = (64, 32, 16, 8, 4, 2)
HG_BATCH = 8


def _hgrn_constants():
    c = HG_CHUNK
    t = np.arange(c)
    u = t[None, :]
    rows = [u <= t[:, None], u > t[:, None]]
    masks = [np.eye(c, dtype=bool)]
    for m in HG_LEVELS:
        ref = ((t // m) * m + m // 2 - 1)[:, None]
        second = (t % m >= m // 2)[:, None]
        rows.append(((u > ref) & (u <= t[:, None]) & second) | ((u > t[:, None]) & (u <= ref) & ~second))
        masks.append((t[:, None] // m == t[None, :] // m) & second & (t[None, :] % m < m // 2))
    pmat = np.concatenate(rows, axis=0).astype(np.float32)
    masks = np.stack([np.tile(mk, (1, 4)) for mk in masks]).astype(np.float32)
    return pmat, masks


def _hgrn_kernel(q_ref, f_ref, i_ref, g_ref, lb_ref, nw_ref, pmat_ref, masks_ref, bd_ref, hm_ref,
                 o_ref, state_ref):
    c = HG_CHUNK

    @pl.when(pl.program_id(1) == 0)
    def _():
        state_ref[...] = jnp.zeros_like(state_ref)

    lb = lb_ref[...]
    bd = bd_ref[...]
    hm = hm_ref[...]
    hm_tiles = [jnp.broadcast_to(hm[h:h + 1], (c, hm.shape[1])).astype(BF16) for h in range(4)]
    pmat = pmat_ref[...]
    a = jnp.log(lb)
    log1m = jnp.log(1.0 - lb)
    nb, _, w = q_ref.shape
    seqs = range(nb)

    fr = f_ref[...].reshape(nb * c, w)
    hq = q_ref[...].reshape(nb * c, w)
    v = i_ref[...].reshape(nb * c, w)
    qf = hq * _sigmoid(hq)
    log_sig = jnp.minimum(fr, 0.0) - jnp.log(1.0 + jnp.exp(-jnp.abs(fr)))
    cc = log1m + log_sig
    log_f = jnp.maximum(a, cc) + jnp.log(1.0 + jnp.exp(-jnp.abs(a - cc)))
    kk = (1.0 - lb) * _sigmoid(-fr)

    hi = log_f.astype(BF16)
    lo = (log_f - hi.astype(F32)).astype(BF16)

    def side_by_side(x):
        return jnp.concatenate([x[b * c:(b + 1) * c] for b in seqs], axis=1)

    sums = jnp.minimum(_dot(pmat, side_by_side(hi)) + _dot(pmat, side_by_side(lo)), 0.0)
    e_all = jnp.exp(sums)

    def e_rows(r, b):
        return e_all[r * c:(r + 1) * c, b * w:(b + 1) * w]

    def stacked(x):
        xb = x.astype(BF16)
        return jnp.concatenate([xb * hm_tiles[h] for h in range(4)], axis=0)

    q_s = [qf[b * c:(b + 1) * c] for b in seqs]
    k_s = [kk[b * c:(b + 1) * c] for b in seqs]
    v_s = [v[b * c:(b + 1) * c] for b in seqs]

    att = [masks_ref[0] * _dot_nt(q_s[b].astype(BF16), stacked(k_s[b])) for b in seqs]
    for li in range(len(HG_LEVELS)):
        for b in seqs:
            e_l = e_rows(2 + li, b)
            att[b] = att[b] + masks_ref[li + 1] * _dot_nt((q_s[b] * e_l).astype(BF16), stacked(k_s[b] * e_l))

    outs = []
    for b in seqs:
        e_b = e_rows(0, b)
        st = state_ref[b]
        o = _dot(att[b].astype(BF16), stacked(v_s[b])) + _dot_nt((q_s[b] * e_b).astype(BF16), st.astype(BF16))
        k_rest = (k_s[b] * e_rows(1, b)).astype(BF16)
        state_ref[b] = st * e_b[c - 1:c] + bd * _dot_tn(v_s[b].astype(BF16), k_rest)
        outs.append(o)

    o = jnp.concatenate(outs, axis=0)
    ms = _head_mean_sq(o, bd)
    y = o * lax.rsqrt(ms + RMS_EPS) * nw_ref[...] * _sigmoid(g_ref[...].reshape(nb * c, w))
    o_ref[...] = y.astype(o_ref.dtype).reshape(nb, c, w)


def hgrn2(h3, lb, norm_w):
    batch, seq, _ = h3.shape
    w = lb.shape[-1]
    c = HG_CHUNK
    nb = math.gcd(batch, HG_BATCH)
    pmat, masks = _hgrn_constants()
    lane_head = np.arange(w) // HEAD_DIM
    bd = (lane_head[:, None] == lane_head[None, :]).astype(np.float32)
    hm = (np.arange(4)[:, None] == lane_head[None, :]).astype(np.float32)

    def col(j):
        return pl.BlockSpec((nb, c, w), lambda b, i, j=j: (b, i, j))

    def const(shape):
        return pl.BlockSpec(shape, lambda b, i: (0,) * len(shape))

    return pl.pallas_call(
        _hgrn_kernel, grid=(batch // nb, seq // c),
        in_specs=[col(0), col(1), col(2), col(3), const((1, w)), const((1, w)),
                  const(pmat.shape), const(masks.shape), const(bd.shape), const(hm.shape)],
        out_specs=pl.BlockSpec((nb, c, w), lambda b, i: (b, i, 0)),
        out_shape=jax.ShapeDtypeStruct((batch, seq, w), BF16),
        scratch_shapes=[pltpu.VMEM((nb, w, w), F32)],
        compiler_params=_params("parallel", "arbitrary"), name="hgrn2",
    )(h3, h3, h3, h3, lb.reshape(1, w), norm_w.reshape(1, w), jnp.asarray(pmat, BF16), jnp.asarray(masks),
      jnp.asarray(bd), jnp.asarray(hm))


def _gmlp_kernel(u_ref, v_ref, lnw_ref, lnb_ref, ws_ref, bias_ref, nw_ref, bd_ref, hm_ref, o_ref):
    c = GM_CHUNK
    groups = ws_ref.shape[0]
    u = _gelu(u_ref[...])
    v = _layer_norm(_gelu(v_ref[...]), lnw_ref[...], lnb_ref[...])
    hm = hm_ref[...]
    bd = bd_ref[...]
    causal = lax.broadcasted_iota(jnp.int32, (c, c), 0) >= lax.broadcasted_iota(jnp.int32, (c, c), 1)
    w_cat = jnp.concatenate([jnp.where(causal, ws_ref[g], 0.0).astype(BF16) for g in range(groups)], axis=1)
    for j in range(u.shape[0] // c):
        rows = slice(j * c, (j + 1) * c)
        v_j = v[rows]
        v_bd = jnp.concatenate([v_j * hm[g:g + 1] for g in range(groups)], axis=0).astype(BF16)
        y = u[rows] * (bias_ref[...] + _dot(w_cat, v_bd))
        ms = _head_mean_sq(y, bd)
        o_ref[rows, :] = (y * lax.rsqrt(ms + RMS_EPS) * nw_ref[...]).astype(o_ref.dtype)


def gmlp(h, ln_w, ln_b, w_s, b_s, norm_w, u_col, v_col):
    n = h.shape[0]
    groups, c, _ = w_s.shape
    w = groups * HEAD_DIM
    lane_head = np.arange(w) // HEAD_DIM
    bd = (lane_head[:, None] == lane_head[None, :]).astype(np.float32)
    hm = (np.arange(groups)[:, None] == lane_head[None, :]).astype(np.float32)
    bias = jnp.repeat(b_s.T, HEAD_DIM, axis=1)

    def const(shape):
        return pl.BlockSpec(shape, lambda i: (0,) * len(shape))

    t = math.gcd(n, GM_TILE_CHUNKS * c)
    return pl.pallas_call(
        _gmlp_kernel, grid=(n // t,),
        in_specs=[pl.BlockSpec((t, w), lambda i: (i, u_col)), pl.BlockSpec((t, w), lambda i: (i, v_col)),
                  const((1, w)), const((1, w)), const(w_s.shape), const((c, w)), const((1, w)),
                  const(bd.shape), const(hm.shape)],
        out_specs=pl.BlockSpec((t, w), lambda i: (i, 0)),
        out_shape=jax.ShapeDtypeStruct((n, w), BF16),
        compiler_params=_params("parallel"), name="gmlp",
    )(h, h, ln_w.reshape(1, w), ln_b.reshape(1, w), w_s, bias, norm_w.reshape(1, w),
      jnp.asarray(bd), jnp.asarray(hm))


def _compress_kernel(u_ref, wtop_ref, wbot_ref, pe_ref, w2_ref, o_ref):
    u = u_ref[...].astype(BF16)
    wtop = wtop_ref[...]
    wbot = wbot_ref[...]
    pe = pe_ref[...].astype(BF16)
    const = _dot(pe[0:1], wtop) + _dot(pe[1:2], wbot)
    p = _dot(u, wtop)
    q = _dot(u, wbot)
    q_next = jnp.concatenate([q[1:], jnp.zeros_like(q[0:1])], axis=0)
    hid = _gelu(p + q_next + const)
    o_ref[...] = _dot(hid.astype(BF16), w2_ref[...]).astype(o_ref.dtype)


def compress(kv, pe, w1, w2, batch, seq):
    g = NSA_KV_GROUPS
    half = CMP_STRIDE
    units = seq // half
    gw = g * HEAD_DIM
    u = kv.reshape(batch * units, half * gw)
    eye = jnp.eye(g, dtype=F32)
    w1r = w1.reshape(2, half, HEAD_DIM, CMP_HIDDEN)
    wbd = jnp.einsum('hjdn,gk->hjgdkn', w1r, eye).reshape(2, half * gw, g * CMP_HIDDEN).astype(BF16)
    w2bd = jnp.einsum('nd,gk->gnkd', w2, eye).reshape(g * CMP_HIDDEN, gw).astype(BF16)
    pe2 = jnp.broadcast_to(pe.reshape(2, half, 1, HEAD_DIM), (2, half, g, HEAD_DIM)).reshape(2, half * gw)

    def const(shape):
        return pl.BlockSpec(shape, lambda b: (0,) * len(shape))

    out = pl.pallas_call(
        _compress_kernel, grid=(batch,),
        in_specs=[pl.BlockSpec((units, half * gw), lambda b: (b, 0)),
                  const(wbd.shape[1:]), const(wbd.shape[1:]), const(pe2.shape), const(w2bd.shape)],
        out_specs=pl.BlockSpec((units, gw), lambda b: (b, 0)),
        out_shape=jax.ShapeDtypeStruct((batch * units, gw), BF16),
        compiler_params=_params("parallel"), name="nsa_compress",
    )(u, wbd[0], wbd[1], pe2, w2bd)
    return out.reshape(batch, units, gw)


def _rot_half_pairs(x):
    lane = lax.broadcasted_iota(jnp.int32, x.shape, 1)
    fwd = pltpu.roll(x, 32, axis=1)
    bwd = pltpu.roll(x, 96, axis=1)
    return jnp.where((lane % HEAD_DIM) < HEAD_DIM // 2, bwd, fwd)


def _kprep_kernel(ks_ref, vs_ref, kw_ref, vw_ref, cos_ref, sin_ref, ksa_ref, vso_ref, kwo_ref, vwo_ref):
    t = ks_ref.shape[0]
    cos = cos_ref[...]
    sin = sin_ref[...]
    cos2 = jnp.concatenate([cos, cos], axis=1)
    sin2 = jnp.concatenate([sin, sin], axis=1)
    ks = ks_ref[...]
    kw = kw_ref[...]
    ks_r = ks * cos2 + _rot_half_pairs(ks) * sin2
    kw_r = kw * cos2 + _rot_half_pairs(kw) * sin2
    pos = pl.program_id(1) * t + lax.broadcasted_iota(jnp.int32, (t, HEAD_DIM), 0)
    onehot = (pos // SEL_BLOCK == lax.broadcasted_iota(jnp.int32, (t, HEAD_DIM), 1)).astype(F32)
    vs_t = vs_ref[...].T
    vw_t = vw_ref[...].T
    tail = (lax.broadcasted_iota(jnp.int32, (V_ROWS - HEAD_DIM, t), 0) == 0).astype(F32)
    for g in range(NSA_KV_GROUPS):
        sl = slice(g * HEAD_DIM, (g + 1) * HEAD_DIM)
        ksa_ref[g] = jnp.concatenate([ks_r[:, sl], onehot], axis=1).astype(BF16)
        vso_ref[g] = jnp.concatenate([vs_t[sl], tail], axis=0).astype(BF16)
        kwo_ref[g] = kw_r[:, sl].astype(BF16)
        vwo_ref[g] = jnp.concatenate([vw_t[sl], tail], axis=0).astype(BF16)


def nsa_kprep(h, cosf, sinf, batch, seq, col0):
    g = NSA_KV_GROUPS
    t = min(seq, 512)
    nt = seq // t

    def col(j):
        return pl.BlockSpec((t, LANES), lambda b, i, j=j: (b * nt + i, col0 + j))

    tab = pl.BlockSpec((t, HEAD_DIM), lambda b, i: (b * nt + i, 0))

    def out(wd):
        return pl.BlockSpec((None, g, t, wd), lambda b, i: (b, 0, i, 0))

    out_t = pl.BlockSpec((None, g, V_ROWS, t), lambda b, i: (b, 0, 0, i))
    k_shape = jax.ShapeDtypeStruct((batch, g, seq, HEAD_DIM), BF16)
    v_shape = jax.ShapeDtypeStruct((batch, g, V_ROWS, seq), BF16)
    return pl.pallas_call(
        _kprep_kernel, grid=(batch, nt),
        in_specs=[col(0), col(1), col(2), col(3), tab, tab],
        out_specs=[out(2 * HEAD_DIM), out_t, out(HEAD_DIM), out_t],
        out_shape=[jax.ShapeDtypeStruct((batch, g, seq, 2 * HEAD_DIM), BF16), v_shape, k_shape, v_shape],
        compiler_params=_params("parallel", "parallel"), name="nsa_kprep",
    )(h, h, h, h, cosf, sinf)


def _nsa_kernel(hq_ref, gate_ref, cos_ref, sin_ref, kc_ref, vc_ref, ksa_ref, vs_ref, kw_ref, vw_ref,
                ovl_ref, nw_ref, o_ref, *, tq, tk, n_sb):
    qi = pl.program_id(1)
    hpg = NSA_HPG
    groups = NSA_KV_GROUPS
    rows = hpg * tq
    t0 = qi * tq
    scale = 1.0 / math.sqrt(HEAD_DIM)
    half = HEAD_DIM // 2

    hq_t = hq_ref[...].T
    cos = cos_ref[...]
    sin = sin_ref[...]
    q_raw, q_rot = [], []
    for g in range(groups):
        raw_g, rot_g = [], []
        for h in range(hpg):
            r0 = (g * hpg + h) * HEAD_DIM
            qh = hq_t[r0:r0 + HEAD_DIM]
            swapped = jnp.concatenate([qh[half:], qh[:half]], axis=0)
            raw_g.append(qh * (scale * LOG2_E))
            rot_g.append((qh * cos + swapped * sin) * (scale * LOG2_E))
        q_raw.append(jnp.concatenate(raw_g, axis=1).astype(BF16))
        q_rot.append(jnp.concatenate(rot_g, axis=1))

    tpos = t0 + lax.broadcasted_iota(jnp.int32, (1, tq), 1)
    tpos_r = jnp.concatenate([tpos] * hpg, axis=1)

    def flash_step(s, v_t, m, acc):
        m_new = jnp.maximum(m, jnp.max(s, axis=0, keepdims=True))
        alpha = jnp.exp2(m - m_new)
        p = jnp.exp2(s - m_new).astype(BF16)
        return m_new, alpha * acc + _dot(v_t, p)

    init = (jnp.full((1, rows), NEG_INF, F32), jnp.zeros((V_ROWS, rows), F32))

    wk = WINDOW + tq
    kw0 = pl.multiple_of(jnp.maximum(t0 - WINDOW, 0), tq)
    kpos_w = kw0 + lax.broadcasted_iota(jnp.int32, (wk, 1), 0)
    mask_w = (kpos_w <= tpos_r) & (kpos_w > tpos_r - WINDOW)
    n_pad = jnp.maximum(WINDOW - 1 - tpos_r, 0).astype(F32)
    n_cmp = kc_ref.shape[1]
    cmp_end = lax.broadcasted_iota(jnp.int32, (n_cmp, 1), 0) * CMP_STRIDE + (CMP_BLOCK - 1)
    mask_c = cmp_end <= tpos_r
    blk = lax.broadcasted_iota(jnp.int32, (n_sb, tq), 0)
    cur = (t0 + lax.broadcasted_iota(jnp.int32, (n_sb, tq), 1)) // SEL_BLOCK
    o_w, o_c, imps = [], [], []
    for g in range(groups):
        s = _dot(kw_ref[g, pl.ds(kw0, wk), :], q_rot[g].astype(BF16))
        m_w, acc_w = flash_step(jnp.where(mask_w, s, NEG_INF), vw_ref[g, :, pl.ds(kw0, wk)], *init)
        m_f = jnp.where(n_pad > 0.0, jnp.maximum(m_w, 0.0), m_w)
        a_w = jnp.exp2(m_w - m_f)
        o_w.append(acc_w[0:HEAD_DIM] * (a_w / (acc_w[HEAD_DIM:HEAD_DIM + 1] * a_w + n_pad * jnp.exp2(-m_f))))

        s_c = jnp.where(mask_c, _dot(kc_ref[g], q_raw[g]), NEG_INF)
        e_c = jnp.exp2(s_c - jnp.max(s_c, axis=0, keepdims=True))
        p_c = jnp.where(mask_c, e_c * (1.0 / jnp.sum(e_c, axis=0, keepdims=True)), 0.0)
        o_c.append(_dot(vc_ref[g], p_c.astype(BF16)))

        p_sum = p_c[:, 0:tq]
        for h in range(1, hpg):
            p_sum = p_sum + p_c[:, h * tq:(h + 1) * tq]
        imp = jnp.dot(ovl_ref[...], p_sum, precision=HIGHEST, preferred_element_type=F32)
        key = pltpu.bitcast(jnp.maximum(imp, 0.0), jnp.int32)
        key = jnp.where((blk == 0) | (blk == cur) | (blk == cur - 1), FORCE_KEY, key)
        imps.append(jnp.where(blk > cur, -1, key))

    n_sel = min(N_SEL, n_sb)

    def bit_body(it, taus):
        bit = lax.shift_left(jnp.int32(1), 30 - it)
        out = []
        for g in range(groups):
            cand = taus[g] | bit
            cnt = jnp.sum(jnp.where(imps[g] >= cand, 1, 0), axis=0, keepdims=True)
            out.append(jnp.where(cnt >= n_sel, cand, taus[g]))
        return tuple(out)

    taus = lax.fori_loop(0, 31, bit_body, (jnp.zeros((1, tq), jnp.int32),) * groups)
    lower = (lax.broadcasted_iota(jnp.int32, (n_sb, n_sb), 0)
             > lax.broadcasted_iota(jnp.int32, (n_sb, n_sb), 1)).astype(BF16)
    q_aug = []
    for g in range(groups):
        above = imps[g] > taus[g]
        equal = imps[g] == taus[g]
        need = n_sel - jnp.sum(jnp.where(above, 1, 0), axis=0, keepdims=True)
        earlier = _dot(lower, jnp.where(equal, 1.0, 0.0).astype(BF16))
        selected = above | (equal & (earlier < need.astype(F32)))
        sel_bias = jnp.where(selected, 0.0, NEG_INF)
        if n_sb < HEAD_DIM:
            sel_bias = jnp.concatenate([sel_bias, jnp.zeros((HEAD_DIM - n_sb, tq), F32)], axis=0)
        q_aug.append(jnp.concatenate([q_rot[g], jnp.concatenate([sel_bias] * hpg, axis=1)], axis=0).astype(BF16))

    def scores(g, kt):
        k0 = pl.multiple_of(kt * tk, tk)
        return _dot(ksa_ref[g, pl.ds(k0, tk), :], q_aug[g])

    def sel_body(kt, carry):
        k0 = pl.multiple_of(kt * tk, tk)
        s = [scores(g, kt) for g in range(groups)]
        m_new = [jnp.maximum(carry[g][0], jnp.max(s[g], axis=0, keepdims=True)) for g in range(groups)]
        p = [jnp.exp2(s[g] - m_new[g]).astype(BF16) for g in range(groups)]
        return tuple((m_new[g], jnp.exp2(carry[g][0] - m_new[g]) * carry[g][1] + _dot(vs_ref[g, :, pl.ds(k0, tk)], p[g]))
                     for g in range(groups))

    n_full = t0 // tk
    carry = lax.fori_loop(0, n_full, sel_body, (init,) * groups)
    k0 = pl.multiple_of(n_full * tk, tk)
    kpos = k0 + lax.broadcasted_iota(jnp.int32, (tk, 1), 0)
    mask_s = kpos <= tpos_r

    gates = _sigmoid(gate_ref[...].T)
    nw = nw_ref[...]
    outs = []
    for g in range(groups):
        _, acc_s = flash_step(jnp.where(mask_s, scores(g, n_full), NEG_INF), vs_ref[g, :, pl.ds(k0, tk)], *carry[g])
        o_s = acc_s[0:HEAD_DIM] * (1.0 / acc_s[HEAD_DIM:HEAD_DIM + 1])
        for h in range(hpg):
            sl = slice(h * tq, (h + 1) * tq)
            r = (g * hpg + h) * N_GATES
            o = gates[r:r + 1] * o_c[g][:, sl] + gates[r + 1:r + 2] * o_s[:, sl] + gates[r + 2:r + 3] * o_w[g][:, sl]
            ms = jnp.mean(o * o, axis=0, keepdims=True)
            outs.append(o * lax.rsqrt(ms + RMS_EPS) * nw[:, g * hpg + h:g * hpg + h + 1])
    o_ref[...] = jnp.concatenate(outs, axis=0).T.astype(o_ref.dtype)


def nsa_attention(h, cos_t, sin_t, kc, vc_t, ksa, vs_t, kw, vw_t, norm_w, batch, seq, q_col0, gate_col):
    g, hpg = NSA_KV_GROUPS, NSA_HPG
    tq = min(seq, 256)
    nq = seq // tq
    n_sb = seq // SEL_BLOCK
    assert n_sb <= HEAD_DIM, "selection-block one-hot shares the 64 spare key lanes"
    n_cmp = kc.shape[2]
    units = np.arange(n_cmp)[:, None] + np.arange(CMP_BLOCK // CMP_STRIDE)[None, :]
    ovl = np.zeros((n_cmp, n_sb), np.float32)
    for c in range((seq - CMP_BLOCK) // CMP_STRIDE + 1):
        for u in units[c]:
            ovl[c, u // (SEL_BLOCK // CMP_STRIDE)] += 1.0
    ovl_t = jnp.asarray(ovl.T)

    def per_b(shape):
        return pl.BlockSpec((None, g) + shape, lambda b, qi: (b, 0, 0, 0))

    width = g * hpg * HEAD_DIM
    tab = pl.BlockSpec((HEAD_DIM, tq), lambda b, qi: (0, b * nq + qi))
    tk = min(seq, 512)
    assert seq >= WINDOW + tq and seq % tk == 0 and tk % tq == 0
    kern = functools.partial(_nsa_kernel, tq=tq, tk=tk, n_sb=n_sb)
    return pl.pallas_call(
        kern, grid=(batch, nq),
        in_specs=[pl.BlockSpec((tq, width), lambda b, qi: (b * nq + qi, q_col0)),
                  pl.BlockSpec((tq, LANES), lambda b, qi: (b * nq + qi, gate_col)),
                  tab, tab,
                  per_b((n_cmp, HEAD_DIM)), per_b((HEAD_DIM, n_cmp)),
                  per_b((seq, 2 * HEAD_DIM)), per_b((V_ROWS, seq)),
                  per_b((seq, HEAD_DIM)), per_b((V_ROWS, seq)),
                  pl.BlockSpec((n_sb, n_cmp), lambda b, qi: (0, 0)),
                  pl.BlockSpec((HEAD_DIM, g * hpg), lambda b, qi: (0, 0))],
        out_specs=pl.BlockSpec((tq, width), lambda b, qi: (b * nq + qi, 0)),
        out_shape=jax.ShapeDtypeStruct((batch * seq, width), BF16),
        compiler_params=_params("parallel", "arbitrary"), name="nsa_attention",
    )(h, h, cos_t, sin_t, kc, vc_t, ksa, vs_t, kw, vw_t, ovl_t, norm_w.reshape(g * hpg, HEAD_DIM).T)


def _out_proj_kernel(x_ref, yhg_ref, ygm_ref, ynsa_ref, whg_ref, wgm_ref, wnsa_ref, lnw_ref, lnb_ref,
                     o_ref, oa_ref, ob_ref, *, alpha):
    mix = (_dot(yhg_ref[...], whg_ref[...]) + _dot(ygm_ref[...], wgm_ref[...])
           + _dot(ynsa_ref[...], wnsa_ref[...]))
    y = _layer_norm(alpha * x_ref[...] + mix, lnw_ref[...], lnb_ref[...])
    o_ref[...] = y
    _store_word_tables((oa_ref, ob_ref), _pack_bf16_pairs(y))


def out_proj_ln(x2d, y_hg, y_gm, y_nsa, w_out, ln_w, ln_b, alpha):
    n, d = x2d.shape
    w1, w2 = y_hg.shape[1], y_hg.shape[1] + y_gm.shape[1]
    whg = w_out[:w1].astype(BF16)
    wgm = w_out[w1:w2].astype(BF16)
    wnsa = w_out[w2:].astype(BF16)
    t = min(n, 512)

    def row(wd):
        return pl.BlockSpec((t, wd), lambda i: (i, 0))

    def const(shape):
        return pl.BlockSpec(shape, lambda i: (0,) * len(shape))

    kern = functools.partial(_out_proj_kernel, alpha=alpha)
    return pl.pallas_call(
        kern, grid=(n // t,),
        in_specs=[row(d), row(y_hg.shape[1]), row(y_gm.shape[1]), row(y_nsa.shape[1]),
                  const(whg.shape), const(wgm.shape), const(wnsa.shape), const((1, d)), const((1, d))],
        out_specs=[row(d), row(SC_ROW_WORDS), row(SC_ROW_WORDS)],
        out_shape=[jax.ShapeDtypeStruct((n, d), F32)] + [jax.ShapeDtypeStruct((n, SC_ROW_WORDS), jnp.uint32)] * 2,
        compiler_params=_params("parallel"), name="out_proj_ln",
    )(x2d, y_hg, y_gm, y_nsa, whg, wgm, wnsa, ln_w.reshape(1, d), ln_b.reshape(1, d))


def _router_kernel(x_ref, w_ref, b_ref, e_ref, p_ref, r_ref, cnt_ref, carry_ref):
    t = x_ref.shape[0]

    @pl.when(pl.program_id(0) == 0)
    def _():
        carry_ref[...] = jnp.zeros_like(carry_ref)

    x = x_ref[...]
    x_hi = x.astype(BF16)
    x_lo = (x - x_hi.astype(F32)).astype(BF16)
    w = w_ref[...]
    w_hi = w.astype(BF16)
    w_lo = (w - w_hi.astype(F32)).astype(BF16)
    logits = _dot(x_hi, w_hi) + (_dot(x_lo, w_hi) + _dot(x_hi, w_lo)) + b_ref[...]
    lane = lax.broadcasted_iota(jnp.int32, logits.shape, 1)
    work = logits
    vals, idxs = [], []
    sel = jnp.zeros(logits.shape, F32)
    for _ in range(TOP_K):
        m = jnp.max(work, axis=-1, keepdims=True)
        idx = jnp.min(jnp.where(work == m, lane, LANES), axis=-1, keepdims=True)
        hit = lane == idx
        sel = jnp.where(hit, 1.0, sel)
        work = jnp.where(hit, -jnp.inf, work)
        vals.append(m)
        idxs.append(idx)
    exps = [jnp.exp(v - vals[0]) for v in vals]
    den = exps[0] + exps[1] + exps[2] + exps[3]
    strict = (lax.broadcasted_iota(jnp.int32, (t, t), 0) > lax.broadcasted_iota(jnp.int32, (t, t), 1))
    before = _dot(strict.astype(BF16), sel.astype(BF16)) + carry_ref[...]
    ranks = [jnp.sum(jnp.where(lane == idx, before, 0.0), axis=-1, keepdims=True) for idx in idxs]
    kcol = lax.broadcasted_iota(jnp.int32, (t, TOP_K), 1)
    e_out = jnp.zeros((t, TOP_K), jnp.int32)
    p_out = jnp.zeros((t, TOP_K), F32)
    r_out = jnp.zeros((t, TOP_K), jnp.int32)
    for k in range(TOP_K):
        e_out = jnp.where(kcol == k, idxs[k], e_out)
        p_out = jnp.where(kcol == k, exps[k] / den, p_out)
        r_out = jnp.where(kcol == k, ranks[k].astype(jnp.int32), r_out)
    e_ref[...] = e_out
    p_ref[...] = p_out
    r_ref[...] = r_out
    carry_ref[...] = carry_ref[...] + jnp.sum(sel, axis=0, keepdims=True)
    cnt_ref[...] = carry_ref[...].astype(jnp.int32)


def moe_router(x2d, router_w, router_b):
    n, d = x2d.shape
    e = router_w.shape[1]
    t = min(n, 512)
    w = jnp.zeros((d, LANES), F32).at[:, :e].set(router_w)
    b = jnp.full((1, LANES), NEG_INF, F32).at[0, :e].set(router_b)
    row4 = pl.BlockSpec((t, TOP_K), lambda i: (i, 0))
    top_e, top_p, rank, counts = pl.pallas_call(
        _router_kernel, grid=(n // t,),
        in_specs=[pl.BlockSpec((t, d), lambda i: (i, 0)), pl.BlockSpec((d, LANES), lambda i: (0, 0)),
                  pl.BlockSpec((1, LANES), lambda i: (0, 0))],
        out_specs=[row4, row4, row4, pl.BlockSpec((1, LANES), lambda i: (0, 0))],
        out_shape=[jax.ShapeDtypeStruct((n, TOP_K), jnp.int32), jax.ShapeDtypeStruct((n, TOP_K), F32),
                   jax.ShapeDtypeStruct((n, TOP_K), jnp.int32), jax.ShapeDtypeStruct((1, LANES), jnp.int32)],
        scratch_shapes=[pltpu.VMEM((1, LANES), F32)],
        compiler_params=_params("arbitrary"), name="moe_router",
    )(x2d, w, b)
    return top_e, top_p, rank, counts[0, :e]


def _expert_kernel(be_ref, valid_ref, xa_ref, xb_ref, wu_ref, bu_ref, wd_ref, bd_ref, oa_ref, ob_ref,
                   wu_bf, wd_bf):
    i = pl.program_id(0)
    f = wd_ref.shape[0]
    n_used = be_ref[pl.num_programs(0)]

    @pl.when((i == 0) | (be_ref[i] != be_ref[jnp.maximum(i - 1, 0)]))
    def _():
        wu_bf[...] = wu_ref[...].astype(BF16)
        wd_bf[...] = wd_ref[...].astype(BF16)

    @pl.when(i < n_used)
    def _():
        packed = jnp.concatenate([xa_ref[...], xb_ref[...]], axis=1)
        live = lax.broadcasted_iota(jnp.int32, packed.shape, 0) < valid_ref[i]
        x_lo, x_hi = _unpack_bf16_pairs(jnp.where(live, packed, jnp.uint32(0)))
        x = jnp.concatenate([x_lo.astype(BF16), x_hi.astype(BF16)], axis=1)
        hcat = _dot(x, wu_bf[...]) + bu_ref[...]
        glu = jnp.minimum(hcat[:, :f], SWIGLU_LIMIT)
        lin = jnp.clip(hcat[:, f:], -SWIGLU_LIMIT, SWIGLU_LIMIT)
        act = glu * _sigmoid(SWIGLU_ALPHA * glu) * (lin + 1.0)
        _store_word_tables((oa_ref, ob_ref), _pack_bf16_pairs(_dot(act.astype(BF16), wd_bf[...]) + bd_ref[...]))

    @pl.when(i >= n_used)
    def _():
        oa_ref[...] = jnp.zeros_like(oa_ref)
        ob_ref[...] = jnp.zeros_like(ob_ref)


def moe_experts(xa, xb, block_e, n_used, block_valid, w_up, b_up, w_down, b_down, layer):
    rows = xa.shape[0]
    _, e, d, f2 = w_up.shape
    f = f2 // 2
    nb = rows // EXPERT_BLOCK
    words = pl.BlockSpec((EXPERT_BLOCK, SC_ROW_WORDS), lambda i, be, nv: (i, 0))
    grid_spec = pltpu.PrefetchScalarGridSpec(
        num_scalar_prefetch=2, grid=(nb,),
        in_specs=[words, words,
                  pl.BlockSpec((None, None, d, f2), lambda i, be, nv: (layer, be[i], 0, 0)),
                  pl.BlockSpec((None, None, 1, f2), lambda i, be, nv: (layer, be[i], 0, 0)),
                  pl.BlockSpec((None, None, f, d), lambda i, be, nv: (layer, be[i], 0, 0)),
                  pl.BlockSpec((None, None, 1, d), lambda i, be, nv: (layer, be[i], 0, 0))],
        out_specs=[words, words],
        scratch_shapes=[pltpu.VMEM((d, f2), BF16), pltpu.VMEM((f, d), BF16)])
    depth = w_up.shape[0]
    return pl.pallas_call(
        _expert_kernel, grid_spec=grid_spec,
        out_shape=[jax.ShapeDtypeStruct((rows, SC_ROW_WORDS), jnp.uint32)] * 2,
        compiler_params=pltpu.CompilerParams(dimension_semantics=("arbitrary",), vmem_limit_bytes=EXPERT_VMEM_LIMIT),
        name="moe_experts",
    )(jnp.concatenate([block_e, n_used.reshape(1)]), block_valid, xa, xb, w_up, b_up.reshape(depth, e, 1, f2),
      w_down, b_down.reshape(depth, e, 1, d))


def _combine_kernel(x_ref, ya_ref, yb_ref, p_ref, lnw_ref, lnb_ref, o_ref, *, alpha):
    p = p_ref[...]
    moe = jnp.zeros(x_ref.shape, F32)
    for k in range(TOP_K):
        y_lo, y_hi = _unpack_bf16_pairs(jnp.concatenate([ya_ref[k], yb_ref[k]], axis=1))
        moe = moe + p[:, k:k + 1] * jnp.concatenate([y_lo, y_hi], axis=1)
    o_ref[...] = _layer_norm(alpha * x_ref[...] + moe, lnw_ref[...], lnb_ref[...])


def combine_ln(x2d, ya, yb, top_p, ln_w, ln_b, alpha):
    n, d = x2d.shape
    t = min(n, 256)
    kern = functools.partial(_combine_kernel, alpha=alpha)
    words = pl.BlockSpec((TOP_K, t, SC_ROW_WORDS), lambda i: (0, i, 0))
    return pl.pallas_call(
        kern, grid=(n // t,),
        in_specs=[pl.BlockSpec((t, d), lambda i: (i, 0)), words, words,
                  pl.BlockSpec((t, TOP_K), lambda i: (i, 0)),
                  pl.BlockSpec((1, d), lambda i: (0, 0)), pl.BlockSpec((1, d), lambda i: (0, 0))],
        out_specs=pl.BlockSpec((t, d), lambda i: (i, 0)),
        out_shape=jax.ShapeDtypeStruct((n, d), F32),
        compiler_params=_params("parallel"), name="moe_combine_ln",
    )(x2d, ya, yb, top_p, ln_w.reshape(1, d), ln_b.reshape(1, d))


def _sc_mesh():
    return plsc.VectorSubcoreMesh(core_axis_name="core", subcore_axis_name="subcore")


def sc_gather_rows(tables, idx):
    r = idx.shape[0]
    nt = len(tables)
    out = jax.ShapeDtypeStruct((r, SC_ROW_WORDS), tables[0].dtype)

    @pl.kernel(out_type=(out,) * nt, mesh=_sc_mesh(), name="sc_gather_rows")
    def gather(*refs):
        x_hbm, i_hbm, o_hbm = refs[:nt], refs[nt], refs[nt + 1:]
        for j in range(nt):
            def body(i_vmem, o_vmem, table=x_hbm[j]):
                pltpu.sync_copy(table.at[i_vmem.at[0]], o_vmem)

            pltpu.emit_pipeline(
                body, grid=(r // SC_WINDOW,),
                in_specs=[pl.BlockSpec((1, SC_WINDOW), lambda i: (0, i))],
                out_specs=[pl.BlockSpec((SC_WINDOW, SC_ROW_WORDS), lambda i: (i, 0))],
                core_axis_name=("core", "subcore"), dimension_semantics=(pltpu.PARALLEL,),
            )(i_hbm, o_hbm[j])

    return gather(*tables, idx.reshape(1, r))


def sc_scatter_rows(tables, dest_t, n_rows):
    n = tables[0].shape[0]
    nt = len(tables)
    copies = dest_t.shape[0]
    out = jax.ShapeDtypeStruct((n_rows, SC_ROW_WORDS), tables[0].dtype)

    @pl.kernel(out_type=(out,) * nt, mesh=_sc_mesh(), scratch_types=[], name="sc_scatter_rows")
    def scatter(*refs):
        x_hbm, i_hbm, o_hbm = refs[:nt], refs[nt], refs[nt + 1:]
        for j in range(nt):
            def body(x_vmem, i_vmem, out_j=o_hbm[j]):
                for k in range(copies):
                    pltpu.sync_copy(x_vmem, out_j.at[i_vmem.at[k]])

            pltpu.emit_pipeline(
                body, grid=(n // SC_WINDOW,),
                in_specs=[pl.BlockSpec((SC_WINDOW, SC_ROW_WORDS), lambda i: (i, 0)),
                          pl.BlockSpec((copies, SC_WINDOW), lambda i: (0, i))],
                out_specs=[],
                core_axis_name=("core", "subcore"), dimension_semantics=(pltpu.PARALLEL,),
            )(x_hbm[j], i_hbm)

    return scatter(*tables, dest_t)


def moe_ffn_ln(x_f32, x_packed, router_w, router_b, w_up, b_up, w_down, b_down, layer, ln_w, ln_b, alpha):
    n, d = x_f32.shape
    top_e, top_p, rank, counts = moe_router(x_f32, router_w, router_b)
    padded = (counts + EXPERT_BLOCK - 1) // EXPERT_BLOCK * EXPERT_BLOCK
    pad_end = jnp.cumsum(padded)
    pad_start = pad_end - padded
    n_assign = n * TOP_K
    n_blocks = -(-(n_assign + N_EXPERTS * (EXPERT_BLOCK - 1)) // EXPERT_BLOCK)
    dest_t = (pad_start[top_e] + rank).T
    block_first = jnp.arange(n_blocks, dtype=jnp.int32) * EXPERT_BLOCK
    block_e = jnp.clip(jnp.sum((pad_end[None, :] <= block_first[:, None]).astype(jnp.int32), axis=1),
                       0, N_EXPERTS - 1)
    block_valid = jnp.clip(counts[block_e] - (block_first - pad_start[block_e]), 0, EXPERT_BLOCK)
    n_used = (pad_end[-1] // EXPERT_BLOCK).astype(jnp.int32)
    xa, xb = sc_scatter_rows(x_packed, dest_t, n_blocks * EXPERT_BLOCK)
    ya, yb = moe_experts(xa, xb, block_e, n_used, block_valid.astype(jnp.int32), w_up, b_up, w_down, b_down,
                         layer)
    ya, yb = sc_gather_rows((ya, yb), dest_t.reshape(-1))
    return combine_ln(x_f32, ya.reshape(TOP_K, n, SC_ROW_WORDS), yb.reshape(TOP_K, n, SC_ROW_WORDS), top_p,
                      ln_w, ln_b, alpha)


def kernel(x, positions, w_in, hg_lower_bounds, hg_norm_w, gm_ln_w, gm_ln_b, gm_spatial_w, gm_spatial_b, gm_norm_w, nsa_cmp_pe, nsa_cmp_w1, nsa_cmp_w2, nsa_norm_w, w_out, ln1_w, ln1_b, router_w, router_b, exp_w_up, exp_b_up, exp_w_down, exp_b_down, ln2_w, ln2_b):
    batch, seq, d = x.shape
    depth = w_in.shape[0]
    n = batch * seq
    alpha = (2 * depth) ** 0.25
    hg_w = hg_norm_w.shape[1]
    gm_w = gm_norm_w.shape[1]
    nsa_w = nsa_norm_w.shape[1]
    kv_w = NSA_KV_GROUPS * HEAD_DIM
    in_width = w_in.shape[2]
    off_gm = 4 * hg_w
    off_q = off_gm + 2 * gm_w
    off_kv = off_q + nsa_w
    off_gate = off_kv + 6 * kv_w
    width_pad = -(-in_width // LANES) * LANES

    cosf, sinf, cos_t, sin_t = rope_tables(positions)
    lb_all = jnp.cumsum(jax.nn.softmax(hg_lower_bounds.astype(F32), axis=0), axis=0)
    lb_all = lb_all - lb_all[0:1]

    x2d = x.reshape(n, d)
    for l in range(depth):
        w_l = jnp.pad(w_in[l], ((0, 0), (0, width_pad - in_width))).astype(BF16)
        h = in_proj(x2d, w_l)
        h3 = h.reshape(batch, seq, width_pad)
        y_hg = hgrn2(h3, lb_all[l], hg_norm_w[l]).reshape(n, hg_w)
        y_gm = gmlp(h, gm_ln_w[l], gm_ln_b[l], gm_spatial_w[l], gm_spatial_b[l], gm_norm_w[l],
                    off_gm // gm_w, off_gm // gm_w + 1)
        kc = compress(h3[:, :, off_kv:off_kv + kv_w], nsa_cmp_pe[l, 0], nsa_cmp_w1[l, 0], nsa_cmp_w2[l, 0], batch, seq)
        vc = compress(h3[:, :, off_kv + kv_w:off_kv + 2 * kv_w], nsa_cmp_pe[l, 1], nsa_cmp_w1[l, 1],
                      nsa_cmp_w2[l, 1], batch, seq)
        n_cmp = kc.shape[1]
        kc = kc.reshape(batch, n_cmp, NSA_KV_GROUPS, HEAD_DIM).transpose(0, 2, 1, 3)
        vc_t = vc.reshape(batch, n_cmp, NSA_KV_GROUPS, HEAD_DIM).transpose(0, 2, 3, 1)
        ksa, vs_t, kw, vw_t = nsa_kprep(h, cosf, sinf, batch, seq, (off_kv + 2 * kv_w) // LANES)
        y_nsa = nsa_attention(h, cos_t, sin_t, kc, vc_t, ksa, vs_t, kw, vw_t, nsa_norm_w[l], batch, seq,
                              off_q // nsa_w, off_gate // LANES)
        x1, x1a, x1b = out_proj_ln(x2d, y_hg, y_gm, y_nsa, w_out[l], ln1_w[l], ln1_b[l], alpha)
        x2d = moe_ffn_ln(x1, (x1a, x1b), router_w[l], router_b[l], exp_w_up, exp_b_up, exp_w_down, exp_b_down, l,
                         ln2_w[l], ln2_b[l], alpha)
    return x2d.reshape(batch, seq, d)
```

```python
import functools
import math

import numpy as np
import jax
import jax.numpy as jnp
from jax import lax
from jax.experimental import pallas as pl
from jax.experimental.pallas import tpu as pltpu
from jax.experimental.pallas import tpu_sc as plsc

F32 = jnp.float32
BF16 = jnp.bfloat16
HIGHEST = lax.Precision.HIGHEST

HEAD_DIM = 64
LANES = 128
VMEM_LIMIT = 48 * 1024 * 1024
EXPERT_VMEM_LIMIT = 56 * 1024 * 1024

HG_CHUNK = 64
GM_CHUNK = 128
GM_TILE_CHUNKS = 4
NSA_KV_GROUPS = 2
NSA_HPG = 4
CMP_BLOCK = 32
CMP_STRIDE = 16
CMP_HIDDEN = 128
SEL_BLOCK = 64
N_SEL = 16
WINDOW = 512
N_GATES = 3
IMP_FORCE = 1e9
FORCE_KEY = int(np.float32(IMP_FORCE).view(np.int32))
NEG_INF = -1e30
N_EXPERTS = 32
TOP_K = 4
SWIGLU_ALPHA = 1.702
SWIGLU_LIMIT = 7.0
EXPERT_BLOCK = 512
SC_ROW_WORDS = 256
SC_WINDOW = 128
ROPE_THETA = 10000.0
LOG2_E = 1.4426950408889634
LN_EPS = 1e-5
RMS_EPS = 1e-6
V_ROWS = HEAD_DIM + 16


def _params(*sem):
    return pltpu.CompilerParams(dimension_semantics=sem, vmem_limit_bytes=VMEM_LIMIT)


def _dot(a, b):
    return jnp.dot(a, b, preferred_element_type=F32)


def _dot_nt(a, b, precision=None):
    return lax.dot_general(a, b, (((1,), (1,)), ((), ())), precision=precision,
                           preferred_element_type=F32)


def _dot_tn(a, b):
    return lax.dot_general(a, b, (((0,), (0,)), ((), ())), preferred_element_type=F32)


def _sigmoid(x):
    return 1.0 / (1.0 + jnp.exp(-x))


def _gelu(x):
    return 0.5 * x * (1.0 + jnp.tanh(0.7978845608028654 * (x + 0.044715 * x * x * x)))


def _layer_norm(x, w, b):
    mu = jnp.mean(x, axis=-1, keepdims=True)
    xc = x - mu
    var = jnp.mean(xc * xc, axis=-1, keepdims=True)
    return xc * lax.rsqrt(var + LN_EPS) * w + b


def _pack_bf16_pairs(y):
    w = y.shape[1] // 2
    bits = pltpu.bitcast(y.astype(BF16).astype(F32), jnp.uint32)
    return lax.shift_right_logical(bits[:, :w], jnp.uint32(16)) | (bits[:, w:] & jnp.uint32(0xFFFF0000))


def _unpack_bf16_pairs(u):
    lo = pltpu.bitcast(lax.shift_left(u, jnp.uint32(16)), F32)
    hi = pltpu.bitcast(u & jnp.uint32(0xFFFF0000), F32)
    return lo, hi


def _store_word_tables(refs, packed):
    for j, ref in enumerate(refs):
        ref[...] = packed[:, j * SC_ROW_WORDS:(j + 1) * SC_ROW_WORDS]


def _head_mean_sq(o, bd_ones):
    sq = o * o
    hi = sq.astype(BF16)
    lo = (sq - hi.astype(F32)).astype(BF16)
    ones = bd_ones.astype(BF16)
    return (_dot(hi, ones) + _dot(lo, ones)) * (1.0 / HEAD_DIM)


def _rope_kernel(pos_ref, inv_ref, cos_ref, sin_ref, cost_ref, sint_ref):
    ang = inv_ref[...] * pos_ref[...]
    c = jnp.cos(ang)
    s = jnp.sin(ang)
    cos_t = jnp.concatenate([c, c], axis=0)
    sin_t = jnp.concatenate([-s, s], axis=0)
    cost_ref[...] = cos_t
    sint_ref[...] = sin_t
    cos_ref[...] = cos_t.T
    sin_ref[...] = sin_t.T


def rope_tables(positions):
    n = positions.size
    tile = min(n, 2048)
    posf = positions.reshape(1, n).astype(F32)
    inv = ROPE_THETA ** (-jnp.arange(0, HEAD_DIM, 2, dtype=F32) / HEAD_DIM)
    row = pl.BlockSpec((tile, HEAD_DIM), lambda i: (i, 0))
    rowt = pl.BlockSpec((HEAD_DIM, tile), lambda i: (0, i))
    return pl.pallas_call(
        _rope_kernel, grid=(n // tile,),
        in_specs=[pl.BlockSpec((1, tile), lambda i: (0, i)), pl.BlockSpec((HEAD_DIM // 2, 1), lambda i: (0, 0))],
        out_specs=[row, row, rowt, rowt],
        out_shape=[jax.ShapeDtypeStruct((n, HEAD_DIM), F32)] * 2 + [jax.ShapeDtypeStruct((HEAD_DIM, n), F32)] * 2,
        compiler_params=_params("parallel"), name="rope_tables",
    )(posf, inv.reshape(HEAD_DIM // 2, 1))


def _in_proj_kernel(x_ref, w_ref, h_ref):
    h_ref[...] = _dot(x_ref[...].astype(BF16), w_ref[...])


def in_proj(x2d, w_bf16):
    n, d = x2d.shape
    width = w_bf16.shape[1]
    tile = min(n, 512)
    return pl.pallas_call(
        _in_proj_kernel, grid=(n // tile,),
        in_specs=[pl.BlockSpec((tile, d), lambda i: (i, 0)), pl.BlockSpec((d, width), lambda i: (0, 0))],
        out_specs=pl.BlockSpec((tile, width), lambda i: (i, 0)),
        out_shape=jax.ShapeDtypeStruct((n, width), F32),
        compiler_params=_params("parallel"), name="in_proj")(x2d, w_bf16)


HG_LEVELS = (64, 32, 16, 8, 4, 2)
HG_BATCH = 8


def _hgrn_constants():
    c = HG_CHUNK
    t = np.arange(c)
    u = t[None, :]
    rows = [u <= t[:, None], u > t[:, None]]
    masks = [np.eye(c, dtype=bool)]
    for m in HG_LEVELS:
        ref = ((t // m) * m + m // 2 - 1)[:, None]
        second = (t % m >= m // 2)[:, None]
        rows.append(((u > ref) & (u <= t[:, None]) & second) | ((u > t[:, None]) & (u <= ref) & ~second))
        masks.append((t[:, None] // m == t[None, :] // m) & second & (t[None, :] % m < m // 2))
    pmat = np.concatenate(rows, axis=0).astype(np.float32)
    masks = np.stack([np.tile(mk, (1, 4)) for mk in masks]).astype(np.float32)
    return pmat, masks


def _hgrn_kernel(q_ref, f_ref, i_ref, g_ref, lb_ref, nw_ref, pmat_ref, masks_ref, bd_ref, hm_ref,
                 o_ref, state_ref):
    c = HG_CHUNK

    @pl.when(pl.program_id(1) == 0)
    def _():
        state_ref[...] = jnp.zeros_like(state_ref)

    lb = lb_ref[...]
    bd = bd_ref[...]
    hm = hm_ref[...]
    hm_tiles = [jnp.broadcast_to(hm[h:h + 1], (c, hm.shape[1])).astype(BF16) for h in range(4)]
    pmat = pmat_ref[...]
    a = jnp.log(lb)
    log1m = jnp.log(1.0 - lb)
    nb, _, w = q_ref.shape
    seqs = range(nb)

    fr = f_ref[...].reshape(nb * c, w)
    hq = q_ref[...].reshape(nb * c, w)
    v = i_ref[...].reshape(nb * c, w)
    qf = hq * _sigmoid(hq)
    log_sig = jnp.minimum(fr, 0.0) - jnp.log(1.0 + jnp.exp(-jnp.abs(fr)))
    cc = log1m + log_sig
    log_f = jnp.maximum(a, cc) + jnp.log(1.0 + jnp.exp(-jnp.abs(a - cc)))
    kk = (1.0 - lb) * _sigmoid(-fr)

    hi = log_f.astype(BF16)
    lo = (log_f - hi.astype(F32)).astype(BF16)

    def side_by_side(x):
        return jnp.concatenate([x[b * c:(b + 1) * c] for b in seqs], axis=1)

    sums = jnp.minimum(_dot(pmat, side_by_side(hi)) + _dot(pmat, side_by_side(lo)), 0.0)
    e_all = jnp.exp(sums)

    def e_rows(r, b):
        return e_all[r * c:(r + 1) * c, b * w:(b + 1) * w]

    def stacked(x):
        xb = x.astype(BF16)
        return jnp.concatenate([xb * hm_tiles[h] for h in range(4)], axis=0)

    q_s = [qf[b * c:(b + 1) * c] for b in seqs]
    k_s = [kk[b * c:(b + 1) * c] for b in seqs]
    v_s = [v[b * c:(b + 1) * c] for b in seqs]

    att = [masks_ref[0] * _dot_nt(q_s[b].astype(BF16), stacked(k_s[b])) for b in seqs]
    for li in range(len(HG_LEVELS)):
        for b in seqs:
            e_l = e_rows(2 + li, b)
            att[b] = att[b] + masks_ref[li + 1] * _dot_nt((q_s[b] * e_l).astype(BF16), stacked(k_s[b] * e_l))

    outs = []
    for b in seqs:
        e_b = e_rows(0, b)
        st = state_ref[b]
        o = _dot(att[b].astype(BF16), stacked(v_s[b])) + _dot_nt((q_s[b] * e_b).astype(BF16), st.astype(BF16))
        k_rest = (k_s[b] * e_rows(1, b)).astype(BF16)
        state_ref[b] = st * e_b[c - 1:c] + bd * _dot_tn(v_s[b].astype(BF16), k_rest)
        outs.append(o)

    o = jnp.concatenate(outs, axis=0)
    ms = _head_mean_sq(o, bd)
    y = o * lax.rsqrt(ms + RMS_EPS) * nw_ref[...] * _sigmoid(g_ref[...].reshape(nb * c, w))
    o_ref[...] = y.astype(o_ref.dtype).reshape(nb, c, w)


def hgrn2(h3, lb, norm_w):
    batch, seq, _ = h3.shape
    w = lb.shape[-1]
    c = HG_CHUNK
    nb = math.gcd(batch, HG_BATCH)
    pmat, masks = _hgrn_constants()
    lane_head = np.arange(w) // HEAD_DIM
    bd = (lane_head[:, None] == lane_head[None, :]).astype(np.float32)
    hm = (np.arange(4)[:, None] == lane_head[None, :]).astype(np.float32)

    def col(j):
        return pl.BlockSpec((nb, c, w), lambda b, i, j=j: (b, i, j))

    def const(shape):
        return pl.BlockSpec(shape, lambda b, i: (0,) * len(shape))

    return pl.pallas_call(
        _hgrn_kernel, grid=(batch // nb, seq // c),
        in_specs=[col(0), col(1), col(2), col(3), const((1, w)), const((1, w)),
                  const(pmat.shape), const(masks.shape), const(bd.shape), const(hm.shape)],
        out_specs=pl.BlockSpec((nb, c, w), lambda b, i: (b, i, 0)),
        out_shape=jax.ShapeDtypeStruct((batch, seq, w), BF16),
        scratch_shapes=[pltpu.VMEM((nb, w, w), F32)],
        compiler_params=_params("parallel", "arbitrary"), name="hgrn2",
    )(h3, h3, h3, h3, lb.reshape(1, w), norm_w.reshape(1, w), jnp.asarray(pmat, BF16), jnp.asarray(masks),
      jnp.asarray(bd), jnp.asarray(hm))


def _gmlp_kernel(u_ref, v_ref, lnw_ref, lnb_ref, ws_ref, bias_ref, nw_ref, bd_ref, hm_ref, o_ref):
    c = GM_CHUNK
    groups = ws_ref.shape[0]
    u = _gelu(u_ref[...])
    v = _layer_norm(_gelu(v_ref[...]), lnw_ref[...], lnb_ref[...])
    hm = hm_ref[...]
    bd = bd_ref[...]
    causal = lax.broadcasted_iota(jnp.int32, (c, c), 0) >= lax.broadcasted_iota(jnp.int32, (c, c), 1)
    w_cat = jnp.concatenate([jnp.where(causal, ws_ref[g], 0.0).astype(BF16) for g in range(groups)], axis=1)
    for j in range(u.shape[0] // c):
        rows = slice(j * c, (j + 1) * c)
        v_j = v[rows]
        v_bd = jnp.concatenate([v_j * hm[g:g + 1] for g in range(groups)], axis=0).astype(BF16)
        y = u[rows] * (bias_ref[...] + _dot(w_cat, v_bd))
        ms = _head_mean_sq(y, bd)
        o_ref[rows, :] = (y * lax.rsqrt(ms + RMS_EPS) * nw_ref[...]).astype(o_ref.dtype)


def gmlp(h, ln_w, ln_b, w_s, b_s, norm_w, u_col, v_col):
    n = h.shape[0]
    groups, c, _ = w_s.shape
    w = groups * HEAD_DIM
    lane_head = np.arange(w) // HEAD_DIM
    bd = (lane_head[:, None] == lane_head[None, :]).astype(np.float32)
    hm = (np.arange(groups)[:, None] == lane_head[None, :]).astype(np.float32)
    bias = jnp.repeat(b_s.T, HEAD_DIM, axis=1)

    def const(shape):
        return pl.BlockSpec(shape, lambda i: (0,) * len(shape))

    t = math.gcd(n, GM_TILE_CHUNKS * c)
    return pl.pallas_call(
        _gmlp_kernel, grid=(n // t,),
        in_specs=[pl.BlockSpec((t, w), lambda i: (i, u_col)), pl.BlockSpec((t, w), lambda i: (i, v_col)),
                  const((1, w)), const((1, w)), const(w_s.shape), const((c, w)), const((1, w)),
                  const(bd.shape), const(hm.shape)],
        out_specs=pl.BlockSpec((t, w), lambda i: (i, 0)),
        out_shape=jax.ShapeDtypeStruct((n, w), BF16),
        compiler_params=_params("parallel"), name="gmlp",
    )(h, h, ln_w.reshape(1, w), ln_b.reshape(1, w), w_s, bias, norm_w.reshape(1, w),
      jnp.asarray(bd), jnp.asarray(hm))


def _compress_kernel(u_ref, wtop_ref, wbot_ref, pe_ref, w2_ref, o_ref):
    u = u_ref[...].astype(BF16)
    wtop = wtop_ref[...]
    wbot = wbot_ref[...]
    pe = pe_ref[...].astype(BF16)
    const = _dot(pe[0:1], wtop) + _dot(pe[1:2], wbot)
    p = _dot(u, wtop)
    q = _dot(u, wbot)
    q_next = jnp.concatenate([q[1:], jnp.zeros_like(q[0:1])], axis=0)
    hid = _gelu(p + q_next + const)
    o_ref[...] = _dot(hid.astype(BF16), w2_ref[...]).astype(o_ref.dtype)


def compress(kv, pe, w1, w2, batch, seq):
    g = NSA_KV_GROUPS
    half = CMP_STRIDE
    units = seq // half
    gw = g * HEAD_DIM
    u = kv.reshape(batch * units, half * gw)
    eye = jnp.eye(g, dtype=F32)
    w1r = w1.reshape(2, half, HEAD_DIM, CMP_HIDDEN)
    wbd = jnp.einsum('hjdn,gk->hjgdkn', w1r, eye).reshape(2, half * gw, g * CMP_HIDDEN).astype(BF16)
    w2bd = jnp.einsum('nd,gk->gnkd', w2, eye).reshape(g * CMP_HIDDEN, gw).astype(BF16)
    pe2 = jnp.broadcast_to(pe.reshape(2, half, 1, HEAD_DIM), (2, half, g, HEAD_DIM)).reshape(2, half * gw)

    def const(shape):
        return pl.BlockSpec(shape, lambda b: (0,) * len(shape))

    out = pl.pallas_call(
        _compress_kernel, grid=(batch,),
        in_specs=[pl.BlockSpec((units, half * gw), lambda b: (b, 0)),
                  const(wbd.shape[1:]), const(wbd.shape[1:]), const(pe2.shape), const(w2bd.shape)],
        out_specs=pl.BlockSpec((units, gw), lambda b: (b, 0)),
        out_shape=jax.ShapeDtypeStruct((batch * units, gw), BF16),
        compiler_params=_params("parallel"), name="nsa_compress",
    )(u, wbd[0], wbd[1], pe2, w2bd)
    return out.reshape(batch, units, gw)


def _rot_half_pairs(x):
    lane = lax.broadcasted_iota(jnp.int32, x.shape, 1)
    fwd = pltpu.roll(x, 32, axis=1)
    bwd = pltpu.roll(x, 96, axis=1)
    return jnp.where((lane % HEAD_DIM) < HEAD_DIM // 2, bwd, fwd)


def _kprep_kernel(ks_ref, vs_ref, kw_ref, vw_ref, cos_ref, sin_ref, ksa_ref, vso_ref, kwo_ref, vwo_ref):
    t = ks_ref.shape[0]
    cos = cos_ref[...]
    sin = sin_ref[...]
    cos2 = jnp.concatenate([cos, cos], axis=1)
    sin2 = jnp.concatenate([sin, sin], axis=1)
    ks = ks_ref[...]
    kw = kw_ref[...]
    ks_r = ks * cos2 + _rot_half_pairs(ks) * sin2
    kw_r = kw * cos2 + _rot_half_pairs(kw) * sin2
    pos = pl.program_id(1) * t + lax.broadcasted_iota(jnp.int32, (t, HEAD_DIM), 0)
    onehot = (pos // SEL_BLOCK == lax.broadcasted_iota(jnp.int32, (t, HEAD_DIM), 1)).astype(F32)
    vs_t = vs_ref[...].T
    vw_t = vw_ref[...].T
    tail = (lax.broadcasted_iota(jnp.int32, (V_ROWS - HEAD_DIM, t), 0) == 0).astype(F32)
    for g in range(NSA_KV_GROUPS):
        sl = slice(g * HEAD_DIM, (g + 1) * HEAD_DIM)
        ksa_ref[g] = jnp.concatenate([ks_r[:, sl], onehot], axis=1).astype(BF16)
        vso_ref[g] = jnp.concatenate([vs_t[sl], tail], axis=0).astype(BF16)
        kwo_ref[g] = kw_r[:, sl].astype(BF16)
        vwo_ref[g] = jnp.concatenate([vw_t[sl], tail], axis=0).astype(BF16)


def nsa_kprep(h, cosf, sinf, batch, seq, col0):
    g = NSA_KV_GROUPS
    t = min(seq, 512)
    nt = seq // t

    def col(j):
        return pl.BlockSpec((t, LANES), lambda b, i, j=j: (b * nt + i, col0 + j))

    tab = pl.BlockSpec((t, HEAD_DIM), lambda b, i: (b * nt + i, 0))

    def out(wd):
        return pl.BlockSpec((None, g, t, wd), lambda b, i: (b, 0, i, 0))

    out_t = pl.BlockSpec((None, g, V_ROWS, t), lambda b, i: (b, 0, 0, i))
    k_shape = jax.ShapeDtypeStruct((batch, g, seq, HEAD_DIM), BF16)
    v_shape = jax.ShapeDtypeStruct((batch, g, V_ROWS, seq), BF16)
    return pl.pallas_call(
        _kprep_kernel, grid=(batch, nt),
        in_specs=[col(0), col(1), col(2), col(3), tab, tab],
        out_specs=[out(2 * HEAD_DIM), out_t, out(HEAD_DIM), out_t],
        out_shape=[jax.ShapeDtypeStruct((batch, g, seq, 2 * HEAD_DIM), BF16), v_shape, k_shape, v_shape],
        compiler_params=_params("parallel", "parallel"), name="nsa_kprep",
    )(h, h, h, h, cosf, sinf)


def _nsa_kernel(hq_ref, gate_ref, cos_ref, sin_ref, kc_ref, vc_ref, ksa_ref, vs_ref, kw_ref, vw_ref,
                ovl_ref, nw_ref, o_ref, *, tq, tk, n_sb):
    qi = pl.program_id(1)
    hpg = NSA_HPG
    groups = NSA_KV_GROUPS
    rows = hpg * tq
    t0 = qi * tq
    scale = 1.0 / math.sqrt(HEAD_DIM)
    half = HEAD_DIM // 2

    hq_t = hq_ref[...].T
    cos = cos_ref[...]
    sin = sin_ref[...]
    q_raw, q_rot = [], []
    for g in range(groups):
        raw_g, rot_g = [], []
        for h in range(hpg):
            r0 = (g * hpg + h) * HEAD_DIM
            qh = hq_t[r0:r0 + HEAD_DIM]
            swapped = jnp.concatenate([qh[half:], qh[:half]], axis=0)
            raw_g.append(qh * (scale * LOG2_E))
            rot_g.append((qh * cos + swapped * sin) * (scale * LOG2_E))
        q_raw.append(jnp.concatenate(raw_g, axis=1).astype(BF16))
        q_rot.append(jnp.concatenate(rot_g, axis=1))

    tpos = t0 + lax.broadcasted_iota(jnp.int32, (1, tq), 1)
    tpos_r = jnp.concatenate([tpos] * hpg, axis=1)

    def flash_steps(s, v_t, carry):
        m_new = [jnp.maximum(carry[g][0], jnp.max(s[g], axis=0, keepdims=True)) for g in range(groups)]
        p = [jnp.exp2(s[g] - m_new[g]).astype(BF16) for g in range(groups)]
        return tuple((m_new[g], jnp.exp2(carry[g][0] - m_new[g]) * carry[g][1] + _dot(v_t[g], p[g]))
                     for g in range(groups))

    init = ((jnp.full((1, rows), NEG_INF, F32), jnp.zeros((V_ROWS, rows), F32)),) * groups

    wk = WINDOW + tq
    kw0 = pl.multiple_of(jnp.maximum(t0 - WINDOW, 0), tq)
    kpos_w = kw0 + lax.broadcasted_iota(jnp.int32, (wk, 1), 0)
    mask_w = (kpos_w <= tpos_r) & (kpos_w > tpos_r - WINDOW)
    n_pad = jnp.maximum(WINDOW - 1 - tpos_r, 0).astype(F32)
    n_cmp = kc_ref.shape[1]
    cmp_end = lax.broadcasted_iota(jnp.int32, (n_cmp, 1), 0) * CMP_STRIDE + (CMP_BLOCK - 1)
    mask_c = cmp_end <= tpos_r
    blk = lax.broadcasted_iota(jnp.int32, (n_sb, tq), 0)
    cur = (t0 + lax.broadcasted_iota(jnp.int32, (n_sb, tq), 1)) // SEL_BLOCK
    s_w = [jnp.where(mask_w, _dot(kw_ref[g, pl.ds(kw0, wk), :], q_rot[g].astype(BF16)), NEG_INF)
           for g in range(groups)]
    s_c = [jnp.where(mask_c, _dot(kc_ref[g], q_raw[g]), NEG_INF) for g in range(groups)]
    win = flash_steps(s_w, [vw_ref[g, :, pl.ds(kw0, wk)] for g in range(groups)], init)
    o_w, o_c, imps = [], [], []
    for g in range(groups):
        m_w, acc_w = win[g]
        m_f = jnp.where(n_pad > 0.0, jnp.maximum(m_w, 0.0), m_w)
        a_w = jnp.exp2(m_w - m_f)
        o_w.append(acc_w[0:HEAD_DIM] * (a_w / (acc_w[HEAD_DIM:HEAD_DIM + 1] * a_w + n_pad * jnp.exp2(-m_f))))

    for g in range(groups):
        e_c = jnp.exp2(s_c[g] - jnp.max(s_c[g], axis=0, keepdims=True))
        p_c = jnp.where(mask_c, e_c * (1.0 / jnp.sum(e_c, axis=0, keepdims=True)), 0.0)
        o_c.append(_dot(vc_ref[g], p_c.astype(BF16)))

        p_sum = p_c[:, 0:tq]
        for h in range(1, hpg):
            p_sum = p_sum + p_c[:, h * tq:(h + 1) * tq]
        imp = jnp.dot(ovl_ref[...], p_sum, precision=HIGHEST, preferred_element_type=F32)
        key = pltpu.bitcast(jnp.maximum(imp, 0.0), jnp.int32)
        key = jnp.where((blk == 0) | (blk == cur) | (blk == cur - 1), FORCE_KEY, key)
        imps.append(jnp.where(blk > cur, -1, key))

    n_sel = min(N_SEL, n_sb)

    def bit_body(it, taus):
        bit = lax.shift_left(jnp.int32(1), 30 - it)
        out = []
        for g in range(groups):
            cand = taus[g] | bit
            cnt = jnp.sum(jnp.where(imps[g] >= cand, 1, 0), axis=0, keepdims=True)
            out.append(jnp.where(cnt >= n_sel, cand, taus[g]))
        return tuple(out)

    taus = lax.fori_loop(0, 31, bit_body, (jnp.zeros((1, tq), jnp.int32),) * groups)
    lower = (lax.broadcasted_iota(jnp.int32, (n_sb, n_sb), 0)
             > lax.broadcasted_iota(jnp.int32, (n_sb, n_sb), 1)).astype(BF16)
    q_aug = []
    for g in range(groups):
        above = imps[g] > taus[g]
        equal = imps[g] == taus[g]
        need = n_sel - jnp.sum(jnp.where(above, 1, 0), axis=0, keepdims=True)
        earlier = _dot(lower, jnp.where(equal, 1.0, 0.0).astype(BF16))
        selected = above | (equal & (earlier < need.astype(F32)))
        sel_bias = jnp.where(selected, 0.0, NEG_INF)
        if n_sb < HEAD_DIM:
            sel_bias = jnp.concatenate([sel_bias, jnp.zeros((HEAD_DIM - n_sb, tq), F32)], axis=0)
        q_aug.append(jnp.concatenate([q_rot[g], jnp.concatenate([sel_bias] * hpg, axis=1)], axis=0).astype(BF16))

    def scores(g, kt):
        k0 = pl.multiple_of(kt * tk, tk)
        return _dot(ksa_ref[g, pl.ds(k0, tk), :], q_aug[g])

    def values(kt):
        k0 = pl.multiple_of(kt * tk, tk)
        return [vs_ref[g, :, pl.ds(k0, tk)] for g in range(groups)]

    def sel_body(kt, carry):
        return flash_steps([scores(g, kt) for g in range(groups)], values(kt), carry)

    n_full = t0 // tk
    carry = lax.fori_loop(0, n_full, sel_body, init)
    kpos = n_full * tk + lax.broadcasted_iota(jnp.int32, (tk, 1), 0)
    mask_s = kpos <= tpos_r
    carry = flash_steps([jnp.where(mask_s, scores(g, n_full), NEG_INF) for g in range(groups)], values(n_full), carry)

    gates = _sigmoid(gate_ref[...].T)
    nw = nw_ref[...]
    outs = []
    for g in range(groups):
        acc_s = carry[g][1]
        o_s = acc_s[0:HEAD_DIM] * (1.0 / acc_s[HEAD_DIM:HEAD_DIM + 1])
        for h in range(hpg):
            sl = slice(h * tq, (h + 1) * tq)
            r = (g * hpg + h) * N_GATES
            o = gates[r:r + 1] * o_c[g][:, sl] + gates[r + 1:r + 2] * o_s[:, sl] + gates[r + 2:r + 3] * o_w[g][:, sl]
            ms = jnp.mean(o * o, axis=0, keepdims=True)
            outs.append(o * lax.rsqrt(ms + RMS_EPS) * nw[:, g * hpg + h:g * hpg + h + 1])
    o_ref[...] = jnp.concatenate(outs, axis=0).T.astype(o_ref.dtype)


def nsa_attention(h, cos_t, sin_t, kc, vc_t, ksa, vs_t, kw, vw_t, norm_w, batch, seq, q_col0, gate_col):
    g, hpg = NSA_KV_GROUPS, NSA_HPG
    tq = min(seq, 256)
    nq = seq // tq
    n_sb = seq // SEL_BLOCK
    assert n_sb <= HEAD_DIM, "selection-block one-hot shares the 64 spare key lanes"
    n_cmp = kc.shape[2]
    units = np.arange(n_cmp)[:, None] + np.arange(CMP_BLOCK // CMP_STRIDE)[None, :]
    ovl = np.zeros((n_cmp, n_sb), np.float32)
    for c in range((seq - CMP_BLOCK) // CMP_STRIDE + 1):
        for u in units[c]:
            ovl[c, u // (SEL_BLOCK // CMP_STRIDE)] += 1.0
    ovl_t = jnp.asarray(ovl.T)

    def per_b(shape):
        return pl.BlockSpec((None, g) + shape, lambda b, qi: (b, 0, 0, 0))

    width = g * hpg * HEAD_DIM
    tab = pl.BlockSpec((HEAD_DIM, tq), lambda b, qi: (0, b * nq + qi))
    tk = min(seq, 512)
    assert seq >= WINDOW + tq and seq % tk == 0 and tk % tq == 0
    kern = functools.partial(_nsa_kernel, tq=tq, tk=tk, n_sb=n_sb)
    return pl.pallas_call(
        kern, grid=(batch, nq),
        in_specs=[pl.BlockSpec((tq, width), lambda b, qi: (b * nq + qi, q_col0)),
                  pl.BlockSpec((tq, LANES), lambda b, qi: (b * nq + qi, gate_col)),
                  tab, tab,
                  per_b((n_cmp, HEAD_DIM)), per_b((HEAD_DIM, n_cmp)),
                  per_b((seq, 2 * HEAD_DIM)), per_b((V_ROWS, seq)),
                  per_b((seq, HEAD_DIM)), per_b((V_ROWS, seq)),
                  pl.BlockSpec((n_sb, n_cmp), lambda b, qi: (0, 0)),
                  pl.BlockSpec((HEAD_DIM, g * hpg), lambda b, qi: (0, 0))],
        out_specs=pl.BlockSpec((tq, width), lambda b, qi: (b * nq + qi, 0)),
        out_shape=jax.ShapeDtypeStruct((batch * seq, width), BF16),
        compiler_params=_params("parallel", "arbitrary"), name="nsa_attention",
    )(h, h, cos_t, sin_t, kc, vc_t, ksa, vs_t, kw, vw_t, ovl_t, norm_w.reshape(g * hpg, HEAD_DIM).T)


def _out_proj_kernel(x_ref, yhg_ref, ygm_ref, ynsa_ref, whg_ref, wgm_ref, wnsa_ref, lnw_ref, lnb_ref,
                     o_ref, oa_ref, ob_ref, *, alpha):
    mix = (_dot(yhg_ref[...], whg_ref[...]) + _dot(ygm_ref[...], wgm_ref[...])
           + _dot(ynsa_ref[...], wnsa_ref[...]))
    y = _layer_norm(alpha * x_ref[...] + mix, lnw_ref[...], lnb_ref[...])
    o_ref[...] = y
    _store_word_tables((oa_ref, ob_ref), _pack_bf16_pairs(y))


def out_proj_ln(x2d, y_hg, y_gm, y_nsa, w_out, ln_w, ln_b, alpha):
    n, d = x2d.shape
    w1, w2 = y_hg.shape[1], y_hg.shape[1] + y_gm.shape[1]
    whg = w_out[:w1].astype(BF16)
    wgm = w_out[w1:w2].astype(BF16)
    wnsa = w_out[w2:].astype(BF16)
    t = min(n, 512)

    def row(wd):
        return pl.BlockSpec((t, wd), lambda i: (i, 0))

    def const(shape):
        return pl.BlockSpec(shape, lambda i: (0,) * len(shape))

    kern = functools.partial(_out_proj_kernel, alpha=alpha)
    return pl.pallas_call(
        kern, grid=(n // t,),
        in_specs=[row(d), row(y_hg.shape[1]), row(y_gm.shape[1]), row(y_nsa.shape[1]),
                  const(whg.shape), const(wgm.shape), const(wnsa.shape), const((1, d)), const((1, d))],
        out_specs=[row(d), row(SC_ROW_WORDS), row(SC_ROW_WORDS)],
        out_shape=[jax.ShapeDtypeStruct((n, d), F32)] + [jax.ShapeDtypeStruct((n, SC_ROW_WORDS), jnp.uint32)] * 2,
        compiler_params=_params("parallel"), name="out_proj_ln",
    )(x2d, y_hg, y_gm, y_nsa, whg, wgm, wnsa, ln_w.reshape(1, d), ln_b.reshape(1, d))


def _router_kernel(x_ref, w_ref, b_ref, e_ref, p_ref, r_ref, cnt_ref, carry_ref):
    t = x_ref.shape[0]

    @pl.when(pl.program_id(0) == 0)
    def _():
        carry_ref[...] = jnp.zeros_like(carry_ref)

    x = x_ref[...]
    x_hi = x.astype(BF16)
    x_lo = (x - x_hi.astype(F32)).astype(BF16)
    w = w_ref[...]
    w_hi = w.astype(BF16)
    w_lo = (w - w_hi.astype(F32)).astype(BF16)
    logits = _dot(x_hi, w_hi) + (_dot(x_lo, w_hi) + _dot(x_hi, w_lo)) + b_ref[...]
    lane = lax.broadcasted_iota(jnp.int32, logits.shape, 1)
    work = logits
    vals, idxs = [], []
    sel = jnp.zeros(logits.shape, F32)
    for _ in range(TOP_K):
        m = jnp.max(work, axis=-1, keepdims=True)
        idx = jnp.min(jnp.where(work == m, lane, LANES), axis=-1, keepdims=True)
        hit = lane == idx
        sel = jnp.where(hit, 1.0, sel)
        work = jnp.where(hit, -jnp.inf, work)
        vals.append(m)
        idxs.append(idx)
    exps = [jnp.exp(v - vals[0]) for v in vals]
    den = exps[0] + exps[1] + exps[2] + exps[3]
    strict = (lax.broadcasted_iota(jnp.int32, (t, t), 0) > lax.broadcasted_iota(jnp.int32, (t, t), 1))
    before = _dot(strict.astype(BF16), sel.astype(BF16)) + carry_ref[...]
    ranks = [jnp.sum(jnp.where(lane == idx, before, 0.0), axis=-1, keepdims=True) for idx in idxs]
    kcol = lax.broadcasted_iota(jnp.int32, (t, TOP_K), 1)
    e_out = jnp.zeros((t, TOP_K), jnp.int32)
    p_out = jnp.zeros((t, TOP_K), F32)
    r_out = jnp.zeros((t, TOP_K), jnp.int32)
    for k in range(TOP_K):
        e_out = jnp.where(kcol == k, idxs[k], e_out)
        p_out = jnp.where(kcol == k, exps[k] / den, p_out)
        r_out = jnp.where(kcol == k, ranks[k].astype(jnp.int32), r_out)
    e_ref[...] = e_out
    p_ref[...] = p_out
    r_ref[...] = r_out
    carry_ref[...] = carry_ref[...] + jnp.sum(sel, axis=0, keepdims=True)
    cnt_ref[...] = carry_ref[...].astype(jnp.int32)


def moe_router(x2d, router_w, router_b):
    n, d = x2d.shape
    e = router_w.shape[1]
    t = min(n, 512)
    w = jnp.zeros((d, LANES), F32).at[:, :e].set(router_w)
    b = jnp.full((1, LANES), NEG_INF, F32).at[0, :e].set(router_b)
    row4 = pl.BlockSpec((t, TOP_K), lambda i: (i, 0))
    top_e, top_p, rank, counts = pl.pallas_call(
        _router_kernel, grid=(n // t,),
        in_specs=[pl.BlockSpec((t, d), lambda i: (i, 0)), pl.BlockSpec((d, LANES), lambda i: (0, 0)),
                  pl.BlockSpec((1, LANES), lambda i: (0, 0))],
        out_specs=[row4, row4, row4, pl.BlockSpec((1, LANES), lambda i: (0, 0))],
        out_shape=[jax.ShapeDtypeStruct((n, TOP_K), jnp.int32), jax.ShapeDtypeStruct((n, TOP_K), F32),
                   jax.ShapeDtypeStruct((n, TOP_K), jnp.int32), jax.ShapeDtypeStruct((1, LANES), jnp.int32)],
        scratch_shapes=[pltpu.VMEM((1, LANES), F32)],
        compiler_params=_params("arbitrary"), name="moe_router",
    )(x2d, w, b)
    return top_e, top_p, rank, counts[0, :e]


def _expert_kernel(be_ref, valid_ref, xa_ref, xb_ref, wu_ref, bu_ref, wd_ref, bd_ref, oa_ref, ob_ref,
                   wu_bf, wd_bf):
    i = pl.program_id(0)
    f = wd_ref.shape[0]
    n_used = be_ref[pl.num_programs(0)]

    @pl.when((i == 0) | (be_ref[i] != be_ref[jnp.maximum(i - 1, 0)]))
    def _():
        wu_bf[...] = wu_ref[...].astype(BF16)
        wd_bf[...] = wd_ref[...].astype(BF16)

    @pl.when(i < n_used)
    def _():
        packed = jnp.concatenate([xa_ref[...], xb_ref[...]], axis=1)
        live = lax.broadcasted_iota(jnp.int32, packed.shape, 0) < valid_ref[i]
        x_lo, x_hi = _unpack_bf16_pairs(jnp.where(live, packed, jnp.uint32(0)))
        x = jnp.concatenate([x_lo.astype(BF16), x_hi.astype(BF16)], axis=1)
        hcat = _dot(x, wu_bf[...]) + bu_ref[...]
        glu = jnp.minimum(hcat[:, :f], SWIGLU_LIMIT)
        lin = jnp.clip(hcat[:, f:], -SWIGLU_LIMIT, SWIGLU_LIMIT)
        act = glu * _sigmoid(SWIGLU_ALPHA * glu) * (lin + 1.0)
        _store_word_tables((oa_ref, ob_ref), _pack_bf16_pairs(_dot(act.astype(BF16), wd_bf[...]) + bd_ref[...]))

    @pl.when(i >= n_used)
    def _():
        oa_ref[...] = jnp.zeros_like(oa_ref)
        ob_ref[...] = jnp.zeros_like(ob_ref)


def moe_experts(xa, xb, block_e, n_used, block_valid, w_up, b_up, w_down, b_down, layer):
    rows = xa.shape[0]
    _, e, d, f2 = w_up.shape
    f = f2 // 2
    nb = rows // EXPERT_BLOCK
    words = pl.BlockSpec((EXPERT_BLOCK, SC_ROW_WORDS), lambda i, be, nv: (i, 0))
    grid_spec = pltpu.PrefetchScalarGridSpec(
        num_scalar_prefetch=2, grid=(nb,),
        in_specs=[words, words,
                  pl.BlockSpec((None, None, d, f2), lambda i, be, nv: (layer, be[i], 0, 0)),
                  pl.BlockSpec((None, None, 1, f2), lambda i, be, nv: (layer, be[i], 0, 0)),
                  pl.BlockSpec((None, None, f, d), lambda i, be, nv: (layer, be[i], 0, 0)),
                  pl.BlockSpec((None, None, 1, d), lambda i, be, nv: (layer, be[i], 0, 0))],
        out_specs=[words, words],
        scratch_shapes=[pltpu.VMEM((d, f2), BF16), pltpu.VMEM((f, d), BF16)])
    depth = w_up.shape[0]
    return pl.pallas_call(
        _expert_kernel, grid_spec=grid_spec,
        out_shape=[jax.ShapeDtypeStruct((rows, SC_ROW_WORDS), jnp.uint32)] * 2,
        compiler_params=pltpu.CompilerParams(dimension_semantics=("arbitrary",), vmem_limit_bytes=EXPERT_VMEM_LIMIT),
        name="moe_experts",
    )(jnp.concatenate([block_e, n_used.reshape(1)]), block_valid, xa, xb, w_up, b_up.reshape(depth, e, 1, f2),
      w_down, b_down.reshape(depth, e, 1, d))


def _combine_kernel(x_ref, ya_ref, yb_ref, p_ref, lnw_ref, lnb_ref, o_ref, *, alpha):
    p = p_ref[...]
    moe = jnp.zeros(x_ref.shape, F32)
    for k in range(TOP_K):
        y_lo, y_hi = _unpack_bf16_pairs(jnp.concatenate([ya_ref[k], yb_ref[k]], axis=1))
        moe = moe + p[:, k:k + 1] * jnp.concatenate([y_lo, y_hi], axis=1)
    o_ref[...] = _layer_norm(alpha * x_ref[...] + moe, lnw_ref[...], lnb_ref[...])


def combine_ln(x2d, ya, yb, top_p, ln_w, ln_b, alpha):
    n, d = x2d.shape
    t = min(n, 256)
    kern = functools.partial(_combine_kernel, alpha=alpha)
    words = pl.BlockSpec((TOP_K, t, SC_ROW_WORDS), lambda i: (0, i, 0))
    return pl.pallas_call(
        kern, grid=(n // t,),
        in_specs=[pl.BlockSpec((t, d), lambda i: (i, 0)), words, words,
                  pl.BlockSpec((t, TOP_K), lambda i: (i, 0)),
                  pl.BlockSpec((1, d), lambda i: (0, 0)), pl.BlockSpec((1, d), lambda i: (0, 0))],
        out_specs=pl.BlockSpec((t, d), lambda i: (i, 0)),
        out_shape=jax.ShapeDtypeStruct((n, d), F32),
        compiler_params=_params("parallel"), name="moe_combine_ln",
    )(x2d, ya, yb, top_p, ln_w.reshape(1, d), ln_b.reshape(1, d))


def _sc_mesh():
    return plsc.VectorSubcoreMesh(core_axis_name="core", subcore_axis_name="subcore")


def sc_gather_rows(tables, idx):
    r = idx.shape[0]
    nt = len(tables)
    out = jax.ShapeDtypeStruct((r, SC_ROW_WORDS), tables[0].dtype)

    @pl.kernel(out_type=(out,) * nt, mesh=_sc_mesh(), name="sc_gather_rows")
    def gather(*refs):
        x_hbm, i_hbm, o_hbm = refs[:nt], refs[nt], refs[nt + 1:]
        for j in range(nt):
            def body(i_vmem, o_vmem, table=x_hbm[j]):
                pltpu.sync_copy(table.at[i_vmem.at[0]], o_vmem)

            pltpu.emit_pipeline(
                body, grid=(r // SC_WINDOW,),
                in_specs=[pl.BlockSpec((1, SC_WINDOW), lambda i: (0, i))],
                out_specs=[pl.BlockSpec((SC_WINDOW, SC_ROW_WORDS), lambda i: (i, 0))],
                core_axis_name=("core", "subcore"), dimension_semantics=(pltpu.PARALLEL,),
            )(i_hbm, o_hbm[j])

    return gather(*tables, idx.reshape(1, r))


def sc_scatter_rows(tables, dest_t, n_rows):
    n = tables[0].shape[0]
    nt = len(tables)
    copies = dest_t.shape[0]
    out = jax.ShapeDtypeStruct((n_rows, SC_ROW_WORDS), tables[0].dtype)

    @pl.kernel(out_type=(out,) * nt, mesh=_sc_mesh(), scratch_types=[], name="sc_scatter_rows")
    def scatter(*refs):
        x_hbm, i_hbm, o_hbm = refs[:nt], refs[nt], refs[nt + 1:]
        for j in range(nt):
            def body(x_vmem, i_vmem, out_j=o_hbm[j]):
                for k in range(copies):
                    pltpu.sync_copy(x_vmem, out_j.at[i_vmem.at[k]])

            pltpu.emit_pipeline(
                body, grid=(n // SC_WINDOW,),
                in_specs=[pl.BlockSpec((SC_WINDOW, SC_ROW_WORDS), lambda i: (i, 0)),
                          pl.BlockSpec((copies, SC_WINDOW), lambda i: (0, i))],
                out_specs=[],
                core_axis_name=("core", "subcore"), dimension_semantics=(pltpu.PARALLEL,),
            )(x_hbm[j], i_hbm)

    return scatter(*tables, dest_t)


def moe_ffn_ln(x_f32, x_packed, router_w, router_b, w_up, b_up, w_down, b_down, layer, ln_w, ln_b, alpha):
    n, d = x_f32.shape
    top_e, top_p, rank, counts = moe_router(x_f32, router_w, router_b)
    padded = (counts + EXPERT_BLOCK - 1) // EXPERT_BLOCK * EXPERT_BLOCK
    pad_end = jnp.cumsum(padded)
    pad_start = pad_end - padded
    n_assign = n * TOP_K
    n_blocks = -(-(n_assign + N_EXPERTS * (EXPERT_BLOCK - 1)) // EXPERT_BLOCK)
    dest_t = (pad_start[top_e] + rank).T
    block_first = jnp.arange(n_blocks, dtype=jnp.int32) * EXPERT_BLOCK
    block_e = jnp.clip(jnp.sum((pad_end[None, :] <= block_first[:, None]).astype(jnp.int32), axis=1),
                       0, N_EXPERTS - 1)
    block_valid = jnp.clip(counts[block_e] - (block_first - pad_start[block_e]), 0, EXPERT_BLOCK)
    n_used = (pad_end[-1] // EXPERT_BLOCK).astype(jnp.int32)
    xa, xb = sc_scatter_rows(x_packed, dest_t, n_blocks * EXPERT_BLOCK)
    ya, yb = moe_experts(xa, xb, block_e, n_used, block_valid.astype(jnp.int32), w_up, b_up, w_down, b_down,
                         layer)
    ya, yb = sc_gather_rows((ya, yb), dest_t.reshape(-1))
    return combine_ln(x_f32, ya.reshape(TOP_K, n, SC_ROW_WORDS), yb.reshape(TOP_K, n, SC_ROW_WORDS), top_p,
                      ln_w, ln_b, alpha)


def kernel(x, positions, w_in, hg_lower_bounds, hg_norm_w, gm_ln_w, gm_ln_b, gm_spatial_w, gm_spatial_b, gm_norm_w, nsa_cmp_pe, nsa_cmp_w1, nsa_cmp_w2, nsa_norm_w, w_out, ln1_w, ln1_b, router_w, router_b, exp_w_up, exp_b_up, exp_w_down, exp_b_down, ln2_w, ln2_b):
    batch, seq, d = x.shape
    depth = w_in.shape[0]
    n = batch * seq
    alpha = (2 * depth) ** 0.25
    hg_w = hg_norm_w.shape[1]
    gm_w = gm_norm_w.shape[1]
    nsa_w = nsa_norm_w.shape[1]
    kv_w = NSA_KV_GROUPS * HEAD_DIM
    in_width = w_in.shape[2]
    off_gm = 4 * hg_w
    off_q = off_gm + 2 * gm_w
    off_kv = off_q + nsa_w
    off_gate = off_kv + 6 * kv_w
    width_pad = -(-in_width // LANES) * LANES

    cosf, sinf, cos_t, sin_t = rope_tables(positions)
    lb_all = jnp.cumsum(jax.nn.softmax(hg_lower_bounds.astype(F32), axis=0), axis=0)
    lb_all = lb_all - lb_all[0:1]

    x2d = x.reshape(n, d)
    for l in range(depth):
        w_l = jnp.pad(w_in[l], ((0, 0), (0, width_pad - in_width))).astype(BF16)
        h = in_proj(x2d, w_l)
        h3 = h.reshape(batch, seq, width_pad)
        y_hg = hgrn2(h3, lb_all[l], hg_norm_w[l]).reshape(n, hg_w)
        y_gm = gmlp(h, gm_ln_w[l], gm_ln_b[l], gm_spatial_w[l], gm_spatial_b[l], gm_norm_w[l],
                    off_gm // gm_w, off_gm // gm_w + 1)
        kc = compress(h3[:, :, off_kv:off_kv + kv_w], nsa_cmp_pe[l, 0], nsa_cmp_w1[l, 0], nsa_cmp_w2[l, 0], batch, seq)
        vc = compress(h3[:, :, off_kv + kv_w:off_kv + 2 * kv_w], nsa_cmp_pe[l, 1], nsa_cmp_w1[l, 1],
                      nsa_cmp_w2[l, 1], batch, seq)
        n_cmp = kc.shape[1]
        kc = kc.reshape(batch, n_cmp, NSA_KV_GROUPS, HEAD_DIM).transpose(0, 2, 1, 3)
        vc_t = vc.reshape(batch, n_cmp, NSA_KV_GROUPS, HEAD_DIM).transpose(0, 2, 3, 1)
        ksa, vs_t, kw, vw_t = nsa_kprep(h, cosf, sinf, batch, seq, (off_kv + 2 * kv_w) // LANES)
        y_nsa = nsa_attention(h, cos_t, sin_t, kc, vc_t, ksa, vs_t, kw, vw_t, nsa_norm_w[l], batch, seq,
                              off_q // nsa_w, off_gate // LANES)
        x1, x1a, x1b = out_proj_ln(x2d, y_hg, y_gm, y_nsa, w_out[l], ln1_w[l], ln1_b[l], alpha)
        x2d = moe_ffn_ln(x1, (x1a, x1b), router_w[l], router_b[l], exp_w_up, exp_b_up, exp_w_down, exp_b_down, l,
                         ln2_w[l], ln2_b[l], alpha)
    return x2d.reshape(batch, seq, d)
```

```python
import functools
import math

import numpy as np
import jax
import jax.numpy as jnp
from jax import lax
from jax.experimental import pallas as pl
from jax.experimental.pallas import tpu as pltpu
from jax.experimental.pallas import tpu_sc as plsc

F32 = jnp.float32
BF16 = jnp.bfloat16
HIGHEST = lax.Precision.HIGHEST

HEAD_DIM = 64
LANES = 128
VMEM_LIMIT = 48 * 1024 * 1024
EXPERT_VMEM_LIMIT = 56 * 1024 * 1024

HG_CHUNK = 64
GM_CHUNK = 128
GM_TILE_CHUNKS = 4
NSA_KV_GROUPS = 2
NSA_HPG = 4
CMP_BLOCK = 32
CMP_STRIDE = 16
CMP_HIDDEN = 128
SEL_BLOCK = 64
N_SEL = 16
WINDOW = 512
N_GATES = 3
IMP_FORCE = 1e9
FORCE_KEY = int(np.float32(IMP_FORCE).view(np.int32))
NEG_INF = -1e30
N_EXPERTS = 32
TOP_K = 4
SWIGLU_ALPHA = 1.702
SWIGLU_LIMIT = 7.0
EXPERT_BLOCK = 512
SC_ROW_WORDS = 256
SC_WINDOW = 128
ROPE_THETA = 10000.0
LOG2_E = 1.4426950408889634
LN_EPS = 1e-5
RMS_EPS = 1e-6
V_ROWS = HEAD_DIM + 16


def _params(*sem):
    return pltpu.CompilerParams(dimension_semantics=sem, vmem_limit_bytes=VMEM_LIMIT)


def _dot(a, b):
    return jnp.dot(a, b, preferred_element_type=F32)


def _dot_nt(a, b, precision=None):
    return lax.dot_general(a, b, (((1,), (1,)), ((), ())), precision=precision,
                           preferred_element_type=F32)


def _dot_tn(a, b):
    return lax.dot_general(a, b, (((0,), (0,)), ((), ())), preferred_element_type=F32)


def _sigmoid(x):
    return 1.0 / (1.0 + jnp.exp(-x))


def _gelu(x):
    return 0.5 * x * (1.0 + jnp.tanh(0.7978845608028654 * (x + 0.044715 * x * x * x)))


def _layer_norm(x, w, b):
    mu = jnp.mean(x, axis=-1, keepdims=True)
    xc = x - mu
    var = jnp.mean(xc * xc, axis=-1, keepdims=True)
    return xc * lax.rsqrt(var + LN_EPS) * w + b


def _pack_bf16_pairs(y):
    w = y.shape[1] // 2
    bits = pltpu.bitcast(y.astype(BF16).astype(F32), jnp.uint32)
    return lax.shift_right_logical(bits[:, :w], jnp.uint32(16)) | (bits[:, w:] & jnp.uint32(0xFFFF0000))


def _unpack_bf16_pairs(u):
    lo = pltpu.bitcast(lax.shift_left(u, jnp.uint32(16)), F32)
    hi = pltpu.bitcast(u & jnp.uint32(0xFFFF0000), F32)
    return lo, hi


def _store_word_tables(refs, packed):
    for j, ref in enumerate(refs):
        ref[...] = packed[:, j * SC_ROW_WORDS:(j + 1) * SC_ROW_WORDS]


def _head_mean_sq(o, bd_ones):
    sq = o * o
    hi = sq.astype(BF16)
    lo = (sq - hi.astype(F32)).astype(BF16)
    ones = bd_ones.astype(BF16)
    return (_dot(hi, ones) + _dot(lo, ones)) * (1.0 / HEAD_DIM)


def _rope_kernel(pos_ref, inv_ref, cos_ref, sin_ref, cost_ref, sint_ref):
    ang = inv_ref[...] * pos_ref[...]
    c = jnp.cos(ang)
    s = jnp.sin(ang)
    cos_t = jnp.concatenate([c, c], axis=0)
    sin_t = jnp.concatenate([-s, s], axis=0)
    cost_ref[...] = cos_t
    sint_ref[...] = sin_t
    cos_ref[...] = cos_t.T
    sin_ref[...] = sin_t.T


def rope_tables(positions):
    n = positions.size
    tile = min(n, 2048)
    posf = positions.reshape(1, n).astype(F32)
    inv = ROPE_THETA ** (-jnp.arange(0, HEAD_DIM, 2, dtype=F32) / HEAD_DIM)
    row = pl.BlockSpec((tile, HEAD_DIM), lambda i: (i, 0))
    rowt = pl.BlockSpec((HEAD_DIM, tile), lambda i: (0, i))
    return pl.pallas_call(
        _rope_kernel, grid=(n // tile,),
        in_specs=[pl.BlockSpec((1, tile), lambda i: (0, i)), pl.BlockSpec((HEAD_DIM // 2, 1), lambda i: (0, 0))],
        out_specs=[row, row, rowt, rowt],
        out_shape=[jax.ShapeDtypeStruct((n, HEAD_DIM), F32)] * 2 + [jax.ShapeDtypeStruct((HEAD_DIM, n), F32)] * 2,
        compiler_params=_params("parallel"), name="rope_tables",
    )(posf, inv.reshape(HEAD_DIM // 2, 1))


def _in_proj_kernel(x_ref, w_ref, h_ref):
    h_ref[...] = _dot(x_ref[...].astype(BF16), w_ref[...])


def in_proj(x2d, w_bf16):
    n, d = x2d.shape
    width = w_bf16.shape[1]
    tile = min(n, 512)
    return pl.pallas_call(
        _in_proj_kernel, grid=(n // tile,),
        in_specs=[pl.BlockSpec((tile, d), lambda i: (i, 0)), pl.BlockSpec((d, width), lambda i: (0, 0))],
        out_specs=pl.BlockSpec((tile, width), lambda i: (i, 0)),
        out_shape=jax.ShapeDtypeStruct((n, width), F32),
        compiler_params=_params("parallel"), name="in_proj")(x2d, w_bf16)


HG_LEVELS = (64, 32, 16, 8, 4, 2)
HG_BATCH = 8


def _hgrn_constants():
    c = HG_CHUNK
    t = np.arange(c)
    u = t[None, :]
    rows = [u <= t[:, None], u > t[:, None]]
    masks = [np.eye(c, dtype=bool)]
    for m in HG_LEVELS:
        ref = ((t // m) * m + m // 2 - 1)[:, None]
        second = (t % m >= m // 2)[:, None]
        rows.append(((u > ref) & (u <= t[:, None]) & second) | ((u > t[:, None]) & (u <= ref) & ~second))
        masks.append((t[:, None] // m == t[None, :] // m) & second & (t[None, :] % m < m // 2))
    pmat = np.concatenate(rows, axis=0).astype(np.float32)
    masks = np.stack([np.tile(mk, (1, 4)) for mk in masks]).astype(np.float32)
    return pmat, masks


def _hgrn_kernel(q_ref, f_ref, i_ref, g_ref, lb_ref, nw_ref, pmat_ref, masks_ref, bd_ref, hm_ref,
                 o_ref, state_ref):
    c = HG_CHUNK

    @pl.when(pl.program_id(1) == 0)
    def _():
        state_ref[...] = jnp.zeros_like(state_ref)

    lb = lb_ref[...]
    bd = bd_ref[...]
    hm = hm_ref[...]
    hm_tiles = [jnp.broadcast_to(hm[h:h + 1], (c, hm.shape[1])).astype(BF16) for h in range(4)]
    pmat = pmat_ref[...]
    a = jnp.log(lb)
    log1m = jnp.log(1.0 - lb)
    nb, _, w = q_ref.shape
    seqs = range(nb)

    fr = f_ref[...].reshape(nb * c, w)
    hq = q_ref[...].reshape(nb * c, w)
    v = i_ref[...].reshape(nb * c, w)
    qf = hq * _sigmoid(hq)
    log_sig = jnp.minimum(fr, 0.0) - jnp.log(1.0 + jnp.exp(-jnp.abs(fr)))
    cc = log1m + log_sig
    log_f = jnp.maximum(a, cc) + jnp.log(1.0 + jnp.exp(-jnp.abs(a - cc)))
    kk = (1.0 - lb) * _sigmoid(-fr)

    hi = log_f.astype(BF16)
    lo = (log_f - hi.astype(F32)).astype(BF16)

    def side_by_side(x):
        return jnp.concatenate([x[b * c:(b + 1) * c] for b in seqs], axis=1)

    sums = jnp.minimum(_dot(pmat, side_by_side(hi)) + _dot(pmat, side_by_side(lo)), 0.0)
    e_all = jnp.exp(sums)

    def e_rows(r, b):
        return e_all[r * c:(r + 1) * c, b * w:(b + 1) * w]

    def stacked(x):
        xb = x.astype(BF16)
        return jnp.concatenate([xb * hm_tiles[h] for h in range(4)], axis=0)

    q_s = [qf[b * c:(b + 1) * c] for b in seqs]
    k_s = [kk[b * c:(b + 1) * c] for b in seqs]
    v_s = [v[b * c:(b + 1) * c] for b in seqs]

    att = [masks_ref[0] * _dot_nt(q_s[b].astype(BF16), stacked(k_s[b])) for b in seqs]
    for li in range(len(HG_LEVELS)):
        for b in seqs:
            e_l = e_rows(2 + li, b)
            att[b] = att[b] + masks_ref[li + 1] * _dot_nt((q_s[b] * e_l).astype(BF16), stacked(k_s[b] * e_l))

    outs = []
    for b in seqs:
        e_b = e_rows(0, b)
        st = state_ref[b]
        o = _dot(att[b].astype(BF16), stacked(v_s[b])) + _dot_nt((q_s[b] * e_b).astype(BF16), st.astype(BF16))
        k_rest = (k_s[b] * e_rows(1, b)).astype(BF16)
        state_ref[b] = st * e_b[c - 1:c] + bd * _dot_tn(v_s[b].astype(BF16), k_rest)
        outs.append(o)

    o = jnp.concatenate(outs, axis=0)
    ms = _head_mean_sq(o, bd)
    y = o * lax.rsqrt(ms + RMS_EPS) * nw_ref[...] * _sigmoid(g_ref[...].reshape(nb * c, w))
    o_ref[...] = y.astype(o_ref.dtype).reshape(nb, c, w)


def hgrn2(h3, lb, norm_w):
    batch, seq, _ = h3.shape
    w = lb.shape[-1]
    c = HG_CHUNK
    nb = math.gcd(batch, HG_BATCH)
    pmat, masks = _hgrn_constants()
    lane_head = np.arange(w) // HEAD_DIM
    bd = (lane_head[:, None] == lane_head[None, :]).astype(np.float32)
    hm = (np.arange(4)[:, None] == lane_head[None, :]).astype(np.float32)

    def col(j):
        return pl.BlockSpec((nb, c, w), lambda b, i, j=j: (b, i, j))

    def const(shape):
        return pl.BlockSpec(shape, lambda b, i: (0,) * len(shape))

    return pl.pallas_call(
        _hgrn_kernel, grid=(batch // nb, seq // c),
        in_specs=[col(0), col(1), col(2), col(3), const((1, w)), const((1, w)),
                  const(pmat.shape), const(masks.shape), const(bd.shape), const(hm.shape)],
        out_specs=pl.BlockSpec((nb, c, w), lambda b, i: (b, i, 0)),
        out_shape=jax.ShapeDtypeStruct((batch, seq, w), BF16),
        scratch_shapes=[pltpu.VMEM((nb, w, w), F32)],
        compiler_params=_params("parallel", "arbitrary"), name="hgrn2",
    )(h3, h3, h3, h3, lb.reshape(1, w), norm_w.reshape(1, w), jnp.asarray(pmat, BF16), jnp.asarray(masks),
      jnp.asarray(bd), jnp.asarray(hm))


def _gmlp_kernel(u_ref, v_ref, lnw_ref, lnb_ref, ws_ref, bias_ref, nw_ref, bd_ref, hm_ref, o_ref):
    c = GM_CHUNK
    groups = ws_ref.shape[0]
    u = _gelu(u_ref[...])
    v = _layer_norm(_gelu(v_ref[...]), lnw_ref[...], lnb_ref[...])
    hm = hm_ref[...]
    bd = bd_ref[...]
    causal = lax.broadcasted_iota(jnp.int32, (c, c), 0) >= lax.broadcasted_iota(jnp.int32, (c, c), 1)
    w_cat = jnp.concatenate([jnp.where(causal, ws_ref[g], 0.0).astype(BF16) for g in range(groups)], axis=1)
    for j in range(u.shape[0] // c):
        rows = slice(j * c, (j + 1) * c)
        v_j = v[rows]
        v_bd = jnp.concatenate([v_j * hm[g:g + 1] for g in range(groups)], axis=0).astype(BF16)
        y = u[rows] * (bias_ref[...] + _dot(w_cat, v_bd))
        ms = _head_mean_sq(y, bd)
        o_ref[rows, :] = (y * lax.rsqrt(ms + RMS_EPS) * nw_ref[...]).astype(o_ref.dtype)


def gmlp(h, ln_w, ln_b, w_s, b_s, norm_w, u_col, v_col):
    n = h.shape[0]
    groups, c, _ = w_s.shape
    w = groups * HEAD_DIM
    lane_head = np.arange(w) // HEAD_DIM
    bd = (lane_head[:, None] == lane_head[None, :]).astype(np.float32)
    hm = (np.arange(groups)[:, None] == lane_head[None, :]).astype(np.float32)
    bias = jnp.repeat(b_s.T, HEAD_DIM, axis=1)

    def const(shape):
        return pl.BlockSpec(shape, lambda i: (0,) * len(shape))

    t = math.gcd(n, GM_TILE_CHUNKS * c)
    return pl.pallas_call(
        _gmlp_kernel, grid=(n // t,),
        in_specs=[pl.BlockSpec((t, w), lambda i: (i, u_col)), pl.BlockSpec((t, w), lambda i: (i, v_col)),
                  const((1, w)), const((1, w)), const(w_s.shape), const((c, w)), const((1, w)),
                  const(bd.shape), const(hm.shape)],
        out_specs=pl.BlockSpec((t, w), lambda i: (i, 0)),
        out_shape=jax.ShapeDtypeStruct((n, w), BF16),
        compiler_params=_params("parallel"), name="gmlp",
    )(h, h, ln_w.reshape(1, w), ln_b.reshape(1, w), w_s, bias, norm_w.reshape(1, w),
      jnp.asarray(bd), jnp.asarray(hm))


def _compress_kernel(u_ref, wtop_ref, wbot_ref, pe_ref, w2_ref, o_ref):
    u = u_ref[...].astype(BF16)
    wtop = wtop_ref[...]
    wbot = wbot_ref[...]
    pe = pe_ref[...].astype(BF16)
    const = _dot(pe[0:1], wtop) + _dot(pe[1:2], wbot)
    p = _dot(u, wtop)
    q = _dot(u, wbot)
    q_next = jnp.concatenate([q[1:], jnp.zeros_like(q[0:1])], axis=0)
    hid = _gelu(p + q_next + const)
    o_ref[...] = _dot(hid.astype(BF16), w2_ref[...]).astype(o_ref.dtype)


def compress(kv, pe, w1, w2, batch, seq):
    g = NSA_KV_GROUPS
    half = CMP_STRIDE
    units = seq // half
    gw = g * HEAD_DIM
    u = kv.reshape(batch * units, half * gw)
    eye = jnp.eye(g, dtype=F32)
    w1r = w1.reshape(2, half, HEAD_DIM, CMP_HIDDEN)
    wbd = jnp.einsum('hjdn,gk->hjgdkn', w1r, eye).reshape(2, half * gw, g * CMP_HIDDEN).astype(BF16)
    w2bd = jnp.einsum('nd,gk->gnkd', w2, eye).reshape(g * CMP_HIDDEN, gw).astype(BF16)
    pe2 = jnp.broadcast_to(pe.reshape(2, half, 1, HEAD_DIM), (2, half, g, HEAD_DIM)).reshape(2, half * gw)

    def const(shape):
        return pl.BlockSpec(shape, lambda b: (0,) * len(shape))

    out = pl.pallas_call(
        _compress_kernel, grid=(batch,),
        in_specs=[pl.BlockSpec((units, half * gw), lambda b: (b, 0)),
                  const(wbd.shape[1:]), const(wbd.shape[1:]), const(pe2.shape), const(w2bd.shape)],
        out_specs=pl.BlockSpec((units, gw), lambda b: (b, 0)),
        out_shape=jax.ShapeDtypeStruct((batch * units, gw), BF16),
        compiler_params=_params("parallel"), name="nsa_compress",
    )(u, wbd[0], wbd[1], pe2, w2bd)
    return out.reshape(batch, units, gw)


def _rot_half_pairs(x):
    lane = lax.broadcasted_iota(jnp.int32, x.shape, 1)
    fwd = pltpu.roll(x, 32, axis=1)
    bwd = pltpu.roll(x, 96, axis=1)
    return jnp.where((lane % HEAD_DIM) < HEAD_DIM // 2, bwd, fwd)


def _kprep_kernel(ks_ref, vs_ref, kw_ref, vw_ref, cos_ref, sin_ref, ksa_ref, vso_ref, kwo_ref, vwo_ref):
    t = ks_ref.shape[0]
    cos = cos_ref[...]
    sin = sin_ref[...]
    cos2 = jnp.concatenate([cos, cos], axis=1)
    sin2 = jnp.concatenate([sin, sin], axis=1)
    ks = ks_ref[...]
    kw = kw_ref[...]
    ks_r = ks * cos2 + _rot_half_pairs(ks) * sin2
    kw_r = kw * cos2 + _rot_half_pairs(kw) * sin2
    pos = pl.program_id(1) * t + lax.broadcasted_iota(jnp.int32, (t, HEAD_DIM), 0)
    onehot = (pos // SEL_BLOCK == lax.broadcasted_iota(jnp.int32, (t, HEAD_DIM), 1)).astype(F32)
    vs_t = vs_ref[...].T
    vw_t = vw_ref[...].T
    tail = (lax.broadcasted_iota(jnp.int32, (V_ROWS - HEAD_DIM, t), 0) == 0).astype(F32)
    for g in range(NSA_KV_GROUPS):
        sl = slice(g * HEAD_DIM, (g + 1) * HEAD_DIM)
        ksa_ref[g] = jnp.concatenate([ks_r[:, sl], onehot], axis=1).astype(BF16)
        vso_ref[g] = jnp.concatenate([vs_t[sl], tail], axis=0).astype(BF16)
        kwo_ref[g] = kw_r[:, sl].astype(BF16)
        vwo_ref[g] = jnp.concatenate([vw_t[sl], tail], axis=0).astype(BF16)


def nsa_kprep(h, cosf, sinf, batch, seq, col0):
    g = NSA_KV_GROUPS
    t = min(seq, 512)
    nt = seq // t

    def col(j):
        return pl.BlockSpec((t, LANES), lambda b, i, j=j: (b * nt + i, col0 + j))

    tab = pl.BlockSpec((t, HEAD_DIM), lambda b, i: (b * nt + i, 0))

    def out(wd):
        return pl.BlockSpec((None, g, t, wd), lambda b, i: (b, 0, i, 0))

    out_t = pl.BlockSpec((None, g, V_ROWS, t), lambda b, i: (b, 0, 0, i))
    k_shape = jax.ShapeDtypeStruct((batch, g, seq, HEAD_DIM), BF16)
    v_shape = jax.ShapeDtypeStruct((batch, g, V_ROWS, seq), BF16)
    return pl.pallas_call(
        _kprep_kernel, grid=(batch, nt),
        in_specs=[col(0), col(1), col(2), col(3), tab, tab],
        out_specs=[out(2 * HEAD_DIM), out_t, out(HEAD_DIM), out_t],
        out_shape=[jax.ShapeDtypeStruct((batch, g, seq, 2 * HEAD_DIM), BF16), v_shape, k_shape, v_shape],
        compiler_params=_params("parallel", "parallel"), name="nsa_kprep",
    )(h, h, h, h, cosf, sinf)


def _nsa_kernel(hq_ref, gate_ref, cos_ref, sin_ref, kc_ref, vc_ref, ksa_ref, vs_ref, kw_ref, vw_ref,
                ovl_ref, nw_ref, o_ref, *, tq, tk, n_sb):
    qi = pl.program_id(1)
    hpg = NSA_HPG
    groups = NSA_KV_GROUPS
    rows = hpg * tq
    t0 = qi * tq
    scale = 1.0 / math.sqrt(HEAD_DIM)
    half = HEAD_DIM // 2

    hq_t = hq_ref[...].T
    cos = cos_ref[...]
    sin = sin_ref[...]
    q_raw, q_rot = [], []
    for g in range(groups):
        raw_g, rot_g = [], []
        for h in range(hpg):
            r0 = (g * hpg + h) * HEAD_DIM
            qh = hq_t[r0:r0 + HEAD_DIM]
            swapped = jnp.concatenate([qh[half:], qh[:half]], axis=0)
            raw_g.append(qh * (scale * LOG2_E))
            rot_g.append((qh * cos + swapped * sin) * (scale * LOG2_E))
        q_raw.append(jnp.concatenate(raw_g, axis=1).astype(BF16))
        q_rot.append(jnp.concatenate(rot_g, axis=1))

    tpos = t0 + lax.broadcasted_iota(jnp.int32, (1, tq), 1)
    tpos_r = jnp.concatenate([tpos] * hpg, axis=1)

    def flash_steps(s, v_t, carry):
        m_new = [jnp.maximum(carry[g][0], jnp.max(s[g], axis=0, keepdims=True)) for g in range(groups)]
        p = [jnp.exp2(s[g] - m_new[g]).astype(BF16) for g in range(groups)]
        return tuple((m_new[g], jnp.exp2(carry[g][0] - m_new[g]) * carry[g][1] + _dot(v_t[g], p[g]))
                     for g in range(groups))

    init = ((jnp.full((1, rows), NEG_INF, F32), jnp.zeros((V_ROWS, rows), F32)),) * groups

    wk = WINDOW + tq
    kw0 = pl.multiple_of(jnp.maximum(t0 - WINDOW, 0), tq)
    kpos_w = kw0 + lax.broadcasted_iota(jnp.int32, (wk, 1), 0)
    mask_w = (kpos_w <= tpos_r) & (kpos_w > tpos_r - WINDOW)
    n_pad = jnp.maximum(WINDOW - 1 - tpos_r, 0).astype(F32)
    n_cmp = kc_ref.shape[1]
    cmp_end = lax.broadcasted_iota(jnp.int32, (n_cmp, 1), 0) * CMP_STRIDE + (CMP_BLOCK - 1)
    mask_c = cmp_end <= tpos_r
    blk = lax.broadcasted_iota(jnp.int32, (n_sb, tq), 0)
    cur = (t0 + lax.broadcasted_iota(jnp.int32, (n_sb, tq), 1)) // SEL_BLOCK
    s_w = [jnp.where(mask_w, _dot(kw_ref[g, pl.ds(kw0, wk), :], q_rot[g].astype(BF16)), NEG_INF)
           for g in range(groups)]
    s_c = [jnp.where(mask_c, _dot(kc_ref[g], q_raw[g]), NEG_INF) for g in range(groups)]
    win = flash_steps(s_w, [vw_ref[g, :, pl.ds(kw0, wk)] for g in range(groups)], init)
    o_w, o_c, imps = [], [], []
    for g in range(groups):
        m_w, acc_w = win[g]
        m_f = jnp.where(n_pad > 0.0, jnp.maximum(m_w, 0.0), m_w)
        a_w = jnp.exp2(m_w - m_f)
        o_w.append(acc_w[0:HEAD_DIM] * (a_w / (acc_w[HEAD_DIM:HEAD_DIM + 1] * a_w + n_pad * jnp.exp2(-m_f))))

    for g in range(groups):
        e_c = jnp.exp2(s_c[g] - jnp.max(s_c[g], axis=0, keepdims=True))
        p_c = jnp.where(mask_c, e_c * (1.0 / jnp.sum(e_c, axis=0, keepdims=True)), 0.0)
        o_c.append(_dot(vc_ref[g], p_c.astype(BF16)))

        p_sum = p_c[:, 0:tq]
        for h in range(1, hpg):
            p_sum = p_sum + p_c[:, h * tq:(h + 1) * tq]
        imp = jnp.dot(ovl_ref[...], p_sum, precision=HIGHEST, preferred_element_type=F32)
        key = pltpu.bitcast(jnp.maximum(imp, 0.0), jnp.int32)
        key = jnp.where((blk == 0) | (blk == cur) | (blk == cur - 1), FORCE_KEY, key)
        imps.append(jnp.where(blk > cur, -1, key))

    n_sel = min(N_SEL, n_sb)

    def bit_body(it, taus):
        bit = lax.shift_left(jnp.int32(1), 30 - it)
        out = []
        for g in range(groups):
            cand = taus[g] | bit
            cnt = jnp.sum(jnp.where(imps[g] >= cand, 1, 0), axis=0, keepdims=True)
            out.append(jnp.where(cnt >= n_sel, cand, taus[g]))
        return tuple(out)

    taus = lax.fori_loop(0, 31, bit_body, (jnp.zeros((1, tq), jnp.int32),) * groups)
    lower = (lax.broadcasted_iota(jnp.int32, (n_sb, n_sb), 0)
             > lax.broadcasted_iota(jnp.int32, (n_sb, n_sb), 1)).astype(BF16)
    q_aug = []
    for g in range(groups):
        above = imps[g] > taus[g]
        equal = imps[g] == taus[g]
        need = n_sel - jnp.sum(jnp.where(above, 1, 0), axis=0, keepdims=True)
        earlier = _dot(lower, jnp.where(equal, 1.0, 0.0).astype(BF16))
        selected = above | (equal & (earlier < need.astype(F32)))
        sel_bias = jnp.where(selected, 0.0, NEG_INF)
        if n_sb < HEAD_DIM:
            sel_bias = jnp.concatenate([sel_bias, jnp.zeros((HEAD_DIM - n_sb, tq), F32)], axis=0)
        q_aug.append(jnp.concatenate([q_rot[g], jnp.concatenate([sel_bias] * hpg, axis=1)], axis=0).astype(BF16))

    def scores(g, kt):
        k0 = pl.multiple_of(kt * tk, tk)
        return _dot(ksa_ref[g, pl.ds(k0, tk), :], q_aug[g])

    def values(kt):
        k0 = pl.multiple_of(kt * tk, tk)
        return [vs_ref[g, :, pl.ds(k0, tk)] for g in range(groups)]

    def sel_body(kt, carry):
        return flash_steps([scores(g, kt) for g in range(groups)], values(kt), carry)

    n_full = t0 // tk
    carry = lax.fori_loop(0, n_full, sel_body, init)
    kpos = n_full * tk + lax.broadcasted_iota(jnp.int32, (tk, 1), 0)
    mask_s = kpos <= tpos_r
    carry = flash_steps([jnp.where(mask_s, scores(g, n_full), NEG_INF) for g in range(groups)], values(n_full), carry)

    gates = _sigmoid(gate_ref[...].T)
    nw = nw_ref[...]
    outs = []
    for g in range(groups):
        acc_s = carry[g][1]
        o_s = acc_s[0:HEAD_DIM] * (1.0 / acc_s[HEAD_DIM:HEAD_DIM + 1])
        for h in range(hpg):
            sl = slice(h * tq, (h + 1) * tq)
            r = (g * hpg + h) * N_GATES
            o = gates[r:r + 1] * o_c[g][:, sl] + gates[r + 1:r + 2] * o_s[:, sl] + gates[r + 2:r + 3] * o_w[g][:, sl]
            ms = jnp.mean(o * o, axis=0, keepdims=True)
            outs.append(o * lax.rsqrt(ms + RMS_EPS) * nw[:, g * hpg + h:g * hpg + h + 1])
    o_ref[...] = jnp.concatenate(outs, axis=0).T.astype(o_ref.dtype)


def nsa_attention(h, cos_t, sin_t, kc, vc_t, ksa, vs_t, kw, vw_t, norm_w, batch, seq, q_col0, gate_col):
    g, hpg = NSA_KV_GROUPS, NSA_HPG
    tq = min(seq, 256)
    nq = seq // tq
    n_sb = seq // SEL_BLOCK
    assert n_sb <= HEAD_DIM, "selection-block one-hot shares the 64 spare key lanes"
    n_cmp = kc.shape[2]
    units = np.arange(n_cmp)[:, None] + np.arange(CMP_BLOCK // CMP_STRIDE)[None, :]
    ovl = np.zeros((n_cmp, n_sb), np.float32)
    for c in range((seq - CMP_BLOCK) // CMP_STRIDE + 1):
        for u in units[c]:
            ovl[c, u // (SEL_BLOCK // CMP_STRIDE)] += 1.0
    ovl_t = jnp.asarray(ovl.T)

    def per_b(shape):
        return pl.BlockSpec((None, g) + shape, lambda b, qi: (b, 0, 0, 0))

    width = g * hpg * HEAD_DIM
    tab = pl.BlockSpec((HEAD_DIM, tq), lambda b, qi: (0, b * nq + qi))
    tk = min(seq, 512)
    assert seq >= WINDOW + tq and seq % tk == 0 and tk % tq == 0
    kern = functools.partial(_nsa_kernel, tq=tq, tk=tk, n_sb=n_sb)
    return pl.pallas_call(
        kern, grid=(batch, nq),
        in_specs=[pl.BlockSpec((tq, width), lambda b, qi: (b * nq + qi, q_col0)),
                  pl.BlockSpec((tq, LANES), lambda b, qi: (b * nq + qi, gate_col)),
                  tab, tab,
                  per_b((n_cmp, HEAD_DIM)), per_b((HEAD_DIM, n_cmp)),
                  per_b((seq, 2 * HEAD_DIM)), per_b((V_ROWS, seq)),
                  per_b((seq, HEAD_DIM)), per_b((V_ROWS, seq)),
                  pl.BlockSpec((n_sb, n_cmp), lambda b, qi: (0, 0)),
                  pl.BlockSpec((HEAD_DIM, g * hpg), lambda b, qi: (0, 0))],
        out_specs=pl.BlockSpec((tq, width), lambda b, qi: (b * nq + qi, 0)),
        out_shape=jax.ShapeDtypeStruct((batch * seq, width), BF16),
        compiler_params=_params("parallel", "arbitrary"), name="nsa_attention",
    )(h, h, cos_t, sin_t, kc, vc_t, ksa, vs_t, kw, vw_t, ovl_t, norm_w.reshape(g * hpg, HEAD_DIM).T)


def _out_proj_kernel(x_ref, yhg_ref, ygm_ref, ynsa_ref, whg_ref, wgm_ref, wnsa_ref, lnw_ref, lnb_ref,
                     o_ref, oa_ref, ob_ref, *, alpha):
    mix = (_dot(yhg_ref[...], whg_ref[...]) + _dot(ygm_ref[...], wgm_ref[...])
           + _dot(ynsa_ref[...], wnsa_ref[...]))
    y = _layer_norm(alpha * x_ref[...] + mix, lnw_ref[...], lnb_ref[...])
    o_ref[...] = y
    _store_word_tables((oa_ref, ob_ref), _pack_bf16_pairs(y))


def out_proj_ln(x2d, y_hg, y_gm, y_nsa, w_out, ln_w, ln_b, alpha):
    n, d = x2d.shape
    w1, w2 = y_hg.shape[1], y_hg.shape[1] + y_gm.shape[1]
    whg = w_out[:w1].astype(BF16)
    wgm = w_out[w1:w2].astype(BF16)
    wnsa = w_out[w2:].astype(BF16)
    t = min(n, 512)

    def row(wd):
        return pl.BlockSpec((t, wd), lambda i: (i, 0))

    def const(shape):
        return pl.BlockSpec(shape, lambda i: (0,) * len(shape))

    kern = functools.partial(_out_proj_kernel, alpha=alpha)
    return pl.pallas_call(
        kern, grid=(n // t,),
        in_specs=[row(d), row(y_hg.shape[1]), row(y_gm.shape[1]), row(y_nsa.shape[1]),
                  const(whg.shape), const(wgm.shape), const(wnsa.shape), const((1, d)), const((1, d))],
        out_specs=[row(d), row(SC_ROW_WORDS), row(SC_ROW_WORDS)],
        out_shape=[jax.ShapeDtypeStruct((n, d), F32)] + [jax.ShapeDtypeStruct((n, SC_ROW_WORDS), jnp.uint32)] * 2,
        compiler_params=_params("parallel"), name="out_proj_ln",
    )(x2d, y_hg, y_gm, y_nsa, whg, wgm, wnsa, ln_w.reshape(1, d), ln_b.reshape(1, d))


def _router_kernel(x_ref, w_ref, b_ref, e_ref, p_ref, r_ref, cnt_ref, carry_ref):
    t = x_ref.shape[0]

    @pl.when(pl.program_id(0) == 0)
    def _():
        carry_ref[...] = jnp.zeros_like(carry_ref)

    x = x_ref[...]
    x_hi = x.astype(BF16)
    x_lo = (x - x_hi.astype(F32)).astype(BF16)
    w = w_ref[...]
    w_hi = w.astype(BF16)
    w_lo = (w - w_hi.astype(F32)).astype(BF16)
    logits = _dot_nt(w_hi, x_hi) + (_dot_nt(w_hi, x_lo) + _dot_nt(w_lo, x_hi)) + b_ref[...]
    n_e = logits.shape[0]
    sub = lax.broadcasted_iota(jnp.int32, logits.shape, 0)
    work = logits
    vals, idxs = [], []
    sel = jnp.zeros(logits.shape, F32)
    for _ in range(TOP_K):
        m = jnp.max(work, axis=0, keepdims=True)
        idx = jnp.min(jnp.where(work == m, sub, n_e), axis=0, keepdims=True)
        hit = sub == idx
        sel = jnp.where(hit, 1.0, sel)
        work = jnp.where(hit, -jnp.inf, work)
        vals.append(m)
        idxs.append(idx)
    exps = [jnp.exp(v - vals[0]) for v in vals]
    inv_den = 1.0 / (exps[0] + exps[1] + exps[2] + exps[3])
    earlier = (lax.broadcasted_iota(jnp.int32, (t, t), 0) < lax.broadcasted_iota(jnp.int32, (t, t), 1))
    before = _dot(sel.astype(BF16), earlier.astype(BF16)) + carry_ref[...]
    ranks = [jnp.sum(jnp.where(sub == idx, before, 0.0), axis=0, keepdims=True) for idx in idxs]
    e_ref[...] = jnp.concatenate(idxs, axis=0)
    p_ref[...] = jnp.concatenate([e * inv_den for e in exps], axis=0)
    r_ref[...] = jnp.concatenate(ranks, axis=0).astype(jnp.int32)
    carry_ref[...] = carry_ref[...] + jnp.sum(sel, axis=1, keepdims=True)
    cnt_ref[...] = carry_ref[...].astype(jnp.int32)


def moe_router(x2d, router_w, router_b):
    n, d = x2d.shape
    e = router_w.shape[1]
    t = min(n, 512)
    row4 = pl.BlockSpec((TOP_K, t), lambda i: (0, i))
    top_e, top_p, rank, counts = pl.pallas_call(
        _router_kernel, grid=(n // t,),
        in_specs=[pl.BlockSpec((t, d), lambda i: (i, 0)), pl.BlockSpec((e, d), lambda i: (0, 0)),
                  pl.BlockSpec((e, 1), lambda i: (0, 0))],
        out_specs=[row4, row4, row4, pl.BlockSpec((e, 1), lambda i: (0, 0))],
        out_shape=[jax.ShapeDtypeStruct((TOP_K, n), jnp.int32), jax.ShapeDtypeStruct((TOP_K, n), F32),
                   jax.ShapeDtypeStruct((TOP_K, n), jnp.int32), jax.ShapeDtypeStruct((e, 1), jnp.int32)],
        scratch_shapes=[pltpu.VMEM((e, 1), F32)],
        compiler_params=_params("arbitrary"), name="moe_router",
    )(x2d, router_w.T, router_b.reshape(e, 1))
    return top_e, top_p, rank, counts[:, 0]


def _expert_kernel(be_ref, valid_ref, xa_ref, xb_ref, wu_ref, bu_ref, wd_ref, bd_ref, oa_ref, ob_ref,
                   wu_bf, wd_bf):
    i = pl.program_id(0)
    f = wd_ref.shape[0]
    n_used = be_ref[pl.num_programs(0)]

    @pl.when((i == 0) | (be_ref[i] != be_ref[jnp.maximum(i - 1, 0)]))
    def _():
        wu_bf[...] = wu_ref[...].astype(BF16)
        wd_bf[...] = wd_ref[...].astype(BF16)

    @pl.when(i < n_used)
    def _():
        packed = jnp.concatenate([xa_ref[...], xb_ref[...]], axis=1)
        live = lax.broadcasted_iota(jnp.int32, packed.shape, 0) < valid_ref[i]
        x_lo, x_hi = _unpack_bf16_pairs(jnp.where(live, packed, jnp.uint32(0)))
        x = jnp.concatenate([x_lo.astype(BF16), x_hi.astype(BF16)], axis=1)
        hcat = _dot(x, wu_bf[...]) + bu_ref[...]
        glu = jnp.minimum(hcat[:, :f], SWIGLU_LIMIT)
        lin = jnp.clip(hcat[:, f:], -SWIGLU_LIMIT, SWIGLU_LIMIT)
        act = glu * _sigmoid(SWIGLU_ALPHA * glu) * (lin + 1.0)
        _store_word_tables((oa_ref, ob_ref), _pack_bf16_pairs(_dot(act.astype(BF16), wd_bf[...]) + bd_ref[...]))

    @pl.when(i >= n_used)
    def _():
        oa_ref[...] = jnp.zeros_like(oa_ref)
        ob_ref[...] = jnp.zeros_like(ob_ref)


def moe_experts(xa, xb, block_e, n_used, block_valid, w_up, b_up, w_down, b_down, layer):
    rows = xa.shape[0]
    _, e, d, f2 = w_up.shape
    f = f2 // 2
    nb = rows // EXPERT_BLOCK
    words = pl.BlockSpec((EXPERT_BLOCK, SC_ROW_WORDS), lambda i, be, nv: (i, 0))
    grid_spec = pltpu.PrefetchScalarGridSpec(
        num_scalar_prefetch=2, grid=(nb,),
        in_specs=[words, words,
                  pl.BlockSpec((None, None, d, f2), lambda i, be, nv: (layer, be[i], 0, 0)),
                  pl.BlockSpec((None, None, 1, f2), lambda i, be, nv: (layer, be[i], 0, 0)),
                  pl.BlockSpec((None, None, f, d), lambda i, be, nv: (layer, be[i], 0, 0)),
                  pl.BlockSpec((None, None, 1, d), lambda i, be, nv: (layer, be[i], 0, 0))],
        out_specs=[words, words],
        scratch_shapes=[pltpu.VMEM((d, f2), BF16), pltpu.VMEM((f, d), BF16)])
    depth = w_up.shape[0]
    return pl.pallas_call(
        _expert_kernel, grid_spec=grid_spec,
        out_shape=[jax.ShapeDtypeStruct((rows, SC_ROW_WORDS), jnp.uint32)] * 2,
        compiler_params=pltpu.CompilerParams(dimension_semantics=("arbitrary",), vmem_limit_bytes=EXPERT_VMEM_LIMIT),
        name="moe_experts",
    )(jnp.concatenate([block_e, n_used.reshape(1)]), block_valid, xa, xb, w_up, b_up.reshape(depth, e, 1, f2),
      w_down, b_down.reshape(depth, e, 1, d))


def _combine_kernel(x_ref, ya_ref, yb_ref, p_ref, lnw_ref, lnb_ref, o_ref, *, alpha):
    p = p_ref[...]
    moe = jnp.zeros(x_ref.shape, F32)
    for k in range(TOP_K):
        y_lo, y_hi = _unpack_bf16_pairs(jnp.concatenate([ya_ref[k], yb_ref[k]], axis=1))
        moe = moe + p[:, k:k + 1] * jnp.concatenate([y_lo, y_hi], axis=1)
    o_ref[...] = _layer_norm(alpha * x_ref[...] + moe, lnw_ref[...], lnb_ref[...])


def combine_ln(x2d, ya, yb, top_p, ln_w, ln_b, alpha):
    n, d = x2d.shape
    t = min(n, 512)
    kern = functools.partial(_combine_kernel, alpha=alpha)
    words = pl.BlockSpec((TOP_K, t, SC_ROW_WORDS), lambda i: (0, i, 0))
    return pl.pallas_call(
        kern, grid=(n // t,),
        in_specs=[pl.BlockSpec((t, d), lambda i: (i, 0)), words, words,
                  pl.BlockSpec((t, TOP_K), lambda i: (i, 0)),
                  pl.BlockSpec((1, d), lambda i: (0, 0)), pl.BlockSpec((1, d), lambda i: (0, 0))],
        out_specs=pl.BlockSpec((t, d), lambda i: (i, 0)),
        out_shape=jax.ShapeDtypeStruct((n, d), F32),
        compiler_params=_params("parallel"), name="moe_combine_ln",
    )(x2d, ya, yb, top_p, ln_w.reshape(1, d), ln_b.reshape(1, d))


def _sc_mesh():
    return plsc.VectorSubcoreMesh(core_axis_name="core", subcore_axis_name="subcore")


def sc_gather_rows(tables, idx):
    r = idx.shape[0]
    nt = len(tables)
    out = jax.ShapeDtypeStruct((r, SC_ROW_WORDS), tables[0].dtype)

    @pl.kernel(out_type=(out,) * nt, mesh=_sc_mesh(), name="sc_gather_rows")
    def gather(*refs):
        x_hbm, i_hbm, o_hbm = refs[:nt], refs[nt], refs[nt + 1:]
        for j in range(nt):
            def body(i_vmem, o_vmem, table=x_hbm[j]):
                pltpu.sync_copy(table.at[i_vmem.at[0]], o_vmem)

            pltpu.emit_pipeline(
                body, grid=(r // SC_WINDOW,),
                in_specs=[pl.BlockSpec((1, SC_WINDOW), lambda i: (0, i))],
                out_specs=[pl.BlockSpec((SC_WINDOW, SC_ROW_WORDS), lambda i: (i, 0))],
                core_axis_name=("core", "subcore"), dimension_semantics=(pltpu.PARALLEL,),
            )(i_hbm, o_hbm[j])

    return gather(*tables, idx.reshape(1, r))


def sc_scatter_rows(tables, dest_t, n_rows):
    n = tables[0].shape[0]
    nt = len(tables)
    copies = dest_t.shape[0]
    out = jax.ShapeDtypeStruct((n_rows, SC_ROW_WORDS), tables[0].dtype)

    @pl.kernel(out_type=(out,) * nt, mesh=_sc_mesh(), scratch_types=[], name="sc_scatter_rows")
    def scatter(*refs):
        x_hbm, i_hbm, o_hbm = refs[:nt], refs[nt], refs[nt + 1:]
        for j in range(nt):
            def body(x_vmem, i_vmem, out_j=o_hbm[j]):
                for k in range(copies):
                    pltpu.sync_copy(x_vmem, out_j.at[i_vmem.at[k]])

            pltpu.emit_pipeline(
                body, grid=(n // SC_WINDOW,),
                in_specs=[pl.BlockSpec((SC_WINDOW, SC_ROW_WORDS), lambda i: (i, 0)),
                          pl.BlockSpec((copies, SC_WINDOW), lambda i: (0, i))],
                out_specs=[],
                core_axis_name=("core", "subcore"), dimension_semantics=(pltpu.PARALLEL,),
            )(x_hbm[j], i_hbm)

    return scatter(*tables, dest_t)


def moe_ffn_ln(x_f32, x_packed, router_w, router_b, w_up, b_up, w_down, b_down, layer, ln_w, ln_b, alpha):
    n, d = x_f32.shape
    top_e, top_p, rank, counts = moe_router(x_f32, router_w, router_b)
    padded = (counts + EXPERT_BLOCK - 1) // EXPERT_BLOCK * EXPERT_BLOCK
    pad_end = jnp.cumsum(padded)
    pad_start = pad_end - padded
    n_assign = n * TOP_K
    n_blocks = -(-(n_assign + N_EXPERTS * (EXPERT_BLOCK - 1)) // EXPERT_BLOCK)
    dest_t = pad_start[top_e] + rank
    block_first = jnp.arange(n_blocks, dtype=jnp.int32) * EXPERT_BLOCK
    block_e = jnp.clip(jnp.sum((pad_end[None, :] <= block_first[:, None]).astype(jnp.int32), axis=1),
                       0, N_EXPERTS - 1)
    block_valid = jnp.clip(counts[block_e] - (block_first - pad_start[block_e]), 0, EXPERT_BLOCK)
    n_used = (pad_end[-1] // EXPERT_BLOCK).astype(jnp.int32)
    xa, xb = sc_scatter_rows(x_packed, dest_t, n_blocks * EXPERT_BLOCK)
    ya, yb = moe_experts(xa, xb, block_e, n_used, block_valid.astype(jnp.int32), w_up, b_up, w_down, b_down,
                         layer)
    ya, yb = sc_gather_rows((ya, yb), dest_t.reshape(-1))
    return combine_ln(x_f32, ya.reshape(TOP_K, n, SC_ROW_WORDS), yb.reshape(TOP_K, n, SC_ROW_WORDS), top_p.T,
                      ln_w, ln_b, alpha)


def kernel(x, positions, w_in, hg_lower_bounds, hg_norm_w, gm_ln_w, gm_ln_b, gm_spatial_w, gm_spatial_b, gm_norm_w, nsa_cmp_pe, nsa_cmp_w1, nsa_cmp_w2, nsa_norm_w, w_out, ln1_w, ln1_b, router_w, router_b, exp_w_up, exp_b_up, exp_w_down, exp_b_down, ln2_w, ln2_b):
    batch, seq, d = x.shape
    depth = w_in.shape[0]
    n = batch * seq
    alpha = (2 * depth) ** 0.25
    hg_w = hg_norm_w.shape[1]
    gm_w = gm_norm_w.shape[1]
    nsa_w = nsa_norm_w.shape[1]
    kv_w = NSA_KV_GROUPS * HEAD_DIM
    in_width = w_in.shape[2]
    off_gm = 4 * hg_w
    off_q = off_gm + 2 * gm_w
    off_kv = off_q + nsa_w
    off_gate = off_kv + 6 * kv_w
    width_pad = -(-in_width // LANES) * LANES

    cosf, sinf, cos_t, sin_t = rope_tables(positions)
    lb_all = jnp.cumsum(jax.nn.softmax(hg_lower_bounds.astype(F32), axis=0), axis=0)
    lb_all = lb_all - lb_all[0:1]

    x2d = x.reshape(n, d)
    for l in range(depth):
        w_l = jnp.pad(w_in[l], ((0, 0), (0, width_pad - in_width))).astype(BF16)
        h = in_proj(x2d, w_l)
        h3 = h.reshape(batch, seq, width_pad)
        y_hg = hgrn2(h3, lb_all[l], hg_norm_w[l]).reshape(n, hg_w)
        y_gm = gmlp(h, gm_ln_w[l], gm_ln_b[l], gm_spatial_w[l], gm_spatial_b[l], gm_norm_w[l],
                    off_gm // gm_w, off_gm // gm_w + 1)
        kc = compress(h3[:, :, off_kv:off_kv + kv_w], nsa_cmp_pe[l, 0], nsa_cmp_w1[l, 0], nsa_cmp_w2[l, 0], batch, seq)
        vc = compress(h3[:, :, off_kv + kv_w:off_kv + 2 * kv_w], nsa_cmp_pe[l, 1], nsa_cmp_w1[l, 1],
                      nsa_cmp_w2[l, 1], batch, seq)
        n_cmp = kc.shape[1]
        kc = kc.reshape(batch, n_cmp, NSA_KV_GROUPS, HEAD_DIM).transpose(0, 2, 1, 3)
        vc_t = vc.reshape(batch, n_cmp, NSA_KV_GROUPS, HEAD_DIM).transpose(0, 2, 3, 1)
        ksa, vs_t, kw, vw_t = nsa_kprep(h, cosf, sinf, batch, seq, (off_kv + 2 * kv_w) // LANES)
        y_nsa = nsa_attention(h, cos_t, sin_t, kc, vc_t, ksa, vs_t, kw, vw_t, nsa_norm_w[l], batch, seq,
                              off_q // nsa_w, off_gate // LANES)
        x1, x1a, x1b = out_proj_ln(x2d, y_hg, y_gm, y_nsa, w_out[l], ln1_w[l], ln1_b[l], alpha)
        x2d = moe_ffn_ln(x1, (x1a, x1b), router_w[l], router_b[l], exp_w_up, exp_b_up, exp_w_down, exp_b_down, l,
                         ln2_w[l], ln2_b[l], alpha)
    return x2d.reshape(batch, seq, d)
```

```python
import functools
import math

import numpy as np
import jax
import jax.numpy as jnp
from jax import lax
from jax.experimental import pallas as pl
from jax.experimental.pallas import tpu as pltpu
from jax.experimental.pallas import tpu_sc as plsc

F32 = jnp.float32
BF16 = jnp.bfloat16
HIGHEST = lax.Precision.HIGHEST

HEAD_DIM = 64
LANES = 128
VMEM_LIMIT = 48 * 1024 * 1024
EXPERT_VMEM_LIMIT = 56 * 1024 * 1024

HG_CHUNK = 64
GM_CHUNK = 128
GM_TILE_CHUNKS = 4
NSA_KV_GROUPS = 2
NSA_HPG = 4
CMP_BLOCK = 32
CMP_STRIDE = 16
CMP_HIDDEN = 128
SEL_BLOCK = 64
N_SEL = 16
WINDOW = 512
N_GATES = 3
IMP_FORCE = 1e9
FORCE_KEY = int(np.float32(IMP_FORCE).view(np.int32))
NEG_INF = -1e30
N_EXPERTS = 32
TOP_K = 4
SWIGLU_ALPHA = 1.702
SWIGLU_LIMIT = 7.0
EXPERT_BLOCK = 512
SC_ROW_WORDS = 256
SC_WINDOW = 128
ROPE_THETA = 10000.0
LOG2_E = 1.4426950408889634
LN_EPS = 1e-5
RMS_EPS = 1e-6
V_ROWS = HEAD_DIM + 16


def _params(*sem):
    return pltpu.CompilerParams(dimension_semantics=sem, vmem_limit_bytes=VMEM_LIMIT)


def _dot(a, b):
    return jnp.dot(a, b, preferred_element_type=F32)


def _dot_nt(a, b, precision=None):
    return lax.dot_general(a, b, (((1,), (1,)), ((), ())), precision=precision,
                           preferred_element_type=F32)


def _dot_tn(a, b):
    return lax.dot_general(a, b, (((0,), (0,)), ((), ())), preferred_element_type=F32)


def _sigmoid(x):
    return 1.0 / (1.0 + jnp.exp(-x))


def _gelu(x):
    return 0.5 * x * (1.0 + jnp.tanh(0.7978845608028654 * (x + 0.044715 * x * x * x)))


def _layer_norm(x, w, b):
    mu = jnp.mean(x, axis=-1, keepdims=True)
    xc = x - mu
    var = jnp.mean(xc * xc, axis=-1, keepdims=True)
    return xc * lax.rsqrt(var + LN_EPS) * w + b


def _pack_bf16_pairs(y):
    w = y.shape[1] // 2
    bits = pltpu.bitcast(y.astype(BF16).astype(F32), jnp.uint32)
    return lax.shift_right_logical(bits[:, :w], jnp.uint32(16)) | (bits[:, w:] & jnp.uint32(0xFFFF0000))


def _unpack_bf16_pairs(u):
    lo = pltpu.bitcast(lax.shift_left(u, jnp.uint32(16)), F32)
    hi = pltpu.bitcast(u & jnp.uint32(0xFFFF0000), F32)
    return lo, hi


def _store_word_tables(refs, packed):
    for j, ref in enumerate(refs):
        ref[...] = packed[:, j * SC_ROW_WORDS:(j + 1) * SC_ROW_WORDS]


def _head_mean_sq(o, bd_ones):
    sq = o * o
    hi = sq.astype(BF16)
    lo = (sq - hi.astype(F32)).astype(BF16)
    ones = bd_ones.astype(BF16)
    return (_dot(hi, ones) + _dot(lo, ones)) * (1.0 / HEAD_DIM)


def _rope_kernel(pos_ref, inv_ref, cos_ref, sin_ref, cost_ref, sint_ref):
    ang = inv_ref[...] * pos_ref[...]
    c = jnp.cos(ang)
    s = jnp.sin(ang)
    cos_t = jnp.concatenate([c, c], axis=0)
    sin_t = jnp.concatenate([-s, s], axis=0)
    cost_ref[...] = cos_t
    sint_ref[...] = sin_t
    cos_ref[...] = cos_t.T
    sin_ref[...] = sin_t.T


def rope_tables(positions):
    n = positions.size
    tile = min(n, 2048)
    posf = positions.reshape(1, n).astype(F32)
    inv = ROPE_THETA ** (-jnp.arange(0, HEAD_DIM, 2, dtype=F32) / HEAD_DIM)
    row = pl.BlockSpec((tile, HEAD_DIM), lambda i: (i, 0))
    rowt = pl.BlockSpec((HEAD_DIM, tile), lambda i: (0, i))
    return pl.pallas_call(
        _rope_kernel, grid=(n // tile,),
        in_specs=[pl.BlockSpec((1, tile), lambda i: (0, i)), pl.BlockSpec((HEAD_DIM // 2, 1), lambda i: (0, 0))],
        out_specs=[row, row, rowt, rowt],
        out_shape=[jax.ShapeDtypeStruct((n, HEAD_DIM), F32)] * 2 + [jax.ShapeDtypeStruct((HEAD_DIM, n), F32)] * 2,
        compiler_params=_params("parallel"), name="rope_tables",
    )(posf, inv.reshape(HEAD_DIM // 2, 1))


def _in_proj_kernel(x_ref, w_ref, h_ref):
    h_ref[...] = _dot(x_ref[...].astype(BF16), w_ref[...])


def in_proj(x2d, w_bf16):
    n, d = x2d.shape
    width = w_bf16.shape[1]
    tile = min(n, 512)
    return pl.pallas_call(
        _in_proj_kernel, grid=(n // tile,),
        in_specs=[pl.BlockSpec((tile, d), lambda i: (i, 0)), pl.BlockSpec((d, width), lambda i: (0, 0))],
        out_specs=pl.BlockSpec((tile, width), lambda i: (i, 0)),
        out_shape=jax.ShapeDtypeStruct((n, width), F32),
        compiler_params=_params("parallel"), name="in_proj")(x2d, w_bf16)


HG_LEVELS = (64, 32, 16, 8, 4, 2)
HG_BATCH = 8


def _hgrn_constants():
    c = HG_CHUNK
    t = np.arange(c)
    u = t[None, :]
    rows = [u <= t[:, None], u > t[:, None]]
    masks = [np.eye(c, dtype=bool)]
    for m in HG_LEVELS:
        ref = ((t // m) * m + m // 2 - 1)[:, None]
        second = (t % m >= m // 2)[:, None]
        rows.append(((u > ref) & (u <= t[:, None]) & second) | ((u > t[:, None]) & (u <= ref) & ~second))
        masks.append((t[:, None] // m == t[None, :] // m) & second & (t[None, :] % m < m // 2))
    pmat = np.concatenate(rows, axis=0).astype(np.float32)
    masks = np.stack([np.tile(mk, (1, 4)) for mk in masks]).astype(np.float32)
    return pmat, masks


def _hgrn_kernel(q_ref, f_ref, i_ref, g_ref, lb_ref, nw_ref, pmat_ref, masks_ref, bd_ref, hm_ref,
                 o_ref, state_ref):
    c = HG_CHUNK

    @pl.when(pl.program_id(1) == 0)
    def _():
        state_ref[...] = jnp.zeros_like(state_ref)

    lb = lb_ref[...]
    bd = bd_ref[...]
    hm = hm_ref[...]
    hm_tiles = [jnp.broadcast_to(hm[h:h + 1], (c, hm.shape[1])).astype(BF16) for h in range(4)]
    pmat = pmat_ref[...]
    a = jnp.log(lb)
    log1m = jnp.log(1.0 - lb)
    nb, _, w = q_ref.shape
    seqs = range(nb)

    fr = f_ref[...].reshape(nb * c, w)
    hq = q_ref[...].reshape(nb * c, w)
    v = i_ref[...].reshape(nb * c, w)
    qf = hq * _sigmoid(hq)
    log_sig = jnp.minimum(fr, 0.0) - jnp.log(1.0 + jnp.exp(-jnp.abs(fr)))
    cc = log1m + log_sig
    log_f = jnp.maximum(a, cc) + jnp.log(1.0 + jnp.exp(-jnp.abs(a - cc)))
    kk = (1.0 - lb) * _sigmoid(-fr)

    hi = log_f.astype(BF16)
    lo = (log_f - hi.astype(F32)).astype(BF16)

    def side_by_side(x):
        return jnp.concatenate([x[b * c:(b + 1) * c] for b in seqs], axis=1)

    sums = jnp.minimum(_dot(pmat, side_by_side(hi)) + _dot(pmat, side_by_side(lo)), 0.0)
    e_all = jnp.exp(sums)

    def e_rows(r, b):
        return e_all[r * c:(r + 1) * c, b * w:(b + 1) * w]

    def stacked(x):
        xb = x.astype(BF16)
        return jnp.concatenate([xb * hm_tiles[h] for h in range(4)], axis=0)

    q_s = [qf[b * c:(b + 1) * c] for b in seqs]
    k_s = [kk[b * c:(b + 1) * c] for b in seqs]
    v_s = [v[b * c:(b + 1) * c] for b in seqs]

    att = [masks_ref[0] * _dot_nt(q_s[b].astype(BF16), stacked(k_s[b])) for b in seqs]
    for li in range(len(HG_LEVELS)):
        for b in seqs:
            e_l = e_rows(2 + li, b)
            att[b] = att[b] + masks_ref[li + 1] * _dot_nt((q_s[b] * e_l).astype(BF16), stacked(k_s[b] * e_l))

    outs = []
    for b in seqs:
        e_b = e_rows(0, b)
        st = state_ref[b]
        o = _dot(att[b].astype(BF16), stacked(v_s[b])) + _dot_nt((q_s[b] * e_b).astype(BF16), st.astype(BF16))
        k_rest = (k_s[b] * e_rows(1, b)).astype(BF16)
        state_ref[b] = st * e_b[c - 1:c] + bd * _dot_tn(v_s[b].astype(BF16), k_rest)
        outs.append(o)

    o = jnp.concatenate(outs, axis=0)
    ms = _head_mean_sq(o, bd)
    y = o * lax.rsqrt(ms + RMS_EPS) * nw_ref[...] * _sigmoid(g_ref[...].reshape(nb * c, w))
    o_ref[...] = y.astype(o_ref.dtype).reshape(nb, c, w)


def hgrn2(h3, lb, norm_w):
    batch, seq, _ = h3.shape
    w = lb.shape[-1]
    c = HG_CHUNK
    nb = math.gcd(batch, HG_BATCH)
    pmat, masks = _hgrn_constants()
    lane_head = np.arange(w) // HEAD_DIM
    bd = (lane_head[:, None] == lane_head[None, :]).astype(np.float32)
    hm = (np.arange(4)[:, None] == lane_head[None, :]).astype(np.float32)

    def col(j):
        return pl.BlockSpec((nb, c, w), lambda b, i, j=j: (b, i, j))

    def const(shape):
        return pl.BlockSpec(shape, lambda b, i: (0,) * len(shape))

    return pl.pallas_call(
        _hgrn_kernel, grid=(batch // nb, seq // c),
        in_specs=[col(0), col(1), col(2), col(3), const((1, w)), const((1, w)),
                  const(pmat.shape), const(masks.shape), const(bd.shape), const(hm.shape)],
        out_specs=pl.BlockSpec((nb, c, w), lambda b, i: (b, i, 0)),
        out_shape=jax.ShapeDtypeStruct((batch, seq, w), BF16),
        scratch_shapes=[pltpu.VMEM((nb, w, w), F32)],
        compiler_params=_params("parallel", "arbitrary"), name="hgrn2",
    )(h3, h3, h3, h3, lb.reshape(1, w), norm_w.reshape(1, w), jnp.asarray(pmat, BF16), jnp.asarray(masks),
      jnp.asarray(bd), jnp.asarray(hm))


def _gmlp_kernel(u_ref, v_ref, lnw_ref, lnb_ref, ws_ref, bias_ref, nw_ref, bd_ref, hm_ref, o_ref):
    c = GM_CHUNK
    groups = ws_ref.shape[0]
    u = _gelu(u_ref[...])
    v = _layer_norm(_gelu(v_ref[...]), lnw_ref[...], lnb_ref[...])
    hm = hm_ref[...]
    bd = bd_ref[...]
    causal = lax.broadcasted_iota(jnp.int32, (c, c), 0) >= lax.broadcasted_iota(jnp.int32, (c, c), 1)
    w_cat = jnp.concatenate([jnp.where(causal, ws_ref[g], 0.0).astype(BF16) for g in range(groups)], axis=1)
    for j in range(u.shape[0] // c):
        rows = slice(j * c, (j + 1) * c)
        v_j = v[rows]
        v_bd = jnp.concatenate([v_j * hm[g:g + 1] for g in range(groups)], axis=0).astype(BF16)
        y = u[rows] * (bias_ref[...] + _dot(w_cat, v_bd))
        ms = _head_mean_sq(y, bd)
        o_ref[rows, :] = (y * lax.rsqrt(ms + RMS_EPS) * nw_ref[...]).astype(o_ref.dtype)


def gmlp(h, ln_w, ln_b, w_s, b_s, norm_w, u_col, v_col):
    n = h.shape[0]
    groups, c, _ = w_s.shape
    w = groups * HEAD_DIM
    lane_head = np.arange(w) // HEAD_DIM
    bd = (lane_head[:, None] == lane_head[None, :]).astype(np.float32)
    hm = (np.arange(groups)[:, None] == lane_head[None, :]).astype(np.float32)
    bias = jnp.repeat(b_s.T, HEAD_DIM, axis=1)

    def const(shape):
        return pl.BlockSpec(shape, lambda i: (0,) * len(shape))

    t = math.gcd(n, GM_TILE_CHUNKS * c)
    return pl.pallas_call(
        _gmlp_kernel, grid=(n // t,),
        in_specs=[pl.BlockSpec((t, w), lambda i: (i, u_col)), pl.BlockSpec((t, w), lambda i: (i, v_col)),
                  const((1, w)), const((1, w)), const(w_s.shape), const((c, w)), const((1, w)),
                  const(bd.shape), const(hm.shape)],
        out_specs=pl.BlockSpec((t, w), lambda i: (i, 0)),
        out_shape=jax.ShapeDtypeStruct((n, w), BF16),
        compiler_params=_params("parallel"), name="gmlp",
    )(h, h, ln_w.reshape(1, w), ln_b.reshape(1, w), w_s, bias, norm_w.reshape(1, w),
      jnp.asarray(bd), jnp.asarray(hm))


def _compress_kernel(u_ref, wtop_ref, wbot_ref, pe_ref, w2_ref, o_ref):
    u = u_ref[...].astype(BF16)
    wtop = wtop_ref[...]
    wbot = wbot_ref[...]
    pe = pe_ref[...].astype(BF16)
    const = _dot(pe[0:1], wtop) + _dot(pe[1:2], wbot)
    p = _dot(u, wtop)
    q = _dot(u, wbot)
    q_next = jnp.concatenate([q[1:], jnp.zeros_like(q[0:1])], axis=0)
    hid = _gelu(p + q_next + const)
    o_ref[...] = _dot(hid.astype(BF16), w2_ref[...]).astype(o_ref.dtype)


def compress(kv, pe, w1, w2, batch, seq):
    g = NSA_KV_GROUPS
    half = CMP_STRIDE
    units = seq // half
    gw = g * HEAD_DIM
    u = kv.reshape(batch * units, half * gw)
    eye = jnp.eye(g, dtype=F32)
    w1r = w1.reshape(2, half, HEAD_DIM, CMP_HIDDEN)
    wbd = jnp.einsum('hjdn,gk->hjgdkn', w1r, eye).reshape(2, half * gw, g * CMP_HIDDEN).astype(BF16)
    w2bd = jnp.einsum('nd,gk->gnkd', w2, eye).reshape(g * CMP_HIDDEN, gw).astype(BF16)
    pe2 = jnp.broadcast_to(pe.reshape(2, half, 1, HEAD_DIM), (2, half, g, HEAD_DIM)).reshape(2, half * gw)

    def const(shape):
        return pl.BlockSpec(shape, lambda b: (0,) * len(shape))

    out = pl.pallas_call(
        _compress_kernel, grid=(batch,),
        in_specs=[pl.BlockSpec((units, half * gw), lambda b: (b, 0)),
                  const(wbd.shape[1:]), const(wbd.shape[1:]), const(pe2.shape), const(w2bd.shape)],
        out_specs=pl.BlockSpec((units, gw), lambda b: (b, 0)),
        out_shape=jax.ShapeDtypeStruct((batch * units, gw), BF16),
        compiler_params=_params("parallel"), name="nsa_compress",
    )(u, wbd[0], wbd[1], pe2, w2bd)
    return out.reshape(batch, units, gw)


def _rot_half_pairs(x):
    lane = lax.broadcasted_iota(jnp.int32, x.shape, 1)
    fwd = pltpu.roll(x, 32, axis=1)
    bwd = pltpu.roll(x, 96, axis=1)
    return jnp.where((lane % HEAD_DIM) < HEAD_DIM // 2, bwd, fwd)


def _kprep_kernel(ks_ref, vs_ref, kw_ref, vw_ref, cos_ref, sin_ref, ksa_ref, vso_ref, kwo_ref, vwo_ref):
    t = ks_ref.shape[0]
    cos = cos_ref[...]
    sin = sin_ref[...]
    cos2 = jnp.concatenate([cos, cos], axis=1)
    sin2 = jnp.concatenate([sin, sin], axis=1)
    ks = ks_ref[...]
    kw = kw_ref[...]
    ks_r = ks * cos2 + _rot_half_pairs(ks) * sin2
    kw_r = kw * cos2 + _rot_half_pairs(kw) * sin2
    pos = pl.program_id(1) * t + lax.broadcasted_iota(jnp.int32, (t, HEAD_DIM), 0)
    onehot = (pos // SEL_BLOCK == lax.broadcasted_iota(jnp.int32, (t, HEAD_DIM), 1)).astype(F32)
    vs_t = vs_ref[...].T
    vw_t = vw_ref[...].T
    tail = (lax.broadcasted_iota(jnp.int32, (V_ROWS - HEAD_DIM, t), 0) == 0).astype(F32)
    for g in range(NSA_KV_GROUPS):
        sl = slice(g * HEAD_DIM, (g + 1) * HEAD_DIM)
        ksa_ref[g] = jnp.concatenate([ks_r[:, sl], onehot], axis=1).astype(BF16)
        vso_ref[g] = jnp.concatenate([vs_t[sl], tail], axis=0).astype(BF16)
        kwo_ref[g] = kw_r[:, sl].astype(BF16)
        vwo_ref[g] = jnp.concatenate([vw_t[sl], tail], axis=0).astype(BF16)


def nsa_kprep(h, cosf, sinf, batch, seq, col0):
    g = NSA_KV_GROUPS
    t = min(seq, 512)
    nt = seq // t

    def col(j):
        return pl.BlockSpec((t, LANES), lambda b, i, j=j: (b * nt + i, col0 + j))

    tab = pl.BlockSpec((t, HEAD_DIM), lambda b, i: (b * nt + i, 0))

    def out(wd):
        return pl.BlockSpec((None, g, t, wd), lambda b, i: (b, 0, i, 0))

    out_t = pl.BlockSpec((None, g, V_ROWS, t), lambda b, i: (b, 0, 0, i))
    k_shape = jax.ShapeDtypeStruct((batch, g, seq, HEAD_DIM), BF16)
    v_shape = jax.ShapeDtypeStruct((batch, g, V_ROWS, seq), BF16)
    return pl.pallas_call(
        _kprep_kernel, grid=(batch, nt),
        in_specs=[col(0), col(1), col(2), col(3), tab, tab],
        out_specs=[out(2 * HEAD_DIM), out_t, out(HEAD_DIM), out_t],
        out_shape=[jax.ShapeDtypeStruct((batch, g, seq, 2 * HEAD_DIM), BF16), v_shape, k_shape, v_shape],
        compiler_params=_params("parallel", "parallel"), name="nsa_kprep",
    )(h, h, h, h, cosf, sinf)


def _nsa_kernel(hq_ref, gate_ref, cos_ref, sin_ref, kc_ref, vc_ref, ksa_ref, vs_ref, kw_ref, vw_ref,
                ovl_ref, nw_ref, o_ref, *, tq, tk, n_sb):
    qi = pl.program_id(1)
    hpg = NSA_HPG
    groups = NSA_KV_GROUPS
    rows = hpg * tq
    t0 = qi * tq
    scale = 1.0 / math.sqrt(HEAD_DIM)
    half = HEAD_DIM // 2

    hq_t = hq_ref[...].T
    cos = cos_ref[...]
    sin = sin_ref[...]
    q_raw, q_rot = [], []
    for g in range(groups):
        raw_g, rot_g = [], []
        for h in range(hpg):
            r0 = (g * hpg + h) * HEAD_DIM
            qh = hq_t[r0:r0 + HEAD_DIM]
            swapped = jnp.concatenate([qh[half:], qh[:half]], axis=0)
            raw_g.append(qh * (scale * LOG2_E))
            rot_g.append((qh * cos + swapped * sin) * (scale * LOG2_E))
        q_raw.append(jnp.concatenate(raw_g, axis=1).astype(BF16))
        q_rot.append(jnp.concatenate(rot_g, axis=1))

    tpos = t0 + lax.broadcasted_iota(jnp.int32, (1, tq), 1)
    tpos_r = jnp.concatenate([tpos] * hpg, axis=1)

    def flash_steps(s, v_t, carry):
        m_new = [jnp.maximum(carry[g][0], jnp.max(s[g], axis=0, keepdims=True)) for g in range(groups)]
        p = [jnp.exp2(s[g] - m_new[g]).astype(BF16) for g in range(groups)]
        return tuple((m_new[g], jnp.exp2(carry[g][0] - m_new[g]) * carry[g][1] + _dot(v_t[g], p[g]))
                     for g in range(groups))

    init = ((jnp.full((1, rows), NEG_INF, F32), jnp.zeros((V_ROWS, rows), F32)),) * groups

    wk = WINDOW + tq
    kw0 = pl.multiple_of(jnp.maximum(t0 - WINDOW, 0), tq)
    kpos_w = kw0 + lax.broadcasted_iota(jnp.int32, (wk, 1), 0)
    mask_w = (kpos_w <= tpos_r) & (kpos_w > tpos_r - WINDOW)
    n_pad = jnp.maximum(WINDOW - 1 - tpos_r, 0).astype(F32)
    n_cmp = kc_ref.shape[1]
    cmp_end = lax.broadcasted_iota(jnp.int32, (n_cmp, 1), 0) * CMP_STRIDE + (CMP_BLOCK - 1)
    mask_c = cmp_end <= tpos_r
    blk = lax.broadcasted_iota(jnp.int32, (n_sb, tq), 0)
    cur = (t0 + lax.broadcasted_iota(jnp.int32, (n_sb, tq), 1)) // SEL_BLOCK
    s_w = [jnp.where(mask_w, _dot(kw_ref[g, pl.ds(kw0, wk), :], q_rot[g].astype(BF16)), NEG_INF)
           for g in range(groups)]
    s_c = [jnp.where(mask_c, _dot(kc_ref[g], q_raw[g]), NEG_INF) for g in range(groups)]
    win = flash_steps(s_w, [vw_ref[g, :, pl.ds(kw0, wk)] for g in range(groups)], init)
    o_w, o_c, imps = [], [], []
    for g in range(groups):
        m_w, acc_w = win[g]
        m_f = jnp.where(n_pad > 0.0, jnp.maximum(m_w, 0.0), m_w)
        a_w = jnp.exp2(m_w - m_f)
        o_w.append(acc_w[0:HEAD_DIM] * (a_w / (acc_w[HEAD_DIM:HEAD_DIM + 1] * a_w + n_pad * jnp.exp2(-m_f))))

    for g in range(groups):
        e_c = jnp.exp2(s_c[g] - jnp.max(s_c[g], axis=0, keepdims=True))
        p_c = jnp.where(mask_c, e_c * (1.0 / jnp.sum(e_c, axis=0, keepdims=True)), 0.0)
        o_c.append(_dot(vc_ref[g], p_c.astype(BF16)))

        p_sum = p_c[:, 0:tq]
        for h in range(1, hpg):
            p_sum = p_sum + p_c[:, h * tq:(h + 1) * tq]
        imp = jnp.dot(ovl_ref[...], p_sum, precision=HIGHEST, preferred_element_type=F32)
        key = pltpu.bitcast(jnp.maximum(imp, 0.0), jnp.int32)
        key = jnp.where((blk == 0) | (blk == cur) | (blk == cur - 1), FORCE_KEY, key)
        imps.append(jnp.where(blk > cur, -1, key))

    n_sel = min(N_SEL, n_sb)

    def bit_body(it, taus):
        bit = lax.shift_left(jnp.int32(1), 30 - it)
        out = []
        for g in range(groups):
            cand = taus[g] | bit
            cnt = jnp.sum(jnp.where(imps[g] >= cand, 1, 0), axis=0, keepdims=True)
            out.append(jnp.where(cnt >= n_sel, cand, taus[g]))
        return tuple(out)

    taus = lax.fori_loop(0, 31, bit_body, (jnp.zeros((1, tq), jnp.int32),) * groups)
    lower = (lax.broadcasted_iota(jnp.int32, (n_sb, n_sb), 0)
             > lax.broadcasted_iota(jnp.int32, (n_sb, n_sb), 1)).astype(BF16)
    q_aug = []
    for g in range(groups):
        above = imps[g] > taus[g]
        equal = imps[g] == taus[g]
        need = n_sel - jnp.sum(jnp.where(above, 1, 0), axis=0, keepdims=True)
        earlier = _dot(lower, jnp.where(equal, 1.0, 0.0).astype(BF16))
        selected = above | (equal & (earlier < need.astype(F32)))
        sel_bias = jnp.where(selected, 0.0, NEG_INF)
        if n_sb < HEAD_DIM:
            sel_bias = jnp.concatenate([sel_bias, jnp.zeros((HEAD_DIM - n_sb, tq), F32)], axis=0)
        q_aug.append(jnp.concatenate([q_rot[g], jnp.concatenate([sel_bias] * hpg, axis=1)], axis=0).astype(BF16))

    def scores(g, kt):
        k0 = pl.multiple_of(kt * tk, tk)
        return _dot(ksa_ref[g, pl.ds(k0, tk), :], q_aug[g])

    def values(kt):
        k0 = pl.multiple_of(kt * tk, tk)
        return [vs_ref[g, :, pl.ds(k0, tk)] for g in range(groups)]

    def sel_body(kt, carry):
        return flash_steps([scores(g, kt) for g in range(groups)], values(kt), carry)

    n_full = t0 // tk
    carry = lax.fori_loop(0, n_full, sel_body, init)
    kpos = n_full * tk + lax.broadcasted_iota(jnp.int32, (tk, 1), 0)
    mask_s = kpos <= tpos_r
    carry = flash_steps([jnp.where(mask_s, scores(g, n_full), NEG_INF) for g in range(groups)], values(n_full), carry)

    gates = _sigmoid(gate_ref[...].T)
    nw = nw_ref[...]
    outs = []
    for g in range(groups):
        acc_s = carry[g][1]
        o_s = acc_s[0:HEAD_DIM] * (1.0 / acc_s[HEAD_DIM:HEAD_DIM + 1])
        for h in range(hpg):
            sl = slice(h * tq, (h + 1) * tq)
            r = (g * hpg + h) * N_GATES
            o = gates[r:r + 1] * o_c[g][:, sl] + gates[r + 1:r + 2] * o_s[:, sl] + gates[r + 2:r + 3] * o_w[g][:, sl]
            ms = jnp.mean(o * o, axis=0, keepdims=True)
            outs.append(o * lax.rsqrt(ms + RMS_EPS) * nw[:, g * hpg + h:g * hpg + h + 1])
    o_ref[...] = jnp.concatenate(outs, axis=0).T.astype(o_ref.dtype)


def nsa_attention(h, cos_t, sin_t, kc, vc_t, ksa, vs_t, kw, vw_t, norm_w, batch, seq, q_col0, gate_col):
    g, hpg = NSA_KV_GROUPS, NSA_HPG
    tq = min(seq, 256)
    nq = seq // tq
    n_sb = seq // SEL_BLOCK
    assert n_sb <= HEAD_DIM, "selection-block one-hot shares the 64 spare key lanes"
    n_cmp = kc.shape[2]
    units = np.arange(n_cmp)[:, None] + np.arange(CMP_BLOCK // CMP_STRIDE)[None, :]
    ovl = np.zeros((n_cmp, n_sb), np.float32)
    for c in range((seq - CMP_BLOCK) // CMP_STRIDE + 1):
        for u in units[c]:
            ovl[c, u // (SEL_BLOCK // CMP_STRIDE)] += 1.0
    ovl_t = jnp.asarray(ovl.T)

    def per_b(shape):
        return pl.BlockSpec((None, g) + shape, lambda b, qi: (b, 0, 0, 0))

    width = g * hpg * HEAD_DIM
    tab = pl.BlockSpec((HEAD_DIM, tq), lambda b, qi: (0, b * nq + qi))
    tk = min(seq, 512)
    assert seq >= WINDOW + tq and seq % tk == 0 and tk % tq == 0
    kern = functools.partial(_nsa_kernel, tq=tq, tk=tk, n_sb=n_sb)
    return pl.pallas_call(
        kern, grid=(batch, nq),
        in_specs=[pl.BlockSpec((tq, width), lambda b, qi: (b * nq + qi, q_col0)),
                  pl.BlockSpec((tq, LANES), lambda b, qi: (b * nq + qi, gate_col)),
                  tab, tab,
                  per_b((n_cmp, HEAD_DIM)), per_b((HEAD_DIM, n_cmp)),
                  per_b((seq, 2 * HEAD_DIM)), per_b((V_ROWS, seq)),
                  per_b((seq, HEAD_DIM)), per_b((V_ROWS, seq)),
                  pl.BlockSpec((n_sb, n_cmp), lambda b, qi: (0, 0)),
                  pl.BlockSpec((HEAD_DIM, g * hpg), lambda b, qi: (0, 0))],
        out_specs=pl.BlockSpec((tq, width), lambda b, qi: (b * nq + qi, 0)),
        out_shape=jax.ShapeDtypeStruct((batch * seq, width), BF16),
        compiler_params=_params("parallel", "arbitrary"), name="nsa_attention",
    )(h, h, cos_t, sin_t, kc, vc_t, ksa, vs_t, kw, vw_t, ovl_t, norm_w.reshape(g * hpg, HEAD_DIM).T)


def _out_proj_kernel(x_ref, yhg_ref, ygm_ref, ynsa_ref, whg_ref, wgm_ref, wnsa_ref, lnw_ref, lnb_ref,
                     o_ref, oa_ref, ob_ref, *, alpha):
    mix = (_dot(yhg_ref[...], whg_ref[...]) + _dot(ygm_ref[...], wgm_ref[...])
           + _dot(ynsa_ref[...], wnsa_ref[...]))
    y = _layer_norm(alpha * x_ref[...] + mix, lnw_ref[...], lnb_ref[...])
    o_ref[...] = y
    _store_word_tables((oa_ref, ob_ref), _pack_bf16_pairs(y))


def out_proj_ln(x2d, y_hg, y_gm, y_nsa, w_out, ln_w, ln_b, alpha):
    n, d = x2d.shape
    w1, w2 = y_hg.shape[1], y_hg.shape[1] + y_gm.shape[1]
    whg = w_out[:w1].astype(BF16)
    wgm = w_out[w1:w2].astype(BF16)
    wnsa = w_out[w2:].astype(BF16)
    t = min(n, 512)

    def row(wd):
        return pl.BlockSpec((t, wd), lambda i: (i, 0))

    def const(shape):
        return pl.BlockSpec(shape, lambda i: (0,) * len(shape))

    kern = functools.partial(_out_proj_kernel, alpha=alpha)
    return pl.pallas_call(
        kern, grid=(n // t,),
        in_specs=[row(d), row(y_hg.shape[1]), row(y_gm.shape[1]), row(y_nsa.shape[1]),
                  const(whg.shape), const(wgm.shape), const(wnsa.shape), const((1, d)), const((1, d))],
        out_specs=[row(d), row(SC_ROW_WORDS), row(SC_ROW_WORDS)],
        out_shape=[jax.ShapeDtypeStruct((n, d), F32)] + [jax.ShapeDtypeStruct((n, SC_ROW_WORDS), jnp.uint32)] * 2,
        compiler_params=_params("parallel"), name="out_proj_ln",
    )(x2d, y_hg, y_gm, y_nsa, whg, wgm, wnsa, ln_w.reshape(1, d), ln_b.reshape(1, d))


def _router_kernel(x_ref, w_ref, b_ref, e_ref, p_ref, r_ref, cnt_ref, carry_ref):
    t = x_ref.shape[0]

    @pl.when(pl.program_id(0) == 0)
    def _():
        carry_ref[...] = jnp.zeros_like(carry_ref)

    x = x_ref[...]
    x_hi = x.astype(BF16)
    x_lo = (x - x_hi.astype(F32)).astype(BF16)
    w = w_ref[...]
    w_hi = w.astype(BF16)
    w_lo = (w - w_hi.astype(F32)).astype(BF16)
    logits = _dot_nt(w_hi, x_hi) + (_dot_nt(w_hi, x_lo) + _dot_nt(w_lo, x_hi)) + b_ref[...]
    n_e = logits.shape[0]
    sub = lax.broadcasted_iota(jnp.int32, logits.shape, 0)
    work = logits
    vals, idxs = [], []
    sel = jnp.zeros(logits.shape, F32)
    for _ in range(TOP_K):
        m = jnp.max(work, axis=0, keepdims=True)
        idx = jnp.min(jnp.where(work == m, sub, n_e), axis=0, keepdims=True)
        hit = sub == idx
        sel = jnp.where(hit, 1.0, sel)
        work = jnp.where(hit, -jnp.inf, work)
        vals.append(m)
        idxs.append(idx)
    exps = [jnp.exp(v - vals[0]) for v in vals]
    inv_den = 1.0 / (exps[0] + exps[1] + exps[2] + exps[3])
    earlier = (lax.broadcasted_iota(jnp.int32, (t, t), 0) < lax.broadcasted_iota(jnp.int32, (t, t), 1))
    before = _dot(sel.astype(BF16), earlier.astype(BF16)) + carry_ref[...]
    ranks = [jnp.sum(jnp.where(sub == idx, before, 0.0), axis=0, keepdims=True) for idx in idxs]
    e_ref[...] = jnp.concatenate(idxs, axis=0)
    p_ref[...] = jnp.concatenate([e * inv_den for e in exps], axis=0)
    r_ref[...] = jnp.concatenate(ranks, axis=0).astype(jnp.int32)
    carry_ref[...] = carry_ref[...] + jnp.sum(sel, axis=1, keepdims=True)
    cnt_ref[...] = carry_ref[...].astype(jnp.int32)


def moe_router(x2d, router_w, router_b):
    n, d = x2d.shape
    e = router_w.shape[1]
    t = min(n, 512)
    row4 = pl.BlockSpec((TOP_K, t), lambda i: (0, i))
    top_e, top_p, rank, counts = pl.pallas_call(
        _router_kernel, grid=(n // t,),
        in_specs=[pl.BlockSpec((t, d), lambda i: (i, 0)), pl.BlockSpec((e, d), lambda i: (0, 0)),
                  pl.BlockSpec((e, 1), lambda i: (0, 0))],
        out_specs=[row4, row4, row4, pl.BlockSpec((e, 1), lambda i: (0, 0))],
        out_shape=[jax.ShapeDtypeStruct((TOP_K, n), jnp.int32), jax.ShapeDtypeStruct((TOP_K, n), F32),
                   jax.ShapeDtypeStruct((TOP_K, n), jnp.int32), jax.ShapeDtypeStruct((e, 1), jnp.int32)],
        scratch_shapes=[pltpu.VMEM((e, 1), F32)],
        compiler_params=_params("arbitrary"), name="moe_router",
    )(x2d, router_w.T, router_b.reshape(e, 1))
    return top_e, top_p, rank, counts[:, 0]


def _expert_kernel(be_ref, valid_ref, xa_ref, xb_ref, wu_ref, bu_ref, wd_ref, bd_ref, oa_ref, ob_ref,
                   wu_bf, wd_bf):
    i = pl.program_id(0)
    f = wd_ref.shape[0]
    n_used = be_ref[pl.num_programs(0)]

    @pl.when((i == 0) | (be_ref[i] != be_ref[jnp.maximum(i - 1, 0)]))
    def _():
        wu_bf[...] = wu_ref[...].astype(BF16)
        wd_bf[...] = wd_ref[...].astype(BF16)

    @pl.when(i < n_used)
    def _():
        packed = jnp.concatenate([xa_ref[...], xb_ref[...]], axis=1)
        live = lax.broadcasted_iota(jnp.int32, packed.shape, 0) < valid_ref[i]
        x_lo, x_hi = _unpack_bf16_pairs(jnp.where(live, packed, jnp.uint32(0)))
        x = jnp.concatenate([x_lo.astype(BF16), x_hi.astype(BF16)], axis=1)
        hcat = _dot(x, wu_bf[...]) + bu_ref[...]
        glu = jnp.minimum(hcat[:, :f], SWIGLU_LIMIT)
        lin = jnp.clip(hcat[:, f:], -SWIGLU_LIMIT, SWIGLU_LIMIT)
        act = glu * _sigmoid(SWIGLU_ALPHA * glu) * (lin + 1.0)
        _store_word_tables((oa_ref, ob_ref), _pack_bf16_pairs(_dot(act.astype(BF16), wd_bf[...]) + bd_ref[...]))

    @pl.when(i >= n_used)
    def _():
        oa_ref[...] = jnp.zeros_like(oa_ref)
        ob_ref[...] = jnp.zeros_like(ob_ref)


def moe_experts(xa, xb, block_e, n_used, block_valid, w_up, b_up, w_down, b_down, layer):
    rows = xa.shape[0]
    _, e, d, f2 = w_up.shape
    f = f2 // 2
    nb = rows // EXPERT_BLOCK
    words = pl.BlockSpec((EXPERT_BLOCK, SC_ROW_WORDS), lambda i, be, nv: (i, 0))
    grid_spec = pltpu.PrefetchScalarGridSpec(
        num_scalar_prefetch=2, grid=(nb,),
        in_specs=[words, words,
                  pl.BlockSpec((None, None, d, f2), lambda i, be, nv: (layer, be[i], 0, 0)),
                  pl.BlockSpec((None, None, 1, f2), lambda i, be, nv: (layer, be[i], 0, 0)),
                  pl.BlockSpec((None, None, f, d), lambda i, be, nv: (layer, be[i], 0, 0)),
                  pl.BlockSpec((None, None, 1, d), lambda i, be, nv: (layer, be[i], 0, 0))],
        out_specs=[words, words],
        scratch_shapes=[pltpu.VMEM((d, f2), BF16), pltpu.VMEM((f, d), BF16)])
    depth = w_up.shape[0]
    return pl.pallas_call(
        _expert_kernel, grid_spec=grid_spec,
        out_shape=[jax.ShapeDtypeStruct((rows, SC_ROW_WORDS), jnp.uint32)] * 2,
        compiler_params=pltpu.CompilerParams(dimension_semantics=("arbitrary",), vmem_limit_bytes=EXPERT_VMEM_LIMIT),
        name="moe_experts",
    )(jnp.concatenate([block_e, n_used.reshape(1)]), block_valid, xa, xb, w_up, b_up.reshape(depth, e, 1, f2),
      w_down, b_down.reshape(depth, e, 1, d))


def _combine_kernel(x_ref, ya_ref, yb_ref, p_ref, lnw_ref, lnb_ref, o_ref, *, alpha):
    p = p_ref[...]
    moe = jnp.zeros(x_ref.shape, F32)
    for k in range(TOP_K):
        y_lo, y_hi = _unpack_bf16_pairs(jnp.concatenate([ya_ref[k], yb_ref[k]], axis=1))
        moe = moe + p[:, k:k + 1] * jnp.concatenate([y_lo, y_hi], axis=1)
    o_ref[...] = _layer_norm(alpha * x_ref[...] + moe, lnw_ref[...], lnb_ref[...])


def combine_ln(x2d, ya, yb, top_p, ln_w, ln_b, alpha):
    n, d = x2d.shape
    t = min(n, 512)
    kern = functools.partial(_combine_kernel, alpha=alpha)
    words = pl.BlockSpec((TOP_K, t, SC_ROW_WORDS), lambda i: (0, i, 0))
    return pl.pallas_call(
        kern, grid=(n // t,),
        in_specs=[pl.BlockSpec((t, d), lambda i: (i, 0)), words, words,
                  pl.BlockSpec((t, TOP_K), lambda i: (i, 0)),
                  pl.BlockSpec((1, d), lambda i: (0, 0)), pl.BlockSpec((1, d), lambda i: (0, 0))],
        out_specs=pl.BlockSpec((t, d), lambda i: (i, 0)),
        out_shape=jax.ShapeDtypeStruct((n, d), F32),
        compiler_params=_params("parallel"), name="moe_combine_ln",
    )(x2d, ya, yb, top_p, ln_w.reshape(1, d), ln_b.reshape(1, d))


def _sc_mesh():
    return plsc.VectorSubcoreMesh(core_axis_name="core", subcore_axis_name="subcore")


def sc_gather_rows(tables, idx):
    r = idx.shape[0]
    nt = len(tables)
    out = jax.ShapeDtypeStruct((r, SC_ROW_WORDS), tables[0].dtype)

    @pl.kernel(out_type=(out,) * nt, mesh=_sc_mesh(), name="sc_gather_rows")
    def gather(*refs):
        x_hbm, i_hbm, o_hbm = refs[:nt], refs[nt], refs[nt + 1:]
        for j in range(nt):
            def body(i_vmem, o_vmem, table=x_hbm[j]):
                pltpu.sync_copy(table.at[i_vmem.at[0]], o_vmem)

            pltpu.emit_pipeline(
                body, grid=(r // SC_WINDOW,),
                in_specs=[pl.BlockSpec((1, SC_WINDOW), lambda i: (0, i))],
                out_specs=[pl.BlockSpec((SC_WINDOW, SC_ROW_WORDS), lambda i: (i, 0))],
                core_axis_name=("core", "subcore"), dimension_semantics=(pltpu.PARALLEL,),
            )(i_hbm, o_hbm[j])

    return gather(*tables, idx.reshape(1, r))


def sc_scatter_rows(tables, dest_t, n_rows):
    n = tables[0].shape[0]
    nt = len(tables)
    copies = dest_t.shape[0]
    out = jax.ShapeDtypeStruct((n_rows, SC_ROW_WORDS), tables[0].dtype)

    @pl.kernel(out_type=(out,) * nt, mesh=_sc_mesh(), scratch_types=[], name="sc_scatter_rows")
    def scatter(*refs):
        x_hbm, i_hbm, o_hbm = refs[:nt], refs[nt], refs[nt + 1:]
        for j in range(nt):
            def body(x_vmem, i_vmem, out_j=o_hbm[j]):
                for k in range(copies):
                    pltpu.sync_copy(x_vmem, out_j.at[i_vmem.at[k]])

            pltpu.emit_pipeline(
                body, grid=(n // SC_WINDOW,),
                in_specs=[pl.BlockSpec((SC_WINDOW, SC_ROW_WORDS), lambda i: (i, 0)),
                          pl.BlockSpec((copies, SC_WINDOW), lambda i: (0, i))],
                out_specs=[],
                core_axis_name=("core", "subcore"), dimension_semantics=(pltpu.PARALLEL,),
            )(x_hbm[j], i_hbm)

    return scatter(*tables, dest_t)


def moe_ffn_ln(x_f32, x_packed, router_w, router_b, w_up, b_up, w_down, b_down, layer, ln_w, ln_b, alpha):
    n, d = x_f32.shape
    top_e, top_p, rank, counts = moe_router(x_f32, router_w, router_b)
    padded = (counts + EXPERT_BLOCK - 1) // EXPERT_BLOCK * EXPERT_BLOCK
    pad_end = jnp.cumsum(padded)
    pad_start = pad_end - padded
    n_assign = n * TOP_K
    n_blocks = -(-(n_assign + N_EXPERTS * (EXPERT_BLOCK - 1)) // EXPERT_BLOCK)
    experts = jnp.arange(N_EXPERTS, dtype=jnp.int32)
    dest_t = rank + jnp.sum(jnp.where(top_e[..., None] == experts, pad_start.astype(jnp.int32), 0), axis=-1)
    block_first = jnp.arange(n_blocks, dtype=jnp.int32) * EXPERT_BLOCK
    block_e = jnp.clip(jnp.sum((pad_end[None, :] <= block_first[:, None]).astype(jnp.int32), axis=1),
                       0, N_EXPERTS - 1)
    block_valid = jnp.clip(counts[block_e] - (block_first - pad_start[block_e]), 0, EXPERT_BLOCK)
    n_used = (pad_end[-1] // EXPERT_BLOCK).astype(jnp.int32)
    xa, xb = sc_scatter_rows(x_packed, dest_t, n_blocks * EXPERT_BLOCK)
    ya, yb = moe_experts(xa, xb, block_e, n_used, block_valid.astype(jnp.int32), w_up, b_up, w_down, b_down,
                         layer)
    ya, yb = sc_gather_rows((ya, yb), dest_t.reshape(-1))
    return combine_ln(x_f32, ya.reshape(TOP_K, n, SC_ROW_WORDS), yb.reshape(TOP_K, n, SC_ROW_WORDS), top_p.T,
                      ln_w, ln_b, alpha)


def kernel(x, positions, w_in, hg_lower_bounds, hg_norm_w, gm_ln_w, gm_ln_b, gm_spatial_w, gm_spatial_b, gm_norm_w, nsa_cmp_pe, nsa_cmp_w1, nsa_cmp_w2, nsa_norm_w, w_out, ln1_w, ln1_b, router_w, router_b, exp_w_up, exp_b_up, exp_w_down, exp_b_down, ln2_w, ln2_b):
    batch, seq, d = x.shape
    depth = w_in.shape[0]
    n = batch * seq
    alpha = (2 * depth) ** 0.25
    hg_w = hg_norm_w.shape[1]
    gm_w = gm_norm_w.shape[1]
    nsa_w = nsa_norm_w.shape[1]
    kv_w = NSA_KV_GROUPS * HEAD_DIM
    in_width = w_in.shape[2]
    off_gm = 4 * hg_w
    off_q = off_gm + 2 * gm_w
    off_kv = off_q + nsa_w
    off_gate = off_kv + 6 * kv_w
    width_pad = -(-in_width // LANES) * LANES

    cosf, sinf, cos_t, sin_t = rope_tables(positions)
    lb_all = jnp.cumsum(jax.nn.softmax(hg_lower_bounds.astype(F32), axis=0), axis=0)
    lb_all = lb_all - lb_all[0:1]

    x2d = x.reshape(n, d)
    for l in range(depth):
        w_l = jnp.pad(w_in[l], ((0, 0), (0, width_pad - in_width))).astype(BF16)
        h = in_proj(x2d, w_l)
        h3 = h.reshape(batch, seq, width_pad)
        y_hg = hgrn2(h3, lb_all[l], hg_norm_w[l]).reshape(n, hg_w)
        y_gm = gmlp(h, gm_ln_w[l], gm_ln_b[l], gm_spatial_w[l], gm_spatial_b[l], gm_norm_w[l],
                    off_gm // gm_w, off_gm // gm_w + 1)
        kc = compress(h3[:, :, off_kv:off_kv + kv_w], nsa_cmp_pe[l, 0], nsa_cmp_w1[l, 0], nsa_cmp_w2[l, 0], batch, seq)
        vc = compress(h3[:, :, off_kv + kv_w:off_kv + 2 * kv_w], nsa_cmp_pe[l, 1], nsa_cmp_w1[l, 1],
                      nsa_cmp_w2[l, 1], batch, seq)
        n_cmp = kc.shape[1]
        kc = kc.reshape(batch, n_cmp, NSA_KV_GROUPS, HEAD_DIM).transpose(0, 2, 1, 3)
        vc_t = vc.reshape(batch, n_cmp, NSA_KV_GROUPS, HEAD_DIM).transpose(0, 2, 3, 1)
        ksa, vs_t, kw, vw_t = nsa_kprep(h, cosf, sinf, batch, seq, (off_kv + 2 * kv_w) // LANES)
        y_nsa = nsa_attention(h, cos_t, sin_t, kc, vc_t, ksa, vs_t, kw, vw_t, nsa_norm_w[l], batch, seq,
                              off_q // nsa_w, off_gate // LANES)
        x1, x1a, x1b = out_proj_ln(x2d, y_hg, y_gm, y_nsa, w_out[l], ln1_w[l], ln1_b[l], alpha)
        x2d = moe_ffn_ln(x1, (x1a, x1b), router_w[l], router_b[l], exp_w_up, exp_b_up, exp_w_down, exp_b_down, l,
                         ln2_w[l], ln2_b[l], alpha)
    return x2d.reshape(batch, seq, d)
```

```python
import functools
import math

import numpy as np
import jax
import jax.numpy as jnp
from jax import lax
from jax.experimental import pallas as pl
from jax.experimental.pallas import tpu as pltpu
from jax.experimental.pallas import tpu_sc as plsc

F32 = jnp.float32
BF16 = jnp.bfloat16
HIGHEST = lax.Precision.HIGHEST

HEAD_DIM = 64
LANES = 128
VMEM_LIMIT = 48 * 1024 * 1024
EXPERT_VMEM_LIMIT = 56 * 1024 * 1024

HG_CHUNK = 64
GM_CHUNK = 128
GM_TILE_CHUNKS = 4
NSA_KV_GROUPS = 2
NSA_HPG = 4
CMP_BLOCK = 32
CMP_STRIDE = 16
CMP_HIDDEN = 128
SEL_BLOCK = 64
N_SEL = 16
WINDOW = 512
N_GATES = 3
IMP_FORCE = 1e9
FORCE_KEY = int(np.float32(IMP_FORCE).view(np.int32))
NEG_INF = -1e30
N_EXPERTS = 32
TOP_K = 4
SWIGLU_ALPHA = 1.702
SWIGLU_LIMIT = 7.0
EXPERT_BLOCK = 512
SC_ROW_WORDS = 256
SC_WINDOW = 128
ROPE_THETA = 10000.0
LOG2_E = 1.4426950408889634
LN_EPS = 1e-5
RMS_EPS = 1e-6
V_ROWS = HEAD_DIM + 16


def _params(*sem):
    return pltpu.CompilerParams(dimension_semantics=sem, vmem_limit_bytes=VMEM_LIMIT)


def _dot(a, b):
    return jnp.dot(a, b, preferred_element_type=F32)


def _dot_nt(a, b, precision=None):
    return lax.dot_general(a, b, (((1,), (1,)), ((), ())), precision=precision,
                           preferred_element_type=F32)


def _dot_tn(a, b):
    return lax.dot_general(a, b, (((0,), (0,)), ((), ())), preferred_element_type=F32)


def _sigmoid(x):
    return 1.0 / (1.0 + jnp.exp(-x))


def _gelu(x):
    return 0.5 * x * (1.0 + jnp.tanh(0.7978845608028654 * (x + 0.044715 * x * x * x)))


def _layer_norm(x, w, b):
    mu = jnp.mean(x, axis=-1, keepdims=True)
    xc = x - mu
    var = jnp.mean(xc * xc, axis=-1, keepdims=True)
    return xc * lax.rsqrt(var + LN_EPS) * w + b


def _pack_bf16_pairs(y):
    w = y.shape[1] // 2
    bits = pltpu.bitcast(y.astype(BF16).astype(F32), jnp.uint32)
    return lax.shift_right_logical(bits[:, :w], jnp.uint32(16)) | (bits[:, w:] & jnp.uint32(0xFFFF0000))


def _unpack_bf16_pairs(u):
    lo = pltpu.bitcast(lax.shift_left(u, jnp.uint32(16)), F32)
    hi = pltpu.bitcast(u & jnp.uint32(0xFFFF0000), F32)
    return lo, hi


def _store_word_tables(refs, packed):
    for j, ref in enumerate(refs):
        ref[...] = packed[:, j * SC_ROW_WORDS:(j + 1) * SC_ROW_WORDS]


def _head_mean_sq(o, bd_ones):
    sq = o * o
    hi = sq.astype(BF16)
    lo = (sq - hi.astype(F32)).astype(BF16)
    ones = bd_ones.astype(BF16)
    return (_dot(hi, ones) + _dot(lo, ones)) * (1.0 / HEAD_DIM)


def _rope_kernel(pos_ref, inv_ref, cos_ref, sin_ref, cost_ref, sint_ref):
    ang = inv_ref[...] * pos_ref[...]
    c = jnp.cos(ang)
    s = jnp.sin(ang)
    cos_t = jnp.concatenate([c, c], axis=0)
    sin_t = jnp.concatenate([-s, s], axis=0)
    cost_ref[...] = cos_t
    sint_ref[...] = sin_t
    cos_ref[...] = cos_t.T
    sin_ref[...] = sin_t.T


def rope_tables(positions):
    n = positions.size
    tile = min(n, 2048)
    posf = positions.reshape(1, n).astype(F32)
    inv = ROPE_THETA ** (-jnp.arange(0, HEAD_DIM, 2, dtype=F32) / HEAD_DIM)
    row = pl.BlockSpec((tile, HEAD_DIM), lambda i: (i, 0))
    rowt = pl.BlockSpec((HEAD_DIM, tile), lambda i: (0, i))
    return pl.pallas_call(
        _rope_kernel, grid=(n // tile,),
        in_specs=[pl.BlockSpec((1, tile), lambda i: (0, i)), pl.BlockSpec((HEAD_DIM // 2, 1), lambda i: (0, 0))],
        out_specs=[row, row, rowt, rowt],
        out_shape=[jax.ShapeDtypeStruct((n, HEAD_DIM), F32)] * 2 + [jax.ShapeDtypeStruct((HEAD_DIM, n), F32)] * 2,
        compiler_params=_params("parallel"), name="rope_tables",
    )(posf, inv.reshape(HEAD_DIM // 2, 1))


def _in_proj_kernel(x_ref, w_ref, h_ref):
    h_ref[...] = _dot(x_ref[...].astype(BF16), w_ref[...])


def in_proj(x2d, w_bf16):
    n, d = x2d.shape
    width = w_bf16.shape[1]
    tile = min(n, 512)
    return pl.pallas_call(
        _in_proj_kernel, grid=(n // tile,),
        in_specs=[pl.BlockSpec((tile, d), lambda i: (i, 0)), pl.BlockSpec((d, width), lambda i: (0, 0))],
        out_specs=pl.BlockSpec((tile, width), lambda i: (i, 0)),
        out_shape=jax.ShapeDtypeStruct((n, width), F32),
        compiler_params=_params("parallel"), name="in_proj")(x2d, w_bf16)


HG_LEVELS = (64, 32, 16, 8, 4, 2)
HG_BATCH = 8


def _hgrn_constants():
    c = HG_CHUNK
    t = np.arange(c)
    u = t[None, :]
    rows = [u <= t[:, None], u > t[:, None]]
    masks = [np.eye(c, dtype=bool)]
    for m in HG_LEVELS:
        ref = ((t // m) * m + m // 2 - 1)[:, None]
        second = (t % m >= m // 2)[:, None]
        rows.append(((u > ref) & (u <= t[:, None]) & second) | ((u > t[:, None]) & (u <= ref) & ~second))
        masks.append((t[:, None] // m == t[None, :] // m) & second & (t[None, :] % m < m // 2))
    pmat = np.concatenate(rows, axis=0).astype(np.float32)
    masks = np.stack([np.tile(mk, (1, 4)) for mk in masks]).astype(np.float32)
    return pmat, masks


def _hgrn_kernel(q_ref, f_ref, i_ref, g_ref, lb_ref, nw_ref, pmat_ref, masks_ref, bd_ref, hm_ref,
                 o_ref, state_ref):
    c = HG_CHUNK

    @pl.when(pl.program_id(1) == 0)
    def _():
        state_ref[...] = jnp.zeros_like(state_ref)

    lb = lb_ref[...]
    bd = bd_ref[...]
    hm = hm_ref[...]
    hm_tiles = [jnp.broadcast_to(hm[h:h + 1], (c, hm.shape[1])).astype(BF16) for h in range(4)]
    pmat = pmat_ref[...]
    a = jnp.log(lb)
    log1m = jnp.log(1.0 - lb)
    nb, _, w = q_ref.shape
    seqs = range(nb)

    fr = f_ref[...].reshape(nb * c, w)
    hq = q_ref[...].reshape(nb * c, w)
    v = i_ref[...].reshape(nb * c, w)
    qf = hq * _sigmoid(hq)
    log_sig = jnp.minimum(fr, 0.0) - jnp.log(1.0 + jnp.exp(-jnp.abs(fr)))
    cc = log1m + log_sig
    log_f = jnp.maximum(a, cc) + jnp.log(1.0 + jnp.exp(-jnp.abs(a - cc)))
    kk = (1.0 - lb) * _sigmoid(-fr)

    hi = log_f.astype(BF16)
    lo = (log_f - hi.astype(F32)).astype(BF16)

    def side_by_side(x):
        return jnp.concatenate([x[b * c:(b + 1) * c] for b in seqs], axis=1)

    sums = jnp.minimum(_dot(pmat, side_by_side(hi)) + _dot(pmat, side_by_side(lo)), 0.0)
    e_all = jnp.exp(sums)

    def e_rows(r, b):
        return e_all[r * c:(r + 1) * c, b * w:(b + 1) * w]

    def stacked(x):
        xb = x.astype(BF16)
        return jnp.concatenate([xb * hm_tiles[h] for h in range(4)], axis=0)

    q_s = [qf[b * c:(b + 1) * c] for b in seqs]
    k_s = [kk[b * c:(b + 1) * c] for b in seqs]
    v_s = [v[b * c:(b + 1) * c] for b in seqs]

    att = [masks_ref[0] * _dot_nt(q_s[b].astype(BF16), stacked(k_s[b])) for b in seqs]
    for li in range(len(HG_LEVELS)):
        for b in seqs:
            e_l = e_rows(2 + li, b)
            att[b] = att[b] + masks_ref[li + 1] * _dot_nt((q_s[b] * e_l).astype(BF16), stacked(k_s[b] * e_l))

    outs = []
    for b in seqs:
        e_b = e_rows(0, b)
        st = state_ref[b]
        o = _dot(att[b].astype(BF16), stacked(v_s[b])) + _dot_nt((q_s[b] * e_b).astype(BF16), st.astype(BF16))
        k_rest = (k_s[b] * e_rows(1, b)).astype(BF16)
        state_ref[b] = st * e_b[c - 1:c] + bd * _dot_tn(v_s[b].astype(BF16), k_rest)
        outs.append(o)

    o = jnp.concatenate(outs, axis=0)
    ms = _head_mean_sq(o, bd)
    y = o * lax.rsqrt(ms + RMS_EPS) * nw_ref[...] * _sigmoid(g_ref[...].reshape(nb * c, w))
    o_ref[...] = y.astype(o_ref.dtype).reshape(nb, c, w)


def hgrn2(h3, lb, norm_w):
    batch, seq, _ = h3.shape
    w = lb.shape[-1]
    c = HG_CHUNK
    nb = math.gcd(batch, HG_BATCH)
    pmat, masks = _hgrn_constants()
    lane_head = np.arange(w) // HEAD_DIM
    bd = (lane_head[:, None] == lane_head[None, :]).astype(np.float32)
    hm = (np.arange(4)[:, None] == lane_head[None, :]).astype(np.float32)

    def col(j):
        return pl.BlockSpec((nb, c, w), lambda b, i, j=j: (b, i, j))

    def const(shape):
        return pl.BlockSpec(shape, lambda b, i: (0,) * len(shape))

    return pl.pallas_call(
        _hgrn_kernel, grid=(batch // nb, seq // c),
        in_specs=[col(0), col(1), col(2), col(3), const((1, w)), const((1, w)),
                  const(pmat.shape), const(masks.shape), const(bd.shape), const(hm.shape)],
        out_specs=pl.BlockSpec((nb, c, w), lambda b, i: (b, i, 0)),
        out_shape=jax.ShapeDtypeStruct((batch, seq, w), BF16),
        scratch_shapes=[pltpu.VMEM((nb, w, w), F32)],
        compiler_params=_params("parallel", "arbitrary"), name="hgrn2",
    )(h3, h3, h3, h3, lb.reshape(1, w), norm_w.reshape(1, w), jnp.asarray(pmat, BF16), jnp.asarray(masks),
      jnp.asarray(bd), jnp.asarray(hm))


def _gmlp_kernel(u_ref, v_ref, lnw_ref, lnb_ref, ws_ref, bias_ref, nw_ref, bd_ref, hm_ref, o_ref):
    c = GM_CHUNK
    groups = ws_ref.shape[0]
    u = _gelu(u_ref[...])
    v = _layer_norm(_gelu(v_ref[...]), lnw_ref[...], lnb_ref[...])
    hm = hm_ref[...]
    bd = bd_ref[...]
    causal = lax.broadcasted_iota(jnp.int32, (c, c), 0) >= lax.broadcasted_iota(jnp.int32, (c, c), 1)
    w_cat = jnp.concatenate([jnp.where(causal, ws_ref[g], 0.0).astype(BF16) for g in range(groups)], axis=1)
    for j in range(u.shape[0] // c):
        rows = slice(j * c, (j + 1) * c)
        v_j = v[rows]
        v_bd = jnp.concatenate([v_j * hm[g:g + 1] for g in range(groups)], axis=0).astype(BF16)
        y = u[rows] * (bias_ref[...] + _dot(w_cat, v_bd))
        ms = _head_mean_sq(y, bd)
        o_ref[rows, :] = (y * lax.rsqrt(ms + RMS_EPS) * nw_ref[...]).astype(o_ref.dtype)


def gmlp(h, ln_w, ln_b, w_s, b_s, norm_w, u_col, v_col):
    n = h.shape[0]
    groups, c, _ = w_s.shape
    w = groups * HEAD_DIM
    lane_head = np.arange(w) // HEAD_DIM
    bd = (lane_head[:, None] == lane_head[None, :]).astype(np.float32)
    hm = (np.arange(groups)[:, None] == lane_head[None, :]).astype(np.float32)
    bias = jnp.repeat(b_s.T, HEAD_DIM, axis=1)

    def const(shape):
        return pl.BlockSpec(shape, lambda i: (0,) * len(shape))

    t = math.gcd(n, GM_TILE_CHUNKS * c)
    return pl.pallas_call(
        _gmlp_kernel, grid=(n // t,),
        in_specs=[pl.BlockSpec((t, w), lambda i: (i, u_col)), pl.BlockSpec((t, w), lambda i: (i, v_col)),
                  const((1, w)), const((1, w)), const(w_s.shape), const((c, w)), const((1, w)),
                  const(bd.shape), const(hm.shape)],
        out_specs=pl.BlockSpec((t, w), lambda i: (i, 0)),
        out_shape=jax.ShapeDtypeStruct((n, w), BF16),
        compiler_params=_params("parallel"), name="gmlp",
    )(h, h, ln_w.reshape(1, w), ln_b.reshape(1, w), w_s, bias, norm_w.reshape(1, w),
      jnp.asarray(bd), jnp.asarray(hm))


def _compress_kernel(u_ref, wtop_ref, wbot_ref, pe_ref, w2_ref, o_ref):
    u = u_ref[...].astype(BF16)
    wtop = wtop_ref[...]
    wbot = wbot_ref[...]
    pe = pe_ref[...].astype(BF16)
    const = _dot(pe[0:1], wtop) + _dot(pe[1:2], wbot)
    p = _dot(u, wtop)
    q = _dot(u, wbot)
    q_next = jnp.concatenate([q[1:], jnp.zeros_like(q[0:1])], axis=0)
    hid = _gelu(p + q_next + const)
    o_ref[...] = _dot(hid.astype(BF16), w2_ref[...]).astype(o_ref.dtype)


def compress(kv, pe, w1, w2, batch, seq):
    g = NSA_KV_GROUPS
    half = CMP_STRIDE
    units = seq // half
    gw = g * HEAD_DIM
    u = kv.reshape(batch * units, half * gw)
    eye = jnp.eye(g, dtype=F32)
    w1r = w1.reshape(2, half, HEAD_DIM, CMP_HIDDEN)
    wbd = jnp.einsum('hjdn,gk->hjgdkn', w1r, eye).reshape(2, half * gw, g * CMP_HIDDEN).astype(BF16)
    w2bd = jnp.einsum('nd,gk->gnkd', w2, eye).reshape(g * CMP_HIDDEN, gw).astype(BF16)
    pe2 = jnp.broadcast_to(pe.reshape(2, half, 1, HEAD_DIM), (2, half, g, HEAD_DIM)).reshape(2, half * gw)

    def const(shape):
        return pl.BlockSpec(shape, lambda b: (0,) * len(shape))

    out = pl.pallas_call(
        _compress_kernel, grid=(batch,),
        in_specs=[pl.BlockSpec((units, half * gw), lambda b: (b, 0)),
                  const(wbd.shape[1:]), const(wbd.shape[1:]), const(pe2.shape), const(w2bd.shape)],
        out_specs=pl.BlockSpec((units, gw), lambda b: (b, 0)),
        out_shape=jax.ShapeDtypeStruct((batch * units, gw), BF16),
        compiler_params=_params("parallel"), name="nsa_compress",
    )(u, wbd[0], wbd[1], pe2, w2bd)
    return out.reshape(batch, units, gw)


def _rot_half_pairs(x):
    lane = lax.broadcasted_iota(jnp.int32, x.shape, 1)
    fwd = pltpu.roll(x, 32, axis=1)
    bwd = pltpu.roll(x, 96, axis=1)
    return jnp.where((lane % HEAD_DIM) < HEAD_DIM // 2, bwd, fwd)


def _kprep_kernel(ks_ref, vs_ref, kw_ref, vw_ref, cos_ref, sin_ref, ksa_ref, vso_ref, kwo_ref, vwo_ref):
    t = ks_ref.shape[0]
    cos = cos_ref[...]
    sin = sin_ref[...]
    cos2 = jnp.concatenate([cos, cos], axis=1)
    sin2 = jnp.concatenate([sin, sin], axis=1)
    ks = ks_ref[...]
    kw = kw_ref[...]
    ks_r = ks * cos2 + _rot_half_pairs(ks) * sin2
    kw_r = kw * cos2 + _rot_half_pairs(kw) * sin2
    pos = pl.program_id(1) * t + lax.broadcasted_iota(jnp.int32, (t, HEAD_DIM), 0)
    onehot = (pos // SEL_BLOCK == lax.broadcasted_iota(jnp.int32, (t, HEAD_DIM), 1)).astype(F32)
    vs_t = vs_ref[...].T
    vw_t = vw_ref[...].T
    tail = (lax.broadcasted_iota(jnp.int32, (V_ROWS - HEAD_DIM, t), 0) == 0).astype(F32)
    for g in range(NSA_KV_GROUPS):
        sl = slice(g * HEAD_DIM, (g + 1) * HEAD_DIM)
        ksa_ref[g] = jnp.concatenate([ks_r[:, sl], onehot], axis=1).astype(BF16)
        vso_ref[g] = jnp.concatenate([vs_t[sl], tail], axis=0).astype(BF16)
        kwo_ref[g] = kw_r[:, sl].astype(BF16)
        vwo_ref[g] = jnp.concatenate([vw_t[sl], tail], axis=0).astype(BF16)


def nsa_kprep(h, cosf, sinf, batch, seq, col0):
    g = NSA_KV_GROUPS
    t = min(seq, 512)
    nt = seq // t

    def col(j):
        return pl.BlockSpec((t, LANES), lambda b, i, j=j: (b * nt + i, col0 + j))

    tab = pl.BlockSpec((t, HEAD_DIM), lambda b, i: (b * nt + i, 0))

    def out(wd):
        return pl.BlockSpec((None, g, t, wd), lambda b, i: (b, 0, i, 0))

    out_t = pl.BlockSpec((None, g, V_ROWS, t), lambda b, i: (b, 0, 0, i))
    k_shape = jax.ShapeDtypeStruct((batch, g, seq, HEAD_DIM), BF16)
    v_shape = jax.ShapeDtypeStruct((batch, g, V_ROWS, seq), BF16)
    return pl.pallas_call(
        _kprep_kernel, grid=(batch, nt),
        in_specs=[col(0), col(1), col(2), col(3), tab, tab],
        out_specs=[out(2 * HEAD_DIM), out_t, out(HEAD_DIM), out_t],
        out_shape=[jax.ShapeDtypeStruct((batch, g, seq, 2 * HEAD_DIM), BF16), v_shape, k_shape, v_shape],
        compiler_params=_params("parallel", "parallel"), name="nsa_kprep",
    )(h, h, h, h, cosf, sinf)


def _nsa_kernel(hq_ref, gate_ref, cos_ref, sin_ref, kc_ref, vc_ref, ksa_ref, vs_ref, kw_ref, vw_ref,
                ovl_ref, nw_ref, o_ref, *, tq, tk, n_sb):
    qi = pl.program_id(1)
    hpg = NSA_HPG
    groups = NSA_KV_GROUPS
    rows = hpg * tq
    t0 = qi * tq
    scale = 1.0 / math.sqrt(HEAD_DIM)
    half = HEAD_DIM // 2

    hq_t = hq_ref[...].T
    cos = cos_ref[...]
    sin = sin_ref[...]
    q_raw, q_rot = [], []
    for g in range(groups):
        raw_g, rot_g = [], []
        for h in range(hpg):
            r0 = (g * hpg + h) * HEAD_DIM
            qh = hq_t[r0:r0 + HEAD_DIM]
            swapped = jnp.concatenate([qh[half:], qh[:half]], axis=0)
            raw_g.append(qh * (scale * LOG2_E))
            rot_g.append((qh * cos + swapped * sin) * (scale * LOG2_E))
        q_raw.append(jnp.concatenate(raw_g, axis=1).astype(BF16))
        q_rot.append(jnp.concatenate(rot_g, axis=1))

    tpos = t0 + lax.broadcasted_iota(jnp.int32, (1, tq), 1)
    tpos_r = jnp.concatenate([tpos] * hpg, axis=1)

    def flash_steps(s, v_t, carry):
        m_new = [jnp.maximum(carry[g][0], jnp.max(s[g], axis=0, keepdims=True)) for g in range(groups)]
        p = [jnp.exp2(s[g] - m_new[g]).astype(BF16) for g in range(groups)]
        return tuple((m_new[g], jnp.exp2(carry[g][0] - m_new[g]) * carry[g][1] + _dot(v_t[g], p[g]))
                     for g in range(groups))

    init = ((jnp.full((1, rows), NEG_INF, F32), jnp.zeros((V_ROWS, rows), F32)),) * groups

    wk = WINDOW + tq
    kw0 = pl.multiple_of(jnp.maximum(t0 - WINDOW, 0), tq)
    kpos_w = kw0 + lax.broadcasted_iota(jnp.int32, (wk, 1), 0)
    mask_w = (kpos_w <= tpos_r) & (kpos_w > tpos_r - WINDOW)
    n_pad = jnp.maximum(WINDOW - 1 - tpos_r, 0).astype(F32)
    n_cmp = kc_ref.shape[1]
    cmp_end = lax.broadcasted_iota(jnp.int32, (n_cmp, 1), 0) * CMP_STRIDE + (CMP_BLOCK - 1)
    mask_c = cmp_end <= tpos_r
    blk = lax.broadcasted_iota(jnp.int32, (n_sb, tq), 0)
    cur = (t0 + lax.broadcasted_iota(jnp.int32, (n_sb, tq), 1)) // SEL_BLOCK
    s_w = [jnp.where(mask_w, _dot(kw_ref[g, pl.ds(kw0, wk), :], q_rot[g].astype(BF16)), NEG_INF)
           for g in range(groups)]
    s_c = [jnp.where(mask_c, _dot(kc_ref[g], q_raw[g]), NEG_INF) for g in range(groups)]
    win = flash_steps(s_w, [vw_ref[g, :, pl.ds(kw0, wk)] for g in range(groups)], init)
    o_w, o_c, imps = [], [], []
    for g in range(groups):
        m_w, acc_w = win[g]
        m_f = jnp.where(n_pad > 0.0, jnp.maximum(m_w, 0.0), m_w)
        a_w = jnp.exp2(m_w - m_f)
        o_w.append(acc_w[0:HEAD_DIM] * (a_w / (acc_w[HEAD_DIM:HEAD_DIM + 1] * a_w + n_pad * jnp.exp2(-m_f))))

    for g in range(groups):
        e_c = jnp.exp2(s_c[g] - jnp.max(s_c[g], axis=0, keepdims=True))
        p_c = jnp.where(mask_c, e_c * (1.0 / jnp.sum(e_c, axis=0, keepdims=True)), 0.0)
        o_c.append(_dot(vc_ref[g], p_c.astype(BF16)))

        p_sum = p_c[:, 0:tq]
        for h in range(1, hpg):
            p_sum = p_sum + p_c[:, h * tq:(h + 1) * tq]
        imp = jnp.dot(ovl_ref[...], p_sum, precision=HIGHEST, preferred_element_type=F32)
        key = pltpu.bitcast(jnp.maximum(imp, 0.0), jnp.int32)
        key = jnp.where((blk == 0) | (blk == cur) | (blk == cur - 1), FORCE_KEY, key)
        imps.append(jnp.where(blk > cur, -1, key))

    n_sel = min(N_SEL, n_sb)

    def bit_body(it, taus):
        bit = lax.shift_left(jnp.int32(1), 30 - it)
        out = []
        for g in range(groups):
            cand = taus[g] | bit
            cnt = jnp.sum(jnp.where(imps[g] >= cand, 1, 0), axis=0, keepdims=True)
            out.append(jnp.where(cnt >= n_sel, cand, taus[g]))
        return tuple(out)

    taus = lax.fori_loop(0, 31, bit_body, (jnp.zeros((1, tq), jnp.int32),) * groups)
    lower = (lax.broadcasted_iota(jnp.int32, (n_sb, n_sb), 0)
             > lax.broadcasted_iota(jnp.int32, (n_sb, n_sb), 1)).astype(BF16)
    q_aug = []
    for g in range(groups):
        above = imps[g] > taus[g]
        equal = imps[g] == taus[g]
        need = n_sel - jnp.sum(jnp.where(above, 1, 0), axis=0, keepdims=True)
        earlier = _dot(lower, jnp.where(equal, 1.0, 0.0).astype(BF16))
        selected = above | (equal & (earlier < need.astype(F32)))
        sel_bias = jnp.where(selected, 0.0, NEG_INF)
        if n_sb < HEAD_DIM:
            sel_bias = jnp.concatenate([sel_bias, jnp.zeros((HEAD_DIM - n_sb, tq), F32)], axis=0)
        q_aug.append(jnp.concatenate([q_rot[g], jnp.concatenate([sel_bias] * hpg, axis=1)], axis=0).astype(BF16))

    def scores(k0, width):
        return [_dot(ksa_ref[g, pl.ds(k0, width), :], q_aug[g]) for g in range(groups)]

    def values(k0, width):
        return [vs_ref[g, :, pl.ds(k0, width)] for g in range(groups)]

    def wide_body(kt, carry):
        k0 = pl.multiple_of(kt * tk, tk)
        return flash_steps(scores(k0, tk), values(k0, tk), carry)

    n_wide = t0 // tk
    carry = lax.fori_loop(0, n_wide, wide_body, init)

    def narrow_body(j, carry):
        k0 = pl.multiple_of(n_wide * tk + j * tq, tq)
        return flash_steps(scores(k0, tq), values(k0, tq), carry)

    carry = lax.fori_loop(0, (t0 - n_wide * tk) // tq, narrow_body, carry)
    k0 = pl.multiple_of(t0, tq)
    mask_s = k0 + lax.broadcasted_iota(jnp.int32, (tq, 1), 0) <= tpos_r
    carry = flash_steps([jnp.where(mask_s, s, NEG_INF) for s in scores(k0, tq)], values(k0, tq), carry)

    gates = _sigmoid(gate_ref[...].T)
    nw = nw_ref[...]
    outs = []
    for g in range(groups):
        acc_s = carry[g][1]
        o_s = acc_s[0:HEAD_DIM] * (1.0 / acc_s[HEAD_DIM:HEAD_DIM + 1])
        for h in range(hpg):
            sl = slice(h * tq, (h + 1) * tq)
            r = (g * hpg + h) * N_GATES
            o = gates[r:r + 1] * o_c[g][:, sl] + gates[r + 1:r + 2] * o_s[:, sl] + gates[r + 2:r + 3] * o_w[g][:, sl]
            ms = jnp.mean(o * o, axis=0, keepdims=True)
            outs.append(o * lax.rsqrt(ms + RMS_EPS) * nw[:, g * hpg + h:g * hpg + h + 1])
    o_ref[...] = jnp.concatenate(outs, axis=0).T.astype(o_ref.dtype)


def nsa_attention(h, cos_t, sin_t, kc, vc_t, ksa, vs_t, kw, vw_t, norm_w, batch, seq, q_col0, gate_col):
    g, hpg = NSA_KV_GROUPS, NSA_HPG
    tq = min(seq, 256)
    nq = seq // tq
    n_sb = seq // SEL_BLOCK
    assert n_sb <= HEAD_DIM, "selection-block one-hot shares the 64 spare key lanes"
    n_cmp = kc.shape[2]
    units = np.arange(n_cmp)[:, None] + np.arange(CMP_BLOCK // CMP_STRIDE)[None, :]
    ovl = np.zeros((n_cmp, n_sb), np.float32)
    for c in range((seq - CMP_BLOCK) // CMP_STRIDE + 1):
        for u in units[c]:
            ovl[c, u // (SEL_BLOCK // CMP_STRIDE)] += 1.0
    ovl_t = jnp.asarray(ovl.T)

    def per_b(shape):
        return pl.BlockSpec((None, g) + shape, lambda b, qi: (b, 0, 0, 0))

    width = g * hpg * HEAD_DIM
    tab = pl.BlockSpec((HEAD_DIM, tq), lambda b, qi: (0, b * nq + qi))
    tk = min(seq, 512)
    assert seq >= WINDOW + tq and seq % tk == 0 and tk % tq == 0
    kern = functools.partial(_nsa_kernel, tq=tq, tk=tk, n_sb=n_sb)
    return pl.pallas_call(
        kern, grid=(batch, nq),
        in_specs=[pl.BlockSpec((tq, width), lambda b, qi: (b * nq + qi, q_col0)),
                  pl.BlockSpec((tq, LANES), lambda b, qi: (b * nq + qi, gate_col)),
                  tab, tab,
                  per_b((n_cmp, HEAD_DIM)), per_b((HEAD_DIM, n_cmp)),
                  per_b((seq, 2 * HEAD_DIM)), per_b((V_ROWS, seq)),
                  per_b((seq, HEAD_DIM)), per_b((V_ROWS, seq)),
                  pl.BlockSpec((n_sb, n_cmp), lambda b, qi: (0, 0)),
                  pl.BlockSpec((HEAD_DIM, g * hpg), lambda b, qi: (0, 0))],
        out_specs=pl.BlockSpec((tq, width), lambda b, qi: (b * nq + qi, 0)),
        out_shape=jax.ShapeDtypeStruct((batch * seq, width), BF16),
        compiler_params=_params("parallel", "arbitrary"), name="nsa_attention",
    )(h, h, cos_t, sin_t, kc, vc_t, ksa, vs_t, kw, vw_t, ovl_t, norm_w.reshape(g * hpg, HEAD_DIM).T)


def _out_proj_kernel(x_ref, yhg_ref, ygm_ref, ynsa_ref, whg_ref, wgm_ref, wnsa_ref, lnw_ref, lnb_ref,
                     o_ref, oa_ref, ob_ref, *, alpha):
    mix = (_dot(yhg_ref[...], whg_ref[...]) + _dot(ygm_ref[...], wgm_ref[...])
           + _dot(ynsa_ref[...], wnsa_ref[...]))
    y = _layer_norm(alpha * x_ref[...] + mix, lnw_ref[...], lnb_ref[...])
    o_ref[...] = y
    _store_word_tables((oa_ref, ob_ref), _pack_bf16_pairs(y))


def out_proj_ln(x2d, y_hg, y_gm, y_nsa, w_out, ln_w, ln_b, alpha):
    n, d = x2d.shape
    w1, w2 = y_hg.shape[1], y_hg.shape[1] + y_gm.shape[1]
    whg = w_out[:w1].astype(BF16)
    wgm = w_out[w1:w2].astype(BF16)
    wnsa = w_out[w2:].astype(BF16)
    t = min(n, 512)

    def row(wd):
        return pl.BlockSpec((t, wd), lambda i: (i, 0))

    def const(shape):
        return pl.BlockSpec(shape, lambda i: (0,) * len(shape))

    kern = functools.partial(_out_proj_kernel, alpha=alpha)
    return pl.pallas_call(
        kern, grid=(n // t,),
        in_specs=[row(d), row(y_hg.shape[1]), row(y_gm.shape[1]), row(y_nsa.shape[1]),
                  const(whg.shape), const(wgm.shape), const(wnsa.shape), const((1, d)), const((1, d))],
        out_specs=[row(d), row(SC_ROW_WORDS), row(SC_ROW_WORDS)],
        out_shape=[jax.ShapeDtypeStruct((n, d), F32)] + [jax.ShapeDtypeStruct((n, SC_ROW_WORDS), jnp.uint32)] * 2,
        compiler_params=_params("parallel"), name="out_proj_ln",
    )(x2d, y_hg, y_gm, y_nsa, whg, wgm, wnsa, ln_w.reshape(1, d), ln_b.reshape(1, d))


def _router_kernel(x_ref, w_ref, b_ref, e_ref, p_ref, r_ref, cnt_ref, carry_ref):
    t = x_ref.shape[0]

    @pl.when(pl.program_id(0) == 0)
    def _():
        carry_ref[...] = jnp.zeros_like(carry_ref)

    x = x_ref[...]
    x_hi = x.astype(BF16)
    x_lo = (x - x_hi.astype(F32)).astype(BF16)
    w = w_ref[...]
    w_hi = w.astype(BF16)
    w_lo = (w - w_hi.astype(F32)).astype(BF16)
    logits = _dot_nt(w_hi, x_hi) + (_dot_nt(w_hi, x_lo) + _dot_nt(w_lo, x_hi)) + b_ref[...]
    n_e = logits.shape[0]
    sub = lax.broadcasted_iota(jnp.int32, logits.shape, 0)
    work = logits
    vals, idxs = [], []
    sel = jnp.zeros(logits.shape, F32)
    for _ in range(TOP_K):
        m = jnp.max(work, axis=0, keepdims=True)
        idx = jnp.min(jnp.where(work == m, sub, n_e), axis=0, keepdims=True)
        hit = sub == idx
        sel = jnp.where(hit, 1.0, sel)
        work = jnp.where(hit, -jnp.inf, work)
        vals.append(m)
        idxs.append(idx)
    exps = [jnp.exp(v - vals[0]) for v in vals]
    inv_den = 1.0 / (exps[0] + exps[1] + exps[2] + exps[3])
    earlier = (lax.broadcasted_iota(jnp.int32, (t, t), 0) < lax.broadcasted_iota(jnp.int32, (t, t), 1))
    before = _dot(sel.astype(BF16), earlier.astype(BF16)) + carry_ref[...]
    ranks = [jnp.sum(jnp.where(sub == idx, before, 0.0), axis=0, keepdims=True) for idx in idxs]
    e_ref[...] = jnp.concatenate(idxs, axis=0)
    p_ref[...] = jnp.concatenate([e * inv_den for e in exps], axis=0)
    r_ref[...] = jnp.concatenate(ranks, axis=0).astype(jnp.int32)
    carry_ref[...] = carry_ref[...] + jnp.sum(sel, axis=1, keepdims=True)
    cnt_ref[...] = carry_ref[...].astype(jnp.int32)


def moe_router(x2d, router_w, router_b):
    n, d = x2d.shape
    e = router_w.shape[1]
    t = min(n, 512)
    row4 = pl.BlockSpec((TOP_K, t), lambda i: (0, i))
    top_e, top_p, rank, counts = pl.pallas_call(
        _router_kernel, grid=(n // t,),
        in_specs=[pl.BlockSpec((t, d), lambda i: (i, 0)), pl.BlockSpec((e, d), lambda i: (0, 0)),
                  pl.BlockSpec((e, 1), lambda i: (0, 0))],
        out_specs=[row4, row4, row4, pl.BlockSpec((e, 1), lambda i: (0, 0))],
        out_shape=[jax.ShapeDtypeStruct((TOP_K, n), jnp.int32), jax.ShapeDtypeStruct((TOP_K, n), F32),
                   jax.ShapeDtypeStruct((TOP_K, n), jnp.int32), jax.ShapeDtypeStruct((e, 1), jnp.int32)],
        scratch_shapes=[pltpu.VMEM((e, 1), F32)],
        compiler_params=_params("arbitrary"), name="moe_router",
    )(x2d, router_w.T, router_b.reshape(e, 1))
    return top_e, top_p, rank, counts[:, 0]


def _expert_kernel(be_ref, valid_ref, xa_ref, xb_ref, wu_ref, bu_ref, wd_ref, bd_ref, oa_ref, ob_ref,
                   wu_bf, wd_bf):
    i = pl.program_id(0)
    f = wd_ref.shape[0]
    n_used = be_ref[pl.num_programs(0)]

    @pl.when((i == 0) | (be_ref[i] != be_ref[jnp.maximum(i - 1, 0)]))
    def _():
        wu_bf[...] = wu_ref[...].astype(BF16)
        wd_bf[...] = wd_ref[...].astype(BF16)

    @pl.when(i < n_used)
    def _():
        packed = jnp.concatenate([xa_ref[...], xb_ref[...]], axis=1)
        live = lax.broadcasted_iota(jnp.int32, packed.shape, 0) < valid_ref[i]
        x_lo, x_hi = _unpack_bf16_pairs(jnp.where(live, packed, jnp.uint32(0)))
        x = jnp.concatenate([x_lo.astype(BF16), x_hi.astype(BF16)], axis=1)
        hcat = _dot(x, wu_bf[...]) + bu_ref[...]
        glu = jnp.minimum(hcat[:, :f], SWIGLU_LIMIT)
        lin = jnp.clip(hcat[:, f:], -SWIGLU_LIMIT, SWIGLU_LIMIT)
        act = glu * _sigmoid(SWIGLU_ALPHA * glu) * (lin + 1.0)
        _store_word_tables((oa_ref, ob_ref), _pack_bf16_pairs(_dot(act.astype(BF16), wd_bf[...]) + bd_ref[...]))

    @pl.when(i >= n_used)
    def _():
        oa_ref[...] = jnp.zeros_like(oa_ref)
        ob_ref[...] = jnp.zeros_like(ob_ref)


def moe_experts(xa, xb, block_e, n_used, block_valid, w_up, b_up, w_down, b_down, layer):
    rows = xa.shape[0]
    _, e, d, f2 = w_up.shape
    f = f2 // 2
    nb = rows // EXPERT_BLOCK
    words = pl.BlockSpec((EXPERT_BLOCK, SC_ROW_WORDS), lambda i, be, nv: (i, 0))
    grid_spec = pltpu.PrefetchScalarGridSpec(
        num_scalar_prefetch=2, grid=(nb,),
        in_specs=[words, words,
                  pl.BlockSpec((None, None, d, f2), lambda i, be, nv: (layer, be[i], 0, 0)),
                  pl.BlockSpec((None, None, 1, f2), lambda i, be, nv: (layer, be[i], 0, 0)),
                  pl.BlockSpec((None, None, f, d), lambda i, be, nv: (layer, be[i], 0, 0)),
                  pl.BlockSpec((None, None, 1, d), lambda i, be, nv: (layer, be[i], 0, 0))],
        out_specs=[words, words],
        scratch_shapes=[pltpu.VMEM((d, f2), BF16), pltpu.VMEM((f, d), BF16)])
    depth = w_up.shape[0]
    return pl.pallas_call(
        _expert_kernel, grid_spec=grid_spec,
        out_shape=[jax.ShapeDtypeStruct((rows, SC_ROW_WORDS), jnp.uint32)] * 2,
        compiler_params=pltpu.CompilerParams(dimension_semantics=("arbitrary",), vmem_limit_bytes=EXPERT_VMEM_LIMIT),
        name="moe_experts",
    )(jnp.concatenate([block_e, n_used.reshape(1)]), block_valid, xa, xb, w_up, b_up.reshape(depth, e, 1, f2),
      w_down, b_down.reshape(depth, e, 1, d))


def _combine_kernel(x_ref, ya_ref, yb_ref, p_ref, lnw_ref, lnb_ref, o_ref, *, alpha):
    p = p_ref[...]
    moe = jnp.zeros(x_ref.shape, F32)
    for k in range(TOP_K):
        y_lo, y_hi = _unpack_bf16_pairs(jnp.concatenate([ya_ref[k], yb_ref[k]], axis=1))
        moe = moe + p[:, k:k + 1] * jnp.concatenate([y_lo, y_hi], axis=1)
    o_ref[...] = _layer_norm(alpha * x_ref[...] + moe, lnw_ref[...], lnb_ref[...])


def combine_ln(x2d, ya, yb, top_p, ln_w, ln_b, alpha):
    n, d = x2d.shape
    t = min(n, 512)
    kern = functools.partial(_combine_kernel, alpha=alpha)
    words = pl.BlockSpec((TOP_K, t, SC_ROW_WORDS), lambda i: (0, i, 0))
    return pl.pallas_call(
        kern, grid=(n // t,),
        in_specs=[pl.BlockSpec((t, d), lambda i: (i, 0)), words, words,
                  pl.BlockSpec((t, TOP_K), lambda i: (i, 0)),
                  pl.BlockSpec((1, d), lambda i: (0, 0)), pl.BlockSpec((1, d), lambda i: (0, 0))],
        out_specs=pl.BlockSpec((t, d), lambda i: (i, 0)),
        out_shape=jax.ShapeDtypeStruct((n, d), F32),
        compiler_params=_params("parallel"), name="moe_combine_ln",
    )(x2d, ya, yb, top_p, ln_w.reshape(1, d), ln_b.reshape(1, d))


def _sc_mesh():
    return plsc.VectorSubcoreMesh(core_axis_name="core", subcore_axis_name="subcore")


def sc_gather_rows(tables, idx):
    r = idx.shape[0]
    nt = len(tables)
    out = jax.ShapeDtypeStruct((r, SC_ROW_WORDS), tables[0].dtype)

    @pl.kernel(out_type=(out,) * nt, mesh=_sc_mesh(), name="sc_gather_rows")
    def gather(*refs):
        x_hbm, i_hbm, o_hbm = refs[:nt], refs[nt], refs[nt + 1:]
        for j in range(nt):
            def body(i_vmem, o_vmem, table=x_hbm[j]):
                pltpu.sync_copy(table.at[i_vmem.at[0]], o_vmem)

            pltpu.emit_pipeline(
                body, grid=(r // SC_WINDOW,),
                in_specs=[pl.BlockSpec((1, SC_WINDOW), lambda i: (0, i))],
                out_specs=[pl.BlockSpec((SC_WINDOW, SC_ROW_WORDS), lambda i: (i, 0))],
                core_axis_name=("core", "subcore"), dimension_semantics=(pltpu.PARALLEL,),
            )(i_hbm, o_hbm[j])

    return gather(*tables, idx.reshape(1, r))


def sc_scatter_rows(tables, dest_t, n_rows):
    n = tables[0].shape[0]
    nt = len(tables)
    copies = dest_t.shape[0]
    out = jax.ShapeDtypeStruct((n_rows, SC_ROW_WORDS), tables[0].dtype)

    @pl.kernel(out_type=(out,) * nt, mesh=_sc_mesh(), scratch_types=[], name="sc_scatter_rows")
    def scatter(*refs):
        x_hbm, i_hbm, o_hbm = refs[:nt], refs[nt], refs[nt + 1:]
        for j in range(nt):
            def body(x_vmem, i_vmem, out_j=o_hbm[j]):
                for k in range(copies):
                    pltpu.sync_copy(x_vmem, out_j.at[i_vmem.at[k]])

            pltpu.emit_pipeline(
                body, grid=(n // SC_WINDOW,),
                in_specs=[pl.BlockSpec((SC_WINDOW, SC_ROW_WORDS), lambda i: (i, 0)),
                          pl.BlockSpec((copies, SC_WINDOW), lambda i: (0, i))],
                out_specs=[],
                core_axis_name=("core", "subcore"), dimension_semantics=(pltpu.PARALLEL,),
            )(x_hbm[j], i_hbm)

    return scatter(*tables, dest_t)


def moe_ffn_ln(x_f32, x_packed, router_w, router_b, w_up, b_up, w_down, b_down, layer, ln_w, ln_b, alpha):
    n, d = x_f32.shape
    top_e, top_p, rank, counts = moe_router(x_f32, router_w, router_b)
    padded = (counts + EXPERT_BLOCK - 1) // EXPERT_BLOCK * EXPERT_BLOCK
    pad_end = jnp.cumsum(padded)
    pad_start = pad_end - padded
    n_assign = n * TOP_K
    n_blocks = -(-(n_assign + N_EXPERTS * (EXPERT_BLOCK - 1)) // EXPERT_BLOCK)
    experts = jnp.arange(N_EXPERTS, dtype=jnp.int32)
    dest_t = rank + jnp.sum(jnp.where(top_e[..., None] == experts, pad_start.astype(jnp.int32), 0), axis=-1)
    block_first = jnp.arange(n_blocks, dtype=jnp.int32) * EXPERT_BLOCK
    block_e = jnp.clip(jnp.sum((pad_end[None, :] <= block_first[:, None]).astype(jnp.int32), axis=1),
                       0, N_EXPERTS - 1)
    block_valid = jnp.clip(counts[block_e] - (block_first - pad_start[block_e]), 0, EXPERT_BLOCK)
    n_used = (pad_end[-1] // EXPERT_BLOCK).astype(jnp.int32)
    xa, xb = sc_scatter_rows(x_packed, dest_t, n_blocks * EXPERT_BLOCK)
    ya, yb = moe_experts(xa, xb, block_e, n_used, block_valid.astype(jnp.int32), w_up, b_up, w_down, b_down,
                         layer)
    ya, yb = sc_gather_rows((ya, yb), dest_t.reshape(-1))
    return combine_ln(x_f32, ya.reshape(TOP_K, n, SC_ROW_WORDS), yb.reshape(TOP_K, n, SC_ROW_WORDS), top_p.T,
                      ln_w, ln_b, alpha)


def kernel(x, positions, w_in, hg_lower_bounds, hg_norm_w, gm_ln_w, gm_ln_b, gm_spatial_w, gm_spatial_b, gm_norm_w, nsa_cmp_pe, nsa_cmp_w1, nsa_cmp_w2, nsa_norm_w, w_out, ln1_w, ln1_b, router_w, router_b, exp_w_up, exp_b_up, exp_w_down, exp_b_down, ln2_w, ln2_b):
    batch, seq, d = x.shape
    depth = w_in.shape[0]
    n = batch * seq
    alpha = (2 * depth) ** 0.25
    hg_w = hg_norm_w.shape[1]
    gm_w = gm_norm_w.shape[1]
    nsa_w = nsa_norm_w.shape[1]
    kv_w = NSA_KV_GROUPS * HEAD_DIM
    in_width = w_in.shape[2]
    off_gm = 4 * hg_w
    off_q = off_gm + 2 * gm_w
    off_kv = off_q + nsa_w
    off_gate = off_kv + 6 * kv_w
    width_pad = -(-in_width // LANES) * LANES

    cosf, sinf, cos_t, sin_t = rope_tables(positions)
    lb_all = jnp.cumsum(jax.nn.softmax(hg_lower_bounds.astype(F32), axis=0), axis=0)
    lb_all = lb_all - lb_all[0:1]

    x2d = x.reshape(n, d)
    for l in range(depth):
        w_l = jnp.pad(w_in[l], ((0, 0), (0, width_pad - in_width))).astype(BF16)
        h = in_proj(x2d, w_l)
        h3 = h.reshape(batch, seq, width_pad)
        y_hg = hgrn2(h3, lb_all[l], hg_norm_w[l]).reshape(n, hg_w)
        y_gm = gmlp(h, gm_ln_w[l], gm_ln_b[l], gm_spatial_w[l], gm_spatial_b[l], gm_norm_w[l],
                    off_gm // gm_w, off_gm // gm_w + 1)
        kc = compress(h3[:, :, off_kv:off_kv + kv_w], nsa_cmp_pe[l, 0], nsa_cmp_w1[l, 0], nsa_cmp_w2[l, 0], batch, seq)
        vc = compress(h3[:, :, off_kv + kv_w:off_kv + 2 * kv_w], nsa_cmp_pe[l, 1], nsa_cmp_w1[l, 1],
                      nsa_cmp_w2[l, 1], batch, seq)
        n_cmp = kc.shape[1]
        kc = kc.reshape(batch, n_cmp, NSA_KV_GROUPS, HEAD_DIM).transpose(0, 2, 1, 3)
        vc_t = vc.reshape(batch, n_cmp, NSA_KV_GROUPS, HEAD_DIM).transpose(0, 2, 3, 1)
        ksa, vs_t, kw, vw_t = nsa_kprep(h, cosf, sinf, batch, seq, (off_kv + 2 * kv_w) // LANES)
        y_nsa = nsa_attention(h, cos_t, sin_t, kc, vc_t, ksa, vs_t, kw, vw_t, nsa_norm_w[l], batch, seq,
                              off_q // nsa_w, off_gate // LANES)
        x1, x1a, x1b = out_proj_ln(x2d, y_hg, y_gm, y_nsa, w_out[l], ln1_w[l], ln1_b[l], alpha)
        x2d = moe_ffn_ln(x1, (x1a, x1b), router_w[l], router_b[l], exp_w_up, exp_b_up, exp_w_down, exp_b_down, l,
                         ln2_w[l], ln2_b[l], alpha)
    return x2d.reshape(batch, seq, d)
```

```python
import functools
import math

import numpy as np
import jax
import jax.numpy as jnp
from jax import lax
from jax.experimental import pallas as pl
from jax.experimental.pallas import tpu as pltpu
from jax.experimental.pallas import tpu_sc as plsc

F32 = jnp.float32
BF16 = jnp.bfloat16
HIGHEST = lax.Precision.HIGHEST

HEAD_DIM = 64
LANES = 128
VMEM_LIMIT = 48 * 1024 * 1024
EXPERT_VMEM_LIMIT = 56 * 1024 * 1024

HG_CHUNK = 64
GM_CHUNK = 128
GM_TILE_CHUNKS = 4
NSA_KV_GROUPS = 2
NSA_HPG = 4
CMP_BLOCK = 32
CMP_STRIDE = 16
CMP_HIDDEN = 128
SEL_BLOCK = 64
N_SEL = 16
WINDOW = 512
N_GATES = 3
IMP_FORCE = 1e9
FORCE_KEY = int(np.float32(IMP_FORCE).view(np.int32))
NEG_INF = -1e30
N_EXPERTS = 32
TOP_K = 4
SWIGLU_ALPHA = 1.702
SWIGLU_LIMIT = 7.0
EXPERT_BLOCK = 512
MOE_COMBINE_PARTS = 2
SC_ROW_WORDS = 256
SC_WINDOW = 128
ROPE_THETA = 10000.0
LOG2_E = 1.4426950408889634
LN_EPS = 1e-5
RMS_EPS = 1e-6
V_ROWS = HEAD_DIM + 16


def _params(*sem):
    return pltpu.CompilerParams(dimension_semantics=sem, vmem_limit_bytes=VMEM_LIMIT)


def _dot(a, b):
    return jnp.dot(a, b, preferred_element_type=F32)


def _dot_nt(a, b, precision=None):
    return lax.dot_general(a, b, (((1,), (1,)), ((), ())), precision=precision,
                           preferred_element_type=F32)


def _dot_tn(a, b):
    return lax.dot_general(a, b, (((0,), (0,)), ((), ())), preferred_element_type=F32)


def _sigmoid(x):
    return 1.0 / (1.0 + jnp.exp(-x))


def _gelu(x):
    return 0.5 * x * (1.0 + jnp.tanh(0.7978845608028654 * (x + 0.044715 * x * x * x)))


def _layer_norm(x, w, b):
    mu = jnp.mean(x, axis=-1, keepdims=True)
    xc = x - mu
    var = jnp.mean(xc * xc, axis=-1, keepdims=True)
    return xc * lax.rsqrt(var + LN_EPS) * w + b


def _pack_bf16_pairs(y):
    w = y.shape[1] // 2
    bits = pltpu.bitcast(y.astype(BF16).astype(F32), jnp.uint32)
    return lax.shift_right_logical(bits[:, :w], jnp.uint32(16)) | (bits[:, w:] & jnp.uint32(0xFFFF0000))


def _unpack_bf16_pairs(u):
    lo = pltpu.bitcast(lax.shift_left(u, jnp.uint32(16)), F32)
    hi = pltpu.bitcast(u & jnp.uint32(0xFFFF0000), F32)
    return lo, hi


def _store_word_tables(refs, packed):
    for j, ref in enumerate(refs):
        ref[...] = packed[:, j * SC_ROW_WORDS:(j + 1) * SC_ROW_WORDS]


def _head_mean_sq(o, bd_ones):
    sq = o * o
    hi = sq.astype(BF16)
    lo = (sq - hi.astype(F32)).astype(BF16)
    ones = bd_ones.astype(BF16)
    return (_dot(hi, ones) + _dot(lo, ones)) * (1.0 / HEAD_DIM)


def _rope_kernel(pos_ref, inv_ref, cos_ref, sin_ref, cost_ref, sint_ref):
    ang = inv_ref[...] * pos_ref[...]
    c = jnp.cos(ang)
    s = jnp.sin(ang)
    cos_t = jnp.concatenate([c, c], axis=0)
    sin_t = jnp.concatenate([-s, s], axis=0)
    cost_ref[...] = cos_t
    sint_ref[...] = sin_t
    cos_ref[...] = cos_t.T
    sin_ref[...] = sin_t.T


def rope_tables(positions):
    n = positions.size
    tile = min(n, 2048)
    posf = positions.reshape(1, n).astype(F32)
    inv = ROPE_THETA ** (-jnp.arange(0, HEAD_DIM, 2, dtype=F32) / HEAD_DIM)
    row = pl.BlockSpec((tile, HEAD_DIM), lambda i: (i, 0))
    rowt = pl.BlockSpec((HEAD_DIM, tile), lambda i: (0, i))
    return pl.pallas_call(
        _rope_kernel, grid=(n // tile,),
        in_specs=[pl.BlockSpec((1, tile), lambda i: (0, i)), pl.BlockSpec((HEAD_DIM // 2, 1), lambda i: (0, 0))],
        out_specs=[row, row, rowt, rowt],
        out_shape=[jax.ShapeDtypeStruct((n, HEAD_DIM), F32)] * 2 + [jax.ShapeDtypeStruct((HEAD_DIM, n), F32)] * 2,
        compiler_params=_params("parallel"), name="rope_tables",
    )(posf, inv.reshape(HEAD_DIM // 2, 1))


def _in_proj_kernel(x_ref, w_ref, h_ref):
    h_ref[...] = _dot(x_ref[...].astype(BF16), w_ref[...])


def in_proj(x2d, w_bf16):
    n, d = x2d.shape
    width = w_bf16.shape[1]
    tile = min(n, 512)
    return pl.pallas_call(
        _in_proj_kernel, grid=(n // tile,),
        in_specs=[pl.BlockSpec((tile, d), lambda i: (i, 0)), pl.BlockSpec((d, width), lambda i: (0, 0))],
        out_specs=pl.BlockSpec((tile, width), lambda i: (i, 0)),
        out_shape=jax.ShapeDtypeStruct((n, width), F32),
        compiler_params=_params("parallel"), name="in_proj")(x2d, w_bf16)


HG_LEVELS = (64, 32, 16, 8, 4, 2)
HG_BATCH = 8


def _hgrn_constants():
    c = HG_CHUNK
    t = np.arange(c)
    u = t[None, :]
    rows = [u <= t[:, None], u > t[:, None]]
    masks = [np.eye(c, dtype=bool)]
    for m in HG_LEVELS:
        ref = ((t // m) * m + m // 2 - 1)[:, None]
        second = (t % m >= m // 2)[:, None]
        rows.append(((u > ref) & (u <= t[:, None]) & second) | ((u > t[:, None]) & (u <= ref) & ~second))
        masks.append((t[:, None] // m == t[None, :] // m) & second & (t[None, :] % m < m // 2))
    pmat = np.concatenate(rows, axis=0).astype(np.float32)
    masks = np.stack([np.tile(mk, (1, 4)) for mk in masks]).astype(np.float32)
    return pmat, masks


def _hgrn_kernel(q_ref, f_ref, i_ref, g_ref, lb_ref, nw_ref, pmat_ref, masks_ref, bd_ref, hm_ref,
                 o_ref, state_ref):
    c = HG_CHUNK

    @pl.when(pl.program_id(1) == 0)
    def _():
        state_ref[...] = jnp.zeros_like(state_ref)

    lb = lb_ref[...]
    bd = bd_ref[...]
    hm = hm_ref[...]
    hm_tiles = [jnp.broadcast_to(hm[h:h + 1], (c, hm.shape[1])).astype(BF16) for h in range(4)]
    pmat = pmat_ref[...]
    a = jnp.log(lb)
    log1m = jnp.log(1.0 - lb)
    nb, _, w = q_ref.shape
    seqs = range(nb)

    fr = f_ref[...].reshape(nb * c, w)
    hq = q_ref[...].reshape(nb * c, w)
    v = i_ref[...].reshape(nb * c, w)
    qf = hq * _sigmoid(hq)
    log_sig = jnp.minimum(fr, 0.0) - jnp.log(1.0 + jnp.exp(-jnp.abs(fr)))
    cc = log1m + log_sig
    log_f = jnp.maximum(a, cc) + jnp.log(1.0 + jnp.exp(-jnp.abs(a - cc)))
    kk = (1.0 - lb) * _sigmoid(-fr)

    hi = log_f.astype(BF16)
    lo = (log_f - hi.astype(F32)).astype(BF16)

    def side_by_side(x):
        return jnp.concatenate([x[b * c:(b + 1) * c] for b in seqs], axis=1)

    sums = jnp.minimum(_dot(pmat, side_by_side(hi)) + _dot(pmat, side_by_side(lo)), 0.0)
    e_all = jnp.exp(sums)

    def e_rows(r, b):
        return e_all[r * c:(r + 1) * c, b * w:(b + 1) * w]

    def stacked(x):
        xb = x.astype(BF16)
        return jnp.concatenate([xb * hm_tiles[h] for h in range(4)], axis=0)

    q_s = [qf[b * c:(b + 1) * c] for b in seqs]
    k_s = [kk[b * c:(b + 1) * c] for b in seqs]
    v_s = [v[b * c:(b + 1) * c] for b in seqs]

    att = [masks_ref[0] * _dot_nt(q_s[b].astype(BF16), stacked(k_s[b])) for b in seqs]
    for li in range(len(HG_LEVELS)):
        for b in seqs:
            e_l = e_rows(2 + li, b)
            att[b] = att[b] + masks_ref[li + 1] * _dot_nt((q_s[b] * e_l).astype(BF16), stacked(k_s[b] * e_l))

    outs = []
    for b in seqs:
        e_b = e_rows(0, b)
        st = state_ref[b]
        o = _dot(att[b].astype(BF16), stacked(v_s[b])) + _dot_nt((q_s[b] * e_b).astype(BF16), st.astype(BF16))
        k_rest = (k_s[b] * e_rows(1, b)).astype(BF16)
        state_ref[b] = st * e_b[c - 1:c] + bd * _dot_tn(v_s[b].astype(BF16), k_rest)
        outs.append(o)

    o = jnp.concatenate(outs, axis=0)
    ms = _head_mean_sq(o, bd)
    y = o * lax.rsqrt(ms + RMS_EPS) * nw_ref[...] * _sigmoid(g_ref[...].reshape(nb * c, w))
    o_ref[...] = y.astype(o_ref.dtype).reshape(nb, c, w)


def hgrn2(h3, lb, norm_w):
    batch, seq, _ = h3.shape
    w = lb.shape[-1]
    c = HG_CHUNK
    nb = math.gcd(batch, HG_BATCH)
    pmat, masks = _hgrn_constants()
    lane_head = np.arange(w) // HEAD_DIM
    bd = (lane_head[:, None] == lane_head[None, :]).astype(np.float32)
    hm = (np.arange(4)[:, None] == lane_head[None, :]).astype(np.float32)

    def col(j):
        return pl.BlockSpec((nb, c, w), lambda b, i, j=j: (b, i, j))

    def const(shape):
        return pl.BlockSpec(shape, lambda b, i: (0,) * len(shape))

    return pl.pallas_call(
        _hgrn_kernel, grid=(batch // nb, seq // c),
        in_specs=[col(0), col(1), col(2), col(3), const((1, w)), const((1, w)),
                  const(pmat.shape), const(masks.shape), const(bd.shape), const(hm.shape)],
        out_specs=pl.BlockSpec((nb, c, w), lambda b, i: (b, i, 0)),
        out_shape=jax.ShapeDtypeStruct((batch, seq, w), BF16),
        scratch_shapes=[pltpu.VMEM((nb, w, w), F32)],
        compiler_params=_params("parallel", "arbitrary"), name="hgrn2",
    )(h3, h3, h3, h3, lb.reshape(1, w), norm_w.reshape(1, w), jnp.asarray(pmat, BF16), jnp.asarray(masks),
      jnp.asarray(bd), jnp.asarray(hm))


def _gmlp_kernel(u_ref, v_ref, lnw_ref, lnb_ref, ws_ref, bias_ref, nw_ref, bd_ref, hm_ref, o_ref):
    c = GM_CHUNK
    groups = ws_ref.shape[0]
    u = _gelu(u_ref[...])
    v = _layer_norm(_gelu(v_ref[...]), lnw_ref[...], lnb_ref[...])
    hm = hm_ref[...]
    bd = bd_ref[...]
    causal = lax.broadcasted_iota(jnp.int32, (c, c), 0) >= lax.broadcasted_iota(jnp.int32, (c, c), 1)
    w_cat = jnp.concatenate([jnp.where(causal, ws_ref[g], 0.0).astype(BF16) for g in range(groups)], axis=1)
    for j in range(u.shape[0] // c):
        rows = slice(j * c, (j + 1) * c)
        v_j = v[rows]
        v_bd = jnp.concatenate([v_j * hm[g:g + 1] for g in range(groups)], axis=0).astype(BF16)
        y = u[rows] * (bias_ref[...] + _dot(w_cat, v_bd))
        ms = _head_mean_sq(y, bd)
        o_ref[rows, :] = (y * lax.rsqrt(ms + RMS_EPS) * nw_ref[...]).astype(o_ref.dtype)


def gmlp(h, ln_w, ln_b, w_s, b_s, norm_w, u_col, v_col):
    n = h.shape[0]
    groups, c, _ = w_s.shape
    w = groups * HEAD_DIM
    lane_head = np.arange(w) // HEAD_DIM
    bd = (lane_head[:, None] == lane_head[None, :]).astype(np.float32)
    hm = (np.arange(groups)[:, None] == lane_head[None, :]).astype(np.float32)
    bias = jnp.repeat(b_s.T, HEAD_DIM, axis=1)

    def const(shape):
        return pl.BlockSpec(shape, lambda i: (0,) * len(shape))

    t = math.gcd(n, GM_TILE_CHUNKS * c)
    return pl.pallas_call(
        _gmlp_kernel, grid=(n // t,),
        in_specs=[pl.BlockSpec((t, w), lambda i: (i, u_col)), pl.BlockSpec((t, w), lambda i: (i, v_col)),
                  const((1, w)), const((1, w)), const(w_s.shape), const((c, w)), const((1, w)),
                  const(bd.shape), const(hm.shape)],
        out_specs=pl.BlockSpec((t, w), lambda i: (i, 0)),
        out_shape=jax.ShapeDtypeStruct((n, w), BF16),
        compiler_params=_params("parallel"), name="gmlp",
    )(h, h, ln_w.reshape(1, w), ln_b.reshape(1, w), w_s, bias, norm_w.reshape(1, w),
      jnp.asarray(bd), jnp.asarray(hm))


def _compress_kernel(u_ref, wtop_ref, wbot_ref, pe_ref, w2_ref, o_ref):
    u = u_ref[...].astype(BF16)
    wtop = wtop_ref[...]
    wbot = wbot_ref[...]
    pe = pe_ref[...].astype(BF16)
    const = _dot(pe[0:1], wtop) + _dot(pe[1:2], wbot)
    p = _dot(u, wtop)
    q = _dot(u, wbot)
    q_next = jnp.concatenate([q[1:], jnp.zeros_like(q[0:1])], axis=0)
    hid = _gelu(p + q_next + const)
    o_ref[...] = _dot(hid.astype(BF16), w2_ref[...]).astype(o_ref.dtype)


def compress(kv, pe, w1, w2, batch, seq):
    g = NSA_KV_GROUPS
    half = CMP_STRIDE
    units = seq // half
    gw = g * HEAD_DIM
    u = kv.reshape(batch * units, half * gw)
    eye = jnp.eye(g, dtype=F32)
    w1r = w1.reshape(2, half, HEAD_DIM, CMP_HIDDEN)
    wbd = jnp.einsum('hjdn,gk->hjgdkn', w1r, eye).reshape(2, half * gw, g * CMP_HIDDEN).astype(BF16)
    w2bd = jnp.einsum('nd,gk->gnkd', w2, eye).reshape(g * CMP_HIDDEN, gw).astype(BF16)
    pe2 = jnp.broadcast_to(pe.reshape(2, half, 1, HEAD_DIM), (2, half, g, HEAD_DIM)).reshape(2, half * gw)

    def const(shape):
        return pl.BlockSpec(shape, lambda b: (0,) * len(shape))

    out = pl.pallas_call(
        _compress_kernel, grid=(batch,),
        in_specs=[pl.BlockSpec((units, half * gw), lambda b: (b, 0)),
                  const(wbd.shape[1:]), const(wbd.shape[1:]), const(pe2.shape), const(w2bd.shape)],
        out_specs=pl.BlockSpec((units, gw), lambda b: (b, 0)),
        out_shape=jax.ShapeDtypeStruct((batch * units, gw), BF16),
        compiler_params=_params("parallel"), name="nsa_compress",
    )(u, wbd[0], wbd[1], pe2, w2bd)
    return out.reshape(batch, units, gw)


def _rot_half_pairs(x):
    lane = lax.broadcasted_iota(jnp.int32, x.shape, 1)
    fwd = pltpu.roll(x, 32, axis=1)
    bwd = pltpu.roll(x, 96, axis=1)
    return jnp.where((lane % HEAD_DIM) < HEAD_DIM // 2, bwd, fwd)


def _kprep_kernel(ks_ref, vs_ref, kw_ref, vw_ref, cos_ref, sin_ref, ksa_ref, vso_ref, kwo_ref, vwo_ref):
    t = ks_ref.shape[0]
    cos = cos_ref[...]
    sin = sin_ref[...]
    cos2 = jnp.concatenate([cos, cos], axis=1)
    sin2 = jnp.concatenate([sin, sin], axis=1)
    ks = ks_ref[...]
    kw = kw_ref[...]
    ks_r = ks * cos2 + _rot_half_pairs(ks) * sin2
    kw_r = kw * cos2 + _rot_half_pairs(kw) * sin2
    pos = pl.program_id(1) * t + lax.broadcasted_iota(jnp.int32, (t, HEAD_DIM), 0)
    onehot = (pos // SEL_BLOCK == lax.broadcasted_iota(jnp.int32, (t, HEAD_DIM), 1)).astype(F32)
    vs_t = vs_ref[...].T
    vw_t = vw_ref[...].T
    tail = (lax.broadcasted_iota(jnp.int32, (V_ROWS - HEAD_DIM, t), 0) == 0).astype(F32)
    for g in range(NSA_KV_GROUPS):
        sl = slice(g * HEAD_DIM, (g + 1) * HEAD_DIM)
        ksa_ref[g] = jnp.concatenate([ks_r[:, sl], onehot], axis=1).astype(BF16)
        vso_ref[g] = jnp.concatenate([vs_t[sl], tail], axis=0).astype(BF16)
        kwo_ref[g] = kw_r[:, sl].astype(BF16)
        vwo_ref[g] = jnp.concatenate([vw_t[sl], tail], axis=0).astype(BF16)


def nsa_kprep(h, cosf, sinf, batch, seq, col0):
    g = NSA_KV_GROUPS
    t = min(seq, 512)
    nt = seq // t

    def col(j):
        return pl.BlockSpec((t, LANES), lambda b, i, j=j: (b * nt + i, col0 + j))

    tab = pl.BlockSpec((t, HEAD_DIM), lambda b, i: (b * nt + i, 0))

    def out(wd):
        return pl.BlockSpec((None, g, t, wd), lambda b, i: (b, 0, i, 0))

    out_t = pl.BlockSpec((None, g, V_ROWS, t), lambda b, i: (b, 0, 0, i))
    k_shape = jax.ShapeDtypeStruct((batch, g, seq, HEAD_DIM), BF16)
    v_shape = jax.ShapeDtypeStruct((batch, g, V_ROWS, seq), BF16)
    return pl.pallas_call(
        _kprep_kernel, grid=(batch, nt),
        in_specs=[col(0), col(1), col(2), col(3), tab, tab],
        out_specs=[out(2 * HEAD_DIM), out_t, out(HEAD_DIM), out_t],
        out_shape=[jax.ShapeDtypeStruct((batch, g, seq, 2 * HEAD_DIM), BF16), v_shape, k_shape, v_shape],
        compiler_params=_params("parallel", "parallel"), name="nsa_kprep",
    )(h, h, h, h, cosf, sinf)


def _nsa_kernel(hq_ref, gate_ref, cos_ref, sin_ref, kc_ref, vc_ref, ksa_ref, vs_ref, kw_ref, vw_ref,
                ovl_ref, nw_ref, o_ref, *, tq, tk, n_sb):
    qi = pl.program_id(1)
    hpg = NSA_HPG
    groups = NSA_KV_GROUPS
    rows = hpg * tq
    t0 = qi * tq
    scale = 1.0 / math.sqrt(HEAD_DIM)
    half = HEAD_DIM // 2

    hq_t = hq_ref[...].T
    cos = cos_ref[...]
    sin = sin_ref[...]
    q_raw, q_rot = [], []
    for g in range(groups):
        raw_g, rot_g = [], []
        for h in range(hpg):
            r0 = (g * hpg + h) * HEAD_DIM
            qh = hq_t[r0:r0 + HEAD_DIM]
            swapped = jnp.concatenate([qh[half:], qh[:half]], axis=0)
            raw_g.append(qh * (scale * LOG2_E))
            rot_g.append((qh * cos + swapped * sin) * (scale * LOG2_E))
        q_raw.append(jnp.concatenate(raw_g, axis=1).astype(BF16))
        q_rot.append(jnp.concatenate(rot_g, axis=1))

    tpos = t0 + lax.broadcasted_iota(jnp.int32, (1, tq), 1)
    tpos_r = jnp.concatenate([tpos] * hpg, axis=1)

    def flash_steps(s, v_t, carry):
        m_new = [jnp.maximum(carry[g][0], jnp.max(s[g], axis=0, keepdims=True)) for g in range(groups)]
        p = [jnp.exp2(s[g] - m_new[g]).astype(BF16) for g in range(groups)]
        return tuple((m_new[g], jnp.exp2(carry[g][0] - m_new[g]) * carry[g][1] + _dot(v_t[g], p[g]))
                     for g in range(groups))

    init = ((jnp.full((1, rows), NEG_INF, F32), jnp.zeros((V_ROWS, rows), F32)),) * groups

    wk = WINDOW + tq
    kw0 = pl.multiple_of(jnp.maximum(t0 - WINDOW, 0), tq)
    kpos_w = kw0 + lax.broadcasted_iota(jnp.int32, (wk, 1), 0)
    mask_w = (kpos_w <= tpos_r) & (kpos_w > tpos_r - WINDOW)
    n_pad = jnp.maximum(WINDOW - 1 - tpos_r, 0).astype(F32)
    n_cmp = kc_ref.shape[1]
    cmp_end = lax.broadcasted_iota(jnp.int32, (n_cmp, 1), 0) * CMP_STRIDE + (CMP_BLOCK - 1)
    mask_c = cmp_end <= tpos_r
    blk = lax.broadcasted_iota(jnp.int32, (n_sb, tq), 0)
    cur = (t0 + lax.broadcasted_iota(jnp.int32, (n_sb, tq), 1)) // SEL_BLOCK
    s_w = [jnp.where(mask_w, _dot(kw_ref[g, pl.ds(kw0, wk), :], q_rot[g].astype(BF16)), NEG_INF)
           for g in range(groups)]
    s_c = [jnp.where(mask_c, _dot(kc_ref[g], q_raw[g]), NEG_INF) for g in range(groups)]
    win = flash_steps(s_w, [vw_ref[g, :, pl.ds(kw0, wk)] for g in range(groups)], init)
    o_w, o_c, imps = [], [], []
    for g in range(groups):
        m_w, acc_w = win[g]
        m_f = jnp.where(n_pad > 0.0, jnp.maximum(m_w, 0.0), m_w)
        a_w = jnp.exp2(m_w - m_f)
        o_w.append(acc_w[0:HEAD_DIM] * (a_w / (acc_w[HEAD_DIM:HEAD_DIM + 1] * a_w + n_pad * jnp.exp2(-m_f))))

    for g in range(groups):
        e_c = jnp.exp2(s_c[g] - jnp.max(s_c[g], axis=0, keepdims=True))
        p_c = jnp.where(mask_c, e_c * (1.0 / jnp.sum(e_c, axis=0, keepdims=True)), 0.0)
        o_c.append(_dot(vc_ref[g], p_c.astype(BF16)))

        p_sum = p_c[:, 0:tq]
        for h in range(1, hpg):
            p_sum = p_sum + p_c[:, h * tq:(h + 1) * tq]
        imp = jnp.dot(ovl_ref[...], p_sum, precision=HIGHEST, preferred_element_type=F32)
        key = pltpu.bitcast(jnp.maximum(imp, 0.0), jnp.int32)
        key = jnp.where((blk == 0) | (blk == cur) | (blk == cur - 1), FORCE_KEY, key)
        imps.append(jnp.where(blk > cur, -1, key))

    n_sel = min(N_SEL, n_sb)

    def bit_body(it, taus):
        bit = lax.shift_left(jnp.int32(1), 30 - it)
        out = []
        for g in range(groups):
            cand = taus[g] | bit
            cnt = jnp.sum(jnp.where(imps[g] >= cand, 1, 0), axis=0, keepdims=True)
            out.append(jnp.where(cnt >= n_sel, cand, taus[g]))
        return tuple(out)

    taus = lax.fori_loop(0, 31, bit_body, (jnp.zeros((1, tq), jnp.int32),) * groups)
    lower = (lax.broadcasted_iota(jnp.int32, (n_sb, n_sb), 0)
             > lax.broadcasted_iota(jnp.int32, (n_sb, n_sb), 1)).astype(BF16)
    q_aug = []
    for g in range(groups):
        above = imps[g] > taus[g]
        equal = imps[g] == taus[g]
        need = n_sel - jnp.sum(jnp.where(above, 1, 0), axis=0, keepdims=True)
        earlier = _dot(lower, jnp.where(equal, 1.0, 0.0).astype(BF16))
        selected = above | (equal & (earlier < need.astype(F32)))
        sel_bias = jnp.where(selected, 0.0, NEG_INF)
        if n_sb < HEAD_DIM:
            sel_bias = jnp.concatenate([sel_bias, jnp.zeros((HEAD_DIM - n_sb, tq), F32)], axis=0)
        q_aug.append(jnp.concatenate([q_rot[g], jnp.concatenate([sel_bias] * hpg, axis=1)], axis=0).astype(BF16))

    def scores(k0, width):
        return [_dot(ksa_ref[g, pl.ds(k0, width), :], q_aug[g]) for g in range(groups)]

    def values(k0, width):
        return [vs_ref[g, :, pl.ds(k0, width)] for g in range(groups)]

    def wide_body(kt, carry):
        k0 = pl.multiple_of(kt * tk, tk)
        return flash_steps(scores(k0, tk), values(k0, tk), carry)

    n_wide = t0 // tk
    carry = lax.fori_loop(0, n_wide, wide_body, init)

    def narrow_body(j, carry):
        k0 = pl.multiple_of(n_wide * tk + j * tq, tq)
        return flash_steps(scores(k0, tq), values(k0, tq), carry)

    carry = lax.fori_loop(0, (t0 - n_wide * tk) // tq, narrow_body, carry)
    k0 = pl.multiple_of(t0, tq)
    mask_s = k0 + lax.broadcasted_iota(jnp.int32, (tq, 1), 0) <= tpos_r
    carry = flash_steps([jnp.where(mask_s, s, NEG_INF) for s in scores(k0, tq)], values(k0, tq), carry)

    gates = _sigmoid(gate_ref[...].T)
    nw = nw_ref[...]
    outs = []
    for g in range(groups):
        acc_s = carry[g][1]
        o_s = acc_s[0:HEAD_DIM] * (1.0 / acc_s[HEAD_DIM:HEAD_DIM + 1])
        for h in range(hpg):
            sl = slice(h * tq, (h + 1) * tq)
            r = (g * hpg + h) * N_GATES
            o = gates[r:r + 1] * o_c[g][:, sl] + gates[r + 1:r + 2] * o_s[:, sl] + gates[r + 2:r + 3] * o_w[g][:, sl]
            ms = jnp.mean(o * o, axis=0, keepdims=True)
            outs.append(o * lax.rsqrt(ms + RMS_EPS) * nw[:, g * hpg + h:g * hpg + h + 1])
    o_ref[...] = jnp.concatenate(outs, axis=0).T.astype(o_ref.dtype)


def nsa_attention(h, cos_t, sin_t, kc, vc_t, ksa, vs_t, kw, vw_t, norm_w, batch, seq, q_col0, gate_col):
    g, hpg = NSA_KV_GROUPS, NSA_HPG
    tq = min(seq, 256)
    nq = seq // tq
    n_sb = seq // SEL_BLOCK
    assert n_sb <= HEAD_DIM, "selection-block one-hot shares the 64 spare key lanes"
    n_cmp = kc.shape[2]
    units = np.arange(n_cmp)[:, None] + np.arange(CMP_BLOCK // CMP_STRIDE)[None, :]
    ovl = np.zeros((n_cmp, n_sb), np.float32)
    for c in range((seq - CMP_BLOCK) // CMP_STRIDE + 1):
        for u in units[c]:
            ovl[c, u // (SEL_BLOCK // CMP_STRIDE)] += 1.0
    ovl_t = jnp.asarray(ovl.T)

    def per_b(shape):
        return pl.BlockSpec((None, g) + shape, lambda b, qi: (b, 0, 0, 0))

    width = g * hpg * HEAD_DIM
    tab = pl.BlockSpec((HEAD_DIM, tq), lambda b, qi: (0, b * nq + qi))
    tk = min(seq, 512)
    assert seq >= WINDOW + tq and seq % tk == 0 and tk % tq == 0
    kern = functools.partial(_nsa_kernel, tq=tq, tk=tk, n_sb=n_sb)
    return pl.pallas_call(
        kern, grid=(batch, nq),
        in_specs=[pl.BlockSpec((tq, width), lambda b, qi: (b * nq + qi, q_col0)),
                  pl.BlockSpec((tq, LANES), lambda b, qi: (b * nq + qi, gate_col)),
                  tab, tab,
                  per_b((n_cmp, HEAD_DIM)), per_b((HEAD_DIM, n_cmp)),
                  per_b((seq, 2 * HEAD_DIM)), per_b((V_ROWS, seq)),
                  per_b((seq, HEAD_DIM)), per_b((V_ROWS, seq)),
                  pl.BlockSpec((n_sb, n_cmp), lambda b, qi: (0, 0)),
                  pl.BlockSpec((HEAD_DIM, g * hpg), lambda b, qi: (0, 0))],
        out_specs=pl.BlockSpec((tq, width), lambda b, qi: (b * nq + qi, 0)),
        out_shape=jax.ShapeDtypeStruct((batch * seq, width), BF16),
        compiler_params=_params("parallel", "arbitrary"), name="nsa_attention",
    )(h, h, cos_t, sin_t, kc, vc_t, ksa, vs_t, kw, vw_t, ovl_t, norm_w.reshape(g * hpg, HEAD_DIM).T)


def _out_proj_kernel(x_ref, yhg_ref, ygm_ref, ynsa_ref, whg_ref, wgm_ref, wnsa_ref, lnw_ref, lnb_ref,
                     o_ref, oa_ref, ob_ref, *, alpha):
    mix = (_dot(yhg_ref[...], whg_ref[...]) + _dot(ygm_ref[...], wgm_ref[...])
           + _dot(ynsa_ref[...], wnsa_ref[...]))
    y = _layer_norm(alpha * x_ref[...] + mix, lnw_ref[...], lnb_ref[...])
    o_ref[...] = y
    _store_word_tables((oa_ref, ob_ref), _pack_bf16_pairs(y))


def out_proj_ln(x2d, y_hg, y_gm, y_nsa, w_out, ln_w, ln_b, alpha):
    n, d = x2d.shape
    w1, w2 = y_hg.shape[1], y_hg.shape[1] + y_gm.shape[1]
    whg = w_out[:w1].astype(BF16)
    wgm = w_out[w1:w2].astype(BF16)
    wnsa = w_out[w2:].astype(BF16)
    t = min(n, 512)

    def row(wd):
        return pl.BlockSpec((t, wd), lambda i: (i, 0))

    def const(shape):
        return pl.BlockSpec(shape, lambda i: (0,) * len(shape))

    kern = functools.partial(_out_proj_kernel, alpha=alpha)
    return pl.pallas_call(
        kern, grid=(n // t,),
        in_specs=[row(d), row(y_hg.shape[1]), row(y_gm.shape[1]), row(y_nsa.shape[1]),
                  const(whg.shape), const(wgm.shape), const(wnsa.shape), const((1, d)), const((1, d))],
        out_specs=[row(d), row(SC_ROW_WORDS), row(SC_ROW_WORDS)],
        out_shape=[jax.ShapeDtypeStruct((n, d), F32)] + [jax.ShapeDtypeStruct((n, SC_ROW_WORDS), jnp.uint32)] * 2,
        compiler_params=_params("parallel"), name="out_proj_ln",
    )(x2d, y_hg, y_gm, y_nsa, whg, wgm, wnsa, ln_w.reshape(1, d), ln_b.reshape(1, d))


def _router_kernel(x_ref, w_ref, b_ref, e_ref, p_ref, r_ref, cnt_ref, carry_ref):
    t = x_ref.shape[0]

    @pl.when(pl.program_id(0) == 0)
    def _():
        carry_ref[...] = jnp.zeros_like(carry_ref)

    x = x_ref[...]
    x_hi = x.astype(BF16)
    x_lo = (x - x_hi.astype(F32)).astype(BF16)
    w = w_ref[...]
    w_hi = w.astype(BF16)
    w_lo = (w - w_hi.astype(F32)).astype(BF16)
    logits = _dot_nt(w_hi, x_hi) + (_dot_nt(w_hi, x_lo) + _dot_nt(w_lo, x_hi)) + b_ref[...]
    n_e = logits.shape[0]
    sub = lax.broadcasted_iota(jnp.int32, logits.shape, 0)
    work = logits
    vals, idxs = [], []
    sel = jnp.zeros(logits.shape, F32)
    for _ in range(TOP_K):
        m = jnp.max(work, axis=0, keepdims=True)
        idx = jnp.min(jnp.where(work == m, sub, n_e), axis=0, keepdims=True)
        hit = sub == idx
        sel = jnp.where(hit, 1.0, sel)
        work = jnp.where(hit, -jnp.inf, work)
        vals.append(m)
        idxs.append(idx)
    exps = [jnp.exp(v - vals[0]) for v in vals]
    inv_den = 1.0 / (exps[0] + exps[1] + exps[2] + exps[3])
    earlier = (lax.broadcasted_iota(jnp.int32, (t, t), 0) < lax.broadcasted_iota(jnp.int32, (t, t), 1))
    before = _dot(sel.astype(BF16), earlier.astype(BF16)) + carry_ref[...]
    ranks = [jnp.sum(jnp.where(sub == idx, before, 0.0), axis=0, keepdims=True) for idx in idxs]
    e_ref[...] = jnp.concatenate(idxs, axis=0)
    p_ref[...] = jnp.concatenate([e * inv_den for e in exps], axis=0)
    r_ref[...] = jnp.concatenate(ranks, axis=0).astype(jnp.int32)
    carry_ref[...] = carry_ref[...] + jnp.sum(sel, axis=1, keepdims=True)
    cnt_ref[...] = carry_ref[...].astype(jnp.int32)


def moe_router(x2d, router_w, router_b):
    n, d = x2d.shape
    e = router_w.shape[1]
    t = min(n, 512)
    row4 = pl.BlockSpec((TOP_K, t), lambda i: (0, i))
    top_e, top_p, rank, counts = pl.pallas_call(
        _router_kernel, grid=(n // t,),
        in_specs=[pl.BlockSpec((t, d), lambda i: (i, 0)), pl.BlockSpec((e, d), lambda i: (0, 0)),
                  pl.BlockSpec((e, 1), lambda i: (0, 0))],
        out_specs=[row4, row4, row4, pl.BlockSpec((e, 1), lambda i: (0, 0))],
        out_shape=[jax.ShapeDtypeStruct((TOP_K, n), jnp.int32), jax.ShapeDtypeStruct((TOP_K, n), F32),
                   jax.ShapeDtypeStruct((TOP_K, n), jnp.int32), jax.ShapeDtypeStruct((e, 1), jnp.int32)],
        scratch_shapes=[pltpu.VMEM((e, 1), F32)],
        compiler_params=_params("arbitrary"), name="moe_router",
    )(x2d, router_w.T, router_b.reshape(e, 1))
    return top_e, top_p, rank, counts[:, 0]


def _expert_kernel(be_ref, valid_ref, xa_ref, xb_ref, wu_ref, bu_ref, wd_ref, bd_ref, oa_ref, ob_ref,
                   wu_bf, wd_bf):
    i = pl.program_id(0)
    f = wd_ref.shape[0]
    n_used = be_ref[pl.num_programs(0)]

    @pl.when((i == 0) | (be_ref[i] != be_ref[jnp.maximum(i - 1, 0)]))
    def _():
        wu_bf[...] = wu_ref[...].astype(BF16)
        wd_bf[...] = wd_ref[...].astype(BF16)

    @pl.when(i < n_used)
    def _():
        packed = jnp.concatenate([xa_ref[...], xb_ref[...]], axis=1)
        live = lax.broadcasted_iota(jnp.int32, packed.shape, 0) < valid_ref[i]
        x_lo, x_hi = _unpack_bf16_pairs(jnp.where(live, packed, jnp.uint32(0)))
        x = jnp.concatenate([x_lo.astype(BF16), x_hi.astype(BF16)], axis=1)
        hcat = _dot(x, wu_bf[...]) + bu_ref[...]
        glu = jnp.minimum(hcat[:, :f], SWIGLU_LIMIT)
        lin = jnp.clip(hcat[:, f:], -SWIGLU_LIMIT, SWIGLU_LIMIT)
        act = glu * _sigmoid(SWIGLU_ALPHA * glu) * (lin + 1.0)
        _store_word_tables((oa_ref, ob_ref), _pack_bf16_pairs(_dot(act.astype(BF16), wd_bf[...]) + bd_ref[...]))

    @pl.when(i >= n_used)
    def _():
        oa_ref[...] = jnp.zeros_like(oa_ref)
        ob_ref[...] = jnp.zeros_like(ob_ref)


def moe_experts(xa, xb, block_e, n_used, block_valid, w_up, b_up, w_down, b_down, layer):
    rows = xa.shape[0]
    _, e, d, f2 = w_up.shape
    f = f2 // 2
    nb = rows // EXPERT_BLOCK
    words = pl.BlockSpec((EXPERT_BLOCK, SC_ROW_WORDS), lambda i, be, nv: (i, 0))
    grid_spec = pltpu.PrefetchScalarGridSpec(
        num_scalar_prefetch=2, grid=(nb,),
        in_specs=[words, words,
                  pl.BlockSpec((None, None, d, f2), lambda i, be, nv: (layer, be[i], 0, 0)),
                  pl.BlockSpec((None, None, 1, f2), lambda i, be, nv: (layer, be[i], 0, 0)),
                  pl.BlockSpec((None, None, f, d), lambda i, be, nv: (layer, be[i], 0, 0)),
                  pl.BlockSpec((None, None, 1, d), lambda i, be, nv: (layer, be[i], 0, 0))],
        out_specs=[words, words],
        scratch_shapes=[pltpu.VMEM((d, f2), BF16), pltpu.VMEM((f, d), BF16)])
    depth = w_up.shape[0]
    return pl.pallas_call(
        _expert_kernel, grid_spec=grid_spec,
        out_shape=[jax.ShapeDtypeStruct((rows, SC_ROW_WORDS), jnp.uint32)] * 2,
        compiler_params=pltpu.CompilerParams(dimension_semantics=("arbitrary",), vmem_limit_bytes=EXPERT_VMEM_LIMIT),
        name="moe_experts",
    )(jnp.concatenate([block_e, n_used.reshape(1)]), block_valid, xa, xb, w_up, b_up.reshape(depth, e, 1, f2),
      w_down, b_down.reshape(depth, e, 1, d))


def _combine_kernel(x_ref, ya_ref, yb_ref, p_ref, lnw_ref, lnb_ref, *rest, alpha):
    o_ref = rest[-1]
    p = p_ref[...]
    moe = jnp.zeros(x_ref.shape, F32)
    for k in range(TOP_K):
        y_lo, y_hi = _unpack_bf16_pairs(jnp.concatenate([ya_ref[k], yb_ref[k]], axis=1))
        moe = moe + p[:, k:k + 1] * jnp.concatenate([y_lo, y_hi], axis=1)
    o_ref[...] = _layer_norm(alpha * x_ref[...] + moe, lnw_ref[...], lnb_ref[...])


def combine_ln(x2d, ya, yb, top_p, ln_w, ln_b, alpha, first_row=0, earlier=None):
    n, d = x2d.shape
    r = ya.shape[1]
    t = min(r, 512)
    off = first_row // t
    kern = functools.partial(_combine_kernel, alpha=alpha)
    words = pl.BlockSpec((TOP_K, t, SC_ROW_WORDS), lambda i: (0, i, 0))
    in_specs = [pl.BlockSpec((t, d), lambda i: (i + off, 0)), words, words,
                pl.BlockSpec((t, TOP_K), lambda i: (i + off, 0)),
                pl.BlockSpec((1, d), lambda i: (0, 0)), pl.BlockSpec((1, d), lambda i: (0, 0))]
    args = [x2d, ya, yb, top_p, ln_w.reshape(1, d), ln_b.reshape(1, d)]
    aliases = {}
    if earlier is not None:
        in_specs.append(pl.BlockSpec(memory_space=pl.ANY))
        args.append(earlier)
        aliases = {len(args) - 1: 0}
    return pl.pallas_call(
        kern, grid=(r // t,), in_specs=in_specs,
        out_specs=pl.BlockSpec((t, d), lambda i: (i + off, 0)),
        out_shape=jax.ShapeDtypeStruct((n, d), F32), input_output_aliases=aliases,
        compiler_params=_params("parallel"), name="moe_combine_ln",
    )(*args)


def _sc_mesh():
    return plsc.VectorSubcoreMesh(core_axis_name="core", subcore_axis_name="subcore")


def sc_gather_rows(tables, idx):
    r = idx.shape[0]
    nt = len(tables)
    out = jax.ShapeDtypeStruct((r, SC_ROW_WORDS), tables[0].dtype)

    @pl.kernel(out_type=(out,) * nt, mesh=_sc_mesh(), name="sc_gather_rows")
    def gather(*refs):
        x_hbm, i_hbm, o_hbm = refs[:nt], refs[nt], refs[nt + 1:]
        for j in range(nt):
            def body(i_vmem, o_vmem, table=x_hbm[j]):
                pltpu.sync_copy(table.at[i_vmem.at[0]], o_vmem)

            pltpu.emit_pipeline(
                body, grid=(r // SC_WINDOW,),
                in_specs=[pl.BlockSpec((1, SC_WINDOW), lambda i: (0, i))],
                out_specs=[pl.BlockSpec((SC_WINDOW, SC_ROW_WORDS), lambda i: (i, 0))],
                core_axis_name=("core", "subcore"), dimension_semantics=(pltpu.PARALLEL,),
            )(i_hbm, o_hbm[j])

    return gather(*tables, idx.reshape(1, r))


def sc_scatter_rows(tables, dest_t, n_rows):
    n = tables[0].shape[0]
    nt = len(tables)
    copies = dest_t.shape[0]
    out = jax.ShapeDtypeStruct((n_rows, SC_ROW_WORDS), tables[0].dtype)

    @pl.kernel(out_type=(out,) * nt, mesh=_sc_mesh(), scratch_types=[], name="sc_scatter_rows")
    def scatter(*refs):
        x_hbm, i_hbm, o_hbm = refs[:nt], refs[nt], refs[nt + 1:]
        for j in range(nt):
            def body(x_vmem, i_vmem, out_j=o_hbm[j]):
                for k in range(copies):
                    pltpu.sync_copy(x_vmem, out_j.at[i_vmem.at[k]])

            pltpu.emit_pipeline(
                body, grid=(n // SC_WINDOW,),
                in_specs=[pl.BlockSpec((SC_WINDOW, SC_ROW_WORDS), lambda i: (i, 0)),
                          pl.BlockSpec((copies, SC_WINDOW), lambda i: (0, i))],
                out_specs=[],
                core_axis_name=("core", "subcore"), dimension_semantics=(pltpu.PARALLEL,),
            )(x_hbm[j], i_hbm)

    return scatter(*tables, dest_t)


def moe_ffn_ln(x_f32, x_packed, router_w, router_b, w_up, b_up, w_down, b_down, layer, ln_w, ln_b, alpha):
    n, d = x_f32.shape
    top_e, top_p, rank, counts = moe_router(x_f32, router_w, router_b)
    padded = (counts + EXPERT_BLOCK - 1) // EXPERT_BLOCK * EXPERT_BLOCK
    pad_end = jnp.cumsum(padded)
    pad_start = pad_end - padded
    n_assign = n * TOP_K
    n_blocks = -(-(n_assign + N_EXPERTS * (EXPERT_BLOCK - 1)) // EXPERT_BLOCK)
    experts = jnp.arange(N_EXPERTS, dtype=jnp.int32)
    dest_t = rank + jnp.sum(jnp.where(top_e[..., None] == experts, pad_start.astype(jnp.int32), 0), axis=-1)
    block_first = jnp.arange(n_blocks, dtype=jnp.int32) * EXPERT_BLOCK
    block_e = jnp.clip(jnp.sum((pad_end[None, :] <= block_first[:, None]).astype(jnp.int32), axis=1),
                       0, N_EXPERTS - 1)
    block_valid = jnp.clip(counts[block_e] - (block_first - pad_start[block_e]), 0, EXPERT_BLOCK)
    n_used = (pad_end[-1] // EXPERT_BLOCK).astype(jnp.int32)
    xa, xb = sc_scatter_rows(x_packed, dest_t, n_blocks * EXPERT_BLOCK)
    ya, yb = moe_experts(xa, xb, block_e, n_used, block_valid.astype(jnp.int32), w_up, b_up, w_down, b_down,
                         layer)
    out = None
    top_p = top_p.T
    half = n // MOE_COMBINE_PARTS
    for part in range(MOE_COMBINE_PARTS):
        ga, gb = sc_gather_rows((ya, yb), dest_t[:, part * half:(part + 1) * half].reshape(-1))
        out = combine_ln(x_f32, ga.reshape(TOP_K, half, SC_ROW_WORDS), gb.reshape(TOP_K, half, SC_ROW_WORDS),
                         top_p, ln_w, ln_b, alpha, first_row=part * half, earlier=out)
    return out


def kernel(x, positions, w_in, hg_lower_bounds, hg_norm_w, gm_ln_w, gm_ln_b, gm_spatial_w, gm_spatial_b, gm_norm_w, nsa_cmp_pe, nsa_cmp_w1, nsa_cmp_w2, nsa_norm_w, w_out, ln1_w, ln1_b, router_w, router_b, exp_w_up, exp_b_up, exp_w_down, exp_b_down, ln2_w, ln2_b):
    batch, seq, d = x.shape
    depth = w_in.shape[0]
    n = batch * seq
    alpha = (2 * depth) ** 0.25
    hg_w = hg_norm_w.shape[1]
    gm_w = gm_norm_w.shape[1]
    nsa_w = nsa_norm_w.shape[1]
    kv_w = NSA_KV_GROUPS * HEAD_DIM
    in_width = w_in.shape[2]
    off_gm = 4 * hg_w
    off_q = off_gm + 2 * gm_w
    off_kv = off_q + nsa_w
    off_gate = off_kv + 6 * kv_w
    width_pad = -(-in_width // LANES) * LANES

    cosf, sinf, cos_t, sin_t = rope_tables(positions)
    lb_all = jnp.cumsum(jax.nn.softmax(hg_lower_bounds.astype(F32), axis=0), axis=0)
    lb_all = lb_all - lb_all[0:1]

    x2d = x.reshape(n, d)
    for l in range(depth):
        w_l = jnp.pad(w_in[l], ((0, 0), (0, width_pad - in_width))).astype(BF16)
        h = in_proj(x2d, w_l)
        h3 = h.reshape(batch, seq, width_pad)
        y_hg = hgrn2(h3, lb_all[l], hg_norm_w[l]).reshape(n, hg_w)
        y_gm = gmlp(h, gm_ln_w[l], gm_ln_b[l], gm_spatial_w[l], gm_spatial_b[l], gm_norm_w[l],
                    off_gm // gm_w, off_gm // gm_w + 1)
        kc = compress(h3[:, :, off_kv:off_kv + kv_w], nsa_cmp_pe[l, 0], nsa_cmp_w1[l, 0], nsa_cmp_w2[l, 0], batch, seq)
        vc = compress(h3[:, :, off_kv + kv_w:off_kv + 2 * kv_w], nsa_cmp_pe[l, 1], nsa_cmp_w1[l, 1],
                      nsa_cmp_w2[l, 1], batch, seq)
        n_cmp = kc.shape[1]
        kc = kc.reshape(batch, n_cmp, NSA_KV_GROUPS, HEAD_DIM).transpose(0, 2, 1, 3)
        vc_t = vc.reshape(batch, n_cmp, NSA_KV_GROUPS, HEAD_DIM).transpose(0, 2, 3, 1)
        ksa, vs_t, kw, vw_t = nsa_kprep(h, cosf, sinf, batch, seq, (off_kv + 2 * kv_w) // LANES)
        y_nsa = nsa_attention(h, cos_t, sin_t, kc, vc_t, ksa, vs_t, kw, vw_t, nsa_norm_w[l], batch, seq,
                              off_q // nsa_w, off_gate // LANES)
        x1, x1a, x1b = out_proj_ln(x2d, y_hg, y_gm, y_nsa, w_out[l], ln1_w[l], ln1_b[l], alpha)
        x2d = moe_ffn_ln(x1, (x1a, x1b), router_w[l], router_b[l], exp_w_up, exp_b_up, exp_w_down, exp_b_down, l,
                         ln2_w[l], ln2_b[l], alpha)
    return x2d.reshape(batch, seq, d)
```

```python
import functools
import math

import numpy as np
import jax
import jax.numpy as jnp
from jax import lax
from jax.experimental import pallas as pl
from jax.experimental.pallas import tpu as pltpu
from jax.experimental.pallas import tpu_sc as plsc

F32 = jnp.float32
BF16 = jnp.bfloat16
HIGHEST = lax.Precision.HIGHEST

HEAD_DIM = 64
LANES = 128
VMEM_LIMIT = 48 * 1024 * 1024
EXPERT_VMEM_LIMIT = 56 * 1024 * 1024

HG_CHUNK = 64
GM_CHUNK = 128
GM_TILE_CHUNKS = 4
NSA_KV_GROUPS = 2
NSA_HPG = 4
CMP_BLOCK = 32
CMP_STRIDE = 16
CMP_HIDDEN = 128
SEL_BLOCK = 64
N_SEL = 16
WINDOW = 512
N_GATES = 3
IMP_FORCE = 1e9
FORCE_KEY = int(np.float32(IMP_FORCE).view(np.int32))
NEG_INF = -1e30
N_EXPERTS = 32
TOP_K = 4
SWIGLU_ALPHA = 1.702
SWIGLU_LIMIT = 7.0
EXPERT_BLOCK = 512
SC_ROW_WORDS = 256
SC_WINDOW = 128
ROPE_THETA = 10000.0
LOG2_E = 1.4426950408889634
LN_EPS = 1e-5
RMS_EPS = 1e-6
V_ROWS = HEAD_DIM + 16


def _params(*sem):
    return pltpu.CompilerParams(dimension_semantics=sem, vmem_limit_bytes=VMEM_LIMIT)


def _dot(a, b):
    return jnp.dot(a, b, preferred_element_type=F32)


def _dot_nt(a, b, precision=None):
    return lax.dot_general(a, b, (((1,), (1,)), ((), ())), precision=precision,
                           preferred_element_type=F32)


def _dot_tn(a, b):
    return lax.dot_general(a, b, (((0,), (0,)), ((), ())), preferred_element_type=F32)


def _sigmoid(x):
    return 1.0 / (1.0 + jnp.exp(-x))


def _gelu(x):
    return 0.5 * x * (1.0 + jnp.tanh(0.7978845608028654 * (x + 0.044715 * x * x * x)))


def _layer_norm(x, w, b):
    mu = jnp.mean(x, axis=-1, keepdims=True)
    xc = x - mu
    var = jnp.mean(xc * xc, axis=-1, keepdims=True)
    return xc * lax.rsqrt(var + LN_EPS) * w + b


def _pack_bf16_pairs(y):
    w = y.shape[1] // 2
    bits = pltpu.bitcast(y.astype(BF16).astype(F32), jnp.uint32)
    return lax.shift_right_logical(bits[:, :w], jnp.uint32(16)) | (bits[:, w:] & jnp.uint32(0xFFFF0000))


def _unpack_bf16_pairs(u):
    lo = pltpu.bitcast(lax.shift_left(u, jnp.uint32(16)), F32)
    hi = pltpu.bitcast(u & jnp.uint32(0xFFFF0000), F32)
    return lo, hi


def _store_word_tables(refs, packed):
    for j, ref in enumerate(refs):
        ref[...] = packed[:, j * SC_ROW_WORDS:(j + 1) * SC_ROW_WORDS]


def _head_mean_sq(o, bd_ones):
    sq = o * o
    hi = sq.astype(BF16)
    lo = (sq - hi.astype(F32)).astype(BF16)
    ones = bd_ones.astype(BF16)
    return (_dot(hi, ones) + _dot(lo, ones)) * (1.0 / HEAD_DIM)


def _rope_kernel(pos_ref, inv_ref, cos_ref, sin_ref, cost_ref, sint_ref):
    ang = inv_ref[...] * pos_ref[...]
    c = jnp.cos(ang)
    s = jnp.sin(ang)
    cos_t = jnp.concatenate([c, c], axis=0)
    sin_t = jnp.concatenate([-s, s], axis=0)
    cost_ref[...] = cos_t
    sint_ref[...] = sin_t
    cos_ref[...] = cos_t.T
    sin_ref[...] = sin_t.T


def rope_tables(positions):
    n = positions.size
    tile = min(n, 2048)
    posf = positions.reshape(1, n).astype(F32)
    inv = ROPE_THETA ** (-jnp.arange(0, HEAD_DIM, 2, dtype=F32) / HEAD_DIM)
    row = pl.BlockSpec((tile, HEAD_DIM), lambda i: (i, 0))
    rowt = pl.BlockSpec((HEAD_DIM, tile), lambda i: (0, i))
    return pl.pallas_call(
        _rope_kernel, grid=(n // tile,),
        in_specs=[pl.BlockSpec((1, tile), lambda i: (0, i)), pl.BlockSpec((HEAD_DIM // 2, 1), lambda i: (0, 0))],
        out_specs=[row, row, rowt, rowt],
        out_shape=[jax.ShapeDtypeStruct((n, HEAD_DIM), F32)] * 2 + [jax.ShapeDtypeStruct((HEAD_DIM, n), F32)] * 2,
        compiler_params=_params("parallel"), name="rope_tables",
    )(posf, inv.reshape(HEAD_DIM // 2, 1))


def _in_proj_kernel(x_ref, w_ref, h_ref):
    h_ref[...] = _dot(x_ref[...].astype(BF16), w_ref[...])


def in_proj(x2d, w_bf16):
    n, d = x2d.shape
    width = w_bf16.shape[1]
    tile = min(n, 512)
    return pl.pallas_call(
        _in_proj_kernel, grid=(n // tile,),
        in_specs=[pl.BlockSpec((tile, d), lambda i: (i, 0)), pl.BlockSpec((d, width), lambda i: (0, 0))],
        out_specs=pl.BlockSpec((tile, width), lambda i: (i, 0)),
        out_shape=jax.ShapeDtypeStruct((n, width), F32),
        compiler_params=_params("parallel"), name="in_proj")(x2d, w_bf16)


HG_LEVELS = (64, 32, 16, 8, 4, 2)
HG_BATCH = 8


def _hgrn_constants():
    c = HG_CHUNK
    t = np.arange(c)
    u = t[None, :]
    rows = [u <= t[:, None], u > t[:, None]]
    masks = [np.eye(c, dtype=bool)]
    for m in HG_LEVELS:
        ref = ((t // m) * m + m // 2 - 1)[:, None]
        second = (t % m >= m // 2)[:, None]
        rows.append(((u > ref) & (u <= t[:, None]) & second) | ((u > t[:, None]) & (u <= ref) & ~second))
        masks.append((t[:, None] // m == t[None, :] // m) & second & (t[None, :] % m < m // 2))
    pmat = np.concatenate(rows, axis=0).astype(np.float32)
    masks = np.stack([np.tile(mk, (1, 4)) for mk in masks]).astype(np.float32)
    return pmat, masks


def _hgrn_kernel(q_ref, f_ref, i_ref, g_ref, lb_ref, nw_ref, pmat_ref, masks_ref, bd_ref, hm_ref,
                 o_ref, state_ref):
    c = HG_CHUNK

    @pl.when(pl.program_id(1) == 0)
    def _():
        state_ref[...] = jnp.zeros_like(state_ref)

    lb = lb_ref[...]
    bd = bd_ref[...]
    hm = hm_ref[...]
    hm_tiles = [jnp.broadcast_to(hm[h:h + 1], (c, hm.shape[1])).astype(BF16) for h in range(4)]
    pmat = pmat_ref[...]
    a = jnp.log(lb)
    log1m = jnp.log(1.0 - lb)
    nb, _, w = q_ref.shape
    seqs = range(nb)

    fr = f_ref[...].reshape(nb * c, w)
    hq = q_ref[...].reshape(nb * c, w)
    v = i_ref[...].reshape(nb * c, w)
    qf = hq * _sigmoid(hq)
    log_sig = jnp.minimum(fr, 0.0) - jnp.log(1.0 + jnp.exp(-jnp.abs(fr)))
    cc = log1m + log_sig
    log_f = jnp.maximum(a, cc) + jnp.log(1.0 + jnp.exp(-jnp.abs(a - cc)))
    kk = (1.0 - lb) * _sigmoid(-fr)

    hi = log_f.astype(BF16)
    lo = (log_f - hi.astype(F32)).astype(BF16)

    def side_by_side(x):
        return jnp.concatenate([x[b * c:(b + 1) * c] for b in seqs], axis=1)

    sums = jnp.minimum(_dot(pmat, side_by_side(hi)) + _dot(pmat, side_by_side(lo)), 0.0)
    e_all = jnp.exp(sums)

    def e_rows(r, b):
        return e_all[r * c:(r + 1) * c, b * w:(b + 1) * w]

    def stacked(x):
        xb = x.astype(BF16)
        return jnp.concatenate([xb * hm_tiles[h] for h in range(4)], axis=0)

    q_s = [qf[b * c:(b + 1) * c] for b in seqs]
    k_s = [kk[b * c:(b + 1) * c] for b in seqs]
    v_s = [v[b * c:(b + 1) * c] for b in seqs]

    att = [masks_ref[0] * _dot_nt(q_s[b].astype(BF16), stacked(k_s[b])) for b in seqs]
    for li in range(len(HG_LEVELS)):
        for b in seqs:
            e_l = e_rows(2 + li, b)
            att[b] = att[b] + masks_ref[li + 1] * _dot_nt((q_s[b] * e_l).astype(BF16), stacked(k_s[b] * e_l))

    outs = []
    for b in seqs:
        e_b = e_rows(0, b)
        st = state_ref[b]
        o = _dot(att[b].astype(BF16), stacked(v_s[b])) + _dot_nt((q_s[b] * e_b).astype(BF16), st.astype(BF16))
        k_rest = (k_s[b] * e_rows(1, b)).astype(BF16)
        state_ref[b] = st * e_b[c - 1:c] + bd * _dot_tn(v_s[b].astype(BF16), k_rest)
        outs.append(o)

    o = jnp.concatenate(outs, axis=0)
    ms = _head_mean_sq(o, bd)
    y = o * lax.rsqrt(ms + RMS_EPS) * nw_ref[...] * _sigmoid(g_ref[...].reshape(nb * c, w))
    o_ref[...] = y.astype(o_ref.dtype).reshape(nb, c, w)


def hgrn2(h3, lb, norm_w):
    batch, seq, _ = h3.shape
    w = lb.shape[-1]
    c = HG_CHUNK
    nb = math.gcd(batch, HG_BATCH)
    pmat, masks = _hgrn_constants()
    lane_head = np.arange(w) // HEAD_DIM
    bd = (lane_head[:, None] == lane_head[None, :]).astype(np.float32)
    hm = (np.arange(4)[:, None] == lane_head[None, :]).astype(np.float32)

    def col(j):
        return pl.BlockSpec((nb, c, w), lambda b, i, j=j: (b, i, j))

    def const(shape):
        return pl.BlockSpec(shape, lambda b, i: (0,) * len(shape))

    return pl.pallas_call(
        _hgrn_kernel, grid=(batch // nb, seq // c),
        in_specs=[col(0), col(1), col(2), col(3), const((1, w)), const((1, w)),
                  const(pmat.shape), const(masks.shape), const(bd.shape), const(hm.shape)],
        out_specs=pl.BlockSpec((nb, c, w), lambda b, i: (b, i, 0)),
        out_shape=jax.ShapeDtypeStruct((batch, seq, w), BF16),
        scratch_shapes=[pltpu.VMEM((nb, w, w), F32)],
        compiler_params=_params("parallel", "arbitrary"), name="hgrn2",
    )(h3, h3, h3, h3, lb.reshape(1, w), norm_w.reshape(1, w), jnp.asarray(pmat, BF16), jnp.asarray(masks),
      jnp.asarray(bd), jnp.asarray(hm))


def _gmlp_kernel(u_ref, v_ref, lnw_ref, lnb_ref, ws_ref, bias_ref, nw_ref, bd_ref, hm_ref, o_ref):
    c = GM_CHUNK
    groups = ws_ref.shape[0]
    u = _gelu(u_ref[...])
    v = _layer_norm(_gelu(v_ref[...]), lnw_ref[...], lnb_ref[...])
    hm = hm_ref[...]
    bd = bd_ref[...]
    causal = lax.broadcasted_iota(jnp.int32, (c, c), 0) >= lax.broadcasted_iota(jnp.int32, (c, c), 1)
    w_cat = jnp.concatenate([jnp.where(causal, ws_ref[g], 0.0).astype(BF16) for g in range(groups)], axis=1)
    for j in range(u.shape[0] // c):
        rows = slice(j * c, (j + 1) * c)
        v_j = v[rows]
        v_bd = jnp.concatenate([v_j * hm[g:g + 1] for g in range(groups)], axis=0).astype(BF16)
        y = u[rows] * (bias_ref[...] + _dot(w_cat, v_bd))
        ms = _head_mean_sq(y, bd)
        o_ref[rows, :] = (y * lax.rsqrt(ms + RMS_EPS) * nw_ref[...]).astype(o_ref.dtype)


def gmlp(h, ln_w, ln_b, w_s, b_s, norm_w, u_col, v_col):
    n = h.shape[0]
    groups, c, _ = w_s.shape
    w = groups * HEAD_DIM
    lane_head = np.arange(w) // HEAD_DIM
    bd = (lane_head[:, None] == lane_head[None, :]).astype(np.float32)
    hm = (np.arange(groups)[:, None] == lane_head[None, :]).astype(np.float32)
    bias = jnp.repeat(b_s.T, HEAD_DIM, axis=1)

    def const(shape):
        return pl.BlockSpec(shape, lambda i: (0,) * len(shape))

    t = math.gcd(n, GM_TILE_CHUNKS * c)
    return pl.pallas_call(
        _gmlp_kernel, grid=(n // t,),
        in_specs=[pl.BlockSpec((t, w), lambda i: (i, u_col)), pl.BlockSpec((t, w), lambda i: (i, v_col)),
                  const((1, w)), const((1, w)), const(w_s.shape), const((c, w)), const((1, w)),
                  const(bd.shape), const(hm.shape)],
        out_specs=pl.BlockSpec((t, w), lambda i: (i, 0)),
        out_shape=jax.ShapeDtypeStruct((n, w), BF16),
        compiler_params=_params("parallel"), name="gmlp",
    )(h, h, ln_w.reshape(1, w), ln_b.reshape(1, w), w_s, bias, norm_w.reshape(1, w),
      jnp.asarray(bd), jnp.asarray(hm))


def _compress_kernel(kv_ref, wtop_ref, wbot_ref, pe_ref, w2_ref, o_ref):
    units = o_ref.shape[0]
    pe = pe_ref[...].astype(BF16)
    p = jnp.zeros((units, wtop_ref.shape[2]), F32)
    q = jnp.zeros_like(p)
    const = jnp.zeros((1, wtop_ref.shape[2]), F32)
    for j in range(CMP_STRIDE):
        rows = kv_ref[pl.ds(j, units, stride=CMP_STRIDE), :].astype(BF16)
        p = p + _dot(rows, wtop_ref[j])
        q = q + _dot(rows, wbot_ref[j])
        const = const + _dot(pe[0, j:j + 1], wtop_ref[j]) + _dot(pe[1, j:j + 1], wbot_ref[j])
    q_next = jnp.concatenate([q[1:], jnp.zeros_like(q[0:1])], axis=0)
    hid = _gelu(p + q_next + const)
    o_ref[...] = _dot(hid.astype(BF16), w2_ref[...]).astype(o_ref.dtype)


def compress(h, col, pe, w1, w2, batch, seq):
    g = NSA_KV_GROUPS
    half = CMP_STRIDE
    units = seq // half
    gw = g * HEAD_DIM
    eye = jnp.eye(g, dtype=F32)
    w1r = w1.reshape(2, half, HEAD_DIM, CMP_HIDDEN)
    wbd = jnp.einsum('hjdn,gk->hjgdkn', w1r, eye).reshape(2, half, gw, g * CMP_HIDDEN).astype(BF16)
    w2bd = jnp.einsum('nd,gk->gnkd', w2, eye).reshape(g * CMP_HIDDEN, gw).astype(BF16)
    pe2 = jnp.broadcast_to(pe.reshape(2, half, 1, HEAD_DIM), (2, half, g, HEAD_DIM)).reshape(2, half, gw)

    def const(shape):
        return pl.BlockSpec(shape, lambda b: (0,) * len(shape))

    out = pl.pallas_call(
        _compress_kernel, grid=(batch,),
        in_specs=[pl.BlockSpec((seq, gw), lambda b: (b, col)),
                  const(wbd.shape[1:]), const(wbd.shape[1:]), const(pe2.shape), const(w2bd.shape)],
        out_specs=pl.BlockSpec((units, gw), lambda b: (b, 0)),
        out_shape=jax.ShapeDtypeStruct((batch * units, gw), BF16),
        compiler_params=_params("parallel"), name="nsa_compress",
    )(h, wbd[0], wbd[1], pe2, w2bd)
    return out.reshape(batch, units, gw)


def _rot_half_pairs(x):
    lane = lax.broadcasted_iota(jnp.int32, x.shape, 1)
    fwd = pltpu.roll(x, 32, axis=1)
    bwd = pltpu.roll(x, 96, axis=1)
    return jnp.where((lane % HEAD_DIM) < HEAD_DIM // 2, bwd, fwd)


def _kprep_kernel(ks_ref, vs_ref, kw_ref, vw_ref, cos_ref, sin_ref, ksa_ref, vso_ref, kwo_ref, vwo_ref):
    t = ks_ref.shape[0]
    cos = cos_ref[...]
    sin = sin_ref[...]
    cos2 = jnp.concatenate([cos, cos], axis=1)
    sin2 = jnp.concatenate([sin, sin], axis=1)
    ks = ks_ref[...]
    kw = kw_ref[...]
    ks_r = ks * cos2 + _rot_half_pairs(ks) * sin2
    kw_r = kw * cos2 + _rot_half_pairs(kw) * sin2
    pos = pl.program_id(1) * t + lax.broadcasted_iota(jnp.int32, (t, HEAD_DIM), 0)
    onehot = (pos // SEL_BLOCK == lax.broadcasted_iota(jnp.int32, (t, HEAD_DIM), 1)).astype(F32)
    vs_t = vs_ref[...].T
    vw_t = vw_ref[...].T
    tail = (lax.broadcasted_iota(jnp.int32, (V_ROWS - HEAD_DIM, t), 0) == 0).astype(F32)
    for g in range(NSA_KV_GROUPS):
        sl = slice(g * HEAD_DIM, (g + 1) * HEAD_DIM)
        ksa_ref[g] = jnp.concatenate([ks_r[:, sl], onehot], axis=1).astype(BF16)
        vso_ref[g] = jnp.concatenate([vs_t[sl], tail], axis=0).astype(BF16)
        kwo_ref[g] = kw_r[:, sl].astype(BF16)
        vwo_ref[g] = jnp.concatenate([vw_t[sl], tail], axis=0).astype(BF16)


def nsa_kprep(h, cosf, sinf, batch, seq, col0):
    g = NSA_KV_GROUPS
    t = min(seq, 512)
    nt = seq // t

    def col(j):
        return pl.BlockSpec((t, LANES), lambda b, i, j=j: (b * nt + i, col0 + j))

    tab = pl.BlockSpec((t, HEAD_DIM), lambda b, i: (b * nt + i, 0))

    def out(wd):
        return pl.BlockSpec((None, g, t, wd), lambda b, i: (b, 0, i, 0))

    out_t = pl.BlockSpec((None, g, V_ROWS, t), lambda b, i: (b, 0, 0, i))
    k_shape = jax.ShapeDtypeStruct((batch, g, seq, HEAD_DIM), BF16)
    v_shape = jax.ShapeDtypeStruct((batch, g, V_ROWS, seq), BF16)
    return pl.pallas_call(
        _kprep_kernel, grid=(batch, nt),
        in_specs=[col(0), col(1), col(2), col(3), tab, tab],
        out_specs=[out(2 * HEAD_DIM), out_t, out(HEAD_DIM), out_t],
        out_shape=[jax.ShapeDtypeStruct((batch, g, seq, 2 * HEAD_DIM), BF16), v_shape, k_shape, v_shape],
        compiler_params=_params("parallel", "parallel"), name="nsa_kprep",
    )(h, h, h, h, cosf, sinf)


def _nsa_kernel(hq_ref, gate_ref, cos_ref, sin_ref, kc_ref, vc_ref, ksa_ref, vs_ref, kw_ref, vw_ref,
                ovl_ref, nw_ref, o_ref, *, tq, tk, n_sb):
    qi = pl.program_id(1)
    hpg = NSA_HPG
    groups = NSA_KV_GROUPS
    rows = hpg * tq
    t0 = qi * tq
    scale = 1.0 / math.sqrt(HEAD_DIM)
    half = HEAD_DIM // 2

    hq_t = hq_ref[...].T
    cos = cos_ref[...]
    sin = sin_ref[...]
    q_raw, q_rot = [], []
    for g in range(groups):
        raw_g, rot_g = [], []
        for h in range(hpg):
            r0 = (g * hpg + h) * HEAD_DIM
            qh = hq_t[r0:r0 + HEAD_DIM]
            swapped = jnp.concatenate([qh[half:], qh[:half]], axis=0)
            raw_g.append(qh * (scale * LOG2_E))
            rot_g.append((qh * cos + swapped * sin) * (scale * LOG2_E))
        q_raw.append(jnp.concatenate(raw_g, axis=1).astype(BF16))
        q_rot.append(jnp.concatenate(rot_g, axis=1))

    tpos = t0 + lax.broadcasted_iota(jnp.int32, (1, tq), 1)
    tpos_r = jnp.concatenate([tpos] * hpg, axis=1)

    def flash_steps(s, v_t, carry):
        m_new = [jnp.maximum(carry[g][0], jnp.max(s[g], axis=0, keepdims=True)) for g in range(groups)]
        p = [jnp.exp2(s[g] - m_new[g]).astype(BF16) for g in range(groups)]
        return tuple((m_new[g], jnp.exp2(carry[g][0] - m_new[g]) * carry[g][1] + _dot(v_t[g], p[g]))
                     for g in range(groups))

    init = ((jnp.full((1, rows), NEG_INF, F32), jnp.zeros((V_ROWS, rows), F32)),) * groups

    wk = WINDOW + tq
    kw0 = pl.multiple_of(jnp.maximum(t0 - WINDOW, 0), tq)
    kpos_w = kw0 + lax.broadcasted_iota(jnp.int32, (wk, 1), 0)
    mask_w = (kpos_w <= tpos_r) & (kpos_w > tpos_r - WINDOW)
    n_pad = jnp.maximum(WINDOW - 1 - tpos_r, 0).astype(F32)
    n_cmp = kc_ref.shape[1]
    cmp_end = lax.broadcasted_iota(jnp.int32, (n_cmp, 1), 0) * CMP_STRIDE + (CMP_BLOCK - 1)
    mask_c = cmp_end <= tpos_r
    blk = lax.broadcasted_iota(jnp.int32, (n_sb, tq), 0)
    cur = (t0 + lax.broadcasted_iota(jnp.int32, (n_sb, tq), 1)) // SEL_BLOCK
    s_w = [jnp.where(mask_w, _dot(kw_ref[g, pl.ds(kw0, wk), :], q_rot[g].astype(BF16)), NEG_INF)
           for g in range(groups)]
    s_c = [jnp.where(mask_c, _dot(kc_ref[g], q_raw[g]), NEG_INF) for g in range(groups)]
    win = flash_steps(s_w, [vw_ref[g, :, pl.ds(kw0, wk)] for g in range(groups)], init)
    o_w, o_c, imps = [], [], []
    for g in range(groups):
        m_w, acc_w = win[g]
        m_f = jnp.where(n_pad > 0.0, jnp.maximum(m_w, 0.0), m_w)
        a_w = jnp.exp2(m_w - m_f)
        o_w.append(acc_w[0:HEAD_DIM] * (a_w / (acc_w[HEAD_DIM:HEAD_DIM + 1] * a_w + n_pad * jnp.exp2(-m_f))))

    for g in range(groups):
        e_c = jnp.exp2(s_c[g] - jnp.max(s_c[g], axis=0, keepdims=True))
        p_c = jnp.where(mask_c, e_c * (1.0 / jnp.sum(e_c, axis=0, keepdims=True)), 0.0)
        o_c.append(_dot(vc_ref[g], p_c.astype(BF16)))

        p_sum = p_c[:, 0:tq]
        for h in range(1, hpg):
            p_sum = p_sum + p_c[:, h * tq:(h + 1) * tq]
        imp = jnp.dot(ovl_ref[...], p_sum, precision=HIGHEST, preferred_element_type=F32)
        key = pltpu.bitcast(jnp.maximum(imp, 0.0), jnp.int32)
        key = jnp.where((blk == 0) | (blk == cur) | (blk == cur - 1), FORCE_KEY, key)
        imps.append(jnp.where(blk > cur, -1, key))

    n_sel = min(N_SEL, n_sb)

    def bit_body(it, taus):
        bit = lax.shift_left(jnp.int32(1), 30 - it)
        out = []
        for g in range(groups):
            cand = taus[g] | bit
            cnt = jnp.sum(jnp.where(imps[g] >= cand, 1, 0), axis=0, keepdims=True)
            out.append(jnp.where(cnt >= n_sel, cand, taus[g]))
        return tuple(out)

    taus = lax.fori_loop(0, 31, bit_body, (jnp.zeros((1, tq), jnp.int32),) * groups)
    lower = (lax.broadcasted_iota(jnp.int32, (n_sb, n_sb), 0)
             > lax.broadcasted_iota(jnp.int32, (n_sb, n_sb), 1)).astype(BF16)
    q_aug = []
    for g in range(groups):
        above = imps[g] > taus[g]
        equal = imps[g] == taus[g]
        need = n_sel - jnp.sum(jnp.where(above, 1, 0), axis=0, keepdims=True)
        earlier = _dot(lower, jnp.where(equal, 1.0, 0.0).astype(BF16))
        selected = above | (equal & (earlier < need.astype(F32)))
        sel_bias = jnp.where(selected, 0.0, NEG_INF)
        if n_sb < HEAD_DIM:
            sel_bias = jnp.concatenate([sel_bias, jnp.zeros((HEAD_DIM - n_sb, tq), F32)], axis=0)
        q_aug.append(jnp.concatenate([q_rot[g], jnp.concatenate([sel_bias] * hpg, axis=1)], axis=0).astype(BF16))

    def scores(k0, width):
        return [_dot(ksa_ref[g, pl.ds(k0, width), :], q_aug[g]) for g in range(groups)]

    def values(k0, width):
        return [vs_ref[g, :, pl.ds(k0, width)] for g in range(groups)]

    def wide_body(kt, carry):
        k0 = pl.multiple_of(kt * tk, tk)
        return flash_steps(scores(k0, tk), values(k0, tk), carry)

    n_wide = t0 // tk
    carry = lax.fori_loop(0, n_wide, wide_body, init)

    def narrow_body(j, carry):
        k0 = pl.multiple_of(n_wide * tk + j * tq, tq)
        return flash_steps(scores(k0, tq), values(k0, tq), carry)

    carry = lax.fori_loop(0, (t0 - n_wide * tk) // tq, narrow_body, carry)
    k0 = pl.multiple_of(t0, tq)
    mask_s = k0 + lax.broadcasted_iota(jnp.int32, (tq, 1), 0) <= tpos_r
    carry = flash_steps([jnp.where(mask_s, s, NEG_INF) for s in scores(k0, tq)], values(k0, tq), carry)

    gates = _sigmoid(gate_ref[...].T)
    nw = nw_ref[...]
    outs = []
    for g in range(groups):
        acc_s = carry[g][1]
        o_s = acc_s[0:HEAD_DIM] * (1.0 / acc_s[HEAD_DIM:HEAD_DIM + 1])
        for h in range(hpg):
            sl = slice(h * tq, (h + 1) * tq)
            r = (g * hpg + h) * N_GATES
            o = gates[r:r + 1] * o_c[g][:, sl] + gates[r + 1:r + 2] * o_s[:, sl] + gates[r + 2:r + 3] * o_w[g][:, sl]
            ms = jnp.mean(o * o, axis=0, keepdims=True)
            outs.append(o * lax.rsqrt(ms + RMS_EPS) * nw[:, g * hpg + h:g * hpg + h + 1])
    o_ref[...] = jnp.concatenate(outs, axis=0).T.astype(o_ref.dtype)


def nsa_attention(h, cos_t, sin_t, kc, vc_t, ksa, vs_t, kw, vw_t, norm_w, batch, seq, q_col0, gate_col):
    g, hpg = NSA_KV_GROUPS, NSA_HPG
    tq = min(seq, 256)
    nq = seq // tq
    n_sb = seq // SEL_BLOCK
    assert n_sb <= HEAD_DIM, "selection-block one-hot shares the 64 spare key lanes"
    n_cmp = kc.shape[2]
    units = np.arange(n_cmp)[:, None] + np.arange(CMP_BLOCK // CMP_STRIDE)[None, :]
    ovl = np.zeros((n_cmp, n_sb), np.float32)
    for c in range((seq - CMP_BLOCK) // CMP_STRIDE + 1):
        for u in units[c]:
            ovl[c, u // (SEL_BLOCK // CMP_STRIDE)] += 1.0
    ovl_t = jnp.asarray(ovl.T)

    def per_b(shape):
        return pl.BlockSpec((None, g) + shape, lambda b, qi: (b, 0, 0, 0))

    width = g * hpg * HEAD_DIM
    tab = pl.BlockSpec((HEAD_DIM, tq), lambda b, qi: (0, b * nq + qi))
    tk = min(seq, 512)
    assert seq >= WINDOW + tq and seq % tk == 0 and tk % tq == 0
    kern = functools.partial(_nsa_kernel, tq=tq, tk=tk, n_sb=n_sb)
    return pl.pallas_call(
        kern, grid=(batch, nq),
        in_specs=[pl.BlockSpec((tq, width), lambda b, qi: (b * nq + qi, q_col0)),
                  pl.BlockSpec((tq, LANES), lambda b, qi: (b * nq + qi, gate_col)),
                  tab, tab,
                  per_b((n_cmp, HEAD_DIM)), per_b((HEAD_DIM, n_cmp)),
                  per_b((seq, 2 * HEAD_DIM)), per_b((V_ROWS, seq)),
                  per_b((seq, HEAD_DIM)), per_b((V_ROWS, seq)),
                  pl.BlockSpec((n_sb, n_cmp), lambda b, qi: (0, 0)),
                  pl.BlockSpec((HEAD_DIM, g * hpg), lambda b, qi: (0, 0))],
        out_specs=pl.BlockSpec((tq, width), lambda b, qi: (b * nq + qi, 0)),
        out_shape=jax.ShapeDtypeStruct((batch * seq, width), BF16),
        compiler_params=_params("parallel", "arbitrary"), name="nsa_attention",
    )(h, h, cos_t, sin_t, kc, vc_t, ksa, vs_t, kw, vw_t, ovl_t, norm_w.reshape(g * hpg, HEAD_DIM).T)


def _out_proj_kernel(x_ref, yhg_ref, ygm_ref, ynsa_ref, whg_ref, wgm_ref, wnsa_ref, lnw_ref, lnb_ref,
                     o_ref, oa_ref, ob_ref, *, alpha):
    mix = (_dot(yhg_ref[...], whg_ref[...]) + _dot(ygm_ref[...], wgm_ref[...])
           + _dot(ynsa_ref[...], wnsa_ref[...]))
    y = _layer_norm(alpha * x_ref[...] + mix, lnw_ref[...], lnb_ref[...])
    o_ref[...] = y
    _store_word_tables((oa_ref, ob_ref), _pack_bf16_pairs(y))


def out_proj_ln(x2d, y_hg, y_gm, y_nsa, w_out, ln_w, ln_b, alpha):
    n, d = x2d.shape
    w1, w2 = y_hg.shape[1], y_hg.shape[1] + y_gm.shape[1]
    whg = w_out[:w1].astype(BF16)
    wgm = w_out[w1:w2].astype(BF16)
    wnsa = w_out[w2:].astype(BF16)
    t = min(n, 512)

    def row(wd):
        return pl.BlockSpec((t, wd), lambda i: (i, 0))

    def const(shape):
        return pl.BlockSpec(shape, lambda i: (0,) * len(shape))

    kern = functools.partial(_out_proj_kernel, alpha=alpha)
    return pl.pallas_call(
        kern, grid=(n // t,),
        in_specs=[row(d), row(y_hg.shape[1]), row(y_gm.shape[1]), row(y_nsa.shape[1]),
                  const(whg.shape), const(wgm.shape), const(wnsa.shape), const((1, d)), const((1, d))],
        out_specs=[row(d), row(SC_ROW_WORDS), row(SC_ROW_WORDS)],
        out_shape=[jax.ShapeDtypeStruct((n, d), F32)] + [jax.ShapeDtypeStruct((n, SC_ROW_WORDS), jnp.uint32)] * 2,
        compiler_params=_params("parallel"), name="out_proj_ln",
    )(x2d, y_hg, y_gm, y_nsa, whg, wgm, wnsa, ln_w.reshape(1, d), ln_b.reshape(1, d))


def _router_kernel(x_ref, w_ref, b_ref, e_ref, p_ref, r_ref, cnt_ref, carry_ref):
    t = x_ref.shape[0]

    @pl.when(pl.program_id(0) == 0)
    def _():
        carry_ref[...] = jnp.zeros_like(carry_ref)

    x = x_ref[...]
    x_hi = x.astype(BF16)
    x_lo = (x - x_hi.astype(F32)).astype(BF16)
    w = w_ref[...]
    w_hi = w.astype(BF16)
    w_lo = (w - w_hi.astype(F32)).astype(BF16)
    logits = _dot_nt(w_hi, x_hi) + (_dot_nt(w_hi, x_lo) + _dot_nt(w_lo, x_hi)) + b_ref[...]
    n_e = logits.shape[0]
    sub = lax.broadcasted_iota(jnp.int32, logits.shape, 0)
    work = logits
    vals, idxs = [], []
    sel = jnp.zeros(logits.shape, F32)
    for _ in range(TOP_K):
        m = jnp.max(work, axis=0, keepdims=True)
        idx = jnp.min(jnp.where(work == m, sub, n_e), axis=0, keepdims=True)
        hit = sub == idx
        sel = jnp.where(hit, 1.0, sel)
        work = jnp.where(hit, -jnp.inf, work)
        vals.append(m)
        idxs.append(idx)
    exps = [jnp.exp(v - vals[0]) for v in vals]
    inv_den = 1.0 / (exps[0] + exps[1] + exps[2] + exps[3])
    earlier = (lax.broadcasted_iota(jnp.int32, (t, t), 0) < lax.broadcasted_iota(jnp.int32, (t, t), 1))
    before = _dot(sel.astype(BF16), earlier.astype(BF16)) + carry_ref[...]
    ranks = [jnp.sum(jnp.where(sub == idx, before, 0.0), axis=0, keepdims=True) for idx in idxs]
    e_ref[...] = jnp.concatenate(idxs, axis=0)
    p_ref[...] = jnp.concatenate([e * inv_den for e in exps], axis=0)
    r_ref[...] = jnp.concatenate(ranks, axis=0).astype(jnp.int32)
    carry_ref[...] = carry_ref[...] + jnp.sum(sel, axis=1, keepdims=True)
    cnt_ref[...] = carry_ref[...].astype(jnp.int32)


def moe_router(x2d, router_w, router_b):
    n, d = x2d.shape
    e = router_w.shape[1]
    t = min(n, 512)
    row4 = pl.BlockSpec((TOP_K, t), lambda i: (0, i))
    top_e, top_p, rank, counts = pl.pallas_call(
        _router_kernel, grid=(n // t,),
        in_specs=[pl.BlockSpec((t, d), lambda i: (i, 0)), pl.BlockSpec((e, d), lambda i: (0, 0)),
                  pl.BlockSpec((e, 1), lambda i: (0, 0))],
        out_specs=[row4, row4, row4, pl.BlockSpec((e, 1), lambda i: (0, 0))],
        out_shape=[jax.ShapeDtypeStruct((TOP_K, n), jnp.int32), jax.ShapeDtypeStruct((TOP_K, n), F32),
                   jax.ShapeDtypeStruct((TOP_K, n), jnp.int32), jax.ShapeDtypeStruct((e, 1), jnp.int32)],
        scratch_shapes=[pltpu.VMEM((e, 1), F32)],
        compiler_params=_params("arbitrary"), name="moe_router",
    )(x2d, router_w.T, router_b.reshape(e, 1))
    return top_e, top_p, rank, counts[:, 0]


def _expert_kernel(be_ref, valid_ref, xa_ref, xb_ref, wu_ref, bu_ref, wd_ref, bd_ref, oa_ref, ob_ref,
                   wu_bf, wd_bf):
    i = pl.program_id(0)
    f = wd_ref.shape[0]
    n_used = be_ref[pl.num_programs(0)]

    @pl.when((i == 0) | (be_ref[i] != be_ref[jnp.maximum(i - 1, 0)]))
    def _():
        wu_bf[...] = wu_ref[...].astype(BF16)
        wd_bf[...] = wd_ref[...].astype(BF16)

    @pl.when(i < n_used)
    def _():
        packed = jnp.concatenate([xa_ref[...], xb_ref[...]], axis=1)
        live = lax.broadcasted_iota(jnp.int32, packed.shape, 0) < valid_ref[i]
        x_lo, x_hi = _unpack_bf16_pairs(jnp.where(live, packed, jnp.uint32(0)))
        x = jnp.concatenate([x_lo.astype(BF16), x_hi.astype(BF16)], axis=1)
        hcat = _dot(x, wu_bf[...]) + bu_ref[...]
        glu = jnp.minimum(hcat[:, :f], SWIGLU_LIMIT)
        lin = jnp.clip(hcat[:, f:], -SWIGLU_LIMIT, SWIGLU_LIMIT)
        act = glu * _sigmoid(SWIGLU_ALPHA * glu) * (lin + 1.0)
        _store_word_tables((oa_ref, ob_ref), _pack_bf16_pairs(_dot(act.astype(BF16), wd_bf[...]) + bd_ref[...]))

    @pl.when(i >= n_used)
    def _():
        oa_ref[...] = jnp.zeros_like(oa_ref)
        ob_ref[...] = jnp.zeros_like(ob_ref)


def moe_experts(xa, xb, block_e, n_used, block_valid, w_up, b_up, w_down, b_down, layer):
    rows = xa.shape[0]
    _, e, d, f2 = w_up.shape
    f = f2 // 2
    nb = rows // EXPERT_BLOCK
    words = pl.BlockSpec((EXPERT_BLOCK, SC_ROW_WORDS), lambda i, be, nv: (i, 0))
    grid_spec = pltpu.PrefetchScalarGridSpec(
        num_scalar_prefetch=2, grid=(nb,),
        in_specs=[words, words,
                  pl.BlockSpec((None, None, d, f2), lambda i, be, nv: (layer, be[i], 0, 0)),
                  pl.BlockSpec((None, None, 1, f2), lambda i, be, nv: (layer, be[i], 0, 0)),
                  pl.BlockSpec((None, None, f, d), lambda i, be, nv: (layer, be[i], 0, 0)),
                  pl.BlockSpec((None, None, 1, d), lambda i, be, nv: (layer, be[i], 0, 0))],
        out_specs=[words, words],
        scratch_shapes=[pltpu.VMEM((d, f2), BF16), pltpu.VMEM((f, d), BF16)])
    depth = w_up.shape[0]
    return pl.pallas_call(
        _expert_kernel, grid_spec=grid_spec,
        out_shape=[jax.ShapeDtypeStruct((rows, SC_ROW_WORDS), jnp.uint32)] * 2,
        compiler_params=pltpu.CompilerParams(dimension_semantics=("arbitrary",), vmem_limit_bytes=EXPERT_VMEM_LIMIT),
        name="moe_experts",
    )(jnp.concatenate([block_e, n_used.reshape(1)]), block_valid, xa, xb, w_up, b_up.reshape(depth, e, 1, f2),
      w_down, b_down.reshape(depth, e, 1, d))


def _combine_kernel(x_ref, ya_ref, yb_ref, p_ref, lnw_ref, lnb_ref, o_ref, *, alpha):
    p = p_ref[...]
    moe = jnp.zeros(x_ref.shape, F32)
    for k in range(TOP_K):
        y_lo, y_hi = _unpack_bf16_pairs(jnp.concatenate([ya_ref[k], yb_ref[k]], axis=1))
        moe = moe + p[:, k:k + 1] * jnp.concatenate([y_lo, y_hi], axis=1)
    o_ref[...] = _layer_norm(alpha * x_ref[...] + moe, lnw_ref[...], lnb_ref[...])


def combine_ln(x2d, ya, yb, top_p, ln_w, ln_b, alpha):
    n, d = x2d.shape
    t = min(n, 512)
    kern = functools.partial(_combine_kernel, alpha=alpha)
    words = pl.BlockSpec((TOP_K, t, SC_ROW_WORDS), lambda i: (0, i, 0))
    return pl.pallas_call(
        kern, grid=(n // t,),
        in_specs=[pl.BlockSpec((t, d), lambda i: (i, 0)), words, words,
                  pl.BlockSpec((t, TOP_K), lambda i: (i, 0)),
                  pl.BlockSpec((1, d), lambda i: (0, 0)), pl.BlockSpec((1, d), lambda i: (0, 0))],
        out_specs=pl.BlockSpec((t, d), lambda i: (i, 0)),
        out_shape=jax.ShapeDtypeStruct((n, d), F32),
        compiler_params=_params("parallel"), name="moe_combine_ln",
    )(x2d, ya, yb, top_p, ln_w.reshape(1, d), ln_b.reshape(1, d))


def _sc_mesh():
    return plsc.VectorSubcoreMesh(core_axis_name="core", subcore_axis_name="subcore")


def sc_gather_rows(tables, idx):
    r = idx.shape[0]
    nt = len(tables)
    out = jax.ShapeDtypeStruct((r, SC_ROW_WORDS), tables[0].dtype)

    @pl.kernel(out_type=(out,) * nt, mesh=_sc_mesh(), name="sc_gather_rows")
    def gather(*refs):
        x_hbm, i_hbm, o_hbm = refs[:nt], refs[nt], refs[nt + 1:]
        for j in range(nt):
            def body(i_vmem, o_vmem, table=x_hbm[j]):
                pltpu.sync_copy(table.at[i_vmem.at[0]], o_vmem)

            pltpu.emit_pipeline(
                body, grid=(r // SC_WINDOW,),
                in_specs=[pl.BlockSpec((1, SC_WINDOW), lambda i: (0, i))],
                out_specs=[pl.BlockSpec((SC_WINDOW, SC_ROW_WORDS), lambda i: (i, 0))],
                core_axis_name=("core", "subcore"), dimension_semantics=(pltpu.PARALLEL,),
            )(i_hbm, o_hbm[j])

    return gather(*tables, idx.reshape(1, r))


def sc_scatter_rows(tables, dest_t, n_rows):
    n = tables[0].shape[0]
    nt = len(tables)
    copies = dest_t.shape[0]
    out = jax.ShapeDtypeStruct((n_rows, SC_ROW_WORDS), tables[0].dtype)

    @pl.kernel(out_type=(out,) * nt, mesh=_sc_mesh(), scratch_types=[], name="sc_scatter_rows")
    def scatter(*refs):
        x_hbm, i_hbm, o_hbm = refs[:nt], refs[nt], refs[nt + 1:]
        for j in range(nt):
            def body(x_vmem, i_vmem, out_j=o_hbm[j]):
                for k in range(copies):
                    pltpu.sync_copy(x_vmem, out_j.at[i_vmem.at[k]])

            pltpu.emit_pipeline(
                body, grid=(n // SC_WINDOW,),
                in_specs=[pl.BlockSpec((SC_WINDOW, SC_ROW_WORDS), lambda i: (i, 0)),
                          pl.BlockSpec((copies, SC_WINDOW), lambda i: (0, i))],
                out_specs=[],
                core_axis_name=("core", "subcore"), dimension_semantics=(pltpu.PARALLEL,),
            )(x_hbm[j], i_hbm)

    return scatter(*tables, dest_t)


def moe_ffn_ln(x_f32, x_packed, router_w, router_b, w_up, b_up, w_down, b_down, layer, ln_w, ln_b, alpha):
    n, d = x_f32.shape
    top_e, top_p, rank, counts = moe_router(x_f32, router_w, router_b)
    padded = (counts + EXPERT_BLOCK - 1) // EXPERT_BLOCK * EXPERT_BLOCK
    pad_end = jnp.cumsum(padded)
    pad_start = pad_end - padded
    n_assign = n * TOP_K
    n_blocks = -(-(n_assign + N_EXPERTS * (EXPERT_BLOCK - 1)) // EXPERT_BLOCK)
    experts = jnp.arange(N_EXPERTS, dtype=jnp.int32)
    dest_t = rank + jnp.sum(jnp.where(top_e[..., None] == experts, pad_start.astype(jnp.int32), 0), axis=-1)
    block_first = jnp.arange(n_blocks, dtype=jnp.int32) * EXPERT_BLOCK
    block_e = jnp.clip(jnp.sum((pad_end[None, :] <= block_first[:, None]).astype(jnp.int32), axis=1),
                       0, N_EXPERTS - 1)
    block_valid = jnp.clip(counts[block_e] - (block_first - pad_start[block_e]), 0, EXPERT_BLOCK)
    n_used = (pad_end[-1] // EXPERT_BLOCK).astype(jnp.int32)
    xa, xb = sc_scatter_rows(x_packed, dest_t, n_blocks * EXPERT_BLOCK)
    ya, yb = moe_experts(xa, xb, block_e, n_used, block_valid.astype(jnp.int32), w_up, b_up, w_down, b_down,
                         layer)
    ya, yb = sc_gather_rows((ya, yb), dest_t.reshape(-1))
    return combine_ln(x_f32, ya.reshape(TOP_K, n, SC_ROW_WORDS), yb.reshape(TOP_K, n, SC_ROW_WORDS), top_p.T,
                      ln_w, ln_b, alpha)


def kernel(x, positions, w_in, hg_lower_bounds, hg_norm_w, gm_ln_w, gm_ln_b, gm_spatial_w, gm_spatial_b, gm_norm_w, nsa_cmp_pe, nsa_cmp_w1, nsa_cmp_w2, nsa_norm_w, w_out, ln1_w, ln1_b, router_w, router_b, exp_w_up, exp_b_up, exp_w_down, exp_b_down, ln2_w, ln2_b):
    batch, seq, d = x.shape
    depth = w_in.shape[0]
    n = batch * seq
    alpha = (2 * depth) ** 0.25
    hg_w = hg_norm_w.shape[1]
    gm_w = gm_norm_w.shape[1]
    nsa_w = nsa_norm_w.shape[1]
    kv_w = NSA_KV_GROUPS * HEAD_DIM
    in_width = w_in.shape[2]
    off_gm = 4 * hg_w
    off_q = off_gm + 2 * gm_w
    off_kv = off_q + nsa_w
    off_gate = off_kv + 6 * kv_w
    width_pad = -(-in_width // LANES) * LANES

    cosf, sinf, cos_t, sin_t = rope_tables(positions)
    lb_all = jnp.cumsum(jax.nn.softmax(hg_lower_bounds.astype(F32), axis=0), axis=0)
    lb_all = lb_all - lb_all[0:1]

    x2d = x.reshape(n, d)
    for l in range(depth):
        w_l = jnp.pad(w_in[l], ((0, 0), (0, width_pad - in_width))).astype(BF16)
        h = in_proj(x2d, w_l)
        h3 = h.reshape(batch, seq, width_pad)
        y_hg = hgrn2(h3, lb_all[l], hg_norm_w[l]).reshape(n, hg_w)
        y_gm = gmlp(h, gm_ln_w[l], gm_ln_b[l], gm_spatial_w[l], gm_spatial_b[l], gm_norm_w[l],
                    off_gm // gm_w, off_gm // gm_w + 1)
        kc = compress(h, off_kv // kv_w, nsa_cmp_pe[l, 0], nsa_cmp_w1[l, 0], nsa_cmp_w2[l, 0], batch, seq)
        vc = compress(h, off_kv // kv_w + 1, nsa_cmp_pe[l, 1], nsa_cmp_w1[l, 1], nsa_cmp_w2[l, 1], batch, seq)
        n_cmp = kc.shape[1]
        kc = kc.reshape(batch, n_cmp, NSA_KV_GROUPS, HEAD_DIM).transpose(0, 2, 1, 3)
        vc_t = vc.reshape(batch, n_cmp, NSA_KV_GROUPS, HEAD_DIM).transpose(0, 2, 3, 1)
        ksa, vs_t, kw, vw_t = nsa_kprep(h, cosf, sinf, batch, seq, (off_kv + 2 * kv_w) // LANES)
        y_nsa = nsa_attention(h, cos_t, sin_t, kc, vc_t, ksa, vs_t, kw, vw_t, nsa_norm_w[l], batch, seq,
                              off_q // nsa_w, off_gate // LANES)
        x1, x1a, x1b = out_proj_ln(x2d, y_hg, y_gm, y_nsa, w_out[l], ln1_w[l], ln1_b[l], alpha)
        x2d = moe_ffn_ln(x1, (x1a, x1b), router_w[l], router_b[l], exp_w_up, exp_b_up, exp_w_down, exp_b_down, l,
                         ln2_w[l], ln2_b[l], alpha)
    return x2d.reshape(batch, seq, d)
```

```python
import functools
import math

import numpy as np
import jax
import jax.numpy as jnp
from jax import lax
from jax.experimental import pallas as pl
from jax.experimental.pallas import tpu as pltpu
from jax.experimental.pallas import tpu_sc as plsc

F32 = jnp.float32
BF16 = jnp.bfloat16
HIGHEST = lax.Precision.HIGHEST

HEAD_DIM = 64
LANES = 128
VMEM_LIMIT = 48 * 1024 * 1024
EXPERT_VMEM_LIMIT = 56 * 1024 * 1024

HG_CHUNK = 64
GM_CHUNK = 128
GM_TILE_CHUNKS = 4
NSA_KV_GROUPS = 2
NSA_HPG = 4
CMP_BLOCK = 32
CMP_STRIDE = 16
CMP_HIDDEN = 128
SEL_BLOCK = 64
N_SEL = 16
WINDOW = 512
N_GATES = 3
IMP_FORCE = 1e9
FORCE_KEY = int(np.float32(IMP_FORCE).view(np.int32))
NEG_INF = -1e30
N_EXPERTS = 32
TOP_K = 4
SWIGLU_ALPHA = 1.702
SWIGLU_LIMIT = 7.0
EXPERT_BLOCK = 512
SC_ROW_WORDS = 256
SC_WINDOW = 128
ROPE_THETA = 10000.0
LOG2_E = 1.4426950408889634
LN_EPS = 1e-5
RMS_EPS = 1e-6
V_ROWS = HEAD_DIM + 16


def _params(*sem):
    return pltpu.CompilerParams(dimension_semantics=sem, vmem_limit_bytes=VMEM_LIMIT)


def _dot(a, b):
    return jnp.dot(a, b, preferred_element_type=F32)


def _dot_nt(a, b, precision=None):
    return lax.dot_general(a, b, (((1,), (1,)), ((), ())), precision=precision,
                           preferred_element_type=F32)


def _dot_tn(a, b):
    return lax.dot_general(a, b, (((0,), (0,)), ((), ())), preferred_element_type=F32)


def _sigmoid(x):
    return 1.0 / (1.0 + jnp.exp(-x))


def _gelu(x):
    return 0.5 * x * (1.0 + jnp.tanh(0.7978845608028654 * (x + 0.044715 * x * x * x)))


def _layer_norm(x, w, b):
    mu = jnp.mean(x, axis=-1, keepdims=True)
    xc = x - mu
    var = jnp.mean(xc * xc, axis=-1, keepdims=True)
    return xc * lax.rsqrt(var + LN_EPS) * w + b


def _pack_bf16_pairs(y):
    w = y.shape[1] // 2
    bits = pltpu.bitcast(y.astype(BF16).astype(F32), jnp.uint32)
    return lax.shift_right_logical(bits[:, :w], jnp.uint32(16)) | (bits[:, w:] & jnp.uint32(0xFFFF0000))


def _unpack_bf16_pairs(u):
    lo = pltpu.bitcast(lax.shift_left(u, jnp.uint32(16)), F32)
    hi = pltpu.bitcast(u & jnp.uint32(0xFFFF0000), F32)
    return lo, hi


def _store_word_tables(refs, packed):
    for j, ref in enumerate(refs):
        ref[...] = packed[:, j * SC_ROW_WORDS:(j + 1) * SC_ROW_WORDS]


def _head_mean_sq(o, bd_ones):
    sq = o * o
    hi = sq.astype(BF16)
    lo = (sq - hi.astype(F32)).astype(BF16)
    ones = bd_ones.astype(BF16)
    return (_dot(hi, ones) + _dot(lo, ones)) * (1.0 / HEAD_DIM)


def _rope_kernel(pos_ref, inv_ref, cos_ref, sin_ref, cost_ref, sint_ref):
    ang = inv_ref[...] * pos_ref[...]
    c = jnp.cos(ang)
    s = jnp.sin(ang)
    cos_t = jnp.concatenate([c, c], axis=0)
    sin_t = jnp.concatenate([-s, s], axis=0)
    cost_ref[...] = cos_t
    sint_ref[...] = sin_t
    cos_ref[...] = cos_t.T
    sin_ref[...] = sin_t.T


def rope_tables(positions):
    n = positions.size
    tile = min(n, 2048)
    posf = positions.reshape(1, n).astype(F32)
    inv = ROPE_THETA ** (-jnp.arange(0, HEAD_DIM, 2, dtype=F32) / HEAD_DIM)
    row = pl.BlockSpec((tile, HEAD_DIM), lambda i: (i, 0))
    rowt = pl.BlockSpec((HEAD_DIM, tile), lambda i: (0, i))
    return pl.pallas_call(
        _rope_kernel, grid=(n // tile,),
        in_specs=[pl.BlockSpec((1, tile), lambda i: (0, i)), pl.BlockSpec((HEAD_DIM // 2, 1), lambda i: (0, 0))],
        out_specs=[row, row, rowt, rowt],
        out_shape=[jax.ShapeDtypeStruct((n, HEAD_DIM), F32)] * 2 + [jax.ShapeDtypeStruct((HEAD_DIM, n), F32)] * 2,
        compiler_params=_params("parallel"), name="rope_tables",
    )(posf, inv.reshape(HEAD_DIM // 2, 1))


def _in_proj_kernel(x_ref, w_ref, h_ref):
    h_ref[...] = _dot(x_ref[...].astype(BF16), w_ref[...])


def in_proj(x2d, w_bf16):
    n, d = x2d.shape
    width = w_bf16.shape[1]
    tile = min(n, 512)
    return pl.pallas_call(
        _in_proj_kernel, grid=(n // tile,),
        in_specs=[pl.BlockSpec((tile, d), lambda i: (i, 0)), pl.BlockSpec((d, width), lambda i: (0, 0))],
        out_specs=pl.BlockSpec((tile, width), lambda i: (i, 0)),
        out_shape=jax.ShapeDtypeStruct((n, width), F32),
        compiler_params=_params("parallel"), name="in_proj")(x2d, w_bf16)


HG_LEVELS = (64, 32, 16, 8, 4, 2)
HG_BATCH = 8


def _hgrn_constants():
    c = HG_CHUNK
    t = np.arange(c)
    u = t[None, :]
    rows = [u <= t[:, None], u > t[:, None]]
    masks = [np.eye(c, dtype=bool)]
    for m in HG_LEVELS:
        ref = ((t // m) * m + m // 2 - 1)[:, None]
        second = (t % m >= m // 2)[:, None]
        rows.append(((u > ref) & (u <= t[:, None]) & second) | ((u > t[:, None]) & (u <= ref) & ~second))
        masks.append((t[:, None] // m == t[None, :] // m) & second & (t[None, :] % m < m // 2))
    pmat = np.concatenate(rows, axis=0).astype(np.float32)
    masks = np.stack([np.tile(mk, (1, 4)) for mk in masks]).astype(np.float32)
    return pmat, masks


def _hgrn_kernel(q_ref, f_ref, i_ref, g_ref, lb_ref, nw_ref, pmat_ref, masks_ref, bd_ref, hm_ref,
                 o_ref, state_ref):
    c = HG_CHUNK

    @pl.when(pl.program_id(1) == 0)
    def _():
        state_ref[...] = jnp.zeros_like(state_ref)

    lb = lb_ref[...]
    bd = bd_ref[...]
    hm = hm_ref[...]
    hm_tiles = [jnp.broadcast_to(hm[h:h + 1], (c, hm.shape[1])).astype(BF16) for h in range(4)]
    pmat = pmat_ref[...]
    a = jnp.log(lb)
    log1m = jnp.log(1.0 - lb)
    nb, _, w = q_ref.shape
    seqs = range(nb)

    fr = f_ref[...].reshape(nb * c, w)
    hq = q_ref[...].reshape(nb * c, w)
    v = i_ref[...].reshape(nb * c, w)
    qf = hq * _sigmoid(hq)
    log_sig = jnp.minimum(fr, 0.0) - jnp.log(1.0 + jnp.exp(-jnp.abs(fr)))
    cc = log1m + log_sig
    log_f = jnp.maximum(a, cc) + jnp.log(1.0 + jnp.exp(-jnp.abs(a - cc)))
    kk = (1.0 - lb) * _sigmoid(-fr)

    hi = log_f.astype(BF16)
    lo = (log_f - hi.astype(F32)).astype(BF16)

    def side_by_side(x):
        return jnp.concatenate([x[b * c:(b + 1) * c] for b in seqs], axis=1)

    sums = jnp.minimum(_dot(pmat, side_by_side(hi)) + _dot(pmat, side_by_side(lo)), 0.0)
    e_all = jnp.exp(sums)

    def e_rows(r, b):
        return e_all[r * c:(r + 1) * c, b * w:(b + 1) * w]

    def stacked(x):
        xb = x.astype(BF16)
        return jnp.concatenate([xb * hm_tiles[h] for h in range(4)], axis=0)

    q_s = [qf[b * c:(b + 1) * c] for b in seqs]
    k_s = [kk[b * c:(b + 1) * c] for b in seqs]
    v_s = [v[b * c:(b + 1) * c] for b in seqs]

    att = [masks_ref[0] * _dot_nt(q_s[b].astype(BF16), stacked(k_s[b])) for b in seqs]
    for li in range(len(HG_LEVELS)):
        for b in seqs:
            e_l = e_rows(2 + li, b)
            att[b] = att[b] + masks_ref[li + 1] * _dot_nt((q_s[b] * e_l).astype(BF16), stacked(k_s[b] * e_l))

    outs = []
    for b in seqs:
        e_b = e_rows(0, b)
        st = state_ref[b]
        o = _dot(att[b].astype(BF16), stacked(v_s[b])) + _dot_nt((q_s[b] * e_b).astype(BF16), st.astype(BF16))
        k_rest = (k_s[b] * e_rows(1, b)).astype(BF16)
        state_ref[b] = st * e_b[c - 1:c] + bd * _dot_tn(v_s[b].astype(BF16), k_rest)
        outs.append(o)

    o = jnp.concatenate(outs, axis=0)
    ms = _head_mean_sq(o, bd)
    y = o * lax.rsqrt(ms + RMS_EPS) * nw_ref[...] * _sigmoid(g_ref[...].reshape(nb * c, w))
    o_ref[...] = y.astype(o_ref.dtype).reshape(nb, c, w)


def hgrn2(h3, lb, norm_w):
    batch, seq, _ = h3.shape
    w = lb.shape[-1]
    c = HG_CHUNK
    nb = math.gcd(batch, HG_BATCH)
    pmat, masks = _hgrn_constants()
    lane_head = np.arange(w) // HEAD_DIM
    bd = (lane_head[:, None] == lane_head[None, :]).astype(np.float32)
    hm = (np.arange(4)[:, None] == lane_head[None, :]).astype(np.float32)

    def col(j):
        return pl.BlockSpec((nb, c, w), lambda b, i, j=j: (b, i, j))

    def const(shape):
        return pl.BlockSpec(shape, lambda b, i: (0,) * len(shape))

    return pl.pallas_call(
        _hgrn_kernel, grid=(batch // nb, seq // c),
        in_specs=[col(0), col(1), col(2), col(3), const((1, w)), const((1, w)),
                  const(pmat.shape), const(masks.shape), const(bd.shape), const(hm.shape)],
        out_specs=pl.BlockSpec((nb, c, w), lambda b, i: (b, i, 0)),
        out_shape=jax.ShapeDtypeStruct((batch, seq, w), BF16),
        scratch_shapes=[pltpu.VMEM((nb, w, w), F32)],
        compiler_params=_params("parallel", "arbitrary"), name="hgrn2",
    )(h3, h3, h3, h3, lb.reshape(1, w), norm_w.reshape(1, w), jnp.asarray(pmat, BF16), jnp.asarray(masks),
      jnp.asarray(bd), jnp.asarray(hm))


def _gmlp_kernel(u_ref, v_ref, lnw_ref, lnb_ref, ws_ref, bias_ref, nw_ref, bd_ref, hm_ref, o_ref):
    c = GM_CHUNK
    groups = ws_ref.shape[0]
    u = _gelu(u_ref[...])
    v = _layer_norm(_gelu(v_ref[...]), lnw_ref[...], lnb_ref[...])
    hm = hm_ref[...]
    bd = bd_ref[...]
    causal = lax.broadcasted_iota(jnp.int32, (c, c), 0) >= lax.broadcasted_iota(jnp.int32, (c, c), 1)
    w_cat = jnp.concatenate([jnp.where(causal, ws_ref[g], 0.0).astype(BF16) for g in range(groups)], axis=1)
    for j in range(u.shape[0] // c):
        rows = slice(j * c, (j + 1) * c)
        v_j = v[rows]
        v_bd = jnp.concatenate([v_j * hm[g:g + 1] for g in range(groups)], axis=0).astype(BF16)
        y = u[rows] * (bias_ref[...] + _dot(w_cat, v_bd))
        ms = _head_mean_sq(y, bd)
        o_ref[rows, :] = (y * lax.rsqrt(ms + RMS_EPS) * nw_ref[...]).astype(o_ref.dtype)


def gmlp(h, ln_w, ln_b, w_s, b_s, norm_w, u_col, v_col):
    n = h.shape[0]
    groups, c, _ = w_s.shape
    w = groups * HEAD_DIM
    lane_head = np.arange(w) // HEAD_DIM
    bd = (lane_head[:, None] == lane_head[None, :]).astype(np.float32)
    hm = (np.arange(groups)[:, None] == lane_head[None, :]).astype(np.float32)
    bias = jnp.repeat(b_s.T, HEAD_DIM, axis=1)

    def const(shape):
        return pl.BlockSpec(shape, lambda i: (0,) * len(shape))

    t = math.gcd(n, GM_TILE_CHUNKS * c)
    return pl.pallas_call(
        _gmlp_kernel, grid=(n // t,),
        in_specs=[pl.BlockSpec((t, w), lambda i: (i, u_col)), pl.BlockSpec((t, w), lambda i: (i, v_col)),
                  const((1, w)), const((1, w)), const(w_s.shape), const((c, w)), const((1, w)),
                  const(bd.shape), const(hm.shape)],
        out_specs=pl.BlockSpec((t, w), lambda i: (i, 0)),
        out_shape=jax.ShapeDtypeStruct((n, w), BF16),
        compiler_params=_params("parallel"), name="gmlp",
    )(h, h, ln_w.reshape(1, w), ln_b.reshape(1, w), w_s, bias, norm_w.reshape(1, w),
      jnp.asarray(bd), jnp.asarray(hm))


def _compress_kernel(kv_ref, wtop_ref, wbot_ref, pe_ref, w2_ref, o_ref, *, transposed):
    units = kv_ref.shape[0] // CMP_STRIDE
    pe = pe_ref[...].astype(BF16)
    p = jnp.zeros((units, wtop_ref.shape[2]), F32)
    q = jnp.zeros_like(p)
    const = jnp.zeros((1, wtop_ref.shape[2]), F32)
    for j in range(CMP_STRIDE):
        rows = kv_ref[pl.ds(j, units, stride=CMP_STRIDE), :].astype(BF16)
        p = p + _dot(rows, wtop_ref[j])
        q = q + _dot(rows, wbot_ref[j])
        const = const + _dot(pe[0, j:j + 1], wtop_ref[j]) + _dot(pe[1, j:j + 1], wbot_ref[j])
    q_next = jnp.concatenate([q[1:], jnp.zeros_like(q[0:1])], axis=0)
    hid = _gelu(p + q_next + const)
    y = _dot(hid.astype(BF16), w2_ref[...])
    if transposed:
        y = y.T
    for g in range(o_ref.shape[0]):
        sl = slice(g * HEAD_DIM, (g + 1) * HEAD_DIM)
        o_ref[g] = (y[sl] if transposed else y[:, sl]).astype(o_ref.dtype)


def compress(h, col, pe, w1, w2, batch, seq, transposed):
    g = NSA_KV_GROUPS
    half = CMP_STRIDE
    units = seq // half
    gw = g * HEAD_DIM
    eye = jnp.eye(g, dtype=F32)
    w1r = w1.reshape(2, half, HEAD_DIM, CMP_HIDDEN)
    wbd = jnp.einsum('hjdn,gk->hjgdkn', w1r, eye).reshape(2, half, gw, g * CMP_HIDDEN).astype(BF16)
    w2bd = jnp.einsum('nd,gk->gnkd', w2, eye).reshape(g * CMP_HIDDEN, gw).astype(BF16)
    pe2 = jnp.broadcast_to(pe.reshape(2, half, 1, HEAD_DIM), (2, half, g, HEAD_DIM)).reshape(2, half, gw)

    def const(shape):
        return pl.BlockSpec(shape, lambda b: (0,) * len(shape))

    per_group = (HEAD_DIM, units) if transposed else (units, HEAD_DIM)
    return pl.pallas_call(
        functools.partial(_compress_kernel, transposed=transposed), grid=(batch,),
        in_specs=[pl.BlockSpec((seq, gw), lambda b: (b, col)),
                  const(wbd.shape[1:]), const(wbd.shape[1:]), const(pe2.shape), const(w2bd.shape)],
        out_specs=pl.BlockSpec((None, g) + per_group, lambda b: (b, 0, 0, 0)),
        out_shape=jax.ShapeDtypeStruct((batch, g) + per_group, BF16),
        compiler_params=_params("parallel"), name="nsa_compress",
    )(h, wbd[0], wbd[1], pe2, w2bd)


def _rot_half_pairs(x):
    lane = lax.broadcasted_iota(jnp.int32, x.shape, 1)
    fwd = pltpu.roll(x, 32, axis=1)
    bwd = pltpu.roll(x, 96, axis=1)
    return jnp.where((lane % HEAD_DIM) < HEAD_DIM // 2, bwd, fwd)


def _kprep_kernel(ks_ref, vs_ref, kw_ref, vw_ref, cos_ref, sin_ref, ksa_ref, vso_ref, kwo_ref, vwo_ref):
    t = ks_ref.shape[0]
    cos = cos_ref[...]
    sin = sin_ref[...]
    cos2 = jnp.concatenate([cos, cos], axis=1)
    sin2 = jnp.concatenate([sin, sin], axis=1)
    ks = ks_ref[...]
    kw = kw_ref[...]
    ks_r = ks * cos2 + _rot_half_pairs(ks) * sin2
    kw_r = kw * cos2 + _rot_half_pairs(kw) * sin2
    pos = pl.program_id(1) * t + lax.broadcasted_iota(jnp.int32, (t, HEAD_DIM), 0)
    onehot = (pos // SEL_BLOCK == lax.broadcasted_iota(jnp.int32, (t, HEAD_DIM), 1)).astype(F32)
    vs_t = vs_ref[...].T
    vw_t = vw_ref[...].T
    tail = (lax.broadcasted_iota(jnp.int32, (V_ROWS - HEAD_DIM, t), 0) == 0).astype(F32)
    for g in range(NSA_KV_GROUPS):
        sl = slice(g * HEAD_DIM, (g + 1) * HEAD_DIM)
        ksa_ref[g] = jnp.concatenate([ks_r[:, sl], onehot], axis=1).astype(BF16)
        vso_ref[g] = jnp.concatenate([vs_t[sl], tail], axis=0).astype(BF16)
        kwo_ref[g] = kw_r[:, sl].astype(BF16)
        vwo_ref[g] = jnp.concatenate([vw_t[sl], tail], axis=0).astype(BF16)


def nsa_kprep(h, cosf, sinf, batch, seq, col0):
    g = NSA_KV_GROUPS
    t = min(seq, 512)
    nt = seq // t

    def col(j):
        return pl.BlockSpec((t, LANES), lambda b, i, j=j: (b * nt + i, col0 + j))

    tab = pl.BlockSpec((t, HEAD_DIM), lambda b, i: (b * nt + i, 0))

    def out(wd):
        return pl.BlockSpec((None, g, t, wd), lambda b, i: (b, 0, i, 0))

    out_t = pl.BlockSpec((None, g, V_ROWS, t), lambda b, i: (b, 0, 0, i))
    k_shape = jax.ShapeDtypeStruct((batch, g, seq, HEAD_DIM), BF16)
    v_shape = jax.ShapeDtypeStruct((batch, g, V_ROWS, seq), BF16)
    return pl.pallas_call(
        _kprep_kernel, grid=(batch, nt),
        in_specs=[col(0), col(1), col(2), col(3), tab, tab],
        out_specs=[out(2 * HEAD_DIM), out_t, out(HEAD_DIM), out_t],
        out_shape=[jax.ShapeDtypeStruct((batch, g, seq, 2 * HEAD_DIM), BF16), v_shape, k_shape, v_shape],
        compiler_params=_params("parallel", "parallel"), name="nsa_kprep",
    )(h, h, h, h, cosf, sinf)


def _nsa_kernel(hq_ref, gate_ref, cos_ref, sin_ref, kc_ref, vc_ref, ksa_ref, vs_ref, kw_ref, vw_ref,
                ovl_ref, nw_ref, o_ref, *, tq, tk, n_sb):
    qi = pl.program_id(1)
    hpg = NSA_HPG
    groups = NSA_KV_GROUPS
    rows = hpg * tq
    t0 = qi * tq
    scale = 1.0 / math.sqrt(HEAD_DIM)
    half = HEAD_DIM // 2

    hq_t = hq_ref[...].T
    cos = cos_ref[...]
    sin = sin_ref[...]
    q_raw, q_rot = [], []
    for g in range(groups):
        raw_g, rot_g = [], []
        for h in range(hpg):
            r0 = (g * hpg + h) * HEAD_DIM
            qh = hq_t[r0:r0 + HEAD_DIM]
            swapped = jnp.concatenate([qh[half:], qh[:half]], axis=0)
            raw_g.append(qh * (scale * LOG2_E))
            rot_g.append((qh * cos + swapped * sin) * (scale * LOG2_E))
        q_raw.append(jnp.concatenate(raw_g, axis=1).astype(BF16))
        q_rot.append(jnp.concatenate(rot_g, axis=1))

    tpos = t0 + lax.broadcasted_iota(jnp.int32, (1, tq), 1)
    tpos_r = jnp.concatenate([tpos] * hpg, axis=1)

    def flash_steps(s, v_t, carry):
        m_new = [jnp.maximum(carry[g][0], jnp.max(s[g], axis=0, keepdims=True)) for g in range(groups)]
        p = [jnp.exp2(s[g] - m_new[g]).astype(BF16) for g in range(groups)]
        return tuple((m_new[g], jnp.exp2(carry[g][0] - m_new[g]) * carry[g][1] + _dot(v_t[g], p[g]))
                     for g in range(groups))

    init = ((jnp.full((1, rows), NEG_INF, F32), jnp.zeros((V_ROWS, rows), F32)),) * groups

    wk = WINDOW + tq
    kw0 = pl.multiple_of(jnp.maximum(t0 - WINDOW, 0), tq)
    kpos_w = kw0 + lax.broadcasted_iota(jnp.int32, (wk, 1), 0)
    mask_w = (kpos_w <= tpos_r) & (kpos_w > tpos_r - WINDOW)
    n_pad = jnp.maximum(WINDOW - 1 - tpos_r, 0).astype(F32)
    n_cmp = kc_ref.shape[1]
    cmp_end = lax.broadcasted_iota(jnp.int32, (n_cmp, 1), 0) * CMP_STRIDE + (CMP_BLOCK - 1)
    mask_c = cmp_end <= tpos_r
    blk = lax.broadcasted_iota(jnp.int32, (n_sb, tq), 0)
    cur = (t0 + lax.broadcasted_iota(jnp.int32, (n_sb, tq), 1)) // SEL_BLOCK
    s_w = [jnp.where(mask_w, _dot(kw_ref[g, pl.ds(kw0, wk), :], q_rot[g].astype(BF16)), NEG_INF)
           for g in range(groups)]
    s_c = [jnp.where(mask_c, _dot(kc_ref[g], q_raw[g]), NEG_INF) for g in range(groups)]
    win = flash_steps(s_w, [vw_ref[g, :, pl.ds(kw0, wk)] for g in range(groups)], init)
    o_w, o_c, imps = [], [], []
    for g in range(groups):
        m_w, acc_w = win[g]
        m_f = jnp.where(n_pad > 0.0, jnp.maximum(m_w, 0.0), m_w)
        a_w = jnp.exp2(m_w - m_f)
        o_w.append(acc_w[0:HEAD_DIM] * (a_w / (acc_w[HEAD_DIM:HEAD_DIM + 1] * a_w + n_pad * jnp.exp2(-m_f))))

    for g in range(groups):
        e_c = jnp.exp2(s_c[g] - jnp.max(s_c[g], axis=0, keepdims=True))
        p_c = jnp.where(mask_c, e_c * (1.0 / jnp.sum(e_c, axis=0, keepdims=True)), 0.0)
        o_c.append(_dot(vc_ref[g], p_c.astype(BF16)))

        p_sum = p_c[:, 0:tq]
        for h in range(1, hpg):
            p_sum = p_sum + p_c[:, h * tq:(h + 1) * tq]
        imp = jnp.dot(ovl_ref[...], p_sum, precision=HIGHEST, preferred_element_type=F32)
        key = pltpu.bitcast(jnp.maximum(imp, 0.0), jnp.int32)
        key = jnp.where((blk == 0) | (blk == cur) | (blk == cur - 1), FORCE_KEY, key)
        imps.append(jnp.where(blk > cur, -1, key))

    n_sel = min(N_SEL, n_sb)

    def bit_body(it, taus):
        bit = lax.shift_left(jnp.int32(1), 30 - it)
        out = []
        for g in range(groups):
            cand = taus[g] | bit
            cnt = jnp.sum(jnp.where(imps[g] >= cand, 1, 0), axis=0, keepdims=True)
            out.append(jnp.where(cnt >= n_sel, cand, taus[g]))
        return tuple(out)

    taus = lax.fori_loop(0, 31, bit_body, (jnp.zeros((1, tq), jnp.int32),) * groups)
    lower = (lax.broadcasted_iota(jnp.int32, (n_sb, n_sb), 0)
             > lax.broadcasted_iota(jnp.int32, (n_sb, n_sb), 1)).astype(BF16)
    q_aug = []
    for g in range(groups):
        above = imps[g] > taus[g]
        equal = imps[g] == taus[g]
        need = n_sel - jnp.sum(jnp.where(above, 1, 0), axis=0, keepdims=True)
        earlier = _dot(lower, jnp.where(equal, 1.0, 0.0).astype(BF16))
        selected = above | (equal & (earlier < need.astype(F32)))
        sel_bias = jnp.where(selected, 0.0, NEG_INF)
        if n_sb < HEAD_DIM:
            sel_bias = jnp.concatenate([sel_bias, jnp.zeros((HEAD_DIM - n_sb, tq), F32)], axis=0)
        q_aug.append(jnp.concatenate([q_rot[g], jnp.concatenate([sel_bias] * hpg, axis=1)], axis=0).astype(BF16))

    def scores(k0, width):
        return [_dot(ksa_ref[g, pl.ds(k0, width), :], q_aug[g]) for g in range(groups)]

    def values(k0, width):
        return [vs_ref[g, :, pl.ds(k0, width)] for g in range(groups)]

    def wide_body(kt, carry):
        k0 = pl.multiple_of(kt * tk, tk)
        return flash_steps(scores(k0, tk), values(k0, tk), carry)

    n_wide = t0 // tk
    carry = lax.fori_loop(0, n_wide, wide_body, init)

    def narrow_body(j, carry):
        k0 = pl.multiple_of(n_wide * tk + j * tq, tq)
        return flash_steps(scores(k0, tq), values(k0, tq), carry)

    carry = lax.fori_loop(0, (t0 - n_wide * tk) // tq, narrow_body, carry)
    k0 = pl.multiple_of(t0, tq)
    mask_s = k0 + lax.broadcasted_iota(jnp.int32, (tq, 1), 0) <= tpos_r
    carry = flash_steps([jnp.where(mask_s, s, NEG_INF) for s in scores(k0, tq)], values(k0, tq), carry)

    gates = _sigmoid(gate_ref[...].T)
    nw = nw_ref[...]
    outs = []
    for g in range(groups):
        acc_s = carry[g][1]
        o_s = acc_s[0:HEAD_DIM] * (1.0 / acc_s[HEAD_DIM:HEAD_DIM + 1])
        for h in range(hpg):
            sl = slice(h * tq, (h + 1) * tq)
            r = (g * hpg + h) * N_GATES
            o = gates[r:r + 1] * o_c[g][:, sl] + gates[r + 1:r + 2] * o_s[:, sl] + gates[r + 2:r + 3] * o_w[g][:, sl]
            ms = jnp.mean(o * o, axis=0, keepdims=True)
            outs.append(o * lax.rsqrt(ms + RMS_EPS) * nw[:, g * hpg + h:g * hpg + h + 1])
    o_ref[...] = jnp.concatenate(outs, axis=0).T.astype(o_ref.dtype)


def nsa_attention(h, cos_t, sin_t, kc, vc_t, ksa, vs_t, kw, vw_t, norm_w, batch, seq, q_col0, gate_col):
    g, hpg = NSA_KV_GROUPS, NSA_HPG
    tq = min(seq, 256)
    nq = seq // tq
    n_sb = seq // SEL_BLOCK
    assert n_sb <= HEAD_DIM, "selection-block one-hot shares the 64 spare key lanes"
    n_cmp = kc.shape[2]
    units = np.arange(n_cmp)[:, None] + np.arange(CMP_BLOCK // CMP_STRIDE)[None, :]
    ovl = np.zeros((n_cmp, n_sb), np.float32)
    for c in range((seq - CMP_BLOCK) // CMP_STRIDE + 1):
        for u in units[c]:
            ovl[c, u // (SEL_BLOCK // CMP_STRIDE)] += 1.0
    ovl_t = jnp.asarray(ovl.T)

    def per_b(shape):
        return pl.BlockSpec((None, g) + shape, lambda b, qi: (b, 0, 0, 0))

    width = g * hpg * HEAD_DIM
    tab = pl.BlockSpec((HEAD_DIM, tq), lambda b, qi: (0, b * nq + qi))
    tk = min(seq, 512)
    assert seq >= WINDOW + tq and seq % tk == 0 and tk % tq == 0
    kern = functools.partial(_nsa_kernel, tq=tq, tk=tk, n_sb=n_sb)
    return pl.pallas_call(
        kern, grid=(batch, nq),
        in_specs=[pl.BlockSpec((tq, width), lambda b, qi: (b * nq + qi, q_col0)),
                  pl.BlockSpec((tq, LANES), lambda b, qi: (b * nq + qi, gate_col)),
                  tab, tab,
                  per_b((n_cmp, HEAD_DIM)), per_b((HEAD_DIM, n_cmp)),
                  per_b((seq, 2 * HEAD_DIM)), per_b((V_ROWS, seq)),
                  per_b((seq, HEAD_DIM)), per_b((V_ROWS, seq)),
                  pl.BlockSpec((n_sb, n_cmp), lambda b, qi: (0, 0)),
                  pl.BlockSpec((HEAD_DIM, g * hpg), lambda b, qi: (0, 0))],
        out_specs=pl.BlockSpec((tq, width), lambda b, qi: (b * nq + qi, 0)),
        out_shape=jax.ShapeDtypeStruct((batch * seq, width), BF16),
        compiler_params=_params("parallel", "arbitrary"), name="nsa_attention",
    )(h, h, cos_t, sin_t, kc, vc_t, ksa, vs_t, kw, vw_t, ovl_t, norm_w.reshape(g * hpg, HEAD_DIM).T)


def _out_proj_kernel(x_ref, yhg_ref, ygm_ref, ynsa_ref, whg_ref, wgm_ref, wnsa_ref, lnw_ref, lnb_ref,
                     o_ref, oa_ref, ob_ref, *, alpha):
    mix = (_dot(yhg_ref[...], whg_ref[...]) + _dot(ygm_ref[...], wgm_ref[...])
           + _dot(ynsa_ref[...], wnsa_ref[...]))
    y = _layer_norm(alpha * x_ref[...] + mix, lnw_ref[...], lnb_ref[...])
    o_ref[...] = y
    _store_word_tables((oa_ref, ob_ref), _pack_bf16_pairs(y))


def out_proj_ln(x2d, y_hg, y_gm, y_nsa, w_out, ln_w, ln_b, alpha):
    n, d = x2d.shape
    w1, w2 = y_hg.shape[1], y_hg.shape[1] + y_gm.shape[1]
    whg = w_out[:w1].astype(BF16)
    wgm = w_out[w1:w2].astype(BF16)
    wnsa = w_out[w2:].astype(BF16)
    t = min(n, 512)

    def row(wd):
        return pl.BlockSpec((t, wd), lambda i: (i, 0))

    def const(shape):
        return pl.BlockSpec(shape, lambda i: (0,) * len(shape))

    kern = functools.partial(_out_proj_kernel, alpha=alpha)
    return pl.pallas_call(
        kern, grid=(n // t,),
        in_specs=[row(d), row(y_hg.shape[1]), row(y_gm.shape[1]), row(y_nsa.shape[1]),
                  const(whg.shape), const(wgm.shape), const(wnsa.shape), const((1, d)), const((1, d))],
        out_specs=[row(d), row(SC_ROW_WORDS), row(SC_ROW_WORDS)],
        out_shape=[jax.ShapeDtypeStruct((n, d), F32)] + [jax.ShapeDtypeStruct((n, SC_ROW_WORDS), jnp.uint32)] * 2,
        compiler_params=_params("parallel"), name="out_proj_ln",
    )(x2d, y_hg, y_gm, y_nsa, whg, wgm, wnsa, ln_w.reshape(1, d), ln_b.reshape(1, d))


def _router_kernel(x_ref, w_ref, b_ref, e_ref, p_ref, r_ref, cnt_ref, carry_ref):
    t = x_ref.shape[0]

    @pl.when(pl.program_id(0) == 0)
    def _():
        carry_ref[...] = jnp.zeros_like(carry_ref)

    x = x_ref[...]
    x_hi = x.astype(BF16)
    x_lo = (x - x_hi.astype(F32)).astype(BF16)
    w = w_ref[...]
    w_hi = w.astype(BF16)
    w_lo = (w - w_hi.astype(F32)).astype(BF16)
    logits = _dot_nt(w_hi, x_hi) + (_dot_nt(w_hi, x_lo) + _dot_nt(w_lo, x_hi)) + b_ref[...]
    n_e = logits.shape[0]
    sub = lax.broadcasted_iota(jnp.int32, logits.shape, 0)
    work = logits
    vals, idxs = [], []
    sel = jnp.zeros(logits.shape, F32)
    for _ in range(TOP_K):
        m = jnp.max(work, axis=0, keepdims=True)
        idx = jnp.min(jnp.where(work == m, sub, n_e), axis=0, keepdims=True)
        hit = sub == idx
        sel = jnp.where(hit, 1.0, sel)
        work = jnp.where(hit, -jnp.inf, work)
        vals.append(m)
        idxs.append(idx)
    exps = [jnp.exp(v - vals[0]) for v in vals]
    inv_den = 1.0 / (exps[0] + exps[1] + exps[2] + exps[3])
    earlier = (lax.broadcasted_iota(jnp.int32, (t, t), 0) < lax.broadcasted_iota(jnp.int32, (t, t), 1))
    before = _dot(sel.astype(BF16), earlier.astype(BF16)) + carry_ref[...]
    ranks = [jnp.sum(jnp.where(sub == idx, before, 0.0), axis=0, keepdims=True) for idx in idxs]
    e_ref[...] = jnp.concatenate(idxs, axis=0)
    p_ref[...] = jnp.concatenate([e * inv_den for e in exps], axis=0)
    r_ref[...] = jnp.concatenate(ranks, axis=0).astype(jnp.int32)
    carry_ref[...] = carry_ref[...] + jnp.sum(sel, axis=1, keepdims=True)
    cnt_ref[...] = carry_ref[...].astype(jnp.int32)


def moe_router(x2d, router_w, router_b):
    n, d = x2d.shape
    e = router_w.shape[1]
    t = min(n, 512)
    row4 = pl.BlockSpec((TOP_K, t), lambda i: (0, i))
    top_e, top_p, rank, counts = pl.pallas_call(
        _router_kernel, grid=(n // t,),
        in_specs=[pl.BlockSpec((t, d), lambda i: (i, 0)), pl.BlockSpec((e, d), lambda i: (0, 0)),
                  pl.BlockSpec((e, 1), lambda i: (0, 0))],
        out_specs=[row4, row4, row4, pl.BlockSpec((e, 1), lambda i: (0, 0))],
        out_shape=[jax.ShapeDtypeStruct((TOP_K, n), jnp.int32), jax.ShapeDtypeStruct((TOP_K, n), F32),
                   jax.ShapeDtypeStruct((TOP_K, n), jnp.int32), jax.ShapeDtypeStruct((e, 1), jnp.int32)],
        scratch_shapes=[pltpu.VMEM((e, 1), F32)],
        compiler_params=_params("arbitrary"), name="moe_router",
    )(x2d, router_w.T, router_b.reshape(e, 1))
    return top_e, top_p, rank, counts[:, 0]


def _expert_kernel(be_ref, valid_ref, xa_ref, xb_ref, wu_ref, bu_ref, wd_ref, bd_ref, oa_ref, ob_ref,
                   wu_bf, wd_bf):
    i = pl.program_id(0)
    f = wd_ref.shape[0]
    n_used = be_ref[pl.num_programs(0)]

    @pl.when((i == 0) | (be_ref[i] != be_ref[jnp.maximum(i - 1, 0)]))
    def _():
        wu_bf[...] = wu_ref[...].astype(BF16)
        wd_bf[...] = wd_ref[...].astype(BF16)

    @pl.when(i < n_used)
    def _():
        packed = jnp.concatenate([xa_ref[...], xb_ref[...]], axis=1)
        live = lax.broadcasted_iota(jnp.int32, packed.shape, 0) < valid_ref[i]
        x_lo, x_hi = _unpack_bf16_pairs(jnp.where(live, packed, jnp.uint32(0)))
        x = jnp.concatenate([x_lo.astype(BF16), x_hi.astype(BF16)], axis=1)
        hcat = _dot(x, wu_bf[...]) + bu_ref[...]
        glu = jnp.minimum(hcat[:, :f], SWIGLU_LIMIT)
        lin = jnp.clip(hcat[:, f:], -SWIGLU_LIMIT, SWIGLU_LIMIT)
        act = glu * _sigmoid(SWIGLU_ALPHA * glu) * (lin + 1.0)
        _store_word_tables((oa_ref, ob_ref), _pack_bf16_pairs(_dot(act.astype(BF16), wd_bf[...]) + bd_ref[...]))

    @pl.when(i >= n_used)
    def _():
        oa_ref[...] = jnp.zeros_like(oa_ref)
        ob_ref[...] = jnp.zeros_like(ob_ref)


def moe_experts(xa, xb, block_e, n_used, block_valid, w_up, b_up, w_down, b_down, layer):
    rows = xa.shape[0]
    _, e, d, f2 = w_up.shape
    f = f2 // 2
    nb = rows // EXPERT_BLOCK
    words = pl.BlockSpec((EXPERT_BLOCK, SC_ROW_WORDS), lambda i, be, nv: (i, 0))
    grid_spec = pltpu.PrefetchScalarGridSpec(
        num_scalar_prefetch=2, grid=(nb,),
        in_specs=[words, words,
                  pl.BlockSpec((None, None, d, f2), lambda i, be, nv: (layer, be[i], 0, 0)),
                  pl.BlockSpec((None, None, 1, f2), lambda i, be, nv: (layer, be[i], 0, 0)),
                  pl.BlockSpec((None, None, f, d), lambda i, be, nv: (layer, be[i], 0, 0)),
                  pl.BlockSpec((None, None, 1, d), lambda i, be, nv: (layer, be[i], 0, 0))],
        out_specs=[words, words],
        scratch_shapes=[pltpu.VMEM((d, f2), BF16), pltpu.VMEM((f, d), BF16)])
    depth = w_up.shape[0]
    return pl.pallas_call(
        _expert_kernel, grid_spec=grid_spec,
        out_shape=[jax.ShapeDtypeStruct((rows, SC_ROW_WORDS), jnp.uint32)] * 2,
        compiler_params=pltpu.CompilerParams(dimension_semantics=("arbitrary",), vmem_limit_bytes=EXPERT_VMEM_LIMIT),
        name="moe_experts",
    )(jnp.concatenate([block_e, n_used.reshape(1)]), block_valid, xa, xb, w_up, b_up.reshape(depth, e, 1, f2),
      w_down, b_down.reshape(depth, e, 1, d))


def _combine_kernel(x_ref, ya_ref, yb_ref, p_ref, lnw_ref, lnb_ref, o_ref, *, alpha):
    p = p_ref[...]
    moe = jnp.zeros(x_ref.shape, F32)
    for k in range(TOP_K):
        y_lo, y_hi = _unpack_bf16_pairs(jnp.concatenate([ya_ref[k], yb_ref[k]], axis=1))
        moe = moe + p[:, k:k + 1] * jnp.concatenate([y_lo, y_hi], axis=1)
    o_ref[...] = _layer_norm(alpha * x_ref[...] + moe, lnw_ref[...], lnb_ref[...])


def combine_ln(x2d, ya, yb, top_p, ln_w, ln_b, alpha):
    n, d = x2d.shape
    t = min(n, 512)
    kern = functools.partial(_combine_kernel, alpha=alpha)
    words = pl.BlockSpec((TOP_K, t, SC_ROW_WORDS), lambda i: (0, i, 0))
    return pl.pallas_call(
        kern, grid=(n // t,),
        in_specs=[pl.BlockSpec((t, d), lambda i: (i, 0)), words, words,
                  pl.BlockSpec((t, TOP_K), lambda i: (i, 0)),
                  pl.BlockSpec((1, d), lambda i: (0, 0)), pl.BlockSpec((1, d), lambda i: (0, 0))],
        out_specs=pl.BlockSpec((t, d), lambda i: (i, 0)),
        out_shape=jax.ShapeDtypeStruct((n, d), F32),
        compiler_params=_params("parallel"), name="moe_combine_ln",
    )(x2d, ya, yb, top_p, ln_w.reshape(1, d), ln_b.reshape(1, d))


def _sc_mesh():
    return plsc.VectorSubcoreMesh(core_axis_name="core", subcore_axis_name="subcore")


def sc_gather_rows(tables, idx):
    r = idx.shape[0]
    nt = len(tables)
    out = jax.ShapeDtypeStruct((r, SC_ROW_WORDS), tables[0].dtype)

    @pl.kernel(out_type=(out,) * nt, mesh=_sc_mesh(), name="sc_gather_rows")
    def gather(*refs):
        x_hbm, i_hbm, o_hbm = refs[:nt], refs[nt], refs[nt + 1:]
        for j in range(nt):
            def body(i_vmem, o_vmem, table=x_hbm[j]):
                pltpu.sync_copy(table.at[i_vmem.at[0]], o_vmem)

            pltpu.emit_pipeline(
                body, grid=(r // SC_WINDOW,),
                in_specs=[pl.BlockSpec((1, SC_WINDOW), lambda i: (0, i))],
                out_specs=[pl.BlockSpec((SC_WINDOW, SC_ROW_WORDS), lambda i: (i, 0))],
                core_axis_name=("core", "subcore"), dimension_semantics=(pltpu.PARALLEL,),
            )(i_hbm, o_hbm[j])

    return gather(*tables, idx.reshape(1, r))


def sc_scatter_rows(tables, dest_t, n_rows):
    n = tables[0].shape[0]
    nt = len(tables)
    copies = dest_t.shape[0]
    out = jax.ShapeDtypeStruct((n_rows, SC_ROW_WORDS), tables[0].dtype)

    @pl.kernel(out_type=(out,) * nt, mesh=_sc_mesh(), scratch_types=[], name="sc_scatter_rows")
    def scatter(*refs):
        x_hbm, i_hbm, o_hbm = refs[:nt], refs[nt], refs[nt + 1:]
        for j in range(nt):
            def body(x_vmem, i_vmem, out_j=o_hbm[j]):
                for k in range(copies):
                    pltpu.sync_copy(x_vmem, out_j.at[i_vmem.at[k]])

            pltpu.emit_pipeline(
                body, grid=(n // SC_WINDOW,),
                in_specs=[pl.BlockSpec((SC_WINDOW, SC_ROW_WORDS), lambda i: (i, 0)),
                          pl.BlockSpec((copies, SC_WINDOW), lambda i: (0, i))],
                out_specs=[],
                core_axis_name=("core", "subcore"), dimension_semantics=(pltpu.PARALLEL,),
            )(x_hbm[j], i_hbm)

    return scatter(*tables, dest_t)


def moe_ffn_ln(x_f32, x_packed, router_w, router_b, w_up, b_up, w_down, b_down, layer, ln_w, ln_b, alpha):
    n, d = x_f32.shape
    top_e, top_p, rank, counts = moe_router(x_f32, router_w, router_b)
    padded = (counts + EXPERT_BLOCK - 1) // EXPERT_BLOCK * EXPERT_BLOCK
    pad_end = jnp.cumsum(padded)
    pad_start = pad_end - padded
    n_assign = n * TOP_K
    n_blocks = -(-(n_assign + N_EXPERTS * (EXPERT_BLOCK - 1)) // EXPERT_BLOCK)
    experts = jnp.arange(N_EXPERTS, dtype=jnp.int32)
    dest_t = rank + jnp.sum(jnp.where(top_e[..., None] == experts, pad_start.astype(jnp.int32), 0), axis=-1)
    block_first = jnp.arange(n_blocks, dtype=jnp.int32) * EXPERT_BLOCK
    block_e = jnp.clip(jnp.sum((pad_end[None, :] <= block_first[:, None]).astype(jnp.int32), axis=1),
                       0, N_EXPERTS - 1)
    block_valid = jnp.clip(counts[block_e] - (block_first - pad_start[block_e]), 0, EXPERT_BLOCK)
    n_used = (pad_end[-1] // EXPERT_BLOCK).astype(jnp.int32)
    xa, xb = sc_scatter_rows(x_packed, dest_t, n_blocks * EXPERT_BLOCK)
    ya, yb = moe_experts(xa, xb, block_e, n_used, block_valid.astype(jnp.int32), w_up, b_up, w_down, b_down,
                         layer)
    ya, yb = sc_gather_rows((ya, yb), dest_t.reshape(-1))
    return combine_ln(x_f32, ya.reshape(TOP_K, n, SC_ROW_WORDS), yb.reshape(TOP_K, n, SC_ROW_WORDS), top_p.T,
                      ln_w, ln_b, alpha)


def kernel(x, positions, w_in, hg_lower_bounds, hg_norm_w, gm_ln_w, gm_ln_b, gm_spatial_w, gm_spatial_b, gm_norm_w, nsa_cmp_pe, nsa_cmp_w1, nsa_cmp_w2, nsa_norm_w, w_out, ln1_w, ln1_b, router_w, router_b, exp_w_up, exp_b_up, exp_w_down, exp_b_down, ln2_w, ln2_b):
    batch, seq, d = x.shape
    depth = w_in.shape[0]
    n = batch * seq
    alpha = (2 * depth) ** 0.25
    hg_w = hg_norm_w.shape[1]
    gm_w = gm_norm_w.shape[1]
    nsa_w = nsa_norm_w.shape[1]
    kv_w = NSA_KV_GROUPS * HEAD_DIM
    in_width = w_in.shape[2]
    off_gm = 4 * hg_w
    off_q = off_gm + 2 * gm_w
    off_kv = off_q + nsa_w
    off_gate = off_kv + 6 * kv_w
    width_pad = -(-in_width // LANES) * LANES

    cosf, sinf, cos_t, sin_t = rope_tables(positions)
    lb_all = jnp.cumsum(jax.nn.softmax(hg_lower_bounds.astype(F32), axis=0), axis=0)
    lb_all = lb_all - lb_all[0:1]

    x2d = x.reshape(n, d)
    for l in range(depth):
        w_l = jnp.pad(w_in[l], ((0, 0), (0, width_pad - in_width))).astype(BF16)
        h = in_proj(x2d, w_l)
        h3 = h.reshape(batch, seq, width_pad)
        y_hg = hgrn2(h3, lb_all[l], hg_norm_w[l]).reshape(n, hg_w)
        y_gm = gmlp(h, gm_ln_w[l], gm_ln_b[l], gm_spatial_w[l], gm_spatial_b[l], gm_norm_w[l],
                    off_gm // gm_w, off_gm // gm_w + 1)
        kc = compress(h, off_kv // kv_w, nsa_cmp_pe[l, 0], nsa_cmp_w1[l, 0], nsa_cmp_w2[l, 0], batch, seq, False)
        vc_t = compress(h, off_kv // kv_w + 1, nsa_cmp_pe[l, 1], nsa_cmp_w1[l, 1], nsa_cmp_w2[l, 1], batch, seq,
                        True)
        ksa, vs_t, kw, vw_t = nsa_kprep(h, cosf, sinf, batch, seq, (off_kv + 2 * kv_w) // LANES)
        y_nsa = nsa_attention(h, cos_t, sin_t, kc, vc_t, ksa, vs_t, kw, vw_t, nsa_norm_w[l], batch, seq,
                              off_q // nsa_w, off_gate // LANES)
        x1, x1a, x1b = out_proj_ln(x2d, y_hg, y_gm, y_nsa, w_out[l], ln1_w[l], ln1_b[l], alpha)
        x2d = moe_ffn_ln(x1, (x1a, x1b), router_w[l], router_b[l], exp_w_up, exp_b_up, exp_w_down, exp_b_down, l,
                         ln2_w[l], ln2_b[l], alpha)
    return x2d.reshape(batch, seq, d)
```

```python
import functools
import math

import numpy as np
import jax
import jax.numpy as jnp
from jax import lax
from jax.experimental import pallas as pl
from jax.experimental.pallas import tpu as pltpu
from jax.experimental.pallas import tpu_sc as plsc

F32 = jnp.float32
BF16 = jnp.bfloat16
HIGHEST = lax.Precision.HIGHEST

HEAD_DIM = 64
LANES = 128
VMEM_LIMIT = 48 * 1024 * 1024
EXPERT_VMEM_LIMIT = 56 * 1024 * 1024

HG_CHUNK = 64
GM_CHUNK = 128
GM_TILE_CHUNKS = 4
NSA_KV_GROUPS = 2
NSA_HPG = 4
CMP_BLOCK = 32
CMP_STRIDE = 16
CMP_HIDDEN = 128
SEL_BLOCK = 64
N_SEL = 16
WINDOW = 512
N_GATES = 3
IMP_FORCE = 1e9
FORCE_KEY = int(np.float32(IMP_FORCE).view(np.int32))
NEG_INF = -1e30
N_EXPERTS = 32
TOP_K = 4
SWIGLU_ALPHA = 1.702
SWIGLU_LIMIT = 7.0
EXPERT_BLOCK = 512
SC_ROW_WORDS = 256
SC_WINDOW = 128
ROPE_THETA = 10000.0
LOG2_E = 1.4426950408889634
LN_EPS = 1e-5
RMS_EPS = 1e-6
V_ROWS = HEAD_DIM + 16


def _params(*sem):
    return pltpu.CompilerParams(dimension_semantics=sem, vmem_limit_bytes=VMEM_LIMIT)


def _dot(a, b):
    return jnp.dot(a, b, preferred_element_type=F32)


def _dot_nt(a, b, precision=None):
    return lax.dot_general(a, b, (((1,), (1,)), ((), ())), precision=precision,
                           preferred_element_type=F32)


def _dot_tn(a, b):
    return lax.dot_general(a, b, (((0,), (0,)), ((), ())), preferred_element_type=F32)


def _sigmoid(x):
    return 1.0 / (1.0 + jnp.exp(-x))


def _gelu(x):
    return 0.5 * x * (1.0 + jnp.tanh(0.7978845608028654 * (x + 0.044715 * x * x * x)))


def _layer_norm(x, w, b):
    mu = jnp.mean(x, axis=-1, keepdims=True)
    xc = x - mu
    var = jnp.mean(xc * xc, axis=-1, keepdims=True)
    return xc * lax.rsqrt(var + LN_EPS) * w + b


def _pack_bf16_pairs(y):
    w = y.shape[1] // 2
    bits = pltpu.bitcast(y.astype(BF16).astype(F32), jnp.uint32)
    return lax.shift_right_logical(bits[:, :w], jnp.uint32(16)) | (bits[:, w:] & jnp.uint32(0xFFFF0000))


def _unpack_bf16_pairs(u):
    lo = pltpu.bitcast(lax.shift_left(u, jnp.uint32(16)), F32)
    hi = pltpu.bitcast(u & jnp.uint32(0xFFFF0000), F32)
    return lo, hi


def _store_word_tables(refs, packed):
    for j, ref in enumerate(refs):
        ref[...] = packed[:, j * SC_ROW_WORDS:(j + 1) * SC_ROW_WORDS]


def _head_mean_sq(o, bd_ones):
    sq = o * o
    hi = sq.astype(BF16)
    lo = (sq - hi.astype(F32)).astype(BF16)
    ones = bd_ones.astype(BF16)
    return (_dot(hi, ones) + _dot(lo, ones)) * (1.0 / HEAD_DIM)


def _rope_kernel(pos_ref, inv_ref, cos_ref, sin_ref, cost_ref, sint_ref):
    ang = inv_ref[...] * pos_ref[...]
    c = jnp.cos(ang)
    s = jnp.sin(ang)
    cos_t = jnp.concatenate([c, c], axis=0)
    sin_t = jnp.concatenate([-s, s], axis=0)
    cost_ref[...] = cos_t
    sint_ref[...] = sin_t
    cos_ref[...] = cos_t.T
    sin_ref[...] = sin_t.T


def rope_tables(positions):
    n = positions.size
    tile = min(n, 2048)
    posf = positions.reshape(1, n).astype(F32)
    inv = ROPE_THETA ** (-jnp.arange(0, HEAD_DIM, 2, dtype=F32) / HEAD_DIM)
    row = pl.BlockSpec((tile, HEAD_DIM), lambda i: (i, 0))
    rowt = pl.BlockSpec((HEAD_DIM, tile), lambda i: (0, i))
    return pl.pallas_call(
        _rope_kernel, grid=(n // tile,),
        in_specs=[pl.BlockSpec((1, tile), lambda i: (0, i)), pl.BlockSpec((HEAD_DIM // 2, 1), lambda i: (0, 0))],
        out_specs=[row, row, rowt, rowt],
        out_shape=[jax.ShapeDtypeStruct((n, HEAD_DIM), F32)] * 2 + [jax.ShapeDtypeStruct((HEAD_DIM, n), F32)] * 2,
        compiler_params=_params("parallel"), name="rope_tables",
    )(posf, inv.reshape(HEAD_DIM // 2, 1))


def _in_proj_kernel(x_ref, w_ref, h_ref):
    h_ref[...] = _dot(x_ref[...].astype(BF16), w_ref[...])


def in_proj(x2d, w_bf16):
    n, d = x2d.shape
    width = w_bf16.shape[1]
    tile = min(n, 512)
    return pl.pallas_call(
        _in_proj_kernel, grid=(n // tile,),
        in_specs=[pl.BlockSpec((tile, d), lambda i: (i, 0)), pl.BlockSpec((d, width), lambda i: (0, 0))],
        out_specs=pl.BlockSpec((tile, width), lambda i: (i, 0)),
        out_shape=jax.ShapeDtypeStruct((n, width), F32),
        compiler_params=_params("parallel"), name="in_proj")(x2d, w_bf16)


HG_LEVELS = (64, 32, 16, 8, 4, 2)
HG_BATCH = 8


def _hgrn_constants():
    c = HG_CHUNK
    t = np.arange(c)
    u = t[None, :]
    rows = [u <= t[:, None], u > t[:, None]]
    masks = [np.eye(c, dtype=bool)]
    for m in HG_LEVELS:
        ref = ((t // m) * m + m // 2 - 1)[:, None]
        second = (t % m >= m // 2)[:, None]
        rows.append(((u > ref) & (u <= t[:, None]) & second) | ((u > t[:, None]) & (u <= ref) & ~second))
        masks.append((t[:, None] // m == t[None, :] // m) & second & (t[None, :] % m < m // 2))
    pmat = np.concatenate(rows, axis=0).astype(np.float32)
    masks = np.stack([np.tile(mk, (1, 4)) for mk in masks]).astype(np.float32)
    return pmat, masks


def _hgrn_kernel(q_ref, f_ref, i_ref, g_ref, lb_ref, nw_ref, pmat_ref, masks_ref, bd_ref, hm_ref,
                 o_ref, state_ref):
    c = HG_CHUNK

    @pl.when(pl.program_id(1) == 0)
    def _():
        state_ref[...] = jnp.zeros_like(state_ref)

    lb = lb_ref[...]
    bd = bd_ref[...]
    hm = hm_ref[...]
    hm_tiles = [jnp.broadcast_to(hm[h:h + 1], (c, hm.shape[1])).astype(BF16) for h in range(4)]
    pmat = pmat_ref[...]
    a = jnp.log(lb)
    log1m = jnp.log(1.0 - lb)
    nb, _, w = q_ref.shape
    seqs = range(nb)

    fr = f_ref[...].reshape(nb * c, w)
    hq = q_ref[...].reshape(nb * c, w)
    v = i_ref[...].reshape(nb * c, w)
    qf = hq * _sigmoid(hq)
    log_sig = jnp.minimum(fr, 0.0) - jnp.log(1.0 + jnp.exp(-jnp.abs(fr)))
    cc = log1m + log_sig
    log_f = jnp.maximum(a, cc) + jnp.log(1.0 + jnp.exp(-jnp.abs(a - cc)))
    kk = (1.0 - lb) * _sigmoid(-fr)

    hi = log_f.astype(BF16)
    lo = (log_f - hi.astype(F32)).astype(BF16)

    def side_by_side(x):
        return jnp.concatenate([x[b * c:(b + 1) * c] for b in seqs], axis=1)

    sums = jnp.minimum(_dot(pmat, side_by_side(hi)) + _dot(pmat, side_by_side(lo)), 0.0)
    e_all = jnp.exp(sums)

    def e_rows(r, b):
        return e_all[r * c:(r + 1) * c, b * w:(b + 1) * w]

    def stacked(x):
        xb = x.astype(BF16)
        return jnp.concatenate([xb * hm_tiles[h] for h in range(4)], axis=0)

    q_s = [qf[b * c:(b + 1) * c] for b in seqs]
    k_s = [kk[b * c:(b + 1) * c] for b in seqs]
    v_s = [v[b * c:(b + 1) * c] for b in seqs]

    att = [masks_ref[0] * _dot_nt(q_s[b].astype(BF16), stacked(k_s[b])) for b in seqs]
    for li in range(len(HG_LEVELS)):
        for b in seqs:
            e_l = e_rows(2 + li, b)
            att[b] = att[b] + masks_ref[li + 1] * _dot_nt((q_s[b] * e_l).astype(BF16), stacked(k_s[b] * e_l))

    outs = []
    for b in seqs:
        e_b = e_rows(0, b)
        st = state_ref[b]
        o = _dot(att[b].astype(BF16), stacked(v_s[b])) + _dot_nt((q_s[b] * e_b).astype(BF16), st.astype(BF16))
        k_rest = (k_s[b] * e_rows(1, b)).astype(BF16)
        state_ref[b] = st * e_b[c - 1:c] + bd * _dot_tn(v_s[b].astype(BF16), k_rest)
        outs.append(o)

    o = jnp.concatenate(outs, axis=0)
    ms = _head_mean_sq(o, bd)
    y = o * lax.rsqrt(ms + RMS_EPS) * nw_ref[...] * _sigmoid(g_ref[...].reshape(nb * c, w))
    o_ref[...] = y.astype(o_ref.dtype).reshape(nb, c, w)


def hgrn2(h3, lb, norm_w):
    batch, seq, _ = h3.shape
    w = lb.shape[-1]
    c = HG_CHUNK
    nb = math.gcd(batch, HG_BATCH)
    pmat, masks = _hgrn_constants()
    lane_head = np.arange(w) // HEAD_DIM
    bd = (lane_head[:, None] == lane_head[None, :]).astype(np.float32)
    hm = (np.arange(4)[:, None] == lane_head[None, :]).astype(np.float32)

    def col(j):
        return pl.BlockSpec((nb, c, w), lambda b, i, j=j: (b, i, j))

    def const(shape):
        return pl.BlockSpec(shape, lambda b, i: (0,) * len(shape))

    return pl.pallas_call(
        _hgrn_kernel, grid=(batch // nb, seq // c),
        in_specs=[col(0), col(1), col(2), col(3), const((1, w)), const((1, w)),
                  const(pmat.shape), const(masks.shape), const(bd.shape), const(hm.shape)],
        out_specs=pl.BlockSpec((nb, c, w), lambda b, i: (b, i, 0)),
        out_shape=jax.ShapeDtypeStruct((batch, seq, w), BF16),
        scratch_shapes=[pltpu.VMEM((nb, w, w), F32)],
        compiler_params=_params("parallel", "arbitrary"), name="hgrn2",
    )(h3, h3, h3, h3, lb.reshape(1, w), norm_w.reshape(1, w), jnp.asarray(pmat, BF16), jnp.asarray(masks),
      jnp.asarray(bd), jnp.asarray(hm))


def _gmlp_kernel(u_ref, v_ref, lnw_ref, lnb_ref, ws_ref, bias_ref, nw_ref, bd_ref, hm_ref, o_ref):
    c = GM_CHUNK
    groups = ws_ref.shape[0]
    u = _gelu(u_ref[...])
    v = _layer_norm(_gelu(v_ref[...]), lnw_ref[...], lnb_ref[...])
    hm = hm_ref[...]
    bd = bd_ref[...]
    causal = lax.broadcasted_iota(jnp.int32, (c, c), 0) >= lax.broadcasted_iota(jnp.int32, (c, c), 1)
    w_cat = jnp.concatenate([jnp.where(causal, ws_ref[g], 0.0).astype(BF16) for g in range(groups)], axis=1)
    n_chunks = u.shape[0] // c
    w = u.shape[1]
    v_bd = jnp.concatenate(
        [jnp.concatenate([v[j * c:(j + 1) * c] * hm[g:g + 1] for g in range(groups)], axis=0) for j in range(n_chunks)],
        axis=1).astype(BF16)
    sv = _dot(w_cat, v_bd)
    y = u * (jnp.concatenate([sv[:, j * w:(j + 1) * w] for j in range(n_chunks)], axis=0)
             + jnp.concatenate([bias_ref[...]] * n_chunks, axis=0))
    ms = _head_mean_sq(y, bd)
    o_ref[...] = (y * lax.rsqrt(ms + RMS_EPS) * nw_ref[...]).astype(o_ref.dtype)


def gmlp(h, ln_w, ln_b, w_s, b_s, norm_w, u_col, v_col):
    n = h.shape[0]
    groups, c, _ = w_s.shape
    w = groups * HEAD_DIM
    lane_head = np.arange(w) // HEAD_DIM
    bd = (lane_head[:, None] == lane_head[None, :]).astype(np.float32)
    hm = (np.arange(groups)[:, None] == lane_head[None, :]).astype(np.float32)
    bias = jnp.repeat(b_s.T, HEAD_DIM, axis=1)

    def const(shape):
        return pl.BlockSpec(shape, lambda i: (0,) * len(shape))

    t = math.gcd(n, GM_TILE_CHUNKS * c)
    return pl.pallas_call(
        _gmlp_kernel, grid=(n // t,),
        in_specs=[pl.BlockSpec((t, w), lambda i: (i, u_col)), pl.BlockSpec((t, w), lambda i: (i, v_col)),
                  const((1, w)), const((1, w)), const(w_s.shape), const((c, w)), const((1, w)),
                  const(bd.shape), const(hm.shape)],
        out_specs=pl.BlockSpec((t, w), lambda i: (i, 0)),
        out_shape=jax.ShapeDtypeStruct((n, w), BF16),
        compiler_params=_params("parallel"), name="gmlp",
    )(h, h, ln_w.reshape(1, w), ln_b.reshape(1, w), w_s, bias, norm_w.reshape(1, w),
      jnp.asarray(bd), jnp.asarray(hm))


def _compress_kernel(kv_ref, wtop_ref, wbot_ref, pe_ref, w2_ref, o_ref, *, transposed):
    units = kv_ref.shape[0] // CMP_STRIDE
    pe = pe_ref[...].astype(BF16)
    p = jnp.zeros((units, wtop_ref.shape[2]), F32)
    q = jnp.zeros_like(p)
    const = jnp.zeros((1, wtop_ref.shape[2]), F32)
    for j in range(CMP_STRIDE):
        rows = kv_ref[pl.ds(j, units, stride=CMP_STRIDE), :].astype(BF16)
        p = p + _dot(rows, wtop_ref[j])
        q = q + _dot(rows, wbot_ref[j])
        const = const + _dot(pe[0, j:j + 1], wtop_ref[j]) + _dot(pe[1, j:j + 1], wbot_ref[j])
    q_next = jnp.concatenate([q[1:], jnp.zeros_like(q[0:1])], axis=0)
    hid = _gelu(p + q_next + const)
    y = _dot(hid.astype(BF16), w2_ref[...])
    if transposed:
        y = y.T
    for g in range(o_ref.shape[0]):
        sl = slice(g * HEAD_DIM, (g + 1) * HEAD_DIM)
        o_ref[g] = (y[sl] if transposed else y[:, sl]).astype(o_ref.dtype)


def compress(h, col, pe, w1, w2, batch, seq, transposed):
    g = NSA_KV_GROUPS
    half = CMP_STRIDE
    units = seq // half
    gw = g * HEAD_DIM
    eye = jnp.eye(g, dtype=F32)
    w1r = w1.reshape(2, half, HEAD_DIM, CMP_HIDDEN)
    wbd = jnp.einsum('hjdn,gk->hjgdkn', w1r, eye).reshape(2, half, gw, g * CMP_HIDDEN).astype(BF16)
    w2bd = jnp.einsum('nd,gk->gnkd', w2, eye).reshape(g * CMP_HIDDEN, gw).astype(BF16)
    pe2 = jnp.broadcast_to(pe.reshape(2, half, 1, HEAD_DIM), (2, half, g, HEAD_DIM)).reshape(2, half, gw)

    def const(shape):
        return pl.BlockSpec(shape, lambda b: (0,) * len(shape))

    per_group = (HEAD_DIM, units) if transposed else (units, HEAD_DIM)
    return pl.pallas_call(
        functools.partial(_compress_kernel, transposed=transposed), grid=(batch,),
        in_specs=[pl.BlockSpec((seq, gw), lambda b: (b, col)),
                  const(wbd.shape[1:]), const(wbd.shape[1:]), const(pe2.shape), const(w2bd.shape)],
        out_specs=pl.BlockSpec((None, g) + per_group, lambda b: (b, 0, 0, 0)),
        out_shape=jax.ShapeDtypeStruct((batch, g) + per_group, BF16),
        compiler_params=_params("parallel"), name="nsa_compress",
    )(h, wbd[0], wbd[1], pe2, w2bd)


def _rot_half_pairs(x):
    lane = lax.broadcasted_iota(jnp.int32, x.shape, 1)
    fwd = pltpu.roll(x, 32, axis=1)
    bwd = pltpu.roll(x, 96, axis=1)
    return jnp.where((lane % HEAD_DIM) < HEAD_DIM // 2, bwd, fwd)


def _kprep_kernel(ks_ref, vs_ref, kw_ref, vw_ref, cos_ref, sin_ref, ksa_ref, vso_ref, kwo_ref, vwo_ref):
    t = ks_ref.shape[0]
    cos = cos_ref[...]
    sin = sin_ref[...]
    cos2 = jnp.concatenate([cos, cos], axis=1)
    sin2 = jnp.concatenate([sin, sin], axis=1)
    ks = ks_ref[...]
    kw = kw_ref[...]
    ks_r = ks * cos2 + _rot_half_pairs(ks) * sin2
    kw_r = kw * cos2 + _rot_half_pairs(kw) * sin2
    pos = pl.program_id(1) * t + lax.broadcasted_iota(jnp.int32, (t, HEAD_DIM), 0)
    onehot = (pos // SEL_BLOCK == lax.broadcasted_iota(jnp.int32, (t, HEAD_DIM), 1)).astype(F32)
    vs_t = vs_ref[...].T
    vw_t = vw_ref[...].T
    tail = (lax.broadcasted_iota(jnp.int32, (V_ROWS - HEAD_DIM, t), 0) == 0).astype(F32)
    for g in range(NSA_KV_GROUPS):
        sl = slice(g * HEAD_DIM, (g + 1) * HEAD_DIM)
        ksa_ref[g] = jnp.concatenate([ks_r[:, sl], onehot], axis=1).astype(BF16)
        vso_ref[g] = jnp.concatenate([vs_t[sl], tail], axis=0).astype(BF16)
        kwo_ref[g] = kw_r[:, sl].astype(BF16)
        vwo_ref[g] = jnp.concatenate([vw_t[sl], tail], axis=0).astype(BF16)


def nsa_kprep(h, cosf, sinf, batch, seq, col0):
    g = NSA_KV_GROUPS
    t = min(seq, 512)
    nt = seq // t

    def col(j):
        return pl.BlockSpec((t, LANES), lambda b, i, j=j: (b * nt + i, col0 + j))

    tab = pl.BlockSpec((t, HEAD_DIM), lambda b, i: (b * nt + i, 0))

    def out(wd):
        return pl.BlockSpec((None, g, t, wd), lambda b, i: (b, 0, i, 0))

    out_t = pl.BlockSpec((None, g, V_ROWS, t), lambda b, i: (b, 0, 0, i))
    k_shape = jax.ShapeDtypeStruct((batch, g, seq, HEAD_DIM), BF16)
    v_shape = jax.ShapeDtypeStruct((batch, g, V_ROWS, seq), BF16)
    return pl.pallas_call(
        _kprep_kernel, grid=(batch, nt),
        in_specs=[col(0), col(1), col(2), col(3), tab, tab],
        out_specs=[out(2 * HEAD_DIM), out_t, out(HEAD_DIM), out_t],
        out_shape=[jax.ShapeDtypeStruct((batch, g, seq, 2 * HEAD_DIM), BF16), v_shape, k_shape, v_shape],
        compiler_params=_params("parallel", "parallel"), name="nsa_kprep",
    )(h, h, h, h, cosf, sinf)


def _nsa_kernel(hq_ref, gate_ref, cos_ref, sin_ref, kc_ref, vc_ref, ksa_ref, vs_ref, kw_ref, vw_ref,
                ovl_ref, nw_ref, o_ref, *, tq, tk, n_sb):
    qi = pl.program_id(1)
    hpg = NSA_HPG
    groups = NSA_KV_GROUPS
    rows = hpg * tq
    t0 = qi * tq
    scale = 1.0 / math.sqrt(HEAD_DIM)
    half = HEAD_DIM // 2

    hq_t = hq_ref[...].T
    cos = cos_ref[...]
    sin = sin_ref[...]
    q_raw, q_rot = [], []
    for g in range(groups):
        raw_g, rot_g = [], []
        for h in range(hpg):
            r0 = (g * hpg + h) * HEAD_DIM
            qh = hq_t[r0:r0 + HEAD_DIM]
            swapped = jnp.concatenate([qh[half:], qh[:half]], axis=0)
            raw_g.append(qh * (scale * LOG2_E))
            rot_g.append((qh * cos + swapped * sin) * (scale * LOG2_E))
        q_raw.append(jnp.concatenate(raw_g, axis=1).astype(BF16))
        q_rot.append(jnp.concatenate(rot_g, axis=1))

    tpos = t0 + lax.broadcasted_iota(jnp.int32, (1, tq), 1)
    tpos_r = jnp.concatenate([tpos] * hpg, axis=1)

    def flash_steps(s, v_t, carry):
        m_new = [jnp.maximum(carry[g][0], jnp.max(s[g], axis=0, keepdims=True)) for g in range(groups)]
        p = [jnp.exp2(s[g] - m_new[g]).astype(BF16) for g in range(groups)]
        return tuple((m_new[g], jnp.exp2(carry[g][0] - m_new[g]) * carry[g][1] + _dot(v_t[g], p[g]))
                     for g in range(groups))

    init = ((jnp.full((1, rows), NEG_INF, F32), jnp.zeros((V_ROWS, rows), F32)),) * groups

    wk = WINDOW + tq
    kw0 = pl.multiple_of(jnp.maximum(t0 - WINDOW, 0), tq)
    kpos_w = kw0 + lax.broadcasted_iota(jnp.int32, (wk, 1), 0)
    mask_w = (kpos_w <= tpos_r) & (kpos_w > tpos_r - WINDOW)
    n_pad = jnp.maximum(WINDOW - 1 - tpos_r, 0).astype(F32)
    n_cmp = kc_ref.shape[1]
    cmp_end = lax.broadcasted_iota(jnp.int32, (n_cmp, 1), 0) * CMP_STRIDE + (CMP_BLOCK - 1)
    mask_c = cmp_end <= tpos_r
    blk = lax.broadcasted_iota(jnp.int32, (n_sb, tq), 0)
    cur = (t0 + lax.broadcasted_iota(jnp.int32, (n_sb, tq), 1)) // SEL_BLOCK
    s_w = [jnp.where(mask_w, _dot(kw_ref[g, pl.ds(kw0, wk), :], q_rot[g].astype(BF16)), NEG_INF)
           for g in range(groups)]
    s_c = [jnp.where(mask_c, _dot(kc_ref[g], q_raw[g]), NEG_INF) for g in range(groups)]
    win = flash_steps(s_w, [vw_ref[g, :, pl.ds(kw0, wk)] for g in range(groups)], init)
    o_w, o_c, imps = [], [], []
    for g in range(groups):
        m_w, acc_w = win[g]
        m_f = jnp.where(n_pad > 0.0, jnp.maximum(m_w, 0.0), m_w)
        a_w = jnp.exp2(m_w - m_f)
        o_w.append(acc_w[0:HEAD_DIM] * (a_w / (acc_w[HEAD_DIM:HEAD_DIM + 1] * a_w + n_pad * jnp.exp2(-m_f))))

    for g in range(groups):
        e_c = jnp.exp2(s_c[g] - jnp.max(s_c[g], axis=0, keepdims=True))
        p_c = jnp.where(mask_c, e_c * (1.0 / jnp.sum(e_c, axis=0, keepdims=True)), 0.0)
        o_c.append(_dot(vc_ref[g], p_c.astype(BF16)))

        p_sum = p_c[:, 0:tq]
        for h in range(1, hpg):
            p_sum = p_sum + p_c[:, h * tq:(h + 1) * tq]
        imp = jnp.dot(ovl_ref[...], p_sum, precision=HIGHEST, preferred_element_type=F32)
        key = pltpu.bitcast(jnp.maximum(imp, 0.0), jnp.int32)
        key = jnp.where((blk == 0) | (blk == cur) | (blk == cur - 1), FORCE_KEY, key)
        imps.append(jnp.where(blk > cur, -1, key))

    n_sel = min(N_SEL, n_sb)

    def bit_body(it, taus):
        bit = lax.shift_left(jnp.int32(1), 30 - it)
        out = []
        for g in range(groups):
            cand = taus[g] | bit
            cnt = jnp.sum(jnp.where(imps[g] >= cand, 1, 0), axis=0, keepdims=True)
            out.append(jnp.where(cnt >= n_sel, cand, taus[g]))
        return tuple(out)

    taus = lax.fori_loop(0, 31, bit_body, (jnp.zeros((1, tq), jnp.int32),) * groups)
    lower = (lax.broadcasted_iota(jnp.int32, (n_sb, n_sb), 0)
             > lax.broadcasted_iota(jnp.int32, (n_sb, n_sb), 1)).astype(BF16)
    q_aug = []
    for g in range(groups):
        above = imps[g] > taus[g]
        equal = imps[g] == taus[g]
        need = n_sel - jnp.sum(jnp.where(above, 1, 0), axis=0, keepdims=True)
        earlier = _dot(lower, jnp.where(equal, 1.0, 0.0).astype(BF16))
        selected = above | (equal & (earlier < need.astype(F32)))
        sel_bias = jnp.where(selected, 0.0, NEG_INF)
        if n_sb < HEAD_DIM:
            sel_bias = jnp.concatenate([sel_bias, jnp.zeros((HEAD_DIM - n_sb, tq), F32)], axis=0)
        q_aug.append(jnp.concatenate([q_rot[g], jnp.concatenate([sel_bias] * hpg, axis=1)], axis=0).astype(BF16))

    def scores(k0, width):
        return [_dot(ksa_ref[g, pl.ds(k0, width), :], q_aug[g]) for g in range(groups)]

    def values(k0, width):
        return [vs_ref[g, :, pl.ds(k0, width)] for g in range(groups)]

    def wide_body(kt, carry):
        k0 = pl.multiple_of(kt * tk, tk)
        return flash_steps(scores(k0, tk), values(k0, tk), carry)

    n_wide = t0 // tk
    carry = lax.fori_loop(0, n_wide, wide_body, init)

    def narrow_body(j, carry):
        k0 = pl.multiple_of(n_wide * tk + j * tq, tq)
        return flash_steps(scores(k0, tq), values(k0, tq), carry)

    carry = lax.fori_loop(0, (t0 - n_wide * tk) // tq, narrow_body, carry)
    k0 = pl.multiple_of(t0, tq)
    mask_s = k0 + lax.broadcasted_iota(jnp.int32, (tq, 1), 0) <= tpos_r
    carry = flash_steps([jnp.where(mask_s, s, NEG_INF) for s in scores(k0, tq)], values(k0, tq), carry)

    gates = _sigmoid(gate_ref[...].T)
    nw = nw_ref[...]
    outs = []
    for g in range(groups):
        acc_s = carry[g][1]
        o_s = acc_s[0:HEAD_DIM] * (1.0 / acc_s[HEAD_DIM:HEAD_DIM + 1])
        for h in range(hpg):
            sl = slice(h * tq, (h + 1) * tq)
            r = (g * hpg + h) * N_GATES
            o = gates[r:r + 1] * o_c[g][:, sl] + gates[r + 1:r + 2] * o_s[:, sl] + gates[r + 2:r + 3] * o_w[g][:, sl]
            ms = jnp.mean(o * o, axis=0, keepdims=True)
            outs.append(o * lax.rsqrt(ms + RMS_EPS) * nw[:, g * hpg + h:g * hpg + h + 1])
    o_ref[...] = jnp.concatenate(outs, axis=0).T.astype(o_ref.dtype)


def nsa_attention(h, cos_t, sin_t, kc, vc_t, ksa, vs_t, kw, vw_t, norm_w, batch, seq, q_col0, gate_col):
    g, hpg = NSA_KV_GROUPS, NSA_HPG
    tq = min(seq, 256)
    nq = seq // tq
    n_sb = seq // SEL_BLOCK
    assert n_sb <= HEAD_DIM, "selection-block one-hot shares the 64 spare key lanes"
    n_cmp = kc.shape[2]
    units = np.arange(n_cmp)[:, None] + np.arange(CMP_BLOCK // CMP_STRIDE)[None, :]
    ovl = np.zeros((n_cmp, n_sb), np.float32)
    for c in range((seq - CMP_BLOCK) // CMP_STRIDE + 1):
        for u in units[c]:
            ovl[c, u // (SEL_BLOCK // CMP_STRIDE)] += 1.0
    ovl_t = jnp.asarray(ovl.T)

    def per_b(shape):
        return pl.BlockSpec((None, g) + shape, lambda b, qi: (b, 0, 0, 0))

    width = g * hpg * HEAD_DIM
    tab = pl.BlockSpec((HEAD_DIM, tq), lambda b, qi: (0, b * nq + qi))
    tk = min(seq, 512)
    assert seq >= WINDOW + tq and seq % tk == 0 and tk % tq == 0
    kern = functools.partial(_nsa_kernel, tq=tq, tk=tk, n_sb=n_sb)
    return pl.pallas_call(
        kern, grid=(batch, nq),
        in_specs=[pl.BlockSpec((tq, width), lambda b, qi: (b * nq + qi, q_col0)),
                  pl.BlockSpec((tq, LANES), lambda b, qi: (b * nq + qi, gate_col)),
                  tab, tab,
                  per_b((n_cmp, HEAD_DIM)), per_b((HEAD_DIM, n_cmp)),
                  per_b((seq, 2 * HEAD_DIM)), per_b((V_ROWS, seq)),
                  per_b((seq, HEAD_DIM)), per_b((V_ROWS, seq)),
                  pl.BlockSpec((n_sb, n_cmp), lambda b, qi: (0, 0)),
                  pl.BlockSpec((HEAD_DIM, g * hpg), lambda b, qi: (0, 0))],
        out_specs=pl.BlockSpec((tq, width), lambda b, qi: (b * nq + qi, 0)),
        out_shape=jax.ShapeDtypeStruct((batch * seq, width), BF16),
        compiler_params=_params("parallel", "arbitrary"), name="nsa_attention",
    )(h, h, cos_t, sin_t, kc, vc_t, ksa, vs_t, kw, vw_t, ovl_t, norm_w.reshape(g * hpg, HEAD_DIM).T)


def _out_proj_kernel(x_ref, yhg_ref, ygm_ref, ynsa_ref, whg_ref, wgm_ref, wnsa_ref, lnw_ref, lnb_ref,
                     o_ref, oa_ref, ob_ref, *, alpha):
    half = x_ref.shape[0] // 2
    parts = [slice(0, half), slice(half, 2 * half)]
    mixes = [(_dot(yhg_ref[r, :], whg_ref[...]) + _dot(ygm_ref[r, :], wgm_ref[...])
              + _dot(ynsa_ref[r, :], wnsa_ref[...])) for r in parts]
    for r, mix in zip(parts, mixes):
        y = _layer_norm(alpha * x_ref[r, :] + mix, lnw_ref[...], lnb_ref[...])
        o_ref[r, :] = y
        packed = _pack_bf16_pairs(y)
        oa_ref[r, :] = packed[:, :SC_ROW_WORDS]
        ob_ref[r, :] = packed[:, SC_ROW_WORDS:]


def out_proj_ln(x2d, y_hg, y_gm, y_nsa, w_out, ln_w, ln_b, alpha):
    n, d = x2d.shape
    w1, w2 = y_hg.shape[1], y_hg.shape[1] + y_gm.shape[1]
    whg = w_out[:w1].astype(BF16)
    wgm = w_out[w1:w2].astype(BF16)
    wnsa = w_out[w2:].astype(BF16)
    t = min(n, 512)

    def row(wd):
        return pl.BlockSpec((t, wd), lambda i: (i, 0))

    def const(shape):
        return pl.BlockSpec(shape, lambda i: (0,) * len(shape))

    kern = functools.partial(_out_proj_kernel, alpha=alpha)
    return pl.pallas_call(
        kern, grid=(n // t,),
        in_specs=[row(d), row(y_hg.shape[1]), row(y_gm.shape[1]), row(y_nsa.shape[1]),
                  const(whg.shape), const(wgm.shape), const(wnsa.shape), const((1, d)), const((1, d))],
        out_specs=[row(d), row(SC_ROW_WORDS), row(SC_ROW_WORDS)],
        out_shape=[jax.ShapeDtypeStruct((n, d), F32)] + [jax.ShapeDtypeStruct((n, SC_ROW_WORDS), jnp.uint32)] * 2,
        compiler_params=_params("parallel"), name="out_proj_ln",
    )(x2d, y_hg, y_gm, y_nsa, whg, wgm, wnsa, ln_w.reshape(1, d), ln_b.reshape(1, d))


def _router_kernel(x_ref, w_ref, b_ref, e_ref, p_ref, r_ref, cnt_ref, carry_ref):
    t = x_ref.shape[0]

    @pl.when(pl.program_id(0) == 0)
    def _():
        carry_ref[...] = jnp.zeros_like(carry_ref)

    x = x_ref[...]
    x_hi = x.astype(BF16)
    x_lo = (x - x_hi.astype(F32)).astype(BF16)
    w = w_ref[...]
    w_hi = w.astype(BF16)
    w_lo = (w - w_hi.astype(F32)).astype(BF16)
    logits = _dot_nt(w_hi, x_hi) + (_dot_nt(w_hi, x_lo) + _dot_nt(w_lo, x_hi)) + b_ref[...]
    n_e = logits.shape[0]
    sub = lax.broadcasted_iota(jnp.int32, logits.shape, 0)
    work = logits
    vals, idxs = [], []
    sel = jnp.zeros(logits.shape, F32)
    for _ in range(TOP_K):
        m = jnp.max(work, axis=0, keepdims=True)
        idx = jnp.min(jnp.where(work == m, sub, n_e), axis=0, keepdims=True)
        hit = sub == idx
        sel = jnp.where(hit, 1.0, sel)
        work = jnp.where(hit, -jnp.inf, work)
        vals.append(m)
        idxs.append(idx)
    exps = [jnp.exp(v - vals[0]) for v in vals]
    inv_den = 1.0 / (exps[0] + exps[1] + exps[2] + exps[3])
    earlier = (lax.broadcasted_iota(jnp.int32, (t, t), 0) < lax.broadcasted_iota(jnp.int32, (t, t), 1))
    before = _dot(sel.astype(BF16), earlier.astype(BF16)) + carry_ref[...]
    ranks = [jnp.sum(jnp.where(sub == idx, before, 0.0), axis=0, keepdims=True) for idx in idxs]
    e_ref[...] = jnp.concatenate(idxs, axis=0)
    p_ref[...] = jnp.concatenate([e * inv_den for e in exps], axis=0)
    r_ref[...] = jnp.concatenate(ranks, axis=0).astype(jnp.int32)
    carry_ref[...] = carry_ref[...] + jnp.sum(sel, axis=1, keepdims=True)
    cnt_ref[...] = carry_ref[...].astype(jnp.int32)


def moe_router(x2d, router_w, router_b):
    n, d = x2d.shape
    e = router_w.shape[1]
    t = min(n, 512)
    row4 = pl.BlockSpec((TOP_K, t), lambda i: (0, i))
    top_e, top_p, rank, counts = pl.pallas_call(
        _router_kernel, grid=(n // t,),
        in_specs=[pl.BlockSpec((t, d), lambda i: (i, 0)), pl.BlockSpec((e, d), lambda i: (0, 0)),
                  pl.BlockSpec((e, 1), lambda i: (0, 0))],
        out_specs=[row4, row4, row4, pl.BlockSpec((e, 1), lambda i: (0, 0))],
        out_shape=[jax.ShapeDtypeStruct((TOP_K, n), jnp.int32), jax.ShapeDtypeStruct((TOP_K, n), F32),
                   jax.ShapeDtypeStruct((TOP_K, n), jnp.int32), jax.ShapeDtypeStruct((e, 1), jnp.int32)],
        scratch_shapes=[pltpu.VMEM((e, 1), F32)],
        compiler_params=_params("arbitrary"), name="moe_router",
    )(x2d, router_w.T, router_b.reshape(e, 1))
    return top_e, top_p, rank, counts[:, 0]


def _expert_kernel(be_ref, valid_ref, xa_ref, xb_ref, wu_ref, bu_ref, wd_ref, bd_ref, oa_ref, ob_ref,
                   wu_bf, wd_bf):
    i = pl.program_id(0)
    f = wd_ref.shape[0]
    n_used = be_ref[pl.num_programs(0)]

    @pl.when((i == 0) | (be_ref[i] != be_ref[jnp.maximum(i - 1, 0)]))
    def _():
        wu_bf[...] = wu_ref[...].astype(BF16)
        wd_bf[...] = wd_ref[...].astype(BF16)

    @pl.when(i < n_used)
    def _():
        packed = jnp.concatenate([xa_ref[...], xb_ref[...]], axis=1)
        live = lax.broadcasted_iota(jnp.int32, packed.shape, 0) < valid_ref[i]
        x_lo, x_hi = _unpack_bf16_pairs(jnp.where(live, packed, jnp.uint32(0)))
        x = jnp.concatenate([x_lo.astype(BF16), x_hi.astype(BF16)], axis=1)
        hcat = _dot(x, wu_bf[...]) + bu_ref[...]
        glu = jnp.minimum(hcat[:, :f], SWIGLU_LIMIT)
        lin = jnp.clip(hcat[:, f:], -SWIGLU_LIMIT, SWIGLU_LIMIT)
        act = glu * _sigmoid(SWIGLU_ALPHA * glu) * (lin + 1.0)
        _store_word_tables((oa_ref, ob_ref), _pack_bf16_pairs(_dot(act.astype(BF16), wd_bf[...]) + bd_ref[...]))

    @pl.when(i >= n_used)
    def _():
        oa_ref[...] = jnp.zeros_like(oa_ref)
        ob_ref[...] = jnp.zeros_like(ob_ref)


def moe_experts(xa, xb, block_e, n_used, block_valid, w_up, b_up, w_down, b_down, layer):
    rows = xa.shape[0]
    _, e, d, f2 = w_up.shape
    f = f2 // 2
    nb = rows // EXPERT_BLOCK
    words = pl.BlockSpec((EXPERT_BLOCK, SC_ROW_WORDS), lambda i, be, nv: (i, 0))
    grid_spec = pltpu.PrefetchScalarGridSpec(
        num_scalar_prefetch=2, grid=(nb,),
        in_specs=[words, words,
                  pl.BlockSpec((None, None, d, f2), lambda i, be, nv: (layer, be[i], 0, 0)),
                  pl.BlockSpec((None, None, 1, f2), lambda i, be, nv: (layer, be[i], 0, 0)),
                  pl.BlockSpec((None, None, f, d), lambda i, be, nv: (layer, be[i], 0, 0)),
                  pl.BlockSpec((None, None, 1, d), lambda i, be, nv: (layer, be[i], 0, 0))],
        out_specs=[words, words],
        scratch_shapes=[pltpu.VMEM((d, f2), BF16), pltpu.VMEM((f, d), BF16)])
    depth = w_up.shape[0]
    return pl.pallas_call(
        _expert_kernel, grid_spec=grid_spec,
        out_shape=[jax.ShapeDtypeStruct((rows, SC_ROW_WORDS), jnp.uint32)] * 2,
        compiler_params=pltpu.CompilerParams(dimension_semantics=("arbitrary",), vmem_limit_bytes=EXPERT_VMEM_LIMIT),
        name="moe_experts",
    )(jnp.concatenate([block_e, n_used.reshape(1)]), block_valid, xa, xb, w_up, b_up.reshape(depth, e, 1, f2),
      w_down, b_down.reshape(depth, e, 1, d))


def _combine_kernel(x_ref, ya_ref, yb_ref, p_ref, lnw_ref, lnb_ref, o_ref, *, alpha):
    p = p_ref[...]
    moe = jnp.zeros(x_ref.shape, F32)
    for k in range(TOP_K):
        y_lo, y_hi = _unpack_bf16_pairs(jnp.concatenate([ya_ref[k], yb_ref[k]], axis=1))
        moe = moe + p[:, k:k + 1] * jnp.concatenate([y_lo, y_hi], axis=1)
    o_ref[...] = _layer_norm(alpha * x_ref[...] + moe, lnw_ref[...], lnb_ref[...])


def combine_ln(x2d, ya, yb, top_p, ln_w, ln_b, alpha):
    n, d = x2d.shape
    t = min(n, 512)
    kern = functools.partial(_combine_kernel, alpha=alpha)
    words = pl.BlockSpec((TOP_K, t, SC_ROW_WORDS), lambda i: (0, i, 0))
    return pl.pallas_call(
        kern, grid=(n // t,),
        in_specs=[pl.BlockSpec((t, d), lambda i: (i, 0)), words, words,
                  pl.BlockSpec((t, TOP_K), lambda i: (i, 0)),
                  pl.BlockSpec((1, d), lambda i: (0, 0)), pl.BlockSpec((1, d), lambda i: (0, 0))],
        out_specs=pl.BlockSpec((t, d), lambda i: (i, 0)),
        out_shape=jax.ShapeDtypeStruct((n, d), F32),
        compiler_params=_params("parallel"), name="moe_combine_ln",
    )(x2d, ya, yb, top_p, ln_w.reshape(1, d), ln_b.reshape(1, d))


def _sc_mesh():
    return plsc.VectorSubcoreMesh(core_axis_name="core", subcore_axis_name="subcore")


def sc_gather_rows(tables, idx):
    r = idx.shape[0]
    nt = len(tables)
    out = jax.ShapeDtypeStruct((r, SC_ROW_WORDS), tables[0].dtype)

    @pl.kernel(out_type=(out,) * nt, mesh=_sc_mesh(), name="sc_gather_rows")
    def gather(*refs):
        x_hbm, i_hbm, o_hbm = refs[:nt], refs[nt], refs[nt + 1:]
        for j in range(nt):
            def body(i_vmem, o_vmem, table=x_hbm[j]):
                pltpu.sync_copy(table.at[i_vmem.at[0]], o_vmem)

            pltpu.emit_pipeline(
                body, grid=(r // SC_WINDOW,),
                in_specs=[pl.BlockSpec((1, SC_WINDOW), lambda i: (0, i))],
                out_specs=[pl.BlockSpec((SC_WINDOW, SC_ROW_WORDS), lambda i: (i, 0))],
                core_axis_name=("core", "subcore"), dimension_semantics=(pltpu.PARALLEL,),
            )(i_hbm, o_hbm[j])

    return gather(*tables, idx.reshape(1, r))


def sc_scatter_rows(tables, dest_t, n_rows):
    n = tables[0].shape[0]
    nt = len(tables)
    copies = dest_t.shape[0]
    out = jax.ShapeDtypeStruct((n_rows, SC_ROW_WORDS), tables[0].dtype)

    @pl.kernel(out_type=(out,) * nt, mesh=_sc_mesh(), scratch_types=[], name="sc_scatter_rows")
    def scatter(*refs):
        x_hbm, i_hbm, o_hbm = refs[:nt], refs[nt], refs[nt + 1:]
        for j in range(nt):
            def body(x_vmem, i_vmem, out_j=o_hbm[j]):
                for k in range(copies):
                    pltpu.sync_copy(x_vmem, out_j.at[i_vmem.at[k]])

            pltpu.emit_pipeline(
                body, grid=(n // SC_WINDOW,),
                in_specs=[pl.BlockSpec((SC_WINDOW, SC_ROW_WORDS), lambda i: (i, 0)),
                          pl.BlockSpec((copies, SC_WINDOW), lambda i: (0, i))],
                out_specs=[],
                core_axis_name=("core", "subcore"), dimension_semantics=(pltpu.PARALLEL,),
            )(x_hbm[j], i_hbm)

    return scatter(*tables, dest_t)


def moe_ffn_ln(x_f32, x_packed, router_w, router_b, w_up, b_up, w_down, b_down, layer, ln_w, ln_b, alpha):
    n, d = x_f32.shape
    top_e, top_p, rank, counts = moe_router(x_f32, router_w, router_b)
    padded = (counts + EXPERT_BLOCK - 1) // EXPERT_BLOCK * EXPERT_BLOCK
    pad_end = jnp.cumsum(padded)
    pad_start = pad_end - padded
    n_assign = n * TOP_K
    n_blocks = -(-(n_assign + N_EXPERTS * (EXPERT_BLOCK - 1)) // EXPERT_BLOCK)
    experts = jnp.arange(N_EXPERTS, dtype=jnp.int32)
    dest_t = rank + jnp.sum(jnp.where(top_e[..., None] == experts, pad_start.astype(jnp.int32), 0), axis=-1)
    block_first = jnp.arange(n_blocks, dtype=jnp.int32) * EXPERT_BLOCK
    block_e = jnp.clip(jnp.sum((pad_end[None, :] <= block_first[:, None]).astype(jnp.int32), axis=1),
                       0, N_EXPERTS - 1)
    block_valid = jnp.clip(counts[block_e] - (block_first - pad_start[block_e]), 0, EXPERT_BLOCK)
    n_used = (pad_end[-1] // EXPERT_BLOCK).astype(jnp.int32)
    xa, xb = sc_scatter_rows(x_packed, dest_t, n_blocks * EXPERT_BLOCK)
    ya, yb = moe_experts(xa, xb, block_e, n_used, block_valid.astype(jnp.int32), w_up, b_up, w_down, b_down,
                         layer)
    ya, yb = sc_gather_rows((ya, yb), dest_t.reshape(-1))
    return combine_ln(x_f32, ya.reshape(TOP_K, n, SC_ROW_WORDS), yb.reshape(TOP_K, n, SC_ROW_WORDS), top_p.T,
                      ln_w, ln_b, alpha)


def kernel(x, positions, w_in, hg_lower_bounds, hg_norm_w, gm_ln_w, gm_ln_b, gm_spatial_w, gm_spatial_b, gm_norm_w, nsa_cmp_pe, nsa_cmp_w1, nsa_cmp_w2, nsa_norm_w, w_out, ln1_w, ln1_b, router_w, router_b, exp_w_up, exp_b_up, exp_w_down, exp_b_down, ln2_w, ln2_b):
    batch, seq, d = x.shape
    depth = w_in.shape[0]
    n = batch * seq
    alpha = (2 * depth) ** 0.25
    hg_w = hg_norm_w.shape[1]
    gm_w = gm_norm_w.shape[1]
    nsa_w = nsa_norm_w.shape[1]
    kv_w = NSA_KV_GROUPS * HEAD_DIM
    in_width = w_in.shape[2]
    off_gm = 4 * hg_w
    off_q = off_gm + 2 * gm_w
    off_kv = off_q + nsa_w
    off_gate = off_kv + 6 * kv_w
    width_pad = -(-in_width // LANES) * LANES

    cosf, sinf, cos_t, sin_t = rope_tables(positions)
    lb_all = jnp.cumsum(jax.nn.softmax(hg_lower_bounds.astype(F32), axis=0), axis=0)
    lb_all = lb_all - lb_all[0:1]

    x2d = x.reshape(n, d)
    for l in range(depth):
        w_l = jnp.pad(w_in[l], ((0, 0), (0, width_pad - in_width))).astype(BF16)
        h = in_proj(x2d, w_l)
        h3 = h.reshape(batch, seq, width_pad)
        y_hg = hgrn2(h3, lb_all[l], hg_norm_w[l]).reshape(n, hg_w)
        y_gm = gmlp(h, gm_ln_w[l], gm_ln_b[l], gm_spatial_w[l], gm_spatial_b[l], gm_norm_w[l],
                    off_gm // gm_w, off_gm // gm_w + 1)
        kc = compress(h, off_kv // kv_w, nsa_cmp_pe[l, 0], nsa_cmp_w1[l, 0], nsa_cmp_w2[l, 0], batch, seq, False)
        vc_t = compress(h, off_kv // kv_w + 1, nsa_cmp_pe[l, 1], nsa_cmp_w1[l, 1], nsa_cmp_w2[l, 1], batch, seq,
                        True)
        ksa, vs_t, kw, vw_t = nsa_kprep(h, cosf, sinf, batch, seq, (off_kv + 2 * kv_w) // LANES)
        y_nsa = nsa_attention(h, cos_t, sin_t, kc, vc_t, ksa, vs_t, kw, vw_t, nsa_norm_w[l], batch, seq,
                              off_q // nsa_w, off_gate // LANES)
        x1, x1a, x1b = out_proj_ln(x2d, y_hg, y_gm, y_nsa, w_out[l], ln1_w[l], ln1_b[l], alpha)
        x2d = moe_ffn_ln(x1, (x1a, x1b), router_w[l], router_b[l], exp_w_up, exp_b_up, exp_w_down, exp_b_down, l,
                         ln2_w[l], ln2_b[l], alpha)
    return x2d.reshape(batch, seq, d)
```

```python
import functools
import math

import numpy as np
import jax
import jax.numpy as jnp
from jax import lax
from jax.experimental import pallas as pl
from jax.experimental.pallas import tpu as pltpu
from jax.experimental.pallas import tpu_sc as plsc

F32 = jnp.float32
BF16 = jnp.bfloat16
HIGHEST = lax.Precision.HIGHEST

HEAD_DIM = 64
LANES = 128
VMEM_LIMIT = 48 * 1024 * 1024
EXPERT_VMEM_LIMIT = 56 * 1024 * 1024

HG_CHUNK = 64
GM_CHUNK = 128
GM_TILE_CHUNKS = 4
NSA_KV_GROUPS = 2
NSA_HPG = 4
CMP_BLOCK = 32
CMP_STRIDE = 16
CMP_HIDDEN = 128
SEL_BLOCK = 64
N_SEL = 16
WINDOW = 512
N_GATES = 3
IMP_FORCE = 1e9
FORCE_KEY = int(np.float32(IMP_FORCE).view(np.int32))
NEG_INF = -1e30
N_EXPERTS = 32
TOP_K = 4
SWIGLU_ALPHA = 1.702
SWIGLU_LIMIT = 7.0
EXPERT_BLOCK = 512
COMBINE_INPUT_BUFFERS = 3
SC_ROW_WORDS = 256
SC_WINDOW = 128
ROPE_THETA = 10000.0
LOG2_E = 1.4426950408889634
LN_EPS = 1e-5
RMS_EPS = 1e-6
V_ROWS = HEAD_DIM + 16


def _params(*sem):
    return pltpu.CompilerParams(dimension_semantics=sem, vmem_limit_bytes=VMEM_LIMIT)


def _dot(a, b):
    return jnp.dot(a, b, preferred_element_type=F32)


def _dot_nt(a, b, precision=None):
    return lax.dot_general(a, b, (((1,), (1,)), ((), ())), precision=precision,
                           preferred_element_type=F32)


def _dot_tn(a, b):
    return lax.dot_general(a, b, (((0,), (0,)), ((), ())), preferred_element_type=F32)


def _sigmoid(x):
    return 1.0 / (1.0 + jnp.exp(-x))


def _gelu(x):
    return 0.5 * x * (1.0 + jnp.tanh(0.7978845608028654 * (x + 0.044715 * x * x * x)))


def _layer_norm(x, w, b):
    mu = jnp.mean(x, axis=-1, keepdims=True)
    xc = x - mu
    var = jnp.mean(xc * xc, axis=-1, keepdims=True)
    return xc * lax.rsqrt(var + LN_EPS) * w + b


def _pack_bf16_pairs(y):
    w = y.shape[1] // 2
    bits = pltpu.bitcast(y.astype(BF16).astype(F32), jnp.uint32)
    return lax.shift_right_logical(bits[:, :w], jnp.uint32(16)) | (bits[:, w:] & jnp.uint32(0xFFFF0000))


def _unpack_bf16_pairs(u):
    lo = pltpu.bitcast(lax.shift_left(u, jnp.uint32(16)), F32)
    hi = pltpu.bitcast(u & jnp.uint32(0xFFFF0000), F32)
    return lo, hi


def _store_word_tables(refs, packed):
    for j, ref in enumerate(refs):
        ref[...] = packed[:, j * SC_ROW_WORDS:(j + 1) * SC_ROW_WORDS]


def _head_mean_sq(o, bd_ones):
    sq = o * o
    hi = sq.astype(BF16)
    lo = (sq - hi.astype(F32)).astype(BF16)
    ones = bd_ones.astype(BF16)
    return (_dot(hi, ones) + _dot(lo, ones)) * (1.0 / HEAD_DIM)


def _rope_kernel(pos_ref, inv_ref, cos_ref, sin_ref, cost_ref, sint_ref):
    ang = inv_ref[...] * pos_ref[...]
    c = jnp.cos(ang)
    s = jnp.sin(ang)
    cos_t = jnp.concatenate([c, c], axis=0)
    sin_t = jnp.concatenate([-s, s], axis=0)
    cost_ref[...] = cos_t
    sint_ref[...] = sin_t
    cos_ref[...] = cos_t.T
    sin_ref[...] = sin_t.T


def rope_tables(positions):
    n = positions.size
    tile = min(n, 2048)
    posf = positions.reshape(1, n).astype(F32)
    inv = ROPE_THETA ** (-jnp.arange(0, HEAD_DIM, 2, dtype=F32) / HEAD_DIM)
    row = pl.BlockSpec((tile, HEAD_DIM), lambda i: (i, 0))
    rowt = pl.BlockSpec((HEAD_DIM, tile), lambda i: (0, i))
    return pl.pallas_call(
        _rope_kernel, grid=(n // tile,),
        in_specs=[pl.BlockSpec((1, tile), lambda i: (0, i)), pl.BlockSpec((HEAD_DIM // 2, 1), lambda i: (0, 0))],
        out_specs=[row, row, rowt, rowt],
        out_shape=[jax.ShapeDtypeStruct((n, HEAD_DIM), F32)] * 2 + [jax.ShapeDtypeStruct((HEAD_DIM, n), F32)] * 2,
        compiler_params=_params("parallel"), name="rope_tables",
    )(posf, inv.reshape(HEAD_DIM // 2, 1))


def _in_proj_kernel(x_ref, w_ref, h_ref):
    h_ref[...] = _dot(x_ref[...].astype(BF16), w_ref[...])


def in_proj(x2d, w_bf16):
    n, d = x2d.shape
    width = w_bf16.shape[1]
    tile = min(n, 512)
    return pl.pallas_call(
        _in_proj_kernel, grid=(n // tile,),
        in_specs=[pl.BlockSpec((tile, d), lambda i: (i, 0)), pl.BlockSpec((d, width), lambda i: (0, 0))],
        out_specs=pl.BlockSpec((tile, width), lambda i: (i, 0)),
        out_shape=jax.ShapeDtypeStruct((n, width), F32),
        compiler_params=_params("parallel"), name="in_proj")(x2d, w_bf16)


HG_LEVELS = (64, 32, 16, 8, 4, 2)
HG_BATCH = 8


def _hgrn_constants():
    c = HG_CHUNK
    t = np.arange(c)
    u = t[None, :]
    rows = [u <= t[:, None], u > t[:, None]]
    masks = [np.eye(c, dtype=bool)]
    for m in HG_LEVELS:
        ref = ((t // m) * m + m // 2 - 1)[:, None]
        second = (t % m >= m // 2)[:, None]
        rows.append(((u > ref) & (u <= t[:, None]) & second) | ((u > t[:, None]) & (u <= ref) & ~second))
        masks.append((t[:, None] // m == t[None, :] // m) & second & (t[None, :] % m < m // 2))
    pmat = np.concatenate(rows, axis=0).astype(np.float32)
    masks = np.stack([np.tile(mk, (1, 4)) for mk in masks]).astype(np.float32)
    return pmat, masks


def _hgrn_kernel(q_ref, f_ref, i_ref, g_ref, lb_ref, nw_ref, pmat_ref, masks_ref, bd_ref, hm_ref,
                 o_ref, state_ref):
    c = HG_CHUNK

    @pl.when(pl.program_id(1) == 0)
    def _():
        state_ref[...] = jnp.zeros_like(state_ref)

    lb = lb_ref[...]
    bd = bd_ref[...]
    hm = hm_ref[...]
    hm_tiles = [jnp.broadcast_to(hm[h:h + 1], (c, hm.shape[1])).astype(BF16) for h in range(4)]
    pmat = pmat_ref[...]
    a = jnp.log(lb)
    log1m = jnp.log(1.0 - lb)
    nb, _, w = q_ref.shape
    seqs = range(nb)

    fr = f_ref[...].reshape(nb * c, w)
    hq = q_ref[...].reshape(nb * c, w)
    v = i_ref[...].reshape(nb * c, w)
    qf = hq * _sigmoid(hq)
    log_sig = jnp.minimum(fr, 0.0) - jnp.log(1.0 + jnp.exp(-jnp.abs(fr)))
    cc = log1m + log_sig
    log_f = jnp.maximum(a, cc) + jnp.log(1.0 + jnp.exp(-jnp.abs(a - cc)))
    kk = (1.0 - lb) * _sigmoid(-fr)

    hi = log_f.astype(BF16)
    lo = (log_f - hi.astype(F32)).astype(BF16)

    def side_by_side(x):
        return jnp.concatenate([x[b * c:(b + 1) * c] for b in seqs], axis=1)

    sums = jnp.minimum(_dot(pmat, side_by_side(hi)) + _dot(pmat, side_by_side(lo)), 0.0)
    e_all = jnp.exp(sums)

    def e_rows(r, b):
        return e_all[r * c:(r + 1) * c, b * w:(b + 1) * w]

    def stacked(x):
        xb = x.astype(BF16)
        return jnp.concatenate([xb * hm_tiles[h] for h in range(4)], axis=0)

    q_s = [qf[b * c:(b + 1) * c] for b in seqs]
    k_s = [kk[b * c:(b + 1) * c] for b in seqs]
    v_s = [v[b * c:(b + 1) * c] for b in seqs]

    att = [masks_ref[0] * _dot_nt(q_s[b].astype(BF16), stacked(k_s[b])) for b in seqs]
    for li in range(len(HG_LEVELS)):
        for b in seqs:
            e_l = e_rows(2 + li, b)
            att[b] = att[b] + masks_ref[li + 1] * _dot_nt((q_s[b] * e_l).astype(BF16), stacked(k_s[b] * e_l))

    outs = []
    for b in seqs:
        e_b = e_rows(0, b)
        st = state_ref[b]
        o = _dot(att[b].astype(BF16), stacked(v_s[b])) + _dot_nt((q_s[b] * e_b).astype(BF16), st.astype(BF16))
        k_rest = (k_s[b] * e_rows(1, b)).astype(BF16)
        state_ref[b] = st * e_b[c - 1:c] + bd * _dot_tn(v_s[b].astype(BF16), k_rest)
        outs.append(o)

    o = jnp.concatenate(outs, axis=0)
    ms = _head_mean_sq(o, bd)
    y = o * lax.rsqrt(ms + RMS_EPS) * nw_ref[...] * _sigmoid(g_ref[...].reshape(nb * c, w))
    o_ref[...] = y.astype(o_ref.dtype).reshape(nb, c, w)


def hgrn2(h3, lb, norm_w):
    batch, seq, _ = h3.shape
    w = lb.shape[-1]
    c = HG_CHUNK
    nb = math.gcd(batch, HG_BATCH)
    pmat, masks = _hgrn_constants()
    lane_head = np.arange(w) // HEAD_DIM
    bd = (lane_head[:, None] == lane_head[None, :]).astype(np.float32)
    hm = (np.arange(4)[:, None] == lane_head[None, :]).astype(np.float32)

    def col(j):
        return pl.BlockSpec((nb, c, w), lambda b, i, j=j: (b, i, j))

    def const(shape):
        return pl.BlockSpec(shape, lambda b, i: (0,) * len(shape))

    return pl.pallas_call(
        _hgrn_kernel, grid=(batch // nb, seq // c),
        in_specs=[col(0), col(1), col(2), col(3), const((1, w)), const((1, w)),
                  const(pmat.shape), const(masks.shape), const(bd.shape), const(hm.shape)],
        out_specs=pl.BlockSpec((nb, c, w), lambda b, i: (b, i, 0)),
        out_shape=jax.ShapeDtypeStruct((batch, seq, w), BF16),
        scratch_shapes=[pltpu.VMEM((nb, w, w), F32)],
        compiler_params=_params("parallel", "arbitrary"), name="hgrn2",
    )(h3, h3, h3, h3, lb.reshape(1, w), norm_w.reshape(1, w), jnp.asarray(pmat, BF16), jnp.asarray(masks),
      jnp.asarray(bd), jnp.asarray(hm))


def _gmlp_kernel(u_ref, v_ref, lnw_ref, lnb_ref, ws_ref, bias_ref, nw_ref, bd_ref, hm_ref, o_ref):
    c = GM_CHUNK
    groups = ws_ref.shape[0]
    u = _gelu(u_ref[...])
    v = _layer_norm(_gelu(v_ref[...]), lnw_ref[...], lnb_ref[...])
    hm = hm_ref[...]
    bd = bd_ref[...]
    causal = lax.broadcasted_iota(jnp.int32, (c, c), 0) >= lax.broadcasted_iota(jnp.int32, (c, c), 1)
    w_cat = jnp.concatenate([jnp.where(causal, ws_ref[g], 0.0).astype(BF16) for g in range(groups)], axis=1)
    for j in range(u.shape[0] // c):
        rows = slice(j * c, (j + 1) * c)
        v_j = v[rows]
        v_bd = jnp.concatenate([v_j * hm[g:g + 1] for g in range(groups)], axis=0).astype(BF16)
        y = u[rows] * (bias_ref[...] + _dot(w_cat, v_bd))
        ms = _head_mean_sq(y, bd)
        o_ref[rows, :] = (y * lax.rsqrt(ms + RMS_EPS) * nw_ref[...]).astype(o_ref.dtype)


def gmlp(h, ln_w, ln_b, w_s, b_s, norm_w, u_col, v_col):
    n = h.shape[0]
    groups, c, _ = w_s.shape
    w = groups * HEAD_DIM
    lane_head = np.arange(w) // HEAD_DIM
    bd = (lane_head[:, None] == lane_head[None, :]).astype(np.float32)
    hm = (np.arange(groups)[:, None] == lane_head[None, :]).astype(np.float32)
    bias = jnp.repeat(b_s.T, HEAD_DIM, axis=1)

    def const(shape):
        return pl.BlockSpec(shape, lambda i: (0,) * len(shape))

    t = math.gcd(n, GM_TILE_CHUNKS * c)
    return pl.pallas_call(
        _gmlp_kernel, grid=(n // t,),
        in_specs=[pl.BlockSpec((t, w), lambda i: (i, u_col)), pl.BlockSpec((t, w), lambda i: (i, v_col)),
                  const((1, w)), const((1, w)), const(w_s.shape), const((c, w)), const((1, w)),
                  const(bd.shape), const(hm.shape)],
        out_specs=pl.BlockSpec((t, w), lambda i: (i, 0)),
        out_shape=jax.ShapeDtypeStruct((n, w), BF16),
        compiler_params=_params("parallel"), name="gmlp",
    )(h, h, ln_w.reshape(1, w), ln_b.reshape(1, w), w_s, bias, norm_w.reshape(1, w),
      jnp.asarray(bd), jnp.asarray(hm))


def _compress_kernel(kv_ref, wtop_ref, wbot_ref, pe_ref, w2_ref, o_ref, *, transposed):
    units = kv_ref.shape[0] // CMP_STRIDE
    pe = pe_ref[...].astype(BF16)
    p = jnp.zeros((units, wtop_ref.shape[2]), F32)
    q = jnp.zeros_like(p)
    const = jnp.zeros((1, wtop_ref.shape[2]), F32)
    for j in range(CMP_STRIDE):
        rows = kv_ref[pl.ds(j, units, stride=CMP_STRIDE), :].astype(BF16)
        p = p + _dot(rows, wtop_ref[j])
        q = q + _dot(rows, wbot_ref[j])
        const = const + _dot(pe[0, j:j + 1], wtop_ref[j]) + _dot(pe[1, j:j + 1], wbot_ref[j])
    q_next = jnp.concatenate([q[1:], jnp.zeros_like(q[0:1])], axis=0)
    hid = _gelu(p + q_next + const)
    y = _dot(hid.astype(BF16), w2_ref[...])
    if transposed:
        y = y.T
    for g in range(o_ref.shape[0]):
        sl = slice(g * HEAD_DIM, (g + 1) * HEAD_DIM)
        o_ref[g] = (y[sl] if transposed else y[:, sl]).astype(o_ref.dtype)


def compress(h, col, pe, w1, w2, batch, seq, transposed):
    g = NSA_KV_GROUPS
    half = CMP_STRIDE
    units = seq // half
    gw = g * HEAD_DIM
    eye = jnp.eye(g, dtype=F32)
    w1r = w1.reshape(2, half, HEAD_DIM, CMP_HIDDEN)
    wbd = jnp.einsum('hjdn,gk->hjgdkn', w1r, eye).reshape(2, half, gw, g * CMP_HIDDEN).astype(BF16)
    w2bd = jnp.einsum('nd,gk->gnkd', w2, eye).reshape(g * CMP_HIDDEN, gw).astype(BF16)
    pe2 = jnp.broadcast_to(pe.reshape(2, half, 1, HEAD_DIM), (2, half, g, HEAD_DIM)).reshape(2, half, gw)

    def const(shape):
        return pl.BlockSpec(shape, lambda b: (0,) * len(shape))

    per_group = (HEAD_DIM, units) if transposed else (units, HEAD_DIM)
    return pl.pallas_call(
        functools.partial(_compress_kernel, transposed=transposed), grid=(batch,),
        in_specs=[pl.BlockSpec((seq, gw), lambda b: (b, col)),
                  const(wbd.shape[1:]), const(wbd.shape[1:]), const(pe2.shape), const(w2bd.shape)],
        out_specs=pl.BlockSpec((None, g) + per_group, lambda b: (b, 0, 0, 0)),
        out_shape=jax.ShapeDtypeStruct((batch, g) + per_group, BF16),
        compiler_params=_params("parallel"), name="nsa_compress",
    )(h, wbd[0], wbd[1], pe2, w2bd)


def _rot_half_pairs(x):
    lane = lax.broadcasted_iota(jnp.int32, x.shape, 1)
    fwd = pltpu.roll(x, 32, axis=1)
    bwd = pltpu.roll(x, 96, axis=1)
    return jnp.where((lane % HEAD_DIM) < HEAD_DIM // 2, bwd, fwd)


def _kprep_kernel(ks_ref, vs_ref, kw_ref, vw_ref, cos_ref, sin_ref, ksa_ref, vso_ref, kwo_ref, vwo_ref):
    t = ks_ref.shape[0]
    cos = cos_ref[...]
    sin = sin_ref[...]
    cos2 = jnp.concatenate([cos, cos], axis=1)
    sin2 = jnp.concatenate([sin, sin], axis=1)
    ks = ks_ref[...]
    kw = kw_ref[...]
    ks_r = ks * cos2 + _rot_half_pairs(ks) * sin2
    kw_r = kw * cos2 + _rot_half_pairs(kw) * sin2
    pos = pl.program_id(1) * t + lax.broadcasted_iota(jnp.int32, (t, HEAD_DIM), 0)
    onehot = (pos // SEL_BLOCK == lax.broadcasted_iota(jnp.int32, (t, HEAD_DIM), 1)).astype(F32)
    vs_t = vs_ref[...].T
    vw_t = vw_ref[...].T
    tail = (lax.broadcasted_iota(jnp.int32, (V_ROWS - HEAD_DIM, t), 0) == 0).astype(F32)
    for g in range(NSA_KV_GROUPS):
        sl = slice(g * HEAD_DIM, (g + 1) * HEAD_DIM)
        ksa_ref[g] = jnp.concatenate([ks_r[:, sl], onehot], axis=1).astype(BF16)
        vso_ref[g] = jnp.concatenate([vs_t[sl], tail], axis=0).astype(BF16)
        kwo_ref[g] = kw_r[:, sl].astype(BF16)
        vwo_ref[g] = jnp.concatenate([vw_t[sl], tail], axis=0).astype(BF16)


def nsa_kprep(h, cosf, sinf, batch, seq, col0):
    g = NSA_KV_GROUPS
    t = min(seq, 512)
    nt = seq // t

    def col(j):
        return pl.BlockSpec((t, LANES), lambda b, i, j=j: (b * nt + i, col0 + j))

    tab = pl.BlockSpec((t, HEAD_DIM), lambda b, i: (b * nt + i, 0))

    def out(wd):
        return pl.BlockSpec((None, g, t, wd), lambda b, i: (b, 0, i, 0))

    out_t = pl.BlockSpec((None, g, V_ROWS, t), lambda b, i: (b, 0, 0, i))
    k_shape = jax.ShapeDtypeStruct((batch, g, seq, HEAD_DIM), BF16)
    v_shape = jax.ShapeDtypeStruct((batch, g, V_ROWS, seq), BF16)
    return pl.pallas_call(
        _kprep_kernel, grid=(batch, nt),
        in_specs=[col(0), col(1), col(2), col(3), tab, tab],
        out_specs=[out(2 * HEAD_DIM), out_t, out(HEAD_DIM), out_t],
        out_shape=[jax.ShapeDtypeStruct((batch, g, seq, 2 * HEAD_DIM), BF16), v_shape, k_shape, v_shape],
        compiler_params=_params("parallel", "parallel"), name="nsa_kprep",
    )(h, h, h, h, cosf, sinf)


def _nsa_kernel(hq_ref, gate_ref, cos_ref, sin_ref, kc_ref, vc_ref, ksa_ref, vs_ref, kw_ref, vw_ref,
                ovl_ref, nw_ref, o_ref, *, tq, tk, n_sb):
    qi = pl.program_id(1)
    hpg = NSA_HPG
    groups = NSA_KV_GROUPS
    rows = hpg * tq
    t0 = qi * tq
    scale = 1.0 / math.sqrt(HEAD_DIM)
    half = HEAD_DIM // 2

    hq_t = hq_ref[...].T
    cos = cos_ref[...]
    sin = sin_ref[...]
    q_raw, q_rot = [], []
    for g in range(groups):
        raw_g, rot_g = [], []
        for h in range(hpg):
            r0 = (g * hpg + h) * HEAD_DIM
            qh = hq_t[r0:r0 + HEAD_DIM]
            swapped = jnp.concatenate([qh[half:], qh[:half]], axis=0)
            raw_g.append(qh * (scale * LOG2_E))
            rot_g.append((qh * cos + swapped * sin) * (scale * LOG2_E))
        q_raw.append(jnp.concatenate(raw_g, axis=1).astype(BF16))
        q_rot.append(jnp.concatenate(rot_g, axis=1))

    tpos = t0 + lax.broadcasted_iota(jnp.int32, (1, tq), 1)
    tpos_r = jnp.concatenate([tpos] * hpg, axis=1)

    def flash_steps(s, v_t, carry):
        m_new = [jnp.maximum(carry[g][0], jnp.max(s[g], axis=0, keepdims=True)) for g in range(groups)]
        p = [jnp.exp2(s[g] - m_new[g]).astype(BF16) for g in range(groups)]
        return tuple((m_new[g], jnp.exp2(carry[g][0] - m_new[g]) * carry[g][1] + _dot(v_t[g], p[g]))
                     for g in range(groups))

    init = ((jnp.full((1, rows), NEG_INF, F32), jnp.zeros((V_ROWS, rows), F32)),) * groups

    wk = WINDOW + tq
    kw0 = pl.multiple_of(jnp.maximum(t0 - WINDOW, 0), tq)
    kpos_w = kw0 + lax.broadcasted_iota(jnp.int32, (wk, 1), 0)
    mask_w = (kpos_w <= tpos_r) & (kpos_w > tpos_r - WINDOW)
    n_pad = jnp.maximum(WINDOW - 1 - tpos_r, 0).astype(F32)
    n_cmp = kc_ref.shape[1]
    cmp_end = lax.broadcasted_iota(jnp.int32, (n_cmp, 1), 0) * CMP_STRIDE + (CMP_BLOCK - 1)
    mask_c = cmp_end <= tpos_r
    blk = lax.broadcasted_iota(jnp.int32, (n_sb, tq), 0)
    cur = (t0 + lax.broadcasted_iota(jnp.int32, (n_sb, tq), 1)) // SEL_BLOCK
    s_w = [jnp.where(mask_w, _dot(kw_ref[g, pl.ds(kw0, wk), :], q_rot[g].astype(BF16)), NEG_INF)
           for g in range(groups)]
    s_c = [jnp.where(mask_c, _dot(kc_ref[g], q_raw[g]), NEG_INF) for g in range(groups)]
    win = flash_steps(s_w, [vw_ref[g, :, pl.ds(kw0, wk)] for g in range(groups)], init)
    o_w, o_c, imps = [], [], []
    for g in range(groups):
        m_w, acc_w = win[g]
        m_f = jnp.where(n_pad > 0.0, jnp.maximum(m_w, 0.0), m_w)
        a_w = jnp.exp2(m_w - m_f)
        o_w.append(acc_w[0:HEAD_DIM] * (a_w / (acc_w[HEAD_DIM:HEAD_DIM + 1] * a_w + n_pad * jnp.exp2(-m_f))))

    for g in range(groups):
        e_c = jnp.exp2(s_c[g] - jnp.max(s_c[g], axis=0, keepdims=True))
        p_c = jnp.where(mask_c, e_c * (1.0 / jnp.sum(e_c, axis=0, keepdims=True)), 0.0)
        o_c.append(_dot(vc_ref[g], p_c.astype(BF16)))

        p_sum = p_c[:, 0:tq]
        for h in range(1, hpg):
            p_sum = p_sum + p_c[:, h * tq:(h + 1) * tq]
        imp = jnp.dot(ovl_ref[...], p_sum, precision=HIGHEST, preferred_element_type=F32)
        key = pltpu.bitcast(jnp.maximum(imp, 0.0), jnp.int32)
        key = jnp.where((blk == 0) | (blk == cur) | (blk == cur - 1), FORCE_KEY, key)
        imps.append(jnp.where(blk > cur, -1, key))

    n_sel = min(N_SEL, n_sb)

    def bit_body(it, taus):
        bit = lax.shift_left(jnp.int32(1), 30 - it)
        out = []
        for g in range(groups):
            cand = taus[g] | bit
            cnt = jnp.sum(jnp.where(imps[g] >= cand, 1, 0), axis=0, keepdims=True)
            out.append(jnp.where(cnt >= n_sel, cand, taus[g]))
        return tuple(out)

    taus = lax.fori_loop(0, 31, bit_body, (jnp.zeros((1, tq), jnp.int32),) * groups)
    lower = (lax.broadcasted_iota(jnp.int32, (n_sb, n_sb), 0)
             > lax.broadcasted_iota(jnp.int32, (n_sb, n_sb), 1)).astype(BF16)
    q_aug = []
    for g in range(groups):
        above = imps[g] > taus[g]
        equal = imps[g] == taus[g]
        need = n_sel - jnp.sum(jnp.where(above, 1, 0), axis=0, keepdims=True)
        earlier = _dot(lower, jnp.where(equal, 1.0, 0.0).astype(BF16))
        selected = above | (equal & (earlier < need.astype(F32)))
        sel_bias = jnp.where(selected, 0.0, NEG_INF)
        if n_sb < HEAD_DIM:
            sel_bias = jnp.concatenate([sel_bias, jnp.zeros((HEAD_DIM - n_sb, tq), F32)], axis=0)
        q_aug.append(jnp.concatenate([q_rot[g], jnp.concatenate([sel_bias] * hpg, axis=1)], axis=0).astype(BF16))

    def scores(k0, width):
        return [_dot(ksa_ref[g, pl.ds(k0, width), :], q_aug[g]) for g in range(groups)]

    def values(k0, width):
        return [vs_ref[g, :, pl.ds(k0, width)] for g in range(groups)]

    def wide_body(kt, carry):
        k0 = pl.multiple_of(kt * tk, tk)
        return flash_steps(scores(k0, tk), values(k0, tk), carry)

    n_wide = t0 // tk
    carry = lax.fori_loop(0, n_wide, wide_body, init)

    def narrow_body(j, carry):
        k0 = pl.multiple_of(n_wide * tk + j * tq, tq)
        return flash_steps(scores(k0, tq), values(k0, tq), carry)

    carry = lax.fori_loop(0, (t0 - n_wide * tk) // tq, narrow_body, carry)
    k0 = pl.multiple_of(t0, tq)
    mask_s = k0 + lax.broadcasted_iota(jnp.int32, (tq, 1), 0) <= tpos_r
    carry = flash_steps([jnp.where(mask_s, s, NEG_INF) for s in scores(k0, tq)], values(k0, tq), carry)

    gates = _sigmoid(gate_ref[...].T)
    nw = nw_ref[...]
    outs = []
    for g in range(groups):
        acc_s = carry[g][1]
        o_s = acc_s[0:HEAD_DIM] * (1.0 / acc_s[HEAD_DIM:HEAD_DIM + 1])
        for h in range(hpg):
            sl = slice(h * tq, (h + 1) * tq)
            r = (g * hpg + h) * N_GATES
            o = gates[r:r + 1] * o_c[g][:, sl] + gates[r + 1:r + 2] * o_s[:, sl] + gates[r + 2:r + 3] * o_w[g][:, sl]
            ms = jnp.mean(o * o, axis=0, keepdims=True)
            outs.append(o * lax.rsqrt(ms + RMS_EPS) * nw[:, g * hpg + h:g * hpg + h + 1])
    o_ref[...] = jnp.concatenate(outs, axis=0).T.astype(o_ref.dtype)


def nsa_attention(h, cos_t, sin_t, kc, vc_t, ksa, vs_t, kw, vw_t, norm_w, batch, seq, q_col0, gate_col):
    g, hpg = NSA_KV_GROUPS, NSA_HPG
    tq = min(seq, 256)
    nq = seq // tq
    n_sb = seq // SEL_BLOCK
    assert n_sb <= HEAD_DIM, "selection-block one-hot shares the 64 spare key lanes"
    n_cmp = kc.shape[2]
    units = np.arange(n_cmp)[:, None] + np.arange(CMP_BLOCK // CMP_STRIDE)[None, :]
    ovl = np.zeros((n_cmp, n_sb), np.float32)
    for c in range((seq - CMP_BLOCK) // CMP_STRIDE + 1):
        for u in units[c]:
            ovl[c, u // (SEL_BLOCK // CMP_STRIDE)] += 1.0
    ovl_t = jnp.asarray(ovl.T)

    def per_b(shape):
        return pl.BlockSpec((None, g) + shape, lambda b, qi: (b, 0, 0, 0))

    width = g * hpg * HEAD_DIM
    tab = pl.BlockSpec((HEAD_DIM, tq), lambda b, qi: (0, b * nq + qi))
    tk = min(seq, 512)
    assert seq >= WINDOW + tq and seq % tk == 0 and tk % tq == 0
    kern = functools.partial(_nsa_kernel, tq=tq, tk=tk, n_sb=n_sb)
    return pl.pallas_call(
        kern, grid=(batch, nq),
        in_specs=[pl.BlockSpec((tq, width), lambda b, qi: (b * nq + qi, q_col0)),
                  pl.BlockSpec((tq, LANES), lambda b, qi: (b * nq + qi, gate_col)),
                  tab, tab,
                  per_b((n_cmp, HEAD_DIM)), per_b((HEAD_DIM, n_cmp)),
                  per_b((seq, 2 * HEAD_DIM)), per_b((V_ROWS, seq)),
                  per_b((seq, HEAD_DIM)), per_b((V_ROWS, seq)),
                  pl.BlockSpec((n_sb, n_cmp), lambda b, qi: (0, 0)),
                  pl.BlockSpec((HEAD_DIM, g * hpg), lambda b, qi: (0, 0))],
        out_specs=pl.BlockSpec((tq, width), lambda b, qi: (b * nq + qi, 0)),
        out_shape=jax.ShapeDtypeStruct((batch * seq, width), BF16),
        compiler_params=_params("parallel", "arbitrary"), name="nsa_attention",
    )(h, h, cos_t, sin_t, kc, vc_t, ksa, vs_t, kw, vw_t, ovl_t, norm_w.reshape(g * hpg, HEAD_DIM).T)


def _out_proj_kernel(x_ref, yhg_ref, ygm_ref, ynsa_ref, whg_ref, wgm_ref, wnsa_ref, lnw_ref, lnb_ref,
                     o_ref, oa_ref, ob_ref, *, alpha):
    mix = (_dot(yhg_ref[...], whg_ref[...]) + _dot(ygm_ref[...], wgm_ref[...])
           + _dot(ynsa_ref[...], wnsa_ref[...]))
    y = _layer_norm(alpha * x_ref[...] + mix, lnw_ref[...], lnb_ref[...])
    o_ref[...] = y
    _store_word_tables((oa_ref, ob_ref), _pack_bf16_pairs(y))


def out_proj_ln(x2d, y_hg, y_gm, y_nsa, w_out, ln_w, ln_b, alpha):
    n, d = x2d.shape
    w1, w2 = y_hg.shape[1], y_hg.shape[1] + y_gm.shape[1]
    whg = w_out[:w1].astype(BF16)
    wgm = w_out[w1:w2].astype(BF16)
    wnsa = w_out[w2:].astype(BF16)
    t = min(n, 512)

    def row(wd):
        return pl.BlockSpec((t, wd), lambda i: (i, 0))

    def const(shape):
        return pl.BlockSpec(shape, lambda i: (0,) * len(shape))

    kern = functools.partial(_out_proj_kernel, alpha=alpha)
    return pl.pallas_call(
        kern, grid=(n // t,),
        in_specs=[row(d), row(y_hg.shape[1]), row(y_gm.shape[1]), row(y_nsa.shape[1]),
                  const(whg.shape), const(wgm.shape), const(wnsa.shape), const((1, d)), const((1, d))],
        out_specs=[row(d), row(SC_ROW_WORDS), row(SC_ROW_WORDS)],
        out_shape=[jax.ShapeDtypeStruct((n, d), F32)] + [jax.ShapeDtypeStruct((n, SC_ROW_WORDS), jnp.uint32)] * 2,
        compiler_params=_params("parallel"), name="out_proj_ln",
    )(x2d, y_hg, y_gm, y_nsa, whg, wgm, wnsa, ln_w.reshape(1, d), ln_b.reshape(1, d))


def _router_kernel(x_ref, w_ref, b_ref, e_ref, p_ref, r_ref, cnt_ref, carry_ref):
    t = x_ref.shape[0]

    @pl.when(pl.program_id(0) == 0)
    def _():
        carry_ref[...] = jnp.zeros_like(carry_ref)

    x = x_ref[...]
    x_hi = x.astype(BF16)
    x_lo = (x - x_hi.astype(F32)).astype(BF16)
    w = w_ref[...]
    w_hi = w.astype(BF16)
    w_lo = (w - w_hi.astype(F32)).astype(BF16)
    logits = _dot_nt(w_hi, x_hi) + (_dot_nt(w_hi, x_lo) + _dot_nt(w_lo, x_hi)) + b_ref[...]
    n_e = logits.shape[0]
    sub = lax.broadcasted_iota(jnp.int32, logits.shape, 0)
    work = logits
    vals, idxs = [], []
    sel = jnp.zeros(logits.shape, F32)
    for _ in range(TOP_K):
        m = jnp.max(work, axis=0, keepdims=True)
        idx = jnp.min(jnp.where(work == m, sub, n_e), axis=0, keepdims=True)
        hit = sub == idx
        sel = jnp.where(hit, 1.0, sel)
        work = jnp.where(hit, -jnp.inf, work)
        vals.append(m)
        idxs.append(idx)
    exps = [jnp.exp(v - vals[0]) for v in vals]
    inv_den = 1.0 / (exps[0] + exps[1] + exps[2] + exps[3])
    earlier = (lax.broadcasted_iota(jnp.int32, (t, t), 0) < lax.broadcasted_iota(jnp.int32, (t, t), 1))
    before = _dot(sel.astype(BF16), earlier.astype(BF16)) + carry_ref[...]
    ranks = [jnp.sum(jnp.where(sub == idx, before, 0.0), axis=0, keepdims=True) for idx in idxs]
    e_ref[...] = jnp.concatenate(idxs, axis=0)
    p_ref[...] = jnp.concatenate([e * inv_den for e in exps], axis=0)
    r_ref[...] = jnp.concatenate(ranks, axis=0).astype(jnp.int32)
    carry_ref[...] = carry_ref[...] + jnp.sum(sel, axis=1, keepdims=True)
    cnt_ref[...] = carry_ref[...].astype(jnp.int32)


def moe_router(x2d, router_w, router_b):
    n, d = x2d.shape
    e = router_w.shape[1]
    t = min(n, 512)
    row4 = pl.BlockSpec((TOP_K, t), lambda i: (0, i))
    top_e, top_p, rank, counts = pl.pallas_call(
        _router_kernel, grid=(n // t,),
        in_specs=[pl.BlockSpec((t, d), lambda i: (i, 0)), pl.BlockSpec((e, d), lambda i: (0, 0)),
                  pl.BlockSpec((e, 1), lambda i: (0, 0))],
        out_specs=[row4, row4, row4, pl.BlockSpec((e, 1), lambda i: (0, 0))],
        out_shape=[jax.ShapeDtypeStruct((TOP_K, n), jnp.int32), jax.ShapeDtypeStruct((TOP_K, n), F32),
                   jax.ShapeDtypeStruct((TOP_K, n), jnp.int32), jax.ShapeDtypeStruct((e, 1), jnp.int32)],
        scratch_shapes=[pltpu.VMEM((e, 1), F32)],
        compiler_params=_params("arbitrary"), name="moe_router",
    )(x2d, router_w.T, router_b.reshape(e, 1))
    return top_e, top_p, rank, counts[:, 0]


def _expert_kernel(be_ref, valid_ref, xa_ref, xb_ref, wu_ref, bu_ref, wd_ref, bd_ref, oa_ref, ob_ref,
                   wu_bf, wd_bf):
    i = pl.program_id(0)
    f = wd_ref.shape[0]
    n_used = be_ref[pl.num_programs(0)]

    @pl.when((i == 0) | (be_ref[i] != be_ref[jnp.maximum(i - 1, 0)]))
    def _():
        wu_bf[...] = wu_ref[...].astype(BF16)
        wd_bf[...] = wd_ref[...].astype(BF16)

    @pl.when(i < n_used)
    def _():
        packed = jnp.concatenate([xa_ref[...], xb_ref[...]], axis=1)
        live = lax.broadcasted_iota(jnp.int32, packed.shape, 0) < valid_ref[i]
        x_lo, x_hi = _unpack_bf16_pairs(jnp.where(live, packed, jnp.uint32(0)))
        x = jnp.concatenate([x_lo.astype(BF16), x_hi.astype(BF16)], axis=1)
        hcat = _dot(x, wu_bf[...]) + bu_ref[...]
        glu = jnp.minimum(hcat[:, :f], SWIGLU_LIMIT)
        lin = jnp.clip(hcat[:, f:], -SWIGLU_LIMIT, SWIGLU_LIMIT)
        act = glu * _sigmoid(SWIGLU_ALPHA * glu) * (lin + 1.0)
        _store_word_tables((oa_ref, ob_ref), _pack_bf16_pairs(_dot(act.astype(BF16), wd_bf[...]) + bd_ref[...]))

    @pl.when(i >= n_used)
    def _():
        oa_ref[...] = jnp.zeros_like(oa_ref)
        ob_ref[...] = jnp.zeros_like(ob_ref)


def moe_experts(xa, xb, block_e, n_used, block_valid, w_up, b_up, w_down, b_down, layer):
    rows = xa.shape[0]
    _, e, d, f2 = w_up.shape
    f = f2 // 2
    nb = rows // EXPERT_BLOCK
    words = pl.BlockSpec((EXPERT_BLOCK, SC_ROW_WORDS), lambda i, be, nv: (i, 0))
    grid_spec = pltpu.PrefetchScalarGridSpec(
        num_scalar_prefetch=2, grid=(nb,),
        in_specs=[words, words,
                  pl.BlockSpec((None, None, d, f2), lambda i, be, nv: (layer, be[i], 0, 0)),
                  pl.BlockSpec((None, None, 1, f2), lambda i, be, nv: (layer, be[i], 0, 0)),
                  pl.BlockSpec((None, None, f, d), lambda i, be, nv: (layer, be[i], 0, 0)),
                  pl.BlockSpec((None, None, 1, d), lambda i, be, nv: (layer, be[i], 0, 0))],
        out_specs=[words, words],
        scratch_shapes=[pltpu.VMEM((d, f2), BF16), pltpu.VMEM((f, d), BF16)])
    depth = w_up.shape[0]
    return pl.pallas_call(
        _expert_kernel, grid_spec=grid_spec,
        out_shape=[jax.ShapeDtypeStruct((rows, SC_ROW_WORDS), jnp.uint32)] * 2,
        compiler_params=pltpu.CompilerParams(dimension_semantics=("arbitrary",), vmem_limit_bytes=EXPERT_VMEM_LIMIT),
        name="moe_experts",
    )(jnp.concatenate([block_e, n_used.reshape(1)]), block_valid, xa, xb, w_up, b_up.reshape(depth, e, 1, f2),
      w_down, b_down.reshape(depth, e, 1, d))


def _combine_kernel(x_ref, ya_ref, yb_ref, p_ref, lnw_ref, lnb_ref, o_ref, *, alpha):
    p = p_ref[...]
    moe = jnp.zeros(x_ref.shape, F32)
    for k in range(TOP_K):
        y_lo, y_hi = _unpack_bf16_pairs(jnp.concatenate([ya_ref[k], yb_ref[k]], axis=1))
        moe = moe + p[:, k:k + 1] * jnp.concatenate([y_lo, y_hi], axis=1)
    o_ref[...] = _layer_norm(alpha * x_ref[...] + moe, lnw_ref[...], lnb_ref[...])


def combine_ln(x2d, ya, yb, top_p, ln_w, ln_b, alpha):
    n, d = x2d.shape
    t = min(n, 512)
    deep = pl.Buffered(COMBINE_INPUT_BUFFERS)
    words = pl.BlockSpec((TOP_K, t, SC_ROW_WORDS), lambda i: (0, i, 0), pipeline_mode=deep)

    def streamed(x_hbm, ya_hbm, yb_hbm, p_hbm, lnw_ref, lnb_ref, o_hbm):
        def body(x_ref, ya_ref, yb_ref, p_ref, o_ref):
            _combine_kernel(x_ref, ya_ref, yb_ref, p_ref, lnw_ref, lnb_ref, o_ref, alpha=alpha)

        pltpu.emit_pipeline(
            body, grid=(n // t,),
            in_specs=[pl.BlockSpec((t, d), lambda i: (i, 0), pipeline_mode=deep), words, words,
                      pl.BlockSpec((t, TOP_K), lambda i: (i, 0))],
            out_specs=[pl.BlockSpec((t, d), lambda i: (i, 0))],
        )(x_hbm, ya_hbm, yb_hbm, p_hbm, o_hbm)

    hbm = pl.BlockSpec(memory_space=pl.ANY)
    whole = pl.BlockSpec(memory_space=pltpu.VMEM)
    return pl.pallas_call(
        streamed, in_specs=[hbm, hbm, hbm, hbm, whole, whole], out_specs=hbm,
        out_shape=jax.ShapeDtypeStruct((n, d), F32),
        compiler_params=pltpu.CompilerParams(vmem_limit_bytes=VMEM_LIMIT), name="moe_combine_ln",
    )(x2d, ya, yb, top_p, ln_w.reshape(1, d), ln_b.reshape(1, d))


def _sc_mesh():
    return plsc.VectorSubcoreMesh(core_axis_name="core", subcore_axis_name="subcore")


def sc_gather_rows(tables, idx):
    r = idx.shape[0]
    nt = len(tables)
    out = jax.ShapeDtypeStruct((r, SC_ROW_WORDS), tables[0].dtype)

    @pl.kernel(out_type=(out,) * nt, mesh=_sc_mesh(), name="sc_gather_rows")
    def gather(*refs):
        x_hbm, i_hbm, o_hbm = refs[:nt], refs[nt], refs[nt + 1:]
        for j in range(nt):
            def body(i_vmem, o_vmem, table=x_hbm[j]):
                pltpu.sync_copy(table.at[i_vmem.at[0]], o_vmem)

            pltpu.emit_pipeline(
                body, grid=(r // SC_WINDOW,),
                in_specs=[pl.BlockSpec((1, SC_WINDOW), lambda i: (0, i))],
                out_specs=[pl.BlockSpec((SC_WINDOW, SC_ROW_WORDS), lambda i: (i, 0))],
                core_axis_name=("core", "subcore"), dimension_semantics=(pltpu.PARALLEL,),
            )(i_hbm, o_hbm[j])

    return gather(*tables, idx.reshape(1, r))


def sc_scatter_rows(tables, dest_t, n_rows):
    n = tables[0].shape[0]
    nt = len(tables)
    copies = dest_t.shape[0]
    out = jax.ShapeDtypeStruct((n_rows, SC_ROW_WORDS), tables[0].dtype)

    @pl.kernel(out_type=(out,) * nt, mesh=_sc_mesh(), scratch_types=[], name="sc_scatter_rows")
    def scatter(*refs):
        x_hbm, i_hbm, o_hbm = refs[:nt], refs[nt], refs[nt + 1:]
        for j in range(nt):
            def body(x_vmem, i_vmem, out_j=o_hbm[j]):
                for k in range(copies):
                    pltpu.sync_copy(x_vmem, out_j.at[i_vmem.at[k]])

            pltpu.emit_pipeline(
                body, grid=(n // SC_WINDOW,),
                in_specs=[pl.BlockSpec((SC_WINDOW, SC_ROW_WORDS), lambda i: (i, 0)),
                          pl.BlockSpec((copies, SC_WINDOW), lambda i: (0, i))],
                out_specs=[],
                core_axis_name=("core", "subcore"), dimension_semantics=(pltpu.PARALLEL,),
            )(x_hbm[j], i_hbm)

    return scatter(*tables, dest_t)


def moe_ffn_ln(x_f32, x_packed, router_w, router_b, w_up, b_up, w_down, b_down, layer, ln_w, ln_b, alpha):
    n, d = x_f32.shape
    top_e, top_p, rank, counts = moe_router(x_f32, router_w, router_b)
    padded = (counts + EXPERT_BLOCK - 1) // EXPERT_BLOCK * EXPERT_BLOCK
    pad_end = jnp.cumsum(padded)
    pad_start = pad_end - padded
    n_assign = n * TOP_K
    n_blocks = -(-(n_assign + N_EXPERTS * (EXPERT_BLOCK - 1)) // EXPERT_BLOCK)
    experts = jnp.arange(N_EXPERTS, dtype=jnp.int32)
    dest_t = rank + jnp.sum(jnp.where(top_e[..., None] == experts, pad_start.astype(jnp.int32), 0), axis=-1)
    block_first = jnp.arange(n_blocks, dtype=jnp.int32) * EXPERT_BLOCK
    block_e = jnp.clip(jnp.sum((pad_end[None, :] <= block_first[:, None]).astype(jnp.int32), axis=1),
                       0, N_EXPERTS - 1)
    block_valid = jnp.clip(counts[block_e] - (block_first - pad_start[block_e]), 0, EXPERT_BLOCK)
    n_used = (pad_end[-1] // EXPERT_BLOCK).astype(jnp.int32)
    xa, xb = sc_scatter_rows(x_packed, dest_t, n_blocks * EXPERT_BLOCK)
    ya, yb = moe_experts(xa, xb, block_e, n_used, block_valid.astype(jnp.int32), w_up, b_up, w_down, b_down,
                         layer)
    ya, yb = sc_gather_rows((ya, yb), dest_t.reshape(-1))
    return combine_ln(x_f32, ya.reshape(TOP_K, n, SC_ROW_WORDS), yb.reshape(TOP_K, n, SC_ROW_WORDS), top_p.T,
                      ln_w, ln_b, alpha)


def kernel(x, positions, w_in, hg_lower_bounds, hg_norm_w, gm_ln_w, gm_ln_b, gm_spatial_w, gm_spatial_b, gm_norm_w, nsa_cmp_pe, nsa_cmp_w1, nsa_cmp_w2, nsa_norm_w, w_out, ln1_w, ln1_b, router_w, router_b, exp_w_up, exp_b_up, exp_w_down, exp_b_down, ln2_w, ln2_b):
    batch, seq, d = x.shape
    depth = w_in.shape[0]
    n = batch * seq
    alpha = (2 * depth) ** 0.25
    hg_w = hg_norm_w.shape[1]
    gm_w = gm_norm_w.shape[1]
    nsa_w = nsa_norm_w.shape[1]
    kv_w = NSA_KV_GROUPS * HEAD_DIM
    in_width = w_in.shape[2]
    off_gm = 4 * hg_w
    off_q = off_gm + 2 * gm_w
    off_kv = off_q + nsa_w
    off_gate = off_kv + 6 * kv_w
    width_pad = -(-in_width // LANES) * LANES

    cosf, sinf, cos_t, sin_t = rope_tables(positions)
    lb_all = jnp.cumsum(jax.nn.softmax(hg_lower_bounds.astype(F32), axis=0), axis=0)
    lb_all = lb_all - lb_all[0:1]

    x2d = x.reshape(n, d)
    for l in range(depth):
        w_l = jnp.pad(w_in[l], ((0, 0), (0, width_pad - in_width))).astype(BF16)
        h = in_proj(x2d, w_l)
        h3 = h.reshape(batch, seq, width_pad)
        y_hg = hgrn2(h3, lb_all[l], hg_norm_w[l]).reshape(n, hg_w)
        y_gm = gmlp(h, gm_ln_w[l], gm_ln_b[l], gm_spatial_w[l], gm_spatial_b[l], gm_norm_w[l],
                    off_gm // gm_w, off_gm // gm_w + 1)
        kc = compress(h, off_kv // kv_w, nsa_cmp_pe[l, 0], nsa_cmp_w1[l, 0], nsa_cmp_w2[l, 0], batch, seq, False)
        vc_t = compress(h, off_kv // kv_w + 1, nsa_cmp_pe[l, 1], nsa_cmp_w1[l, 1], nsa_cmp_w2[l, 1], batch, seq,
                        True)
        ksa, vs_t, kw, vw_t = nsa_kprep(h, cosf, sinf, batch, seq, (off_kv + 2 * kv_w) // LANES)
        y_nsa = nsa_attention(h, cos_t, sin_t, kc, vc_t, ksa, vs_t, kw, vw_t, nsa_norm_w[l], batch, seq,
                              off_q // nsa_w, off_gate // LANES)
        x1, x1a, x1b = out_proj_ln(x2d, y_hg, y_gm, y_nsa, w_out[l], ln1_w[l], ln1_b[l], alpha)
        x2d = moe_ffn_ln(x1, (x1a, x1b), router_w[l], router_b[l], exp_w_up, exp_b_up, exp_w_down, exp_b_down, l,
                         ln2_w[l], ln2_b[l], alpha)
    return x2d.reshape(batch, seq, d)
```

```python
import functools
import math

import numpy as np
import jax
import jax.numpy as jnp
from jax import lax
from jax.experimental import pallas as pl
from jax.experimental.pallas import tpu as pltpu
from jax.experimental.pallas import tpu_sc as plsc

F32 = jnp.float32
BF16 = jnp.bfloat16
HIGHEST = lax.Precision.HIGHEST

HEAD_DIM = 64
LANES = 128
VMEM_LIMIT = 48 * 1024 * 1024
EXPERT_VMEM_LIMIT = 56 * 1024 * 1024

HG_CHUNK = 64
GM_CHUNK = 128
GM_TILE_CHUNKS = 4
NSA_KV_GROUPS = 2
NSA_HPG = 4
CMP_BLOCK = 32
CMP_STRIDE = 16
CMP_HIDDEN = 128
SEL_BLOCK = 64
N_SEL = 16
WINDOW = 512
N_GATES = 3
IMP_FORCE = 1e9
FORCE_KEY = int(np.float32(IMP_FORCE).view(np.int32))
NEG_INF = -1e30
N_EXPERTS = 32
TOP_K = 4
SWIGLU_ALPHA = 1.702
SWIGLU_LIMIT = 7.0
EXPERT_BLOCK = 512
SC_ROW_WORDS = 256
SC_WINDOW = 128
ROPE_THETA = 10000.0
LOG2_E = 1.4426950408889634
LN_EPS = 1e-5
RMS_EPS = 1e-6
V_ROWS = HEAD_DIM + 16


def _params(*sem):
    return pltpu.CompilerParams(dimension_semantics=sem, vmem_limit_bytes=VMEM_LIMIT)


def _dot(a, b):
    return jnp.dot(a, b, preferred_element_type=F32)


def _dot_nt(a, b, precision=None):
    return lax.dot_general(a, b, (((1,), (1,)), ((), ())), precision=precision,
                           preferred_element_type=F32)


def _dot_tn(a, b):
    return lax.dot_general(a, b, (((0,), (0,)), ((), ())), preferred_element_type=F32)


def _sigmoid(x):
    return 1.0 / (1.0 + jnp.exp(-x))


def _gelu(x):
    return 0.5 * x * (1.0 + jnp.tanh(0.7978845608028654 * (x + 0.044715 * x * x * x)))


def _layer_norm(x, w, b):
    mu = jnp.mean(x, axis=-1, keepdims=True)
    xc = x - mu
    var = jnp.mean(xc * xc, axis=-1, keepdims=True)
    return xc * lax.rsqrt(var + LN_EPS) * w + b


def _pack_bf16_pairs(y):
    w = y.shape[1] // 2
    bits = pltpu.bitcast(y.astype(BF16).astype(F32), jnp.uint32)
    return lax.shift_right_logical(bits[:, :w], jnp.uint32(16)) | (bits[:, w:] & jnp.uint32(0xFFFF0000))


def _unpack_bf16_pairs(u):
    lo = pltpu.bitcast(lax.shift_left(u, jnp.uint32(16)), F32)
    hi = pltpu.bitcast(u & jnp.uint32(0xFFFF0000), F32)
    return lo, hi


def _store_word_tables(refs, packed):
    for j, ref in enumerate(refs):
        ref[...] = packed[:, j * SC_ROW_WORDS:(j + 1) * SC_ROW_WORDS]


def _head_mean_sq(o, bd_ones):
    sq = o * o
    hi = sq.astype(BF16)
    lo = (sq - hi.astype(F32)).astype(BF16)
    ones = bd_ones.astype(BF16)
    return (_dot(hi, ones) + _dot(lo, ones)) * (1.0 / HEAD_DIM)


def _rope_kernel(pos_ref, inv_ref, cos_ref, sin_ref, cost_ref, sint_ref):
    ang = inv_ref[...] * pos_ref[...]
    c = jnp.cos(ang)
    s = jnp.sin(ang)
    cos_t = jnp.concatenate([c, c], axis=0)
    sin_t = jnp.concatenate([-s, s], axis=0)
    cost_ref[...] = cos_t
    sint_ref[...] = sin_t
    cos_ref[...] = cos_t.T
    sin_ref[...] = sin_t.T


def rope_tables(positions):
    n = positions.size
    tile = min(n, 2048)
    posf = positions.reshape(1, n).astype(F32)
    inv = ROPE_THETA ** (-jnp.arange(0, HEAD_DIM, 2, dtype=F32) / HEAD_DIM)
    row = pl.BlockSpec((tile, HEAD_DIM), lambda i: (i, 0))
    rowt = pl.BlockSpec((HEAD_DIM, tile), lambda i: (0, i))
    return pl.pallas_call(
        _rope_kernel, grid=(n // tile,),
        in_specs=[pl.BlockSpec((1, tile), lambda i: (0, i)), pl.BlockSpec((HEAD_DIM // 2, 1), lambda i: (0, 0))],
        out_specs=[row, row, rowt, rowt],
        out_shape=[jax.ShapeDtypeStruct((n, HEAD_DIM), F32)] * 2 + [jax.ShapeDtypeStruct((HEAD_DIM, n), F32)] * 2,
        compiler_params=_params("parallel"), name="rope_tables",
    )(posf, inv.reshape(HEAD_DIM // 2, 1))


def _in_proj_kernel(x_ref, w_ref, h_ref):
    h_ref[...] = _dot(x_ref[...].astype(BF16), w_ref[...])


def in_proj(x2d, w_bf16):
    n, d = x2d.shape
    width = w_bf16.shape[1]
    tile = min(n, 512)
    return pl.pallas_call(
        _in_proj_kernel, grid=(n // tile,),
        in_specs=[pl.BlockSpec((tile, d), lambda i: (i, 0)), pl.BlockSpec((d, width), lambda i: (0, 0))],
        out_specs=pl.BlockSpec((tile, width), lambda i: (i, 0)),
        out_shape=jax.ShapeDtypeStruct((n, width), F32),
        compiler_params=_params("parallel"), name="in_proj")(x2d, w_bf16)


HG_LEVELS = (64, 32, 16, 8, 4, 2)
HG_BATCH = 8


def _hgrn_constants():
    c = HG_CHUNK
    t = np.arange(c)
    u = t[None, :]
    rows = [u <= t[:, None], u > t[:, None]]
    masks = [np.eye(c, dtype=bool)]
    for m in HG_LEVELS:
        ref = ((t // m) * m + m // 2 - 1)[:, None]
        second = (t % m >= m // 2)[:, None]
        rows.append(((u > ref) & (u <= t[:, None]) & second) | ((u > t[:, None]) & (u <= ref) & ~second))
        masks.append((t[:, None] // m == t[None, :] // m) & second & (t[None, :] % m < m // 2))
    pmat = np.concatenate(rows, axis=0).astype(np.float32)
    masks = np.stack([np.tile(mk, (1, 4)) for mk in masks]).astype(np.float32)
    return pmat, masks


def _hgrn_kernel(q_ref, f_ref, i_ref, g_ref, lb_ref, nw_ref, pmat_ref, masks_ref, bd_ref, hm_ref,
                 o_ref, state_ref):
    c = HG_CHUNK

    @pl.when(pl.program_id(1) == 0)
    def _():
        state_ref[...] = jnp.zeros_like(state_ref)

    lb = lb_ref[...]
    bd = bd_ref[...]
    hm = hm_ref[...]
    hm_tiles = [jnp.broadcast_to(hm[h:h + 1], (c, hm.shape[1])).astype(BF16) for h in range(4)]
    pmat = pmat_ref[...]
    a = jnp.log(lb)
    log1m = jnp.log(1.0 - lb)
    nb, _, w = q_ref.shape
    seqs = range(nb)

    fr = f_ref[...].reshape(nb * c, w)
    hq = q_ref[...].reshape(nb * c, w)
    v = i_ref[...].reshape(nb * c, w)
    qf = hq * _sigmoid(hq)
    log_sig = jnp.minimum(fr, 0.0) - jnp.log(1.0 + jnp.exp(-jnp.abs(fr)))
    cc = log1m + log_sig
    log_f = jnp.maximum(a, cc) + jnp.log(1.0 + jnp.exp(-jnp.abs(a - cc)))
    kk = (1.0 - lb) * _sigmoid(-fr)

    hi = log_f.astype(BF16)
    lo = (log_f - hi.astype(F32)).astype(BF16)

    def side_by_side(x):
        return jnp.concatenate([x[b * c:(b + 1) * c] for b in seqs], axis=1)

    sums = jnp.minimum(_dot(pmat, side_by_side(hi)) + _dot(pmat, side_by_side(lo)), 0.0)
    e_all = jnp.exp(sums)

    def e_rows(r, b):
        return e_all[r * c:(r + 1) * c, b * w:(b + 1) * w]

    def stacked(x):
        xb = x.astype(BF16)
        return jnp.concatenate([xb * hm_tiles[h] for h in range(4)], axis=0)

    q_s = [qf[b * c:(b + 1) * c] for b in seqs]
    k_s = [kk[b * c:(b + 1) * c] for b in seqs]
    v_s = [v[b * c:(b + 1) * c] for b in seqs]

    att = [masks_ref[0] * _dot_nt(q_s[b].astype(BF16), stacked(k_s[b])) for b in seqs]
    for li in range(len(HG_LEVELS)):
        for b in seqs:
            e_l = e_rows(2 + li, b)
            att[b] = att[b] + masks_ref[li + 1] * _dot_nt((q_s[b] * e_l).astype(BF16), stacked(k_s[b] * e_l))

    outs = []
    for b in seqs:
        e_b = e_rows(0, b)
        st = state_ref[b]
        o = _dot(att[b].astype(BF16), stacked(v_s[b])) + _dot_nt((q_s[b] * e_b).astype(BF16), st.astype(BF16))
        k_rest = (k_s[b] * e_rows(1, b)).astype(BF16)
        state_ref[b] = st * e_b[c - 1:c] + bd * _dot_tn(v_s[b].astype(BF16), k_rest)
        outs.append(o)

    o = jnp.concatenate(outs, axis=0)
    ms = _head_mean_sq(o, bd)
    y = o * lax.rsqrt(ms + RMS_EPS) * nw_ref[...] * _sigmoid(g_ref[...].reshape(nb * c, w))
    o_ref[...] = y.astype(o_ref.dtype).reshape(nb, c, w)


def hgrn2(h3, lb, norm_w):
    batch, seq, _ = h3.shape
    w = lb.shape[-1]
    c = HG_CHUNK
    nb = math.gcd(batch, HG_BATCH)
    pmat, masks = _hgrn_constants()
    lane_head = np.arange(w) // HEAD_DIM
    bd = (lane_head[:, None] == lane_head[None, :]).astype(np.float32)
    hm = (np.arange(4)[:, None] == lane_head[None, :]).astype(np.float32)

    def col(j):
        return pl.BlockSpec((nb, c, w), lambda b, i, j=j: (b, i, j))

    def const(shape):
        return pl.BlockSpec(shape, lambda b, i: (0,) * len(shape))

    return pl.pallas_call(
        _hgrn_kernel, grid=(batch // nb, seq // c),
        in_specs=[col(0), col(1), col(2), col(3), const((1, w)), const((1, w)),
                  const(pmat.shape), const(masks.shape), const(bd.shape), const(hm.shape)],
        out_specs=pl.BlockSpec((nb, c, w), lambda b, i: (b, i, 0)),
        out_shape=jax.ShapeDtypeStruct((batch, seq, w), BF16),
        scratch_shapes=[pltpu.VMEM((nb, w, w), F32)],
        compiler_params=_params("parallel", "arbitrary"), name="hgrn2",
    )(h3, h3, h3, h3, lb.reshape(1, w), norm_w.reshape(1, w), jnp.asarray(pmat, BF16), jnp.asarray(masks),
      jnp.asarray(bd), jnp.asarray(hm))


def _gmlp_kernel(u_ref, v_ref, lnw_ref, lnb_ref, ws_ref, bias_ref, nw_ref, bd_ref, hm_ref, o_ref):
    c = GM_CHUNK
    groups = ws_ref.shape[0]
    u = _gelu(u_ref[...])
    v = _layer_norm(_gelu(v_ref[...]), lnw_ref[...], lnb_ref[...])
    hm = hm_ref[...]
    bd = bd_ref[...]
    causal = lax.broadcasted_iota(jnp.int32, (c, c), 0) >= lax.broadcasted_iota(jnp.int32, (c, c), 1)
    w_cat = jnp.concatenate([jnp.where(causal, ws_ref[g], 0.0).astype(BF16) for g in range(groups)], axis=1)
    for j in range(u.shape[0] // c):
        rows = slice(j * c, (j + 1) * c)
        v_j = v[rows]
        v_bd = jnp.concatenate([v_j * hm[g:g + 1] for g in range(groups)], axis=0).astype(BF16)
        y = u[rows] * (bias_ref[...] + _dot(w_cat, v_bd))
        ms = _head_mean_sq(y, bd)
        o_ref[rows, :] = (y * lax.rsqrt(ms + RMS_EPS) * nw_ref[...]).astype(o_ref.dtype)


def gmlp(h, ln_w, ln_b, w_s, b_s, norm_w, u_col, v_col):
    n = h.shape[0]
    groups, c, _ = w_s.shape
    w = groups * HEAD_DIM
    lane_head = np.arange(w) // HEAD_DIM
    bd = (lane_head[:, None] == lane_head[None, :]).astype(np.float32)
    hm = (np.arange(groups)[:, None] == lane_head[None, :]).astype(np.float32)
    bias = jnp.repeat(b_s.T, HEAD_DIM, axis=1)

    def const(shape):
        return pl.BlockSpec(shape, lambda i: (0,) * len(shape))

    t = math.gcd(n, GM_TILE_CHUNKS * c)
    return pl.pallas_call(
        _gmlp_kernel, grid=(n // t,),
        in_specs=[pl.BlockSpec((t, w), lambda i: (i, u_col)), pl.BlockSpec((t, w), lambda i: (i, v_col)),
                  const((1, w)), const((1, w)), const(w_s.shape), const((c, w)), const((1, w)),
                  const(bd.shape), const(hm.shape)],
        out_specs=pl.BlockSpec((t, w), lambda i: (i, 0)),
        out_shape=jax.ShapeDtypeStruct((n, w), BF16),
        compiler_params=_params("parallel"), name="gmlp",
    )(h, h, ln_w.reshape(1, w), ln_b.reshape(1, w), w_s, bias, norm_w.reshape(1, w),
      jnp.asarray(bd), jnp.asarray(hm))


def _compress_kernel(kv_ref, wtop_ref, wbot_ref, pe_ref, w2_ref, o_ref, *, transposed):
    units = kv_ref.shape[0] // CMP_STRIDE
    pe = pe_ref[...].astype(BF16)
    p = jnp.zeros((units, wtop_ref.shape[2]), F32)
    q = jnp.zeros_like(p)
    const = jnp.zeros((1, wtop_ref.shape[2]), F32)
    for j in range(CMP_STRIDE):
        rows = kv_ref[pl.ds(j, units, stride=CMP_STRIDE), :].astype(BF16)
        p = p + _dot(rows, wtop_ref[j])
        q = q + _dot(rows, wbot_ref[j])
        const = const + _dot(pe[0, j:j + 1], wtop_ref[j]) + _dot(pe[1, j:j + 1], wbot_ref[j])
    q_next = jnp.concatenate([q[1:], jnp.zeros_like(q[0:1])], axis=0)
    hid = _gelu(p + q_next + const)
    y = _dot(hid.astype(BF16), w2_ref[...])
    if transposed:
        y = y.T
    for g in range(o_ref.shape[0]):
        sl = slice(g * HEAD_DIM, (g + 1) * HEAD_DIM)
        o_ref[g] = (y[sl] if transposed else y[:, sl]).astype(o_ref.dtype)


def compress(h, col, pe, w1, w2, batch, seq, transposed):
    g = NSA_KV_GROUPS
    half = CMP_STRIDE
    units = seq // half
    gw = g * HEAD_DIM
    eye = jnp.eye(g, dtype=F32)
    w1r = w1.reshape(2, half, HEAD_DIM, CMP_HIDDEN)
    wbd = jnp.einsum('hjdn,gk->hjgdkn', w1r, eye).reshape(2, half, gw, g * CMP_HIDDEN).astype(BF16)
    w2bd = jnp.einsum('nd,gk->gnkd', w2, eye).reshape(g * CMP_HIDDEN, gw).astype(BF16)
    pe2 = jnp.broadcast_to(pe.reshape(2, half, 1, HEAD_DIM), (2, half, g, HEAD_DIM)).reshape(2, half, gw)

    def const(shape):
        return pl.BlockSpec(shape, lambda b: (0,) * len(shape))

    per_group = (HEAD_DIM, units) if transposed else (units, HEAD_DIM)
    return pl.pallas_call(
        functools.partial(_compress_kernel, transposed=transposed), grid=(batch,),
        in_specs=[pl.BlockSpec((seq, gw), lambda b: (b, col)),
                  const(wbd.shape[1:]), const(wbd.shape[1:]), const(pe2.shape), const(w2bd.shape)],
        out_specs=pl.BlockSpec((None, g) + per_group, lambda b: (b, 0, 0, 0)),
        out_shape=jax.ShapeDtypeStruct((batch, g) + per_group, BF16),
        compiler_params=_params("parallel"), name="nsa_compress",
    )(h, wbd[0], wbd[1], pe2, w2bd)


def _rot_half_pairs(x):
    lane = lax.broadcasted_iota(jnp.int32, x.shape, 1)
    fwd = pltpu.roll(x, 32, axis=1)
    bwd = pltpu.roll(x, 96, axis=1)
    return jnp.where((lane % HEAD_DIM) < HEAD_DIM // 2, bwd, fwd)


def _kprep_kernel(ks_ref, vs_ref, kw_ref, vw_ref, cos_ref, sin_ref, ksa_ref, vso_ref, kwo_ref, vwo_ref):
    t = ks_ref.shape[0]
    cos = cos_ref[...]
    sin = sin_ref[...]
    cos2 = jnp.concatenate([cos, cos], axis=1)
    sin2 = jnp.concatenate([sin, sin], axis=1)
    ks = ks_ref[...]
    kw = kw_ref[...]
    ks_r = ks * cos2 + _rot_half_pairs(ks) * sin2
    kw_r = kw * cos2 + _rot_half_pairs(kw) * sin2
    pos = pl.program_id(1) * t + lax.broadcasted_iota(jnp.int32, (t, HEAD_DIM), 0)
    onehot = (pos // SEL_BLOCK == lax.broadcasted_iota(jnp.int32, (t, HEAD_DIM), 1)).astype(F32)
    vs_t = vs_ref[...].T
    vw_t = vw_ref[...].T
    tail = (lax.broadcasted_iota(jnp.int32, (V_ROWS - HEAD_DIM, t), 0) == 0).astype(F32)
    for g in range(NSA_KV_GROUPS):
        sl = slice(g * HEAD_DIM, (g + 1) * HEAD_DIM)
        ksa_ref[g] = jnp.concatenate([ks_r[:, sl], onehot], axis=1).astype(BF16)
        vso_ref[g] = jnp.concatenate([vs_t[sl], tail], axis=0).astype(BF16)
        kwo_ref[g] = kw_r[:, sl].astype(BF16)
        vwo_ref[g] = jnp.concatenate([vw_t[sl], tail], axis=0).astype(BF16)


def nsa_kprep(h, cosf, sinf, batch, seq, col0):
    g = NSA_KV_GROUPS
    t = min(seq, 512)
    nt = seq // t

    def col(j):
        return pl.BlockSpec((t, LANES), lambda b, i, j=j: (b * nt + i, col0 + j))

    tab = pl.BlockSpec((t, HEAD_DIM), lambda b, i: (b * nt + i, 0))

    def out(wd):
        return pl.BlockSpec((None, g, t, wd), lambda b, i: (b, 0, i, 0))

    out_t = pl.BlockSpec((None, g, V_ROWS, t), lambda b, i: (b, 0, 0, i))
    k_shape = jax.ShapeDtypeStruct((batch, g, seq, HEAD_DIM), BF16)
    v_shape = jax.ShapeDtypeStruct((batch, g, V_ROWS, seq), BF16)
    return pl.pallas_call(
        _kprep_kernel, grid=(batch, nt),
        in_specs=[col(0), col(1), col(2), col(3), tab, tab],
        out_specs=[out(2 * HEAD_DIM), out_t, out(HEAD_DIM), out_t],
        out_shape=[jax.ShapeDtypeStruct((batch, g, seq, 2 * HEAD_DIM), BF16), v_shape, k_shape, v_shape],
        compiler_params=_params("parallel", "parallel"), name="nsa_kprep",
    )(h, h, h, h, cosf, sinf)


def _nsa_kernel(hq_ref, gate_ref, cos_ref, sin_ref, kc_ref, vc_ref, ksa_ref, vs_ref, kw_ref, vw_ref,
                ovl_ref, nw_ref, o_ref, *, tq, tk, n_sb):
    qi = pl.program_id(1)
    hpg = NSA_HPG
    groups = NSA_KV_GROUPS
    rows = hpg * tq
    t0 = qi * tq
    scale = 1.0 / math.sqrt(HEAD_DIM)
    half = HEAD_DIM // 2

    hq_t = hq_ref[...].T
    cos = cos_ref[...]
    sin = sin_ref[...]
    q_raw, q_rot = [], []
    for g in range(groups):
        raw_g, rot_g = [], []
        for h in range(hpg):
            r0 = (g * hpg + h) * HEAD_DIM
            qh = hq_t[r0:r0 + HEAD_DIM]
            swapped = jnp.concatenate([qh[half:], qh[:half]], axis=0)
            raw_g.append(qh * (scale * LOG2_E))
            rot_g.append((qh * cos + swapped * sin) * (scale * LOG2_E))
        q_raw.append(jnp.concatenate(raw_g, axis=1).astype(BF16))
        q_rot.append(jnp.concatenate(rot_g, axis=1))

    tpos = t0 + lax.broadcasted_iota(jnp.int32, (1, tq), 1)
    tpos_r = jnp.concatenate([tpos] * hpg, axis=1)

    def flash_steps(s, v_t, carry):
        m_new = [jnp.maximum(carry[g][0], jnp.max(s[g], axis=0, keepdims=True)) for g in range(groups)]
        p = [jnp.exp2(s[g] - m_new[g]).astype(BF16) for g in range(groups)]
        return tuple((m_new[g], jnp.exp2(carry[g][0] - m_new[g]) * carry[g][1] + _dot(v_t[g], p[g]))
                     for g in range(groups))

    init = ((jnp.full((1, rows), NEG_INF, F32), jnp.zeros((V_ROWS, rows), F32)),) * groups

    wk = WINDOW + tq
    kw0 = pl.multiple_of(jnp.maximum(t0 - WINDOW, 0), tq)
    kpos_w = kw0 + lax.broadcasted_iota(jnp.int32, (wk, 1), 0)
    mask_w = (kpos_w <= tpos_r) & (kpos_w > tpos_r - WINDOW)
    n_pad = jnp.maximum(WINDOW - 1 - tpos_r, 0).astype(F32)
    n_cmp = kc_ref.shape[1]
    cmp_end = lax.broadcasted_iota(jnp.int32, (n_cmp, 1), 0) * CMP_STRIDE + (CMP_BLOCK - 1)
    mask_c = cmp_end <= tpos_r
    blk = lax.broadcasted_iota(jnp.int32, (n_sb, tq), 0)
    cur = (t0 + lax.broadcasted_iota(jnp.int32, (n_sb, tq), 1)) // SEL_BLOCK
    s_w = [jnp.where(mask_w, _dot(kw_ref[g, pl.ds(kw0, wk), :], q_rot[g].astype(BF16)), NEG_INF)
           for g in range(groups)]
    s_c = [jnp.where(mask_c, _dot(kc_ref[g], q_raw[g]), NEG_INF) for g in range(groups)]
    win = flash_steps(s_w, [vw_ref[g, :, pl.ds(kw0, wk)] for g in range(groups)], init)
    o_w, o_c, imps = [], [], []
    for g in range(groups):
        m_w, acc_w = win[g]
        m_f = jnp.where(n_pad > 0.0, jnp.maximum(m_w, 0.0), m_w)
        a_w = jnp.exp2(m_w - m_f)
        o_w.append(acc_w[0:HEAD_DIM] * (a_w / (acc_w[HEAD_DIM:HEAD_DIM + 1] * a_w + n_pad * jnp.exp2(-m_f))))

    for g in range(groups):
        e_c = jnp.exp2(s_c[g] - jnp.max(s_c[g], axis=0, keepdims=True))
        p_c = jnp.where(mask_c, e_c * (1.0 / jnp.sum(e_c, axis=0, keepdims=True)), 0.0)
        o_c.append(_dot(vc_ref[g], p_c.astype(BF16)))

        p_sum = p_c[:, 0:tq]
        for h in range(1, hpg):
            p_sum = p_sum + p_c[:, h * tq:(h + 1) * tq]
        imp = jnp.dot(ovl_ref[...], p_sum, precision=HIGHEST, preferred_element_type=F32)
        key = pltpu.bitcast(jnp.maximum(imp, 0.0), jnp.int32)
        key = jnp.where((blk == 0) | (blk == cur) | (blk == cur - 1), FORCE_KEY, key)
        imps.append(jnp.where(blk > cur, -1, key))

    n_sel = min(N_SEL, n_sb)

    def bit_body(it, taus):
        bit = lax.shift_left(jnp.int32(1), 30 - it)
        out = []
        for g in range(groups):
            cand = taus[g] | bit
            cnt = jnp.sum(jnp.where(imps[g] >= cand, 1, 0), axis=0, keepdims=True)
            out.append(jnp.where(cnt >= n_sel, cand, taus[g]))
        return tuple(out)

    taus = lax.fori_loop(0, 31, bit_body, (jnp.zeros((1, tq), jnp.int32),) * groups)
    lower = (lax.broadcasted_iota(jnp.int32, (n_sb, n_sb), 0)
             > lax.broadcasted_iota(jnp.int32, (n_sb, n_sb), 1)).astype(BF16)
    q_aug = []
    for g in range(groups):
        above = imps[g] > taus[g]
        equal = imps[g] == taus[g]
        need = n_sel - jnp.sum(jnp.where(above, 1, 0), axis=0, keepdims=True)
        earlier = _dot(lower, jnp.where(equal, 1.0, 0.0).astype(BF16))
        selected = above | (equal & (earlier < need.astype(F32)))
        sel_bias = jnp.where(selected, 0.0, NEG_INF)
        if n_sb < HEAD_DIM:
            sel_bias = jnp.concatenate([sel_bias, jnp.zeros((HEAD_DIM - n_sb, tq), F32)], axis=0)
        q_aug.append(jnp.concatenate([q_rot[g], jnp.concatenate([sel_bias] * hpg, axis=1)], axis=0).astype(BF16))

    def scores(k0, width):
        return [_dot(ksa_ref[g, pl.ds(k0, width), :], q_aug[g]) for g in range(groups)]

    def values(k0, width):
        return [vs_ref[g, :, pl.ds(k0, width)] for g in range(groups)]

    def wide_body(kt, carry):
        k0 = pl.multiple_of(kt * tk, tk)
        return flash_steps(scores(k0, tk), values(k0, tk), carry)

    n_wide = t0 // tk
    carry = lax.fori_loop(0, n_wide, wide_body, init)

    def narrow_body(j, carry):
        k0 = pl.multiple_of(n_wide * tk + j * tq, tq)
        return flash_steps(scores(k0, tq), values(k0, tq), carry)

    carry = lax.fori_loop(0, (t0 - n_wide * tk) // tq, narrow_body, carry)
    k0 = pl.multiple_of(t0, tq)
    mask_s = k0 + lax.broadcasted_iota(jnp.int32, (tq, 1), 0) <= tpos_r
    carry = flash_steps([jnp.where(mask_s, s, NEG_INF) for s in scores(k0, tq)], values(k0, tq), carry)

    gates = _sigmoid(gate_ref[...].T)
    nw = nw_ref[...]
    outs = []
    for g in range(groups):
        acc_s = carry[g][1]
        o_s = acc_s[0:HEAD_DIM] * (1.0 / acc_s[HEAD_DIM:HEAD_DIM + 1])
        for h in range(hpg):
            sl = slice(h * tq, (h + 1) * tq)
            r = (g * hpg + h) * N_GATES
            o = gates[r:r + 1] * o_c[g][:, sl] + gates[r + 1:r + 2] * o_s[:, sl] + gates[r + 2:r + 3] * o_w[g][:, sl]
            ms = jnp.mean(o * o, axis=0, keepdims=True)
            outs.append(o * lax.rsqrt(ms + RMS_EPS) * nw[:, g * hpg + h:g * hpg + h + 1])
    o_ref[...] = jnp.concatenate(outs, axis=0).T.astype(o_ref.dtype)


def nsa_attention(h, cos_t, sin_t, kc, vc_t, ksa, vs_t, kw, vw_t, norm_w, batch, seq, q_col0, gate_col):
    g, hpg = NSA_KV_GROUPS, NSA_HPG
    tq = min(seq, 256)
    nq = seq // tq
    n_sb = seq // SEL_BLOCK
    assert n_sb <= HEAD_DIM, "selection-block one-hot shares the 64 spare key lanes"
    n_cmp = kc.shape[2]
    units = np.arange(n_cmp)[:, None] + np.arange(CMP_BLOCK // CMP_STRIDE)[None, :]
    ovl = np.zeros((n_cmp, n_sb), np.float32)
    for c in range((seq - CMP_BLOCK) // CMP_STRIDE + 1):
        for u in units[c]:
            ovl[c, u // (SEL_BLOCK // CMP_STRIDE)] += 1.0
    ovl_t = jnp.asarray(ovl.T)

    def per_b(shape):
        return pl.BlockSpec((None, g) + shape, lambda b, qi: (b, 0, 0, 0))

    width = g * hpg * HEAD_DIM
    tab = pl.BlockSpec((HEAD_DIM, tq), lambda b, qi: (0, b * nq + qi))
    tk = min(seq, 512)
    assert seq >= WINDOW + tq and seq % tk == 0 and tk % tq == 0
    kern = functools.partial(_nsa_kernel, tq=tq, tk=tk, n_sb=n_sb)
    return pl.pallas_call(
        kern, grid=(batch, nq),
        in_specs=[pl.BlockSpec((tq, width), lambda b, qi: (b * nq + qi, q_col0)),
                  pl.BlockSpec((tq, LANES), lambda b, qi: (b * nq + qi, gate_col)),
                  tab, tab,
                  per_b((n_cmp, HEAD_DIM)), per_b((HEAD_DIM, n_cmp)),
                  per_b((seq, 2 * HEAD_DIM)), per_b((V_ROWS, seq)),
                  per_b((seq, HEAD_DIM)), per_b((V_ROWS, seq)),
                  pl.BlockSpec((n_sb, n_cmp), lambda b, qi: (0, 0)),
                  pl.BlockSpec((HEAD_DIM, g * hpg), lambda b, qi: (0, 0))],
        out_specs=pl.BlockSpec((tq, width), lambda b, qi: (b * nq + qi, 0)),
        out_shape=jax.ShapeDtypeStruct((batch * seq, width), BF16),
        compiler_params=_params("parallel", "arbitrary"), name="nsa_attention",
    )(h, h, cos_t, sin_t, kc, vc_t, ksa, vs_t, kw, vw_t, ovl_t, norm_w.reshape(g * hpg, HEAD_DIM).T)


def _out_proj_kernel(x_ref, yhg_ref, ygm_ref, ynsa_ref, whg_ref, wgm_ref, wnsa_ref, lnw_ref, lnb_ref,
                     rw_ref, rb_ref, o_ref, oa_ref, ob_ref, e_ref, p_ref, r_ref, cnt_ref, carry_ref, *, alpha):
    mix = (_dot(yhg_ref[...], whg_ref[...]) + _dot(ygm_ref[...], wgm_ref[...])
           + _dot(ynsa_ref[...], wnsa_ref[...]))
    y = _layer_norm(alpha * x_ref[...] + mix, lnw_ref[...], lnb_ref[...])
    o_ref[...] = y
    _store_word_tables((oa_ref, ob_ref), _pack_bf16_pairs(y))
    _router_kernel(o_ref, rw_ref, rb_ref, e_ref, p_ref, r_ref, cnt_ref, carry_ref)


def out_proj_ln(x2d, y_hg, y_gm, y_nsa, w_out, ln_w, ln_b, router_w, router_b, alpha):
    n, d = x2d.shape
    n_e = router_w.shape[1]
    w1, w2 = y_hg.shape[1], y_hg.shape[1] + y_gm.shape[1]
    whg = w_out[:w1].astype(BF16)
    wgm = w_out[w1:w2].astype(BF16)
    wnsa = w_out[w2:].astype(BF16)
    t = min(n, 512)

    def row(wd):
        return pl.BlockSpec((t, wd), lambda i: (i, 0))

    def const(shape):
        return pl.BlockSpec(shape, lambda i: (0,) * len(shape))

    kern = functools.partial(_out_proj_kernel, alpha=alpha)
    row4 = pl.BlockSpec((TOP_K, t), lambda i: (0, i))
    x1, x1a, x1b, top_e, top_p, rank, counts = pl.pallas_call(
        kern, grid=(n // t,),
        in_specs=[row(d), row(y_hg.shape[1]), row(y_gm.shape[1]), row(y_nsa.shape[1]),
                  const(whg.shape), const(wgm.shape), const(wnsa.shape), const((1, d)), const((1, d)),
                  const((n_e, d)), const((n_e, 1))],
        out_specs=[row(d), row(SC_ROW_WORDS), row(SC_ROW_WORDS), row4, row4, row4, const((n_e, 1))],
        out_shape=[jax.ShapeDtypeStruct((n, d), F32)] + [jax.ShapeDtypeStruct((n, SC_ROW_WORDS), jnp.uint32)] * 2
        + [jax.ShapeDtypeStruct((TOP_K, n), jnp.int32), jax.ShapeDtypeStruct((TOP_K, n), F32),
           jax.ShapeDtypeStruct((TOP_K, n), jnp.int32), jax.ShapeDtypeStruct((n_e, 1), jnp.int32)],
        scratch_shapes=[pltpu.VMEM((n_e, 1), F32)],
        compiler_params=_params("arbitrary"), name="out_proj_ln",
    )(x2d, y_hg, y_gm, y_nsa, whg, wgm, wnsa, ln_w.reshape(1, d), ln_b.reshape(1, d),
      router_w.T, router_b.reshape(n_e, 1))
    return x1, (x1a, x1b), (top_e, top_p, rank, counts[:, 0])


def _router_kernel(x_ref, w_ref, b_ref, e_ref, p_ref, r_ref, cnt_ref, carry_ref):
    t = x_ref.shape[0]

    @pl.when(pl.program_id(0) == 0)
    def _():
        carry_ref[...] = jnp.zeros_like(carry_ref)

    x = x_ref[...]
    x_hi = x.astype(BF16)
    x_lo = (x - x_hi.astype(F32)).astype(BF16)
    w = w_ref[...]
    w_hi = w.astype(BF16)
    w_lo = (w - w_hi.astype(F32)).astype(BF16)
    logits = _dot_nt(w_hi, x_hi) + (_dot_nt(w_hi, x_lo) + _dot_nt(w_lo, x_hi)) + b_ref[...]
    n_e = logits.shape[0]
    sub = lax.broadcasted_iota(jnp.int32, logits.shape, 0)
    work = logits
    vals, idxs = [], []
    sel = jnp.zeros(logits.shape, F32)
    for _ in range(TOP_K):
        m = jnp.max(work, axis=0, keepdims=True)
        idx = jnp.min(jnp.where(work == m, sub, n_e), axis=0, keepdims=True)
        hit = sub == idx
        sel = jnp.where(hit, 1.0, sel)
        work = jnp.where(hit, -jnp.inf, work)
        vals.append(m)
        idxs.append(idx)
    exps = [jnp.exp(v - vals[0]) for v in vals]
    inv_den = 1.0 / (exps[0] + exps[1] + exps[2] + exps[3])
    earlier = (lax.broadcasted_iota(jnp.int32, (t, t), 0) < lax.broadcasted_iota(jnp.int32, (t, t), 1))
    before = _dot(sel.astype(BF16), earlier.astype(BF16)) + carry_ref[...]
    ranks = [jnp.sum(jnp.where(sub == idx, before, 0.0), axis=0, keepdims=True) for idx in idxs]
    e_ref[...] = jnp.concatenate(idxs, axis=0)
    p_ref[...] = jnp.concatenate([e * inv_den for e in exps], axis=0)
    r_ref[...] = jnp.concatenate(ranks, axis=0).astype(jnp.int32)
    carry_ref[...] = carry_ref[...] + jnp.sum(sel, axis=1, keepdims=True)
    cnt_ref[...] = carry_ref[...].astype(jnp.int32)


def _expert_kernel(be_ref, valid_ref, xa_ref, xb_ref, wu_ref, bu_ref, wd_ref, bd_ref, oa_ref, ob_ref,
                   wu_bf, wd_bf):
    i = pl.program_id(0)
    f = wd_ref.shape[0]
    n_used = be_ref[pl.num_programs(0)]

    @pl.when((i == 0) | (be_ref[i] != be_ref[jnp.maximum(i - 1, 0)]))
    def _():
        wu_bf[...] = wu_ref[...].astype(BF16)
        wd_bf[...] = wd_ref[...].astype(BF16)

    @pl.when(i < n_used)
    def _():
        packed = jnp.concatenate([xa_ref[...], xb_ref[...]], axis=1)
        live = lax.broadcasted_iota(jnp.int32, packed.shape, 0) < valid_ref[i]
        x_lo, x_hi = _unpack_bf16_pairs(jnp.where(live, packed, jnp.uint32(0)))
        x = jnp.concatenate([x_lo.astype(BF16), x_hi.astype(BF16)], axis=1)
        hcat = _dot(x, wu_bf[...]) + bu_ref[...]
        glu = jnp.minimum(hcat[:, :f], SWIGLU_LIMIT)
        lin = jnp.clip(hcat[:, f:], -SWIGLU_LIMIT, SWIGLU_LIMIT)
        act = glu * _sigmoid(SWIGLU_ALPHA * glu) * (lin + 1.0)
        _store_word_tables((oa_ref, ob_ref), _pack_bf16_pairs(_dot(act.astype(BF16), wd_bf[...]) + bd_ref[...]))

    @pl.when(i >= n_used)
    def _():
        oa_ref[...] = jnp.zeros_like(oa_ref)
        ob_ref[...] = jnp.zeros_like(ob_ref)


def moe_experts(xa, xb, block_e, n_used, block_valid, w_up, b_up, w_down, b_down, layer):
    rows = xa.shape[0]
    _, e, d, f2 = w_up.shape
    f = f2 // 2
    nb = rows // EXPERT_BLOCK
    words = pl.BlockSpec((EXPERT_BLOCK, SC_ROW_WORDS), lambda i, be, nv: (i, 0))
    grid_spec = pltpu.PrefetchScalarGridSpec(
        num_scalar_prefetch=2, grid=(nb,),
        in_specs=[words, words,
                  pl.BlockSpec((None, None, d, f2), lambda i, be, nv: (layer, be[i], 0, 0)),
                  pl.BlockSpec((None, None, 1, f2), lambda i, be, nv: (layer, be[i], 0, 0)),
                  pl.BlockSpec((None, None, f, d), lambda i, be, nv: (layer, be[i], 0, 0)),
                  pl.BlockSpec((None, None, 1, d), lambda i, be, nv: (layer, be[i], 0, 0))],
        out_specs=[words, words],
        scratch_shapes=[pltpu.VMEM((d, f2), BF16), pltpu.VMEM((f, d), BF16)])
    depth = w_up.shape[0]
    return pl.pallas_call(
        _expert_kernel, grid_spec=grid_spec,
        out_shape=[jax.ShapeDtypeStruct((rows, SC_ROW_WORDS), jnp.uint32)] * 2,
        compiler_params=pltpu.CompilerParams(dimension_semantics=("arbitrary",), vmem_limit_bytes=EXPERT_VMEM_LIMIT),
        name="moe_experts",
    )(jnp.concatenate([block_e, n_used.reshape(1)]), block_valid, xa, xb, w_up, b_up.reshape(depth, e, 1, f2),
      w_down, b_down.reshape(depth, e, 1, d))


def _combine_kernel(x_ref, ya_ref, yb_ref, p_ref, lnw_ref, lnb_ref, o_ref, *, alpha):
    p = p_ref[...]
    moe = jnp.zeros(x_ref.shape, F32)
    for k in range(TOP_K):
        y_lo, y_hi = _unpack_bf16_pairs(jnp.concatenate([ya_ref[k], yb_ref[k]], axis=1))
        moe = moe + p[:, k:k + 1] * jnp.concatenate([y_lo, y_hi], axis=1)
    o_ref[...] = _layer_norm(alpha * x_ref[...] + moe, lnw_ref[...], lnb_ref[...])


def combine_ln(x2d, ya, yb, top_p, ln_w, ln_b, alpha):
    n, d = x2d.shape
    t = min(n, 512)
    kern = functools.partial(_combine_kernel, alpha=alpha)
    words = pl.BlockSpec((TOP_K, t, SC_ROW_WORDS), lambda i: (0, i, 0))
    return pl.pallas_call(
        kern, grid=(n // t,),
        in_specs=[pl.BlockSpec((t, d), lambda i: (i, 0)), words, words,
                  pl.BlockSpec((t, TOP_K), lambda i: (i, 0)),
                  pl.BlockSpec((1, d), lambda i: (0, 0)), pl.BlockSpec((1, d), lambda i: (0, 0))],
        out_specs=pl.BlockSpec((t, d), lambda i: (i, 0)),
        out_shape=jax.ShapeDtypeStruct((n, d), F32),
        compiler_params=_params("parallel"), name="moe_combine_ln",
    )(x2d, ya, yb, top_p, ln_w.reshape(1, d), ln_b.reshape(1, d))


def _sc_mesh():
    return plsc.VectorSubcoreMesh(core_axis_name="core", subcore_axis_name="subcore")


def sc_gather_rows(tables, idx):
    r = idx.shape[0]
    nt = len(tables)
    out = jax.ShapeDtypeStruct((r, SC_ROW_WORDS), tables[0].dtype)

    @pl.kernel(out_type=(out,) * nt, mesh=_sc_mesh(), name="sc_gather_rows")
    def gather(*refs):
        x_hbm, i_hbm, o_hbm = refs[:nt], refs[nt], refs[nt + 1:]
        for j in range(nt):
            def body(i_vmem, o_vmem, table=x_hbm[j]):
                pltpu.sync_copy(table.at[i_vmem.at[0]], o_vmem)

            pltpu.emit_pipeline(
                body, grid=(r // SC_WINDOW,),
                in_specs=[pl.BlockSpec((1, SC_WINDOW), lambda i: (0, i))],
                out_specs=[pl.BlockSpec((SC_WINDOW, SC_ROW_WORDS), lambda i: (i, 0))],
                core_axis_name=("core", "subcore"), dimension_semantics=(pltpu.PARALLEL,),
            )(i_hbm, o_hbm[j])

    return gather(*tables, idx.reshape(1, r))


def sc_scatter_rows(tables, dest_t, n_rows):
    n = tables[0].shape[0]
    nt = len(tables)
    copies = dest_t.shape[0]
    out = jax.ShapeDtypeStruct((n_rows, SC_ROW_WORDS), tables[0].dtype)

    @pl.kernel(out_type=(out,) * nt, mesh=_sc_mesh(), scratch_types=[], name="sc_scatter_rows")
    def scatter(*refs):
        x_hbm, i_hbm, o_hbm = refs[:nt], refs[nt], refs[nt + 1:]
        for j in range(nt):
            def body(x_vmem, i_vmem, out_j=o_hbm[j]):
                for k in range(copies):
                    pltpu.sync_copy(x_vmem, out_j.at[i_vmem.at[k]])

            pltpu.emit_pipeline(
                body, grid=(n // SC_WINDOW,),
                in_specs=[pl.BlockSpec((SC_WINDOW, SC_ROW_WORDS), lambda i: (i, 0)),
                          pl.BlockSpec((copies, SC_WINDOW), lambda i: (0, i))],
                out_specs=[],
                core_axis_name=("core", "subcore"), dimension_semantics=(pltpu.PARALLEL,),
            )(x_hbm[j], i_hbm)

    return scatter(*tables, dest_t)


def moe_ffn_ln(x_f32, x_packed, routing, w_up, b_up, w_down, b_down, layer, ln_w, ln_b, alpha):
    n, d = x_f32.shape
    top_e, top_p, rank, counts = routing
    padded = (counts + EXPERT_BLOCK - 1) // EXPERT_BLOCK * EXPERT_BLOCK
    pad_end = jnp.cumsum(padded)
    pad_start = pad_end - padded
    n_assign = n * TOP_K
    n_blocks = -(-(n_assign + N_EXPERTS * (EXPERT_BLOCK - 1)) // EXPERT_BLOCK)
    experts = jnp.arange(N_EXPERTS, dtype=jnp.int32)
    dest_t = rank + jnp.sum(jnp.where(top_e[..., None] == experts, pad_start.astype(jnp.int32), 0), axis=-1)
    block_first = jnp.arange(n_blocks, dtype=jnp.int32) * EXPERT_BLOCK
    block_e = jnp.clip(jnp.sum((pad_end[None, :] <= block_first[:, None]).astype(jnp.int32), axis=1),
                       0, N_EXPERTS - 1)
    block_valid = jnp.clip(counts[block_e] - (block_first - pad_start[block_e]), 0, EXPERT_BLOCK)
    n_used = (pad_end[-1] // EXPERT_BLOCK).astype(jnp.int32)
    xa, xb = sc_scatter_rows(x_packed, dest_t, n_blocks * EXPERT_BLOCK)
    ya, yb = moe_experts(xa, xb, block_e, n_used, block_valid.astype(jnp.int32), w_up, b_up, w_down, b_down,
                         layer)
    ya, yb = sc_gather_rows((ya, yb), dest_t.reshape(-1))
    return combine_ln(x_f32, ya.reshape(TOP_K, n, SC_ROW_WORDS), yb.reshape(TOP_K, n, SC_ROW_WORDS), top_p.T,
                      ln_w, ln_b, alpha)


def kernel(x, positions, w_in, hg_lower_bounds, hg_norm_w, gm_ln_w, gm_ln_b, gm_spatial_w, gm_spatial_b, gm_norm_w, nsa_cmp_pe, nsa_cmp_w1, nsa_cmp_w2, nsa_norm_w, w_out, ln1_w, ln1_b, router_w, router_b, exp_w_up, exp_b_up, exp_w_down, exp_b_down, ln2_w, ln2_b):
    batch, seq, d = x.shape
    depth = w_in.shape[0]
    n = batch * seq
    alpha = (2 * depth) ** 0.25
    hg_w = hg_norm_w.shape[1]
    gm_w = gm_norm_w.shape[1]
    nsa_w = nsa_norm_w.shape[1]
    kv_w = NSA_KV_GROUPS * HEAD_DIM
    in_width = w_in.shape[2]
    off_gm = 4 * hg_w
    off_q = off_gm + 2 * gm_w
    off_kv = off_q + nsa_w
    off_gate = off_kv + 6 * kv_w
    width_pad = -(-in_width // LANES) * LANES

    cosf, sinf, cos_t, sin_t = rope_tables(positions)
    lb_all = jnp.cumsum(jax.nn.softmax(hg_lower_bounds.astype(F32), axis=0), axis=0)
    lb_all = lb_all - lb_all[0:1]

    x2d = x.reshape(n, d)
    for l in range(depth):
        w_l = jnp.pad(w_in[l], ((0, 0), (0, width_pad - in_width))).astype(BF16)
        h = in_proj(x2d, w_l)
        h3 = h.reshape(batch, seq, width_pad)
        y_hg = hgrn2(h3, lb_all[l], hg_norm_w[l]).reshape(n, hg_w)
        y_gm = gmlp(h, gm_ln_w[l], gm_ln_b[l], gm_spatial_w[l], gm_spatial_b[l], gm_norm_w[l],
                    off_gm // gm_w, off_gm // gm_w + 1)
        kc = compress(h, off_kv // kv_w, nsa_cmp_pe[l, 0], nsa_cmp_w1[l, 0], nsa_cmp_w2[l, 0], batch, seq, False)
        vc_t = compress(h, off_kv // kv_w + 1, nsa_cmp_pe[l, 1], nsa_cmp_w1[l, 1], nsa_cmp_w2[l, 1], batch, seq,
                        True)
        ksa, vs_t, kw, vw_t = nsa_kprep(h, cosf, sinf, batch, seq, (off_kv + 2 * kv_w) // LANES)
        y_nsa = nsa_attention(h, cos_t, sin_t, kc, vc_t, ksa, vs_t, kw, vw_t, nsa_norm_w[l], batch, seq,
                              off_q // nsa_w, off_gate // LANES)
        x1, x1_packed, routing = out_proj_ln(x2d, y_hg, y_gm, y_nsa, w_out[l], ln1_w[l], ln1_b[l], router_w[l],
                                             router_b[l], alpha)
        x2d = moe_ffn_ln(x1, x1_packed, routing, exp_w_up, exp_b_up, exp_w_down, exp_b_down, l,
                         ln2_w[l], ln2_b[l], alpha)
    return x2d.reshape(batch, seq, d)
```
